```python
import math
import jax, jax.numpy as jnp
from jax import lax
import numpy as np

D_MODEL = 2048
BATCH = 32
SEQ = 256
DEPTH = 2
DEC_BATCH = 2
DEC_SEQ = 2048
PAST_LEN = 256

GRID_W = 64
MIX_WIDTH = D_MODEL
RET_HEADS = 8
RET_DK = 128
RET_DV = 128
RET_WIDTH = RET_HEADS * RET_DV
RET_CHUNK = 128
ROPE_BASE = 10000.0
S5_WIDTH = MIX_WIDTH // 4
S5_GROUP = 16
S5_GROUPS = S5_WIDTH // S5_GROUP
S5_STATE = 64
HY_WIDTH = MIX_WIDTH - RET_WIDTH - S5_WIDTH
HY_BANDS = 16
HY_EMB = 2 * HY_BANDS + 1
HY_HIDDEN = 64
IN_WIDTH = 4 * RET_WIDTH + S5_WIDTH + 3 * HY_WIDTH
SPLITS = [RET_WIDTH, 2 * RET_WIDTH, 3 * RET_WIDTH, 4 * RET_WIDTH, 4 * RET_WIDTH + S5_WIDTH]
N_EXPERTS = 64
TOP_K = 6
D_EXPERT = 512
D_SHARED = 512
ROUTED_SCALE = 2.5
MOE_BLOCK = 128
EPS = 1e-6
F32 = jnp.float32

kernel_name = 'hybrid_flow_retention_s5_hyena_moe_step'


def rmsnorm(x, g):
    xf = x.astype(F32)
    y = xf * lax.rsqrt(jnp.mean(xf * xf, axis=-1, keepdims=True) + EPS)
    return (y * g.astype(F32)).astype(x.dtype)


def adaln_params(cond, w, b):
    m = jax.nn.silu(cond) @ w + b
    return jnp.split(m[..., None, :], 6, axis=-1)


def rope_tables(L):
    rows_n = L // GRID_W
    rows = jnp.repeat(jnp.arange(rows_n, dtype=F32), GRID_W)
    cols = jnp.tile(jnp.arange(GRID_W, dtype=F32), rows_n)
    nf = RET_DK // 4
    inv = ROPE_BASE ** (-jnp.arange(nf, dtype=F32) / nf)
    ang = jnp.concatenate([rows[:, None] * inv, cols[:, None] * inv], axis=-1)
    return jnp.cos(ang), jnp.sin(ang)


def apply_rope(x, cos, sin):
    half = RET_DK // 2
    x1, x2 = x[..., :half], x[..., half:]
    c = cos[None, :, None, :]
    s = sin[None, :, None, :]
    return jnp.concatenate([x1 * c - x2 * s, x1 * s + x2 * c], axis=-1)


def retention_dir(q, k, v, log_gamma, S0):
    B, L, H, _ = q.shape
    dv = v.shape[-1]
    C = RET_CHUNK
    N = L // C
    idx = jnp.arange(C, dtype=F32)
    rel = idx[:, None] - idx[None, :]
    decay_mat = jnp.where(rel >= 0, jnp.exp(log_gamma[:, None, None] * jnp.maximum(rel, 0.0)), 0.0)
    q_decay = jnp.exp(log_gamma[:, None] * (idx + 1.0))
    k_decay = jnp.exp(log_gamma[:, None] * (C - 1.0 - idx))
    chunk_decay = jnp.exp(log_gamma * C)

    def to_chunks(t):
        return t.reshape(B, N, C, H, t.shape[-1]).transpose(1, 0, 3, 2, 4)

    def step(S, inp):
        qb, kb, vb = inp
        scores = jnp.einsum('bhid,bhjd->bhij', qb, kb) * decay_mat
        inner = jnp.einsum('bhij,bhjv->bhiv', scores, vb)
        cross = jnp.einsum('bhid,bhdv->bhiv', qb, S) * q_decay[None, :, :, None]
        S_new = S * chunk_decay[None, :, None, None] + jnp.einsum('bhjd,hj,bhjv->bhdv', kb, k_decay, vb)
        return S_new, inner + cross

    S_fin, out = lax.scan(step, S0, (to_chunks(q), to_chunks(k), to_chunks(v)))
    return out.transpose(1, 0, 3, 2, 4).reshape(B, L, H, dv), S_fin


def retention(q, k, v, log_gamma, S0):
    o_f, s_f = retention_dir(q, k, v, log_gamma[0], S0[:, 0])
    o_b, s_b = retention_dir(q[:, ::-1], k[:, ::-1], v[:, ::-1], log_gamma[1], S0[:, 1])
    o = o_f + o_b[:, ::-1]
    mu = jnp.mean(o, axis=-1, keepdims=True)
    var = jnp.mean(jnp.square(o - mu), axis=-1, keepdims=True)
    o = (o - mu) * lax.rsqrt(var + EPS)
    B, L = o.shape[:2]
    return o.reshape(B, L, RET_WIDTH), jnp.stack([s_f, s_b], axis=1)


def _linear_recurrence_op(e1, e2):
    a1, b1 = e1
    a2, b2 = e2
    return a1 * a2, a2 * b1 + b2


def s5_dir(u, lam_re, lam_im, log_dt, b_re, b_im, c_re, c_im, h0_re, h0_im):
    lam = lax.complex(jnp.minimum(lam_re.astype(F32), -1e-4), lam_im.astype(F32))
    dt = jnp.exp(log_dt.astype(F32))[:, None]
    lam_bar = jnp.exp(lam * dt)
    b_bar = ((lam_bar - 1.0) / lam)[..., None] * lax.complex(b_re.astype(F32), b_im.astype(F32))
    bu = jnp.einsum('blgc,gpc->blgp', u.astype(jnp.complex64), b_bar)
    bu = bu.at[:, 0].add(lam_bar * lax.complex(h0_re.astype(F32), h0_im.astype(F32)))
    a = jnp.broadcast_to(lam_bar, bu.shape)
    _, h = lax.associative_scan(_linear_recurrence_op, (a, bu), axis=1)
    y = jnp.real(jnp.einsum('blgp,gcp->blgc', h, lax.complex(c_re.astype(F32), c_im.astype(F32))))
    return y, h[:, -1]


def s5_mixer(u, lam_re, lam_im, log_dt, b_re, b_im, c_re, c_im, d, glu_w, glu_b, h0_re, h0_im):
    B, L, _ = u.shape
    uf = u.astype(F32)
    ug = uf.reshape(B, L, S5_GROUPS, S5_GROUP)
    y_f, h_f = s5_dir(ug, lam_re[0], lam_im[0], log_dt[0], b_re[0], b_im[0], c_re[0], c_im[0], h0_re[:, 0], h0_im[:, 0])
    y_b, h_b = s5_dir(ug[:, ::-1], lam_re[1], lam_im[1], log_dt[1], b_re[1], b_im[1], c_re[1], c_im[1], h0_re[:, 1], h0_im[:, 1])
    y = (y_f + y_b[:, ::-1]).reshape(B, L, S5_WIDTH) + d.astype(F32) * uf
    y = jax.nn.gelu(y).astype(u.dtype)
    y = y * jax.nn.sigmoid(y @ glu_w + glu_b)
    h = jnp.stack([h_f, h_b], axis=1)
    return y, jnp.real(h), jnp.imag(h)


def hyena_filters(L, f1_w, f1_b, f2_w, f2_b, f3_w, f3_b, freq, decay):
    t = (jnp.arange(L, dtype=F32) / L)[:, None]
    bands = jnp.arange(1, HY_BANDS + 1, dtype=F32)[None, :]
    z = jnp.concatenate([t, jnp.cos(2.0 * math.pi * t * bands), jnp.sin(2.0 * math.pi * t * bands)], axis=-1)
    fr = freq.astype(F32)
    h = jnp.sin(fr * (z @ f1_w.astype(F32) + f1_b.astype(F32)))
    h = jnp.sin(fr * (h @ f2_w.astype(F32) + f2_b.astype(F32)))
    h = h @ f3_w.astype(F32) + f3_b.astype(F32)
    h = h * jnp.exp(-t * jnp.abs(decay.astype(F32)))
    h = h.reshape(L, 2, HY_WIDTH)
    h = h / jnp.sum(jnp.abs(h), axis=(0, 1), keepdims=True)
    return h[:, 0], h[:, 1]


def hyena(hy, conv_w, conv_b, f1_w, f1_b, f2_w, f2_b, f3_w, f3_b, freq, decay, bias):
    B, L, _ = hy.shape
    xp = jnp.pad(hy, ((0, 0), (1, 1), (0, 0)))
    hy = xp[:, :-2] * conv_w[0] + xp[:, 1:-1] * conv_w[1] + xp[:, 2:] * conv_w[2] + conv_b
    x0, x1, v = jnp.split(hy.astype(F32), 3, axis=-1)
    h_f, h_b = hyena_filters(L, f1_w, f1_b, f2_w, f2_b, f3_w, f3_b, freq, decay)
    z = x1 * v
    hc = jnp.concatenate([h_f, h_b[::-1]], axis=0)
    conv = jnp.fft.irfft(jnp.fft.rfft(z, n=2 * L, axis=1) * jnp.fft.rfft(hc, axis=0)[None], n=2 * L, axis=1)[:, :L]
    return x0 * (conv + bias.astype(F32) * z)


def moe(h, router, router_bias, w_gate, w_up, w_down, s_gate, s_up, s_down):
    B, L, D = h.shape
    T = B * L
    x = h.reshape(T, D)
    scores = jax.nn.sigmoid((x @ router).astype(F32))
    _, idx = lax.top_k(scores + router_bias.astype(F32), TOP_K)
    sel = jnp.take_along_axis(scores, idx, axis=1)
    gates = ROUTED_SCALE * sel / jnp.sum(sel, axis=-1, keepdims=True)
    flat_e = idx.reshape(-1).astype(jnp.int32)
    flat_tok = jnp.repeat(jnp.arange(T, dtype=jnp.int32), TOP_K)
    flat_w = gates.reshape(-1)
    order = jnp.argsort(flat_e)
    se = flat_e[order]
    counts = jax.ops.segment_sum(jnp.ones_like(flat_e), flat_e, num_segments=N_EXPERTS)
    padded = (counts + MOE_BLOCK - 1) // MOE_BLOCK * MOE_BLOCK
    pend = jnp.cumsum(padded)
    pstart = pend - padded
    ustart = jnp.cumsum(counts) - counts
    dest = pstart[se] + jnp.arange(T * TOP_K, dtype=jnp.int32) - ustart[se]
    nb = -(-(T * TOP_K) // MOE_BLOCK) + N_EXPERTS
    slot_tok = jnp.zeros((nb * MOE_BLOCK,), jnp.int32).at[dest].set(flat_tok[order])
    slot_w = jnp.zeros((nb * MOE_BLOCK,), F32).at[dest].set(flat_w[order])
    blk_e = jnp.minimum(jnp.searchsorted(pend, jnp.arange(nb, dtype=jnp.int32) * MOE_BLOCK, side='right'), N_EXPERTS - 1).astype(jnp.int32)

    def expert_block(args):
        tok, e, w = args
        xb = x[tok]
        hb = jax.nn.silu(xb @ w_gate[e]) * (xb @ w_up[e])
        return (hb @ w_down[e]).astype(F32) * w[:, None]

    out = lax.map(expert_block, (slot_tok.reshape(nb, MOE_BLOCK), blk_e, slot_w.reshape(nb, MOE_BLOCK)))
    routed = jnp.zeros((T, D), F32).at[slot_tok].add(out.reshape(-1, D))
    shared = (jax.nn.silu(x @ s_gate) * (x @ s_up)) @ s_down
    return (routed.astype(h.dtype) + shared).reshape(B, L, D)


def trunk_layer(x, cond, p, ret_S0, s5_re0, s5_im0, rope):
    B, L, _ = x.shape
    sh1, sc1, g1, sh2, sc2, g2 = adaln_params(cond, p['w_ada'], p['b_ada'])
    h = rmsnorm(x, p['norm_mix']) * (1.0 + sc1) + sh1
    proj = h @ p['w_in']
    q, k, v, gate, u, hy = jnp.split(proj, SPLITS, axis=-1)
    q = q.astype(F32).reshape(B, L, RET_HEADS, RET_DK)
    k = k.astype(F32).reshape(B, L, RET_HEADS, RET_DK) * RET_DK ** -0.5
    v = v.astype(F32).reshape(B, L, RET_HEADS, RET_DV)
    if rope is not None:
        q = apply_rope(q, rope[0], rope[1])
        k = apply_rope(k, rope[0], rope[1])
    log_gamma = jax.nn.log_sigmoid(p['ret_decay'].astype(F32))
    ret_o, ret_S = retention(q, k, v, log_gamma, ret_S0.astype(F32))
    ret_o = (jax.nn.silu(gate.astype(F32)) * ret_o).astype(x.dtype)
    s5_o, s5_re, s5_im = s5_mixer(u, p['s5_lam_re'], p['s5_lam_im'], p['s5_log_dt'], p['s5_b_re'], p['s5_b_im'], p['s5_c_re'], p['s5_c_im'], p['s5_d'], p['s5_glu_w'], p['s5_glu_b'], s5_re0, s5_im0)
    hy_o = hyena(hy, p['hy_conv_w'], p['hy_conv_b'], p['hy_f1_w'], p['hy_f1_b'], p['hy_f2_w'], p['hy_f2_b'], p['hy_f3_w'], p['hy_f3_b'], p['hy_freq'], p['hy_decay'], p['hy_bias'])
    mix = jnp.concatenate([ret_o, s5_o.astype(x.dtype), hy_o.astype(x.dtype)], axis=-1) @ p['w_out']
    x = x + g1 * mix
    h = rmsnorm(x, p['norm_ffn']) * (1.0 + sc2) + sh2
    x = x + g2 * moe(h, p['moe_router'], p['moe_router_bias'], p['moe_w_gate'], p['moe_w_up'], p['moe_w_down'], p['sh_w_gate'], p['sh_w_up'], p['sh_w_down'])
    return x, ret_S, s5_re, s5_im


def setup_inputs(seed: int = 0) -> dict:
    key = jax.random.key(seed)
    ks = iter(jax.random.split(key, 64))
    D = D_MODEL

    def nrm(shape, std):
        return std * jax.random.normal(next(ks), shape, F32)

    eps_h = 2.0 ** (-5.0 - jnp.arange(RET_HEADS, dtype=F32))
    decay_logit = jnp.log1p(-eps_h) - jnp.log(eps_h)
    hy_decay0 = jnp.abs(jnp.linspace(math.log(1e-2) / 1.5, math.log(1e-2) / 0.3, 2 * HY_WIDTH, dtype=F32))
    ret_std = (min(PAST_LEN, 1024) ** 0.5) * RET_DK ** -0.5
    conv_center = jnp.array([0.0, 1.0, 0.0], F32)[None, :, None]
    return {
        'x_prompt': nrm((BATCH, SEQ, D), 1.0),
        'x_sample': nrm((DEC_BATCH, DEC_SEQ, D), 1.0),
        'state_ret': nrm((DEC_BATCH, DEPTH, 2, RET_HEADS, RET_DK, RET_DV), ret_std),
        'state_s5_re': nrm((DEC_BATCH, DEPTH, 2, S5_GROUPS, S5_STATE), 0.5),
        'state_s5_im': nrm((DEC_BATCH, DEPTH, 2, S5_GROUPS, S5_STATE), 0.5),
        'c': nrm((DEC_BATCH, D), 1.0),
        'c_ctx': nrm((D,), 1.0),
        'w_ada': nrm((DEPTH, D, 6 * D), 0.5 * D ** -0.5),
        'b_ada': nrm((DEPTH, 6 * D), 0.02),
        'norm_mix': 1.0 + nrm((DEPTH, D), 0.02),
        'norm_ffn': 1.0 + nrm((DEPTH, D), 0.02),
        'w_in': nrm((DEPTH, D, IN_WIDTH), D ** -0.5),
        'w_out': nrm((DEPTH, MIX_WIDTH, D), MIX_WIDTH ** -0.5),
        'ret_decay': decay_logit + nrm((DEPTH, 2, RET_HEADS), 0.05),
        's5_lam_re': -0.5 + nrm((DEPTH, 2, S5_GROUPS, S5_STATE), 0.01),
        's5_lam_im': math.pi * jnp.arange(S5_STATE, dtype=F32) + nrm((DEPTH, 2, S5_GROUPS, S5_STATE), 0.01),
        's5_log_dt': jax.random.uniform(next(ks), (DEPTH, 2, S5_GROUPS), F32, math.log(1e-3), math.log(1e-1)),
        's5_b_re': nrm((DEPTH, 2, S5_GROUPS, S5_STATE, S5_GROUP), (2.0 * S5_GROUP) ** -0.5),
        's5_b_im': nrm((DEPTH, 2, S5_GROUPS, S5_STATE, S5_GROUP), (2.0 * S5_GROUP) ** -0.5),
        's5_c_re': nrm((DEPTH, 2, S5_GROUPS, S5_GROUP, S5_STATE), (2.0 * S5_STATE) ** -0.5),
        's5_c_im': nrm((DEPTH, 2, S5_GROUPS, S5_GROUP, S5_STATE), (2.0 * S5_STATE) ** -0.5),
        's5_d': nrm((DEPTH, S5_WIDTH), 1.0),
        's5_glu_w': nrm((DEPTH, S5_WIDTH, S5_WIDTH), S5_WIDTH ** -0.5),
        's5_glu_b': nrm((DEPTH, S5_WIDTH), 0.01),
        'hy_conv_w': conv_center + nrm((DEPTH, 3, 3 * HY_WIDTH), 0.3),
        'hy_conv_b': nrm((DEPTH, 3 * HY_WIDTH), 0.01),
        'hy_f1_w': nrm((DEPTH, HY_EMB, HY_HIDDEN), HY_EMB ** -0.5),
        'hy_f1_b': nrm((DEPTH, HY_HIDDEN), 0.01),
        'hy_f2_w': nrm((DEPTH, HY_HIDDEN, HY_HIDDEN), HY_HIDDEN ** -0.5),
        'hy_f2_b': nrm((DEPTH, HY_HIDDEN), 0.01),
        'hy_f3_w': nrm((DEPTH, HY_HIDDEN, 2 * HY_WIDTH), HY_HIDDEN ** -0.5),
        'hy_f3_b': nrm((DEPTH, 2 * HY_WIDTH), 0.01),
        'hy_freq': 1.0 + nrm((DEPTH, HY_HIDDEN), 0.01),
        'hy_decay': hy_decay0 + nrm((DEPTH, 2 * HY_WIDTH), 0.01),
        'hy_bias': nrm((DEPTH, HY_WIDTH), 0.5),
        'moe_router': nrm((DEPTH, D, N_EXPERTS), D ** -0.5),
        'moe_router_bias': nrm((DEPTH, N_EXPERTS), 0.01),
        'moe_w_gate': nrm((DEPTH, N_EXPERTS, D, D_EXPERT), D ** -0.5),
        'moe_w_up': nrm((DEPTH, N_EXPERTS, D, D_EXPERT), D ** -0.5),
        'moe_w_down': nrm((DEPTH, N_EXPERTS, D_EXPERT, D), D_EXPERT ** -0.5),
        'sh_w_gate': nrm((DEPTH, D, D_SHARED), D ** -0.5),
        'sh_w_up': nrm((DEPTH, D, D_SHARED), D ** -0.5),
        'sh_w_down': nrm((DEPTH, D_SHARED, D), D_SHARED ** -0.5),
        'final_norm': 1.0 + nrm((D,), 0.02),
    }


def reference(x_prompt, x_sample, state_ret, state_s5_re, state_s5_im, c, c_ctx, w_ada, b_ada, norm_mix, norm_ffn, w_in, w_out, ret_decay, s5_lam_re, s5_lam_im, s5_log_dt, s5_b_re, s5_b_im, s5_c_re, s5_c_im, s5_d, s5_glu_w, s5_glu_b, hy_conv_w, hy_conv_b, hy_f1_w, hy_f1_b, hy_f2_w, hy_f2_b, hy_f3_w, hy_f3_b, hy_freq, hy_decay, hy_bias, moe_router, moe_router_bias, moe_w_gate, moe_w_up, moe_w_down, sh_w_gate, sh_w_up, sh_w_down, final_norm):
    n_ctx = x_prompt.shape[0]
    zero_ret = jnp.zeros((n_ctx, 2, RET_HEADS, RET_DK, RET_DV), F32)
    zero_s5 = jnp.zeros((n_ctx, 2, S5_GROUPS, S5_STATE), F32)
    rope = rope_tables(x_sample.shape[1])
    xc = x_prompt
    xs = x_sample
    ret_list, s5r_list, s5i_list = [], [], []
    for l in range(DEPTH):
        p = {
            'w_ada': w_ada[l], 'b_ada': b_ada[l], 'norm_mix': norm_mix[l], 'norm_ffn': norm_ffn[l],
            'w_in': w_in[l], 'w_out': w_out[l], 'ret_decay': ret_decay[l],
            's5_lam_re': s5_lam_re[l], 's5_lam_im': s5_lam_im[l], 's5_log_dt': s5_log_dt[l],
            's5_b_re': s5_b_re[l], 's5_b_im': s5_b_im[l], 's5_c_re': s5_c_re[l], 's5_c_im': s5_c_im[l],
            's5_d': s5_d[l], 's5_glu_w': s5_glu_w[l], 's5_glu_b': s5_glu_b[l],
            'hy_conv_w': hy_conv_w[l], 'hy_conv_b': hy_conv_b[l], 'hy_f1_w': hy_f1_w[l], 'hy_f1_b': hy_f1_b[l],
            'hy_f2_w': hy_f2_w[l], 'hy_f2_b': hy_f2_b[l], 'hy_f3_w': hy_f3_w[l], 'hy_f3_b': hy_f3_b[l],
            'hy_freq': hy_freq[l], 'hy_decay': hy_decay[l], 'hy_bias': hy_bias[l],
            'moe_router': moe_router[l], 'moe_router_bias': moe_router_bias[l],
            'moe_w_gate': moe_w_gate[l], 'moe_w_up': moe_w_up[l], 'moe_w_down': moe_w_down[l],
            'sh_w_gate': sh_w_gate[l], 'sh_w_up': sh_w_up[l], 'sh_w_down': sh_w_down[l],
        }
        xc, ret_S, s5_re, s5_im = trunk_layer(xc, c_ctx, p, zero_ret, zero_s5, zero_s5, None)
        ret_list.append(ret_S)
        s5r_list.append(s5_re)
        s5i_list.append(s5_im)
        xs, _, _, _ = trunk_layer(xs, c, p, state_ret[:, l], state_s5_re[:, l], state_s5_im[:, l], rope)
    y_prompt = rmsnorm(xc, final_norm)
    y_sample = rmsnorm(xs, final_norm)
    new_state_ret = jnp.stack(ret_list, axis=1)
    new_state_s5_re = jnp.stack(s5r_list, axis=1)
    new_state_s5_im = jnp.stack(s5i_list, axis=1)
    return (y_prompt, y_sample, new_state_ret, new_state_s5_re, new_state_s5_im)
```

```python
import functools
import math

import jax
import jax.numpy as jnp
from jax import lax
from jax.experimental import pallas as pl
from jax.experimental.pallas import tpu as pltpu

F32 = jnp.float32
BF16 = jnp.bfloat16

D_MODEL = 2048
DEPTH = 2
GRID_W = 64
RET_HEADS = 8
RET_DK = 128
RET_DV = 128
RET_WIDTH = RET_HEADS * RET_DV
RET_CHUNK = 128
ROPE_BASE = 10000.0
S5_WIDTH = 512
S5_GROUP = 16
S5_GROUPS = S5_WIDTH // S5_GROUP
S5_STATE = 64
S5_Q = 16
HY_WIDTH = 512
HY_BANDS = 16
IN_WIDTH = 4 * RET_WIDTH + S5_WIDTH + 3 * HY_WIDTH
U_COL = 4 * RET_WIDTH
HY_COL = U_COL + S5_WIDTH
N_EXPERTS = 64
TOP_K = 6
D_EXPERT = 512
D_SHARED = 512
ROUTED_SCALE = 2.5
EPS = 1e-6

LANES = 128
SUBLANES = 8
VMEM_LIMIT = 56 * 1024 * 1024

ROW_TILE = 512
MOE_BM = 256


def _cparams(sem):
    return pltpu.CompilerParams(dimension_semantics=sem, vmem_limit_bytes=VMEM_LIMIT)


def _dot(a, b):
    return jnp.dot(a, b, preferred_element_type=F32)


def _rms(x, g):
    var = jnp.mean(x * x, axis=-1, keepdims=True)
    return x * lax.rsqrt(var + EPS) * g


def _ada_kernel(c_ref, w_ref, b_ref, o_ref):
    c = c_ref[...]
    s = (c * jax.nn.sigmoid(c)).astype(BF16)
    o_ref[0] = _dot(s, w_ref[0].astype(BF16)) + b_ref[0]


def _ada(cond8, w_ada, b_ada):
    tn = 512
    n = w_ada.shape[-1]
    return pl.pallas_call(
        _ada_kernel,
        grid=(DEPTH, n // tn),
        in_specs=[
            pl.BlockSpec((SUBLANES, D_MODEL), lambda l, j: (0, 0)),
            pl.BlockSpec((1, D_MODEL, tn), lambda l, j: (l, 0, j)),
            pl.BlockSpec((1, 1, tn), lambda l, j: (l, 0, j)),
        ],
        out_specs=pl.BlockSpec((1, SUBLANES, tn), lambda l, j: (l, 0, j)),
        out_shape=jax.ShapeDtypeStruct((DEPTH, SUBLANES, n), F32),
        compiler_params=_cparams(("parallel", "parallel")),
        name="ada",
    )(cond8, w_ada, b_ada.reshape(DEPTH, 1, n))


def _group_of(i, tile, n_ctx_rows, dec_seq):
    ctx_tiles = n_ctx_rows // tile
    per = dec_seq // tile
    return jnp.where(i < ctx_tiles, 0, 1 + (i - ctx_tiles) // per)


def _in_kernel(x_ref, mod_ref, g_ref, w_ref, o_ref, h_scr):
    @pl.when(pl.program_id(1) == 0)
    def _():
        m = mod_ref[0]
        h = _rms(x_ref[...], g_ref[...]) * (1.0 + m[1:2]) + m[0:1]
        h_scr[...] = h.astype(BF16)

    o_ref[...] = _dot(h_scr[...], w_ref[...])


def _in_proj(x, mod, g, w_bf, n_ctx_rows, dec_seq):
    t = x.shape[0]
    tm, tn = 1024, 512
    grp = functools.partial(_group_of, tile=tm, n_ctx_rows=n_ctx_rows, dec_seq=dec_seq)
    return pl.pallas_call(
        _in_kernel,
        grid=(t // tm, IN_WIDTH // tn),
        in_specs=[
            pl.BlockSpec((tm, D_MODEL), lambda i, j: (i, 0)),
            pl.BlockSpec((1, 6, D_MODEL), lambda i, j: (grp(i), 0, 0)),
            pl.BlockSpec((1, D_MODEL), lambda i, j: (0, 0)),
            pl.BlockSpec((D_MODEL, tn), lambda i, j: (0, j)),
        ],
        out_specs=pl.BlockSpec((tm, tn), lambda i, j: (i, j)),
        out_shape=jax.ShapeDtypeStruct((t, IN_WIDTH), F32),
        scratch_shapes=[pltpu.VMEM((tm, D_MODEL), BF16)],
        compiler_params=_cparams(("parallel", "arbitrary")),
        name="in_proj",
    )(x, mod, g.reshape(1, D_MODEL), w_bf)


def _ret_kernel(lg_ref, q_ref, k_ref, v_ref, gt_ref, cos_ref, sin_ref, s0_ref,
                o_ref, sfin_ref, acc_scr, q_scr, k_scr, *, seq_len, hb, rope):
    c = RET_CHUNK
    n_chunks = seq_len // c
    ii = lax.broadcasted_iota(jnp.int32, (c, c), 0)
    jj = lax.broadcasted_iota(jnp.int32, (c, c), 1)
    rel = (ii - jj).astype(F32)
    ci = lax.broadcasted_iota(jnp.int32, (c, 1), 0).astype(F32)
    one = jnp.ones((1, 1), F32)
    tdot = functools.partial(lax.dot_general, preferred_element_type=F32)

    for hh in range(hb):
        head = pl.program_id(1) * hb + hh
        lgf = lg_ref[0, head]
        lgb = lg_ref[1, head]
        dmask = (jnp.where(rel >= 0, jnp.exp(lgf * jnp.maximum(rel, 0.0)), 0.0)
                 + jnp.where(rel <= 0, jnp.exp(lgb * jnp.maximum(-rel, 0.0)), 0.0))
        qd_f = jnp.exp(lgf * (ci + 1.0))
        kd_f = jnp.exp(lgf * (c - 1.0 - ci))
        cd_f = jnp.exp(lgf * c * one)
        qd_b = jnp.exp(lgb * (c - ci))
        kd_b = jnp.exp(lgb * ci)
        cd_b = jnp.exp(lgb * c * one)
        lanes = slice(hh * LANES, (hh + 1) * LANES)

        def rows_of(n):
            if isinstance(n, int):
                return slice(n * c, (n + 1) * c)
            return pl.ds(pl.multiple_of(n * c, c), c)

        def fwd_chunk(n, s_f):
            rows = rows_of(n)
            q = q_ref[rows, lanes]
            k = k_ref[rows, lanes] * (RET_DK ** -0.5)
            if rope:
                cs = cos_ref[rows, :]
                sn = sin_ref[rows, :]
                q = q * cs + pltpu.roll(q, RET_DK // 2, 1) * sn
                k = k * cs + pltpu.roll(k, RET_DK // 2, 1) * sn
            qb = q.astype(BF16)
            vb = v_ref[rows, lanes].astype(BF16)
            q_scr[rows, :] = qb
            k_scr[rows, :] = k
            scores = tdot(qb, k.astype(BF16), (((1,), (1,)), ((), ()))) * dmask
            inner = _dot(scores.astype(BF16), vb)
            cross = _dot(qb, s_f.astype(BF16)) * qd_f
            acc_scr[rows, :] = inner + cross
            upd = tdot((k * kd_f).astype(BF16), vb, (((0,), (0,)), ((), ())))
            return s_f * cd_f + upd

        def bwd_chunk(m, s_b):
            n = n_chunks - 1 - m
            rows = rows_of(n)
            qb = q_scr[rows, :]
            k = k_scr[rows, :]
            vb = v_ref[rows, lanes].astype(BF16)
            o = acc_scr[rows, :] + _dot(qb, s_b.astype(BF16)) * qd_b
            mu = jnp.mean(o, axis=-1, keepdims=True)
            oc = o - mu
            var = jnp.mean(oc * oc, axis=-1, keepdims=True)
            o = oc * lax.rsqrt(var + EPS)
            g = gt_ref[rows, lanes]
            o_ref[rows, lanes] = (g * jax.nn.sigmoid(g) * o).astype(o_ref.dtype)
            upd = tdot((k * kd_b).astype(BF16), vb, (((0,), (0,)), ((), ())))
            return s_b * cd_b + upd

        s_f = s0_ref[0, 0, hh]
        s_b = s0_ref[0, 1, hh]
        if n_chunks <= 4:
            for n in range(n_chunks):
                s_f = fwd_chunk(n, s_f)
            for m in range(n_chunks):
                s_b = bwd_chunk(m, s_b)
        else:
            s_f = lax.fori_loop(0, n_chunks, fwd_chunk, s_f)
            s_b = lax.fori_loop(0, n_chunks, bwd_chunk, s_b)
        sfin_ref[0, 0, hh] = s_f
        sfin_ref[0, 1, hh] = s_b


def _retention(proj, log_gamma, s0, cos2, sin2, *, row0, n_seq, seq_len, hb, rope):
    blk0 = row0 // seq_len
    w = hb * LANES
    hblocks = RET_HEADS // hb
    col = lambda part: (lambda b, h, lg: (blk0 + b, part * hblocks + h))
    grid_spec = pltpu.PrefetchScalarGridSpec(
        num_scalar_prefetch=1,
        grid=(n_seq, hblocks),
        in_specs=[
            pl.BlockSpec((seq_len, w), col(0)),
            pl.BlockSpec((seq_len, w), col(1)),
            pl.BlockSpec((seq_len, w), col(2)),
            pl.BlockSpec((seq_len, w), col(3)),
            pl.BlockSpec((seq_len, LANES), lambda b, h, lg: (0, 0)),
            pl.BlockSpec((seq_len, LANES), lambda b, h, lg: (0, 0)),
            pl.BlockSpec((1, 2, hb, RET_DK, RET_DV), lambda b, h, lg: (b, 0, h, 0, 0)),
        ],
        out_specs=[
            pl.BlockSpec((seq_len, w), lambda b, h, lg: (b, h)),
            pl.BlockSpec((1, 2, hb, RET_DK, RET_DV), lambda b, h, lg: (b, 0, h, 0, 0)),
        ],
        scratch_shapes=[
            pltpu.VMEM((seq_len, LANES), F32),
            pltpu.VMEM((seq_len, LANES), BF16),
            pltpu.VMEM((seq_len, LANES), F32),
        ],
    )
    return pl.pallas_call(
        functools.partial(_ret_kernel, seq_len=seq_len, hb=hb, rope=rope),
        grid_spec=grid_spec,
        out_shape=[
            jax.ShapeDtypeStruct((n_seq * seq_len, RET_WIDTH), BF16),
            jax.ShapeDtypeStruct((n_seq, 2, RET_HEADS, RET_DK, RET_DV), F32),
        ],
        compiler_params=_cparams(("parallel", "arbitrary")),
        name="retention",
    )(log_gamma, proj, proj, proj, proj, cos2, sin2, s0)


def _rope_tables(seq_len):
    rows_n = seq_len // GRID_W
    rows = jnp.repeat(jnp.arange(rows_n, dtype=F32), GRID_W)
    cols = jnp.tile(jnp.arange(GRID_W, dtype=F32), rows_n)
    nf = RET_DK // 4
    inv = ROPE_BASE ** (-jnp.arange(nf, dtype=F32) / nf)
    ang = jnp.concatenate([rows[:, None] * inv, cols[:, None] * inv], axis=-1)
    cs, sn = jnp.cos(ang), jnp.sin(ang)
    return jnp.concatenate([cs, cs], axis=-1), jnp.concatenate([-sn, sn], axis=-1)


def _s5_kernel(u_ref, t_ref, bq_ref, cq_ref, ar_ref, ai_ref, d_ref, h0_ref,
               y_ref, hf_ref, sm_scr, hp_scr, *, n_chunks, rows):
    p = S5_STATE
    u = u_ref[0]
    ub = u.astype(BF16)
    sm_scr[...] = _dot(ub, bq_ref[0])
    ar = ar_ref[0]
    ai = ai_ref[0]
    is_fwd = lax.broadcasted_iota(jnp.int32, (rows, 2 * p), 1) < p
    h0 = h0_ref[0]

    def body(n, carry):
        hr, hi = carry
        rf = pl.ds(pl.multiple_of(n * rows, SUBLANES), rows)
        rb = pl.ds(pl.multiple_of((n_chunks - 1 - n) * rows, SUBLANES), rows)
        hp_scr[rf, 0:p] = hr[:, 0:p]
        hp_scr[rf, 2 * p:3 * p] = hi[:, 0:p]
        hp_scr[rb, p:2 * p] = hr[:, p:2 * p]
        hp_scr[rb, 3 * p:4 * p] = hi[:, p:2 * p]
        sf = sm_scr[rf, :]
        sb = sm_scr[rb, :]
        sr = jnp.where(is_fwd, sf[:, :2 * p], sb[:, :2 * p])
        si = jnp.where(is_fwd, sf[:, 2 * p:], sb[:, 2 * p:])
        return ar * hr - ai * hi + sr, ar * hi + ai * hr + si

    hr, hi = lax.fori_loop(0, n_chunks, body, (h0[:, :2 * p], h0[:, 2 * p:]))
    hf_ref[0] = jnp.concatenate([hr, hi], axis=1)
    y = _dot(ub, t_ref[0]) + _dot(hp_scr[...].astype(BF16), cq_ref[0]) + d_ref[0] * u
    y_ref[0] = jax.nn.gelu(y)


def _s5_mats(lam_re, lam_im, log_dt, b_re, b_im, c_re, c_im, d):
    q, g, p, ch = S5_Q, S5_GROUPS, S5_STATE, S5_GROUP
    lam = lax.complex(jnp.minimum(lam_re.astype(F32), -1e-4), lam_im.astype(F32))
    ldt = lam * jnp.exp(log_dt.astype(F32))[..., None]
    lam_bar = jnp.exp(ldt)
    b_bar = ((lam_bar - 1.0) / lam)[..., None] * lax.complex(b_re.astype(F32), b_im.astype(F32))
    cc = lax.complex(c_re.astype(F32), c_im.astype(F32))
    pw = jnp.exp(ldt[..., None] * jnp.arange(q + 1, dtype=F32))
    hi = lax.Precision.HIGHEST
    kern = jnp.real(jnp.einsum('zgcp,zgpd,zgpe->zgdce', cc, pw[..., :q], b_bar, precision=hi))
    i = jnp.arange(q)[None, :]
    j = jnp.arange(q)[:, None]
    dif = i - j
    kf = jnp.where((dif >= 0)[None, :, :, None, None], kern[0][:, jnp.maximum(dif, 0)], 0.0)
    kb = jnp.where((dif <= 0)[None, :, :, None, None], kern[1][:, jnp.maximum(-dif, 0)], 0.0)
    tm = (kf + kb).transpose(0, 1, 4, 2, 3).reshape(g, q * ch, q * ch)

    bf = pw[0][..., ::-1][..., 1:][:, :, :, None] * b_bar[0][:, :, None, :]
    bb = pw[1][..., :q][:, :, :, None] * b_bar[1][:, :, None, :]
    to_rows = lambda m: m.transpose(0, 2, 3, 1).reshape(g, q * ch, p)
    bq = jnp.concatenate([to_rows(jnp.real(bf)), to_rows(jnp.real(bb)),
                          to_rows(jnp.imag(bf)), to_rows(jnp.imag(bb))], axis=-1)

    cf = cc[0].transpose(0, 2, 1)[:, :, None, :] * pw[0][..., 1:][:, :, :, None]
    cb = cc[1].transpose(0, 2, 1)[:, :, None, :] * pw[1][..., ::-1][..., :q][:, :, :, None]
    to_cols = lambda m: m.reshape(g, p, q * ch)
    cq = jnp.concatenate([to_cols(jnp.real(cf)), to_cols(jnp.real(cb)),
                          to_cols(-jnp.imag(cf)), to_cols(-jnp.imag(cb))], axis=1)

    lq = pw[..., q]
    ar = jnp.concatenate([jnp.real(lq[0]), jnp.real(lq[1])], axis=-1)[:, None, :]
    ai = jnp.concatenate([jnp.imag(lq[0]), jnp.imag(lq[1])], axis=-1)[:, None, :]
    dd = jnp.tile(d.astype(F32).reshape(g, 1, ch), (1, 1, q))
    return tm.astype(BF16), bq.astype(BF16), cq.astype(BF16), ar, ai, dd


def _s5(u, mats, h0_re, h0_im, *, n_seq, seq_len):
    q, g, p, ch = S5_Q, S5_GROUPS, S5_STATE, S5_GROUP
    n_chunks = seq_len // q
    rows = -(-n_seq // SUBLANES) * SUBLANES
    m = n_chunks * rows
    w = q * ch
    tm, bq, cq, ar, ai, dd = mats
    ug = u.reshape(n_seq, n_chunks, q, g, ch).transpose(3, 1, 0, 2, 4)
    ug = jnp.pad(ug, ((0, 0), (0, 0), (0, rows - n_seq), (0, 0), (0, 0))).reshape(g, m, w)
    h0 = jnp.concatenate([h0_re[:, 0], h0_re[:, 1], h0_im[:, 0], h0_im[:, 1]], axis=-1).transpose(1, 0, 2)
    h0 = jnp.pad(h0.astype(F32), ((0, 0), (0, rows - n_seq), (0, 0)))
    per_g = lambda shape: pl.BlockSpec((1,) + shape, lambda i: (i, 0, 0))
    y, hf = pl.pallas_call(
        functools.partial(_s5_kernel, n_chunks=n_chunks, rows=rows),
        grid=(g,),
        in_specs=[per_g((m, w)), per_g((w, w)), per_g((w, 4 * p)), per_g((4 * p, w)),
                  per_g((1, 2 * p)), per_g((1, 2 * p)), per_g((1, w)), per_g((rows, 4 * p))],
        out_specs=[per_g((m, w)), per_g((rows, 4 * p))],
        out_shape=[jax.ShapeDtypeStruct((g, m, w), F32), jax.ShapeDtypeStruct((g, rows, 4 * p), F32)],
        scratch_shapes=[pltpu.VMEM((m, 4 * p), F32), pltpu.VMEM((m, 4 * p), F32)],
        compiler_params=_cparams(("parallel",)),
        name="s5",
    )(ug, tm, bq, cq, ar, ai, dd, h0)
    y = y.reshape(g, n_chunks, rows, q, ch)[:, :, :n_seq].transpose(2, 1, 3, 0, 4).reshape(n_seq * seq_len, g * ch)
    hf = hf[:, :n_seq].reshape(g, n_seq, 4, p).transpose(1, 2, 0, 3)
    return y, hf[:, 0:2], hf[:, 2:4]


def _conv3(x, w, b):
    n = x.shape[0]
    row = lax.broadcasted_iota(jnp.int32, x.shape, 0)
    prev = jnp.where(row == 0, 0.0, pltpu.roll(x, 1, 0))
    nxt = jnp.where(row == n - 1, 0.0, pltpu.roll(x, n - 1, 0))
    return prev * w[0:1] + x * w[1:2] + nxt * w[2:3] + b


def _hy_fwd_kernel(x0_ref, x1_ref, v_ref, w0_ref, w1_ref, wv_ref, b0_ref, b1_ref, bv_ref,
                   fc_ref, fs_ref, m1_ref, m2_ref, m3_ref, p_ref, z_ref, x0c_ref, zb_scr):
    @pl.when(pl.program_id(2) == 0)
    def _():
        z = _conv3(x1_ref[...], w1_ref[...], b1_ref[...]) * _conv3(v_ref[...], wv_ref[...], bv_ref[...])
        z_ref[...] = z
        zb_scr[...] = z.astype(BF16)
        x0c_ref[...] = _conv3(x0_ref[...], w0_ref[...], b0_ref[...])

    zb = zb_scr[...]
    a = _dot(fc_ref[...], zb)
    b = _dot(fs_ref[...], zb)
    m2 = m2_ref[...]
    p_ref[0, 0] = (m1_ref[...] * a + m2 * b).astype(BF16)
    p_ref[0, 1] = (m3_ref[...] * b - m2 * a).astype(BF16)


def _hy_inv_kernel(p_ref, gc_ref, gs_ref, z_ref, x0c_ref, bias_ref, o_ref):
    conv = _dot(gc_ref[...], p_ref[0, 0]) + _dot(gs_ref[...], p_ref[0, 1])
    o_ref[...] = (x0c_ref[...] * (conv + bias_ref[...] * z_ref[...])).astype(o_ref.dtype)


def _dft_mats(seq_len):
    k = jnp.arange(seq_len, dtype=jnp.int32)
    kj = (k[:, None] * k[None, :]) % (2 * seq_len)
    ang = kj.astype(F32) * (math.pi / seq_len)
    cm = jnp.cos(ang)
    sm = -jnp.sin(ang)
    nyq = jnp.where(k % 2 == 0, 1.0, -1.0).astype(F32)
    sm = sm.at[0, :].set(nyq)
    return cm.astype(BF16), sm.astype(BF16), sm.T.astype(BF16)


def _hy_filter_mults(seq_len, f1_w, f1_b, f2_w, f2_b, f3_w, f3_b, freq, decay):
    n = seq_len
    t = (jnp.arange(n, dtype=F32) / n)[:, None]
    bands = jnp.arange(1, HY_BANDS + 1, dtype=F32)[None, :]
    z = jnp.concatenate([t, jnp.cos(2.0 * math.pi * t * bands), jnp.sin(2.0 * math.pi * t * bands)], axis=-1)
    hi = lax.Precision.HIGHEST
    fr = freq.astype(F32)
    h = jnp.sin(fr * (jnp.dot(z, f1_w.astype(F32), precision=hi) + f1_b.astype(F32)))
    h = jnp.sin(fr * (jnp.dot(h, f2_w.astype(F32), precision=hi) + f2_b.astype(F32)))
    h = jnp.dot(h, f3_w.astype(F32), precision=hi) + f3_b.astype(F32)
    h = h * jnp.exp(-t * jnp.abs(decay.astype(F32)))
    h = h.reshape(n, 2, HY_WIDTH)
    h = h / jnp.sum(jnp.abs(h), axis=(0, 1), keepdims=True)
    hc = jnp.concatenate([h[:, 0], h[::-1, 1]], axis=0)
    spec = jnp.fft.rfft(hc, axis=0)
    hr, him = jnp.real(spec), jnp.imag(spec)
    wk = jnp.where(jnp.arange(n) == 0, 1.0, 2.0)[:, None] / (2.0 * n)
    first = (jnp.arange(n) == 0)[:, None]
    m1 = hr[:n] * wk
    m2 = jnp.where(first, 0.0, -him[:n]) * wk
    m3 = jnp.where(first, hr[n:n + 1], hr[:n]) * wk
    return m1.astype(F32), m2.astype(F32), m3.astype(F32)


def _hyena(proj, conv_w, conv_b, bias, dft, mults, *, row0, n_seq, seq_len, cb, tk):
    blk0 = row0 // seq_len
    nc = HY_WIDTH // cb
    nk = seq_len // tk
    c0 = HY_COL // cb
    cm, sm, smt = dft
    m1, m2, m3 = mults
    xcol = lambda part: pl.BlockSpec((seq_len, cb), lambda b, c, k: (blk0 + b, c0 + part * nc + c))
    wcol = lambda part: pl.BlockSpec((3, cb), lambda b, c, k: (0, part * nc + c))
    bcol = lambda part: pl.BlockSpec((1, cb), lambda b, c, k: (0, part * nc + c))
    frow = pl.BlockSpec((tk, seq_len), lambda b, c, k: (k, 0))
    mblk = pl.BlockSpec((tk, cb), lambda b, c, k: (k, c))
    cb2 = conv_b.reshape(1, 3 * HY_WIDTH)
    pspec, z, x0c = pl.pallas_call(
        _hy_fwd_kernel,
        grid=(n_seq, nc, nk),
        in_specs=[xcol(0), xcol(1), xcol(2), wcol(0), wcol(1), wcol(2), bcol(0), bcol(1), bcol(2),
                  frow, frow, mblk, mblk, mblk],
        out_specs=[
            pl.BlockSpec((1, 2, tk, cb), lambda b, c, k: (b, 0, k, c)),
            pl.BlockSpec((seq_len, cb), lambda b, c, k: (b, c)),
            pl.BlockSpec((seq_len, cb), lambda b, c, k: (b, c)),
        ],
        out_shape=[
            jax.ShapeDtypeStruct((n_seq, 2, seq_len, HY_WIDTH), BF16),
            jax.ShapeDtypeStruct((n_seq * seq_len, HY_WIDTH), F32),
            jax.ShapeDtypeStruct((n_seq * seq_len, HY_WIDTH), F32),
        ],
        scratch_shapes=[pltpu.VMEM((seq_len, cb), BF16)],
        compiler_params=_cparams(("parallel", "parallel", "arbitrary")),
        name="hyena_fwd",
    )(proj, proj, proj, conv_w, conv_w, conv_w, cb2, cb2, cb2, cm, sm, m1, m2, m3)
    grow = pl.BlockSpec((tk, seq_len), lambda b, c, k: (k, 0))
    tile = pl.BlockSpec((tk, cb), lambda b, c, k: (b * nk + k, c))
    return pl.pallas_call(
        _hy_inv_kernel,
        grid=(n_seq, nc, nk),
        in_specs=[pl.BlockSpec((1, 2, seq_len, cb), lambda b, c, k: (b, 0, 0, c)),
                  grow, grow, tile, tile, pl.BlockSpec((1, cb), lambda b, c, k: (0, c))],
        out_specs=tile,
        out_shape=jax.ShapeDtypeStruct((n_seq * seq_len, HY_WIDTH), BF16),
        compiler_params=_cparams(("parallel", "parallel", "arbitrary")),
        name="hyena_inv",
    )(pspec, cm, smt, z, x0c, bias.reshape(1, HY_WIDTH))


def _out_kernel(x_ref, ret_ref, s5_ref, hy_ref, mod_ref, g_ref, gw_ref, gb_ref,
                wr_ref, ws_ref, wh_ref, rt_ref, xo_ref, h_ref, lg_ref):
    m = mod_ref[0]
    y = s5_ref[...]
    s5o = y * jax.nn.sigmoid(_dot(y.astype(BF16), gw_ref[...]) + gb_ref[...])
    mix = (_dot(ret_ref[...], wr_ref[...]) + _dot(s5o.astype(BF16), ws_ref[...])
           + _dot(hy_ref[...], wh_ref[...]))
    x = x_ref[...] + m[2:3] * mix
    xo_ref[...] = x
    h = _rms(x, g_ref[...]) * (1.0 + m[4:5]) + m[3:4]
    hb = h.astype(BF16)
    h_ref[...] = hb
    lg_ref[...] = _dot(hb, rt_ref[...])


def _out_proj(x, ret_o, s5_y, hy_o, mod, g, glu_w, glu_b, w_out, router, n_ctx_rows, dec_seq):
    t = x.shape[0]
    tm = ROW_TILE
    grp = functools.partial(_group_of, tile=tm, n_ctx_rows=n_ctx_rows, dec_seq=dec_seq)
    row = lambda w: pl.BlockSpec((tm, w), lambda i: (i, 0))
    full = lambda a, b: pl.BlockSpec((a, b), lambda i: (0, 0))
    wo = w_out.astype(BF16)
    return pl.pallas_call(
        _out_kernel,
        grid=(t // tm,),
        in_specs=[row(D_MODEL), row(RET_WIDTH), row(S5_WIDTH), row(HY_WIDTH),
                  pl.BlockSpec((1, 6, D_MODEL), lambda i: (grp(i), 0, 0)),
                  full(1, D_MODEL), full(S5_WIDTH, S5_WIDTH), full(1, S5_WIDTH),
                  full(RET_WIDTH, D_MODEL), full(S5_WIDTH, D_MODEL), full(HY_WIDTH, D_MODEL),
                  full(D_MODEL, N_EXPERTS)],
        out_specs=[row(D_MODEL), row(D_MODEL), row(N_EXPERTS)],
        out_shape=[jax.ShapeDtypeStruct((t, D_MODEL), F32), jax.ShapeDtypeStruct((t, D_MODEL), BF16),
                   jax.ShapeDtypeStruct((t, N_EXPERTS), F32)],
        compiler_params=_cparams(("parallel",)),
        name="out_proj",
    )(x, ret_o, s5_y, hy_o, mod, g.reshape(1, D_MODEL), glu_w.astype(BF16), glu_b.reshape(1, S5_WIDTH),
      wo[:RET_WIDTH], wo[RET_WIDTH:RET_WIDTH + S5_WIDTH], wo[RET_WIDTH + S5_WIDTH:], router.astype(BF16))


def _moe_kernel(be_ref, nu_ref, xs_ref, sw_ref, wg_ref, wu_ref, wd_ref, o_ref, wg_scr, wu_scr, wd_scr):
    i = pl.program_id(0)
    changed = jnp.logical_or(i == 0, be_ref[i] != be_ref[jnp.maximum(i - 1, 0)])

    @pl.when(jnp.logical_and(changed, i < nu_ref[0]))
    def _():
        wg_scr[...] = wg_ref[0].astype(BF16)
        wu_scr[...] = wu_ref[0].astype(BF16)
        wd_scr[...] = wd_ref[0].astype(BF16)

    @pl.when(i < nu_ref[0])
    def _():
        xb = xs_ref[...]
        gate = _dot(xb, wg_scr[...])
        hb = gate * jax.nn.sigmoid(gate) * _dot(xb, wu_scr[...])
        o_ref[...] = _dot(hb.astype(BF16), wd_scr[...]) * sw_ref[...]

    @pl.when(i >= nu_ref[0])
    def _():
        o_ref[...] = jnp.zeros_like(o_ref)


def _moe_grouped(xs, slot_w, blk_e, n_used, w_gate, w_up, w_down):
    pr = xs.shape[0]
    bm = MOE_BM
    nb = pr // bm
    grid_spec = pltpu.PrefetchScalarGridSpec(
        num_scalar_prefetch=2,
        grid=(nb,),
        in_specs=[
            pl.BlockSpec((bm, D_MODEL), lambda i, be, nu: (i, 0)),
            pl.BlockSpec((bm, 1), lambda i, be, nu: (i, 0)),
            pl.BlockSpec((1, D_MODEL, D_EXPERT), lambda i, be, nu: (be[i], 0, 0)),
            pl.BlockSpec((1, D_MODEL, D_EXPERT), lambda i, be, nu: (be[i], 0, 0)),
            pl.BlockSpec((1, D_EXPERT, D_MODEL), lambda i, be, nu: (be[i], 0, 0)),
        ],
        out_specs=pl.BlockSpec((bm, D_MODEL), lambda i, be, nu: (i, 0)),
        scratch_shapes=[pltpu.VMEM((D_MODEL, D_EXPERT), BF16), pltpu.VMEM((D_MODEL, D_EXPERT), BF16),
                        pltpu.VMEM((D_EXPERT, D_MODEL), BF16)],
    )
    return pl.pallas_call(
        _moe_kernel,
        grid_spec=grid_spec,
        out_shape=jax.ShapeDtypeStruct((pr, D_MODEL), F32),
        compiler_params=_cparams(("arbitrary",)),
        name="moe_grouped",
    )(blk_e, n_used, xs, slot_w.reshape(pr, 1), w_gate, w_up, w_down)


def _route(logits, router_bias):
    t = logits.shape[0]
    bm = MOE_BM
    scores = jax.nn.sigmoid(logits)
    _, idx = lax.top_k(scores + router_bias.astype(F32), TOP_K)
    sel = jnp.take_along_axis(scores, idx, axis=1)
    gates = ROUTED_SCALE * sel / jnp.sum(sel, axis=-1, keepdims=True)
    flat_e = idx.reshape(-1).astype(jnp.int32)
    flat_tok = jnp.repeat(jnp.arange(t, dtype=jnp.int32), TOP_K)
    order = jnp.argsort(flat_e)
    se = flat_e[order]
    counts = jnp.zeros((N_EXPERTS,), jnp.int32).at[flat_e].add(1)
    padded = (counts + bm - 1) // bm * bm
    pend = jnp.cumsum(padded)
    pstart = pend - padded
    ustart = jnp.cumsum(counts) - counts
    dest = pstart[se] + jnp.arange(t * TOP_K, dtype=jnp.int32) - ustart[se]
    nb = -(-(t * TOP_K) // bm) + N_EXPERTS
    slot_tok = jnp.zeros((nb * bm,), jnp.int32).at[dest].set(flat_tok[order])
    slot_w = jnp.zeros((nb * bm,), F32).at[dest].set(gates.reshape(-1)[order])
    blk_e = jnp.minimum(jnp.searchsorted(pend, jnp.arange(nb, dtype=jnp.int32) * bm, side='right'),
                        N_EXPERTS - 1).astype(jnp.int32)
    pos = jnp.zeros((t * TOP_K,), jnp.int32).at[order].set(dest).reshape(t, TOP_K)
    n_used = (pend[-1] // bm).astype(jnp.int32).reshape(1)
    return slot_tok, slot_w, blk_e, n_used, pos


def _shared_kernel(x_ref, h_ref, r_ref, mod_ref, sg_ref, su_ref, sd_ref, fn_ref, o_ref, *, final):
    m = mod_ref[0]
    hb = h_ref[...]
    gate = _dot(hb, sg_ref[...])
    act = gate * jax.nn.sigmoid(gate) * _dot(hb, su_ref[...])
    shared = _dot(act.astype(BF16), sd_ref[...])
    x = x_ref[...] + m[5:6] * (r_ref[...] + shared)
    if final:
        x = _rms(x, fn_ref[...])
    o_ref[...] = x


def _shared(x, h, routed, mod, sg, su, sd, final_norm, n_ctx_rows, dec_seq, *, final, row0=0, rows=None):
    rows = x.shape[0] if rows is None else rows
    tm = ROW_TILE
    b0 = row0 // tm
    grp = lambda i: _group_of(i + b0, tm, n_ctx_rows, dec_seq)
    row = pl.BlockSpec((tm, D_MODEL), lambda i: (i + b0, 0))
    full = lambda a, b: pl.BlockSpec((a, b), lambda i: (0, 0))
    return pl.pallas_call(
        functools.partial(_shared_kernel, final=final),
        grid=(rows // tm,),
        in_specs=[row, row, row, pl.BlockSpec((1, 6, D_MODEL), lambda i: (grp(i), 0, 0)),
                  full(D_MODEL, D_SHARED), full(D_MODEL, D_SHARED), full(D_SHARED, D_MODEL), full(1, D_MODEL)],
        out_specs=pl.BlockSpec((tm, D_MODEL), lambda i: (i, 0)),
        out_shape=jax.ShapeDtypeStruct((rows, D_MODEL), F32),
        compiler_params=_cparams(("parallel",)),
        name="shared_final" if final else "shared",
    )(x, h, routed, mod, sg.astype(BF16), su.astype(BF16), sd.astype(BF16), final_norm.reshape(1, D_MODEL))


def kernel(x_prompt, x_sample, state_ret, state_s5_re, state_s5_im, c, c_ctx, w_ada, b_ada, norm_mix, norm_ffn, w_in, w_out, ret_decay, s5_lam_re, s5_lam_im, s5_log_dt, s5_b_re, s5_b_im, s5_c_re, s5_c_im, s5_d, s5_glu_w, s5_glu_b, hy_conv_w, hy_conv_b, hy_f1_w, hy_f1_b, hy_f2_w, hy_f2_b, hy_f3_w, hy_f3_b, hy_freq, hy_decay, hy_bias, moe_router, moe_router_bias, moe_w_gate, moe_w_up, moe_w_down, sh_w_gate, sh_w_up, sh_w_down, final_norm):
    n_ctx, seq, d = x_prompt.shape
    n_dec, dec_seq, _ = x_sample.shape
    n_ctx_rows = n_ctx * seq
    t = n_ctx_rows + n_dec * dec_seq

    x = jnp.concatenate([x_prompt.reshape(n_ctx_rows, d), x_sample.reshape(n_dec * dec_seq, d)], axis=0)
    cond = jnp.concatenate([c_ctx[None, :], c], axis=0)
    cond8 = jnp.pad(cond, ((0, SUBLANES - cond.shape[0]), (0, 0)))
    mods = _ada(cond8, w_ada, b_ada)[:, :1 + n_dec].reshape(DEPTH, 1 + n_dec, 6, d)

    cos2, sin2 = _rope_tables(dec_seq)
    no_rope = jnp.zeros((seq, LANES), F32)
    zero_ret = jnp.zeros((n_ctx, 2, RET_HEADS, RET_DK, RET_DV), F32)
    zero_s5 = jnp.zeros((n_ctx, 2, S5_GROUPS, S5_STATE), F32)
    dft_ctx = _dft_mats(seq)
    dft_dec = _dft_mats(dec_seq)

    ret_list, s5r_list, s5i_list = [], [], []
    for l in range(DEPTH):
        mod = mods[l]
        proj = _in_proj(x, mod, norm_mix[l], w_in[l].astype(BF16), n_ctx_rows, dec_seq)

        log_gamma = jax.nn.log_sigmoid(ret_decay[l].astype(F32))
        ret_c, ret_s = _retention(proj, log_gamma, zero_ret, no_rope, no_rope,
                                  row0=0, n_seq=n_ctx, seq_len=seq, hb=RET_HEADS, rope=False)
        ret_d, _ = _retention(proj, log_gamma, state_ret[:, l].astype(F32), cos2, sin2,
                              row0=n_ctx_rows, n_seq=n_dec, seq_len=dec_seq, hb=1, rope=True)
        ret_list.append(ret_s)

        mats = _s5_mats(s5_lam_re[l], s5_lam_im[l], s5_log_dt[l], s5_b_re[l], s5_b_im[l],
                        s5_c_re[l], s5_c_im[l], s5_d[l])
        u = proj[:, U_COL:U_COL + S5_WIDTH]
        s5_c, s5_re, s5_im = _s5(u[:n_ctx_rows], mats, zero_s5, zero_s5, n_seq=n_ctx, seq_len=seq)
        s5_d_, _, _ = _s5(u[n_ctx_rows:], mats, state_s5_re[:, l], state_s5_im[:, l], n_seq=n_dec, seq_len=dec_seq)
        s5r_list.append(s5_re)
        s5i_list.append(s5_im)

        filt = (hy_f1_w[l], hy_f1_b[l], hy_f2_w[l], hy_f2_b[l], hy_f3_w[l], hy_f3_b[l], hy_freq[l], hy_decay[l])
        hy_c = _hyena(proj, hy_conv_w[l], hy_conv_b[l], hy_bias[l], dft_ctx, _hy_filter_mults(seq, *filt),
                      row0=0, n_seq=n_ctx, seq_len=seq, cb=HY_WIDTH, tk=seq)
        hy_d = _hyena(proj, hy_conv_w[l], hy_conv_b[l], hy_bias[l], dft_dec, _hy_filter_mults(dec_seq, *filt),
                      row0=n_ctx_rows, n_seq=n_dec, seq_len=dec_seq, cb=HY_WIDTH // 2, tk=512)

        ret_o = jnp.concatenate([ret_c, ret_d], axis=0)
        s5_y = jnp.concatenate([s5_c, s5_d_], axis=0)
        hy_o = jnp.concatenate([hy_c, hy_d], axis=0)
        x, h2, logits = _out_proj(x, ret_o, s5_y, hy_o, mod, norm_ffn[l], s5_glu_w[l], s5_glu_b[l],
                                  w_out[l], moe_router[l], n_ctx_rows, dec_seq)

        slot_tok, slot_w, blk_e, n_used, pos = _route(logits, moe_router_bias[l])
        xs = jnp.take(h2, slot_tok, axis=0)
        eo = _moe_grouped(xs, slot_w, blk_e, n_used, moe_w_gate[l], moe_w_up[l], moe_w_down[l])
        routed = jnp.sum(jnp.take(eo, pos, axis=0), axis=1)

        sh = (sh_w_gate[l], sh_w_up[l], sh_w_down[l])
        if l < DEPTH - 1:
            x = _shared(x, h2, routed, mod, *sh, final_norm, n_ctx_rows, dec_seq, final=False)
        else:
            y_c = _shared(x, h2, routed, mod, *sh, final_norm, n_ctx_rows, dec_seq, final=True,
                          row0=0, rows=n_ctx_rows)
            y_d = _shared(x, h2, routed, mod, *sh, final_norm, n_ctx_rows, dec_seq, final=True,
                          row0=n_ctx_rows, rows=n_dec * dec_seq)

    return (y_c.reshape(n_ctx, seq, d), y_d.reshape(n_dec, dec_seq, d),
            jnp.stack(ret_list, axis=1), jnp.stack(s5r_list, axis=1), jnp.stack(s5i_list, axis=1))
```

```python
import functools
import math

import jax
import jax.numpy as jnp
from jax import lax
from jax.experimental import pallas as pl
from jax.experimental.pallas import tpu as pltpu

F32 = jnp.float32
BF16 = jnp.bfloat16

D_MODEL = 2048
DEPTH = 2
GRID_W = 64
RET_HEADS = 8
RET_DK = 128
RET_DV = 128
RET_WIDTH = RET_HEADS * RET_DV
RET_CHUNK = 128
ROPE_BASE = 10000.0
S5_WIDTH = 512
S5_GROUP = 16
S5_GROUPS = S5_WIDTH // S5_GROUP
S5_STATE = 64
S5_Q = 16
HY_WIDTH = 512
HY_BANDS = 16
IN_WIDTH = 4 * RET_WIDTH + S5_WIDTH + 3 * HY_WIDTH
U_COL = 4 * RET_WIDTH
HY_COL = U_COL + S5_WIDTH
N_EXPERTS = 64
TOP_K = 6
D_EXPERT = 512
D_SHARED = 512
ROUTED_SCALE = 2.5
EPS = 1e-6

LANES = 128
SUBLANES = 8
VMEM_LIMIT = 56 * 1024 * 1024

ROW_TILE = 512
MOE_BM = 256


def _cparams(sem):
    return pltpu.CompilerParams(dimension_semantics=sem, vmem_limit_bytes=VMEM_LIMIT)


def _dot(a, b):
    return jnp.dot(a, b, preferred_element_type=F32)


def _rms(x, g):
    var = jnp.mean(x * x, axis=-1, keepdims=True)
    return x * lax.rsqrt(var + EPS) * g


def _ada_kernel(c_ref, w_ref, b_ref, o_ref):
    c = c_ref[...]
    s = (c * jax.nn.sigmoid(c)).astype(BF16)
    o_ref[0] = _dot(s, w_ref[0].astype(BF16)) + b_ref[0]


def _ada(cond8, w_ada, b_ada):
    tn = 512
    n = w_ada.shape[-1]
    return pl.pallas_call(
        _ada_kernel,
        grid=(DEPTH, n // tn),
        in_specs=[
            pl.BlockSpec((SUBLANES, D_MODEL), lambda l, j: (0, 0)),
            pl.BlockSpec((1, D_MODEL, tn), lambda l, j: (l, 0, j)),
            pl.BlockSpec((1, 1, tn), lambda l, j: (l, 0, j)),
        ],
        out_specs=pl.BlockSpec((1, SUBLANES, tn), lambda l, j: (l, 0, j)),
        out_shape=jax.ShapeDtypeStruct((DEPTH, SUBLANES, n), F32),
        compiler_params=_cparams(("parallel", "parallel")),
        name="ada",
    )(cond8, w_ada, b_ada.reshape(DEPTH, 1, n))


def _group_of(i, tile, n_ctx_rows, dec_seq):
    ctx_tiles = n_ctx_rows // tile
    per = dec_seq // tile
    return jnp.where(i < ctx_tiles, 0, 1 + (i - ctx_tiles) // per)


def _in_kernel(x_ref, mod_ref, g_ref, w_ref, o_ref, h_scr):
    @pl.when(pl.program_id(1) == 0)
    def _():
        m = mod_ref[0]
        h = _rms(x_ref[...], g_ref[...]) * (1.0 + m[1:2]) + m[0:1]
        h_scr[...] = h.astype(BF16)

    o_ref[...] = _dot(h_scr[...], w_ref[...])


def _in_proj(x, mod, g, w_bf, n_ctx_rows, dec_seq):
    t = x.shape[0]
    tm, tn = 1024, 512
    grp = functools.partial(_group_of, tile=tm, n_ctx_rows=n_ctx_rows, dec_seq=dec_seq)
    return pl.pallas_call(
        _in_kernel,
        grid=(t // tm, IN_WIDTH // tn),
        in_specs=[
            pl.BlockSpec((tm, D_MODEL), lambda i, j: (i, 0)),
            pl.BlockSpec((1, 6, D_MODEL), lambda i, j: (grp(i), 0, 0)),
            pl.BlockSpec((1, D_MODEL), lambda i, j: (0, 0)),
            pl.BlockSpec((D_MODEL, tn), lambda i, j: (0, j)),
        ],
        out_specs=pl.BlockSpec((tm, tn), lambda i, j: (i, j)),
        out_shape=jax.ShapeDtypeStruct((t, IN_WIDTH), F32),
        scratch_shapes=[pltpu.VMEM((tm, D_MODEL), BF16)],
        compiler_params=_cparams(("parallel", "arbitrary")),
        name="in_proj",
    )(x, mod, g.reshape(1, D_MODEL), w_bf)


def _ret_kernel(lg_ref, q_ref, k_ref, v_ref, gt_ref, cos_ref, sin_ref, s0_ref,
                o_ref, sfin_ref, acc_scr, q_scr, k_scr, *, seq_len, hb, rope):
    c = RET_CHUNK
    n_chunks = seq_len // c
    ii = lax.broadcasted_iota(jnp.int32, (c, c), 0)
    jj = lax.broadcasted_iota(jnp.int32, (c, c), 1)
    rel = (ii - jj).astype(F32)
    ci = lax.broadcasted_iota(jnp.int32, (c, 1), 0).astype(F32)
    one = jnp.ones((1, 1), F32)
    tdot = functools.partial(lax.dot_general, preferred_element_type=F32)

    for hh in range(hb):
        head = pl.program_id(1) * hb + hh
        lgf = lg_ref[0, head]
        lgb = lg_ref[1, head]
        dmask = (jnp.where(rel >= 0, jnp.exp(lgf * jnp.maximum(rel, 0.0)), 0.0)
                 + jnp.where(rel <= 0, jnp.exp(lgb * jnp.maximum(-rel, 0.0)), 0.0))
        qd_f = jnp.exp(lgf * (ci + 1.0))
        kd_f = jnp.exp(lgf * (c - 1.0 - ci))
        cd_f = jnp.exp(lgf * c * one)
        qd_b = jnp.exp(lgb * (c - ci))
        kd_b = jnp.exp(lgb * ci)
        cd_b = jnp.exp(lgb * c * one)
        lanes = slice(hh * LANES, (hh + 1) * LANES)

        def rows_of(n):
            if isinstance(n, int):
                return slice(n * c, (n + 1) * c)
            return pl.ds(pl.multiple_of(n * c, c), c)

        def fwd_chunk(n, s_f):
            rows = rows_of(n)
            q = q_ref[rows, lanes]
            k = k_ref[rows, lanes] * (RET_DK ** -0.5)
            if rope:
                cs = cos_ref[rows, :]
                sn = sin_ref[rows, :]
                q = q * cs + pltpu.roll(q, RET_DK // 2, 1) * sn
                k = k * cs + pltpu.roll(k, RET_DK // 2, 1) * sn
            qb = q.astype(BF16)
            vb = v_ref[rows, lanes].astype(BF16)
            q_scr[rows, :] = qb
            k_scr[rows, :] = k
            scores = tdot(qb, k.astype(BF16), (((1,), (1,)), ((), ()))) * dmask
            inner = _dot(scores.astype(BF16), vb)
            cross = _dot(qb, s_f.astype(BF16)) * qd_f
            acc_scr[rows, :] = inner + cross
            upd = tdot((k * kd_f).astype(BF16), vb, (((0,), (0,)), ((), ())))
            return s_f * cd_f + upd

        def bwd_chunk(m, s_b):
            n = n_chunks - 1 - m
            rows = rows_of(n)
            qb = q_scr[rows, :]
            k = k_scr[rows, :]
            vb = v_ref[rows, lanes].astype(BF16)
            o = acc_scr[rows, :] + _dot(qb, s_b.astype(BF16)) * qd_b
            mu = jnp.mean(o, axis=-1, keepdims=True)
            oc = o - mu
            var = jnp.mean(oc * oc, axis=-1, keepdims=True)
            o = oc * lax.rsqrt(var + EPS)
            g = gt_ref[rows, lanes]
            o_ref[rows, lanes] = (g * jax.nn.sigmoid(g) * o).astype(o_ref.dtype)
            upd = tdot((k * kd_b).astype(BF16), vb, (((0,), (0,)), ((), ())))
            return s_b * cd_b + upd

        s_f = s0_ref[0, 0, hh]
        s_b = s0_ref[0, 1, hh]
        if n_chunks <= 4:
            for n in range(n_chunks):
                s_f = fwd_chunk(n, s_f)
            for m in range(n_chunks):
                s_b = bwd_chunk(m, s_b)
        else:
            s_f = lax.fori_loop(0, n_chunks, fwd_chunk, s_f)
            s_b = lax.fori_loop(0, n_chunks, bwd_chunk, s_b)
        sfin_ref[0, 0, hh] = s_f
        sfin_ref[0, 1, hh] = s_b


def _retention(proj, log_gamma, s0, cos2, sin2, *, row0, n_seq, seq_len, hb, rope):
    blk0 = row0 // seq_len
    w = hb * LANES
    hblocks = RET_HEADS // hb
    col = lambda part: (lambda b, h, lg: (blk0 + b, part * hblocks + h))
    grid_spec = pltpu.PrefetchScalarGridSpec(
        num_scalar_prefetch=1,
        grid=(n_seq, hblocks),
        in_specs=[
            pl.BlockSpec((seq_len, w), col(0)),
            pl.BlockSpec((seq_len, w), col(1)),
            pl.BlockSpec((seq_len, w), col(2)),
            pl.BlockSpec((seq_len, w), col(3)),
            pl.BlockSpec((seq_len, LANES), lambda b, h, lg: (0, 0)),
            pl.BlockSpec((seq_len, LANES), lambda b, h, lg: (0, 0)),
            pl.BlockSpec((1, 2, hb, RET_DK, RET_DV), lambda b, h, lg: (b, 0, h, 0, 0)),
        ],
        out_specs=[
            pl.BlockSpec((seq_len, w), lambda b, h, lg: (b, h)),
            pl.BlockSpec((1, 2, hb, RET_DK, RET_DV), lambda b, h, lg: (b, 0, h, 0, 0)),
        ],
        scratch_shapes=[
            pltpu.VMEM((seq_len, LANES), F32),
            pltpu.VMEM((seq_len, LANES), BF16),
            pltpu.VMEM((seq_len, LANES), F32),
        ],
    )
    return pl.pallas_call(
        functools.partial(_ret_kernel, seq_len=seq_len, hb=hb, rope=rope),
        grid_spec=grid_spec,
        out_shape=[
            jax.ShapeDtypeStruct((n_seq * seq_len, RET_WIDTH), BF16),
            jax.ShapeDtypeStruct((n_seq, 2, RET_HEADS, RET_DK, RET_DV), F32),
        ],
        compiler_params=_cparams(("parallel", "arbitrary")),
        name="retention",
    )(log_gamma, proj, proj, proj, proj, cos2, sin2, s0)


def _rope_tables(seq_len):
    rows_n = seq_len // GRID_W
    rows = jnp.repeat(jnp.arange(rows_n, dtype=F32), GRID_W)
    cols = jnp.tile(jnp.arange(GRID_W, dtype=F32), rows_n)
    nf = RET_DK // 4
    inv = ROPE_BASE ** (-jnp.arange(nf, dtype=F32) / nf)
    ang = jnp.concatenate([rows[:, None] * inv, cols[:, None] * inv], axis=-1)
    cs, sn = jnp.cos(ang), jnp.sin(ang)
    return jnp.concatenate([cs, cs], axis=-1), jnp.concatenate([-sn, sn], axis=-1)


def _s5_kernel(u_ref, t_ref, bq_ref, cq_ref, ar_ref, ai_ref, d_ref, h0_ref,
               y_ref, hf_ref, sm_scr, hp_scr, *, n_chunks, rows):
    p = S5_STATE
    u = u_ref[0]
    ub = u.astype(BF16)
    sm_scr[...] = _dot(ub, bq_ref[0])
    ar = ar_ref[0]
    ai = ai_ref[0]
    is_fwd = lax.broadcasted_iota(jnp.int32, (rows, 2 * p), 1) < p
    h0 = h0_ref[0]

    def body(n, carry):
        hr, hi = carry
        rf = pl.ds(pl.multiple_of(n * rows, SUBLANES), rows)
        rb = pl.ds(pl.multiple_of((n_chunks - 1 - n) * rows, SUBLANES), rows)
        hp_scr[rf, 0:p] = hr[:, 0:p]
        hp_scr[rf, 2 * p:3 * p] = hi[:, 0:p]
        hp_scr[rb, p:2 * p] = hr[:, p:2 * p]
        hp_scr[rb, 3 * p:4 * p] = hi[:, p:2 * p]
        sf = sm_scr[rf, :]
        sb = sm_scr[rb, :]
        sr = jnp.where(is_fwd, sf[:, :2 * p], sb[:, :2 * p])
        si = jnp.where(is_fwd, sf[:, 2 * p:], sb[:, 2 * p:])
        return ar * hr - ai * hi + sr, ar * hi + ai * hr + si

    hr, hi = lax.fori_loop(0, n_chunks, body, (h0[:, :2 * p], h0[:, 2 * p:]))
    hf_ref[0] = jnp.concatenate([hr, hi], axis=1)
    y = _dot(ub, t_ref[0]) + _dot(hp_scr[...].astype(BF16), cq_ref[0]) + d_ref[0] * u
    y_ref[0] = jax.nn.gelu(y)


def _s5_mats(lam_re, lam_im, log_dt, b_re, b_im, c_re, c_im, d):
    q, g, p, ch = S5_Q, S5_GROUPS, S5_STATE, S5_GROUP
    lam = lax.complex(jnp.minimum(lam_re.astype(F32), -1e-4), lam_im.astype(F32))
    ldt = lam * jnp.exp(log_dt.astype(F32))[..., None]
    lam_bar = jnp.exp(ldt)
    b_bar = ((lam_bar - 1.0) / lam)[..., None] * lax.complex(b_re.astype(F32), b_im.astype(F32))
    cc = lax.complex(c_re.astype(F32), c_im.astype(F32))
    pw = jnp.exp(ldt[..., None] * jnp.arange(q + 1, dtype=F32))
    hi = lax.Precision.HIGHEST
    kern = jnp.real(jnp.einsum('zgcp,zgpd,zgpe->zgdce', cc, pw[..., :q], b_bar, precision=hi))
    i = jnp.arange(q)[None, :]
    j = jnp.arange(q)[:, None]
    dif = i - j
    kf = jnp.where((dif >= 0)[None, :, :, None, None], kern[0][:, jnp.maximum(dif, 0)], 0.0)
    kb = jnp.where((dif <= 0)[None, :, :, None, None], kern[1][:, jnp.maximum(-dif, 0)], 0.0)
    tm = (kf + kb).transpose(0, 1, 4, 2, 3).reshape(g, q * ch, q * ch)

    bf = pw[0][..., ::-1][..., 1:][:, :, :, None] * b_bar[0][:, :, None, :]
    bb = pw[1][..., :q][:, :, :, None] * b_bar[1][:, :, None, :]
    to_rows = lambda m: m.transpose(0, 2, 3, 1).reshape(g, q * ch, p)
    bq = jnp.concatenate([to_rows(jnp.real(bf)), to_rows(jnp.real(bb)),
                          to_rows(jnp.imag(bf)), to_rows(jnp.imag(bb))], axis=-1)

    cf = cc[0].transpose(0, 2, 1)[:, :, None, :] * pw[0][..., 1:][:, :, :, None]
    cb = cc[1].transpose(0, 2, 1)[:, :, None, :] * pw[1][..., ::-1][..., :q][:, :, :, None]
    to_cols = lambda m: m.reshape(g, p, q * ch)
    cq = jnp.concatenate([to_cols(jnp.real(cf)), to_cols(jnp.real(cb)),
                          to_cols(-jnp.imag(cf)), to_cols(-jnp.imag(cb))], axis=1)

    lq = pw[..., q]
    ar = jnp.concatenate([jnp.real(lq[0]), jnp.real(lq[1])], axis=-1)[:, None, :]
    ai = jnp.concatenate([jnp.imag(lq[0]), jnp.imag(lq[1])], axis=-1)[:, None, :]
    dd = jnp.tile(d.astype(F32).reshape(g, 1, ch), (1, 1, q))
    return tm.astype(BF16), bq.astype(BF16), cq.astype(BF16), ar, ai, dd


def _s5(u, mats, h0_re, h0_im, *, n_seq, seq_len):
    q, g, p, ch = S5_Q, S5_GROUPS, S5_STATE, S5_GROUP
    n_chunks = seq_len // q
    rows = -(-n_seq // SUBLANES) * SUBLANES
    m = n_chunks * rows
    w = q * ch
    tm, bq, cq, ar, ai, dd = mats
    ug = u.reshape(n_seq, n_chunks, q, g, ch).transpose(3, 1, 0, 2, 4)
    ug = jnp.pad(ug, ((0, 0), (0, 0), (0, rows - n_seq), (0, 0), (0, 0))).reshape(g, m, w)
    h0 = jnp.concatenate([h0_re[:, 0], h0_re[:, 1], h0_im[:, 0], h0_im[:, 1]], axis=-1).transpose(1, 0, 2)
    h0 = jnp.pad(h0.astype(F32), ((0, 0), (0, rows - n_seq), (0, 0)))
    per_g = lambda shape: pl.BlockSpec((1,) + shape, lambda i: (i, 0, 0))
    y, hf = pl.pallas_call(
        functools.partial(_s5_kernel, n_chunks=n_chunks, rows=rows),
        grid=(g,),
        in_specs=[per_g((m, w)), per_g((w, w)), per_g((w, 4 * p)), per_g((4 * p, w)),
                  per_g((1, 2 * p)), per_g((1, 2 * p)), per_g((1, w)), per_g((rows, 4 * p))],
        out_specs=[per_g((m, w)), per_g((rows, 4 * p))],
        out_shape=[jax.ShapeDtypeStruct((g, m, w), F32), jax.ShapeDtypeStruct((g, rows, 4 * p), F32)],
        scratch_shapes=[pltpu.VMEM((m, 4 * p), F32), pltpu.VMEM((m, 4 * p), F32)],
        compiler_params=_cparams(("parallel",)),
        name="s5",
    )(ug, tm, bq, cq, ar, ai, dd, h0)
    y = y.reshape(g, n_chunks, rows, q, ch)[:, :, :n_seq].transpose(2, 1, 3, 0, 4).reshape(n_seq * seq_len, g * ch)
    hf = hf[:, :n_seq].reshape(g, n_seq, 4, p).transpose(1, 2, 0, 3)
    return y, hf[:, 0:2], hf[:, 2:4]


def _conv3(x, w, b):
    n = x.shape[0]
    row = lax.broadcasted_iota(jnp.int32, x.shape, 0)
    prev = jnp.where(row == 0, 0.0, pltpu.roll(x, 1, 0))
    nxt = jnp.where(row == n - 1, 0.0, pltpu.roll(x, n - 1, 0))
    return prev * w[0:1] + x * w[1:2] + nxt * w[2:3] + b


def _hy_fwd_kernel(x0_ref, x1_ref, v_ref, w0_ref, w1_ref, wv_ref, b0_ref, b1_ref, bv_ref,
                   fc_ref, fs_ref, m1_ref, m2_ref, m3_ref, p_ref, z_ref, x0c_ref, zb_scr):
    @pl.when(pl.program_id(2) == 0)
    def _():
        z = _conv3(x1_ref[...], w1_ref[...], b1_ref[...]) * _conv3(v_ref[...], wv_ref[...], bv_ref[...])
        z_ref[...] = z
        zb_scr[...] = z.astype(BF16)
        x0c_ref[...] = _conv3(x0_ref[...], w0_ref[...], b0_ref[...])

    zb = zb_scr[...]
    a = _dot(fc_ref[...], zb)
    b = _dot(fs_ref[...], zb)
    m2 = m2_ref[...]
    p_ref[0, 0] = (m1_ref[...] * a + m2 * b).astype(BF16)
    p_ref[0, 1] = (m3_ref[...] * b - m2 * a).astype(BF16)


def _hy_inv_kernel(p_ref, gc_ref, gs_ref, z_ref, x0c_ref, bias_ref, o_ref):
    conv = _dot(gc_ref[...], p_ref[0, 0]) + _dot(gs_ref[...], p_ref[0, 1])
    o_ref[...] = (x0c_ref[...] * (conv + bias_ref[...] * z_ref[...])).astype(o_ref.dtype)


def _dft_mats(seq_len):
    k = jnp.arange(seq_len, dtype=jnp.int32)
    kj = (k[:, None] * k[None, :]) % (2 * seq_len)
    ang = kj.astype(F32) * (math.pi / seq_len)
    cm = jnp.cos(ang)
    sm = -jnp.sin(ang)
    nyq = jnp.where(k % 2 == 0, 1.0, -1.0).astype(F32)
    sm = sm.at[0, :].set(nyq)
    return cm.astype(BF16), sm.astype(BF16), sm.T.astype(BF16)


def _hy_filter_mults(seq_len, f1_w, f1_b, f2_w, f2_b, f3_w, f3_b, freq, decay):
    n = seq_len
    t = (jnp.arange(n, dtype=F32) / n)[:, None]
    bands = jnp.arange(1, HY_BANDS + 1, dtype=F32)[None, :]
    z = jnp.concatenate([t, jnp.cos(2.0 * math.pi * t * bands), jnp.sin(2.0 * math.pi * t * bands)], axis=-1)
    hi = lax.Precision.HIGHEST
    fr = freq.astype(F32)
    h = jnp.sin(fr * (jnp.dot(z, f1_w.astype(F32), precision=hi) + f1_b.astype(F32)))
    h = jnp.sin(fr * (jnp.dot(h, f2_w.astype(F32), precision=hi) + f2_b.astype(F32)))
    h = jnp.dot(h, f3_w.astype(F32), precision=hi) + f3_b.astype(F32)
    h = h * jnp.exp(-t * jnp.abs(decay.astype(F32)))
    h = h.reshape(n, 2, HY_WIDTH)
    h = h / jnp.sum(jnp.abs(h), axis=(0, 1), keepdims=True)
    hc = jnp.concatenate([h[:, 0], h[::-1, 1]], axis=0)
    spec = jnp.fft.rfft(hc, axis=0)
    hr, him = jnp.real(spec), jnp.imag(spec)
    wk = jnp.where(jnp.arange(n) == 0, 1.0, 2.0)[:, None] / (2.0 * n)
    first = (jnp.arange(n) == 0)[:, None]
    m1 = hr[:n] * wk
    m2 = jnp.where(first, 0.0, -him[:n]) * wk
    m3 = jnp.where(first, hr[n:n + 1], hr[:n]) * wk
    return m1.astype(F32), m2.astype(F32), m3.astype(F32)


def _hyena(proj, conv_w, conv_b, bias, dft, mults, *, row0, n_seq, seq_len, cb, tk):
    blk0 = row0 // seq_len
    nc = HY_WIDTH // cb
    nk = seq_len // tk
    c0 = HY_COL // cb
    cm, sm, smt = dft
    m1, m2, m3 = mults
    xcol = lambda part: pl.BlockSpec((seq_len, cb), lambda b, c, k: (blk0 + b, c0 + part * nc + c))
    wcol = lambda part: pl.BlockSpec((3, cb), lambda b, c, k: (0, part * nc + c))
    bcol = lambda part: pl.BlockSpec((1, cb), lambda b, c, k: (0, part * nc + c))
    frow = pl.BlockSpec((tk, seq_len), lambda b, c, k: (k, 0))
    mblk = pl.BlockSpec((tk, cb), lambda b, c, k: (k, c))
    cb2 = conv_b.reshape(1, 3 * HY_WIDTH)
    pspec, z, x0c = pl.pallas_call(
        _hy_fwd_kernel,
        grid=(n_seq, nc, nk),
        in_specs=[xcol(0), xcol(1), xcol(2), wcol(0), wcol(1), wcol(2), bcol(0), bcol(1), bcol(2),
                  frow, frow, mblk, mblk, mblk],
        out_specs=[
            pl.BlockSpec((1, 2, tk, cb), lambda b, c, k: (b, 0, k, c)),
            pl.BlockSpec((seq_len, cb), lambda b, c, k: (b, c)),
            pl.BlockSpec((seq_len, cb), lambda b, c, k: (b, c)),
        ],
        out_shape=[
            jax.ShapeDtypeStruct((n_seq, 2, seq_len, HY_WIDTH), BF16),
            jax.ShapeDtypeStruct((n_seq * seq_len, HY_WIDTH), F32),
            jax.ShapeDtypeStruct((n_seq * seq_len, HY_WIDTH), F32),
        ],
        scratch_shapes=[pltpu.VMEM((seq_len, cb), BF16)],
        compiler_params=_cparams(("parallel", "parallel", "arbitrary")),
        name="hyena_fwd",
    )(proj, proj, proj, conv_w, conv_w, conv_w, cb2, cb2, cb2, cm, sm, m1, m2, m3)
    grow = pl.BlockSpec((tk, seq_len), lambda b, c, k: (k, 0))
    tile = pl.BlockSpec((tk, cb), lambda b, c, k: (b * nk + k, c))
    return pl.pallas_call(
        _hy_inv_kernel,
        grid=(n_seq, nc, nk),
        in_specs=[pl.BlockSpec((1, 2, seq_len, cb), lambda b, c, k: (b, 0, 0, c)),
                  grow, grow, tile, tile, pl.BlockSpec((1, cb), lambda b, c, k: (0, c))],
        out_specs=tile,
        out_shape=jax.ShapeDtypeStruct((n_seq * seq_len, HY_WIDTH), BF16),
        compiler_params=_cparams(("parallel", "parallel", "arbitrary")),
        name="hyena_inv",
    )(pspec, cm, smt, z, x0c, bias.reshape(1, HY_WIDTH))


def _out_kernel(x_ref, ret_ref, s5_ref, hy_ref, mod_ref, g_ref, gw_ref, gb_ref,
                wr_ref, ws_ref, wh_ref, rt_ref, xo_ref, h_ref, lg_ref):
    m = mod_ref[0]
    y = s5_ref[...]
    s5o = y * jax.nn.sigmoid(_dot(y.astype(BF16), gw_ref[...]) + gb_ref[...])
    mix = (_dot(ret_ref[...], wr_ref[...]) + _dot(s5o.astype(BF16), ws_ref[...])
           + _dot(hy_ref[...], wh_ref[...]))
    x = x_ref[...] + m[2:3] * mix
    xo_ref[...] = x
    h = _rms(x, g_ref[...]) * (1.0 + m[4:5]) + m[3:4]
    hb = h.astype(BF16)
    h_ref[...] = hb
    lg_ref[...] = lax.dot_general(rt_ref[...], hb, (((1,), (1,)), ((), ())), preferred_element_type=F32)


def _out_proj(x, ret_o, s5_y, hy_o, mod, g, glu_w, glu_b, w_out, router, n_ctx_rows, dec_seq):
    t = x.shape[0]
    tm = ROW_TILE
    grp = functools.partial(_group_of, tile=tm, n_ctx_rows=n_ctx_rows, dec_seq=dec_seq)
    row = lambda w: pl.BlockSpec((tm, w), lambda i: (i, 0))
    full = lambda a, b: pl.BlockSpec((a, b), lambda i: (0, 0))
    wo = w_out.astype(BF16)
    return pl.pallas_call(
        _out_kernel,
        grid=(t // tm,),
        in_specs=[row(D_MODEL), row(RET_WIDTH), row(S5_WIDTH), row(HY_WIDTH),
                  pl.BlockSpec((1, 6, D_MODEL), lambda i: (grp(i), 0, 0)),
                  full(1, D_MODEL), full(S5_WIDTH, S5_WIDTH), full(1, S5_WIDTH),
                  full(RET_WIDTH, D_MODEL), full(S5_WIDTH, D_MODEL), full(HY_WIDTH, D_MODEL),
                  full(N_EXPERTS, D_MODEL)],
        out_specs=[row(D_MODEL), row(D_MODEL), pl.BlockSpec((N_EXPERTS, tm), lambda i: (0, i))],
        out_shape=[jax.ShapeDtypeStruct((t, D_MODEL), F32), jax.ShapeDtypeStruct((t, D_MODEL), BF16),
                   jax.ShapeDtypeStruct((N_EXPERTS, t), F32)],
        compiler_params=_cparams(("parallel",)),
        name="out_proj",
    )(x, ret_o, s5_y, hy_o, mod, g.reshape(1, D_MODEL), glu_w.astype(BF16), glu_b.reshape(1, S5_WIDTH),
      wo[:RET_WIDTH], wo[RET_WIDTH:RET_WIDTH + S5_WIDTH], wo[RET_WIDTH + S5_WIDTH:], router.T.astype(BF16))


def _moe_kernel(be_ref, first_ref, slot_ref, nxt_ref, nu_ref, xs_ref, wg_hbm, wu_hbm, wd_hbm, o_ref,
                wg_f, wu_f, wd_f, wg_b, wu_b, wd_b, sem):
    i = pl.program_id(0)

    def copies(e, s):
        return (pltpu.make_async_copy(wg_hbm.at[e], wg_f.at[s], sem.at[s, 0]),
                pltpu.make_async_copy(wu_hbm.at[e], wu_f.at[s], sem.at[s, 1]),
                pltpu.make_async_copy(wd_hbm.at[e], wd_f.at[s], sem.at[s, 2]))

    @pl.when(i == 0)
    def _():
        for cp in copies(be_ref[0], 0):
            cp.start()

    @pl.when(first_ref[i] == 1)
    def _():
        s = slot_ref[i]
        for cp in copies(be_ref[i], s):
            cp.wait()

        @pl.when(nxt_ref[i] >= 0)
        def _():
            for cp in copies(nxt_ref[i], 1 - s):
                cp.start()

        wg_b[...] = wg_f[s].astype(BF16)
        wu_b[...] = wu_f[s].astype(BF16)
        wd_b[...] = wd_f[s].astype(BF16)

    @pl.when(i < nu_ref[0])
    def _():
        xb = xs_ref[...]
        gate = _dot(xb, wg_b[...])
        hb = gate * jax.nn.sigmoid(gate) * _dot(xb, wu_b[...])
        o_ref[...] = _dot(hb.astype(BF16), wd_b[...]).astype(o_ref.dtype)

    @pl.when(i >= nu_ref[0])
    def _():
        o_ref[...] = jnp.zeros_like(o_ref)


def _moe_grouped(xs, blk_e, first, slot, nxt, n_used, w_gate, w_up, w_down):
    pr = xs.shape[0]
    bm = MOE_BM
    nb = pr // bm
    grid_spec = pltpu.PrefetchScalarGridSpec(
        num_scalar_prefetch=5,
        grid=(nb,),
        in_specs=[
            pl.BlockSpec((bm, D_MODEL), lambda i, *_: (i, 0)),
            pl.BlockSpec(memory_space=pl.ANY),
            pl.BlockSpec(memory_space=pl.ANY),
            pl.BlockSpec(memory_space=pl.ANY),
        ],
        out_specs=pl.BlockSpec((bm, D_MODEL), lambda i, *_: (i, 0)),
        scratch_shapes=[pltpu.VMEM((2, D_MODEL, D_EXPERT), F32), pltpu.VMEM((2, D_MODEL, D_EXPERT), F32),
                        pltpu.VMEM((2, D_EXPERT, D_MODEL), F32),
                        pltpu.VMEM((D_MODEL, D_EXPERT), BF16), pltpu.VMEM((D_MODEL, D_EXPERT), BF16),
                        pltpu.VMEM((D_EXPERT, D_MODEL), BF16),
                        pltpu.SemaphoreType.DMA((2, 3))],
    )
    return pl.pallas_call(
        _moe_kernel,
        grid_spec=grid_spec,
        out_shape=jax.ShapeDtypeStruct((pr, D_MODEL), BF16),
        compiler_params=_cparams(("arbitrary",)),
        name="moe_grouped",
    )(blk_e, first, slot, nxt, n_used, xs, w_gate, w_up, w_down)


ROUTE_TILE = 512


def _router_kernel(lg_ref, bias_ref, pos_ref, gate_ref, be_ref, info_ref, rank_scr, ek_scr, *, t, nbp):
    tl = ROUTE_TILE
    ne = N_EXPERTS
    bm = MOE_BM
    row = lax.broadcasted_iota(jnp.int32, (ne, tl), 0)
    tri = (lax.broadcasted_iota(jnp.int32, (tl, tl), 0) < lax.broadcasted_iota(jnp.int32, (tl, tl), 1)).astype(BF16)
    bias = bias_ref[...]

    def select(i, counts):
        cols = pl.ds(pl.multiple_of(i * tl, tl), tl)
        s = jax.nn.sigmoid(lg_ref[:, cols])
        sel = s + bias
        mask = jnp.zeros((ne, tl), F32)
        vals = []
        for k in range(TOP_K):
            best = jnp.max(sel, axis=0, keepdims=True)
            idx = jnp.min(jnp.where(sel == best, row, ne), axis=0, keepdims=True)
            hit = row == idx
            vals.append(jnp.sum(jnp.where(hit, s, 0.0), axis=0, keepdims=True))
            sel = jnp.where(hit, -jnp.inf, sel)
            mask = jnp.where(hit, 1.0, mask)
            ek_scr[k:k + 1, cols] = idx
        total = vals[0]
        for v in vals[1:]:
            total = total + v
        scale = ROUTED_SCALE / total
        for k in range(TOP_K):
            gate_ref[k:k + 1, cols] = vals[k] * scale
        gate_ref[TOP_K:SUBLANES, cols] = jnp.zeros((SUBLANES - TOP_K, tl), F32)
        rank_scr[:, cols] = _dot(mask.astype(BF16), tri) + counts
        return counts + jnp.sum(mask, axis=1, keepdims=True)

    counts = lax.fori_loop(0, t // tl, select, jnp.zeros((ne, 1), F32))
    counts = counts.astype(jnp.int32)
    shift = bm.bit_length() - 1
    assert bm == 1 << shift
    padded = ((counts + (bm - 1)) >> shift) << shift
    e0 = lax.broadcasted_iota(jnp.int32, (ne, ne), 0)
    e1 = lax.broadcasted_iota(jnp.int32, (ne, ne), 1)
    padded_row = jnp.sum(jnp.where(e0 == e1, padded, 0), axis=0, keepdims=True)
    counts_row = jnp.sum(jnp.where(e0 == e1, counts, 0), axis=0, keepdims=True)
    pstart = jnp.sum(jnp.where(e1 < e0, padded_row, 0), axis=1, keepdims=True)
    ustart = jnp.sum(jnp.where(e1 < e0, counts_row, 0), axis=1, keepdims=True)
    pend = pstart + padded
    lane = lax.broadcasted_iota(jnp.int32, (ne, LANES), 1)
    info_ref[...] = jnp.where(lane == 0, counts, jnp.where(lane == 1, pstart, jnp.where(lane == 2, ustart, pend)))
    blk = lax.broadcasted_iota(jnp.int32, (ne, nbp), 1) * bm
    owner = jnp.sum(jnp.where(pend <= blk, 1, 0), axis=0, keepdims=True)
    be_ref[...] = jnp.minimum(owner, ne - 1)
    pstart_f = pstart.astype(F32)

    def place(i, carry):
        cols = pl.ds(pl.multiple_of(i * tl, tl), tl)
        dest = rank_scr[:, cols] + pstart_f
        for k in range(TOP_K):
            hit = row == ek_scr[k:k + 1, cols]
            pos_ref[k:k + 1, cols] = jnp.sum(jnp.where(hit, dest, 0.0), axis=0, keepdims=True).astype(jnp.int32)
        pos_ref[TOP_K:SUBLANES, cols] = jnp.zeros((SUBLANES - TOP_K, tl), jnp.int32)
        return carry

    lax.fori_loop(0, t // tl, place, 0)


def _router(logits_t, router_bias, nb):
    t = logits_t.shape[1]
    nbp = -(-nb // LANES) * LANES
    return pl.pallas_call(
        functools.partial(_router_kernel, t=t, nbp=nbp),
        out_shape=[jax.ShapeDtypeStruct((SUBLANES, t), jnp.int32), jax.ShapeDtypeStruct((SUBLANES, t), F32),
                   jax.ShapeDtypeStruct((1, nbp), jnp.int32), jax.ShapeDtypeStruct((N_EXPERTS, LANES), jnp.int32)],
        scratch_shapes=[pltpu.VMEM((N_EXPERTS, t), F32), pltpu.VMEM((SUBLANES, t), jnp.int32)],
        compiler_params=pltpu.CompilerParams(vmem_limit_bytes=VMEM_LIMIT),
        name="router",
    )(logits_t, router_bias.astype(F32).reshape(N_EXPERTS, 1))


def _dispatch_plan(pos, blk_e_row, info, nb):
    t = pos.shape[1]
    bm = MOE_BM
    counts, pstart, ustart, pend = info[:, 0], info[:, 1], info[:, 2], info[:, 3]
    blk_e = blk_e_row[0, :nb]
    n_used = pend[-1] // bm
    keys = pos[:TOP_K].reshape(-1)
    toks = jnp.tile(jnp.arange(t, dtype=jnp.int32), TOP_K)
    _, sorted_tok = lax.sort_key_val(keys, toks)
    r = jnp.arange(nb * bm, dtype=jnp.int32).reshape(nb, bm) - pstart[blk_e][:, None]
    src = jnp.clip(ustart[blk_e][:, None] + r, 0, t * TOP_K - 1)
    slot_tok = jnp.where(r < counts[blk_e][:, None], sorted_tok[src], 0).reshape(-1)
    blk = jnp.arange(nb, dtype=jnp.int32)
    prev_e = jnp.concatenate([jnp.full((1,), -1, jnp.int32), blk_e[:-1]])
    first = jnp.logical_and(blk < n_used, blk_e != prev_e)
    slot = (jnp.cumsum(first.astype(jnp.int32)) - 1) % 2
    first_at = jnp.where(first, blk, nb)
    nxt_first = lax.cummin(jnp.concatenate([first_at[1:], jnp.full((1,), nb, jnp.int32)]), reverse=True)
    nxt = jnp.where(nxt_first < nb, blk_e[jnp.minimum(nxt_first, nb - 1)], -1)
    return (slot_tok, blk_e, first.astype(jnp.int32), slot.astype(jnp.int32), nxt.astype(jnp.int32),
            n_used.astype(jnp.int32).reshape(1))


def _shared_kernel(x_ref, h_ref, gt_ref, *rest, final):
    e_refs = rest[:TOP_K]
    mod_ref, sg_ref, su_ref, sd_ref, fn_ref, o_ref = rest[TOP_K:]
    m = mod_ref[0]
    hb = h_ref[...]
    gate = _dot(hb, sg_ref[...])
    act = gate * jax.nn.sigmoid(gate) * _dot(hb, su_ref[...])
    shared = _dot(act.astype(BF16), sd_ref[...])
    gt = gt_ref[...]
    routed = gt[:, 0:1] * e_refs[0][0].astype(F32)
    for k in range(1, TOP_K):
        routed = routed + gt[:, k:k + 1] * e_refs[k][0].astype(F32)
    x = x_ref[...] + m[5:6] * (routed + shared)
    if final:
        x = _rms(x, fn_ref[...])
    o_ref[...] = x


def _shared(x, h, gates_t, picked, mod, sg, su, sd, final_norm, n_ctx_rows, dec_seq, *, final, row0=0, rows=None):
    rows = x.shape[0] if rows is None else rows
    tm = ROW_TILE // 2
    b0 = row0 // tm
    grp = lambda i: _group_of(i + b0, tm, n_ctx_rows, dec_seq)
    row = pl.BlockSpec((tm, D_MODEL), lambda i: (i + b0, 0))
    full = lambda a, b: pl.BlockSpec((a, b), lambda i: (0, 0))
    pick = lambda k: pl.BlockSpec((1, tm, D_MODEL), lambda i: (k, i + b0, 0))
    return pl.pallas_call(
        functools.partial(_shared_kernel, final=final),
        grid=(rows // tm,),
        in_specs=[row, row, pl.BlockSpec((tm, SUBLANES), lambda i: (i + b0, 0))]
                 + [pick(k) for k in range(TOP_K)]
                 + [pl.BlockSpec((1, 6, D_MODEL), lambda i: (grp(i), 0, 0)),
                    full(D_MODEL, D_SHARED), full(D_MODEL, D_SHARED), full(D_SHARED, D_MODEL), full(1, D_MODEL)],
        out_specs=pl.BlockSpec((tm, D_MODEL), lambda i: (i, 0)),
        out_shape=jax.ShapeDtypeStruct((rows, D_MODEL), F32),
        compiler_params=_cparams(("parallel",)),
        name="shared_final" if final else "shared",
    )(x, h, gates_t, *([picked] * TOP_K), mod, sg.astype(BF16), su.astype(BF16), sd.astype(BF16),
      final_norm.reshape(1, D_MODEL))


def kernel(x_prompt, x_sample, state_ret, state_s5_re, state_s5_im, c, c_ctx, w_ada, b_ada, norm_mix, norm_ffn, w_in, w_out, ret_decay, s5_lam_re, s5_lam_im, s5_log_dt, s5_b_re, s5_b_im, s5_c_re, s5_c_im, s5_d, s5_glu_w, s5_glu_b, hy_conv_w, hy_conv_b, hy_f1_w, hy_f1_b, hy_f2_w, hy_f2_b, hy_f3_w, hy_f3_b, hy_freq, hy_decay, hy_bias, moe_router, moe_router_bias, moe_w_gate, moe_w_up, moe_w_down, sh_w_gate, sh_w_up, sh_w_down, final_norm):
    n_ctx, seq, d = x_prompt.shape
    n_dec, dec_seq, _ = x_sample.shape
    n_ctx_rows = n_ctx * seq
    t = n_ctx_rows + n_dec * dec_seq

    x = jnp.concatenate([x_prompt.reshape(n_ctx_rows, d), x_sample.reshape(n_dec * dec_seq, d)], axis=0)
    cond = jnp.concatenate([c_ctx[None, :], c], axis=0)
    cond8 = jnp.pad(cond, ((0, SUBLANES - cond.shape[0]), (0, 0)))
    mods = _ada(cond8, w_ada, b_ada)[:, :1 + n_dec].reshape(DEPTH, 1 + n_dec, 6, d)

    cos2, sin2 = _rope_tables(dec_seq)
    no_rope = jnp.zeros((seq, LANES), F32)
    zero_ret = jnp.zeros((n_ctx, 2, RET_HEADS, RET_DK, RET_DV), F32)
    zero_s5 = jnp.zeros((n_ctx, 2, S5_GROUPS, S5_STATE), F32)
    dft_ctx = _dft_mats(seq)
    dft_dec = _dft_mats(dec_seq)

    ret_list, s5r_list, s5i_list = [], [], []
    for l in range(DEPTH):
        mod = mods[l]
        proj = _in_proj(x, mod, norm_mix[l], w_in[l].astype(BF16), n_ctx_rows, dec_seq)

        log_gamma = jax.nn.log_sigmoid(ret_decay[l].astype(F32))
        ret_c, ret_s = _retention(proj, log_gamma, zero_ret, no_rope, no_rope,
                                  row0=0, n_seq=n_ctx, seq_len=seq, hb=RET_HEADS, rope=False)
        ret_d, _ = _retention(proj, log_gamma, state_ret[:, l].astype(F32), cos2, sin2,
                              row0=n_ctx_rows, n_seq=n_dec, seq_len=dec_seq, hb=1, rope=True)
        ret_list.append(ret_s)

        mats = _s5_mats(s5_lam_re[l], s5_lam_im[l], s5_log_dt[l], s5_b_re[l], s5_b_im[l],
                        s5_c_re[l], s5_c_im[l], s5_d[l])
        u = proj[:, U_COL:U_COL + S5_WIDTH]
        s5_c, s5_re, s5_im = _s5(u[:n_ctx_rows], mats, zero_s5, zero_s5, n_seq=n_ctx, seq_len=seq)
        s5_d_, _, _ = _s5(u[n_ctx_rows:], mats, state_s5_re[:, l], state_s5_im[:, l], n_seq=n_dec, seq_len=dec_seq)
        s5r_list.append(s5_re)
        s5i_list.append(s5_im)

        filt = (hy_f1_w[l], hy_f1_b[l], hy_f2_w[l], hy_f2_b[l], hy_f3_w[l], hy_f3_b[l], hy_freq[l], hy_decay[l])
        hy_c = _hyena(proj, hy_conv_w[l], hy_conv_b[l], hy_bias[l], dft_ctx, _hy_filter_mults(seq, *filt),
                      row0=0, n_seq=n_ctx, seq_len=seq, cb=HY_WIDTH, tk=seq)
        hy_d = _hyena(proj, hy_conv_w[l], hy_conv_b[l], hy_bias[l], dft_dec, _hy_filter_mults(dec_seq, *filt),
                      row0=n_ctx_rows, n_seq=n_dec, seq_len=dec_seq, cb=HY_WIDTH // 2, tk=512)

        ret_o = jnp.concatenate([ret_c, ret_d], axis=0)
        s5_y = jnp.concatenate([s5_c, s5_d_], axis=0)
        hy_o = jnp.concatenate([hy_c, hy_d], axis=0)
        x, h2, logits = _out_proj(x, ret_o, s5_y, hy_o, mod, norm_ffn[l], s5_glu_w[l], s5_glu_b[l],
                                  w_out[l], moe_router[l], n_ctx_rows, dec_seq)

        nb = -(-(t * TOP_K) // MOE_BM) + N_EXPERTS
        pos, gates, blk_e_row, info = _router(logits, moe_router_bias[l], nb)
        slot_tok, blk_e, first, slot, nxt, n_used = _dispatch_plan(pos, blk_e_row, info, nb)
        xs = jnp.take(h2, slot_tok, axis=0)
        eo = _moe_grouped(xs, blk_e, first, slot, nxt, n_used, moe_w_gate[l], moe_w_up[l], moe_w_down[l])
        picked = jnp.take(eo, pos[:TOP_K].reshape(-1), axis=0).reshape(TOP_K, t, d)
        gates_t = gates.T

        sh = (sh_w_gate[l], sh_w_up[l], sh_w_down[l])
        if l < DEPTH - 1:
            x = _shared(x, h2, gates_t, picked, mod, *sh, final_norm, n_ctx_rows, dec_seq, final=False)
        else:
            y_c = _shared(x, h2, gates_t, picked, mod, *sh, final_norm, n_ctx_rows, dec_seq, final=True,
                          row0=0, rows=n_ctx_rows)
            y_d = _shared(x, h2, gates_t, picked, mod, *sh, final_norm, n_ctx_rows, dec_seq, final=True,
                          row0=n_ctx_rows, rows=n_dec * dec_seq)

    return (y_c.reshape(n_ctx, seq, d), y_d.reshape(n_dec, dec_seq, d),
            jnp.stack(ret_list, axis=1), jnp.stack(s5r_list, axis=1), jnp.stack(s5i_list, axis=1))
```

```python
import functools
import math

import jax
import jax.numpy as jnp
from jax import lax
from jax.experimental import pallas as pl
from jax.experimental.pallas import tpu as pltpu

F32 = jnp.float32
BF16 = jnp.bfloat16

D_MODEL = 2048
DEPTH = 2
GRID_W = 64
RET_HEADS = 8
RET_DK = 128
RET_DV = 128
RET_WIDTH = RET_HEADS * RET_DV
RET_CHUNK = 128
ROPE_BASE = 10000.0
S5_WIDTH = 512
S5_GROUP = 16
S5_GROUPS = S5_WIDTH // S5_GROUP
S5_STATE = 64
S5_Q = 16
HY_WIDTH = 512
HY_BANDS = 16
IN_WIDTH = 4 * RET_WIDTH + S5_WIDTH + 3 * HY_WIDTH
U_COL = 4 * RET_WIDTH
HY_COL = U_COL + S5_WIDTH
N_EXPERTS = 64
TOP_K = 6
D_EXPERT = 512
D_SHARED = 512
ROUTED_SCALE = 2.5
EPS = 1e-6

LANES = 128
SUBLANES = 8
VMEM_LIMIT = 56 * 1024 * 1024

ROW_TILE = 512
MOE_BM = 256


def _cparams(sem):
    return pltpu.CompilerParams(dimension_semantics=sem, vmem_limit_bytes=VMEM_LIMIT)


def _dot(a, b):
    return jnp.dot(a, b, preferred_element_type=F32)


def _rms(x, g):
    var = jnp.mean(x * x, axis=-1, keepdims=True)
    return x * lax.rsqrt(var + EPS) * g


def _pack_halves(xb):
    n = xb.shape[1] // 2
    lo = lax.bitcast_convert_type(xb[:, :n].astype(F32), jnp.uint32) >> 16
    hi = lax.bitcast_convert_type(xb[:, n:].astype(F32), jnp.uint32)
    return lo | hi


def _unpack_halves(w):
    lo = lax.bitcast_convert_type(w << 16, F32)
    hi = lax.bitcast_convert_type(w & jnp.uint32(0xFFFF0000), F32)
    return lo, hi


def _ada_kernel(c_ref, w_ref, b_ref, o_ref):
    c = c_ref[...]
    s = (c * jax.nn.sigmoid(c)).astype(BF16)
    o_ref[0] = _dot(s, w_ref[0].astype(BF16)) + b_ref[0]


def _ada(cond8, w_ada, b_ada):
    tn = 512
    n = w_ada.shape[-1]
    return pl.pallas_call(
        _ada_kernel,
        grid=(DEPTH, n // tn),
        in_specs=[
            pl.BlockSpec((SUBLANES, D_MODEL), lambda l, j: (0, 0)),
            pl.BlockSpec((1, D_MODEL, tn), lambda l, j: (l, 0, j)),
            pl.BlockSpec((1, 1, tn), lambda l, j: (l, 0, j)),
        ],
        out_specs=pl.BlockSpec((1, SUBLANES, tn), lambda l, j: (l, 0, j)),
        out_shape=jax.ShapeDtypeStruct((DEPTH, SUBLANES, n), F32),
        compiler_params=_cparams(("parallel", "parallel")),
        name="ada",
    )(cond8, w_ada, b_ada.reshape(DEPTH, 1, n))


def _group_of(i, tile, n_ctx_rows, dec_seq):
    ctx_tiles = n_ctx_rows // tile
    per = dec_seq // tile
    return jnp.where(i < ctx_tiles, 0, 1 + (i - ctx_tiles) // per)


def _in_kernel(x_ref, mod_ref, g_ref, w_ref, o_ref, h_scr):
    @pl.when(pl.program_id(1) == 0)
    def _():
        m = mod_ref[0]
        h = _rms(x_ref[...], g_ref[...]) * (1.0 + m[1:2]) + m[0:1]
        h_scr[...] = h.astype(BF16)

    o_ref[...] = _dot(h_scr[...], w_ref[...])


def _in_proj(x, mod, g, w_bf, n_ctx_rows, dec_seq):
    t = x.shape[0]
    tm, tn = 1024, 512
    grp = functools.partial(_group_of, tile=tm, n_ctx_rows=n_ctx_rows, dec_seq=dec_seq)
    return pl.pallas_call(
        _in_kernel,
        grid=(t // tm, IN_WIDTH // tn),
        in_specs=[
            pl.BlockSpec((tm, D_MODEL), lambda i, j: (i, 0)),
            pl.BlockSpec((1, 6, D_MODEL), lambda i, j: (grp(i), 0, 0)),
            pl.BlockSpec((1, D_MODEL), lambda i, j: (0, 0)),
            pl.BlockSpec((D_MODEL, tn), lambda i, j: (0, j)),
        ],
        out_specs=pl.BlockSpec((tm, tn), lambda i, j: (i, j)),
        out_shape=jax.ShapeDtypeStruct((t, IN_WIDTH), F32),
        scratch_shapes=[pltpu.VMEM((tm, D_MODEL), BF16)],
        compiler_params=_cparams(("parallel", "arbitrary")),
        name="in_proj",
    )(x, mod, g.reshape(1, D_MODEL), w_bf)


def _ret_kernel(lg_ref, q_ref, k_ref, v_ref, gt_ref, cos_ref, sin_ref, s0_ref,
                o_ref, sfin_ref, acc_scr, q_scr, k_scr, *, seq_len, hb, rope):
    c = RET_CHUNK
    n_chunks = seq_len // c
    ii = lax.broadcasted_iota(jnp.int32, (c, c), 0)
    jj = lax.broadcasted_iota(jnp.int32, (c, c), 1)
    rel = (ii - jj).astype(F32)
    ci = lax.broadcasted_iota(jnp.int32, (c, 1), 0).astype(F32)
    one = jnp.ones((1, 1), F32)
    tdot = functools.partial(lax.dot_general, preferred_element_type=F32)

    for hh in range(hb):
        head = pl.program_id(1) * hb + hh
        lgf = lg_ref[0, head]
        lgb = lg_ref[1, head]
        dmask = (jnp.where(rel >= 0, jnp.exp(lgf * jnp.maximum(rel, 0.0)), 0.0)
                 + jnp.where(rel <= 0, jnp.exp(lgb * jnp.maximum(-rel, 0.0)), 0.0))
        qd_f = jnp.exp(lgf * (ci + 1.0))
        kd_f = jnp.exp(lgf * (c - 1.0 - ci))
        cd_f = jnp.exp(lgf * c * one)
        qd_b = jnp.exp(lgb * (c - ci))
        kd_b = jnp.exp(lgb * ci)
        cd_b = jnp.exp(lgb * c * one)
        lanes = slice(hh * LANES, (hh + 1) * LANES)

        def rows_of(n):
            if isinstance(n, int):
                return slice(n * c, (n + 1) * c)
            return pl.ds(pl.multiple_of(n * c, c), c)

        def fwd_chunk(n, s_f):
            rows = rows_of(n)
            q = q_ref[rows, lanes]
            k = k_ref[rows, lanes] * (RET_DK ** -0.5)
            if rope:
                cs = cos_ref[rows, :]
                sn = sin_ref[rows, :]
                q = q * cs + pltpu.roll(q, RET_DK // 2, 1) * sn
                k = k * cs + pltpu.roll(k, RET_DK // 2, 1) * sn
            qb = q.astype(BF16)
            vb = v_ref[rows, lanes].astype(BF16)
            q_scr[rows, :] = qb
            k_scr[rows, :] = k
            scores = tdot(qb, k.astype(BF16), (((1,), (1,)), ((), ()))) * dmask
            inner = _dot(scores.astype(BF16), vb)
            cross = _dot(qb, s_f.astype(BF16)) * qd_f
            acc_scr[rows, :] = inner + cross
            upd = tdot((k * kd_f).astype(BF16), vb, (((0,), (0,)), ((), ())))
            return s_f * cd_f + upd

        def bwd_chunk(m, s_b):
            n = n_chunks - 1 - m
            rows = rows_of(n)
            qb = q_scr[rows, :]
            k = k_scr[rows, :]
            vb = v_ref[rows, lanes].astype(BF16)
            o = acc_scr[rows, :] + _dot(qb, s_b.astype(BF16)) * qd_b
            mu = jnp.mean(o, axis=-1, keepdims=True)
            oc = o - mu
            var = jnp.mean(oc * oc, axis=-1, keepdims=True)
            o = oc * lax.rsqrt(var + EPS)
            g = gt_ref[rows, lanes]
            o_ref[rows, lanes] = (g * jax.nn.sigmoid(g) * o).astype(o_ref.dtype)
            upd = tdot((k * kd_b).astype(BF16), vb, (((0,), (0,)), ((), ())))
            return s_b * cd_b + upd

        s_f = s0_ref[0, 0, hh]
        s_b = s0_ref[0, 1, hh]
        if n_chunks <= 4:
            for n in range(n_chunks):
                s_f = fwd_chunk(n, s_f)
            for m in range(n_chunks):
                s_b = bwd_chunk(m, s_b)
        else:
            s_f = lax.fori_loop(0, n_chunks, fwd_chunk, s_f)
            s_b = lax.fori_loop(0, n_chunks, bwd_chunk, s_b)
        sfin_ref[0, 0, hh] = s_f
        sfin_ref[0, 1, hh] = s_b


def _into(kernel_fn, n_in, dst):
    if dst is None:
        return kernel_fn, [], [], {}

    def body(*refs):
        return kernel_fn(*refs[:n_in], *refs[n_in + 1:])

    return body, [pl.BlockSpec(memory_space=pl.ANY)], [dst], {n_in: 0}


def _retention(proj, log_gamma, s0, cos2, sin2, *, row0, n_seq, seq_len, hb, rope, dst=None):
    blk0 = row0 // seq_len
    body, dst_spec, dst_arg, alias = _into(
        functools.partial(_ret_kernel, seq_len=seq_len, hb=hb, rope=rope), 8, dst)
    w = hb * LANES
    hblocks = RET_HEADS // hb
    col = lambda part: (lambda b, h, lg: (blk0 + b, part * hblocks + h))
    grid_spec = pltpu.PrefetchScalarGridSpec(
        num_scalar_prefetch=1,
        grid=(n_seq, hblocks),
        in_specs=[
            pl.BlockSpec((seq_len, w), col(0)),
            pl.BlockSpec((seq_len, w), col(1)),
            pl.BlockSpec((seq_len, w), col(2)),
            pl.BlockSpec((seq_len, w), col(3)),
            pl.BlockSpec((seq_len, LANES), lambda b, h, lg: (0, 0)),
            pl.BlockSpec((seq_len, LANES), lambda b, h, lg: (0, 0)),
            pl.BlockSpec((1, 2, hb, RET_DK, RET_DV), lambda b, h, lg: (b, 0, h, 0, 0)),
        ] + dst_spec,
        out_specs=[
            pl.BlockSpec((seq_len, w), lambda b, h, lg: (blk0 + b, h)),
            pl.BlockSpec((1, 2, hb, RET_DK, RET_DV), lambda b, h, lg: (b, 0, h, 0, 0)),
        ],
        scratch_shapes=[
            pltpu.VMEM((seq_len, LANES), F32),
            pltpu.VMEM((seq_len, LANES), BF16),
            pltpu.VMEM((seq_len, LANES), F32),
        ],
    )
    return pl.pallas_call(
        body,
        grid_spec=grid_spec,
        out_shape=[
            jax.ShapeDtypeStruct((proj.shape[0], RET_WIDTH), BF16),
            jax.ShapeDtypeStruct((n_seq, 2, RET_HEADS, RET_DK, RET_DV), F32),
        ],
        input_output_aliases=alias,
        compiler_params=_cparams(("parallel", "arbitrary")),
        name="retention",
    )(log_gamma, proj, proj, proj, proj, cos2, sin2, s0, *dst_arg)


def _rope_tables(seq_len):
    rows_n = seq_len // GRID_W
    rows = jnp.repeat(jnp.arange(rows_n, dtype=F32), GRID_W)
    cols = jnp.tile(jnp.arange(GRID_W, dtype=F32), rows_n)
    nf = RET_DK // 4
    inv = ROPE_BASE ** (-jnp.arange(nf, dtype=F32) / nf)
    ang = jnp.concatenate([rows[:, None] * inv, cols[:, None] * inv], axis=-1)
    cs, sn = jnp.cos(ang), jnp.sin(ang)
    return jnp.concatenate([cs, cs], axis=-1), jnp.concatenate([-sn, sn], axis=-1)


S5_GB = LANES // S5_GROUP
S5_W = S5_Q * LANES
S5_SPLIT = 4
S5_SW = S5_GB * S5_STATE
S5_SB = S5_SW // LANES


def _s5_kernel(u_ref, bq_ref, t_ref, cq_ref, ar_ref, ai_ref, d_ref, h0_ref,
               y_ref, hf_ref, ub_scr, sm_scr, hp_scr, hpb_scr, *, n_seq, n_chunks):
    s = pl.program_id(1)
    m = n_seq * n_chunks
    q = S5_Q
    nblk = S5_W // LANES
    sb = S5_SB

    per = q // S5_SPLIT

    @pl.when(s == 0)
    def _():
        for j in range(q):
            ub_scr[j // per, :, (j % per) * LANES:(j % per + 1) * LANES] = (
                u_ref[pl.ds(j, m, stride=q), :].astype(BF16))

    @pl.when(s < S5_SPLIT)
    def _():
        part = _dot(ub_scr[jnp.minimum(s, S5_SPLIT - 1)], bq_ref[0])

        @pl.when(s == 0)
        def _():
            for cb in range(nblk):
                sm_scr[cb] = part[:, cb * LANES:(cb + 1) * LANES]

        @pl.when(s > 0)
        def _():
            for cb in range(nblk):
                sm_scr[cb] += part[:, cb * LANES:(cb + 1) * LANES]

    @pl.when(s == S5_SPLIT - 1)
    def _():
        ar = ar_ref[0]
        ai = ai_ref[0]
        h0 = h0_ref[0]
        blk = lambda a, cb: a[:, cb * LANES:(cb + 1) * LANES]

        def body(n, carry):
            rows_f = pl.ds(n, n_seq, stride=n_chunks)
            rows_b = pl.ds(n_chunks - 1 - n, n_seq, stride=n_chunks)
            new = list(carry)
            for d, rows in ((0, rows_f), (1, rows_b)):
                for c in range(sb):
                    re_i = d * sb + c
                    im_i = (2 + d) * sb + c
                    hr, hi = carry[re_i], carry[im_i]
                    hp_scr[re_i, rows, :] = hr
                    hp_scr[im_i, rows, :] = hi
                    a_r, a_i = blk(ar, re_i), blk(ai, re_i)
                    new[re_i] = a_r * hr - a_i * hi + sm_scr[re_i, rows, :]
                    new[im_i] = a_r * hi + a_i * hr + sm_scr[im_i, rows, :]
            return tuple(new)

        fin = lax.fori_loop(0, n_chunks, body, tuple(blk(h0, cb) for cb in range(nblk)))
        hf_ref[0] = jnp.concatenate(fin, axis=1)
        for cb in range(nblk):
            hpb_scr[:, cb * LANES:(cb + 1) * LANES] = hp_scr[cb].astype(BF16)

    @pl.when(s >= S5_SPLIT)
    def _():
        ub = jnp.concatenate([ub_scr[k] for k in range(S5_SPLIT)], axis=1)
        y = _dot(ub, t_ref[0]) + _dot(hpb_scr[...], cq_ref[0])
        dd = d_ref[0]
        for ii in range(per):
            rows = pl.ds((s - S5_SPLIT) * per + ii, m, stride=q)
            yi = y[:, ii * LANES:(ii + 1) * LANES] + dd * u_ref[rows, :]
            y_ref[rows, :] = jax.nn.gelu(yi)


def _s5_mats(lam_re, lam_im, log_dt, b_re, b_im, c_re, c_im, d):
    q, g, p, ch = S5_Q, S5_GROUPS, S5_STATE, S5_GROUP
    lam = lax.complex(jnp.minimum(lam_re.astype(F32), -1e-4), lam_im.astype(F32))
    ldt = lam * jnp.exp(log_dt.astype(F32))[..., None]
    lam_bar = jnp.exp(ldt)
    b_bar = ((lam_bar - 1.0) / lam)[..., None] * lax.complex(b_re.astype(F32), b_im.astype(F32))
    cc = lax.complex(c_re.astype(F32), c_im.astype(F32))
    pw = jnp.exp(ldt[..., None] * jnp.arange(q + 1, dtype=F32))
    hi = lax.Precision.HIGHEST
    kern = jnp.real(jnp.einsum('zgcp,zgpd,zgpe->zgdce', cc, pw[..., :q], b_bar, precision=hi))
    i = jnp.arange(q)[None, :]
    j = jnp.arange(q)[:, None]
    dif = i - j
    kf = jnp.where((dif >= 0)[None, :, :, None, None], kern[0][:, jnp.maximum(dif, 0)], 0.0)
    kb = jnp.where((dif <= 0)[None, :, :, None, None], kern[1][:, jnp.maximum(-dif, 0)], 0.0)
    tm = (kf + kb).transpose(0, 1, 4, 2, 3).reshape(g, q * ch, q * ch)

    bf = pw[0][..., ::-1][..., 1:][:, :, :, None] * b_bar[0][:, :, None, :]
    bb = pw[1][..., :q][:, :, :, None] * b_bar[1][:, :, None, :]
    to_rows = lambda m: m.transpose(0, 2, 3, 1).reshape(g, q * ch, p)
    bq = jnp.concatenate([to_rows(jnp.real(bf)), to_rows(jnp.real(bb)),
                          to_rows(jnp.imag(bf)), to_rows(jnp.imag(bb))], axis=-1)

    cf = cc[0].transpose(0, 2, 1)[:, :, None, :] * pw[0][..., 1:][:, :, :, None]
    cb = cc[1].transpose(0, 2, 1)[:, :, None, :] * pw[1][..., ::-1][..., :q][:, :, :, None]
    to_cols = lambda m: m.reshape(g, p, q * ch)
    cq = jnp.concatenate([to_cols(jnp.real(cf)), to_cols(jnp.real(cb)),
                          to_cols(-jnp.imag(cf)), to_cols(-jnp.imag(cb))], axis=1)

    gb, nb = S5_GB, g // S5_GB
    eye = jnp.eye(gb, dtype=F32)
    tm6 = tm.reshape(nb, gb, q, ch, q, ch).transpose(0, 2, 1, 3, 4, 5)
    tm_bd = (tm6[:, :, :, :, :, None, :] * eye[None, None, :, None, None, :, None]).reshape(nb, S5_W, S5_W)
    bq6 = bq.reshape(nb, gb, q, ch, 4, p).transpose(0, 2, 1, 3, 4, 5)
    bq_bd = (bq6[:, :, :, :, :, None, :] * eye[None, None, :, None, None, :, None]).reshape(nb, S5_W, 4 * S5_SW)
    cq6 = cq.reshape(nb, gb, 4, p, q, ch).transpose(0, 2, 1, 3, 4, 5)
    cq_bd = (cq6[:, :, :, :, :, None, :] * eye[None, None, :, None, None, :, None]).reshape(nb, 4 * S5_SW, S5_W)

    lq = pw[..., q].reshape(2, nb, 1, S5_SW)
    ar = jnp.concatenate([jnp.real(lq[0]), jnp.real(lq[1])], axis=-1)
    ai = jnp.concatenate([jnp.imag(lq[0]), jnp.imag(lq[1])], axis=-1)
    dd = d.astype(F32).reshape(nb, 1, LANES)
    return tm_bd.astype(BF16), bq_bd.astype(BF16), cq_bd.astype(BF16), ar, ai, dd


def _s5(proj, mats, h0_re, h0_im, *, row0, n_seq, seq_len, dst=None):
    q, p = S5_Q, S5_STATE
    n_chunks = seq_len // q
    m = n_seq * n_chunks
    rows = n_seq * seq_len
    nb = S5_GROUPS // S5_GB
    nblk = S5_W // LANES
    kw = S5_W // S5_SPLIT
    tm, bq, cq, ar, ai, dd = mats
    part = lambda a: a.astype(F32).reshape(n_seq, nb, S5_SW)
    h0 = jnp.concatenate([part(h0_re[:, 0]), part(h0_re[:, 1]), part(h0_im[:, 0]), part(h0_im[:, 1])],
                         axis=-1).transpose(1, 0, 2)
    per_b = lambda shape: pl.BlockSpec((1,) + shape, lambda b, s: (b, 0, 0))
    body, dst_spec, dst_arg, alias = _into(functools.partial(_s5_kernel, n_seq=n_seq, n_chunks=n_chunks), 8, dst)
    y, hf = pl.pallas_call(
        body,
        grid=(nb, 2 * S5_SPLIT),
        in_specs=[
            pl.BlockSpec((rows, LANES), lambda b, s: (row0 // rows, U_COL // LANES + b)),
            pl.BlockSpec((1, kw, 4 * S5_SW), lambda b, s: (b, jnp.minimum(s, S5_SPLIT - 1), 0)),
            pl.BlockSpec((1, S5_W, kw), lambda b, s: (b, 0, jnp.maximum(s - S5_SPLIT, 0))),
            pl.BlockSpec((1, 4 * S5_SW, kw), lambda b, s: (b, 0, jnp.maximum(s - S5_SPLIT, 0))),
            per_b((1, 2 * S5_SW)), per_b((1, 2 * S5_SW)), per_b((1, LANES)), per_b((n_seq, 4 * S5_SW)),
        ] + dst_spec,
        out_specs=[pl.BlockSpec((rows, LANES), lambda b, s: (row0 // rows, b)), per_b((n_seq, 4 * S5_SW))],
        out_shape=[jax.ShapeDtypeStruct((proj.shape[0], S5_WIDTH), F32),
                   jax.ShapeDtypeStruct((nb, n_seq, 4 * S5_SW), F32)],
        scratch_shapes=[pltpu.VMEM((S5_SPLIT, m, kw), BF16), pltpu.VMEM((nblk, m, LANES), F32),
                        pltpu.VMEM((nblk, m, LANES), F32), pltpu.VMEM((m, 4 * S5_SW), BF16)],
        input_output_aliases=alias,
        compiler_params=_cparams(("parallel", "arbitrary")),
        name="s5",
    )(proj, bq, tm, cq, ar, ai, dd, h0, *dst_arg)
    hf = hf.reshape(nb, n_seq, 4, S5_GB, p).transpose(1, 2, 0, 3, 4).reshape(n_seq, 4, S5_GROUPS, p)
    return y, hf[:, 0:2], hf[:, 2:4]


def _conv3(x, w, b):
    n = x.shape[0]
    row = lax.broadcasted_iota(jnp.int32, x.shape, 0)
    prev = jnp.where(row == 0, 0.0, pltpu.roll(x, 1, 0))
    nxt = jnp.where(row == n - 1, 0.0, pltpu.roll(x, n - 1, 0))
    return prev * w[0:1] + x * w[1:2] + nxt * w[2:3] + b


def _hy_fwd_kernel(x0_ref, x1_ref, v_ref, w0_ref, w1_ref, wv_ref, b0_ref, b1_ref, bv_ref,
                   fc_ref, fs_ref, m1_ref, m2_ref, m3_ref, p_ref, z_ref, x0c_ref, zb_scr):
    @pl.when(pl.program_id(2) == 0)
    def _():
        z = _conv3(x1_ref[...], w1_ref[...], b1_ref[...]) * _conv3(v_ref[...], wv_ref[...], bv_ref[...])
        z_ref[...] = z
        zb_scr[...] = z.astype(BF16)
        x0c_ref[...] = _conv3(x0_ref[...], w0_ref[...], b0_ref[...])

    zb = zb_scr[...]
    a = _dot(fc_ref[...], zb)
    b = _dot(fs_ref[...], zb)
    m2 = m2_ref[...]
    p_ref[0, 0] = (m1_ref[...] * a + m2 * b).astype(BF16)
    p_ref[0, 1] = (m3_ref[...] * b - m2 * a).astype(BF16)


def _hy_inv_kernel(p_ref, gc_ref, gs_ref, z_ref, x0c_ref, bias_ref, o_ref):
    conv = _dot(gc_ref[...], p_ref[0, 0]) + _dot(gs_ref[...], p_ref[0, 1])
    o_ref[...] = (x0c_ref[...] * (conv + bias_ref[...] * z_ref[...])).astype(o_ref.dtype)


def _dft_mats(seq_len):
    k = jnp.arange(seq_len, dtype=jnp.int32)
    kj = (k[:, None] * k[None, :]) % (2 * seq_len)
    ang = kj.astype(F32) * (math.pi / seq_len)
    cm = jnp.cos(ang)
    sm = -jnp.sin(ang)
    nyq = jnp.where(k % 2 == 0, 1.0, -1.0).astype(F32)
    sm = sm.at[0, :].set(nyq)
    return cm.astype(BF16), sm.astype(BF16), sm.T.astype(BF16)


def _hy_filter_mults(seq_len, f1_w, f1_b, f2_w, f2_b, f3_w, f3_b, freq, decay):
    n = seq_len
    t = (jnp.arange(n, dtype=F32) / n)[:, None]
    bands = jnp.arange(1, HY_BANDS + 1, dtype=F32)[None, :]
    z = jnp.concatenate([t, jnp.cos(2.0 * math.pi * t * bands), jnp.sin(2.0 * math.pi * t * bands)], axis=-1)
    hi = lax.Precision.HIGHEST
    fr = freq.astype(F32)
    h = jnp.sin(fr * (jnp.dot(z, f1_w.astype(F32), precision=hi) + f1_b.astype(F32)))
    h = jnp.sin(fr * (jnp.dot(h, f2_w.astype(F32), precision=hi) + f2_b.astype(F32)))
    h = jnp.dot(h, f3_w.astype(F32), precision=hi) + f3_b.astype(F32)
    h = h * jnp.exp(-t * jnp.abs(decay.astype(F32)))
    h = h.reshape(n, 2, HY_WIDTH)
    h = h / jnp.sum(jnp.abs(h), axis=(0, 1), keepdims=True)
    hc = jnp.concatenate([h[:, 0], h[::-1, 1]], axis=0)
    spec = jnp.fft.rfft(hc, axis=0)
    hr, him = jnp.real(spec), jnp.imag(spec)
    wk = jnp.where(jnp.arange(n) == 0, 1.0, 2.0)[:, None] / (2.0 * n)
    first = (jnp.arange(n) == 0)[:, None]
    m1 = hr[:n] * wk
    m2 = jnp.where(first, 0.0, -him[:n]) * wk
    m3 = jnp.where(first, hr[n:n + 1], hr[:n]) * wk
    return m1.astype(F32), m2.astype(F32), m3.astype(F32)


def _hyena(proj, conv_w, conv_b, bias, dft, mults, *, row0, n_seq, seq_len, cb, tk, dst=None):
    blk0 = row0 // seq_len
    nc = HY_WIDTH // cb
    nk = seq_len // tk
    c0 = HY_COL // cb
    cm, sm, smt = dft
    m1, m2, m3 = mults
    xcol = lambda part: pl.BlockSpec((seq_len, cb), lambda b, c, k: (blk0 + b, c0 + part * nc + c))
    wcol = lambda part: pl.BlockSpec((3, cb), lambda b, c, k: (0, part * nc + c))
    bcol = lambda part: pl.BlockSpec((1, cb), lambda b, c, k: (0, part * nc + c))
    frow = pl.BlockSpec((tk, seq_len), lambda b, c, k: (k, 0))
    mblk = pl.BlockSpec((tk, cb), lambda b, c, k: (k, c))
    cb2 = conv_b.reshape(1, 3 * HY_WIDTH)
    pspec, z, x0c = pl.pallas_call(
        _hy_fwd_kernel,
        grid=(n_seq, nc, nk),
        in_specs=[xcol(0), xcol(1), xcol(2), wcol(0), wcol(1), wcol(2), bcol(0), bcol(1), bcol(2),
                  frow, frow, mblk, mblk, mblk],
        out_specs=[
            pl.BlockSpec((1, 2, tk, cb), lambda b, c, k: (b, 0, k, c)),
            pl.BlockSpec((seq_len, cb), lambda b, c, k: (b, c)),
            pl.BlockSpec((seq_len, cb), lambda b, c, k: (b, c)),
        ],
        out_shape=[
            jax.ShapeDtypeStruct((n_seq, 2, seq_len, HY_WIDTH), BF16),
            jax.ShapeDtypeStruct((n_seq * seq_len, HY_WIDTH), F32),
            jax.ShapeDtypeStruct((n_seq * seq_len, HY_WIDTH), F32),
        ],
        scratch_shapes=[pltpu.VMEM((seq_len, cb), BF16)],
        compiler_params=_cparams(("parallel", "parallel", "arbitrary")),
        name="hyena_fwd",
    )(proj, proj, proj, conv_w, conv_w, conv_w, cb2, cb2, cb2, cm, sm, m1, m2, m3)
    grow = pl.BlockSpec((tk, seq_len), lambda b, c, k: (k, 0))
    tile = pl.BlockSpec((tk, cb), lambda b, c, k: (b * nk + k, c))
    body, dst_spec, dst_arg, alias = _into(_hy_inv_kernel, 6, dst)
    return pl.pallas_call(
        body,
        grid=(n_seq, nc, nk),
        in_specs=[pl.BlockSpec((1, 2, seq_len, cb), lambda b, c, k: (b, 0, 0, c)),
                  grow, grow, tile, tile, pl.BlockSpec((1, cb), lambda b, c, k: (0, c))] + dst_spec,
        out_specs=pl.BlockSpec((tk, cb), lambda b, c, k: (row0 // tk + b * nk + k, c)),
        out_shape=jax.ShapeDtypeStruct((proj.shape[0], HY_WIDTH), BF16),
        input_output_aliases=alias,
        compiler_params=_cparams(("parallel", "parallel", "arbitrary")),
        name="hyena_inv",
    )(pspec, cm, smt, z, x0c, bias.reshape(1, HY_WIDTH), *dst_arg)


def _out_kernel(x_ref, ret_ref, s5_ref, hy_ref, mod_ref, g_ref, gw_ref, gb_ref,
                wr_ref, ws_ref, wh_ref, rt_ref, xo_ref, h_ref, lg_ref):
    m = mod_ref[0]
    y = s5_ref[...]
    s5o = y * jax.nn.sigmoid(_dot(y.astype(BF16), gw_ref[...]) + gb_ref[...])
    mix = (_dot(ret_ref[...], wr_ref[...]) + _dot(s5o.astype(BF16), ws_ref[...])
           + _dot(hy_ref[...], wh_ref[...]))
    x = x_ref[...] + m[2:3] * mix
    xo_ref[...] = x
    h = _rms(x, g_ref[...]) * (1.0 + m[4:5]) + m[3:4]
    hb = h.astype(BF16)
    h_ref[...] = _pack_halves(hb)
    lg_ref[...] = lax.dot_general(rt_ref[...], hb, (((1,), (1,)), ((), ())), preferred_element_type=F32)


def _out_proj(x, ret_o, s5_y, hy_o, mod, g, glu_w, glu_b, w_out, router, n_ctx_rows, dec_seq):
    t = x.shape[0]
    tm = ROW_TILE
    grp = functools.partial(_group_of, tile=tm, n_ctx_rows=n_ctx_rows, dec_seq=dec_seq)
    row = lambda w: pl.BlockSpec((tm, w), lambda i: (i, 0))
    full = lambda a, b: pl.BlockSpec((a, b), lambda i: (0, 0))
    wo = w_out.astype(BF16)
    return pl.pallas_call(
        _out_kernel,
        grid=(t // tm,),
        in_specs=[row(D_MODEL), row(RET_WIDTH), row(S5_WIDTH), row(HY_WIDTH),
                  pl.BlockSpec((1, 6, D_MODEL), lambda i: (grp(i), 0, 0)),
                  full(1, D_MODEL), full(S5_WIDTH, S5_WIDTH), full(1, S5_WIDTH),
                  full(RET_WIDTH, D_MODEL), full(S5_WIDTH, D_MODEL), full(HY_WIDTH, D_MODEL),
                  full(N_EXPERTS, D_MODEL)],
        out_specs=[row(D_MODEL), row(D_MODEL // 2), pl.BlockSpec((N_EXPERTS, tm), lambda i: (0, i))],
        out_shape=[jax.ShapeDtypeStruct((t, D_MODEL), F32), jax.ShapeDtypeStruct((t, D_MODEL // 2), jnp.uint32),
                   jax.ShapeDtypeStruct((N_EXPERTS, t), F32)],
        compiler_params=_cparams(("parallel",)),
        name="out_proj",
    )(x, ret_o, s5_y, hy_o, mod, g.reshape(1, D_MODEL), glu_w.astype(BF16), glu_b.reshape(1, S5_WIDTH),
      wo[:RET_WIDTH], wo[RET_WIDTH:RET_WIDTH + S5_WIDTH], wo[RET_WIDTH + S5_WIDTH:], router.T.astype(BF16))


def _moe_kernel(be_ref, first_ref, slot_ref, nxt_ref, nu_ref, xs_ref, wg_hbm, wu_hbm, wd_hbm, o_ref,
                wg_f, wu_f, wd_f, wg_b, wu_b, wd_b, sem):
    i = pl.program_id(0)

    def copies(e, s):
        return (pltpu.make_async_copy(wg_hbm.at[e], wg_f.at[s], sem.at[s, 0]),
                pltpu.make_async_copy(wu_hbm.at[e], wu_f.at[s], sem.at[s, 1]),
                pltpu.make_async_copy(wd_hbm.at[e], wd_f.at[s], sem.at[s, 2]))

    @pl.when(i == 0)
    def _():
        for cp in copies(be_ref[0], 0):
            cp.start()

    @pl.when(first_ref[i] == 1)
    def _():
        s = slot_ref[i]
        for cp in copies(be_ref[i], s):
            cp.wait()

        @pl.when(nxt_ref[i] >= 0)
        def _():
            for cp in copies(nxt_ref[i], 1 - s):
                cp.start()

        wg_b[...] = wg_f[s].astype(BF16)
        wu_b[...] = wu_f[s].astype(BF16)
        wd_b[...] = wd_f[s].astype(BF16)

    @pl.when(i < nu_ref[0])
    def _():
        half = D_MODEL // 2
        x_lo, x_hi = _unpack_halves(xs_ref[...])
        x_lo = x_lo.astype(BF16)
        x_hi = x_hi.astype(BF16)
        gate = _dot(x_lo, wg_b[0:half, :]) + _dot(x_hi, wg_b[half:, :])
        up = _dot(x_lo, wu_b[0:half, :]) + _dot(x_hi, wu_b[half:, :])
        hb = gate * jax.nn.sigmoid(gate) * up
        o_ref[...] = _pack_halves(_dot(hb.astype(BF16), wd_b[...]).astype(BF16))

    @pl.when(i >= nu_ref[0])
    def _():
        o_ref[...] = jnp.zeros_like(o_ref)


def _moe_grouped(xs, blk_e, first, slot, nxt, n_used, w_gate, w_up, w_down):
    pr = xs.shape[0]
    bm = MOE_BM
    nb = pr // bm
    grid_spec = pltpu.PrefetchScalarGridSpec(
        num_scalar_prefetch=5,
        grid=(nb,),
        in_specs=[
            pl.BlockSpec((bm, D_MODEL // 2), lambda i, *_: (i, 0)),
            pl.BlockSpec(memory_space=pl.ANY),
            pl.BlockSpec(memory_space=pl.ANY),
            pl.BlockSpec(memory_space=pl.ANY),
        ],
        out_specs=pl.BlockSpec((bm, D_MODEL // 2), lambda i, *_: (i, 0)),
        scratch_shapes=[pltpu.VMEM((2, D_MODEL, D_EXPERT), F32), pltpu.VMEM((2, D_MODEL, D_EXPERT), F32),
                        pltpu.VMEM((2, D_EXPERT, D_MODEL), F32),
                        pltpu.VMEM((D_MODEL, D_EXPERT), BF16), pltpu.VMEM((D_MODEL, D_EXPERT), BF16),
                        pltpu.VMEM((D_EXPERT, D_MODEL), BF16),
                        pltpu.SemaphoreType.DMA((2, 3))],
    )
    return pl.pallas_call(
        _moe_kernel,
        grid_spec=grid_spec,
        out_shape=jax.ShapeDtypeStruct((pr, D_MODEL // 2), jnp.uint32),
        compiler_params=_cparams(("arbitrary",)),
        name="moe_grouped",
    )(blk_e, first, slot, nxt, n_used, xs, w_gate, w_up, w_down)


GATHER_CHUNK = 512


def _gather_kernel(idx_ref, src, dst, sem):
    ch = GATHER_CHUNK
    i = pl.program_id(0)
    last = pl.num_programs(0) - 1

    def issue(r, carry):
        row_copy = pltpu.make_async_copy(src.at[pl.ds(idx_ref[0, 0, r], 1)], dst.at[pl.ds(i * ch + r, 1)],
                                         sem.at[i % 2])
        row_copy.start()
        return carry

    def drain(step):
        pltpu.make_async_copy(src.at[pl.ds(0, ch)], dst.at[pl.ds(step * ch, ch)], sem.at[step % 2]).wait()

    lax.fori_loop(0, ch, issue, 0, unroll=8)

    @pl.when(i > 0)
    def _():
        drain(i - 1)

    @pl.when(i == last)
    def _():
        drain(i)


def _gather_rows(src, idx):
    n = idx.shape[0]
    ch = GATHER_CHUNK
    return pl.pallas_call(
        _gather_kernel,
        grid=(n // ch,),
        in_specs=[pl.BlockSpec((1, 1, ch), lambda i: (i, 0, 0), memory_space=pltpu.SMEM),
                  pl.BlockSpec(memory_space=pl.ANY)],
        out_specs=pl.BlockSpec(memory_space=pl.ANY),
        out_shape=jax.ShapeDtypeStruct((n, src.shape[1]), src.dtype),
        scratch_shapes=[pltpu.SemaphoreType.DMA((2,))],
        compiler_params=pltpu.CompilerParams(dimension_semantics=("arbitrary",)),
        name="gather_rows",
    )(idx.reshape(n // ch, 1, ch), src)


ROUTE_TILE = 512


def _router_kernel(lg_ref, bias_ref, pos_ref, gate_ref, be_ref, info_ref, rank_scr, ek_scr, *, t, nbp):
    tl = ROUTE_TILE
    ne = N_EXPERTS
    bm = MOE_BM
    row = lax.broadcasted_iota(jnp.int32, (ne, tl), 0)
    tri = (lax.broadcasted_iota(jnp.int32, (tl, tl), 0) < lax.broadcasted_iota(jnp.int32, (tl, tl), 1)).astype(BF16)
    bias = bias_ref[...]

    def select(i, counts):
        cols = pl.ds(pl.multiple_of(i * tl, tl), tl)
        s = jax.nn.sigmoid(lg_ref[:, cols])
        sel = s + bias
        mask = jnp.zeros((ne, tl), F32)
        vals = []
        for k in range(TOP_K):
            best = jnp.max(sel, axis=0, keepdims=True)
            idx = jnp.min(jnp.where(sel == best, row, ne), axis=0, keepdims=True)
            hit = row == idx
            vals.append(jnp.sum(jnp.where(hit, s, 0.0), axis=0, keepdims=True))
            sel = jnp.where(hit, -jnp.inf, sel)
            mask = jnp.where(hit, 1.0, mask)
            ek_scr[k:k + 1, cols] = idx
        total = vals[0]
        for v in vals[1:]:
            total = total + v
        scale = ROUTED_SCALE / total
        for k in range(TOP_K):
            gate_ref[k:k + 1, cols] = vals[k] * scale
        gate_ref[TOP_K:SUBLANES, cols] = jnp.zeros((SUBLANES - TOP_K, tl), F32)
        rank_scr[:, cols] = _dot(mask.astype(BF16), tri) + counts
        return counts + jnp.sum(mask, axis=1, keepdims=True)

    counts = lax.fori_loop(0, t // tl, select, jnp.zeros((ne, 1), F32))
    counts = counts.astype(jnp.int32)
    shift = bm.bit_length() - 1
    assert bm == 1 << shift
    padded = ((counts + (bm - 1)) >> shift) << shift
    e0 = lax.broadcasted_iota(jnp.int32, (ne, ne), 0)
    e1 = lax.broadcasted_iota(jnp.int32, (ne, ne), 1)
    padded_row = jnp.sum(jnp.where(e0 == e1, padded, 0), axis=0, keepdims=True)
    counts_row = jnp.sum(jnp.where(e0 == e1, counts, 0), axis=0, keepdims=True)
    pstart = jnp.sum(jnp.where(e1 < e0, padded_row, 0), axis=1, keepdims=True)
    ustart = jnp.sum(jnp.where(e1 < e0, counts_row, 0), axis=1, keepdims=True)
    pend = pstart + padded
    lane = lax.broadcasted_iota(jnp.int32, (ne, LANES), 1)
    info_ref[...] = jnp.where(lane == 0, counts, jnp.where(lane == 1, pstart, jnp.where(lane == 2, ustart, pend)))
    blk = lax.broadcasted_iota(jnp.int32, (ne, nbp), 1) * bm
    owner = jnp.sum(jnp.where(pend <= blk, 1, 0), axis=0, keepdims=True)
    be_ref[...] = jnp.minimum(owner, ne - 1)
    pstart_f = pstart.astype(F32)

    def place(i, carry):
        cols = pl.ds(pl.multiple_of(i * tl, tl), tl)
        dest = rank_scr[:, cols] + pstart_f
        for k in range(TOP_K):
            hit = row == ek_scr[k:k + 1, cols]
            pos_ref[k:k + 1, cols] = jnp.sum(jnp.where(hit, dest, 0.0), axis=0, keepdims=True).astype(jnp.int32)
        pos_ref[TOP_K:SUBLANES, cols] = jnp.zeros((SUBLANES - TOP_K, tl), jnp.int32)
        return carry

    lax.fori_loop(0, t // tl, place, 0)


def _router(logits_t, router_bias, nb):
    t = logits_t.shape[1]
    nbp = -(-nb // LANES) * LANES
    return pl.pallas_call(
        functools.partial(_router_kernel, t=t, nbp=nbp),
        out_shape=[jax.ShapeDtypeStruct((SUBLANES, t), jnp.int32), jax.ShapeDtypeStruct((SUBLANES, t), F32),
                   jax.ShapeDtypeStruct((1, nbp), jnp.int32), jax.ShapeDtypeStruct((N_EXPERTS, LANES), jnp.int32)],
        scratch_shapes=[pltpu.VMEM((N_EXPERTS, t), F32), pltpu.VMEM((SUBLANES, t), jnp.int32)],
        compiler_params=pltpu.CompilerParams(vmem_limit_bytes=VMEM_LIMIT),
        name="router",
    )(logits_t, router_bias.astype(F32).reshape(N_EXPERTS, 1))


def _dispatch_plan(pos, blk_e_row, info, nb):
    t = pos.shape[1]
    bm = MOE_BM
    counts, pstart, ustart, pend = info[:, 0], info[:, 1], info[:, 2], info[:, 3]
    blk_e = blk_e_row[0, :nb]
    n_used = pend[-1] // bm
    keys = pos[:TOP_K].reshape(-1)
    toks = jnp.tile(jnp.arange(t, dtype=jnp.int32), TOP_K)
    _, sorted_tok = lax.sort_key_val(keys, toks)
    r = jnp.arange(nb * bm, dtype=jnp.int32).reshape(nb, bm) - pstart[blk_e][:, None]
    src = jnp.clip(ustart[blk_e][:, None] + r, 0, t * TOP_K - 1)
    slot_tok = jnp.where(r < counts[blk_e][:, None], sorted_tok[src], 0).reshape(-1)
    blk = jnp.arange(nb, dtype=jnp.int32)
    prev_e = jnp.concatenate([jnp.full((1,), -1, jnp.int32), blk_e[:-1]])
    first = jnp.logical_and(blk < n_used, blk_e != prev_e)
    slot = (jnp.cumsum(first.astype(jnp.int32)) - 1) % 2
    first_at = jnp.where(first, blk, nb)
    nxt_first = lax.cummin(jnp.concatenate([first_at[1:], jnp.full((1,), nb, jnp.int32)]), reverse=True)
    nxt = jnp.where(nxt_first < nb, blk_e[jnp.minimum(nxt_first, nb - 1)], -1)
    return (slot_tok, blk_e, first.astype(jnp.int32), slot.astype(jnp.int32), nxt.astype(jnp.int32),
            n_used.astype(jnp.int32).reshape(1))


def _shared_kernel(x_ref, h_ref, gt_ref, *rest, final):
    e_refs = rest[:TOP_K]
    mod_ref, sg_ref, su_ref, sd_ref, fn_ref, o_ref = rest[TOP_K:]
    m = mod_ref[0]
    half = D_MODEL // 2
    h_lo, h_hi = _unpack_halves(h_ref[...])
    h_lo = h_lo.astype(BF16)
    h_hi = h_hi.astype(BF16)
    gate = _dot(h_lo, sg_ref[0:half, :]) + _dot(h_hi, sg_ref[half:, :])
    up = _dot(h_lo, su_ref[0:half, :]) + _dot(h_hi, su_ref[half:, :])
    act = gate * jax.nn.sigmoid(gate) * up
    shared = _dot(act.astype(BF16), sd_ref[...])
    gt = gt_ref[...]
    r_lo = jnp.zeros((h_lo.shape[0], half), F32)
    r_hi = r_lo
    for k in range(TOP_K):
        e_lo, e_hi = _unpack_halves(e_refs[k][0])
        r_lo = r_lo + gt[:, k:k + 1] * e_lo
        r_hi = r_hi + gt[:, k:k + 1] * e_hi
    routed = jnp.concatenate([r_lo, r_hi], axis=1)
    x = x_ref[...] + m[5:6] * (routed + shared)
    if final:
        x = _rms(x, fn_ref[...])
    o_ref[...] = x


def _shared(x, h, gates_t, picked, mod, sg, su, sd, final_norm, n_ctx_rows, dec_seq, *, final, row0=0, rows=None):
    rows = x.shape[0] if rows is None else rows
    tm = ROW_TILE // 2
    b0 = row0 // tm
    grp = lambda i: _group_of(i + b0, tm, n_ctx_rows, dec_seq)
    row = pl.BlockSpec((tm, D_MODEL), lambda i: (i + b0, 0))
    prow = pl.BlockSpec((tm, D_MODEL // 2), lambda i: (i + b0, 0))
    full = lambda a, b: pl.BlockSpec((a, b), lambda i: (0, 0))
    pick = lambda k: pl.BlockSpec((1, tm, D_MODEL // 2), lambda i: (k, i + b0, 0))
    return pl.pallas_call(
        functools.partial(_shared_kernel, final=final),
        grid=(rows // tm,),
        in_specs=[row, prow, pl.BlockSpec((tm, SUBLANES), lambda i: (i + b0, 0))]
                 + [pick(k) for k in range(TOP_K)]
                 + [pl.BlockSpec((1, 6, D_MODEL), lambda i: (grp(i), 0, 0)),
                    full(D_MODEL, D_SHARED), full(D_MODEL, D_SHARED), full(D_SHARED, D_MODEL), full(1, D_MODEL)],
        out_specs=pl.BlockSpec((tm, D_MODEL), lambda i: (i, 0)),
        out_shape=jax.ShapeDtypeStruct((rows, D_MODEL), F32),
        compiler_params=_cparams(("parallel",)),
        name="shared_final" if final else "shared",
    )(x, h, gates_t, *([picked] * TOP_K), mod, sg.astype(BF16), su.astype(BF16), sd.astype(BF16),
      final_norm.reshape(1, D_MODEL))


def kernel(x_prompt, x_sample, state_ret, state_s5_re, state_s5_im, c, c_ctx, w_ada, b_ada, norm_mix, norm_ffn, w_in, w_out, ret_decay, s5_lam_re, s5_lam_im, s5_log_dt, s5_b_re, s5_b_im, s5_c_re, s5_c_im, s5_d, s5_glu_w, s5_glu_b, hy_conv_w, hy_conv_b, hy_f1_w, hy_f1_b, hy_f2_w, hy_f2_b, hy_f3_w, hy_f3_b, hy_freq, hy_decay, hy_bias, moe_router, moe_router_bias, moe_w_gate, moe_w_up, moe_w_down, sh_w_gate, sh_w_up, sh_w_down, final_norm):
    n_ctx, seq, d = x_prompt.shape
    n_dec, dec_seq, _ = x_sample.shape
    n_ctx_rows = n_ctx * seq
    t = n_ctx_rows + n_dec * dec_seq

    x = jnp.concatenate([x_prompt.reshape(n_ctx_rows, d), x_sample.reshape(n_dec * dec_seq, d)], axis=0)
    cond = jnp.concatenate([c_ctx[None, :], c], axis=0)
    cond8 = jnp.pad(cond, ((0, SUBLANES - cond.shape[0]), (0, 0)))
    mods = _ada(cond8, w_ada, b_ada)[:, :1 + n_dec].reshape(DEPTH, 1 + n_dec, 6, d)

    cos2, sin2 = _rope_tables(dec_seq)
    no_rope = jnp.zeros((seq, LANES), F32)
    zero_ret = jnp.zeros((n_ctx, 2, RET_HEADS, RET_DK, RET_DV), F32)
    zero_s5 = jnp.zeros((n_ctx, 2, S5_GROUPS, S5_STATE), F32)
    dft_ctx = _dft_mats(seq)
    dft_dec = _dft_mats(dec_seq)

    ret_list, s5r_list, s5i_list = [], [], []
    for l in range(DEPTH):
        mod = mods[l]
        proj = _in_proj(x, mod, norm_mix[l], w_in[l].astype(BF16), n_ctx_rows, dec_seq)

        log_gamma = jax.nn.log_sigmoid(ret_decay[l].astype(F32))
        ret_o, ret_s = _retention(proj, log_gamma, zero_ret, no_rope, no_rope,
                                  row0=0, n_seq=n_ctx, seq_len=seq, hb=RET_HEADS, rope=False,
                                  dst=jnp.zeros((t, RET_WIDTH), BF16))
        ret_o, _ = _retention(proj, log_gamma, state_ret[:, l].astype(F32), cos2, sin2,
                              row0=n_ctx_rows, n_seq=n_dec, seq_len=dec_seq, hb=1, rope=True, dst=ret_o)
        ret_list.append(ret_s)

        mats = _s5_mats(s5_lam_re[l], s5_lam_im[l], s5_log_dt[l], s5_b_re[l], s5_b_im[l],
                        s5_c_re[l], s5_c_im[l], s5_d[l])
        s5_y, s5_re, s5_im = _s5(proj, mats, zero_s5, zero_s5, row0=0, n_seq=n_ctx, seq_len=seq,
                                 dst=jnp.zeros((t, S5_WIDTH), F32))
        s5_y, _, _ = _s5(proj, mats, state_s5_re[:, l], state_s5_im[:, l],
                         row0=n_ctx_rows, n_seq=n_dec, seq_len=dec_seq, dst=s5_y)
        s5r_list.append(s5_re)
        s5i_list.append(s5_im)

        filt = (hy_f1_w[l], hy_f1_b[l], hy_f2_w[l], hy_f2_b[l], hy_f3_w[l], hy_f3_b[l], hy_freq[l], hy_decay[l])
        hy_o = _hyena(proj, hy_conv_w[l], hy_conv_b[l], hy_bias[l], dft_ctx, _hy_filter_mults(seq, *filt),
                      row0=0, n_seq=n_ctx, seq_len=seq, cb=HY_WIDTH, tk=seq, dst=jnp.zeros((t, HY_WIDTH), BF16))
        hy_o = _hyena(proj, hy_conv_w[l], hy_conv_b[l], hy_bias[l], dft_dec, _hy_filter_mults(dec_seq, *filt),
                      row0=n_ctx_rows, n_seq=n_dec, seq_len=dec_seq, cb=HY_WIDTH // 2, tk=512, dst=hy_o)

        x, h2, logits = _out_proj(x, ret_o, s5_y, hy_o, mod, norm_ffn[l], s5_glu_w[l], s5_glu_b[l],
                                  w_out[l], moe_router[l], n_ctx_rows, dec_seq)

        nb = -(-(t * TOP_K) // MOE_BM) + N_EXPERTS
        pos, gates, blk_e_row, info = _router(logits, moe_router_bias[l], nb)
        slot_tok, blk_e, first, slot, nxt, n_used = _dispatch_plan(pos, blk_e_row, info, nb)
        xs = _gather_rows(h2, slot_tok)
        eo = _moe_grouped(xs, blk_e, first, slot, nxt, n_used, moe_w_gate[l], moe_w_up[l], moe_w_down[l])
        picked = _gather_rows(eo, pos[:TOP_K].reshape(-1)).reshape(TOP_K, t, d // 2)
        gates_t = gates.T

        sh = (sh_w_gate[l], sh_w_up[l], sh_w_down[l])
        if l < DEPTH - 1:
            x = _shared(x, h2, gates_t, picked, mod, *sh, final_norm, n_ctx_rows, dec_seq, final=False)
        else:
            y_c = _shared(x, h2, gates_t, picked, mod, *sh, final_norm, n_ctx_rows, dec_seq, final=True,
                          row0=0, rows=n_ctx_rows)
            y_d = _shared(x, h2, gates_t, picked, mod, *sh, final_norm, n_ctx_rows, dec_seq, final=True,
                          row0=n_ctx_rows, rows=n_dec * dec_seq)

    return (y_c.reshape(n_ctx, seq, d), y_d.reshape(n_dec, dec_seq, d),
            jnp.stack(ret_list, axis=1), jnp.stack(s5r_list, axis=1), jnp.stack(s5i_list, axis=1))
```

```python
import functools
import math

import jax
import jax.numpy as jnp
from jax import lax
from jax.experimental import pallas as pl
from jax.experimental.pallas import tpu as pltpu

F32 = jnp.float32
BF16 = jnp.bfloat16

D_MODEL = 2048
DEPTH = 2
GRID_W = 64
RET_HEADS = 8
RET_DK = 128
RET_DV = 128
RET_WIDTH = RET_HEADS * RET_DV
RET_CHUNK = 128
ROPE_BASE = 10000.0
S5_WIDTH = 512
S5_GROUP = 16
S5_GROUPS = S5_WIDTH // S5_GROUP
S5_STATE = 64
S5_Q = 16
HY_WIDTH = 512
HY_BANDS = 16
IN_WIDTH = 4 * RET_WIDTH + S5_WIDTH + 3 * HY_WIDTH
U_COL = 4 * RET_WIDTH
HY_COL = U_COL + S5_WIDTH
N_EXPERTS = 64
TOP_K = 6
D_EXPERT = 512
D_SHARED = 512
ROUTED_SCALE = 2.5
EPS = 1e-6

LANES = 128
SUBLANES = 8
VMEM_LIMIT = 56 * 1024 * 1024

ROW_TILE = 512
MOE_BM = 256


def _cparams(sem):
    return pltpu.CompilerParams(dimension_semantics=sem, vmem_limit_bytes=VMEM_LIMIT)


def _dot(a, b):
    return jnp.dot(a, b, preferred_element_type=F32)


def _rms(x, g):
    var = jnp.mean(x * x, axis=-1, keepdims=True)
    return x * lax.rsqrt(var + EPS) * g


def _pack_halves(xb):
    n = xb.shape[1] // 2
    lo = lax.bitcast_convert_type(xb[:, :n].astype(F32), jnp.uint32) >> 16
    hi = lax.bitcast_convert_type(xb[:, n:].astype(F32), jnp.uint32)
    return lo | hi


def _unpack_halves(w):
    lo = lax.bitcast_convert_type(w << 16, F32)
    hi = lax.bitcast_convert_type(w & jnp.uint32(0xFFFF0000), F32)
    return lo, hi


TOKEN_ROWS = D_MODEL // 2 // LANES


def _store_token_tiles(ref, w):
    m = w.shape[0]
    for s in range(TOKEN_ROWS):
        ref[pl.ds(s, m, stride=TOKEN_ROWS), :] = w[:, s * LANES:(s + 1) * LANES]


def _load_token_tiles(ref, m):
    return jnp.concatenate([ref[pl.ds(s, m, stride=TOKEN_ROWS), :] for s in range(TOKEN_ROWS)], axis=1)


def _ada_kernel(c_ref, w_ref, b_ref, o_ref):
    c = c_ref[...]
    s = (c * jax.nn.sigmoid(c)).astype(BF16)
    o_ref[0] = _dot(s, w_ref[0].astype(BF16)) + b_ref[0]


def _ada(cond8, w_ada, b_ada):
    tn = 512
    n = w_ada.shape[-1]
    return pl.pallas_call(
        _ada_kernel,
        grid=(DEPTH, n // tn),
        in_specs=[
            pl.BlockSpec((SUBLANES, D_MODEL), lambda l, j: (0, 0)),
            pl.BlockSpec((1, D_MODEL, tn), lambda l, j: (l, 0, j)),
            pl.BlockSpec((1, 1, tn), lambda l, j: (l, 0, j)),
        ],
        out_specs=pl.BlockSpec((1, SUBLANES, tn), lambda l, j: (l, 0, j)),
        out_shape=jax.ShapeDtypeStruct((DEPTH, SUBLANES, n), F32),
        compiler_params=_cparams(("parallel", "parallel")),
        name="ada",
    )(cond8, w_ada, b_ada.reshape(DEPTH, 1, n))


def _group_of(i, tile, n_ctx_rows, dec_seq):
    ctx_tiles = n_ctx_rows // tile
    per = dec_seq // tile
    return jnp.where(i < ctx_tiles, 0, 1 + (i - ctx_tiles) // per)


def _in_kernel(x_ref, mod_ref, g_ref, w_ref, o_ref, h_scr):
    @pl.when(pl.program_id(1) == 0)
    def _():
        m = mod_ref[0]
        h = _rms(x_ref[...], g_ref[...]) * (1.0 + m[1:2]) + m[0:1]
        h_scr[...] = h.astype(BF16)

    o_ref[...] = _dot(h_scr[...], w_ref[...])


def _in_proj(x, mod, g, w_bf, n_ctx_rows, dec_seq):
    t = x.shape[0]
    tm, tn = 1024, 512
    grp = functools.partial(_group_of, tile=tm, n_ctx_rows=n_ctx_rows, dec_seq=dec_seq)
    return pl.pallas_call(
        _in_kernel,
        grid=(t // tm, IN_WIDTH // tn),
        in_specs=[
            pl.BlockSpec((tm, D_MODEL), lambda i, j: (i, 0)),
            pl.BlockSpec((1, 6, D_MODEL), lambda i, j: (grp(i), 0, 0)),
            pl.BlockSpec((1, D_MODEL), lambda i, j: (0, 0)),
            pl.BlockSpec((D_MODEL, tn), lambda i, j: (0, j)),
        ],
        out_specs=pl.BlockSpec((tm, tn), lambda i, j: (i, j)),
        out_shape=jax.ShapeDtypeStruct((t, IN_WIDTH), F32),
        scratch_shapes=[pltpu.VMEM((tm, D_MODEL), BF16)],
        compiler_params=_cparams(("parallel", "arbitrary")),
        name="in_proj",
    )(x, mod, g.reshape(1, D_MODEL), w_bf)


def _ret_kernel(lg_ref, q_ref, k_ref, v_ref, gt_ref, cos_ref, sin_ref, s0_ref,
                o_ref, sfin_ref, acc_scr, q_scr, k_scr, *, seq_len, hb, rope):
    c = RET_CHUNK
    n_chunks = seq_len // c
    ii = lax.broadcasted_iota(jnp.int32, (c, c), 0)
    jj = lax.broadcasted_iota(jnp.int32, (c, c), 1)
    rel = (ii - jj).astype(F32)
    ci = lax.broadcasted_iota(jnp.int32, (c, 1), 0).astype(F32)
    one = jnp.ones((1, 1), F32)
    tdot = functools.partial(lax.dot_general, preferred_element_type=F32)

    for hh in range(hb):
        head = pl.program_id(1) * hb + hh
        lgf = lg_ref[0, head]
        lgb = lg_ref[1, head]
        dmask = (jnp.where(rel >= 0, jnp.exp(lgf * jnp.maximum(rel, 0.0)), 0.0)
                 + jnp.where(rel <= 0, jnp.exp(lgb * jnp.maximum(-rel, 0.0)), 0.0))
        qd_f = jnp.exp(lgf * (ci + 1.0))
        kd_f = jnp.exp(lgf * (c - 1.0 - ci))
        cd_f = jnp.exp(lgf * c * one)
        qd_b = jnp.exp(lgb * (c - ci))
        kd_b = jnp.exp(lgb * ci)
        cd_b = jnp.exp(lgb * c * one)
        lanes = slice(hh * LANES, (hh + 1) * LANES)

        def rows_of(n):
            if isinstance(n, int):
                return slice(n * c, (n + 1) * c)
            return pl.ds(pl.multiple_of(n * c, c), c)

        def fwd_chunk(n, s_f):
            rows = rows_of(n)
            q = q_ref[rows, lanes]
            k = k_ref[rows, lanes] * (RET_DK ** -0.5)
            if rope:
                cs = cos_ref[rows, :]
                sn = sin_ref[rows, :]
                q = q * cs + pltpu.roll(q, RET_DK // 2, 1) * sn
                k = k * cs + pltpu.roll(k, RET_DK // 2, 1) * sn
            qb = q.astype(BF16)
            vb = v_ref[rows, lanes].astype(BF16)
            q_scr[rows, :] = qb
            k_scr[rows, :] = k
            scores = tdot(qb, k.astype(BF16), (((1,), (1,)), ((), ()))) * dmask
            inner = _dot(scores.astype(BF16), vb)
            cross = _dot(qb, s_f.astype(BF16)) * qd_f
            acc_scr[rows, :] = inner + cross
            upd = tdot((k * kd_f).astype(BF16), vb, (((0,), (0,)), ((), ())))
            return s_f * cd_f + upd

        def bwd_chunk(m, s_b):
            n = n_chunks - 1 - m
            rows = rows_of(n)
            qb = q_scr[rows, :]
            k = k_scr[rows, :]
            vb = v_ref[rows, lanes].astype(BF16)
            o = acc_scr[rows, :] + _dot(qb, s_b.astype(BF16)) * qd_b
            mu = jnp.mean(o, axis=-1, keepdims=True)
            oc = o - mu
            var = jnp.mean(oc * oc, axis=-1, keepdims=True)
            o = oc * lax.rsqrt(var + EPS)
            g = gt_ref[rows, lanes]
            o_ref[rows, lanes] = (g * jax.nn.sigmoid(g) * o).astype(o_ref.dtype)
            upd = tdot((k * kd_b).astype(BF16), vb, (((0,), (0,)), ((), ())))
            return s_b * cd_b + upd

        s_f = s0_ref[0, 0, hh]
        s_b = s0_ref[0, 1, hh]
        if n_chunks <= 4:
            for n in range(n_chunks):
                s_f = fwd_chunk(n, s_f)
            for m in range(n_chunks):
                s_b = bwd_chunk(m, s_b)
        else:
            s_f = lax.fori_loop(0, n_chunks, fwd_chunk, s_f)
            s_b = lax.fori_loop(0, n_chunks, bwd_chunk, s_b)
        sfin_ref[0, 0, hh] = s_f
        sfin_ref[0, 1, hh] = s_b


def _into(kernel_fn, n_in, dst):
    if dst is None:
        return kernel_fn, [], [], {}

    def body(*refs):
        return kernel_fn(*refs[:n_in], *refs[n_in + 1:])

    return body, [pl.BlockSpec(memory_space=pl.ANY)], [dst], {n_in: 0}


def _retention(proj, log_gamma, s0, cos2, sin2, *, row0, n_seq, seq_len, hb, rope, dst=None):
    blk0 = row0 // seq_len
    body, dst_spec, dst_arg, alias = _into(
        functools.partial(_ret_kernel, seq_len=seq_len, hb=hb, rope=rope), 8, dst)
    w = hb * LANES
    hblocks = RET_HEADS // hb
    col = lambda part: (lambda b, h, lg: (blk0 + b, part * hblocks + h))
    grid_spec = pltpu.PrefetchScalarGridSpec(
        num_scalar_prefetch=1,
        grid=(n_seq, hblocks),
        in_specs=[
            pl.BlockSpec((seq_len, w), col(0)),
            pl.BlockSpec((seq_len, w), col(1)),
            pl.BlockSpec((seq_len, w), col(2)),
            pl.BlockSpec((seq_len, w), col(3)),
            pl.BlockSpec((seq_len, LANES), lambda b, h, lg: (0, 0)),
            pl.BlockSpec((seq_len, LANES), lambda b, h, lg: (0, 0)),
            pl.BlockSpec((1, 2, hb, RET_DK, RET_DV), lambda b, h, lg: (b, 0, h, 0, 0)),
        ] + dst_spec,
        out_specs=[
            pl.BlockSpec((seq_len, w), lambda b, h, lg: (blk0 + b, h)),
            pl.BlockSpec((1, 2, hb, RET_DK, RET_DV), lambda b, h, lg: (b, 0, h, 0, 0)),
        ],
        scratch_shapes=[
            pltpu.VMEM((seq_len, LANES), F32),
            pltpu.VMEM((seq_len, LANES), BF16),
            pltpu.VMEM((seq_len, LANES), F32),
        ],
    )
    return pl.pallas_call(
        body,
        grid_spec=grid_spec,
        out_shape=[
            jax.ShapeDtypeStruct((proj.shape[0], RET_WIDTH), BF16),
            jax.ShapeDtypeStruct((n_seq, 2, RET_HEADS, RET_DK, RET_DV), F32),
        ],
        input_output_aliases=alias,
        compiler_params=_cparams(("parallel", "arbitrary")),
        name="retention",
    )(log_gamma, proj, proj, proj, proj, cos2, sin2, s0, *dst_arg)


def _rope_tables(seq_len):
    rows_n = seq_len // GRID_W
    rows = jnp.repeat(jnp.arange(rows_n, dtype=F32), GRID_W)
    cols = jnp.tile(jnp.arange(GRID_W, dtype=F32), rows_n)
    nf = RET_DK // 4
    inv = ROPE_BASE ** (-jnp.arange(nf, dtype=F32) / nf)
    ang = jnp.concatenate([rows[:, None] * inv, cols[:, None] * inv], axis=-1)
    cs, sn = jnp.cos(ang), jnp.sin(ang)
    return jnp.concatenate([cs, cs], axis=-1), jnp.concatenate([-sn, sn], axis=-1)


S5_GB = LANES // S5_GROUP
S5_W = S5_Q * LANES
S5_SPLIT = 4
S5_SW = S5_GB * S5_STATE
S5_SB = S5_SW // LANES


def _s5_kernel(u_ref, bq_ref, t_ref, cq_ref, ar_ref, ai_ref, d_ref, h0_ref,
               y_ref, hf_ref, ub_scr, sm_scr, hp_scr, hpb_scr, *, n_seq, n_chunks):
    s = pl.program_id(1)
    m = n_seq * n_chunks
    q = S5_Q
    nblk = S5_W // LANES
    sb = S5_SB

    per = q // S5_SPLIT

    @pl.when(s == 0)
    def _():
        for j in range(q):
            ub_scr[j // per, :, (j % per) * LANES:(j % per + 1) * LANES] = (
                u_ref[pl.ds(j, m, stride=q), :].astype(BF16))

    @pl.when(s < S5_SPLIT)
    def _():
        part = _dot(ub_scr[jnp.minimum(s, S5_SPLIT - 1)], bq_ref[0])

        @pl.when(s == 0)
        def _():
            for cb in range(nblk):
                sm_scr[cb] = part[:, cb * LANES:(cb + 1) * LANES]

        @pl.when(s > 0)
        def _():
            for cb in range(nblk):
                sm_scr[cb] += part[:, cb * LANES:(cb + 1) * LANES]

    @pl.when(s == S5_SPLIT - 1)
    def _():
        ar = ar_ref[0]
        ai = ai_ref[0]
        h0 = h0_ref[0]
        blk = lambda a, cb: a[:, cb * LANES:(cb + 1) * LANES]

        def body(n, carry):
            rows_f = pl.ds(n, n_seq, stride=n_chunks)
            rows_b = pl.ds(n_chunks - 1 - n, n_seq, stride=n_chunks)
            new = list(carry)
            for d, rows in ((0, rows_f), (1, rows_b)):
                for c in range(sb):
                    re_i = d * sb + c
                    im_i = (2 + d) * sb + c
                    hr, hi = carry[re_i], carry[im_i]
                    hp_scr[re_i, rows, :] = hr
                    hp_scr[im_i, rows, :] = hi
                    a_r, a_i = blk(ar, re_i), blk(ai, re_i)
                    new[re_i] = a_r * hr - a_i * hi + sm_scr[re_i, rows, :]
                    new[im_i] = a_r * hi + a_i * hr + sm_scr[im_i, rows, :]
            return tuple(new)

        fin = lax.fori_loop(0, n_chunks, body, tuple(blk(h0, cb) for cb in range(nblk)))
        hf_ref[0] = jnp.concatenate(fin, axis=1)
        for cb in range(nblk):
            hpb_scr[:, cb * LANES:(cb + 1) * LANES] = hp_scr[cb].astype(BF16)

    @pl.when(s >= S5_SPLIT)
    def _():
        ub = jnp.concatenate([ub_scr[k] for k in range(S5_SPLIT)], axis=1)
        y = _dot(ub, t_ref[0]) + _dot(hpb_scr[...], cq_ref[0])
        dd = d_ref[0]
        for ii in range(per):
            rows = pl.ds((s - S5_SPLIT) * per + ii, m, stride=q)
            yi = y[:, ii * LANES:(ii + 1) * LANES] + dd * u_ref[rows, :]
            y_ref[rows, :] = jax.nn.gelu(yi)


def _s5_expand_kernel(mc_ref, o_ref, *, xsize, ysize):
    xs, ys, gs = xsize.bit_length() - 1, ysize.bit_length() - 1, S5_GB.bit_length() - 1
    assert xsize == 1 << xs and ysize == 1 << ys and S5_GB == 1 << gs
    cw = o_ref.shape[2]
    nc = mc_ref.shape[2]
    col0 = pl.program_id(1) * cw
    r = lax.broadcasted_iota(jnp.int32, (nc, cw), 0)
    col = lax.broadcasted_iota(jnp.int32, (nc, cw), 1) + col0
    spread = jnp.logical_and(r >> ys == col >> (ys + gs), (r & (ysize - 1)) == (col & (ysize - 1)))
    big = _dot(mc_ref[0], jnp.where(spread, 1.0, 0.0).astype(BF16))
    row = lax.broadcasted_iota(jnp.int32, big.shape, 0)
    colb = lax.broadcasted_iota(jnp.int32, big.shape, 1) + col0
    same = ((row >> xs) & (S5_GB - 1)) == ((colb >> ys) & (S5_GB - 1))
    o_ref[0] = jnp.where(same, big, 0.0).astype(BF16)


def _s5_expand(mc, *, xsize, ysize):
    nb, rows, nc = mc.shape
    cols = nc * S5_GB
    cw = 512
    return pl.pallas_call(
        functools.partial(_s5_expand_kernel, xsize=xsize, ysize=ysize),
        grid=(nb, cols // cw),
        in_specs=[pl.BlockSpec((1, rows, nc), lambda b, j: (b, 0, 0))],
        out_specs=pl.BlockSpec((1, rows, cw), lambda b, j: (b, 0, j)),
        out_shape=jax.ShapeDtypeStruct((nb, rows, cols), BF16),
        compiler_params=_cparams(("parallel", "parallel")),
        name="s5_expand",
    )(mc)


def _s5_mats(lam_re, lam_im, log_dt, b_re, b_im, c_re, c_im, d):
    q, g, p, ch = S5_Q, S5_GROUPS, S5_STATE, S5_GROUP
    lam = lax.complex(jnp.minimum(lam_re.astype(F32), -1e-4), lam_im.astype(F32))
    ldt = lam * jnp.exp(log_dt.astype(F32))[..., None]
    lam_bar = jnp.exp(ldt)
    b_bar = ((lam_bar - 1.0) / lam)[..., None] * lax.complex(b_re.astype(F32), b_im.astype(F32))
    cc = lax.complex(c_re.astype(F32), c_im.astype(F32))
    pw = jnp.exp(ldt[..., None] * jnp.arange(q + 1, dtype=F32))
    hi = lax.Precision.HIGHEST
    kern = jnp.real(jnp.einsum('zgcp,zgpd,zgpe->zgdce', cc, pw[..., :q], b_bar, precision=hi))
    i = jnp.arange(q)[None, :]
    j = jnp.arange(q)[:, None]
    dif = i - j
    kf = jnp.where((dif >= 0)[None, :, :, None, None], kern[0][:, jnp.maximum(dif, 0)], 0.0)
    kb = jnp.where((dif <= 0)[None, :, :, None, None], kern[1][:, jnp.maximum(-dif, 0)], 0.0)
    tm = (kf + kb).transpose(0, 1, 4, 2, 3).reshape(g, q * ch, q * ch)

    pw_dn = jnp.exp(ldt[..., None] * (q - jnp.arange(q + 1, dtype=F32)))
    bf = pw_dn[0][..., 1:][:, :, :, None] * b_bar[0][:, :, None, :]
    bb = pw[1][..., :q][:, :, :, None] * b_bar[1][:, :, None, :]
    to_rows = lambda m: m.transpose(0, 2, 3, 1).reshape(g, q * ch, p)
    bq = jnp.concatenate([to_rows(jnp.real(bf)), to_rows(jnp.real(bb)),
                          to_rows(jnp.imag(bf)), to_rows(jnp.imag(bb))], axis=-1)

    cf = cc[0].transpose(0, 2, 1)[:, :, None, :] * pw[0][..., 1:][:, :, :, None]
    cb = cc[1].transpose(0, 2, 1)[:, :, None, :] * pw_dn[1][..., :q][:, :, :, None]
    to_cols = lambda m: m.reshape(g, p, q * ch)
    cq = jnp.concatenate([to_cols(jnp.real(cf)), to_cols(jnp.real(cb)),
                          to_cols(-jnp.imag(cf)), to_cols(-jnp.imag(cb))], axis=1)

    gb, nb = S5_GB, g // S5_GB
    rows_of = lambda a, outer, inner: (a.reshape(nb, gb, outer, inner, a.shape[-1]).transpose(0, 2, 1, 3, 4)
                                       .reshape(nb, outer * gb * inner, a.shape[-1]).astype(BF16))
    tm_bd = _s5_expand(rows_of(tm, q, ch), xsize=ch, ysize=ch)
    bq_bd = _s5_expand(rows_of(bq, q, ch), xsize=ch, ysize=p)
    cq_bd = _s5_expand(rows_of(cq, 4, p), xsize=p, ysize=ch)

    lq = pw[..., q].reshape(2, nb, 1, S5_SW)
    ar = jnp.concatenate([jnp.real(lq[0]), jnp.real(lq[1])], axis=-1)
    ai = jnp.concatenate([jnp.imag(lq[0]), jnp.imag(lq[1])], axis=-1)
    dd = d.astype(F32).reshape(nb, 1, LANES)
    return tm_bd, bq_bd, cq_bd, ar, ai, dd


def _s5(proj, mats, h0_re, h0_im, *, row0, n_seq, seq_len, dst=None):
    q, p = S5_Q, S5_STATE
    n_chunks = seq_len // q
    m = n_seq * n_chunks
    rows = n_seq * seq_len
    nb = S5_GROUPS // S5_GB
    nblk = S5_W // LANES
    kw = S5_W // S5_SPLIT
    tm, bq, cq, ar, ai, dd = mats
    part = lambda a: a.astype(F32).reshape(n_seq, nb, S5_SW)
    h0 = jnp.concatenate([part(h0_re[:, 0]), part(h0_re[:, 1]), part(h0_im[:, 0]), part(h0_im[:, 1])],
                         axis=-1).transpose(1, 0, 2)
    per_b = lambda shape: pl.BlockSpec((1,) + shape, lambda b, s: (b, 0, 0))
    body, dst_spec, dst_arg, alias = _into(functools.partial(_s5_kernel, n_seq=n_seq, n_chunks=n_chunks), 8, dst)
    y, hf = pl.pallas_call(
        body,
        grid=(nb, 2 * S5_SPLIT),
        in_specs=[
            pl.BlockSpec((rows, LANES), lambda b, s: (row0 // rows, U_COL // LANES + b)),
            pl.BlockSpec((1, kw, 4 * S5_SW), lambda b, s: (b, jnp.minimum(s, S5_SPLIT - 1), 0)),
            pl.BlockSpec((1, S5_W, kw), lambda b, s: (b, 0, jnp.maximum(s - S5_SPLIT, 0))),
            pl.BlockSpec((1, 4 * S5_SW, kw), lambda b, s: (b, 0, jnp.maximum(s - S5_SPLIT, 0))),
            per_b((1, 2 * S5_SW)), per_b((1, 2 * S5_SW)), per_b((1, LANES)), per_b((n_seq, 4 * S5_SW)),
        ] + dst_spec,
        out_specs=[pl.BlockSpec((rows, LANES), lambda b, s: (row0 // rows, b)), per_b((n_seq, 4 * S5_SW))],
        out_shape=[jax.ShapeDtypeStruct((proj.shape[0], S5_WIDTH), F32),
                   jax.ShapeDtypeStruct((nb, n_seq, 4 * S5_SW), F32)],
        scratch_shapes=[pltpu.VMEM((S5_SPLIT, m, kw), BF16), pltpu.VMEM((nblk, m, LANES), F32),
                        pltpu.VMEM((nblk, m, LANES), F32), pltpu.VMEM((m, 4 * S5_SW), BF16)],
        input_output_aliases=alias,
        compiler_params=_cparams(("parallel", "arbitrary")),
        name="s5",
    )(proj, bq, tm, cq, ar, ai, dd, h0, *dst_arg)
    hf = hf.reshape(nb, n_seq, 4, S5_GB, p).transpose(1, 2, 0, 3, 4).reshape(n_seq, 4, S5_GROUPS, p)
    return y, hf[:, 0:2], hf[:, 2:4]


def _conv3(x, w, b):
    n = x.shape[0]
    row = lax.broadcasted_iota(jnp.int32, x.shape, 0)
    prev = jnp.where(row == 0, 0.0, pltpu.roll(x, 1, 0))
    nxt = jnp.where(row == n - 1, 0.0, pltpu.roll(x, n - 1, 0))
    return prev * w[0:1] + x * w[1:2] + nxt * w[2:3] + b


def _hy_fwd_kernel(x0_ref, x1_ref, v_ref, w0_ref, w1_ref, wv_ref, b0_ref, b1_ref, bv_ref,
                   fc_ref, fs_ref, m1_ref, m2_ref, m3_ref, p_ref, z_ref, x0c_ref, zb_scr):
    @pl.when(pl.program_id(2) == 0)
    def _():
        z = _conv3(x1_ref[...], w1_ref[...], b1_ref[...]) * _conv3(v_ref[...], wv_ref[...], bv_ref[...])
        z_ref[...] = z
        zb_scr[...] = z.astype(BF16)
        x0c_ref[...] = _conv3(x0_ref[...], w0_ref[...], b0_ref[...])

    zb = zb_scr[...]
    a = _dot(fc_ref[...], zb)
    b = _dot(fs_ref[...], zb)
    m2 = m2_ref[...]
    p_ref[0, 0] = (m1_ref[...] * a + m2 * b).astype(BF16)
    p_ref[0, 1] = (m3_ref[...] * b - m2 * a).astype(BF16)


def _hy_inv_kernel(p_ref, gc_ref, gs_ref, z_ref, x0c_ref, bias_ref, o_ref):
    conv = _dot(gc_ref[...], p_ref[0, 0]) + _dot(gs_ref[...], p_ref[0, 1])
    o_ref[...] = (x0c_ref[...] * (conv + bias_ref[...] * z_ref[...])).astype(o_ref.dtype)


def _dft_mats(seq_len):
    k = jnp.arange(seq_len, dtype=jnp.int32)
    kj = (k[:, None] * k[None, :]) % (2 * seq_len)
    ang = kj.astype(F32) * (math.pi / seq_len)
    cm = jnp.cos(ang)
    sm = -jnp.sin(ang)
    nyq = jnp.where(k % 2 == 0, 1.0, -1.0).astype(F32)
    sm = sm.at[0, :].set(nyq)
    return cm.astype(BF16), sm.astype(BF16), sm.T.astype(BF16)


def _hy_filter_mults(seq_len, f1_w, f1_b, f2_w, f2_b, f3_w, f3_b, freq, decay):
    n = seq_len
    t = (jnp.arange(n, dtype=F32) / n)[:, None]
    bands = jnp.arange(1, HY_BANDS + 1, dtype=F32)[None, :]
    z = jnp.concatenate([t, jnp.cos(2.0 * math.pi * t * bands), jnp.sin(2.0 * math.pi * t * bands)], axis=-1)
    hi = lax.Precision.HIGHEST
    fr = freq.astype(F32)
    h = jnp.sin(fr * (jnp.dot(z, f1_w.astype(F32), precision=hi) + f1_b.astype(F32)))
    h = jnp.sin(fr * (jnp.dot(h, f2_w.astype(F32), precision=hi) + f2_b.astype(F32)))
    h = jnp.dot(h, f3_w.astype(F32), precision=hi) + f3_b.astype(F32)
    h = h * jnp.exp(-t * jnp.abs(decay.astype(F32)))
    h = h.reshape(n, 2, HY_WIDTH)
    h = h / jnp.sum(jnp.abs(h), axis=(0, 1), keepdims=True)
    hc = jnp.concatenate([h[:, 0], h[::-1, 1]], axis=0)
    spec = jnp.fft.rfft(hc, axis=0)
    hr, him = jnp.real(spec), jnp.imag(spec)
    wk = jnp.where(jnp.arange(n) == 0, 1.0, 2.0)[:, None] / (2.0 * n)
    first = (jnp.arange(n) == 0)[:, None]
    m1 = hr[:n] * wk
    m2 = jnp.where(first, 0.0, -him[:n]) * wk
    m3 = jnp.where(first, hr[n:n + 1], hr[:n]) * wk
    return m1.astype(F32), m2.astype(F32), m3.astype(F32)


def _hyena(proj, conv_w, conv_b, bias, dft, mults, *, row0, n_seq, seq_len, cb, tk, dst=None):
    blk0 = row0 // seq_len
    nc = HY_WIDTH // cb
    nk = seq_len // tk
    c0 = HY_COL // cb
    cm, sm, smt = dft
    m1, m2, m3 = mults
    xcol = lambda part: pl.BlockSpec((seq_len, cb), lambda b, c, k: (blk0 + b, c0 + part * nc + c))
    wcol = lambda part: pl.BlockSpec((3, cb), lambda b, c, k: (0, part * nc + c))
    bcol = lambda part: pl.BlockSpec((1, cb), lambda b, c, k: (0, part * nc + c))
    frow = pl.BlockSpec((tk, seq_len), lambda b, c, k: (k, 0))
    mblk = pl.BlockSpec((tk, cb), lambda b, c, k: (k, c))
    cb2 = conv_b.reshape(1, 3 * HY_WIDTH)
    pspec, z, x0c = pl.pallas_call(
        _hy_fwd_kernel,
        grid=(n_seq, nc, nk),
        in_specs=[xcol(0), xcol(1), xcol(2), wcol(0), wcol(1), wcol(2), bcol(0), bcol(1), bcol(2),
                  frow, frow, mblk, mblk, mblk],
        out_specs=[
            pl.BlockSpec((1, 2, tk, cb), lambda b, c, k: (b, 0, k, c)),
            pl.BlockSpec((seq_len, cb), lambda b, c, k: (b, c)),
            pl.BlockSpec((seq_len, cb), lambda b, c, k: (b, c)),
        ],
        out_shape=[
            jax.ShapeDtypeStruct((n_seq, 2, seq_len, HY_WIDTH), BF16),
            jax.ShapeDtypeStruct((n_seq * seq_len, HY_WIDTH), F32),
            jax.ShapeDtypeStruct((n_seq * seq_len, HY_WIDTH), F32),
        ],
        scratch_shapes=[pltpu.VMEM((seq_len, cb), BF16)],
        compiler_params=_cparams(("parallel", "parallel", "arbitrary")),
        name="hyena_fwd",
    )(proj, proj, proj, conv_w, conv_w, conv_w, cb2, cb2, cb2, cm, sm, m1, m2, m3)
    grow = pl.BlockSpec((tk, seq_len), lambda b, c, k: (k, 0))
    tile = pl.BlockSpec((tk, cb), lambda b, c, k: (b * nk + k, c))
    body, dst_spec, dst_arg, alias = _into(_hy_inv_kernel, 6, dst)
    return pl.pallas_call(
        body,
        grid=(n_seq, nc, nk),
        in_specs=[pl.BlockSpec((1, 2, seq_len, cb), lambda b, c, k: (b, 0, 0, c)),
                  grow, grow, tile, tile, pl.BlockSpec((1, cb), lambda b, c, k: (0, c))] + dst_spec,
        out_specs=pl.BlockSpec((tk, cb), lambda b, c, k: (row0 // tk + b * nk + k, c)),
        out_shape=jax.ShapeDtypeStruct((proj.shape[0], HY_WIDTH), BF16),
        input_output_aliases=alias,
        compiler_params=_cparams(("parallel", "parallel", "arbitrary")),
        name="hyena_inv",
    )(pspec, cm, smt, z, x0c, bias.reshape(1, HY_WIDTH), *dst_arg)


def _out_kernel(x_ref, ret_ref, s5_ref, hy_ref, mod_ref, g_ref, gw_ref, gb_ref,
                wr_ref, ws_ref, wh_ref, rt_ref, xo_ref, h_ref, lg_ref):
    m = mod_ref[0]
    y = s5_ref[...]
    s5o = y * jax.nn.sigmoid(_dot(y.astype(BF16), gw_ref[...]) + gb_ref[...])
    mix = (_dot(ret_ref[...], wr_ref[...]) + _dot(s5o.astype(BF16), ws_ref[...])
           + _dot(hy_ref[...], wh_ref[...]))
    x = x_ref[...] + m[2:3] * mix
    xo_ref[...] = x
    h = _rms(x, g_ref[...]) * (1.0 + m[4:5]) + m[3:4]
    hb = h.astype(BF16)
    _store_token_tiles(h_ref, _pack_halves(hb))
    lg_ref[...] = lax.dot_general(rt_ref[...], hb, (((1,), (1,)), ((), ())), preferred_element_type=F32)


def _out_proj(x, ret_o, s5_y, hy_o, mod, g, glu_w, glu_b, w_out, router, n_ctx_rows, dec_seq):
    t = x.shape[0]
    tm = ROW_TILE
    grp = functools.partial(_group_of, tile=tm, n_ctx_rows=n_ctx_rows, dec_seq=dec_seq)
    row = lambda w: pl.BlockSpec((tm, w), lambda i: (i, 0))
    full = lambda a, b: pl.BlockSpec((a, b), lambda i: (0, 0))
    wo = w_out.astype(BF16)
    return pl.pallas_call(
        _out_kernel,
        grid=(t // tm,),
        in_specs=[row(D_MODEL), row(RET_WIDTH), row(S5_WIDTH), row(HY_WIDTH),
                  pl.BlockSpec((1, 6, D_MODEL), lambda i: (grp(i), 0, 0)),
                  full(1, D_MODEL), full(S5_WIDTH, S5_WIDTH), full(1, S5_WIDTH),
                  full(RET_WIDTH, D_MODEL), full(S5_WIDTH, D_MODEL), full(HY_WIDTH, D_MODEL),
                  full(N_EXPERTS, D_MODEL)],
        out_specs=[row(D_MODEL), pl.BlockSpec((tm * TOKEN_ROWS, LANES), lambda i: (i, 0)),
                   pl.BlockSpec((N_EXPERTS, tm), lambda i: (0, i))],
        out_shape=[jax.ShapeDtypeStruct((t, D_MODEL), F32), jax.ShapeDtypeStruct((t * TOKEN_ROWS, LANES), jnp.uint32),
                   jax.ShapeDtypeStruct((N_EXPERTS, t), F32)],
        compiler_params=_cparams(("parallel",)),
        name="out_proj",
    )(x, ret_o, s5_y, hy_o, mod, g.reshape(1, D_MODEL), glu_w.astype(BF16), glu_b.reshape(1, S5_WIDTH),
      wo[:RET_WIDTH], wo[RET_WIDTH:RET_WIDTH + S5_WIDTH], wo[RET_WIDTH + S5_WIDTH:], router.T.astype(BF16))


def _moe_kernel(be_ref, first_ref, slot_ref, nxt_ref, nu_ref, xs_ref, wg_hbm, wu_hbm, wd_hbm, o_ref,
                wg_f, wu_f, wd_f, wg_b, wu_b, wd_b, sem, *, layer):
    i = pl.program_id(0)

    def copies(e, s):
        return (pltpu.make_async_copy(wg_hbm.at[layer, e], wg_f.at[s], sem.at[s, 0]),
                pltpu.make_async_copy(wu_hbm.at[layer, e], wu_f.at[s], sem.at[s, 1]),
                pltpu.make_async_copy(wd_hbm.at[layer, e], wd_f.at[s], sem.at[s, 2]))

    @pl.when(i == 0)
    def _():
        for cp in copies(be_ref[0], 0):
            cp.start()

    @pl.when(first_ref[i] == 1)
    def _():
        s = slot_ref[i]
        for cp in copies(be_ref[i], s):
            cp.wait()

        @pl.when(nxt_ref[i] >= 0)
        def _():
            for cp in copies(nxt_ref[i], 1 - s):
                cp.start()

        wg_b[...] = wg_f[s].astype(BF16)
        wu_b[...] = wu_f[s].astype(BF16)
        wd_b[...] = wd_f[s].astype(BF16)

    @pl.when(i < nu_ref[0])
    def _():
        half = D_MODEL // 2
        x_lo, x_hi = _unpack_halves(_load_token_tiles(xs_ref, MOE_BM))
        x_lo = x_lo.astype(BF16)
        x_hi = x_hi.astype(BF16)
        gate = _dot(x_lo, wg_b[0:half, :]) + _dot(x_hi, wg_b[half:, :])
        up = _dot(x_lo, wu_b[0:half, :]) + _dot(x_hi, wu_b[half:, :])
        hb = gate * jax.nn.sigmoid(gate) * up
        _store_token_tiles(o_ref, _pack_halves(_dot(hb.astype(BF16), wd_b[...]).astype(BF16)))

    @pl.when(i >= nu_ref[0])
    def _():
        o_ref[...] = jnp.zeros_like(o_ref)


def _moe_grouped(xs, blk_e, first, slot, nxt, n_used, w_gate, w_up, w_down, layer):
    pr = xs.shape[0] // TOKEN_ROWS
    bm = MOE_BM
    nb = pr // bm
    grid_spec = pltpu.PrefetchScalarGridSpec(
        num_scalar_prefetch=5,
        grid=(nb,),
        in_specs=[
            pl.BlockSpec((bm * TOKEN_ROWS, LANES), lambda i, *_: (i, 0)),
            pl.BlockSpec(memory_space=pl.ANY),
            pl.BlockSpec(memory_space=pl.ANY),
            pl.BlockSpec(memory_space=pl.ANY),
        ],
        out_specs=pl.BlockSpec((bm * TOKEN_ROWS, LANES), lambda i, *_: (i, 0)),
        scratch_shapes=[pltpu.VMEM((2, D_MODEL, D_EXPERT), F32), pltpu.VMEM((2, D_MODEL, D_EXPERT), F32),
                        pltpu.VMEM((2, D_EXPERT, D_MODEL), F32),
                        pltpu.VMEM((D_MODEL, D_EXPERT), BF16), pltpu.VMEM((D_MODEL, D_EXPERT), BF16),
                        pltpu.VMEM((D_EXPERT, D_MODEL), BF16),
                        pltpu.SemaphoreType.DMA((2, 3))],
    )
    return pl.pallas_call(
        functools.partial(_moe_kernel, layer=layer),
        grid_spec=grid_spec,
        out_shape=jax.ShapeDtypeStruct((pr * TOKEN_ROWS, LANES), jnp.uint32),
        compiler_params=_cparams(("arbitrary",)),
        name="moe_grouped",
    )(blk_e, first, slot, nxt, n_used, xs, w_gate, w_up, w_down)


GATHER_CHUNK = 512


def _gather_kernel(idx_ref, src, dst, sem):
    ch = GATHER_CHUNK
    tr = TOKEN_ROWS
    i = pl.program_id(0)
    last = pl.num_programs(0) - 1

    def issue(r, carry):
        from_row = pl.multiple_of(idx_ref[0, 0, r] * tr, tr)
        to_row = pl.multiple_of((i * ch + r) * tr, tr)
        pltpu.make_async_copy(src.at[pl.ds(from_row, tr)], dst.at[pl.ds(to_row, tr)], sem.at[i % 2]).start()
        return carry

    def drain(step):
        rows = pl.ds(pl.multiple_of(step * ch * tr, tr), ch * tr)
        pltpu.make_async_copy(src.at[pl.ds(0, ch * tr)], dst.at[rows], sem.at[step % 2]).wait()

    lax.fori_loop(0, ch, issue, 0, unroll=8)

    @pl.when(i > 0)
    def _():
        drain(i - 1)

    @pl.when(i == last)
    def _():
        drain(i)


def _gather_rows(src, idx):
    n = idx.shape[0]
    ch = GATHER_CHUNK
    return pl.pallas_call(
        _gather_kernel,
        grid=(n // ch,),
        in_specs=[pl.BlockSpec((1, 1, ch), lambda i: (i, 0, 0), memory_space=pltpu.SMEM),
                  pl.BlockSpec(memory_space=pl.ANY)],
        out_specs=pl.BlockSpec(memory_space=pl.ANY),
        out_shape=jax.ShapeDtypeStruct((n * TOKEN_ROWS, LANES), src.dtype),
        scratch_shapes=[pltpu.SemaphoreType.DMA((2,))],
        compiler_params=pltpu.CompilerParams(dimension_semantics=("arbitrary",)),
        name="gather_rows",
    )(idx.reshape(n // ch, 1, ch), src)


ROUTE_TILE = 512


def _router_kernel(lg_ref, bias_ref, pos_ref, gate_ref, be_ref, info_ref, rank_scr, ek_scr, *, t, nbp):
    tl = ROUTE_TILE
    ne = N_EXPERTS
    bm = MOE_BM
    row = lax.broadcasted_iota(jnp.int32, (ne, tl), 0)
    tri = (lax.broadcasted_iota(jnp.int32, (tl, tl), 0) < lax.broadcasted_iota(jnp.int32, (tl, tl), 1)).astype(BF16)
    bias = bias_ref[...]

    def select(i, counts):
        cols = pl.ds(pl.multiple_of(i * tl, tl), tl)
        s = jax.nn.sigmoid(lg_ref[:, cols])
        sel = s + bias
        mask = jnp.zeros((ne, tl), F32)
        vals = []
        for k in range(TOP_K):
            best = jnp.max(sel, axis=0, keepdims=True)
            idx = jnp.min(jnp.where(sel == best, row, ne), axis=0, keepdims=True)
            hit = row == idx
            vals.append(jnp.sum(jnp.where(hit, s, 0.0), axis=0, keepdims=True))
            sel = jnp.where(hit, -jnp.inf, sel)
            mask = jnp.where(hit, 1.0, mask)
            ek_scr[k:k + 1, cols] = idx
        total = vals[0]
        for v in vals[1:]:
            total = total + v
        scale = ROUTED_SCALE / total
        for k in range(TOP_K):
            gate_ref[k:k + 1, cols] = vals[k] * scale
        gate_ref[TOP_K:SUBLANES, cols] = jnp.zeros((SUBLANES - TOP_K, tl), F32)
        rank_scr[:, cols] = _dot(mask.astype(BF16), tri) + counts
        return counts + jnp.sum(mask, axis=1, keepdims=True)

    counts = lax.fori_loop(0, t // tl, select, jnp.zeros((ne, 1), F32))
    counts = counts.astype(jnp.int32)
    shift = bm.bit_length() - 1
    assert bm == 1 << shift
    padded = ((counts + (bm - 1)) >> shift) << shift
    e0 = lax.broadcasted_iota(jnp.int32, (ne, ne), 0)
    e1 = lax.broadcasted_iota(jnp.int32, (ne, ne), 1)
    padded_row = jnp.sum(jnp.where(e0 == e1, padded, 0), axis=0, keepdims=True)
    counts_row = jnp.sum(jnp.where(e0 == e1, counts, 0), axis=0, keepdims=True)
    pstart = jnp.sum(jnp.where(e1 < e0, padded_row, 0), axis=1, keepdims=True)
    ustart = jnp.sum(jnp.where(e1 < e0, counts_row, 0), axis=1, keepdims=True)
    pend = pstart + padded
    lane = lax.broadcasted_iota(jnp.int32, (ne, LANES), 1)
    info_ref[...] = jnp.where(lane == 0, counts, jnp.where(lane == 1, pstart, jnp.where(lane == 2, ustart, pend)))
    blk = lax.broadcasted_iota(jnp.int32, (ne, nbp), 1) * bm
    owner = jnp.sum(jnp.where(pend <= blk, 1, 0), axis=0, keepdims=True)
    be_ref[...] = jnp.minimum(owner, ne - 1)
    pstart_f = pstart.astype(F32)

    def place(i, carry):
        cols = pl.ds(pl.multiple_of(i * tl, tl), tl)
        dest = rank_scr[:, cols] + pstart_f
        for k in range(TOP_K):
            hit = row == ek_scr[k:k + 1, cols]
            pos_ref[k:k + 1, cols] = jnp.sum(jnp.where(hit, dest, 0.0), axis=0, keepdims=True).astype(jnp.int32)
        pos_ref[TOP_K:SUBLANES, cols] = jnp.zeros((SUBLANES - TOP_K, tl), jnp.int32)
        return carry

    lax.fori_loop(0, t // tl, place, 0)


def _router(logits_t, router_bias, nb):
    t = logits_t.shape[1]
    nbp = -(-nb // LANES) * LANES
    return pl.pallas_call(
        functools.partial(_router_kernel, t=t, nbp=nbp),
        out_shape=[jax.ShapeDtypeStruct((SUBLANES, t), jnp.int32), jax.ShapeDtypeStruct((SUBLANES, t), F32),
                   jax.ShapeDtypeStruct((1, nbp), jnp.int32), jax.ShapeDtypeStruct((N_EXPERTS, LANES), jnp.int32)],
        scratch_shapes=[pltpu.VMEM((N_EXPERTS, t), F32), pltpu.VMEM((SUBLANES, t), jnp.int32)],
        compiler_params=pltpu.CompilerParams(vmem_limit_bytes=VMEM_LIMIT),
        name="router",
    )(logits_t, router_bias.astype(F32).reshape(N_EXPERTS, 1))


def _dispatch_plan(pos, blk_e_row, info, nb):
    t = pos.shape[1]
    bm = MOE_BM
    counts, pstart, ustart, pend = info[:, 0], info[:, 1], info[:, 2], info[:, 3]
    blk_e = blk_e_row[0, :nb]
    n_used = pend[-1] // bm
    keys = pos[:TOP_K].reshape(-1)
    toks = jnp.tile(jnp.arange(t, dtype=jnp.int32), TOP_K)
    _, sorted_tok = lax.sort_key_val(keys, toks)
    r = jnp.arange(nb * bm, dtype=jnp.int32).reshape(nb, bm) - pstart[blk_e][:, None]
    src = jnp.clip(ustart[blk_e][:, None] + r, 0, t * TOP_K - 1)
    slot_tok = jnp.where(r < counts[blk_e][:, None], sorted_tok[src], 0).reshape(-1)
    blk = jnp.arange(nb, dtype=jnp.int32)
    prev_e = jnp.concatenate([jnp.full((1,), -1, jnp.int32), blk_e[:-1]])
    first = jnp.logical_and(blk < n_used, blk_e != prev_e)
    slot = (jnp.cumsum(first.astype(jnp.int32)) - 1) % 2
    first_at = jnp.where(first, blk, nb)
    nxt_first = lax.cummin(jnp.concatenate([first_at[1:], jnp.full((1,), nb, jnp.int32)]), reverse=True)
    nxt = jnp.where(nxt_first < nb, blk_e[jnp.minimum(nxt_first, nb - 1)], -1)
    return (slot_tok, blk_e, first.astype(jnp.int32), slot.astype(jnp.int32), nxt.astype(jnp.int32),
            n_used.astype(jnp.int32).reshape(1))


def _shared_kernel(x_ref, h_ref, gt_ref, *rest, final):
    e_refs = rest[:TOP_K]
    mod_ref, sg_ref, su_ref, sd_ref, fn_ref, o_ref = rest[TOP_K:]
    m = mod_ref[0]
    half = D_MODEL // 2
    tm = x_ref.shape[0]
    h_lo, h_hi = _unpack_halves(_load_token_tiles(h_ref, tm))
    h_lo = h_lo.astype(BF16)
    h_hi = h_hi.astype(BF16)
    gate = _dot(h_lo, sg_ref[0:half, :]) + _dot(h_hi, sg_ref[half:, :])
    up = _dot(h_lo, su_ref[0:half, :]) + _dot(h_hi, su_ref[half:, :])
    act = gate * jax.nn.sigmoid(gate) * up
    shared = _dot(act.astype(BF16), sd_ref[...])
    gt = gt_ref[...]
    r_lo = jnp.zeros((h_lo.shape[0], half), F32)
    r_hi = r_lo
    for k in range(TOP_K):
        e_lo, e_hi = _unpack_halves(_load_token_tiles(e_refs[k].at[0], tm))
        r_lo = r_lo + gt[:, k:k + 1] * e_lo
        r_hi = r_hi + gt[:, k:k + 1] * e_hi
    routed = jnp.concatenate([r_lo, r_hi], axis=1)
    x = x_ref[...] + m[5:6] * (routed + shared)
    if final:
        x = _rms(x, fn_ref[...])
    o_ref[...] = x


def _shared(x, h, gates_t, picked, mod, sg, su, sd, final_norm, n_ctx_rows, dec_seq, *, final, row0=0, rows=None):
    rows = x.shape[0] if rows is None else rows
    tm = ROW_TILE // 2
    b0 = row0 // tm
    grp = lambda i: _group_of(i + b0, tm, n_ctx_rows, dec_seq)
    row = pl.BlockSpec((tm, D_MODEL), lambda i: (i + b0, 0))
    prow = pl.BlockSpec((tm * TOKEN_ROWS, LANES), lambda i: (i + b0, 0))
    full = lambda a, b: pl.BlockSpec((a, b), lambda i: (0, 0))
    pick = lambda k: pl.BlockSpec((1, tm * TOKEN_ROWS, LANES), lambda i: (k, i + b0, 0))
    return pl.pallas_call(
        functools.partial(_shared_kernel, final=final),
        grid=(rows // tm,),
        in_specs=[row, prow, pl.BlockSpec((tm, SUBLANES), lambda i: (i + b0, 0))]
                 + [pick(k) for k in range(TOP_K)]
                 + [pl.BlockSpec((1, 6, D_MODEL), lambda i: (grp(i), 0, 0)),
                    full(D_MODEL, D_SHARED), full(D_MODEL, D_SHARED), full(D_SHARED, D_MODEL), full(1, D_MODEL)],
        out_specs=pl.BlockSpec((tm, D_MODEL), lambda i: (i, 0)),
        out_shape=jax.ShapeDtypeStruct((rows, D_MODEL), F32),
        compiler_params=_cparams(("parallel",)),
        name="shared_final" if final else "shared",
    )(x, h, gates_t, *([picked] * TOP_K), mod, sg.astype(BF16), su.astype(BF16), sd.astype(BF16),
      final_norm.reshape(1, D_MODEL))


def kernel(x_prompt, x_sample, state_ret, state_s5_re, state_s5_im, c, c_ctx, w_ada, b_ada, norm_mix, norm_ffn, w_in, w_out, ret_decay, s5_lam_re, s5_lam_im, s5_log_dt, s5_b_re, s5_b_im, s5_c_re, s5_c_im, s5_d, s5_glu_w, s5_glu_b, hy_conv_w, hy_conv_b, hy_f1_w, hy_f1_b, hy_f2_w, hy_f2_b, hy_f3_w, hy_f3_b, hy_freq, hy_decay, hy_bias, moe_router, moe_router_bias, moe_w_gate, moe_w_up, moe_w_down, sh_w_gate, sh_w_up, sh_w_down, final_norm):
    n_ctx, seq, d = x_prompt.shape
    n_dec, dec_seq, _ = x_sample.shape
    n_ctx_rows = n_ctx * seq
    t = n_ctx_rows + n_dec * dec_seq

    x = jnp.concatenate([x_prompt.reshape(n_ctx_rows, d), x_sample.reshape(n_dec * dec_seq, d)], axis=0)
    cond = jnp.concatenate([c_ctx[None, :], c], axis=0)
    cond8 = jnp.pad(cond, ((0, SUBLANES - cond.shape[0]), (0, 0)))
    mods = _ada(cond8, w_ada, b_ada)[:, :1 + n_dec].reshape(DEPTH, 1 + n_dec, 6, d)

    cos2, sin2 = _rope_tables(dec_seq)
    no_rope = jnp.zeros((seq, LANES), F32)
    zero_ret = jnp.zeros((n_ctx, 2, RET_HEADS, RET_DK, RET_DV), F32)
    zero_s5 = jnp.zeros((n_ctx, 2, S5_GROUPS, S5_STATE), F32)
    dft_ctx = _dft_mats(seq)
    dft_dec = _dft_mats(dec_seq)

    ret_list, s5r_list, s5i_list = [], [], []
    for l in range(DEPTH):
        mod = mods[l]
        proj = _in_proj(x, mod, norm_mix[l], w_in[l].astype(BF16), n_ctx_rows, dec_seq)

        log_gamma = jax.nn.log_sigmoid(ret_decay[l].astype(F32))
        ret_o, ret_s = _retention(proj, log_gamma, zero_ret, no_rope, no_rope,
                                  row0=0, n_seq=n_ctx, seq_len=seq, hb=RET_HEADS, rope=False,
                                  dst=jnp.zeros((t, RET_WIDTH), BF16))
        ret_o, _ = _retention(proj, log_gamma, state_ret[:, l].astype(F32), cos2, sin2,
                              row0=n_ctx_rows, n_seq=n_dec, seq_len=dec_seq, hb=1, rope=True, dst=ret_o)
        ret_list.append(ret_s)

        mats = _s5_mats(s5_lam_re[l], s5_lam_im[l], s5_log_dt[l], s5_b_re[l], s5_b_im[l],
                        s5_c_re[l], s5_c_im[l], s5_d[l])
        s5_y, s5_re, s5_im = _s5(proj, mats, zero_s5, zero_s5, row0=0, n_seq=n_ctx, seq_len=seq,
                                 dst=jnp.zeros((t, S5_WIDTH), F32))
        s5_y, _, _ = _s5(proj, mats, state_s5_re[:, l], state_s5_im[:, l],
                         row0=n_ctx_rows, n_seq=n_dec, seq_len=dec_seq, dst=s5_y)
        s5r_list.append(s5_re)
        s5i_list.append(s5_im)

        filt = (hy_f1_w[l], hy_f1_b[l], hy_f2_w[l], hy_f2_b[l], hy_f3_w[l], hy_f3_b[l], hy_freq[l], hy_decay[l])
        hy_o = _hyena(proj, hy_conv_w[l], hy_conv_b[l], hy_bias[l], dft_ctx, _hy_filter_mults(seq, *filt),
                      row0=0, n_seq=n_ctx, seq_len=seq, cb=HY_WIDTH, tk=seq, dst=jnp.zeros((t, HY_WIDTH), BF16))
        hy_o = _hyena(proj, hy_conv_w[l], hy_conv_b[l], hy_bias[l], dft_dec, _hy_filter_mults(dec_seq, *filt),
                      row0=n_ctx_rows, n_seq=n_dec, seq_len=dec_seq, cb=HY_WIDTH // 2, tk=512, dst=hy_o)

        x, h2, logits = _out_proj(x, ret_o, s5_y, hy_o, mod, norm_ffn[l], s5_glu_w[l], s5_glu_b[l],
                                  w_out[l], moe_router[l], n_ctx_rows, dec_seq)

        nb = -(-(t * TOP_K) // MOE_BM) + N_EXPERTS
        pos, gates, blk_e_row, info = _router(logits, moe_router_bias[l], nb)
        slot_tok, blk_e, first, slot, nxt, n_used = _dispatch_plan(pos, blk_e_row, info, nb)
        xs = _gather_rows(h2, slot_tok)
        eo = _moe_grouped(xs, blk_e, first, slot, nxt, n_used, moe_w_gate, moe_w_up, moe_w_down, l)
        picked = _gather_rows(eo, pos[:TOP_K].reshape(-1)).reshape(TOP_K, t * TOKEN_ROWS, LANES)
        gates_t = gates.T

        sh = (sh_w_gate[l], sh_w_up[l], sh_w_down[l])
        if l < DEPTH - 1:
            x = _shared(x, h2, gates_t, picked, mod, *sh, final_norm, n_ctx_rows, dec_seq, final=False)
        else:
            y_c = _shared(x, h2, gates_t, picked, mod, *sh, final_norm, n_ctx_rows, dec_seq, final=True,
                          row0=0, rows=n_ctx_rows)
            y_d = _shared(x, h2, gates_t, picked, mod, *sh, final_norm, n_ctx_rows, dec_seq, final=True,
                          row0=n_ctx_rows, rows=n_dec * dec_seq)

    return (y_c.reshape(n_ctx, seq, d), y_d.reshape(n_dec, dec_seq, d),
            jnp.stack(ret_list, axis=1), jnp.stack(s5r_list, axis=1), jnp.stack(s5i_list, axis=1))
```

```python
import functools
import math

import jax
import jax.numpy as jnp
from jax import lax
from jax.experimental import pallas as pl
from jax.experimental.pallas import tpu as pltpu

F32 = jnp.float32
BF16 = jnp.bfloat16

D_MODEL = 2048
DEPTH = 2
GRID_W = 64
RET_HEADS = 8
RET_DK = 128
RET_DV = 128
RET_WIDTH = RET_HEADS * RET_DV
RET_CHUNK = 128
ROPE_BASE = 10000.0
S5_WIDTH = 512
S5_GROUP = 16
S5_GROUPS = S5_WIDTH // S5_GROUP
S5_STATE = 64
S5_Q = 16
HY_WIDTH = 512
HY_BANDS = 16
IN_WIDTH = 4 * RET_WIDTH + S5_WIDTH + 3 * HY_WIDTH
U_COL = 4 * RET_WIDTH
HY_COL = U_COL + S5_WIDTH
N_EXPERTS = 64
TOP_K = 6
D_EXPERT = 512
D_SHARED = 512
ROUTED_SCALE = 2.5
EPS = 1e-6

LANES = 128
SUBLANES = 8
VMEM_LIMIT = 56 * 1024 * 1024

ROW_TILE = 512
MOE_BM = 256


def _cparams(sem):
    return pltpu.CompilerParams(dimension_semantics=sem, vmem_limit_bytes=VMEM_LIMIT)


def _dot(a, b):
    return jnp.dot(a, b, preferred_element_type=F32)


def _rms(x, g):
    var = jnp.mean(x * x, axis=-1, keepdims=True)
    return x * lax.rsqrt(var + EPS) * g


def _pack_halves(xb):
    n = xb.shape[1] // 2
    lo = lax.bitcast_convert_type(xb[:, :n].astype(F32), jnp.uint32) >> 16
    hi = lax.bitcast_convert_type(xb[:, n:].astype(F32), jnp.uint32)
    return lo | hi


def _unpack_halves(w):
    lo = lax.bitcast_convert_type(w << 16, F32)
    hi = lax.bitcast_convert_type(w & jnp.uint32(0xFFFF0000), F32)
    return lo, hi


TOKEN_ROWS = D_MODEL // 2 // LANES


def _store_token_tiles(ref, w):
    m = w.shape[0]
    for s in range(TOKEN_ROWS):
        ref[pl.ds(s, m, stride=TOKEN_ROWS), :] = w[:, s * LANES:(s + 1) * LANES]


def _load_token_tiles(ref, m):
    return jnp.concatenate([ref[pl.ds(s, m, stride=TOKEN_ROWS), :] for s in range(TOKEN_ROWS)], axis=1)


def _ada_kernel(c_ref, w_ref, b_ref, o_ref):
    c = c_ref[...]
    s = (c * jax.nn.sigmoid(c)).astype(BF16)
    o_ref[0] = _dot(s, w_ref[0].astype(BF16)) + b_ref[0]


def _ada(cond8, w_ada, b_ada):
    tn = 512
    n = w_ada.shape[-1]
    return pl.pallas_call(
        _ada_kernel,
        grid=(DEPTH, n // tn),
        in_specs=[
            pl.BlockSpec((SUBLANES, D_MODEL), lambda l, j: (0, 0)),
            pl.BlockSpec((1, D_MODEL, tn), lambda l, j: (l, 0, j)),
            pl.BlockSpec((1, 1, tn), lambda l, j: (l, 0, j)),
        ],
        out_specs=pl.BlockSpec((1, SUBLANES, tn), lambda l, j: (l, 0, j)),
        out_shape=jax.ShapeDtypeStruct((DEPTH, SUBLANES, n), F32),
        compiler_params=_cparams(("parallel", "parallel")),
        name="ada",
    )(cond8, w_ada, b_ada.reshape(DEPTH, 1, n))


def _group_of(i, tile, n_ctx_rows, dec_seq):
    ctx_tiles = n_ctx_rows // tile
    per = dec_seq // tile
    return jnp.where(i < ctx_tiles, 0, 1 + (i - ctx_tiles) // per)


def _in_kernel(x_ref, mod_ref, g_ref, w_ref, o_ref, h_scr):
    @pl.when(pl.program_id(1) == 0)
    def _():
        m = mod_ref[0]
        h = _rms(x_ref[...], g_ref[...]) * (1.0 + m[1:2]) + m[0:1]
        h_scr[...] = h.astype(BF16)

    o_ref[...] = _dot(h_scr[...], w_ref[...])


def _in_proj(x, mod, g, w_bf, n_ctx_rows, dec_seq):
    t = x.shape[0]
    tm, tn = 1024, 512
    grp = functools.partial(_group_of, tile=tm, n_ctx_rows=n_ctx_rows, dec_seq=dec_seq)
    return pl.pallas_call(
        _in_kernel,
        grid=(t // tm, IN_WIDTH // tn),
        in_specs=[
            pl.BlockSpec((tm, D_MODEL), lambda i, j: (i, 0)),
            pl.BlockSpec((1, 6, D_MODEL), lambda i, j: (grp(i), 0, 0)),
            pl.BlockSpec((1, D_MODEL), lambda i, j: (0, 0)),
            pl.BlockSpec((D_MODEL, tn), lambda i, j: (0, j)),
        ],
        out_specs=pl.BlockSpec((tm, tn), lambda i, j: (i, j)),
        out_shape=jax.ShapeDtypeStruct((t, IN_WIDTH), F32),
        scratch_shapes=[pltpu.VMEM((tm, D_MODEL), BF16)],
        compiler_params=_cparams(("parallel", "arbitrary")),
        name="in_proj",
    )(x, mod, g.reshape(1, D_MODEL), w_bf)


def _ret_kernel(lg_ref, q_ref, k_ref, v_ref, gt_ref, cos_ref, sin_ref, s0_ref,
                o_ref, sfin_ref, acc_scr, q_scr, k_scr, *, seq_len, hb, rope):
    c = RET_CHUNK
    n_chunks = seq_len // c
    ii = lax.broadcasted_iota(jnp.int32, (c, c), 0)
    jj = lax.broadcasted_iota(jnp.int32, (c, c), 1)
    rel = (ii - jj).astype(F32)
    ci = lax.broadcasted_iota(jnp.int32, (c, 1), 0).astype(F32)
    one = jnp.ones((1, 1), F32)
    tdot = functools.partial(lax.dot_general, preferred_element_type=F32)

    for hh in range(hb):
        head = pl.program_id(1) * hb + hh
        lgf = lg_ref[0, head]
        lgb = lg_ref[1, head]
        dmask = (jnp.where(rel >= 0, jnp.exp(lgf * jnp.maximum(rel, 0.0)), 0.0)
                 + jnp.where(rel <= 0, jnp.exp(lgb * jnp.maximum(-rel, 0.0)), 0.0))
        qd_f = jnp.exp(lgf * (ci + 1.0))
        kd_f = jnp.exp(lgf * (c - 1.0 - ci))
        cd_f = jnp.exp(lgf * c * one)
        qd_b = jnp.exp(lgb * (c - ci))
        kd_b = jnp.exp(lgb * ci)
        cd_b = jnp.exp(lgb * c * one)
        lanes = slice(hh * LANES, (hh + 1) * LANES)

        def rows_of(n):
            if isinstance(n, int):
                return slice(n * c, (n + 1) * c)
            return pl.ds(pl.multiple_of(n * c, c), c)

        def fwd_chunk(n, s_f):
            rows = rows_of(n)
            q = q_ref[rows, lanes]
            k = k_ref[rows, lanes] * (RET_DK ** -0.5)
            if rope:
                cs = cos_ref[rows, :]
                sn = sin_ref[rows, :]
                q = q * cs + pltpu.roll(q, RET_DK // 2, 1) * sn
                k = k * cs + pltpu.roll(k, RET_DK // 2, 1) * sn
            qb = q.astype(BF16)
            vb = v_ref[rows, lanes].astype(BF16)
            q_scr[rows, :] = qb
            k_scr[rows, :] = k
            scores = tdot(qb, k.astype(BF16), (((1,), (1,)), ((), ()))) * dmask
            inner = _dot(scores.astype(BF16), vb)
            cross = _dot(qb, s_f.astype(BF16)) * qd_f
            acc_scr[rows, :] = inner + cross
            upd = tdot((k * kd_f).astype(BF16), vb, (((0,), (0,)), ((), ())))
            return s_f * cd_f + upd

        def bwd_chunk(m, s_b):
            n = n_chunks - 1 - m
            rows = rows_of(n)
            qb = q_scr[rows, :]
            k = k_scr[rows, :]
            vb = v_ref[rows, lanes].astype(BF16)
            o = acc_scr[rows, :] + _dot(qb, s_b.astype(BF16)) * qd_b
            mu = jnp.mean(o, axis=-1, keepdims=True)
            oc = o - mu
            var = jnp.mean(oc * oc, axis=-1, keepdims=True)
            o = oc * lax.rsqrt(var + EPS)
            g = gt_ref[rows, lanes]
            o_ref[rows, lanes] = (g * jax.nn.sigmoid(g) * o).astype(o_ref.dtype)
            upd = tdot((k * kd_b).astype(BF16), vb, (((0,), (0,)), ((), ())))
            return s_b * cd_b + upd

        s_f = s0_ref[0, 0, hh]
        s_b = s0_ref[0, 1, hh]
        if n_chunks <= 4:
            for n in range(n_chunks):
                s_f = fwd_chunk(n, s_f)
            for m in range(n_chunks):
                s_b = bwd_chunk(m, s_b)
        else:
            s_f = lax.fori_loop(0, n_chunks, fwd_chunk, s_f)
            s_b = lax.fori_loop(0, n_chunks, bwd_chunk, s_b)
        sfin_ref[0, 0, hh] = s_f
        sfin_ref[0, 1, hh] = s_b


def _into(kernel_fn, n_in, dst):
    if dst is None:
        return kernel_fn, [], [], {}

    def body(*refs):
        return kernel_fn(*refs[:n_in], *refs[n_in + 1:])

    return body, [pl.BlockSpec(memory_space=pl.ANY)], [dst], {n_in: 0}


def _retention(proj, log_gamma, s0, cos2, sin2, *, row0, n_seq, seq_len, hb, rope, dst=None):
    blk0 = row0 // seq_len
    body, dst_spec, dst_arg, alias = _into(
        functools.partial(_ret_kernel, seq_len=seq_len, hb=hb, rope=rope), 8, dst)
    w = hb * LANES
    hblocks = RET_HEADS // hb
    col = lambda part: (lambda b, h, lg: (blk0 + b, part * hblocks + h))
    grid_spec = pltpu.PrefetchScalarGridSpec(
        num_scalar_prefetch=1,
        grid=(n_seq, hblocks),
        in_specs=[
            pl.BlockSpec((seq_len, w), col(0)),
            pl.BlockSpec((seq_len, w), col(1)),
            pl.BlockSpec((seq_len, w), col(2)),
            pl.BlockSpec((seq_len, w), col(3)),
            pl.BlockSpec((seq_len, LANES), lambda b, h, lg: (0, 0)),
            pl.BlockSpec((seq_len, LANES), lambda b, h, lg: (0, 0)),
            pl.BlockSpec((1, 2, hb, RET_DK, RET_DV), lambda b, h, lg: (b, 0, h, 0, 0)),
        ] + dst_spec,
        out_specs=[
            pl.BlockSpec((seq_len, w), lambda b, h, lg: (blk0 + b, h)),
            pl.BlockSpec((1, 2, hb, RET_DK, RET_DV), lambda b, h, lg: (b, 0, h, 0, 0)),
        ],
        scratch_shapes=[
            pltpu.VMEM((seq_len, LANES), F32),
            pltpu.VMEM((seq_len, LANES), BF16),
            pltpu.VMEM((seq_len, LANES), F32),
        ],
    )
    return pl.pallas_call(
        body,
        grid_spec=grid_spec,
        out_shape=[
            jax.ShapeDtypeStruct((proj.shape[0], RET_WIDTH), BF16),
            jax.ShapeDtypeStruct((n_seq, 2, RET_HEADS, RET_DK, RET_DV), F32),
        ],
        input_output_aliases=alias,
        compiler_params=_cparams(("parallel", "arbitrary")),
        name="retention",
    )(log_gamma, proj, proj, proj, proj, cos2, sin2, s0, *dst_arg)


def _rope_tables(seq_len):
    rows_n = seq_len // GRID_W
    rows = jnp.repeat(jnp.arange(rows_n, dtype=F32), GRID_W)
    cols = jnp.tile(jnp.arange(GRID_W, dtype=F32), rows_n)
    nf = RET_DK // 4
    inv = ROPE_BASE ** (-jnp.arange(nf, dtype=F32) / nf)
    ang = jnp.concatenate([rows[:, None] * inv, cols[:, None] * inv], axis=-1)
    cs, sn = jnp.cos(ang), jnp.sin(ang)
    return jnp.concatenate([cs, cs], axis=-1), jnp.concatenate([-sn, sn], axis=-1)


S5_GB = LANES // S5_GROUP
S5_W = S5_Q * LANES
S5_SPLIT = 4
S5_SW = S5_GB * S5_STATE
S5_SB = S5_SW // LANES


def _s5_kernel(u_ref, bq_ref, t_ref, cq_ref, ar_ref, ai_ref, d_ref, h0_ref,
               y_ref, hf_ref, ub_scr, sm_scr, hp_scr, hpb_scr, *, n_seq, n_chunks):
    s = pl.program_id(1)
    m = n_seq * n_chunks
    q = S5_Q
    nblk = S5_W // LANES
    sb = S5_SB

    per = q // S5_SPLIT

    @pl.when(s == 0)
    def _():
        for j in range(q):
            ub_scr[j // per, :, (j % per) * LANES:(j % per + 1) * LANES] = (
                u_ref[pl.ds(j, m, stride=q), :].astype(BF16))

    @pl.when(s < S5_SPLIT)
    def _():
        part = _dot(ub_scr[jnp.minimum(s, S5_SPLIT - 1)], bq_ref[0])

        @pl.when(s == 0)
        def _():
            for cb in range(nblk):
                sm_scr[cb] = part[:, cb * LANES:(cb + 1) * LANES]

        @pl.when(s > 0)
        def _():
            for cb in range(nblk):
                sm_scr[cb] += part[:, cb * LANES:(cb + 1) * LANES]

    @pl.when(s == S5_SPLIT - 1)
    def _():
        ar = ar_ref[0]
        ai = ai_ref[0]
        h0 = h0_ref[0]
        blk = lambda a, cb: a[:, cb * LANES:(cb + 1) * LANES]

        def body(n, carry):
            rows_f = pl.ds(n, n_seq, stride=n_chunks)
            rows_b = pl.ds(n_chunks - 1 - n, n_seq, stride=n_chunks)
            new = list(carry)
            for d, rows in ((0, rows_f), (1, rows_b)):
                for c in range(sb):
                    re_i = d * sb + c
                    im_i = (2 + d) * sb + c
                    hr, hi = carry[re_i], carry[im_i]
                    hp_scr[re_i, rows, :] = hr
                    hp_scr[im_i, rows, :] = hi
                    a_r, a_i = blk(ar, re_i), blk(ai, re_i)
                    new[re_i] = a_r * hr - a_i * hi + sm_scr[re_i, rows, :]
                    new[im_i] = a_r * hi + a_i * hr + sm_scr[im_i, rows, :]
            return tuple(new)

        fin = lax.fori_loop(0, n_chunks, body, tuple(blk(h0, cb) for cb in range(nblk)))
        hf_ref[0] = jnp.concatenate(fin, axis=1)
        for cb in range(nblk):
            hpb_scr[:, cb * LANES:(cb + 1) * LANES] = hp_scr[cb].astype(BF16)

    @pl.when(s >= S5_SPLIT)
    def _():
        ub = jnp.concatenate([ub_scr[k] for k in range(S5_SPLIT)], axis=1)
        y = _dot(ub, t_ref[0]) + _dot(hpb_scr[...], cq_ref[0])
        dd = d_ref[0]
        for ii in range(per):
            rows = pl.ds((s - S5_SPLIT) * per + ii, m, stride=q)
            yi = y[:, ii * LANES:(ii + 1) * LANES] + dd * u_ref[rows, :]
            y_ref[rows, :] = jax.nn.gelu(yi)


def _s5_expand_kernel(mc_ref, o_ref, *, xsize, ysize):
    xs, ys, gs = xsize.bit_length() - 1, ysize.bit_length() - 1, S5_GB.bit_length() - 1
    assert xsize == 1 << xs and ysize == 1 << ys and S5_GB == 1 << gs
    cw = o_ref.shape[2]
    nc = mc_ref.shape[2]
    col0 = pl.program_id(1) * cw
    r = lax.broadcasted_iota(jnp.int32, (nc, cw), 0)
    col = lax.broadcasted_iota(jnp.int32, (nc, cw), 1) + col0
    spread = jnp.logical_and(r >> ys == col >> (ys + gs), (r & (ysize - 1)) == (col & (ysize - 1)))
    big = _dot(mc_ref[0], jnp.where(spread, 1.0, 0.0).astype(BF16))
    row = lax.broadcasted_iota(jnp.int32, big.shape, 0)
    colb = lax.broadcasted_iota(jnp.int32, big.shape, 1) + col0
    same = ((row >> xs) & (S5_GB - 1)) == ((colb >> ys) & (S5_GB - 1))
    o_ref[0] = jnp.where(same, big, 0.0).astype(BF16)


def _s5_expand(mc, *, xsize, ysize):
    nb, rows, nc = mc.shape
    cols = nc * S5_GB
    cw = 512
    return pl.pallas_call(
        functools.partial(_s5_expand_kernel, xsize=xsize, ysize=ysize),
        grid=(nb, cols // cw),
        in_specs=[pl.BlockSpec((1, rows, nc), lambda b, j: (b, 0, 0))],
        out_specs=pl.BlockSpec((1, rows, cw), lambda b, j: (b, 0, j)),
        out_shape=jax.ShapeDtypeStruct((nb, rows, cols), BF16),
        compiler_params=_cparams(("parallel", "parallel")),
        name="s5_expand",
    )(mc)


def _s5_mats(lam_re, lam_im, log_dt, b_re, b_im, c_re, c_im, d):
    q, g, p, ch = S5_Q, S5_GROUPS, S5_STATE, S5_GROUP
    lam = lax.complex(jnp.minimum(lam_re.astype(F32), -1e-4), lam_im.astype(F32))
    ldt = lam * jnp.exp(log_dt.astype(F32))[..., None]
    lam_bar = jnp.exp(ldt)
    b_bar = ((lam_bar - 1.0) / lam)[..., None] * lax.complex(b_re.astype(F32), b_im.astype(F32))
    cc = lax.complex(c_re.astype(F32), c_im.astype(F32))
    pw = jnp.exp(ldt[..., None] * jnp.arange(q + 1, dtype=F32))
    hi = lax.Precision.HIGHEST
    kern = jnp.real(jnp.einsum('zgcp,zgpd,zgpe->zgdce', cc, pw[..., :q], b_bar, precision=hi))
    i = jnp.arange(q)[None, :]
    j = jnp.arange(q)[:, None]
    dif = i - j
    kf = jnp.where((dif >= 0)[None, :, :, None, None], kern[0][:, jnp.maximum(dif, 0)], 0.0)
    kb = jnp.where((dif <= 0)[None, :, :, None, None], kern[1][:, jnp.maximum(-dif, 0)], 0.0)
    tm = (kf + kb).transpose(0, 1, 4, 2, 3).reshape(g, q * ch, q * ch)

    pw_dn = jnp.exp(ldt[..., None] * (q - jnp.arange(q + 1, dtype=F32)))
    bf = pw_dn[0][..., 1:][:, :, :, None] * b_bar[0][:, :, None, :]
    bb = pw[1][..., :q][:, :, :, None] * b_bar[1][:, :, None, :]
    to_rows = lambda m: m.transpose(0, 2, 3, 1).reshape(g, q * ch, p)
    bq = jnp.concatenate([to_rows(jnp.real(bf)), to_rows(jnp.real(bb)),
                          to_rows(jnp.imag(bf)), to_rows(jnp.imag(bb))], axis=-1)

    cf = cc[0].transpose(0, 2, 1)[:, :, None, :] * pw[0][..., 1:][:, :, :, None]
    cb = cc[1].transpose(0, 2, 1)[:, :, None, :] * pw_dn[1][..., :q][:, :, :, None]
    to_cols = lambda m: m.reshape(g, p, q * ch)
    cq = jnp.concatenate([to_cols(jnp.real(cf)), to_cols(jnp.real(cb)),
                          to_cols(-jnp.imag(cf)), to_cols(-jnp.imag(cb))], axis=1)

    gb, nb = S5_GB, g // S5_GB
    rows_of = lambda a, outer, inner: (a.reshape(nb, gb, outer, inner, a.shape[-1]).transpose(0, 2, 1, 3, 4)
                                       .reshape(nb, outer * gb * inner, a.shape[-1]).astype(BF16))
    tm_bd = _s5_expand(rows_of(tm, q, ch), xsize=ch, ysize=ch)
    bq_bd = _s5_expand(rows_of(bq, q, ch), xsize=ch, ysize=p)
    cq_bd = _s5_expand(rows_of(cq, 4, p), xsize=p, ysize=ch)

    lq = pw[..., q].reshape(2, nb, 1, S5_SW)
    ar = jnp.concatenate([jnp.real(lq[0]), jnp.real(lq[1])], axis=-1)
    ai = jnp.concatenate([jnp.imag(lq[0]), jnp.imag(lq[1])], axis=-1)
    dd = d.astype(F32).reshape(nb, 1, LANES)
    return tm_bd, bq_bd, cq_bd, ar, ai, dd


def _s5(proj, mats, h0_re, h0_im, *, row0, n_seq, seq_len, dst=None):
    q, p = S5_Q, S5_STATE
    n_chunks = seq_len // q
    m = n_seq * n_chunks
    rows = n_seq * seq_len
    nb = S5_GROUPS // S5_GB
    nblk = S5_W // LANES
    kw = S5_W // S5_SPLIT
    tm, bq, cq, ar, ai, dd = mats
    part = lambda a: a.astype(F32).reshape(n_seq, nb, S5_SW)
    h0 = jnp.concatenate([part(h0_re[:, 0]), part(h0_re[:, 1]), part(h0_im[:, 0]), part(h0_im[:, 1])],
                         axis=-1).transpose(1, 0, 2)
    per_b = lambda shape: pl.BlockSpec((1,) + shape, lambda b, s: (b, 0, 0))
    body, dst_spec, dst_arg, alias = _into(functools.partial(_s5_kernel, n_seq=n_seq, n_chunks=n_chunks), 8, dst)
    y, hf = pl.pallas_call(
        body,
        grid=(nb, 2 * S5_SPLIT),
        in_specs=[
            pl.BlockSpec((rows, LANES), lambda b, s: (row0 // rows, U_COL // LANES + b)),
            pl.BlockSpec((1, kw, 4 * S5_SW), lambda b, s: (b, jnp.minimum(s, S5_SPLIT - 1), 0)),
            pl.BlockSpec((1, S5_W, kw), lambda b, s: (b, 0, jnp.maximum(s - S5_SPLIT, 0))),
            pl.BlockSpec((1, 4 * S5_SW, kw), lambda b, s: (b, 0, jnp.maximum(s - S5_SPLIT, 0))),
            per_b((1, 2 * S5_SW)), per_b((1, 2 * S5_SW)), per_b((1, LANES)), per_b((n_seq, 4 * S5_SW)),
        ] + dst_spec,
        out_specs=[pl.BlockSpec((rows, LANES), lambda b, s: (row0 // rows, b)), per_b((n_seq, 4 * S5_SW))],
        out_shape=[jax.ShapeDtypeStruct((proj.shape[0], S5_WIDTH), F32),
                   jax.ShapeDtypeStruct((nb, n_seq, 4 * S5_SW), F32)],
        scratch_shapes=[pltpu.VMEM((S5_SPLIT, m, kw), BF16), pltpu.VMEM((nblk, m, LANES), F32),
                        pltpu.VMEM((nblk, m, LANES), F32), pltpu.VMEM((m, 4 * S5_SW), BF16)],
        input_output_aliases=alias,
        compiler_params=_cparams(("parallel", "arbitrary")),
        name="s5",
    )(proj, bq, tm, cq, ar, ai, dd, h0, *dst_arg)
    hf = hf.reshape(nb, n_seq, 4, S5_GB, p).transpose(1, 2, 0, 3, 4).reshape(n_seq, 4, S5_GROUPS, p)
    return y, hf[:, 0:2], hf[:, 2:4]


def _conv3(x, w, b):
    n = x.shape[0]
    row = lax.broadcasted_iota(jnp.int32, x.shape, 0)
    prev = jnp.where(row == 0, 0.0, pltpu.roll(x, 1, 0))
    nxt = jnp.where(row == n - 1, 0.0, pltpu.roll(x, n - 1, 0))
    return prev * w[0:1] + x * w[1:2] + nxt * w[2:3] + b


def _hy_fwd_kernel(x0_ref, x1_ref, v_ref, w0_ref, w1_ref, wv_ref, b0_ref, b1_ref, bv_ref,
                   fc_ref, fs_ref, m1_ref, m2_ref, m3_ref, p_ref, z_ref, x0c_ref, zb_scr):
    @pl.when(pl.program_id(2) == 0)
    def _():
        z = _conv3(x1_ref[...], w1_ref[...], b1_ref[...]) * _conv3(v_ref[...], wv_ref[...], bv_ref[...])
        z_ref[...] = z
        zb_scr[...] = z.astype(BF16)
        x0c_ref[...] = _conv3(x0_ref[...], w0_ref[...], b0_ref[...])

    zb = zb_scr[...]
    a = _dot(fc_ref[...], zb)
    b = _dot(fs_ref[...], zb)
    m2 = m2_ref[...]
    p_ref[0, 0] = (m1_ref[...] * a + m2 * b).astype(BF16)
    p_ref[0, 1] = (m3_ref[...] * b - m2 * a).astype(BF16)


def _hy_inv_kernel(p_ref, gc_ref, gs_ref, z_ref, x0c_ref, bias_ref, o_ref):
    conv = _dot(gc_ref[...], p_ref[0, 0]) + _dot(gs_ref[...], p_ref[0, 1])
    o_ref[...] = (x0c_ref[...] * (conv + bias_ref[...] * z_ref[...])).astype(o_ref.dtype)


def _dft_mats(seq_len):
    k = jnp.arange(seq_len, dtype=jnp.int32)
    kj = (k[:, None] * k[None, :]) % (2 * seq_len)
    ang = kj.astype(F32) * (math.pi / seq_len)
    cm = jnp.cos(ang)
    sm = -jnp.sin(ang)
    nyq = jnp.where(k % 2 == 0, 1.0, -1.0).astype(F32)
    sm = sm.at[0, :].set(nyq)
    return cm.astype(BF16), sm.astype(BF16), sm.T.astype(BF16)


def _hy_filter_mults(seq_len, f1_w, f1_b, f2_w, f2_b, f3_w, f3_b, freq, decay):
    n = seq_len
    t = (jnp.arange(n, dtype=F32) / n)[:, None]
    bands = jnp.arange(1, HY_BANDS + 1, dtype=F32)[None, :]
    z = jnp.concatenate([t, jnp.cos(2.0 * math.pi * t * bands), jnp.sin(2.0 * math.pi * t * bands)], axis=-1)
    hi = lax.Precision.HIGHEST
    fr = freq.astype(F32)
    h = jnp.sin(fr * (jnp.dot(z, f1_w.astype(F32), precision=hi) + f1_b.astype(F32)))
    h = jnp.sin(fr * (jnp.dot(h, f2_w.astype(F32), precision=hi) + f2_b.astype(F32)))
    h = jnp.dot(h, f3_w.astype(F32), precision=hi) + f3_b.astype(F32)
    h = h * jnp.exp(-t * jnp.abs(decay.astype(F32)))
    h = h.reshape(n, 2, HY_WIDTH)
    h = h / jnp.sum(jnp.abs(h), axis=(0, 1), keepdims=True)
    hc = jnp.concatenate([h[:, 0], h[::-1, 1]], axis=0)
    spec = jnp.fft.rfft(hc, axis=0)
    hr, him = jnp.real(spec), jnp.imag(spec)
    wk = jnp.where(jnp.arange(n) == 0, 1.0, 2.0)[:, None] / (2.0 * n)
    first = (jnp.arange(n) == 0)[:, None]
    m1 = hr[:n] * wk
    m2 = jnp.where(first, 0.0, -him[:n]) * wk
    m3 = jnp.where(first, hr[n:n + 1], hr[:n]) * wk
    return m1.astype(F32), m2.astype(F32), m3.astype(F32)


def _hyena(proj, conv_w, conv_b, bias, dft, mults, *, row0, n_seq, seq_len, cb, tk, dst=None):
    blk0 = row0 // seq_len
    nc = HY_WIDTH // cb
    nk = seq_len // tk
    c0 = HY_COL // cb
    cm, sm, smt = dft
    m1, m2, m3 = mults
    xcol = lambda part: pl.BlockSpec((seq_len, cb), lambda b, c, k: (blk0 + b, c0 + part * nc + c))
    wcol = lambda part: pl.BlockSpec((3, cb), lambda b, c, k: (0, part * nc + c))
    bcol = lambda part: pl.BlockSpec((1, cb), lambda b, c, k: (0, part * nc + c))
    frow = pl.BlockSpec((tk, seq_len), lambda b, c, k: (k, 0))
    mblk = pl.BlockSpec((tk, cb), lambda b, c, k: (k, c))
    cb2 = conv_b.reshape(1, 3 * HY_WIDTH)
    pspec, z, x0c = pl.pallas_call(
        _hy_fwd_kernel,
        grid=(n_seq, nc, nk),
        in_specs=[xcol(0), xcol(1), xcol(2), wcol(0), wcol(1), wcol(2), bcol(0), bcol(1), bcol(2),
                  frow, frow, mblk, mblk, mblk],
        out_specs=[
            pl.BlockSpec((1, 2, tk, cb), lambda b, c, k: (b, 0, k, c)),
            pl.BlockSpec((seq_len, cb), lambda b, c, k: (b, c)),
            pl.BlockSpec((seq_len, cb), lambda b, c, k: (b, c)),
        ],
        out_shape=[
            jax.ShapeDtypeStruct((n_seq, 2, seq_len, HY_WIDTH), BF16),
            jax.ShapeDtypeStruct((n_seq * seq_len, HY_WIDTH), F32),
            jax.ShapeDtypeStruct((n_seq * seq_len, HY_WIDTH), F32),
        ],
        scratch_shapes=[pltpu.VMEM((seq_len, cb), BF16)],
        compiler_params=_cparams(("parallel", "parallel", "arbitrary")),
        name="hyena_fwd",
    )(proj, proj, proj, conv_w, conv_w, conv_w, cb2, cb2, cb2, cm, sm, m1, m2, m3)
    grow = pl.BlockSpec((tk, seq_len), lambda b, c, k: (k, 0))
    tile = pl.BlockSpec((tk, cb), lambda b, c, k: (b * nk + k, c))
    body, dst_spec, dst_arg, alias = _into(_hy_inv_kernel, 6, dst)
    return pl.pallas_call(
        body,
        grid=(n_seq, nc, nk),
        in_specs=[pl.BlockSpec((1, 2, seq_len, cb), lambda b, c, k: (b, 0, 0, c)),
                  grow, grow, tile, tile, pl.BlockSpec((1, cb), lambda b, c, k: (0, c))] + dst_spec,
        out_specs=pl.BlockSpec((tk, cb), lambda b, c, k: (row0 // tk + b * nk + k, c)),
        out_shape=jax.ShapeDtypeStruct((proj.shape[0], HY_WIDTH), BF16),
        input_output_aliases=alias,
        compiler_params=_cparams(("parallel", "parallel", "arbitrary")),
        name="hyena_inv",
    )(pspec, cm, smt, z, x0c, bias.reshape(1, HY_WIDTH), *dst_arg)


def _out_kernel(x_ref, ret_ref, s5_ref, hy_ref, mod_ref, g_ref, gw_ref, gb_ref,
                wr_ref, ws_ref, wh_ref, rt_ref, xo_ref, h_ref, lg_ref):
    m = mod_ref[0]
    y = s5_ref[...]
    s5o = y * jax.nn.sigmoid(_dot(y.astype(BF16), gw_ref[...]) + gb_ref[...])
    mix = (_dot(ret_ref[...], wr_ref[...]) + _dot(s5o.astype(BF16), ws_ref[...])
           + _dot(hy_ref[...], wh_ref[...]))
    x = x_ref[...] + m[2:3] * mix
    xo_ref[...] = x
    h = _rms(x, g_ref[...]) * (1.0 + m[4:5]) + m[3:4]
    hb = h.astype(BF16)
    _store_token_tiles(h_ref, _pack_halves(hb))
    lg_ref[...] = lax.dot_general(rt_ref[...], hb, (((1,), (1,)), ((), ())), preferred_element_type=F32)


def _out_proj(x, ret_o, s5_y, hy_o, mod, g, glu_w, glu_b, w_out, router, n_ctx_rows, dec_seq):
    t = x.shape[0]
    tm = ROW_TILE
    grp = functools.partial(_group_of, tile=tm, n_ctx_rows=n_ctx_rows, dec_seq=dec_seq)
    row = lambda w: pl.BlockSpec((tm, w), lambda i: (i, 0))
    full = lambda a, b: pl.BlockSpec((a, b), lambda i: (0, 0))
    wo = w_out.astype(BF16)
    return pl.pallas_call(
        _out_kernel,
        grid=(t // tm,),
        in_specs=[row(D_MODEL), row(RET_WIDTH), row(S5_WIDTH), row(HY_WIDTH),
                  pl.BlockSpec((1, 6, D_MODEL), lambda i: (grp(i), 0, 0)),
                  full(1, D_MODEL), full(S5_WIDTH, S5_WIDTH), full(1, S5_WIDTH),
                  full(RET_WIDTH, D_MODEL), full(S5_WIDTH, D_MODEL), full(HY_WIDTH, D_MODEL),
                  full(N_EXPERTS, D_MODEL)],
        out_specs=[row(D_MODEL), pl.BlockSpec((tm * TOKEN_ROWS, LANES), lambda i: (i, 0)),
                   pl.BlockSpec((N_EXPERTS, tm), lambda i: (0, i))],
        out_shape=[jax.ShapeDtypeStruct((t, D_MODEL), F32), jax.ShapeDtypeStruct((t * TOKEN_ROWS, LANES), jnp.uint32),
                   jax.ShapeDtypeStruct((N_EXPERTS, t), F32)],
        compiler_params=_cparams(("parallel",)),
        name="out_proj",
    )(x, ret_o, s5_y, hy_o, mod, g.reshape(1, D_MODEL), glu_w.astype(BF16), glu_b.reshape(1, S5_WIDTH),
      wo[:RET_WIDTH], wo[RET_WIDTH:RET_WIDTH + S5_WIDTH], wo[RET_WIDTH + S5_WIDTH:], router.T.astype(BF16))


def _moe_kernel(be_ref, first_ref, slot_ref, nxt_ref, nu_ref, xs_ref, wg_hbm, wu_hbm, wd_hbm, o_ref,
                wg_f, wu_f, wd_f, wg_b, wu_b, wd_b, sem, *, layer):
    i = pl.program_id(0)

    def copies(e, s):
        return (pltpu.make_async_copy(wg_hbm.at[layer, e], wg_f.at[s], sem.at[s, 0]),
                pltpu.make_async_copy(wu_hbm.at[layer, e], wu_f.at[s], sem.at[s, 1]),
                pltpu.make_async_copy(wd_hbm.at[layer, e], wd_f.at[s], sem.at[s, 2]))

    @pl.when(i == 0)
    def _():
        for cp in copies(be_ref[0], 0):
            cp.start()

    @pl.when(first_ref[i] == 1)
    def _():
        s = slot_ref[i]
        for cp in copies(be_ref[i], s):
            cp.wait()

        @pl.when(nxt_ref[i] >= 0)
        def _():
            for cp in copies(nxt_ref[i], 1 - s):
                cp.start()

        wg_b[...] = wg_f[s].astype(BF16)
        wu_b[...] = wu_f[s].astype(BF16)
        wd_b[...] = wd_f[s].astype(BF16)

    @pl.when(i < nu_ref[0])
    def _():
        half = D_MODEL // 2
        x_lo, x_hi = _unpack_halves(_load_token_tiles(xs_ref, MOE_BM))
        x_lo = x_lo.astype(BF16)
        x_hi = x_hi.astype(BF16)
        gate = _dot(x_lo, wg_b[0:half, :]) + _dot(x_hi, wg_b[half:, :])
        up = _dot(x_lo, wu_b[0:half, :]) + _dot(x_hi, wu_b[half:, :])
        hb = gate * jax.nn.sigmoid(gate) * up
        _store_token_tiles(o_ref, _pack_halves(_dot(hb.astype(BF16), wd_b[...]).astype(BF16)))

    @pl.when(i >= nu_ref[0])
    def _():
        o_ref[...] = jnp.zeros_like(o_ref)


def _moe_grouped(xs, blk_e, first, slot, nxt, n_used, w_gate, w_up, w_down, layer):
    pr = xs.shape[0] // TOKEN_ROWS
    bm = MOE_BM
    nb = pr // bm
    grid_spec = pltpu.PrefetchScalarGridSpec(
        num_scalar_prefetch=5,
        grid=(nb,),
        in_specs=[
            pl.BlockSpec((bm * TOKEN_ROWS, LANES), lambda i, *_: (i, 0)),
            pl.BlockSpec(memory_space=pl.ANY),
            pl.BlockSpec(memory_space=pl.ANY),
            pl.BlockSpec(memory_space=pl.ANY),
        ],
        out_specs=pl.BlockSpec((bm * TOKEN_ROWS, LANES), lambda i, *_: (i, 0)),
        scratch_shapes=[pltpu.VMEM((2, D_MODEL, D_EXPERT), F32), pltpu.VMEM((2, D_MODEL, D_EXPERT), F32),
                        pltpu.VMEM((2, D_EXPERT, D_MODEL), F32),
                        pltpu.VMEM((D_MODEL, D_EXPERT), BF16), pltpu.VMEM((D_MODEL, D_EXPERT), BF16),
                        pltpu.VMEM((D_EXPERT, D_MODEL), BF16),
                        pltpu.SemaphoreType.DMA((2, 3))],
    )
    return pl.pallas_call(
        functools.partial(_moe_kernel, layer=layer),
        grid_spec=grid_spec,
        out_shape=jax.ShapeDtypeStruct((pr * TOKEN_ROWS, LANES), jnp.uint32),
        compiler_params=_cparams(("arbitrary",)),
        name="moe_grouped",
    )(blk_e, first, slot, nxt, n_used, xs, w_gate, w_up, w_down)


DISPATCH_TILE = 512


def _dispatch_kernel(pos_ref, h_ref, xs_in, xs_out, sem):
    del xs_in
    tm = pos_ref.shape[1]
    tr = TOKEN_ROWS

    def send(tok, carry):
        src = h_ref.at[pl.ds(pl.multiple_of(tok * tr, tr), tr)]
        for k in range(TOP_K):
            row = pl.multiple_of(pos_ref[k, tok] * tr, tr)
            pltpu.make_async_copy(src, xs_out.at[pl.ds(row, tr)], sem).start()
        return carry

    lax.fori_loop(0, tm, send, 0, unroll=4)
    n = tm * TOP_K * tr
    pltpu.make_async_copy(xs_out.at[pl.ds(0, n)], xs_out.at[pl.ds(0, n)], sem).wait()


def _dispatch(h2, pos, n_rows):
    t = pos.shape[1]
    tm = DISPATCH_TILE
    return pl.pallas_call(
        _dispatch_kernel,
        grid=(t // tm,),
        in_specs=[pl.BlockSpec((SUBLANES, tm), lambda i: (0, i), memory_space=pltpu.SMEM),
                  pl.BlockSpec((tm * TOKEN_ROWS, LANES), lambda i: (i, 0)),
                  pl.BlockSpec(memory_space=pl.ANY)],
        out_specs=pl.BlockSpec(memory_space=pl.ANY),
        out_shape=jax.ShapeDtypeStruct((n_rows * TOKEN_ROWS, LANES), h2.dtype),
        scratch_shapes=[pltpu.SemaphoreType.DMA(())],
        input_output_aliases={2: 0},
        compiler_params=pltpu.CompilerParams(dimension_semantics=("arbitrary",)),
        name="dispatch",
    )(pos, h2, jnp.zeros((n_rows * TOKEN_ROWS, LANES), h2.dtype))


ROUTE_TILE = 512


def _router_kernel(lg_ref, bias_ref, pos_ref, gate_ref, be_ref, info_ref, rank_scr, ek_scr, *, t, nbp):
    tl = ROUTE_TILE
    ne = N_EXPERTS
    bm = MOE_BM
    row = lax.broadcasted_iota(jnp.int32, (ne, tl), 0)
    tri = (lax.broadcasted_iota(jnp.int32, (tl, tl), 0) < lax.broadcasted_iota(jnp.int32, (tl, tl), 1)).astype(BF16)
    bias = bias_ref[...]

    def select(i, counts):
        cols = pl.ds(pl.multiple_of(i * tl, tl), tl)
        s = jax.nn.sigmoid(lg_ref[:, cols])
        sel = s + bias
        mask = jnp.zeros((ne, tl), F32)
        vals = []
        for k in range(TOP_K):
            best = jnp.max(sel, axis=0, keepdims=True)
            idx = jnp.min(jnp.where(sel == best, row, ne), axis=0, keepdims=True)
            hit = row == idx
            vals.append(jnp.sum(jnp.where(hit, s, 0.0), axis=0, keepdims=True))
            sel = jnp.where(hit, -jnp.inf, sel)
            mask = jnp.where(hit, 1.0, mask)
            ek_scr[k:k + 1, cols] = idx
        total = vals[0]
        for v in vals[1:]:
            total = total + v
        scale = ROUTED_SCALE / total
        for k in range(TOP_K):
            gate_ref[k:k + 1, cols] = vals[k] * scale
        gate_ref[TOP_K:SUBLANES, cols] = jnp.zeros((SUBLANES - TOP_K, tl), F32)
        rank_scr[:, cols] = _dot(mask.astype(BF16), tri) + counts
        return counts + jnp.sum(mask, axis=1, keepdims=True)

    counts = lax.fori_loop(0, t // tl, select, jnp.zeros((ne, 1), F32))
    counts = counts.astype(jnp.int32)
    shift = bm.bit_length() - 1
    assert bm == 1 << shift
    padded = ((counts + (bm - 1)) >> shift) << shift
    e0 = lax.broadcasted_iota(jnp.int32, (ne, ne), 0)
    e1 = lax.broadcasted_iota(jnp.int32, (ne, ne), 1)
    padded_row = jnp.sum(jnp.where(e0 == e1, padded, 0), axis=0, keepdims=True)
    counts_row = jnp.sum(jnp.where(e0 == e1, counts, 0), axis=0, keepdims=True)
    pstart = jnp.sum(jnp.where(e1 < e0, padded_row, 0), axis=1, keepdims=True)
    ustart = jnp.sum(jnp.where(e1 < e0, counts_row, 0), axis=1, keepdims=True)
    pend = pstart + padded
    lane = lax.broadcasted_iota(jnp.int32, (ne, LANES), 1)
    info_ref[...] = jnp.where(lane == 0, counts, jnp.where(lane == 1, pstart, jnp.where(lane == 2, ustart, pend)))
    blk = lax.broadcasted_iota(jnp.int32, (ne, nbp), 1) * bm
    owner = jnp.sum(jnp.where(pend <= blk, 1, 0), axis=0, keepdims=True)
    be_ref[...] = jnp.minimum(owner, ne - 1)
    pstart_f = pstart.astype(F32)

    def place(i, carry):
        cols = pl.ds(pl.multiple_of(i * tl, tl), tl)
        dest = rank_scr[:, cols] + pstart_f
        for k in range(TOP_K):
            hit = row == ek_scr[k:k + 1, cols]
            pos_ref[k:k + 1, cols] = jnp.sum(jnp.where(hit, dest, 0.0), axis=0, keepdims=True).astype(jnp.int32)
        pos_ref[TOP_K:SUBLANES, cols] = jnp.zeros((SUBLANES - TOP_K, tl), jnp.int32)
        return carry

    lax.fori_loop(0, t // tl, place, 0)


def _router(logits_t, router_bias, nb):
    t = logits_t.shape[1]
    nbp = -(-nb // LANES) * LANES
    return pl.pallas_call(
        functools.partial(_router_kernel, t=t, nbp=nbp),
        out_shape=[jax.ShapeDtypeStruct((SUBLANES, t), jnp.int32), jax.ShapeDtypeStruct((SUBLANES, t), F32),
                   jax.ShapeDtypeStruct((1, nbp), jnp.int32), jax.ShapeDtypeStruct((N_EXPERTS, LANES), jnp.int32)],
        scratch_shapes=[pltpu.VMEM((N_EXPERTS, t), F32), pltpu.VMEM((SUBLANES, t), jnp.int32)],
        compiler_params=pltpu.CompilerParams(vmem_limit_bytes=VMEM_LIMIT),
        name="router",
    )(logits_t, router_bias.astype(F32).reshape(N_EXPERTS, 1))


def _dispatch_plan(blk_e_row, info, nb):
    bm = MOE_BM
    pend = info[:, 3]
    blk_e = blk_e_row[0, :nb]
    n_used = pend[-1] // bm
    blk = jnp.arange(nb, dtype=jnp.int32)
    prev_e = jnp.concatenate([jnp.full((1,), -1, jnp.int32), blk_e[:-1]])
    first = jnp.logical_and(blk < n_used, blk_e != prev_e)
    slot = (jnp.cumsum(first.astype(jnp.int32)) - 1) % 2
    first_at = jnp.where(first, blk, nb)
    nxt_first = lax.cummin(jnp.concatenate([first_at[1:], jnp.full((1,), nb, jnp.int32)]), reverse=True)
    nxt = jnp.where(nxt_first < nb, blk_e[jnp.minimum(nxt_first, nb - 1)], -1)
    return (blk_e, first.astype(jnp.int32), slot.astype(jnp.int32), nxt.astype(jnp.int32),
            n_used.astype(jnp.int32).reshape(1))


def _shared_kernel(pos_ref, nxt_ref, x_ref, h_ref, gt_ref, mod_ref, sg_ref, su_ref, sd_ref, fn_ref, eo_hbm,
                   o_ref, buf0, buf1, sem, *, final):
    i = pl.program_id(0)
    n = pl.num_programs(0)
    tm = x_ref.shape[0]
    tr = TOKEN_ROWS

    def fetch(idx_ref, buf, slot):
        def one(tok, carry):
            for k in range(TOP_K):
                row = pl.multiple_of(idx_ref[k, tok] * tr, tr)
                pltpu.make_async_copy(eo_hbm.at[pl.ds(row, tr)], buf.at[k, pl.ds(pl.multiple_of(tok * tr, tr), tr)],
                                      sem.at[slot]).start()
            return carry
        lax.fori_loop(0, tm, one, 0, unroll=4)

    def arrived(buf, slot):
        for k in range(TOP_K):
            pltpu.make_async_copy(eo_hbm.at[pl.ds(0, tm * tr)], buf.at[k], sem.at[slot]).wait()

    def compute(buf):
        m = mod_ref[0]
        half = D_MODEL // 2
        h_lo, h_hi = _unpack_halves(_load_token_tiles(h_ref, tm))
        h_lo = h_lo.astype(BF16)
        h_hi = h_hi.astype(BF16)
        gate = _dot(h_lo, sg_ref[0:half, :]) + _dot(h_hi, sg_ref[half:, :])
        up = _dot(h_lo, su_ref[0:half, :]) + _dot(h_hi, su_ref[half:, :])
        act = gate * jax.nn.sigmoid(gate) * up
        shared = _dot(act.astype(BF16), sd_ref[...])
        gt = gt_ref[...]
        r_lo = jnp.zeros((tm, half), F32)
        r_hi = r_lo
        for k in range(TOP_K):
            e_lo, e_hi = _unpack_halves(_load_token_tiles(buf.at[k], tm))
            r_lo = r_lo + gt[:, k:k + 1] * e_lo
            r_hi = r_hi + gt[:, k:k + 1] * e_hi
        routed = jnp.concatenate([r_lo, r_hi], axis=1)
        x = x_ref[...] + m[5:6] * (routed + shared)
        if final:
            x = _rms(x, fn_ref[...])
        o_ref[...] = x

    @pl.when(i == 0)
    def _():
        fetch(pos_ref, buf0, 0)

    for parity, cur, nxt in ((0, buf0, buf1), (1, buf1, buf0)):
        @pl.when(i % 2 == parity)
        def _(parity=parity, cur=cur, nxt=nxt):
            @pl.when(i + 1 < n)
            def _():
                fetch(nxt_ref, nxt, 1 - parity)

            arrived(cur, parity)
            compute(cur)


def _shared(x, h, gates_t, pos, eo, mod, sg, su, sd, final_norm, n_ctx_rows, dec_seq, *, final, row0=0, rows=None):
    rows = x.shape[0] if rows is None else rows
    tm = ROW_TILE // 2
    b0 = row0 // tm
    steps = rows // tm
    grp = lambda i: _group_of(i + b0, tm, n_ctx_rows, dec_seq)
    row = pl.BlockSpec((tm, D_MODEL), lambda i: (i + b0, 0))
    prow = pl.BlockSpec((tm * TOKEN_ROWS, LANES), lambda i: (i + b0, 0))
    full = lambda a, b: pl.BlockSpec((a, b), lambda i: (0, 0))
    buf = pltpu.VMEM((TOP_K, tm * TOKEN_ROWS, LANES), eo.dtype)
    return pl.pallas_call(
        functools.partial(_shared_kernel, final=final),
        grid=(steps,),
        in_specs=[pl.BlockSpec((SUBLANES, tm), lambda i: (0, i + b0), memory_space=pltpu.SMEM),
                  pl.BlockSpec((SUBLANES, tm), lambda i: (0, jnp.minimum(i + 1, steps - 1) + b0),
                               memory_space=pltpu.SMEM),
                  row, prow, pl.BlockSpec((tm, SUBLANES), lambda i: (i + b0, 0)),
                  pl.BlockSpec((1, 6, D_MODEL), lambda i: (grp(i), 0, 0)),
                  full(D_MODEL, D_SHARED), full(D_MODEL, D_SHARED), full(D_SHARED, D_MODEL), full(1, D_MODEL),
                  pl.BlockSpec(memory_space=pl.ANY)],
        out_specs=pl.BlockSpec((tm, D_MODEL), lambda i: (i, 0)),
        out_shape=jax.ShapeDtypeStruct((rows, D_MODEL), F32),
        scratch_shapes=[buf, buf, pltpu.SemaphoreType.DMA((2,))],
        compiler_params=_cparams(("arbitrary",)),
        name="shared_final" if final else "shared",
    )(pos, pos, x, h, gates_t, mod, sg.astype(BF16), su.astype(BF16), sd.astype(BF16),
      final_norm.reshape(1, D_MODEL), eo)


def kernel(x_prompt, x_sample, state_ret, state_s5_re, state_s5_im, c, c_ctx, w_ada, b_ada, norm_mix, norm_ffn, w_in, w_out, ret_decay, s5_lam_re, s5_lam_im, s5_log_dt, s5_b_re, s5_b_im, s5_c_re, s5_c_im, s5_d, s5_glu_w, s5_glu_b, hy_conv_w, hy_conv_b, hy_f1_w, hy_f1_b, hy_f2_w, hy_f2_b, hy_f3_w, hy_f3_b, hy_freq, hy_decay, hy_bias, moe_router, moe_router_bias, moe_w_gate, moe_w_up, moe_w_down, sh_w_gate, sh_w_up, sh_w_down, final_norm):
    n_ctx, seq, d = x_prompt.shape
    n_dec, dec_seq, _ = x_sample.shape
    n_ctx_rows = n_ctx * seq
    t = n_ctx_rows + n_dec * dec_seq

    x = jnp.concatenate([x_prompt.reshape(n_ctx_rows, d), x_sample.reshape(n_dec * dec_seq, d)], axis=0)
    cond = jnp.concatenate([c_ctx[None, :], c], axis=0)
    cond8 = jnp.pad(cond, ((0, SUBLANES - cond.shape[0]), (0, 0)))
    mods = _ada(cond8, w_ada, b_ada)[:, :1 + n_dec].reshape(DEPTH, 1 + n_dec, 6, d)

    cos2, sin2 = _rope_tables(dec_seq)
    no_rope = jnp.zeros((seq, LANES), F32)
    zero_ret = jnp.zeros((n_ctx, 2, RET_HEADS, RET_DK, RET_DV), F32)
    zero_s5 = jnp.zeros((n_ctx, 2, S5_GROUPS, S5_STATE), F32)
    dft_ctx = _dft_mats(seq)
    dft_dec = _dft_mats(dec_seq)

    ret_list, s5r_list, s5i_list = [], [], []
    for l in range(DEPTH):
        mod = mods[l]
        proj = _in_proj(x, mod, norm_mix[l], w_in[l].astype(BF16), n_ctx_rows, dec_seq)

        log_gamma = jax.nn.log_sigmoid(ret_decay[l].astype(F32))
        ret_o, ret_s = _retention(proj, log_gamma, zero_ret, no_rope, no_rope,
                                  row0=0, n_seq=n_ctx, seq_len=seq, hb=RET_HEADS, rope=False,
                                  dst=jnp.zeros((t, RET_WIDTH), BF16))
        ret_o, _ = _retention(proj, log_gamma, state_ret[:, l].astype(F32), cos2, sin2,
                              row0=n_ctx_rows, n_seq=n_dec, seq_len=dec_seq, hb=1, rope=True, dst=ret_o)
        ret_list.append(ret_s)

        mats = _s5_mats(s5_lam_re[l], s5_lam_im[l], s5_log_dt[l], s5_b_re[l], s5_b_im[l],
                        s5_c_re[l], s5_c_im[l], s5_d[l])
        s5_y, s5_re, s5_im = _s5(proj, mats, zero_s5, zero_s5, row0=0, n_seq=n_ctx, seq_len=seq,
                                 dst=jnp.zeros((t, S5_WIDTH), F32))
        s5_y, _, _ = _s5(proj, mats, state_s5_re[:, l], state_s5_im[:, l],
                         row0=n_ctx_rows, n_seq=n_dec, seq_len=dec_seq, dst=s5_y)
        s5r_list.append(s5_re)
        s5i_list.append(s5_im)

        filt = (hy_f1_w[l], hy_f1_b[l], hy_f2_w[l], hy_f2_b[l], hy_f3_w[l], hy_f3_b[l], hy_freq[l], hy_decay[l])
        hy_o = _hyena(proj, hy_conv_w[l], hy_conv_b[l], hy_bias[l], dft_ctx, _hy_filter_mults(seq, *filt),
                      row0=0, n_seq=n_ctx, seq_len=seq, cb=HY_WIDTH, tk=seq, dst=jnp.zeros((t, HY_WIDTH), BF16))
        hy_o = _hyena(proj, hy_conv_w[l], hy_conv_b[l], hy_bias[l], dft_dec, _hy_filter_mults(dec_seq, *filt),
                      row0=n_ctx_rows, n_seq=n_dec, seq_len=dec_seq, cb=HY_WIDTH // 2, tk=512, dst=hy_o)

        x, h2, logits = _out_proj(x, ret_o, s5_y, hy_o, mod, norm_ffn[l], s5_glu_w[l], s5_glu_b[l],
                                  w_out[l], moe_router[l], n_ctx_rows, dec_seq)

        nb = -(-(t * TOP_K) // MOE_BM) + N_EXPERTS
        pos, gates, blk_e_row, info = _router(logits, moe_router_bias[l], nb)
        blk_e, first, slot, nxt, n_used = _dispatch_plan(blk_e_row, info, nb)
        xs = _dispatch(h2, pos, nb * MOE_BM)
        eo = _moe_grouped(xs, blk_e, first, slot, nxt, n_used, moe_w_gate, moe_w_up, moe_w_down, l)
        gates_t = gates.T

        sh = (sh_w_gate[l], sh_w_up[l], sh_w_down[l])
        if l < DEPTH - 1:
            x = _shared(x, h2, gates_t, pos, eo, mod, *sh, final_norm, n_ctx_rows, dec_seq, final=False)
        else:
            y_c = _shared(x, h2, gates_t, pos, eo, mod, *sh, final_norm, n_ctx_rows, dec_seq, final=True,
                          row0=0, rows=n_ctx_rows)
            y_d = _shared(x, h2, gates_t, pos, eo, mod, *sh, final_norm, n_ctx_rows, dec_seq, final=True,
                          row0=n_ctx_rows, rows=n_dec * dec_seq)

    return (y_c.reshape(n_ctx, seq, d), y_d.reshape(n_dec, dec_seq, d),
            jnp.stack(ret_list, axis=1), jnp.stack(s5r_list, axis=1), jnp.stack(s5i_list, axis=1))
```

```python
import functools
import math

import jax
import jax.numpy as jnp
from jax import lax
from jax.experimental import pallas as pl
from jax.experimental.pallas import tpu as pltpu

F32 = jnp.float32
BF16 = jnp.bfloat16

D_MODEL = 2048
DEPTH = 2
GRID_W = 64
RET_HEADS = 8
RET_DK = 128
RET_DV = 128
RET_WIDTH = RET_HEADS * RET_DV
RET_CHUNK = 128
ROPE_BASE = 10000.0
S5_WIDTH = 512
S5_GROUP = 16
S5_GROUPS = S5_WIDTH // S5_GROUP
S5_STATE = 64
S5_Q = 16
HY_WIDTH = 512
HY_BANDS = 16
IN_WIDTH = 4 * RET_WIDTH + S5_WIDTH + 3 * HY_WIDTH
U_COL = 4 * RET_WIDTH
HY_COL = U_COL + S5_WIDTH
N_EXPERTS = 64
TOP_K = 6
D_EXPERT = 512
D_SHARED = 512
ROUTED_SCALE = 2.5
EPS = 1e-6

LANES = 128
SUBLANES = 8
VMEM_LIMIT = 56 * 1024 * 1024

ROW_TILE = 512
MOE_BM = 256


def _cparams(sem):
    return pltpu.CompilerParams(dimension_semantics=sem, vmem_limit_bytes=VMEM_LIMIT)


def _dot(a, b):
    return jnp.dot(a, b, preferred_element_type=F32)


def _rms(x, g):
    var = jnp.mean(x * x, axis=-1, keepdims=True)
    return x * lax.rsqrt(var + EPS) * g


def _pack_halves(xb):
    n = xb.shape[1] // 2
    lo = lax.bitcast_convert_type(xb[:, :n].astype(F32), jnp.uint32) >> 16
    hi = lax.bitcast_convert_type(xb[:, n:].astype(F32), jnp.uint32)
    return lo | hi


def _unpack_halves(w):
    lo = lax.bitcast_convert_type(w << 16, F32)
    hi = lax.bitcast_convert_type(w & jnp.uint32(0xFFFF0000), F32)
    return lo, hi


TOKEN_ROWS = D_MODEL // 2 // LANES


def _store_token_tiles(ref, w):
    m = w.shape[0]
    for s in range(TOKEN_ROWS):
        ref[pl.ds(s, m, stride=TOKEN_ROWS), :] = w[:, s * LANES:(s + 1) * LANES]


def _load_token_tiles(ref, m):
    return jnp.concatenate([ref[pl.ds(s, m, stride=TOKEN_ROWS), :] for s in range(TOKEN_ROWS)], axis=1)


def _ada_kernel(c_ref, w_ref, b_ref, o_ref):
    c = c_ref[...]
    s = (c * jax.nn.sigmoid(c)).astype(BF16)
    o_ref[0] = _dot(s, w_ref[0].astype(BF16)) + b_ref[0]


def _ada(cond8, w_ada, b_ada):
    tn = 512
    n = w_ada.shape[-1]
    return pl.pallas_call(
        _ada_kernel,
        grid=(DEPTH, n // tn),
        in_specs=[
            pl.BlockSpec((SUBLANES, D_MODEL), lambda l, j: (0, 0)),
            pl.BlockSpec((1, D_MODEL, tn), lambda l, j: (l, 0, j)),
            pl.BlockSpec((1, 1, tn), lambda l, j: (l, 0, j)),
        ],
        out_specs=pl.BlockSpec((1, SUBLANES, tn), lambda l, j: (l, 0, j)),
        out_shape=jax.ShapeDtypeStruct((DEPTH, SUBLANES, n), F32),
        compiler_params=_cparams(("parallel", "parallel")),
        name="ada",
    )(cond8, w_ada, b_ada.reshape(DEPTH, 1, n))


def _group_of(i, tile, n_ctx_rows, dec_seq):
    ctx_tiles = n_ctx_rows // tile
    per = dec_seq // tile
    return jnp.where(i < ctx_tiles, 0, 1 + (i - ctx_tiles) // per)


def _in_kernel(x_ref, mod_ref, g_ref, w_ref, o_ref, h_scr):
    @pl.when(pl.program_id(1) == 0)
    def _():
        m = mod_ref[0]
        h = _rms(x_ref[...], g_ref[...]) * (1.0 + m[1:2]) + m[0:1]
        h_scr[...] = h.astype(BF16)

    o_ref[...] = _dot(h_scr[...], w_ref[...])


def _in_proj(x, mod, g, w_bf, n_ctx_rows, dec_seq):
    t = x.shape[0]
    tm, tn = 1024, 1024
    grp = functools.partial(_group_of, tile=tm, n_ctx_rows=n_ctx_rows, dec_seq=dec_seq)
    return pl.pallas_call(
        _in_kernel,
        grid=(t // tm, IN_WIDTH // tn),
        in_specs=[
            pl.BlockSpec((tm, D_MODEL), lambda i, j: (i, 0)),
            pl.BlockSpec((1, 6, D_MODEL), lambda i, j: (grp(i), 0, 0)),
            pl.BlockSpec((1, D_MODEL), lambda i, j: (0, 0)),
            pl.BlockSpec((D_MODEL, tn), lambda i, j: (0, j)),
        ],
        out_specs=pl.BlockSpec((tm, tn), lambda i, j: (i, j)),
        out_shape=jax.ShapeDtypeStruct((t, IN_WIDTH), F32),
        scratch_shapes=[pltpu.VMEM((tm, D_MODEL), BF16)],
        compiler_params=_cparams(("parallel", "arbitrary")),
        name="in_proj",
    )(x, mod, g.reshape(1, D_MODEL), w_bf)


def _ret_kernel(lg_ref, q_ref, k_ref, v_ref, gt_ref, cos_ref, sin_ref, s0_ref,
                o_ref, sfin_ref, acc_scr, q_scr, k_scr, *, seq_len, hb, rope):
    c = RET_CHUNK
    n_chunks = seq_len // c
    ii = lax.broadcasted_iota(jnp.int32, (c, c), 0)
    jj = lax.broadcasted_iota(jnp.int32, (c, c), 1)
    rel = (ii - jj).astype(F32)
    ci = lax.broadcasted_iota(jnp.int32, (c, 1), 0).astype(F32)
    one = jnp.ones((1, 1), F32)
    tdot = functools.partial(lax.dot_general, preferred_element_type=F32)

    for hh in range(hb):
        head = pl.program_id(1) * hb + hh
        lgf = lg_ref[0, head]
        lgb = lg_ref[1, head]
        dmask = (jnp.where(rel >= 0, jnp.exp(lgf * jnp.maximum(rel, 0.0)), 0.0)
                 + jnp.where(rel <= 0, jnp.exp(lgb * jnp.maximum(-rel, 0.0)), 0.0))
        qd_f = jnp.exp(lgf * (ci + 1.0))
        kd_f = jnp.exp(lgf * (c - 1.0 - ci))
        cd_f = jnp.exp(lgf * c * one)
        qd_b = jnp.exp(lgb * (c - ci))
        kd_b = jnp.exp(lgb * ci)
        cd_b = jnp.exp(lgb * c * one)
        lanes = slice(hh * LANES, (hh + 1) * LANES)

        def rows_of(n):
            if isinstance(n, int):
                return slice(n * c, (n + 1) * c)
            return pl.ds(pl.multiple_of(n * c, c), c)

        def fwd_chunk(n, s_f):
            rows = rows_of(n)
            q = q_ref[rows, lanes]
            k = k_ref[rows, lanes] * (RET_DK ** -0.5)
            if rope:
                cs = cos_ref[rows, :]
                sn = sin_ref[rows, :]
                q = q * cs + pltpu.roll(q, RET_DK // 2, 1) * sn
                k = k * cs + pltpu.roll(k, RET_DK // 2, 1) * sn
            qb = q.astype(BF16)
            vb = v_ref[rows, lanes].astype(BF16)
            q_scr[rows, :] = qb
            k_scr[rows, :] = k
            scores = tdot(qb, k.astype(BF16), (((1,), (1,)), ((), ()))) * dmask
            inner = _dot(scores.astype(BF16), vb)
            cross = _dot(qb, s_f.astype(BF16)) * qd_f
            acc_scr[rows, :] = inner + cross
            upd = tdot((k * kd_f).astype(BF16), vb, (((0,), (0,)), ((), ())))
            return s_f * cd_f + upd

        def bwd_chunk(m, s_b):
            n = n_chunks - 1 - m
            rows = rows_of(n)
            qb = q_scr[rows, :]
            k = k_scr[rows, :]
            vb = v_ref[rows, lanes].astype(BF16)
            o = acc_scr[rows, :] + _dot(qb, s_b.astype(BF16)) * qd_b
            mu = jnp.mean(o, axis=-1, keepdims=True)
            oc = o - mu
            var = jnp.mean(oc * oc, axis=-1, keepdims=True)
            o = oc * lax.rsqrt(var + EPS)
            g = gt_ref[rows, lanes]
            o_ref[rows, lanes] = (g * jax.nn.sigmoid(g) * o).astype(o_ref.dtype)
            upd = tdot((k * kd_b).astype(BF16), vb, (((0,), (0,)), ((), ())))
            return s_b * cd_b + upd

        s_f = s0_ref[0, 0, hh]
        s_b = s0_ref[0, 1, hh]
        if n_chunks <= 4:
            for n in range(n_chunks):
                s_f = fwd_chunk(n, s_f)
            for m in range(n_chunks):
                s_b = bwd_chunk(m, s_b)
        else:
            s_f = lax.fori_loop(0, n_chunks, fwd_chunk, s_f)
            s_b = lax.fori_loop(0, n_chunks, bwd_chunk, s_b)
        sfin_ref[0, 0, hh] = s_f
        sfin_ref[0, 1, hh] = s_b


def _into(kernel_fn, n_in, dst):
    dsts = [d for d in (dst if isinstance(dst, (list, tuple)) else [dst])]
    outs = [k for k, d in enumerate(dsts) if d is not None]
    if not outs:
        return kernel_fn, [], [], {}

    def body(*refs):
        return kernel_fn(*refs[:n_in], *refs[n_in + len(outs):])

    return (body, [pl.BlockSpec(memory_space=pl.ANY)] * len(outs), [dsts[k] for k in outs],
            {n_in + pos: k for pos, k in enumerate(outs)})


def _retention(proj, log_gamma, s0, cos2, sin2, *, row0, n_seq, seq_len, hb, rope, dst=None,
               states=None, layer=0):
    blk0 = row0 // seq_len
    body, dst_spec, dst_arg, alias = _into(
        functools.partial(_ret_kernel, seq_len=seq_len, hb=hb, rope=rope), 8, [dst, states])
    if states is None:
        st_spec = pl.BlockSpec((1, 2, hb, RET_DK, RET_DV), lambda b, h, lg: (b, 0, h, 0, 0))
        st_shape = jax.ShapeDtypeStruct((n_seq, 2, RET_HEADS, RET_DK, RET_DV), F32)
    else:
        st_spec = pl.BlockSpec((1, None, 2, hb, RET_DK, RET_DV), lambda b, h, lg: (b, layer, 0, h, 0, 0))
        st_shape = jax.ShapeDtypeStruct(states.shape, F32)
    w = hb * LANES
    hblocks = RET_HEADS // hb
    col = lambda part: (lambda b, h, lg: (blk0 + b, part * hblocks + h))
    grid_spec = pltpu.PrefetchScalarGridSpec(
        num_scalar_prefetch=1,
        grid=(n_seq, hblocks),
        in_specs=[
            pl.BlockSpec((seq_len, w), col(0)),
            pl.BlockSpec((seq_len, w), col(1)),
            pl.BlockSpec((seq_len, w), col(2)),
            pl.BlockSpec((seq_len, w), col(3)),
            pl.BlockSpec((seq_len, LANES), lambda b, h, lg: (0, 0)),
            pl.BlockSpec((seq_len, LANES), lambda b, h, lg: (0, 0)),
            pl.BlockSpec((1, 2, hb, RET_DK, RET_DV), lambda b, h, lg: (b, 0, h, 0, 0)),
        ] + dst_spec,
        out_specs=[
            pl.BlockSpec((seq_len, w), lambda b, h, lg: (blk0 + b, h)),
            st_spec,
        ],
        scratch_shapes=[
            pltpu.VMEM((seq_len, LANES), F32),
            pltpu.VMEM((seq_len, LANES), BF16),
            pltpu.VMEM((seq_len, LANES), F32),
        ],
    )
    return pl.pallas_call(
        body,
        grid_spec=grid_spec,
        out_shape=[jax.ShapeDtypeStruct((proj.shape[0], RET_WIDTH), BF16), st_shape],
        input_output_aliases=alias,
        compiler_params=_cparams(("parallel", "arbitrary")),
        name="retention",
    )(log_gamma, proj, proj, proj, proj, cos2, sin2, s0, *dst_arg)


def _rope_tables(seq_len):
    rows_n = seq_len // GRID_W
    rows = jnp.repeat(jnp.arange(rows_n, dtype=F32), GRID_W)
    cols = jnp.tile(jnp.arange(GRID_W, dtype=F32), rows_n)
    nf = RET_DK // 4
    inv = ROPE_BASE ** (-jnp.arange(nf, dtype=F32) / nf)
    ang = jnp.concatenate([rows[:, None] * inv, cols[:, None] * inv], axis=-1)
    cs, sn = jnp.cos(ang), jnp.sin(ang)
    return jnp.concatenate([cs, cs], axis=-1), jnp.concatenate([-sn, sn], axis=-1)


S5_GB = LANES // S5_GROUP
S5_W = S5_Q * LANES
S5_SPLIT = 4
S5_SW = S5_GB * S5_STATE
S5_SB = S5_SW // LANES


def _s5_kernel(u_ref, bq_ref, k_ref, cq_ref, ar_ref, ai_ref, d_ref, h0_ref,
               y_ref, hf_ref, ub_scr, sm_scr, hp_scr, hpb_scr, t_scr, *, n_seq, n_chunks):
    s = pl.program_id(1)
    m = n_seq * n_chunks
    q = S5_Q
    nblk = S5_W // LANES
    sb = S5_SB

    per = q // S5_SPLIT
    kw = S5_W // S5_SPLIT

    @pl.when(s == 0)
    def _():
        for j in range(q):
            ub_scr[j // per, :, (j % per) * LANES:(j % per + 1) * LANES] = (
                u_ref[pl.ds(j, m, stride=q), :].astype(BF16))
        for ib in range(S5_SPLIT):
            for j in range(q):
                c0 = (q - 1 - j) * LANES + ib * kw
                t_scr[ib, j * LANES:(j + 1) * LANES, :] = k_ref[0, :, c0:c0 + kw]

    @pl.when(s < S5_SPLIT)
    def _():
        part = _dot(ub_scr[jnp.minimum(s, S5_SPLIT - 1)], bq_ref[0])

        @pl.when(s == 0)
        def _():
            for cb in range(nblk):
                sm_scr[cb] = part[:, cb * LANES:(cb + 1) * LANES]

        @pl.when(s > 0)
        def _():
            for cb in range(nblk):
                sm_scr[cb] += part[:, cb * LANES:(cb + 1) * LANES]

    @pl.when(s == S5_SPLIT - 1)
    def _():
        ar = ar_ref[0]
        ai = ai_ref[0]
        h0 = h0_ref[0]
        blk = lambda a, cb: a[:, cb * LANES:(cb + 1) * LANES]

        def body(n, carry):
            rows_f = pl.ds(n, n_seq, stride=n_chunks)
            rows_b = pl.ds(n_chunks - 1 - n, n_seq, stride=n_chunks)
            new = list(carry)
            for d, rows in ((0, rows_f), (1, rows_b)):
                for c in range(sb):
                    re_i = d * sb + c
                    im_i = (2 + d) * sb + c
                    hr, hi = carry[re_i], carry[im_i]
                    hp_scr[re_i, rows, :] = hr
                    hp_scr[im_i, rows, :] = hi
                    a_r, a_i = blk(ar, re_i), blk(ai, re_i)
                    new[re_i] = a_r * hr - a_i * hi + sm_scr[re_i, rows, :]
                    new[im_i] = a_r * hi + a_i * hr + sm_scr[im_i, rows, :]
            return tuple(new)

        fin = lax.fori_loop(0, n_chunks, body, tuple(blk(h0, cb) for cb in range(nblk)))
        hf_ref[0] = jnp.concatenate(fin, axis=1)
        for cb in range(nblk):
            hpb_scr[:, cb * LANES:(cb + 1) * LANES] = hp_scr[cb].astype(BF16)

    @pl.when(s >= S5_SPLIT)
    def _():
        ub = jnp.concatenate([ub_scr[k] for k in range(S5_SPLIT)], axis=1)
        y = _dot(ub, t_scr[jnp.maximum(s - S5_SPLIT, 0)]) + _dot(hpb_scr[...], cq_ref[0])
        dd = d_ref[0]
        for ii in range(per):
            rows = pl.ds((s - S5_SPLIT) * per + ii, m, stride=q)
            yi = y[:, ii * LANES:(ii + 1) * LANES] + dd * u_ref[rows, :]
            y_ref[rows, :] = jax.nn.gelu(yi)


def _s5_expand_kernel(mc_ref, o_ref, *, xsize, ysize):
    xs, ys, gs = xsize.bit_length() - 1, ysize.bit_length() - 1, S5_GB.bit_length() - 1
    assert xsize == 1 << xs and ysize == 1 << ys and S5_GB == 1 << gs
    cw = o_ref.shape[2]
    nc = mc_ref.shape[2]
    col0 = pl.program_id(1) * cw
    r = lax.broadcasted_iota(jnp.int32, (nc, cw), 0)
    col = lax.broadcasted_iota(jnp.int32, (nc, cw), 1) + col0
    spread = jnp.logical_and(r >> ys == col >> (ys + gs), (r & (ysize - 1)) == (col & (ysize - 1)))
    big = _dot(mc_ref[0], jnp.where(spread, 1.0, 0.0).astype(BF16))
    row = lax.broadcasted_iota(jnp.int32, big.shape, 0)
    colb = lax.broadcasted_iota(jnp.int32, big.shape, 1) + col0
    same = ((row >> xs) & (S5_GB - 1)) == ((colb >> ys) & (S5_GB - 1))
    o_ref[0] = jnp.where(same, big, 0.0).astype(BF16)


def _s5_expand(mc, *, xsize, ysize):
    nb, rows, nc = mc.shape
    cols = nc * S5_GB
    cw = 512
    return pl.pallas_call(
        functools.partial(_s5_expand_kernel, xsize=xsize, ysize=ysize),
        grid=(nb, cols // cw),
        in_specs=[pl.BlockSpec((1, rows, nc), lambda b, j: (b, 0, 0))],
        out_specs=pl.BlockSpec((1, rows, cw), lambda b, j: (b, 0, j)),
        out_shape=jax.ShapeDtypeStruct((nb, rows, cols), BF16),
        compiler_params=_cparams(("parallel", "parallel")),
        name="s5_expand",
    )(mc)


def _s5_mats(lam_re, lam_im, log_dt, b_re, b_im, c_re, c_im, d):
    q, g, p, ch = S5_Q, S5_GROUPS, S5_STATE, S5_GROUP
    lam = lax.complex(jnp.minimum(lam_re.astype(F32), -1e-4), lam_im.astype(F32))
    ldt = lam * jnp.exp(log_dt.astype(F32))[..., None]
    lam_bar = jnp.exp(ldt)
    b_bar = ((lam_bar - 1.0) / lam)[..., None] * lax.complex(b_re.astype(F32), b_im.astype(F32))
    cc = lax.complex(c_re.astype(F32), c_im.astype(F32))
    pw = jnp.exp(ldt[..., None] * jnp.arange(q + 1, dtype=F32))
    hi = lax.Precision.HIGHEST
    lag = jnp.arange(2 * q, dtype=F32) - (q - 1)
    wf = jnp.where(lag >= 0, jnp.exp(ldt[0][..., None] * jnp.maximum(lag, 0.0)), 0.0)
    wb = jnp.where(lag <= 0, jnp.exp(ldt[1][..., None] * jnp.maximum(-lag, 0.0)), 0.0)
    kc = jnp.real(jnp.einsum('gcp,gpd,gpe->gedc', cc[0], wf, b_bar[0], precision=hi)
                  + jnp.einsum('gcp,gpd,gpe->gedc', cc[1], wb, b_bar[1], precision=hi))

    pw_dn = jnp.exp(ldt[..., None] * (q - jnp.arange(q + 1, dtype=F32)))
    bf = pw_dn[0][..., 1:][:, :, :, None] * b_bar[0][:, :, None, :]
    bb = pw[1][..., :q][:, :, :, None] * b_bar[1][:, :, None, :]
    to_rows = lambda m: m.transpose(0, 2, 3, 1).reshape(g, q * ch, p)
    bq = jnp.concatenate([to_rows(jnp.real(bf)), to_rows(jnp.real(bb)),
                          to_rows(jnp.imag(bf)), to_rows(jnp.imag(bb))], axis=-1)

    cf = cc[0].transpose(0, 2, 1)[:, :, None, :] * pw[0][..., 1:][:, :, :, None]
    cb = cc[1].transpose(0, 2, 1)[:, :, None, :] * pw_dn[1][..., :q][:, :, :, None]
    to_cols = lambda m: m.reshape(g, p, q * ch)
    cq = jnp.concatenate([to_cols(jnp.real(cf)), to_cols(jnp.real(cb)),
                          to_cols(-jnp.imag(cf)), to_cols(-jnp.imag(cb))], axis=1)

    gb, nb = S5_GB, g // S5_GB
    rows_of = lambda a, outer, inner: (a.reshape(nb, gb, outer, inner, a.shape[-1]).transpose(0, 2, 1, 3, 4)
                                       .reshape(nb, outer * gb * inner, a.shape[-1]).astype(BF16))
    tm_bd = _s5_expand(kc.reshape(nb, gb * ch, 2 * q * ch).astype(BF16), xsize=ch, ysize=ch)
    bq_bd = _s5_expand(rows_of(bq, q, ch), xsize=ch, ysize=p)
    cq_bd = _s5_expand(rows_of(cq, 4, p), xsize=p, ysize=ch)

    lq = pw[..., q].reshape(2, nb, 1, S5_SW)
    ar = jnp.concatenate([jnp.real(lq[0]), jnp.real(lq[1])], axis=-1)
    ai = jnp.concatenate([jnp.imag(lq[0]), jnp.imag(lq[1])], axis=-1)
    dd = d.astype(F32).reshape(nb, 1, LANES)
    return tm_bd, bq_bd, cq_bd, ar, ai, dd


def _s5(proj, mats, h0_re, h0_im, *, row0, n_seq, seq_len, dst=None):
    q, p = S5_Q, S5_STATE
    n_chunks = seq_len // q
    m = n_seq * n_chunks
    rows = n_seq * seq_len
    nb = S5_GROUPS // S5_GB
    nblk = S5_W // LANES
    kw = S5_W // S5_SPLIT
    tm, bq, cq, ar, ai, dd = mats
    part = lambda a: a.astype(F32).reshape(n_seq, nb, S5_SW)
    h0 = jnp.concatenate([part(h0_re[:, 0]), part(h0_re[:, 1]), part(h0_im[:, 0]), part(h0_im[:, 1])],
                         axis=-1).transpose(1, 0, 2)
    per_b = lambda shape: pl.BlockSpec((1,) + shape, lambda b, s: (b, 0, 0))
    body, dst_spec, dst_arg, alias = _into(functools.partial(_s5_kernel, n_seq=n_seq, n_chunks=n_chunks), 8, dst)
    y, hf = pl.pallas_call(
        body,
        grid=(nb, 2 * S5_SPLIT),
        in_specs=[
            pl.BlockSpec((rows, LANES), lambda b, s: (row0 // rows, U_COL // LANES + b)),
            pl.BlockSpec((1, kw, 4 * S5_SW), lambda b, s: (b, jnp.minimum(s, S5_SPLIT - 1), 0)),
            per_b((LANES, 2 * q * LANES)),
            pl.BlockSpec((1, 4 * S5_SW, kw), lambda b, s: (b, 0, jnp.maximum(s - S5_SPLIT, 0))),
            per_b((1, 2 * S5_SW)), per_b((1, 2 * S5_SW)), per_b((1, LANES)), per_b((n_seq, 4 * S5_SW)),
        ] + dst_spec,
        out_specs=[pl.BlockSpec((rows, LANES), lambda b, s: (row0 // rows, b)), per_b((n_seq, 4 * S5_SW))],
        out_shape=[jax.ShapeDtypeStruct((proj.shape[0], S5_WIDTH), F32),
                   jax.ShapeDtypeStruct((nb, n_seq, 4 * S5_SW), F32)],
        scratch_shapes=[pltpu.VMEM((S5_SPLIT, m, kw), BF16), pltpu.VMEM((nblk, m, LANES), F32),
                        pltpu.VMEM((nblk, m, LANES), F32), pltpu.VMEM((m, 4 * S5_SW), BF16),
                        pltpu.VMEM((S5_SPLIT, S5_W, kw), BF16)],
        input_output_aliases=alias,
        compiler_params=_cparams(("parallel", "arbitrary")),
        name="s5",
    )(proj, bq, tm, cq, ar, ai, dd, h0, *dst_arg)
    hf = hf.reshape(nb, n_seq, 4, S5_GB, p).transpose(1, 2, 0, 3, 4).reshape(n_seq, 4, S5_GROUPS, p)
    return y, hf[:, 0:2], hf[:, 2:4]


def _conv3(x, w, b):
    n = x.shape[0]
    row = lax.broadcasted_iota(jnp.int32, x.shape, 0)
    prev = jnp.where(row == 0, 0.0, pltpu.roll(x, 1, 0))
    nxt = jnp.where(row == n - 1, 0.0, pltpu.roll(x, n - 1, 0))
    return prev * w[0:1] + x * w[1:2] + nxt * w[2:3] + b


def _hy_fwd_kernel(x0_ref, x1_ref, v_ref, w0_ref, w1_ref, wv_ref, b0_ref, b1_ref, bv_ref,
                   fc_ref, fs_ref, m1_ref, m2_ref, m3_ref, p_ref, z_ref, x0c_ref, zb_scr):
    @pl.when(pl.program_id(2) == 0)
    def _():
        z = _conv3(x1_ref[...], w1_ref[...], b1_ref[...]) * _conv3(v_ref[...], wv_ref[...], bv_ref[...])
        z_ref[...] = z
        zb_scr[...] = z.astype(BF16)
        x0c_ref[...] = _conv3(x0_ref[...], w0_ref[...], b0_ref[...])

    zb = zb_scr[...]
    a = _dot(fc_ref[...], zb)
    b = _dot(fs_ref[...], zb)
    m2 = m2_ref[...]
    p_ref[0, 0] = (m1_ref[...] * a + m2 * b).astype(BF16)
    p_ref[0, 1] = (m3_ref[...] * b - m2 * a).astype(BF16)


def _hy_inv_kernel(p_ref, gc_ref, gs_ref, z_ref, x0c_ref, bias_ref, o_ref):
    conv = _dot(gc_ref[...], p_ref[0, 0]) + _dot(gs_ref[...], p_ref[0, 1])
    o_ref[...] = (x0c_ref[...] * (conv + bias_ref[...] * z_ref[...])).astype(o_ref.dtype)


def _dft_mats(seq_len):
    n, w = seq_len, 64
    k = jnp.arange(n, dtype=jnp.int32)
    ang = lambda j: ((k[:, None] * j[None, :]) % (2 * n)).astype(F32) * (math.pi / n)
    ang_a = ang(jnp.arange(n // w, dtype=jnp.int32) * w)
    ang_b = ang(jnp.arange(w, dtype=jnp.int32))
    ca, sa = jnp.cos(ang_a)[:, :, None], jnp.sin(ang_a)[:, :, None]
    cb, sb = jnp.cos(ang_b)[:, None, :], jnp.sin(ang_b)[:, None, :]
    cm = (ca * cb - sa * sb).reshape(n, n)
    sm = -(sa * cb + ca * sb).reshape(n, n)
    nyq = jnp.where(k % 2 == 0, 1.0, -1.0).astype(F32)
    return cm.astype(BF16), sm.at[0, :].set(nyq).astype(BF16), sm.at[:, 0].set(nyq).astype(BF16)


def _hy_filter_mults(seq_len, f1_w, f1_b, f2_w, f2_b, f3_w, f3_b, freq, decay):
    n = seq_len
    t = (jnp.arange(n, dtype=F32) / n)[:, None]
    bands = jnp.arange(1, HY_BANDS + 1, dtype=F32)[None, :]
    z = jnp.concatenate([t, jnp.cos(2.0 * math.pi * t * bands), jnp.sin(2.0 * math.pi * t * bands)], axis=-1)
    hi = lax.Precision.HIGHEST
    fr = freq.astype(F32)
    h = jnp.sin(fr * (jnp.dot(z, f1_w.astype(F32), precision=hi) + f1_b.astype(F32)))
    h = jnp.sin(fr * (jnp.dot(h, f2_w.astype(F32), precision=hi) + f2_b.astype(F32)))
    h = jnp.dot(h, f3_w.astype(F32), precision=hi) + f3_b.astype(F32)
    h = h * jnp.exp(-t * jnp.abs(decay.astype(F32)))
    h = h.reshape(n, 2, HY_WIDTH)
    h = h / jnp.sum(jnp.abs(h), axis=(0, 1), keepdims=True)
    hc = jnp.concatenate([h[:, 0], h[::-1, 1]], axis=0)
    spec = jnp.fft.rfft(hc, axis=0)
    hr, him = jnp.real(spec), jnp.imag(spec)
    wk = jnp.where(jnp.arange(n) == 0, 1.0, 2.0)[:, None] / (2.0 * n)
    first = (jnp.arange(n) == 0)[:, None]
    m1 = hr[:n] * wk
    m2 = jnp.where(first, 0.0, -him[:n]) * wk
    m3 = jnp.where(first, hr[n:n + 1], hr[:n]) * wk
    return m1.astype(F32), m2.astype(F32), m3.astype(F32)


def _hyena(proj, conv_w, conv_b, bias, dft, mults, *, row0, n_seq, seq_len, cb, tk, dst=None):
    blk0 = row0 // seq_len
    nc = HY_WIDTH // cb
    nk = seq_len // tk
    c0 = HY_COL // cb
    cm, sm, smt = dft
    m1, m2, m3 = mults
    xcol = lambda part: pl.BlockSpec((seq_len, cb), lambda b, c, k: (blk0 + b, c0 + part * nc + c))
    wcol = lambda part: pl.BlockSpec((3, cb), lambda b, c, k: (0, part * nc + c))
    bcol = lambda part: pl.BlockSpec((1, cb), lambda b, c, k: (0, part * nc + c))
    frow = pl.BlockSpec((tk, seq_len), lambda b, c, k: (k, 0))
    mblk = pl.BlockSpec((tk, cb), lambda b, c, k: (k, c))
    cb2 = conv_b.reshape(1, 3 * HY_WIDTH)
    pspec, z, x0c = pl.pallas_call(
        _hy_fwd_kernel,
        grid=(n_seq, nc, nk),
        in_specs=[xcol(0), xcol(1), xcol(2), wcol(0), wcol(1), wcol(2), bcol(0), bcol(1), bcol(2),
                  frow, frow, mblk, mblk, mblk],
        out_specs=[
            pl.BlockSpec((1, 2, tk, cb), lambda b, c, k: (b, 0, k, c)),
            pl.BlockSpec((seq_len, cb), lambda b, c, k: (b, c)),
            pl.BlockSpec((seq_len, cb), lambda b, c, k: (b, c)),
        ],
        out_shape=[
            jax.ShapeDtypeStruct((n_seq, 2, seq_len, HY_WIDTH), BF16),
            jax.ShapeDtypeStruct((n_seq * seq_len, HY_WIDTH), F32),
            jax.ShapeDtypeStruct((n_seq * seq_len, HY_WIDTH), F32),
        ],
        scratch_shapes=[pltpu.VMEM((seq_len, cb), BF16)],
        compiler_params=_cparams(("parallel", "parallel", "arbitrary")),
        name="hyena_fwd",
    )(proj, proj, proj, conv_w, conv_w, conv_w, cb2, cb2, cb2, cm, sm, m1, m2, m3)
    grow = pl.BlockSpec((tk, seq_len), lambda b, c, k: (k, 0))
    tile = pl.BlockSpec((tk, cb), lambda b, c, k: (b * nk + k, c))
    body, dst_spec, dst_arg, alias = _into(_hy_inv_kernel, 6, dst)
    return pl.pallas_call(
        body,
        grid=(n_seq, nc, nk),
        in_specs=[pl.BlockSpec((1, 2, seq_len, cb), lambda b, c, k: (b, 0, 0, c)),
                  grow, grow, tile, tile, pl.BlockSpec((1, cb), lambda b, c, k: (0, c))] + dst_spec,
        out_specs=pl.BlockSpec((tk, cb), lambda b, c, k: (row0 // tk + b * nk + k, c)),
        out_shape=jax.ShapeDtypeStruct((proj.shape[0], HY_WIDTH), BF16),
        input_output_aliases=alias,
        compiler_params=_cparams(("parallel", "parallel", "arbitrary")),
        name="hyena_inv",
    )(pspec, cm, smt, z, x0c, bias.reshape(1, HY_WIDTH), *dst_arg)


def _out_kernel(x_ref, ret_ref, s5_ref, hy_ref, mod_ref, g_ref, gw_ref, gb_ref,
                wr_ref, ws_ref, wh_ref, rt_ref, xo_ref, h_ref, lg_ref):
    m = mod_ref[0]
    y = s5_ref[...]
    s5o = y * jax.nn.sigmoid(_dot(y.astype(BF16), gw_ref[...]) + gb_ref[...])
    mix = (_dot(ret_ref[...], wr_ref[...]) + _dot(s5o.astype(BF16), ws_ref[...])
           + _dot(hy_ref[...], wh_ref[...]))
    x = x_ref[...] + m[2:3] * mix
    xo_ref[...] = x
    h = _rms(x, g_ref[...]) * (1.0 + m[4:5]) + m[3:4]
    hb = h.astype(BF16)
    _store_token_tiles(h_ref, _pack_halves(hb))
    lg_ref[...] = lax.dot_general(rt_ref[...], hb, (((1,), (1,)), ((), ())), preferred_element_type=F32)


def _out_proj(x, ret_o, s5_y, hy_o, mod, g, glu_w, glu_b, w_out, router, n_ctx_rows, dec_seq):
    t = x.shape[0]
    tm = ROW_TILE
    grp = functools.partial(_group_of, tile=tm, n_ctx_rows=n_ctx_rows, dec_seq=dec_seq)
    row = lambda w: pl.BlockSpec((tm, w), lambda i: (i, 0))
    full = lambda a, b: pl.BlockSpec((a, b), lambda i: (0, 0))
    wo = w_out.astype(BF16)
    return pl.pallas_call(
        _out_kernel,
        grid=(t // tm,),
        in_specs=[row(D_MODEL), row(RET_WIDTH), row(S5_WIDTH), row(HY_WIDTH),
                  pl.BlockSpec((1, 6, D_MODEL), lambda i: (grp(i), 0, 0)),
                  full(1, D_MODEL), full(S5_WIDTH, S5_WIDTH), full(1, S5_WIDTH),
                  full(RET_WIDTH, D_MODEL), full(S5_WIDTH, D_MODEL), full(HY_WIDTH, D_MODEL),
                  full(N_EXPERTS, D_MODEL)],
        out_specs=[row(D_MODEL), pl.BlockSpec((tm * TOKEN_ROWS, LANES), lambda i: (i, 0)),
                   pl.BlockSpec((N_EXPERTS, tm), lambda i: (0, i))],
        out_shape=[jax.ShapeDtypeStruct((t, D_MODEL), F32), jax.ShapeDtypeStruct((t * TOKEN_ROWS, LANES), jnp.uint32),
                   jax.ShapeDtypeStruct((N_EXPERTS, t), F32)],
        compiler_params=_cparams(("parallel",)),
        name="out_proj",
    )(x, ret_o, s5_y, hy_o, mod, g.reshape(1, D_MODEL), glu_w.astype(BF16), glu_b.reshape(1, S5_WIDTH),
      wo[:RET_WIDTH], wo[RET_WIDTH:RET_WIDTH + S5_WIDTH], wo[RET_WIDTH + S5_WIDTH:], router.T.astype(BF16))


def _moe_kernel(be_ref, first_ref, slot_ref, nxt_ref, nu_ref, xs_ref, wg_hbm, wu_hbm, wd_hbm, o_ref,
                wg_f, wu_f, wd_f, wg_b, wu_b, wd_b, sem, *, layer):
    i = pl.program_id(0)

    def copies(e, s):
        return (pltpu.make_async_copy(wg_hbm.at[layer, e], wg_f.at[s], sem.at[s, 0]),
                pltpu.make_async_copy(wu_hbm.at[layer, e], wu_f.at[s], sem.at[s, 1]),
                pltpu.make_async_copy(wd_hbm.at[layer, e], wd_f.at[s], sem.at[s, 2]))

    @pl.when(i == 0)
    def _():
        for cp in copies(be_ref[0], 0):
            cp.start()

    @pl.when(first_ref[i] == 1)
    def _():
        s = slot_ref[i]
        for cp in copies(be_ref[i], s):
            cp.wait()

        @pl.when(nxt_ref[i] >= 0)
        def _():
            for cp in copies(nxt_ref[i], 1 - s):
                cp.start()

        wg_b[...] = wg_f[s].astype(BF16)
        wu_b[...] = wu_f[s].astype(BF16)
        wd_b[...] = wd_f[s].astype(BF16)

    @pl.when(i < nu_ref[0])
    def _():
        half = D_MODEL // 2
        x_lo, x_hi = _unpack_halves(_load_token_tiles(xs_ref, MOE_BM))
        x_lo = x_lo.astype(BF16)
        x_hi = x_hi.astype(BF16)
        gate = _dot(x_lo, wg_b[0:half, :]) + _dot(x_hi, wg_b[half:, :])
        up = _dot(x_lo, wu_b[0:half, :]) + _dot(x_hi, wu_b[half:, :])
        hb = gate * jax.nn.sigmoid(gate) * up
        _store_token_tiles(o_ref, _pack_halves(_dot(hb.astype(BF16), wd_b[...]).astype(BF16)))

    @pl.when(i >= nu_ref[0])
    def _():
        o_ref[...] = jnp.zeros_like(o_ref)


def _moe_grouped(xs, blk_e, first, slot, nxt, n_used, w_gate, w_up, w_down, layer):
    pr = xs.shape[0] // TOKEN_ROWS
    bm = MOE_BM
    nb = pr // bm
    grid_spec = pltpu.PrefetchScalarGridSpec(
        num_scalar_prefetch=5,
        grid=(nb,),
        in_specs=[
            pl.BlockSpec((bm * TOKEN_ROWS, LANES), lambda i, *_: (i, 0)),
            pl.BlockSpec(memory_space=pl.ANY),
            pl.BlockSpec(memory_space=pl.ANY),
            pl.BlockSpec(memory_space=pl.ANY),
        ],
        out_specs=pl.BlockSpec((bm * TOKEN_ROWS, LANES), lambda i, *_: (i, 0)),
        scratch_shapes=[pltpu.VMEM((2, D_MODEL, D_EXPERT), F32), pltpu.VMEM((2, D_MODEL, D_EXPERT), F32),
                        pltpu.VMEM((2, D_EXPERT, D_MODEL), F32),
                        pltpu.VMEM((D_MODEL, D_EXPERT), BF16), pltpu.VMEM((D_MODEL, D_EXPERT), BF16),
                        pltpu.VMEM((D_EXPERT, D_MODEL), BF16),
                        pltpu.SemaphoreType.DMA((2, 3))],
    )
    return pl.pallas_call(
        functools.partial(_moe_kernel, layer=layer),
        grid_spec=grid_spec,
        out_shape=jax.ShapeDtypeStruct((pr * TOKEN_ROWS, LANES), jnp.uint32),
        compiler_params=_cparams(("arbitrary",)),
        name="moe_grouped",
    )(blk_e, first, slot, nxt, n_used, xs, w_gate, w_up, w_down)


DISPATCH_TILE = 512


def _dispatch_kernel(info_ref, nu_ref, pos_ref, h_ref, xs_out, zbuf, sem, zsem, *, nb):
    tm = pos_ref.shape[1]
    tr = TOKEN_ROWS
    bm = MOE_BM

    @pl.when(pl.program_id(0) == 0)
    def _():
        zbuf[...] = jnp.zeros_like(zbuf)

        def zero_block(first_row):
            rows = pl.ds(pl.multiple_of(first_row * tr, tr), bm * tr)
            return pltpu.make_async_copy(zbuf, xs_out.at[rows], zsem)

        for act in ("start", "wait"):
            def last_of_expert(e, carry, act=act):
                @pl.when(info_ref[e, 3] > info_ref[e, 1])
                def _():
                    getattr(zero_block(info_ref[e, 3] - bm), act)()
                return carry

            def tail_block(b, carry, act=act):
                getattr(zero_block(b * bm), act)()
                return carry

            lax.fori_loop(0, N_EXPERTS, last_of_expert, 0)
            lax.fori_loop(nu_ref[0], nb, tail_block, 0)

    def send(tok, carry):
        src = h_ref.at[pl.ds(pl.multiple_of(tok * tr, tr), tr)]
        for k in range(TOP_K):
            row = pl.multiple_of(pos_ref[k, tok] * tr, tr)
            pltpu.make_async_copy(src, xs_out.at[pl.ds(row, tr)], sem).start()
        return carry

    lax.fori_loop(0, tm, send, 0, unroll=4)
    n = tm * TOP_K * tr
    pltpu.make_async_copy(xs_out.at[pl.ds(0, n)], xs_out.at[pl.ds(0, n)], sem).wait()


def _dispatch(h2, pos, info, n_used, nb):
    t = pos.shape[1]
    tm = DISPATCH_TILE
    return pl.pallas_call(
        functools.partial(_dispatch_kernel, nb=nb),
        grid=(t // tm,),
        in_specs=[pl.BlockSpec(memory_space=pltpu.SMEM),
                  pl.BlockSpec(memory_space=pltpu.SMEM),
                  pl.BlockSpec((SUBLANES, tm), lambda i: (0, i), memory_space=pltpu.SMEM),
                  pl.BlockSpec((tm * TOKEN_ROWS, LANES), lambda i: (i, 0))],
        out_specs=pl.BlockSpec(memory_space=pl.ANY),
        out_shape=jax.ShapeDtypeStruct((nb * MOE_BM * TOKEN_ROWS, LANES), h2.dtype),
        scratch_shapes=[pltpu.VMEM((MOE_BM * TOKEN_ROWS, LANES), h2.dtype),
                        pltpu.SemaphoreType.DMA(()), pltpu.SemaphoreType.DMA(())],
        compiler_params=pltpu.CompilerParams(dimension_semantics=("arbitrary",)),
        name="dispatch",
    )(info, n_used, pos, h2)


ROUTE_TILE = 512


def _router_kernel(lg_ref, bias_ref, pos_ref, gate_ref, be_ref, info_ref, rank_scr, ek_scr, *, t, nbp):
    tl = ROUTE_TILE
    ne = N_EXPERTS
    bm = MOE_BM
    row = lax.broadcasted_iota(jnp.int32, (ne, tl), 0)
    tri = (lax.broadcasted_iota(jnp.int32, (tl, tl), 0) < lax.broadcasted_iota(jnp.int32, (tl, tl), 1)).astype(BF16)
    bias = bias_ref[...]

    def select(i, counts):
        cols = pl.ds(pl.multiple_of(i * tl, tl), tl)
        s = jax.nn.sigmoid(lg_ref[:, cols])
        sel = s + bias
        mask = jnp.zeros((ne, tl), F32)
        vals = []
        for k in range(TOP_K):
            best = jnp.max(sel, axis=0, keepdims=True)
            idx = jnp.min(jnp.where(sel == best, row, ne), axis=0, keepdims=True)
            hit = row == idx
            vals.append(jnp.sum(jnp.where(hit, s, 0.0), axis=0, keepdims=True))
            sel = jnp.where(hit, -jnp.inf, sel)
            mask = jnp.where(hit, 1.0, mask)
            ek_scr[k:k + 1, cols] = idx
        total = vals[0]
        for v in vals[1:]:
            total = total + v
        scale = ROUTED_SCALE / total
        for k in range(TOP_K):
            gate_ref[k:k + 1, cols] = vals[k] * scale
        gate_ref[TOP_K:SUBLANES, cols] = jnp.zeros((SUBLANES - TOP_K, tl), F32)
        rank_scr[:, cols] = _dot(mask.astype(BF16), tri) + counts
        return counts + jnp.sum(mask, axis=1, keepdims=True)

    counts = lax.fori_loop(0, t // tl, select, jnp.zeros((ne, 1), F32))
    counts = counts.astype(jnp.int32)
    shift = bm.bit_length() - 1
    assert bm == 1 << shift
    padded = ((counts + (bm - 1)) >> shift) << shift
    e0 = lax.broadcasted_iota(jnp.int32, (ne, ne), 0)
    e1 = lax.broadcasted_iota(jnp.int32, (ne, ne), 1)
    padded_row = jnp.sum(jnp.where(e0 == e1, padded, 0), axis=0, keepdims=True)
    counts_row = jnp.sum(jnp.where(e0 == e1, counts, 0), axis=0, keepdims=True)
    pstart = jnp.sum(jnp.where(e1 < e0, padded_row, 0), axis=1, keepdims=True)
    ustart = jnp.sum(jnp.where(e1 < e0, counts_row, 0), axis=1, keepdims=True)
    pend = pstart + padded
    lane = lax.broadcasted_iota(jnp.int32, (ne, LANES), 1)
    info_ref[...] = jnp.where(lane == 0, counts, jnp.where(lane == 1, pstart, jnp.where(lane == 2, ustart, pend)))
    blk = lax.broadcasted_iota(jnp.int32, (ne, nbp), 1) * bm
    owner = jnp.sum(jnp.where(pend <= blk, 1, 0), axis=0, keepdims=True)
    be_ref[...] = jnp.minimum(owner, ne - 1)
    pstart_f = pstart.astype(F32)

    def place(i, carry):
        cols = pl.ds(pl.multiple_of(i * tl, tl), tl)
        dest = rank_scr[:, cols] + pstart_f
        for k in range(TOP_K):
            hit = row == ek_scr[k:k + 1, cols]
            pos_ref[k:k + 1, cols] = jnp.sum(jnp.where(hit, dest, 0.0), axis=0, keepdims=True).astype(jnp.int32)
        pos_ref[TOP_K:SUBLANES, cols] = jnp.zeros((SUBLANES - TOP_K, tl), jnp.int32)
        return carry

    lax.fori_loop(0, t // tl, place, 0)


def _router(logits_t, router_bias, nb):
    t = logits_t.shape[1]
    nbp = -(-nb // LANES) * LANES
    return pl.pallas_call(
        functools.partial(_router_kernel, t=t, nbp=nbp),
        out_shape=[jax.ShapeDtypeStruct((SUBLANES, t), jnp.int32), jax.ShapeDtypeStruct((SUBLANES, t), F32),
                   jax.ShapeDtypeStruct((1, nbp), jnp.int32), jax.ShapeDtypeStruct((N_EXPERTS, LANES), jnp.int32)],
        scratch_shapes=[pltpu.VMEM((N_EXPERTS, t), F32), pltpu.VMEM((SUBLANES, t), jnp.int32)],
        compiler_params=pltpu.CompilerParams(vmem_limit_bytes=VMEM_LIMIT),
        name="router",
    )(logits_t, router_bias.astype(F32).reshape(N_EXPERTS, 1))


def _dispatch_plan(blk_e_row, info, nb):
    bm = MOE_BM
    pend = info[:, 3]
    blk_e = blk_e_row[0, :nb]
    n_used = pend[-1] // bm
    blk = jnp.arange(nb, dtype=jnp.int32)
    prev_e = jnp.concatenate([jnp.full((1,), -1, jnp.int32), blk_e[:-1]])
    first = jnp.logical_and(blk < n_used, blk_e != prev_e)
    slot = (jnp.cumsum(first.astype(jnp.int32)) - 1) % 2
    first_at = jnp.where(first, blk, nb)
    nxt_first = lax.cummin(jnp.concatenate([first_at[1:], jnp.full((1,), nb, jnp.int32)]), reverse=True)
    nxt = jnp.where(nxt_first < nb, blk_e[jnp.minimum(nxt_first, nb - 1)], -1)
    return (blk_e, first.astype(jnp.int32), slot.astype(jnp.int32), nxt.astype(jnp.int32),
            n_used.astype(jnp.int32).reshape(1))


def _shared_kernel(pos_ref, nxt_ref, x_ref, h_ref, gt_ref, mod_ref, sg_ref, su_ref, sd_ref, fn_ref, eo_hbm,
                   o_ref, buf0, buf1, sem, *, final):
    i = pl.program_id(0)
    n = pl.num_programs(0)
    tm = x_ref.shape[0]
    tr = TOKEN_ROWS

    def fetch(idx_ref, buf, slot):
        def one(tok, carry):
            for k in range(TOP_K):
                row = pl.multiple_of(idx_ref[k, tok] * tr, tr)
                pltpu.make_async_copy(eo_hbm.at[pl.ds(row, tr)], buf.at[k, pl.ds(pl.multiple_of(tok * tr, tr), tr)],
                                      sem.at[slot]).start()
            return carry
        lax.fori_loop(0, tm, one, 0, unroll=4)

    def arrived(buf, slot):
        for k in range(TOP_K):
            pltpu.make_async_copy(eo_hbm.at[pl.ds(0, tm * tr)], buf.at[k], sem.at[slot]).wait()

    def compute(buf):
        m = mod_ref[0]
        half = D_MODEL // 2
        h_lo, h_hi = _unpack_halves(_load_token_tiles(h_ref, tm))
        h_lo = h_lo.astype(BF16)
        h_hi = h_hi.astype(BF16)
        gate = _dot(h_lo, sg_ref[0:half, :]) + _dot(h_hi, sg_ref[half:, :])
        up = _dot(h_lo, su_ref[0:half, :]) + _dot(h_hi, su_ref[half:, :])
        act = gate * jax.nn.sigmoid(gate) * up
        shared = _dot(act.astype(BF16), sd_ref[...])
        gt = gt_ref[...]
        r_lo = jnp.zeros((tm, half), F32)
        r_hi = r_lo
        for k in range(TOP_K):
            e_lo, e_hi = _unpack_halves(_load_token_tiles(buf.at[k], tm))
            r_lo = r_lo + gt[:, k:k + 1] * e_lo
            r_hi = r_hi + gt[:, k:k + 1] * e_hi
        routed = jnp.concatenate([r_lo, r_hi], axis=1)
        x = x_ref[...] + m[5:6] * (routed + shared)
        if final:
            x = _rms(x, fn_ref[...])
        o_ref[...] = x

    @pl.when(i == 0)
    def _():
        fetch(pos_ref, buf0, 0)

    for parity, cur, nxt in ((0, buf0, buf1), (1, buf1, buf0)):
        @pl.when(i % 2 == parity)
        def _(parity=parity, cur=cur, nxt=nxt):
            @pl.when(i + 1 < n)
            def _():
                fetch(nxt_ref, nxt, 1 - parity)

            arrived(cur, parity)
            compute(cur)


def _shared(x, h, gates_t, pos, eo, mod, sg, su, sd, final_norm, n_ctx_rows, dec_seq, *, final, row0=0, rows=None):
    rows = x.shape[0] if rows is None else rows
    tm = ROW_TILE // 2
    b0 = row0 // tm
    steps = rows // tm
    grp = lambda i: _group_of(i + b0, tm, n_ctx_rows, dec_seq)
    row = pl.BlockSpec((tm, D_MODEL), lambda i: (i + b0, 0))
    prow = pl.BlockSpec((tm * TOKEN_ROWS, LANES), lambda i: (i + b0, 0))
    full = lambda a, b: pl.BlockSpec((a, b), lambda i: (0, 0))
    buf = pltpu.VMEM((TOP_K, tm * TOKEN_ROWS, LANES), eo.dtype)
    return pl.pallas_call(
        functools.partial(_shared_kernel, final=final),
        grid=(steps,),
        in_specs=[pl.BlockSpec((SUBLANES, tm), lambda i: (0, i + b0), memory_space=pltpu.SMEM),
                  pl.BlockSpec((SUBLANES, tm), lambda i: (0, jnp.minimum(i + 1, steps - 1) + b0),
                               memory_space=pltpu.SMEM),
                  row, prow, pl.BlockSpec((tm, SUBLANES), lambda i: (i + b0, 0)),
                  pl.BlockSpec((1, 6, D_MODEL), lambda i: (grp(i), 0, 0)),
                  full(D_MODEL, D_SHARED), full(D_MODEL, D_SHARED), full(D_SHARED, D_MODEL), full(1, D_MODEL),
                  pl.BlockSpec(memory_space=pl.ANY)],
        out_specs=pl.BlockSpec((tm, D_MODEL), lambda i: (i, 0)),
        out_shape=jax.ShapeDtypeStruct((rows, D_MODEL), F32),
        scratch_shapes=[buf, buf, pltpu.SemaphoreType.DMA((2,))],
        compiler_params=_cparams(("arbitrary",)),
        name="shared_final" if final else "shared",
    )(pos, pos, x, h, gates_t, mod, sg.astype(BF16), su.astype(BF16), sd.astype(BF16),
      final_norm.reshape(1, D_MODEL), eo)


def kernel(x_prompt, x_sample, state_ret, state_s5_re, state_s5_im, c, c_ctx, w_ada, b_ada, norm_mix, norm_ffn, w_in, w_out, ret_decay, s5_lam_re, s5_lam_im, s5_log_dt, s5_b_re, s5_b_im, s5_c_re, s5_c_im, s5_d, s5_glu_w, s5_glu_b, hy_conv_w, hy_conv_b, hy_f1_w, hy_f1_b, hy_f2_w, hy_f2_b, hy_f3_w, hy_f3_b, hy_freq, hy_decay, hy_bias, moe_router, moe_router_bias, moe_w_gate, moe_w_up, moe_w_down, sh_w_gate, sh_w_up, sh_w_down, final_norm):
    n_ctx, seq, d = x_prompt.shape
    n_dec, dec_seq, _ = x_sample.shape
    n_ctx_rows = n_ctx * seq
    t = n_ctx_rows + n_dec * dec_seq

    x = jnp.concatenate([x_prompt.reshape(n_ctx_rows, d), x_sample.reshape(n_dec * dec_seq, d)], axis=0)
    cond = jnp.concatenate([c_ctx[None, :], c], axis=0)
    cond8 = jnp.pad(cond, ((0, SUBLANES - cond.shape[0]), (0, 0)))
    mods = _ada(cond8, w_ada, b_ada)[:, :1 + n_dec].reshape(DEPTH, 1 + n_dec, 6, d)

    cos2, sin2 = _rope_tables(dec_seq)
    no_rope = jnp.zeros((seq, LANES), F32)
    zero_ret = jnp.zeros((n_ctx, 2, RET_HEADS, RET_DK, RET_DV), F32)
    zero_s5 = jnp.zeros((n_ctx, 2, S5_GROUPS, S5_STATE), F32)
    dft_ctx = _dft_mats(seq)
    dft_dec = _dft_mats(dec_seq)

    ret_states = jnp.zeros((n_ctx, DEPTH, 2, RET_HEADS, RET_DK, RET_DV), F32)
    s5r_list, s5i_list = [], []
    for l in range(DEPTH):
        mod = mods[l]
        proj = _in_proj(x, mod, norm_mix[l], w_in[l].astype(BF16), n_ctx_rows, dec_seq)

        log_gamma = jax.nn.log_sigmoid(ret_decay[l].astype(F32))
        ret_o, ret_states = _retention(proj, log_gamma, zero_ret, no_rope, no_rope,
                                       row0=0, n_seq=n_ctx, seq_len=seq, hb=RET_HEADS, rope=False,
                                       dst=jnp.zeros((t, RET_WIDTH), BF16), states=ret_states, layer=l)
        ret_o, _ = _retention(proj, log_gamma, state_ret[:, l].astype(F32), cos2, sin2,
                              row0=n_ctx_rows, n_seq=n_dec, seq_len=dec_seq, hb=1, rope=True, dst=ret_o)

        mats = _s5_mats(s5_lam_re[l], s5_lam_im[l], s5_log_dt[l], s5_b_re[l], s5_b_im[l],
                        s5_c_re[l], s5_c_im[l], s5_d[l])
        s5_y, s5_re, s5_im = _s5(proj, mats, zero_s5, zero_s5, row0=0, n_seq=n_ctx, seq_len=seq,
                                 dst=jnp.zeros((t, S5_WIDTH), F32))
        s5_y, _, _ = _s5(proj, mats, state_s5_re[:, l], state_s5_im[:, l],
                         row0=n_ctx_rows, n_seq=n_dec, seq_len=dec_seq, dst=s5_y)
        s5r_list.append(s5_re)
        s5i_list.append(s5_im)

        filt = (hy_f1_w[l], hy_f1_b[l], hy_f2_w[l], hy_f2_b[l], hy_f3_w[l], hy_f3_b[l], hy_freq[l], hy_decay[l])
        hy_o = _hyena(proj, hy_conv_w[l], hy_conv_b[l], hy_bias[l], dft_ctx, _hy_filter_mults(seq, *filt),
                      row0=0, n_seq=n_ctx, seq_len=seq, cb=HY_WIDTH, tk=seq, dst=jnp.zeros((t, HY_WIDTH), BF16))
        hy_o = _hyena(proj, hy_conv_w[l], hy_conv_b[l], hy_bias[l], dft_dec, _hy_filter_mults(dec_seq, *filt),
                      row0=n_ctx_rows, n_seq=n_dec, seq_len=dec_seq, cb=HY_WIDTH // 2, tk=512, dst=hy_o)

        x, h2, logits = _out_proj(x, ret_o, s5_y, hy_o, mod, norm_ffn[l], s5_glu_w[l], s5_glu_b[l],
                                  w_out[l], moe_router[l], n_ctx_rows, dec_seq)

        nb = -(-(t * TOP_K) // MOE_BM) + N_EXPERTS
        pos, gates, blk_e_row, info = _router(logits, moe_router_bias[l], nb)
        blk_e, first, slot, nxt, n_used = _dispatch_plan(blk_e_row, info, nb)
        xs = _dispatch(h2, pos, info, n_used, nb)
        eo = _moe_grouped(xs, blk_e, first, slot, nxt, n_used, moe_w_gate, moe_w_up, moe_w_down, l)
        gates_t = gates.T

        sh = (sh_w_gate[l], sh_w_up[l], sh_w_down[l])
        if l < DEPTH - 1:
            x = _shared(x, h2, gates_t, pos, eo, mod, *sh, final_norm, n_ctx_rows, dec_seq, final=False)
        else:
            y_c = _shared(x, h2, gates_t, pos, eo, mod, *sh, final_norm, n_ctx_rows, dec_seq, final=True,
                          row0=0, rows=n_ctx_rows)
            y_d = _shared(x, h2, gates_t, pos, eo, mod, *sh, final_norm, n_ctx_rows, dec_seq, final=True,
                          row0=n_ctx_rows, rows=n_dec * dec_seq)

    return (y_c.reshape(n_ctx, seq, d), y_d.reshape(n_dec, dec_seq, d),
            ret_states, jnp.stack(s5r_list, axis=1), jnp.stack(s5i_list, axis=1))
```

```python
import functools
import math

import jax
import jax.numpy as jnp
from jax import lax
from jax.experimental import pallas as pl
from jax.experimental.pallas import tpu as pltpu

F32 = jnp.float32
BF16 = jnp.bfloat16

D_MODEL = 2048
DEPTH = 2
GRID_W = 64
RET_HEADS = 8
RET_DK = 128
RET_DV = 128
RET_WIDTH = RET_HEADS * RET_DV
RET_CHUNK = 128
ROPE_BASE = 10000.0
S5_WIDTH = 512
S5_GROUP = 16
S5_GROUPS = S5_WIDTH // S5_GROUP
S5_STATE = 64
S5_Q = 16
HY_WIDTH = 512
HY_BANDS = 16
IN_WIDTH = 4 * RET_WIDTH + S5_WIDTH + 3 * HY_WIDTH
U_COL = 4 * RET_WIDTH
HY_COL = U_COL + S5_WIDTH
N_EXPERTS = 64
TOP_K = 6
D_EXPERT = 512
D_SHARED = 512
ROUTED_SCALE = 2.5
EPS = 1e-6

LANES = 128
SUBLANES = 8
VMEM_LIMIT = 56 * 1024 * 1024

ROW_TILE = 512
MOE_BM = 256


def _cparams(sem):
    return pltpu.CompilerParams(dimension_semantics=sem, vmem_limit_bytes=VMEM_LIMIT)


def _dot(a, b):
    return jnp.dot(a, b, preferred_element_type=F32)


def _rms(x, g):
    var = jnp.mean(x * x, axis=-1, keepdims=True)
    return x * lax.rsqrt(var + EPS) * g


def _pack_halves(xb):
    n = xb.shape[1] // 2
    lo = lax.bitcast_convert_type(xb[:, :n].astype(F32), jnp.uint32) >> 16
    hi = lax.bitcast_convert_type(xb[:, n:].astype(F32), jnp.uint32)
    return lo | hi


def _unpack_halves(w):
    lo = lax.bitcast_convert_type(w << 16, F32)
    hi = lax.bitcast_convert_type(w & jnp.uint32(0xFFFF0000), F32)
    return lo, hi


TOKEN_ROWS = D_MODEL // 2 // LANES


def _store_token_tiles(ref, w):
    m = w.shape[0]
    for s in range(TOKEN_ROWS):
        ref[pl.ds(s, m, stride=TOKEN_ROWS), :] = w[:, s * LANES:(s + 1) * LANES]


def _load_token_tiles(ref, m):
    return jnp.concatenate([ref[pl.ds(s, m, stride=TOKEN_ROWS), :] for s in range(TOKEN_ROWS)], axis=1)


def _ada_kernel(c_ref, w_ref, b_ref, o_ref):
    c = c_ref[...]
    s = (c * jax.nn.sigmoid(c)).astype(BF16)
    o_ref[0] = _dot(s, w_ref[0].astype(BF16)) + b_ref[0]


def _ada(cond8, w_ada, b_ada):
    tn = 512
    n = w_ada.shape[-1]
    return pl.pallas_call(
        _ada_kernel,
        grid=(DEPTH, n // tn),
        in_specs=[
            pl.BlockSpec((SUBLANES, D_MODEL), lambda l, j: (0, 0)),
            pl.BlockSpec((1, D_MODEL, tn), lambda l, j: (l, 0, j)),
            pl.BlockSpec((1, 1, tn), lambda l, j: (l, 0, j)),
        ],
        out_specs=pl.BlockSpec((1, SUBLANES, tn), lambda l, j: (l, 0, j)),
        out_shape=jax.ShapeDtypeStruct((DEPTH, SUBLANES, n), F32),
        compiler_params=_cparams(("parallel", "parallel")),
        name="ada",
    )(cond8, w_ada, b_ada.reshape(DEPTH, 1, n))


def _group_of(i, tile, n_ctx_rows, dec_seq):
    ctx_tiles = n_ctx_rows // tile
    per = dec_seq // tile
    return jnp.where(i < ctx_tiles, 0, 1 + (i - ctx_tiles) // per)


def _in_kernel(x_ref, mod_ref, g_ref, w_ref, o_ref, h_scr):
    @pl.when(pl.program_id(1) == 0)
    def _():
        m = mod_ref[0]
        h = _rms(x_ref[...], g_ref[...]) * (1.0 + m[1:2]) + m[0:1]
        h_scr[...] = h.astype(BF16)

    o_ref[...] = _dot(h_scr[...], w_ref[...])


def _in_proj(x, mod, g, w_bf, n_ctx_rows, dec_seq):
    t = x.shape[0]
    tm, tn = 1024, 1024
    grp = functools.partial(_group_of, tile=tm, n_ctx_rows=n_ctx_rows, dec_seq=dec_seq)
    return pl.pallas_call(
        _in_kernel,
        grid=(t // tm, IN_WIDTH // tn),
        in_specs=[
            pl.BlockSpec((tm, D_MODEL), lambda i, j: (i, 0)),
            pl.BlockSpec((1, 6, D_MODEL), lambda i, j: (grp(i), 0, 0)),
            pl.BlockSpec((1, D_MODEL), lambda i, j: (0, 0)),
            pl.BlockSpec((D_MODEL, tn), lambda i, j: (0, j)),
        ],
        out_specs=pl.BlockSpec((tm, tn), lambda i, j: (i, j)),
        out_shape=jax.ShapeDtypeStruct((t, IN_WIDTH), F32),
        scratch_shapes=[pltpu.VMEM((tm, D_MODEL), BF16)],
        compiler_params=_cparams(("parallel", "arbitrary")),
        name="in_proj",
    )(x, mod, g.reshape(1, D_MODEL), w_bf)


def _ret_kernel(lg_ref, q_ref, k_ref, v_ref, gt_ref, cos_ref, sin_ref, s0_ref,
                o_ref, sfin_ref, acc_scr, q_scr, k_scr, *, seq_len, hb, rope):
    c = RET_CHUNK
    n_chunks = seq_len // c
    ii = lax.broadcasted_iota(jnp.int32, (c, c), 0)
    jj = lax.broadcasted_iota(jnp.int32, (c, c), 1)
    rel = (ii - jj).astype(F32)
    ci = lax.broadcasted_iota(jnp.int32, (c, 1), 0).astype(F32)
    one = jnp.ones((1, 1), F32)
    tdot = functools.partial(lax.dot_general, preferred_element_type=F32)

    def make_head(hh):
        head = pl.program_id(1) * hb + hh
        lgf = lg_ref[0, head]
        lgb = lg_ref[1, head]
        dmask = (jnp.where(rel >= 0, jnp.exp(lgf * jnp.maximum(rel, 0.0)), 0.0)
                 + jnp.where(rel <= 0, jnp.exp(lgb * jnp.maximum(-rel, 0.0)), 0.0))
        qd_f = jnp.exp(lgf * (ci + 1.0))
        kd_f = jnp.exp(lgf * (c - 1.0 - ci))
        cd_f = jnp.exp(lgf * c * one)
        qd_b = jnp.exp(lgb * (c - ci))
        kd_b = jnp.exp(lgb * ci)
        cd_b = jnp.exp(lgb * c * one)
        lanes = slice(hh * LANES, (hh + 1) * LANES)

        def rows_of(n):
            if isinstance(n, int):
                return slice(n * c, (n + 1) * c)
            return pl.ds(pl.multiple_of(n * c, c), c)

        def fwd_chunk(n, s_f):
            rows = rows_of(n)
            q = q_ref[rows, lanes]
            k = k_ref[rows, lanes] * (RET_DK ** -0.5)
            if rope:
                cs = cos_ref[rows, :]
                sn = sin_ref[rows, :]
                q = q * cs + pltpu.roll(q, RET_DK // 2, 1) * sn
                k = k * cs + pltpu.roll(k, RET_DK // 2, 1) * sn
            qb = q.astype(BF16)
            vb = v_ref[rows, lanes].astype(BF16)
            q_scr[rows, lanes] = qb
            k_scr[rows, lanes] = k
            scores = tdot(qb, k.astype(BF16), (((1,), (1,)), ((), ()))) * dmask
            inner = _dot(scores.astype(BF16), vb)
            cross = _dot(qb, s_f.astype(BF16)) * qd_f
            acc_scr[rows, lanes] = inner + cross
            upd = tdot((k * kd_f).astype(BF16), vb, (((0,), (0,)), ((), ())))
            return s_f * cd_f + upd

        def bwd_chunk(m, s_b):
            n = n_chunks - 1 - m
            rows = rows_of(n)
            qb = q_scr[rows, lanes]
            k = k_scr[rows, lanes]
            vb = v_ref[rows, lanes].astype(BF16)
            o = acc_scr[rows, lanes] + _dot(qb, s_b.astype(BF16)) * qd_b
            mu = jnp.mean(o, axis=-1, keepdims=True)
            oc = o - mu
            var = jnp.mean(oc * oc, axis=-1, keepdims=True)
            o = oc * lax.rsqrt(var + EPS)
            g = gt_ref[rows, lanes]
            o_ref[rows, lanes] = (g * jax.nn.sigmoid(g) * o).astype(o_ref.dtype)
            upd = tdot((k * kd_b).astype(BF16), vb, (((0,), (0,)), ((), ())))
            return s_b * cd_b + upd

        return fwd_chunk, bwd_chunk

    if n_chunks <= 4:
        for hh in range(hb):
            fwd_chunk, bwd_chunk = make_head(hh)
            s_f = s0_ref[0, 0, hh]
            s_b = s0_ref[0, 1, hh]
            for n in range(n_chunks):
                s_f = fwd_chunk(n, s_f)
            for m in range(n_chunks):
                s_b = bwd_chunk(m, s_b)
            sfin_ref[0, 0, hh] = s_f
            sfin_ref[0, 1, hh] = s_b
    else:
        fns = [make_head(hh) for hh in range(hb)]
        s_f = lax.fori_loop(0, n_chunks, lambda n, ss: tuple(f[0](n, s) for f, s in zip(fns, ss)),
                            tuple(s0_ref[0, 0, hh] for hh in range(hb)))
        s_b = lax.fori_loop(0, n_chunks, lambda m, ss: tuple(f[1](m, s) for f, s in zip(fns, ss)),
                            tuple(s0_ref[0, 1, hh] for hh in range(hb)))
        for hh in range(hb):
            sfin_ref[0, 0, hh] = s_f[hh]
            sfin_ref[0, 1, hh] = s_b[hh]


def _into(kernel_fn, n_in, dst):
    dsts = [d for d in (dst if isinstance(dst, (list, tuple)) else [dst])]
    outs = [k for k, d in enumerate(dsts) if d is not None]
    if not outs:
        return kernel_fn, [], [], {}

    def body(*refs):
        return kernel_fn(*refs[:n_in], *refs[n_in + len(outs):])

    return (body, [pl.BlockSpec(memory_space=pl.ANY)] * len(outs), [dsts[k] for k in outs],
            {n_in + pos: k for pos, k in enumerate(outs)})


def _retention(proj, log_gamma, s0, cos2, sin2, *, row0, n_seq, seq_len, hb, rope, dst=None,
               states=None, layer=0):
    blk0 = row0 // seq_len
    body, dst_spec, dst_arg, alias = _into(
        functools.partial(_ret_kernel, seq_len=seq_len, hb=hb, rope=rope), 8, [dst, states])
    if states is None:
        st_spec = pl.BlockSpec((1, 2, hb, RET_DK, RET_DV), lambda b, h, lg: (b, 0, h, 0, 0))
        st_shape = jax.ShapeDtypeStruct((n_seq, 2, RET_HEADS, RET_DK, RET_DV), F32)
    else:
        st_spec = pl.BlockSpec((1, None, 2, hb, RET_DK, RET_DV), lambda b, h, lg: (b, layer, 0, h, 0, 0))
        st_shape = jax.ShapeDtypeStruct(states.shape, F32)
    w = hb * LANES
    hblocks = RET_HEADS // hb
    col = lambda part: (lambda b, h, lg: (blk0 + b, part * hblocks + h))
    grid_spec = pltpu.PrefetchScalarGridSpec(
        num_scalar_prefetch=1,
        grid=(n_seq, hblocks),
        in_specs=[
            pl.BlockSpec((seq_len, w), col(0)),
            pl.BlockSpec((seq_len, w), col(1)),
            pl.BlockSpec((seq_len, w), col(2)),
            pl.BlockSpec((seq_len, w), col(3)),
            pl.BlockSpec((seq_len, LANES), lambda b, h, lg: (0, 0)),
            pl.BlockSpec((seq_len, LANES), lambda b, h, lg: (0, 0)),
            pl.BlockSpec((1, 2, hb, RET_DK, RET_DV), lambda b, h, lg: (b, 0, h, 0, 0)),
        ] + dst_spec,
        out_specs=[
            pl.BlockSpec((seq_len, w), lambda b, h, lg: (blk0 + b, h)),
            st_spec,
        ],
        scratch_shapes=[
            pltpu.VMEM((seq_len, w), F32),
            pltpu.VMEM((seq_len, w), BF16),
            pltpu.VMEM((seq_len, w), F32),
        ],
    )
    return pl.pallas_call(
        body,
        grid_spec=grid_spec,
        out_shape=[jax.ShapeDtypeStruct((proj.shape[0], RET_WIDTH), BF16), st_shape],
        input_output_aliases=alias,
        compiler_params=_cparams(("parallel", "arbitrary")),
        name="retention",
    )(log_gamma, proj, proj, proj, proj, cos2, sin2, s0, *dst_arg)


def _rope_tables(seq_len):
    rows_n = seq_len // GRID_W
    rows = jnp.repeat(jnp.arange(rows_n, dtype=F32), GRID_W)
    cols = jnp.tile(jnp.arange(GRID_W, dtype=F32), rows_n)
    nf = RET_DK // 4
    inv = ROPE_BASE ** (-jnp.arange(nf, dtype=F32) / nf)
    ang = jnp.concatenate([rows[:, None] * inv, cols[:, None] * inv], axis=-1)
    cs, sn = jnp.cos(ang), jnp.sin(ang)
    return jnp.concatenate([cs, cs], axis=-1), jnp.concatenate([-sn, sn], axis=-1)


S5_GB = LANES // S5_GROUP
S5_W = S5_Q * LANES
S5_SPLIT = 4
S5_SW = S5_GB * S5_STATE
S5_SB = S5_SW // LANES


def _s5_kernel(u_ref, bq_ref, k_ref, cq_ref, ar_ref, ai_ref, d_ref, h0_ref,
               y_ref, hf_ref, ub_scr, sm_scr, hp_scr, hpb_scr, t_scr, *, n_seq, n_chunks):
    s = pl.program_id(1)
    m = n_seq * n_chunks
    q = S5_Q
    nblk = S5_W // LANES
    sb = S5_SB

    per = q // S5_SPLIT
    kw = S5_W // S5_SPLIT

    @pl.when(s == 0)
    def _():
        for j in range(q):
            ub_scr[j // per, :, (j % per) * LANES:(j % per + 1) * LANES] = (
                u_ref[pl.ds(j, m, stride=q), :].astype(BF16))
        for ib in range(S5_SPLIT):
            for j in range(q):
                c0 = (q - 1 - j) * LANES + ib * kw
                t_scr[ib, j * LANES:(j + 1) * LANES, :] = k_ref[0, :, c0:c0 + kw]

    @pl.when(s < S5_SPLIT)
    def _():
        part = _dot(ub_scr[jnp.minimum(s, S5_SPLIT - 1)], bq_ref[0])

        @pl.when(s == 0)
        def _():
            for cb in range(nblk):
                sm_scr[cb] = part[:, cb * LANES:(cb + 1) * LANES]

        @pl.when(s > 0)
        def _():
            for cb in range(nblk):
                sm_scr[cb] += part[:, cb * LANES:(cb + 1) * LANES]

    @pl.when(s == S5_SPLIT - 1)
    def _():
        ar = ar_ref[0]
        ai = ai_ref[0]
        h0 = h0_ref[0]
        blk = lambda a, cb: a[:, cb * LANES:(cb + 1) * LANES]

        def body(n, carry):
            rows_f = pl.ds(n, n_seq, stride=n_chunks)
            rows_b = pl.ds(n_chunks - 1 - n, n_seq, stride=n_chunks)
            new = list(carry)
            for d, rows in ((0, rows_f), (1, rows_b)):
                for c in range(sb):
                    re_i = d * sb + c
                    im_i = (2 + d) * sb + c
                    hr, hi = carry[re_i], carry[im_i]
                    hp_scr[re_i, rows, :] = hr
                    hp_scr[im_i, rows, :] = hi
                    a_r, a_i = blk(ar, re_i), blk(ai, re_i)
                    new[re_i] = a_r * hr - a_i * hi + sm_scr[re_i, rows, :]
                    new[im_i] = a_r * hi + a_i * hr + sm_scr[im_i, rows, :]
            return tuple(new)

        fin = lax.fori_loop(0, n_chunks, body, tuple(blk(h0, cb) for cb in range(nblk)))
        hf_ref[0] = jnp.concatenate(fin, axis=1)
        for cb in range(nblk):
            hpb_scr[:, cb * LANES:(cb + 1) * LANES] = hp_scr[cb].astype(BF16)

    @pl.when(s >= S5_SPLIT)
    def _():
        ub = jnp.concatenate([ub_scr[k] for k in range(S5_SPLIT)], axis=1)
        y = _dot(ub, t_scr[jnp.maximum(s - S5_SPLIT, 0)]) + _dot(hpb_scr[...], cq_ref[0])
        dd = d_ref[0]
        for ii in range(per):
            rows = pl.ds((s - S5_SPLIT) * per + ii, m, stride=q)
            yi = y[:, ii * LANES:(ii + 1) * LANES] + dd * u_ref[rows, :]
            y_ref[rows, :] = jax.nn.gelu(yi)


def _s5_expand_kernel(mc_ref, o_ref, *, xsize, ysize):
    xs, ys, gs = xsize.bit_length() - 1, ysize.bit_length() - 1, S5_GB.bit_length() - 1
    assert xsize == 1 << xs and ysize == 1 << ys and S5_GB == 1 << gs
    cw = o_ref.shape[2]
    nc = mc_ref.shape[2]
    col0 = pl.program_id(1) * cw
    r = lax.broadcasted_iota(jnp.int32, (nc, cw), 0)
    col = lax.broadcasted_iota(jnp.int32, (nc, cw), 1) + col0
    spread = jnp.logical_and(r >> ys == col >> (ys + gs), (r & (ysize - 1)) == (col & (ysize - 1)))
    big = _dot(mc_ref[0], jnp.where(spread, 1.0, 0.0).astype(BF16))
    row = lax.broadcasted_iota(jnp.int32, big.shape, 0)
    colb = lax.broadcasted_iota(jnp.int32, big.shape, 1) + col0
    same = ((row >> xs) & (S5_GB - 1)) == ((colb >> ys) & (S5_GB - 1))
    o_ref[0] = jnp.where(same, big, 0.0).astype(BF16)


def _s5_expand(mc, *, xsize, ysize):
    nb, rows, nc = mc.shape
    cols = nc * S5_GB
    cw = 512
    return pl.pallas_call(
        functools.partial(_s5_expand_kernel, xsize=xsize, ysize=ysize),
        grid=(nb, cols // cw),
        in_specs=[pl.BlockSpec((1, rows, nc), lambda b, j: (b, 0, 0))],
        out_specs=pl.BlockSpec((1, rows, cw), lambda b, j: (b, 0, j)),
        out_shape=jax.ShapeDtypeStruct((nb, rows, cols), BF16),
        compiler_params=_cparams(("parallel", "parallel")),
        name="s5_expand",
    )(mc)


def _s5_mats(lam_re, lam_im, log_dt, b_re, b_im, c_re, c_im, d):
    q, g, p, ch = S5_Q, S5_GROUPS, S5_STATE, S5_GROUP
    lam = lax.complex(jnp.minimum(lam_re.astype(F32), -1e-4), lam_im.astype(F32))
    ldt = lam * jnp.exp(log_dt.astype(F32))[..., None]
    lam_bar = jnp.exp(ldt)
    b_bar = ((lam_bar - 1.0) / lam)[..., None] * lax.complex(b_re.astype(F32), b_im.astype(F32))
    cc = lax.complex(c_re.astype(F32), c_im.astype(F32))
    pw = jnp.exp(ldt[..., None] * jnp.arange(q + 1, dtype=F32))
    hi = lax.Precision.HIGHEST
    lag = jnp.arange(2 * q, dtype=F32) - (q - 1)
    wf = jnp.where(lag >= 0, jnp.exp(ldt[0][..., None] * jnp.maximum(lag, 0.0)), 0.0)
    wb = jnp.where(lag <= 0, jnp.exp(ldt[1][..., None] * jnp.maximum(-lag, 0.0)), 0.0)
    kc = jnp.real(jnp.einsum('gcp,gpd,gpe->gedc', cc[0], wf, b_bar[0], precision=hi)
                  + jnp.einsum('gcp,gpd,gpe->gedc', cc[1], wb, b_bar[1], precision=hi))

    pw_dn = jnp.exp(ldt[..., None] * (q - jnp.arange(q + 1, dtype=F32)))
    bf = pw_dn[0][..., 1:][:, :, :, None] * b_bar[0][:, :, None, :]
    bb = pw[1][..., :q][:, :, :, None] * b_bar[1][:, :, None, :]
    to_rows = lambda m: m.transpose(0, 2, 3, 1).reshape(g, q * ch, p)
    bq = jnp.concatenate([to_rows(jnp.real(bf)), to_rows(jnp.real(bb)),
                          to_rows(jnp.imag(bf)), to_rows(jnp.imag(bb))], axis=-1)

    cf = cc[0].transpose(0, 2, 1)[:, :, None, :] * pw[0][..., 1:][:, :, :, None]
    cb = cc[1].transpose(0, 2, 1)[:, :, None, :] * pw_dn[1][..., :q][:, :, :, None]
    to_cols = lambda m: m.reshape(g, p, q * ch)
    cq = jnp.concatenate([to_cols(jnp.real(cf)), to_cols(jnp.real(cb)),
                          to_cols(-jnp.imag(cf)), to_cols(-jnp.imag(cb))], axis=1)

    gb, nb = S5_GB, g // S5_GB
    rows_of = lambda a, outer, inner: (a.reshape(nb, gb, outer, inner, a.shape[-1]).transpose(0, 2, 1, 3, 4)
                                       .reshape(nb, outer * gb * inner, a.shape[-1]).astype(BF16))
    tm_bd = _s5_expand(kc.reshape(nb, gb * ch, 2 * q * ch).astype(BF16), xsize=ch, ysize=ch)
    bq_bd = _s5_expand(rows_of(bq, q, ch), xsize=ch, ysize=p)
    cq_bd = _s5_expand(rows_of(cq, 4, p), xsize=p, ysize=ch)

    lq = pw[..., q].reshape(2, nb, 1, S5_SW)
    ar = jnp.concatenate([jnp.real(lq[0]), jnp.real(lq[1])], axis=-1)
    ai = jnp.concatenate([jnp.imag(lq[0]), jnp.imag(lq[1])], axis=-1)
    dd = d.astype(F32).reshape(nb, 1, LANES)
    return tm_bd, bq_bd, cq_bd, ar, ai, dd


def _s5(proj, mats, h0_re, h0_im, *, row0, n_seq, seq_len, dst=None):
    q, p = S5_Q, S5_STATE
    n_chunks = seq_len // q
    m = n_seq * n_chunks
    rows = n_seq * seq_len
    nb = S5_GROUPS // S5_GB
    nblk = S5_W // LANES
    kw = S5_W // S5_SPLIT
    tm, bq, cq, ar, ai, dd = mats
    part = lambda a: a.astype(F32).reshape(n_seq, nb, S5_SW)
    h0 = jnp.concatenate([part(h0_re[:, 0]), part(h0_re[:, 1]), part(h0_im[:, 0]), part(h0_im[:, 1])],
                         axis=-1).transpose(1, 0, 2)
    per_b = lambda shape: pl.BlockSpec((1,) + shape, lambda b, s: (b, 0, 0))
    body, dst_spec, dst_arg, alias = _into(functools.partial(_s5_kernel, n_seq=n_seq, n_chunks=n_chunks), 8, dst)
    y, hf = pl.pallas_call(
        body,
        grid=(nb, 2 * S5_SPLIT),
        in_specs=[
            pl.BlockSpec((rows, LANES), lambda b, s: (row0 // rows, U_COL // LANES + b)),
            pl.BlockSpec((1, kw, 4 * S5_SW), lambda b, s: (b, jnp.minimum(s, S5_SPLIT - 1), 0)),
            per_b((LANES, 2 * q * LANES)),
            pl.BlockSpec((1, 4 * S5_SW, kw), lambda b, s: (b, 0, jnp.maximum(s - S5_SPLIT, 0))),
            per_b((1, 2 * S5_SW)), per_b((1, 2 * S5_SW)), per_b((1, LANES)), per_b((n_seq, 4 * S5_SW)),
        ] + dst_spec,
        out_specs=[pl.BlockSpec((rows, LANES), lambda b, s: (row0 // rows, b)), per_b((n_seq, 4 * S5_SW))],
        out_shape=[jax.ShapeDtypeStruct((proj.shape[0], S5_WIDTH), F32),
                   jax.ShapeDtypeStruct((nb, n_seq, 4 * S5_SW), F32)],
        scratch_shapes=[pltpu.VMEM((S5_SPLIT, m, kw), BF16), pltpu.VMEM((nblk, m, LANES), F32),
                        pltpu.VMEM((nblk, m, LANES), F32), pltpu.VMEM((m, 4 * S5_SW), BF16),
                        pltpu.VMEM((S5_SPLIT, S5_W, kw), BF16)],
        input_output_aliases=alias,
        compiler_params=_cparams(("parallel", "arbitrary")),
        name="s5",
    )(proj, bq, tm, cq, ar, ai, dd, h0, *dst_arg)
    hf = hf.reshape(nb, n_seq, 4, S5_GB, p).transpose(1, 2, 0, 3, 4).reshape(n_seq, 4, S5_GROUPS, p)
    return y, hf[:, 0:2], hf[:, 2:4]


def _conv3(x, w, b):
    n = x.shape[0]
    row = lax.broadcasted_iota(jnp.int32, x.shape, 0)
    prev = jnp.where(row == 0, 0.0, pltpu.roll(x, 1, 0))
    nxt = jnp.where(row == n - 1, 0.0, pltpu.roll(x, n - 1, 0))
    return prev * w[0:1] + x * w[1:2] + nxt * w[2:3] + b


def _hy_fwd_kernel(x0_ref, x1_ref, v_ref, w0_ref, w1_ref, wv_ref, b0_ref, b1_ref, bv_ref,
                   fc_ref, fs_ref, m1_ref, m2_ref, m3_ref, p_ref, z_ref, x0c_ref, zb_scr):
    @pl.when(pl.program_id(2) == 0)
    def _():
        z = _conv3(x1_ref[...], w1_ref[...], b1_ref[...]) * _conv3(v_ref[...], wv_ref[...], bv_ref[...])
        z_ref[...] = z
        zb_scr[...] = z.astype(BF16)
        x0c_ref[...] = _conv3(x0_ref[...], w0_ref[...], b0_ref[...])

    zb = zb_scr[...]
    a = _dot(fc_ref[...], zb)
    b = _dot(fs_ref[...], zb)
    m2 = m2_ref[...]
    p_ref[0, 0] = (m1_ref[...] * a + m2 * b).astype(BF16)
    p_ref[0, 1] = (m3_ref[...] * b - m2 * a).astype(BF16)


def _hy_inv_kernel(p_ref, gc_ref, gs_ref, z_ref, x0c_ref, bias_ref, o_ref):
    conv = _dot(gc_ref[...], p_ref[0, 0]) + _dot(gs_ref[...], p_ref[0, 1])
    o_ref[...] = (x0c_ref[...] * (conv + bias_ref[...] * z_ref[...])).astype(o_ref.dtype)


def _dft_mats(seq_len):
    n, w = seq_len, 64
    k = jnp.arange(n, dtype=jnp.int32)
    ang = lambda j: ((k[:, None] * j[None, :]) % (2 * n)).astype(F32) * (math.pi / n)
    ang_a = ang(jnp.arange(n // w, dtype=jnp.int32) * w)
    ang_b = ang(jnp.arange(w, dtype=jnp.int32))
    ca, sa = jnp.cos(ang_a)[:, :, None], jnp.sin(ang_a)[:, :, None]
    cb, sb = jnp.cos(ang_b)[:, None, :], jnp.sin(ang_b)[:, None, :]
    cm = (ca * cb - sa * sb).reshape(n, n)
    sm = -(sa * cb + ca * sb).reshape(n, n)
    nyq = jnp.where(k % 2 == 0, 1.0, -1.0).astype(F32)
    return cm.astype(BF16), sm.at[0, :].set(nyq).astype(BF16), sm.at[:, 0].set(nyq).astype(BF16)


def _hy_filter_mults(seq_len, f1_w, f1_b, f2_w, f2_b, f3_w, f3_b, freq, decay):
    n = seq_len
    t = (jnp.arange(n, dtype=F32) / n)[:, None]
    bands = jnp.arange(1, HY_BANDS + 1, dtype=F32)[None, :]
    z = jnp.concatenate([t, jnp.cos(2.0 * math.pi * t * bands), jnp.sin(2.0 * math.pi * t * bands)], axis=-1)
    hi = lax.Precision.HIGHEST
    fr = freq.astype(F32)
    h = jnp.sin(fr * (jnp.dot(z, f1_w.astype(F32), precision=hi) + f1_b.astype(F32)))
    h = jnp.sin(fr * (jnp.dot(h, f2_w.astype(F32), precision=hi) + f2_b.astype(F32)))
    h = jnp.dot(h, f3_w.astype(F32), precision=hi) + f3_b.astype(F32)
    h = h * jnp.exp(-t * jnp.abs(decay.astype(F32)))
    h = h.reshape(n, 2, HY_WIDTH)
    h = h / jnp.sum(jnp.abs(h), axis=(0, 1), keepdims=True)
    hc = jnp.concatenate([h[:, 0], h[::-1, 1]], axis=0)
    spec = jnp.fft.rfft(hc, axis=0)
    hr, him = jnp.real(spec), jnp.imag(spec)
    wk = jnp.where(jnp.arange(n) == 0, 1.0, 2.0)[:, None] / (2.0 * n)
    first = (jnp.arange(n) == 0)[:, None]
    m1 = hr[:n] * wk
    m2 = jnp.where(first, 0.0, -him[:n]) * wk
    m3 = jnp.where(first, hr[n:n + 1], hr[:n]) * wk
    return m1.astype(F32), m2.astype(F32), m3.astype(F32)


def _hyena(proj, conv_w, conv_b, bias, dft, mults, *, row0, n_seq, seq_len, cb, tk, dst=None):
    blk0 = row0 // seq_len
    nc = HY_WIDTH // cb
    nk = seq_len // tk
    c0 = HY_COL // cb
    cm, sm, smt = dft
    m1, m2, m3 = mults
    xcol = lambda part: pl.BlockSpec((seq_len, cb), lambda b, c, k: (blk0 + b, c0 + part * nc + c))
    wcol = lambda part: pl.BlockSpec((3, cb), lambda b, c, k: (0, part * nc + c))
    bcol = lambda part: pl.BlockSpec((1, cb), lambda b, c, k: (0, part * nc + c))
    frow = pl.BlockSpec((tk, seq_len), lambda b, c, k: (k, 0))
    mblk = pl.BlockSpec((tk, cb), lambda b, c, k: (k, c))
    cb2 = conv_b.reshape(1, 3 * HY_WIDTH)
    pspec, z, x0c = pl.pallas_call(
        _hy_fwd_kernel,
        grid=(n_seq, nc, nk),
        in_specs=[xcol(0), xcol(1), xcol(2), wcol(0), wcol(1), wcol(2), bcol(0), bcol(1), bcol(2),
                  frow, frow, mblk, mblk, mblk],
        out_specs=[
            pl.BlockSpec((1, 2, tk, cb), lambda b, c, k: (b, 0, k, c)),
            pl.BlockSpec((seq_len, cb), lambda b, c, k: (b, c)),
            pl.BlockSpec((seq_len, cb), lambda b, c, k: (b, c)),
        ],
        out_shape=[
            jax.ShapeDtypeStruct((n_seq, 2, seq_len, HY_WIDTH), BF16),
            jax.ShapeDtypeStruct((n_seq * seq_len, HY_WIDTH), F32),
            jax.ShapeDtypeStruct((n_seq * seq_len, HY_WIDTH), F32),
        ],
        scratch_shapes=[pltpu.VMEM((seq_len, cb), BF16)],
        compiler_params=_cparams(("parallel", "parallel", "arbitrary")),
        name="hyena_fwd",
    )(proj, proj, proj, conv_w, conv_w, conv_w, cb2, cb2, cb2, cm, sm, m1, m2, m3)
    grow = pl.BlockSpec((tk, seq_len), lambda b, c, k: (k, 0))
    tile = pl.BlockSpec((tk, cb), lambda b, c, k: (b * nk + k, c))
    body, dst_spec, dst_arg, alias = _into(_hy_inv_kernel, 6, dst)
    return pl.pallas_call(
        body,
        grid=(n_seq, nc, nk),
        in_specs=[pl.BlockSpec((1, 2, seq_len, cb), lambda b, c, k: (b, 0, 0, c)),
                  grow, grow, tile, tile, pl.BlockSpec((1, cb), lambda b, c, k: (0, c))] + dst_spec,
        out_specs=pl.BlockSpec((tk, cb), lambda b, c, k: (row0 // tk + b * nk + k, c)),
        out_shape=jax.ShapeDtypeStruct((proj.shape[0], HY_WIDTH), BF16),
        input_output_aliases=alias,
        compiler_params=_cparams(("parallel", "parallel", "arbitrary")),
        name="hyena_inv",
    )(pspec, cm, smt, z, x0c, bias.reshape(1, HY_WIDTH), *dst_arg)


def _out_kernel(x_ref, ret_ref, s5_ref, hy_ref, mod_ref, g_ref, gw_ref, gb_ref,
                wr_ref, ws_ref, wh_ref, rt_ref, xo_ref, h_ref, lg_ref):
    m = mod_ref[0]
    y = s5_ref[...]
    s5o = y * jax.nn.sigmoid(_dot(y.astype(BF16), gw_ref[...]) + gb_ref[...])
    mix = (_dot(ret_ref[...], wr_ref[...]) + _dot(s5o.astype(BF16), ws_ref[...])
           + _dot(hy_ref[...], wh_ref[...]))
    x = x_ref[...] + m[2:3] * mix
    xo_ref[...] = x
    h = _rms(x, g_ref[...]) * (1.0 + m[4:5]) + m[3:4]
    hb = h.astype(BF16)
    _store_token_tiles(h_ref, _pack_halves(hb))
    lg_ref[...] = lax.dot_general(rt_ref[...], hb, (((1,), (1,)), ((), ())), preferred_element_type=F32)


def _out_proj(x, ret_o, s5_y, hy_o, mod, g, glu_w, glu_b, w_out, router, n_ctx_rows, dec_seq):
    t = x.shape[0]
    tm = ROW_TILE
    grp = functools.partial(_group_of, tile=tm, n_ctx_rows=n_ctx_rows, dec_seq=dec_seq)
    row = lambda w: pl.BlockSpec((tm, w), lambda i: (i, 0))
    full = lambda a, b: pl.BlockSpec((a, b), lambda i: (0, 0))
    wo = w_out.astype(BF16)
    return pl.pallas_call(
        _out_kernel,
        grid=(t // tm,),
        in_specs=[row(D_MODEL), row(RET_WIDTH), row(S5_WIDTH), row(HY_WIDTH),
                  pl.BlockSpec((1, 6, D_MODEL), lambda i: (grp(i), 0, 0)),
                  full(1, D_MODEL), full(S5_WIDTH, S5_WIDTH), full(1, S5_WIDTH),
                  full(RET_WIDTH, D_MODEL), full(S5_WIDTH, D_MODEL), full(HY_WIDTH, D_MODEL),
                  full(N_EXPERTS, D_MODEL)],
        out_specs=[row(D_MODEL), pl.BlockSpec((tm * TOKEN_ROWS, LANES), lambda i: (i, 0)),
                   pl.BlockSpec((N_EXPERTS, tm), lambda i: (0, i))],
        out_shape=[jax.ShapeDtypeStruct((t, D_MODEL), F32), jax.ShapeDtypeStruct((t * TOKEN_ROWS, LANES), jnp.uint32),
                   jax.ShapeDtypeStruct((N_EXPERTS, t), F32)],
        compiler_params=_cparams(("parallel",)),
        name="out_proj",
    )(x, ret_o, s5_y, hy_o, mod, g.reshape(1, D_MODEL), glu_w.astype(BF16), glu_b.reshape(1, S5_WIDTH),
      wo[:RET_WIDTH], wo[RET_WIDTH:RET_WIDTH + S5_WIDTH], wo[RET_WIDTH + S5_WIDTH:], router.T.astype(BF16))


def _moe_kernel(be_ref, first_ref, slot_ref, nxt_ref, nu_ref, xs_ref, wg_hbm, wu_hbm, wd_hbm, o_ref,
                wg_f, wu_f, wd_f, wg_b, wu_b, wd_b, sem, *, layer):
    i = pl.program_id(0)

    def copies(e, s):
        return (pltpu.make_async_copy(wg_hbm.at[layer, e], wg_f.at[s], sem.at[s, 0]),
                pltpu.make_async_copy(wu_hbm.at[layer, e], wu_f.at[s], sem.at[s, 1]),
                pltpu.make_async_copy(wd_hbm.at[layer, e], wd_f.at[s], sem.at[s, 2]))

    @pl.when(i == 0)
    def _():
        for cp in copies(be_ref[0], 0):
            cp.start()

    @pl.when(first_ref[i] == 1)
    def _():
        s = slot_ref[i]
        for cp in copies(be_ref[i], s):
            cp.wait()

        @pl.when(nxt_ref[i] >= 0)
        def _():
            for cp in copies(nxt_ref[i], 1 - s):
                cp.start()

        wg_b[...] = wg_f[s].astype(BF16)
        wu_b[...] = wu_f[s].astype(BF16)
        wd_b[...] = wd_f[s].astype(BF16)

    @pl.when(i < nu_ref[0])
    def _():
        half = D_MODEL // 2
        x_lo, x_hi = _unpack_halves(_load_token_tiles(xs_ref, MOE_BM))
        x_lo = x_lo.astype(BF16)
        x_hi = x_hi.astype(BF16)
        gate = _dot(x_lo, wg_b[0:half, :]) + _dot(x_hi, wg_b[half:, :])
        up = _dot(x_lo, wu_b[0:half, :]) + _dot(x_hi, wu_b[half:, :])
        hb = gate * jax.nn.sigmoid(gate) * up
        _store_token_tiles(o_ref, _pack_halves(_dot(hb.astype(BF16), wd_b[...]).astype(BF16)))

    @pl.when(i >= nu_ref[0])
    def _():
        o_ref[...] = jnp.zeros_like(o_ref)


def _moe_grouped(xs, blk_e, first, slot, nxt, n_used, w_gate, w_up, w_down, layer):
    pr = xs.shape[0] // TOKEN_ROWS
    bm = MOE_BM
    nb = pr // bm
    grid_spec = pltpu.PrefetchScalarGridSpec(
        num_scalar_prefetch=5,
        grid=(nb,),
        in_specs=[
            pl.BlockSpec((bm * TOKEN_ROWS, LANES), lambda i, *_: (i, 0)),
            pl.BlockSpec(memory_space=pl.ANY),
            pl.BlockSpec(memory_space=pl.ANY),
            pl.BlockSpec(memory_space=pl.ANY),
        ],
        out_specs=pl.BlockSpec((bm * TOKEN_ROWS, LANES), lambda i, *_: (i, 0)),
        scratch_shapes=[pltpu.VMEM((2, D_MODEL, D_EXPERT), F32), pltpu.VMEM((2, D_MODEL, D_EXPERT), F32),
                        pltpu.VMEM((2, D_EXPERT, D_MODEL), F32),
                        pltpu.VMEM((D_MODEL, D_EXPERT), BF16), pltpu.VMEM((D_MODEL, D_EXPERT), BF16),
                        pltpu.VMEM((D_EXPERT, D_MODEL), BF16),
                        pltpu.SemaphoreType.DMA((2, 3))],
    )
    return pl.pallas_call(
        functools.partial(_moe_kernel, layer=layer),
        grid_spec=grid_spec,
        out_shape=jax.ShapeDtypeStruct((pr * TOKEN_ROWS, LANES), jnp.uint32),
        compiler_params=_cparams(("arbitrary",)),
        name="moe_grouped",
    )(blk_e, first, slot, nxt, n_used, xs, w_gate, w_up, w_down)


DISPATCH_TILE = 512


def _dispatch_kernel(info_ref, nu_ref, pos_ref, h_ref, xs_out, zbuf, sem, zsem, *, nb):
    tm = pos_ref.shape[1]
    tr = TOKEN_ROWS
    bm = MOE_BM

    @pl.when(pl.program_id(0) == 0)
    def _():
        zbuf[...] = jnp.zeros_like(zbuf)

        def zero_block(first_row):
            rows = pl.ds(pl.multiple_of(first_row * tr, tr), bm * tr)
            return pltpu.make_async_copy(zbuf, xs_out.at[rows], zsem)

        for act in ("start", "wait"):
            def last_of_expert(e, carry, act=act):
                @pl.when(info_ref[e, 3] > info_ref[e, 1])
                def _():
                    getattr(zero_block(info_ref[e, 3] - bm), act)()
                return carry

            def tail_block(b, carry, act=act):
                getattr(zero_block(b * bm), act)()
                return carry

            lax.fori_loop(0, N_EXPERTS, last_of_expert, 0)
            lax.fori_loop(nu_ref[0], nb, tail_block, 0)

    def send(tok, carry):
        src = h_ref.at[pl.ds(pl.multiple_of(tok * tr, tr), tr)]
        for k in range(TOP_K):
            row = pl.multiple_of(pos_ref[k, tok] * tr, tr)
            pltpu.make_async_copy(src, xs_out.at[pl.ds(row, tr)], sem).start()
        return carry

    lax.fori_loop(0, tm, send, 0, unroll=4)
    n = tm * TOP_K * tr
    pltpu.make_async_copy(xs_out.at[pl.ds(0, n)], xs_out.at[pl.ds(0, n)], sem).wait()


def _dispatch(h2, pos, info, n_used, nb):
    t = pos.shape[1]
    tm = DISPATCH_TILE
    return pl.pallas_call(
        functools.partial(_dispatch_kernel, nb=nb),
        grid=(t // tm,),
        in_specs=[pl.BlockSpec(memory_space=pltpu.SMEM),
                  pl.BlockSpec(memory_space=pltpu.SMEM),
                  pl.BlockSpec((SUBLANES, tm), lambda i: (0, i), memory_space=pltpu.SMEM),
                  pl.BlockSpec((tm * TOKEN_ROWS, LANES), lambda i: (i, 0))],
        out_specs=pl.BlockSpec(memory_space=pl.ANY),
        out_shape=jax.ShapeDtypeStruct((nb * MOE_BM * TOKEN_ROWS, LANES), h2.dtype),
        scratch_shapes=[pltpu.VMEM((MOE_BM * TOKEN_ROWS, LANES), h2.dtype),
                        pltpu.SemaphoreType.DMA(()), pltpu.SemaphoreType.DMA(())],
        compiler_params=pltpu.CompilerParams(dimension_semantics=("arbitrary",)),
        name="dispatch",
    )(info, n_used, pos, h2)


ROUTE_TILE = 512


def _router_kernel(lg_ref, bias_ref, pos_ref, gate_ref, be_ref, info_ref, rank_scr, ek_scr, *, t, nbp):
    tl = ROUTE_TILE
    ne = N_EXPERTS
    bm = MOE_BM
    row = lax.broadcasted_iota(jnp.int32, (ne, tl), 0)
    tri = (lax.broadcasted_iota(jnp.int32, (tl, tl), 0) < lax.broadcasted_iota(jnp.int32, (tl, tl), 1)).astype(BF16)
    bias = bias_ref[...]

    def select(i, counts):
        cols = pl.ds(pl.multiple_of(i * tl, tl), tl)
        s = jax.nn.sigmoid(lg_ref[:, cols])
        sel = s + bias
        mask = jnp.zeros((ne, tl), F32)
        vals = []
        for k in range(TOP_K):
            best = jnp.max(sel, axis=0, keepdims=True)
            idx = jnp.min(jnp.where(sel == best, row, ne), axis=0, keepdims=True)
            hit = row == idx
            vals.append(jnp.sum(jnp.where(hit, s, 0.0), axis=0, keepdims=True))
            sel = jnp.where(hit, -jnp.inf, sel)
            mask = jnp.where(hit, 1.0, mask)
            ek_scr[k:k + 1, cols] = idx
        total = vals[0]
        for v in vals[1:]:
            total = total + v
        scale = ROUTED_SCALE / total
        for k in range(TOP_K):
            gate_ref[k:k + 1, cols] = vals[k] * scale
        gate_ref[TOP_K:SUBLANES, cols] = jnp.zeros((SUBLANES - TOP_K, tl), F32)
        rank_scr[:, cols] = _dot(mask.astype(BF16), tri) + counts
        return counts + jnp.sum(mask, axis=1, keepdims=True)

    counts = lax.fori_loop(0, t // tl, select, jnp.zeros((ne, 1), F32))
    counts = counts.astype(jnp.int32)
    shift = bm.bit_length() - 1
    assert bm == 1 << shift
    padded = ((counts + (bm - 1)) >> shift) << shift
    e0 = lax.broadcasted_iota(jnp.int32, (ne, ne), 0)
    e1 = lax.broadcasted_iota(jnp.int32, (ne, ne), 1)
    padded_row = jnp.sum(jnp.where(e0 == e1, padded, 0), axis=0, keepdims=True)
    counts_row = jnp.sum(jnp.where(e0 == e1, counts, 0), axis=0, keepdims=True)
    pstart = jnp.sum(jnp.where(e1 < e0, padded_row, 0), axis=1, keepdims=True)
    ustart = jnp.sum(jnp.where(e1 < e0, counts_row, 0), axis=1, keepdims=True)
    pend = pstart + padded
    lane = lax.broadcasted_iota(jnp.int32, (ne, LANES), 1)
    info_ref[...] = jnp.where(lane == 0, counts, jnp.where(lane == 1, pstart, jnp.where(lane == 2, ustart, pend)))
    blk = lax.broadcasted_iota(jnp.int32, (ne, nbp), 1) * bm
    owner = jnp.sum(jnp.where(pend <= blk, 1, 0), axis=0, keepdims=True)
    be_ref[...] = jnp.minimum(owner, ne - 1)
    pstart_f = pstart.astype(F32)

    def place(i, carry):
        cols = pl.ds(pl.multiple_of(i * tl, tl), tl)
        dest = rank_scr[:, cols] + pstart_f
        for k in range(TOP_K):
            hit = row == ek_scr[k:k + 1, cols]
            pos_ref[k:k + 1, cols] = jnp.sum(jnp.where(hit, dest, 0.0), axis=0, keepdims=True).astype(jnp.int32)
        pos_ref[TOP_K:SUBLANES, cols] = jnp.zeros((SUBLANES - TOP_K, tl), jnp.int32)
        return carry

    lax.fori_loop(0, t // tl, place, 0)


def _router(logits_t, router_bias, nb):
    t = logits_t.shape[1]
    nbp = -(-nb // LANES) * LANES
    return pl.pallas_call(
        functools.partial(_router_kernel, t=t, nbp=nbp),
        out_shape=[jax.ShapeDtypeStruct((SUBLANES, t), jnp.int32), jax.ShapeDtypeStruct((SUBLANES, t), F32),
                   jax.ShapeDtypeStruct((1, nbp), jnp.int32), jax.ShapeDtypeStruct((N_EXPERTS, LANES), jnp.int32)],
        scratch_shapes=[pltpu.VMEM((N_EXPERTS, t), F32), pltpu.VMEM((SUBLANES, t), jnp.int32)],
        compiler_params=pltpu.CompilerParams(vmem_limit_bytes=VMEM_LIMIT),
        name="router",
    )(logits_t, router_bias.astype(F32).reshape(N_EXPERTS, 1))


def _dispatch_plan(blk_e_row, info, nb):
    bm = MOE_BM
    pend = info[:, 3]
    blk_e = blk_e_row[0, :nb]
    n_used = pend[-1] // bm
    blk = jnp.arange(nb, dtype=jnp.int32)
    prev_e = jnp.concatenate([jnp.full((1,), -1, jnp.int32), blk_e[:-1]])
    first = jnp.logical_and(blk < n_used, blk_e != prev_e)
    slot = (jnp.cumsum(first.astype(jnp.int32)) - 1) % 2
    first_at = jnp.where(first, blk, nb)
    nxt_first = lax.cummin(jnp.concatenate([first_at[1:], jnp.full((1,), nb, jnp.int32)]), reverse=True)
    nxt = jnp.where(nxt_first < nb, blk_e[jnp.minimum(nxt_first, nb - 1)], -1)
    return (blk_e, first.astype(jnp.int32), slot.astype(jnp.int32), nxt.astype(jnp.int32),
            n_used.astype(jnp.int32).reshape(1))


def _shared_kernel(pos_ref, nxt_ref, x_ref, h_ref, gt_ref, mod_ref, sg_ref, su_ref, sd_ref, fn_ref, eo_hbm,
                   o_ref, buf0, buf1, acc_scr, sem, *, final):
    i = pl.program_id(0)
    n = pl.num_programs(0)
    tm = x_ref.shape[0]
    tr = TOKEN_ROWS

    half = D_MODEL // 2
    grp = SUBLANES

    def fetch_tokens(idx_ref, buf, slot, tok0):
        for tt in range(grp):
            tok = tok0 + tt
            for k in range(TOP_K):
                row = pl.multiple_of(idx_ref[k, tok] * tr, tr)
                pltpu.make_async_copy(eo_hbm.at[pl.ds(row, tr)], buf.at[k, pl.ds(pl.multiple_of(tok * tr, tr), tr)],
                                      sem.at[slot]).start()

    def arrived(buf, slot):
        for k in range(TOP_K):
            pltpu.make_async_copy(eo_hbm.at[pl.ds(0, tm * tr)], buf.at[k], sem.at[slot]).wait()

    def step(cur, cur_slot, nxt, nxt_slot):
        arrived(cur, cur_slot)
        h_lo, h_hi = _unpack_halves(_load_token_tiles(h_ref, tm))
        h_lo = h_lo.astype(BF16)
        h_hi = h_hi.astype(BF16)
        gate = _dot(h_lo, sg_ref[0:half, :]) + _dot(h_hi, sg_ref[half:, :])
        up = _dot(h_lo, su_ref[0:half, :]) + _dot(h_hi, su_ref[half:, :])
        act = gate * jax.nn.sigmoid(gate) * up
        acc_scr[...] = _dot(act.astype(BF16), sd_ref[...])

        def group(c, carry):
            tok0 = pl.multiple_of(c * grp, grp)
            fetch_tokens(nxt_ref, nxt, nxt_slot, tok0)
            gt = gt_ref[pl.ds(tok0, grp), :]
            r_lo = jnp.zeros((grp, half), F32)
            r_hi = r_lo
            for k in range(TOP_K):
                words = jnp.concatenate(
                    [cur[k, pl.ds(tok0 * tr + s, grp, stride=tr), :] for s in range(tr)], axis=1)
                e_lo, e_hi = _unpack_halves(words)
                r_lo = r_lo + gt[:, k:k + 1] * e_lo
                r_hi = r_hi + gt[:, k:k + 1] * e_hi
            acc_scr[pl.ds(tok0, grp), :] += jnp.concatenate([r_lo, r_hi], axis=1)
            return carry

        lax.fori_loop(0, tm // grp, group, 0)
        x = x_ref[...] + mod_ref[0][5:6] * acc_scr[...]
        if final:
            x = _rms(x, fn_ref[...])
        o_ref[...] = x

        @pl.when(i == n - 1)
        def _():
            arrived(nxt, nxt_slot)

    @pl.when(i == 0)
    def _():
        def first(c, carry):
            fetch_tokens(pos_ref, buf0, 0, pl.multiple_of(c * grp, grp))
            return carry
        lax.fori_loop(0, tm // grp, first, 0)

    for parity, cur, nxt in ((0, buf0, buf1), (1, buf1, buf0)):
        @pl.when(i % 2 == parity)
        def _(parity=parity, cur=cur, nxt=nxt):
            step(cur, parity, nxt, 1 - parity)


def _shared(x, h, gates_t, pos, eo, mod, sg, su, sd, final_norm, n_ctx_rows, dec_seq, *, final, row0=0, rows=None):
    rows = x.shape[0] if rows is None else rows
    tm = ROW_TILE // 2
    b0 = row0 // tm
    steps = rows // tm
    grp = lambda i: _group_of(i + b0, tm, n_ctx_rows, dec_seq)
    row = pl.BlockSpec((tm, D_MODEL), lambda i: (i + b0, 0))
    prow = pl.BlockSpec((tm * TOKEN_ROWS, LANES), lambda i: (i + b0, 0))
    full = lambda a, b: pl.BlockSpec((a, b), lambda i: (0, 0))
    buf = pltpu.VMEM((TOP_K, tm * TOKEN_ROWS, LANES), eo.dtype)
    return pl.pallas_call(
        functools.partial(_shared_kernel, final=final),
        grid=(steps,),
        in_specs=[pl.BlockSpec((SUBLANES, tm), lambda i: (0, i + b0), memory_space=pltpu.SMEM),
                  pl.BlockSpec((SUBLANES, tm), lambda i: (0, jnp.minimum(i + 1, steps - 1) + b0),
                               memory_space=pltpu.SMEM),
                  row, prow, pl.BlockSpec((tm, SUBLANES), lambda i: (i + b0, 0)),
                  pl.BlockSpec((1, 6, D_MODEL), lambda i: (grp(i), 0, 0)),
                  full(D_MODEL, D_SHARED), full(D_MODEL, D_SHARED), full(D_SHARED, D_MODEL), full(1, D_MODEL),
                  pl.BlockSpec(memory_space=pl.ANY)],
        out_specs=pl.BlockSpec((tm, D_MODEL), lambda i: (i, 0)),
        out_shape=jax.ShapeDtypeStruct((rows, D_MODEL), F32),
        scratch_shapes=[buf, buf, pltpu.VMEM((tm, D_MODEL), F32), pltpu.SemaphoreType.DMA((2,))],
        compiler_params=_cparams(("arbitrary",)),
        name="shared_final" if final else "shared",
    )(pos, pos, x, h, gates_t, mod, sg.astype(BF16), su.astype(BF16), sd.astype(BF16),
      final_norm.reshape(1, D_MODEL), eo)


def kernel(x_prompt, x_sample, state_ret, state_s5_re, state_s5_im, c, c_ctx, w_ada, b_ada, norm_mix, norm_ffn, w_in, w_out, ret_decay, s5_lam_re, s5_lam_im, s5_log_dt, s5_b_re, s5_b_im, s5_c_re, s5_c_im, s5_d, s5_glu_w, s5_glu_b, hy_conv_w, hy_conv_b, hy_f1_w, hy_f1_b, hy_f2_w, hy_f2_b, hy_f3_w, hy_f3_b, hy_freq, hy_decay, hy_bias, moe_router, moe_router_bias, moe_w_gate, moe_w_up, moe_w_down, sh_w_gate, sh_w_up, sh_w_down, final_norm):
    n_ctx, seq, d = x_prompt.shape
    n_dec, dec_seq, _ = x_sample.shape
    n_ctx_rows = n_ctx * seq
    t = n_ctx_rows + n_dec * dec_seq

    x = jnp.concatenate([x_prompt.reshape(n_ctx_rows, d), x_sample.reshape(n_dec * dec_seq, d)], axis=0)
    cond = jnp.concatenate([c_ctx[None, :], c], axis=0)
    cond8 = jnp.pad(cond, ((0, SUBLANES - cond.shape[0]), (0, 0)))
    mods = _ada(cond8, w_ada, b_ada)[:, :1 + n_dec].reshape(DEPTH, 1 + n_dec, 6, d)

    cos2, sin2 = _rope_tables(dec_seq)
    no_rope = jnp.zeros((seq, LANES), F32)
    zero_ret = jnp.zeros((n_ctx, 2, RET_HEADS, RET_DK, RET_DV), F32)
    zero_s5 = jnp.zeros((n_ctx, 2, S5_GROUPS, S5_STATE), F32)
    dft_ctx = _dft_mats(seq)
    dft_dec = _dft_mats(dec_seq)

    ret_states = jnp.zeros((n_ctx, DEPTH, 2, RET_HEADS, RET_DK, RET_DV), F32)
    s5r_list, s5i_list = [], []
    for l in range(DEPTH):
        mod = mods[l]
        proj = _in_proj(x, mod, norm_mix[l], w_in[l].astype(BF16), n_ctx_rows, dec_seq)

        log_gamma = jax.nn.log_sigmoid(ret_decay[l].astype(F32))
        ret_o, ret_states = _retention(proj, log_gamma, zero_ret, no_rope, no_rope,
                                       row0=0, n_seq=n_ctx, seq_len=seq, hb=RET_HEADS, rope=False,
                                       dst=jnp.zeros((t, RET_WIDTH), BF16), states=ret_states, layer=l)
        ret_o, _ = _retention(proj, log_gamma, state_ret[:, l].astype(F32), cos2, sin2,
                              row0=n_ctx_rows, n_seq=n_dec, seq_len=dec_seq, hb=2, rope=True, dst=ret_o)

        mats = _s5_mats(s5_lam_re[l], s5_lam_im[l], s5_log_dt[l], s5_b_re[l], s5_b_im[l],
                        s5_c_re[l], s5_c_im[l], s5_d[l])
        s5_y, s5_re, s5_im = _s5(proj, mats, zero_s5, zero_s5, row0=0, n_seq=n_ctx, seq_len=seq,
                                 dst=jnp.zeros((t, S5_WIDTH), F32))
        s5_y, _, _ = _s5(proj, mats, state_s5_re[:, l], state_s5_im[:, l],
                         row0=n_ctx_rows, n_seq=n_dec, seq_len=dec_seq, dst=s5_y)
        s5r_list.append(s5_re)
        s5i_list.append(s5_im)

        filt = (hy_f1_w[l], hy_f1_b[l], hy_f2_w[l], hy_f2_b[l], hy_f3_w[l], hy_f3_b[l], hy_freq[l], hy_decay[l])
        hy_o = _hyena(proj, hy_conv_w[l], hy_conv_b[l], hy_bias[l], dft_ctx, _hy_filter_mults(seq, *filt),
                      row0=0, n_seq=n_ctx, seq_len=seq, cb=HY_WIDTH, tk=seq, dst=jnp.zeros((t, HY_WIDTH), BF16))
        hy_o = _hyena(proj, hy_conv_w[l], hy_conv_b[l], hy_bias[l], dft_dec, _hy_filter_mults(dec_seq, *filt),
                      row0=n_ctx_rows, n_seq=n_dec, seq_len=dec_seq, cb=HY_WIDTH // 2, tk=512, dst=hy_o)

        x, h2, logits = _out_proj(x, ret_o, s5_y, hy_o, mod, norm_ffn[l], s5_glu_w[l], s5_glu_b[l],
                                  w_out[l], moe_router[l], n_ctx_rows, dec_seq)

        nb = -(-(t * TOP_K) // MOE_BM) + N_EXPERTS
        pos, gates, blk_e_row, info = _router(logits, moe_router_bias[l], nb)
        blk_e, first, slot, nxt, n_used = _dispatch_plan(blk_e_row, info, nb)
        xs = _dispatch(h2, pos, info, n_used, nb)
        eo = _moe_grouped(xs, blk_e, first, slot, nxt, n_used, moe_w_gate, moe_w_up, moe_w_down, l)
        gates_t = gates.T

        sh = (sh_w_gate[l], sh_w_up[l], sh_w_down[l])
        if l < DEPTH - 1:
            x = _shared(x, h2, gates_t, pos, eo, mod, *sh, final_norm, n_ctx_rows, dec_seq, final=False)
        else:
            y_c = _shared(x, h2, gates_t, pos, eo, mod, *sh, final_norm, n_ctx_rows, dec_seq, final=True,
                          row0=0, rows=n_ctx_rows)
            y_d = _shared(x, h2, gates_t, pos, eo, mod, *sh, final_norm, n_ctx_rows, dec_seq, final=True,
                          row0=n_ctx_rows, rows=n_dec * dec_seq)

    return (y_c.reshape(n_ctx, seq, d), y_d.reshape(n_dec, dec_seq, d),
            ret_states, jnp.stack(s5r_list, axis=1), jnp.stack(s5i_list, axis=1))
```

```python
import functools
import math

import jax
import jax.numpy as jnp
from jax import lax
from jax.experimental import pallas as pl
from jax.experimental.pallas import tpu as pltpu

F32 = jnp.float32
BF16 = jnp.bfloat16

D_MODEL = 2048
DEPTH = 2
GRID_W = 64
RET_HEADS = 8
RET_DK = 128
RET_DV = 128
RET_WIDTH = RET_HEADS * RET_DV
RET_CHUNK = 128
ROPE_BASE = 10000.0
S5_WIDTH = 512
S5_GROUP = 16
S5_GROUPS = S5_WIDTH // S5_GROUP
S5_STATE = 64
S5_Q = 16
HY_WIDTH = 512
HY_BANDS = 16
IN_WIDTH = 4 * RET_WIDTH + S5_WIDTH + 3 * HY_WIDTH
U_COL = 4 * RET_WIDTH
HY_COL = U_COL + S5_WIDTH
N_EXPERTS = 64
TOP_K = 6
D_EXPERT = 512
D_SHARED = 512
ROUTED_SCALE = 2.5
EPS = 1e-6

LANES = 128
SUBLANES = 8
VMEM_LIMIT = 56 * 1024 * 1024

ROW_TILE = 512
MOE_BM = 256


def _cparams(sem):
    return pltpu.CompilerParams(dimension_semantics=sem, vmem_limit_bytes=VMEM_LIMIT)


def _dot(a, b):
    return jnp.dot(a, b, preferred_element_type=F32)


def _rms(x, g):
    var = jnp.mean(x * x, axis=-1, keepdims=True)
    return x * lax.rsqrt(var + EPS) * g


def _pack_halves(xb):
    n = xb.shape[1] // 2
    lo = lax.bitcast_convert_type(xb[:, :n].astype(F32), jnp.uint32) >> 16
    hi = lax.bitcast_convert_type(xb[:, n:].astype(F32), jnp.uint32)
    return lo | hi


def _unpack_halves(w):
    lo = lax.bitcast_convert_type(w << 16, F32)
    hi = lax.bitcast_convert_type(w & jnp.uint32(0xFFFF0000), F32)
    return lo, hi


TOKEN_ROWS = D_MODEL // 2 // LANES


def _store_token_tiles(ref, w):
    m = w.shape[0]
    for s in range(TOKEN_ROWS):
        ref[pl.ds(s, m, stride=TOKEN_ROWS), :] = w[:, s * LANES:(s + 1) * LANES]


def _load_token_tiles(ref, m):
    return jnp.concatenate([ref[pl.ds(s, m, stride=TOKEN_ROWS), :] for s in range(TOKEN_ROWS)], axis=1)


def _ada_kernel(c_ref, w_ref, b_ref, o_ref):
    c = c_ref[...]
    s = (c * jax.nn.sigmoid(c)).astype(BF16)
    o_ref[0] = _dot(s, w_ref[0].astype(BF16)) + b_ref[0]


def _ada(cond8, w_ada, b_ada):
    tn = 512
    n = w_ada.shape[-1]
    return pl.pallas_call(
        _ada_kernel,
        grid=(DEPTH, n // tn),
        in_specs=[
            pl.BlockSpec((SUBLANES, D_MODEL), lambda l, j: (0, 0)),
            pl.BlockSpec((1, D_MODEL, tn), lambda l, j: (l, 0, j)),
            pl.BlockSpec((1, 1, tn), lambda l, j: (l, 0, j)),
        ],
        out_specs=pl.BlockSpec((1, SUBLANES, tn), lambda l, j: (l, 0, j)),
        out_shape=jax.ShapeDtypeStruct((DEPTH, SUBLANES, n), F32),
        compiler_params=_cparams(("parallel", "parallel")),
        name="ada",
    )(cond8, w_ada, b_ada.reshape(DEPTH, 1, n))


def _group_of(i, tile, n_ctx_rows, dec_seq):
    ctx_tiles = n_ctx_rows // tile
    per = dec_seq // tile
    return jnp.where(i < ctx_tiles, 0, 1 + (i - ctx_tiles) // per)


def _in_kernel(x_ref, mod_ref, g_ref, w_ref, o_ref, h_scr):
    @pl.when(pl.program_id(1) == 0)
    def _():
        m = mod_ref[0]
        h = _rms(x_ref[...], g_ref[...]) * (1.0 + m[1:2]) + m[0:1]
        h_scr[...] = h.astype(BF16)

    o_ref[...] = _dot(h_scr[...], w_ref[...])


def _in_proj(x, mod, g, w_bf, n_ctx_rows, dec_seq):
    t = x.shape[0]
    tm, tn = 1024, 1024
    grp = functools.partial(_group_of, tile=tm, n_ctx_rows=n_ctx_rows, dec_seq=dec_seq)
    return pl.pallas_call(
        _in_kernel,
        grid=(t // tm, IN_WIDTH // tn),
        in_specs=[
            pl.BlockSpec((tm, D_MODEL), lambda i, j: (i, 0)),
            pl.BlockSpec((1, 6, D_MODEL), lambda i, j: (grp(i), 0, 0)),
            pl.BlockSpec((1, D_MODEL), lambda i, j: (0, 0)),
            pl.BlockSpec((D_MODEL, tn), lambda i, j: (0, j)),
        ],
        out_specs=pl.BlockSpec((tm, tn), lambda i, j: (i, j)),
        out_shape=jax.ShapeDtypeStruct((t, IN_WIDTH), F32),
        scratch_shapes=[pltpu.VMEM((tm, D_MODEL), BF16)],
        compiler_params=_cparams(("parallel", "arbitrary")),
        name="in_proj",
    )(x, mod, g.reshape(1, D_MODEL), w_bf)


def _ret_kernel(lg_ref, q_ref, k_ref, v_ref, gt_ref, cos_ref, sin_ref, s0_ref,
                o_ref, sfin_ref, acc_scr, q_scr, k_scr, *, seq_len, hb, rope):
    c = RET_CHUNK
    n_chunks = seq_len // c
    ii = lax.broadcasted_iota(jnp.int32, (c, c), 0)
    jj = lax.broadcasted_iota(jnp.int32, (c, c), 1)
    rel = (ii - jj).astype(F32)
    ci = lax.broadcasted_iota(jnp.int32, (c, 1), 0).astype(F32)
    one = jnp.ones((1, 1), F32)
    tdot = functools.partial(lax.dot_general, preferred_element_type=F32)

    def make_head(hh):
        head = pl.program_id(1) * hb + hh
        lgf = lg_ref[0, head]
        lgb = lg_ref[1, head]
        dmask = (jnp.where(rel >= 0, jnp.exp(lgf * jnp.maximum(rel, 0.0)), 0.0)
                 + jnp.where(rel <= 0, jnp.exp(lgb * jnp.maximum(-rel, 0.0)), 0.0))
        qd_f = jnp.exp(lgf * (ci + 1.0))
        kd_f = jnp.exp(lgf * (c - 1.0 - ci))
        cd_f = jnp.exp(lgf * c * one)
        qd_b = jnp.exp(lgb * (c - ci))
        kd_b = jnp.exp(lgb * ci)
        cd_b = jnp.exp(lgb * c * one)
        lanes = slice(hh * LANES, (hh + 1) * LANES)

        def rows_of(n):
            if isinstance(n, int):
                return slice(n * c, (n + 1) * c)
            return pl.ds(pl.multiple_of(n * c, c), c)

        def fwd_chunk(n, s_f):
            rows = rows_of(n)
            q = q_ref[rows, lanes]
            k = k_ref[rows, lanes] * (RET_DK ** -0.5)
            if rope:
                cs = cos_ref[rows, :]
                sn = sin_ref[rows, :]
                q = q * cs + pltpu.roll(q, RET_DK // 2, 1) * sn
                k = k * cs + pltpu.roll(k, RET_DK // 2, 1) * sn
            qb = q.astype(BF16)
            vb = v_ref[rows, lanes].astype(BF16)
            q_scr[rows, lanes] = qb
            k_scr[rows, lanes] = k
            scores = tdot(qb, k.astype(BF16), (((1,), (1,)), ((), ()))) * dmask
            inner = _dot(scores.astype(BF16), vb)
            cross = _dot(qb, s_f.astype(BF16)) * qd_f
            acc_scr[rows, lanes] = inner + cross
            upd = tdot((k * kd_f).astype(BF16), vb, (((0,), (0,)), ((), ())))
            return s_f * cd_f + upd

        def bwd_chunk(m, s_b):
            n = n_chunks - 1 - m
            rows = rows_of(n)
            qb = q_scr[rows, lanes]
            k = k_scr[rows, lanes]
            vb = v_ref[rows, lanes].astype(BF16)
            o = acc_scr[rows, lanes] + _dot(qb, s_b.astype(BF16)) * qd_b
            mu = jnp.mean(o, axis=-1, keepdims=True)
            oc = o - mu
            var = jnp.mean(oc * oc, axis=-1, keepdims=True)
            o = oc * lax.rsqrt(var + EPS)
            g = gt_ref[rows, lanes]
            o_ref[rows, lanes] = (g * jax.nn.sigmoid(g) * o).astype(o_ref.dtype)
            upd = tdot((k * kd_b).astype(BF16), vb, (((0,), (0,)), ((), ())))
            return s_b * cd_b + upd

        return fwd_chunk, bwd_chunk

    if n_chunks <= 4:
        for hh in range(hb):
            fwd_chunk, bwd_chunk = make_head(hh)
            s_f = s0_ref[0, 0, hh]
            s_b = s0_ref[0, 1, hh]
            for n in range(n_chunks):
                s_f = fwd_chunk(n, s_f)
            for m in range(n_chunks):
                s_b = bwd_chunk(m, s_b)
            sfin_ref[0, 0, hh] = s_f
            sfin_ref[0, 1, hh] = s_b
    else:
        fns = [make_head(hh) for hh in range(hb)]
        s_f = lax.fori_loop(0, n_chunks, lambda n, ss: tuple(f[0](n, s) for f, s in zip(fns, ss)),
                            tuple(s0_ref[0, 0, hh] for hh in range(hb)))
        s_b = lax.fori_loop(0, n_chunks, lambda m, ss: tuple(f[1](m, s) for f, s in zip(fns, ss)),
                            tuple(s0_ref[0, 1, hh] for hh in range(hb)))
        for hh in range(hb):
            sfin_ref[0, 0, hh] = s_f[hh]
            sfin_ref[0, 1, hh] = s_b[hh]


def _into(kernel_fn, n_in, dst):
    dsts = [d for d in (dst if isinstance(dst, (list, tuple)) else [dst])]
    outs = [k for k, d in enumerate(dsts) if d is not None]
    if not outs:
        return kernel_fn, [], [], {}

    def body(*refs):
        return kernel_fn(*refs[:n_in], *refs[n_in + len(outs):])

    return (body, [pl.BlockSpec(memory_space=pl.ANY)] * len(outs), [dsts[k] for k in outs],
            {n_in + pos: k for pos, k in enumerate(outs)})


def _retention(proj, log_gamma, s0, cos2, sin2, *, row0, n_seq, seq_len, hb, rope, dst=None,
               states=None, layer=0):
    blk0 = row0 // seq_len
    body, dst_spec, dst_arg, alias = _into(
        functools.partial(_ret_kernel, seq_len=seq_len, hb=hb, rope=rope), 8, [dst, states])
    if states is None:
        st_spec = pl.BlockSpec((1, 2, hb, RET_DK, RET_DV), lambda b, h, lg: (b, 0, h, 0, 0))
        st_shape = jax.ShapeDtypeStruct((n_seq, 2, RET_HEADS, RET_DK, RET_DV), F32)
    else:
        st_spec = pl.BlockSpec((1, None, 2, hb, RET_DK, RET_DV), lambda b, h, lg: (b, layer, 0, h, 0, 0))
        st_shape = jax.ShapeDtypeStruct(states.shape, F32)
    w = hb * LANES
    hblocks = RET_HEADS // hb
    col = lambda part: (lambda b, h, lg: (blk0 + b, part * hblocks + h))
    grid_spec = pltpu.PrefetchScalarGridSpec(
        num_scalar_prefetch=1,
        grid=(n_seq, hblocks),
        in_specs=[
            pl.BlockSpec((seq_len, w), col(0)),
            pl.BlockSpec((seq_len, w), col(1)),
            pl.BlockSpec((seq_len, w), col(2)),
            pl.BlockSpec((seq_len, w), col(3)),
            pl.BlockSpec((seq_len, LANES), lambda b, h, lg: (0, 0)),
            pl.BlockSpec((seq_len, LANES), lambda b, h, lg: (0, 0)),
            pl.BlockSpec((1, 2, hb, RET_DK, RET_DV), lambda b, h, lg: (b, 0, h, 0, 0)),
        ] + dst_spec,
        out_specs=[
            pl.BlockSpec((seq_len, w), lambda b, h, lg: (blk0 + b, h)),
            st_spec,
        ],
        scratch_shapes=[
            pltpu.VMEM((seq_len, w), F32),
            pltpu.VMEM((seq_len, w), BF16),
            pltpu.VMEM((seq_len, w), F32),
        ],
    )
    return pl.pallas_call(
        body,
        grid_spec=grid_spec,
        out_shape=[jax.ShapeDtypeStruct((proj.shape[0], RET_WIDTH), BF16), st_shape],
        input_output_aliases=alias,
        compiler_params=_cparams(("parallel", "arbitrary")),
        name="retention",
    )(log_gamma, proj, proj, proj, proj, cos2, sin2, s0, *dst_arg)


def _rope_tables(seq_len):
    rows_n = seq_len // GRID_W
    rows = jnp.repeat(jnp.arange(rows_n, dtype=F32), GRID_W)
    cols = jnp.tile(jnp.arange(GRID_W, dtype=F32), rows_n)
    nf = RET_DK // 4
    inv = ROPE_BASE ** (-jnp.arange(nf, dtype=F32) / nf)
    ang = jnp.concatenate([rows[:, None] * inv, cols[:, None] * inv], axis=-1)
    cs, sn = jnp.cos(ang), jnp.sin(ang)
    return jnp.concatenate([cs, cs], axis=-1), jnp.concatenate([-sn, sn], axis=-1)


S5_GB = LANES // S5_GROUP
S5_W = S5_Q * LANES
S5_SPLIT = 4
S5_SW = S5_GB * S5_STATE
S5_SB = S5_SW // LANES


def _s5_kernel(u_ref, bq_ref, k_ref, cq_ref, ar_ref, ai_ref, d_ref, h0_ref,
               y_ref, hf_ref, ub_scr, sm_scr, hp_scr, hpb_scr, t_scr, *, n_seq, n_chunks):
    s = pl.program_id(1)
    m = n_seq * n_chunks
    q = S5_Q
    nblk = S5_W // LANES
    sb = S5_SB

    per = q // S5_SPLIT
    kw = S5_W // S5_SPLIT

    @pl.when(s == 0)
    def _():
        for j in range(q):
            ub_scr[j // per, :, (j % per) * LANES:(j % per + 1) * LANES] = (
                u_ref[pl.ds(j, m, stride=q), :].astype(BF16))
        for ib in range(S5_SPLIT):
            for j in range(q):
                c0 = (q - 1 - j) * LANES + ib * kw
                t_scr[ib, j * LANES:(j + 1) * LANES, :] = k_ref[0, :, c0:c0 + kw]

    @pl.when(s < S5_SPLIT)
    def _():
        part = _dot(ub_scr[jnp.minimum(s, S5_SPLIT - 1)], bq_ref[0])

        @pl.when(s == 0)
        def _():
            for cb in range(nblk):
                sm_scr[cb] = part[:, cb * LANES:(cb + 1) * LANES]

        @pl.when(s > 0)
        def _():
            for cb in range(nblk):
                sm_scr[cb] += part[:, cb * LANES:(cb + 1) * LANES]

    @pl.when(s == S5_SPLIT - 1)
    def _():
        ar = ar_ref[0]
        ai = ai_ref[0]
        h0 = h0_ref[0]
        blk = lambda a, cb: a[:, cb * LANES:(cb + 1) * LANES]

        def body(n, carry):
            rows_f = pl.ds(n, n_seq, stride=n_chunks)
            rows_b = pl.ds(n_chunks - 1 - n, n_seq, stride=n_chunks)
            new = list(carry)
            for d, rows in ((0, rows_f), (1, rows_b)):
                for c in range(sb):
                    re_i = d * sb + c
                    im_i = (2 + d) * sb + c
                    hr, hi = carry[re_i], carry[im_i]
                    hp_scr[re_i, rows, :] = hr
                    hp_scr[im_i, rows, :] = hi
                    a_r, a_i = blk(ar, re_i), blk(ai, re_i)
                    new[re_i] = a_r * hr - a_i * hi + sm_scr[re_i, rows, :]
                    new[im_i] = a_r * hi + a_i * hr + sm_scr[im_i, rows, :]
            return tuple(new)

        fin = lax.fori_loop(0, n_chunks, body, tuple(blk(h0, cb) for cb in range(nblk)))
        hf_ref[0] = jnp.concatenate(fin, axis=1)
        for cb in range(nblk):
            hpb_scr[:, cb * LANES:(cb + 1) * LANES] = hp_scr[cb].astype(BF16)

    @pl.when(s >= S5_SPLIT)
    def _():
        ub = jnp.concatenate([ub_scr[k] for k in range(S5_SPLIT)], axis=1)
        y = _dot(ub, t_scr[jnp.maximum(s - S5_SPLIT, 0)]) + _dot(hpb_scr[...], cq_ref[0])
        dd = d_ref[0]
        for ii in range(per):
            rows = pl.ds((s - S5_SPLIT) * per + ii, m, stride=q)
            yi = y[:, ii * LANES:(ii + 1) * LANES] + dd * u_ref[rows, :]
            y_ref[rows, :] = jax.nn.gelu(yi)


def _s5_expand_kernel(mc_ref, o_ref, *, xsize, ysize):
    xs, ys, gs = xsize.bit_length() - 1, ysize.bit_length() - 1, S5_GB.bit_length() - 1
    assert xsize == 1 << xs and ysize == 1 << ys and S5_GB == 1 << gs
    cw = o_ref.shape[2]
    nc = mc_ref.shape[2]
    col0 = pl.program_id(1) * cw
    r = lax.broadcasted_iota(jnp.int32, (nc, cw), 0)
    col = lax.broadcasted_iota(jnp.int32, (nc, cw), 1) + col0
    spread = jnp.logical_and(r >> ys == col >> (ys + gs), (r & (ysize - 1)) == (col & (ysize - 1)))
    big = _dot(mc_ref[0], jnp.where(spread, 1.0, 0.0).astype(BF16))
    row = lax.broadcasted_iota(jnp.int32, big.shape, 0)
    colb = lax.broadcasted_iota(jnp.int32, big.shape, 1) + col0
    same = ((row >> xs) & (S5_GB - 1)) == ((colb >> ys) & (S5_GB - 1))
    o_ref[0] = jnp.where(same, big, 0.0).astype(BF16)


def _s5_expand(mc, *, xsize, ysize):
    nb, rows, nc = mc.shape
    cols = nc * S5_GB
    cw = 512
    return pl.pallas_call(
        functools.partial(_s5_expand_kernel, xsize=xsize, ysize=ysize),
        grid=(nb, cols // cw),
        in_specs=[pl.BlockSpec((1, rows, nc), lambda b, j: (b, 0, 0))],
        out_specs=pl.BlockSpec((1, rows, cw), lambda b, j: (b, 0, j)),
        out_shape=jax.ShapeDtypeStruct((nb, rows, cols), BF16),
        compiler_params=_cparams(("parallel", "parallel")),
        name="s5_expand",
    )(mc)


def _s5_mats(lam_re, lam_im, log_dt, b_re, b_im, c_re, c_im, d):
    q, g, p, ch = S5_Q, S5_GROUPS, S5_STATE, S5_GROUP
    lam = lax.complex(jnp.minimum(lam_re.astype(F32), -1e-4), lam_im.astype(F32))
    ldt = lam * jnp.exp(log_dt.astype(F32))[..., None]
    lam_bar = jnp.exp(ldt)
    b_bar = ((lam_bar - 1.0) / lam)[..., None] * lax.complex(b_re.astype(F32), b_im.astype(F32))
    cc = lax.complex(c_re.astype(F32), c_im.astype(F32))
    pw = jnp.exp(ldt[..., None] * jnp.arange(q + 1, dtype=F32))
    hi = lax.Precision.HIGHEST
    lag = jnp.arange(2 * q, dtype=F32) - (q - 1)
    wf = jnp.where(lag >= 0, jnp.exp(ldt[0][..., None] * jnp.maximum(lag, 0.0)), 0.0)
    wb = jnp.where(lag <= 0, jnp.exp(ldt[1][..., None] * jnp.maximum(-lag, 0.0)), 0.0)
    kc = jnp.real(jnp.einsum('gcp,gpd,gpe->gedc', cc[0], wf, b_bar[0], precision=hi)
                  + jnp.einsum('gcp,gpd,gpe->gedc', cc[1], wb, b_bar[1], precision=hi))

    pw_dn = jnp.exp(ldt[..., None] * (q - jnp.arange(q + 1, dtype=F32)))
    bf = pw_dn[0][..., 1:][:, :, :, None] * b_bar[0][:, :, None, :]
    bb = pw[1][..., :q][:, :, :, None] * b_bar[1][:, :, None, :]
    to_rows = lambda m: m.transpose(0, 2, 3, 1).reshape(g, q * ch, p)
    bq = jnp.concatenate([to_rows(jnp.real(bf)), to_rows(jnp.real(bb)),
                          to_rows(jnp.imag(bf)), to_rows(jnp.imag(bb))], axis=-1)

    cf = cc[0].transpose(0, 2, 1)[:, :, None, :] * pw[0][..., 1:][:, :, :, None]
    cb = cc[1].transpose(0, 2, 1)[:, :, None, :] * pw_dn[1][..., :q][:, :, :, None]
    to_cols = lambda m: m.reshape(g, p, q * ch)
    cq = jnp.concatenate([to_cols(jnp.real(cf)), to_cols(jnp.real(cb)),
                          to_cols(-jnp.imag(cf)), to_cols(-jnp.imag(cb))], axis=1)

    gb, nb = S5_GB, g // S5_GB
    rows_of = lambda a, outer, inner: (a.reshape(nb, gb, outer, inner, a.shape[-1]).transpose(0, 2, 1, 3, 4)
                                       .reshape(nb, outer * gb * inner, a.shape[-1]).astype(BF16))
    tm_bd = _s5_expand(kc.reshape(nb, gb * ch, 2 * q * ch).astype(BF16), xsize=ch, ysize=ch)
    bq_bd = _s5_expand(rows_of(bq, q, ch), xsize=ch, ysize=p)
    cq_bd = _s5_expand(rows_of(cq, 4, p), xsize=p, ysize=ch)

    lq = pw[..., q].reshape(2, nb, 1, S5_SW)
    ar = jnp.concatenate([jnp.real(lq[0]), jnp.real(lq[1])], axis=-1)
    ai = jnp.concatenate([jnp.imag(lq[0]), jnp.imag(lq[1])], axis=-1)
    dd = d.astype(F32).reshape(nb, 1, LANES)
    return tm_bd, bq_bd, cq_bd, ar, ai, dd


def _s5(proj, mats, h0_re, h0_im, *, row0, n_seq, seq_len, dst=None):
    q, p = S5_Q, S5_STATE
    n_chunks = seq_len // q
    m = n_seq * n_chunks
    rows = n_seq * seq_len
    nb = S5_GROUPS // S5_GB
    nblk = S5_W // LANES
    kw = S5_W // S5_SPLIT
    tm, bq, cq, ar, ai, dd = mats
    part = lambda a: a.astype(F32).reshape(n_seq, nb, S5_SW)
    h0 = jnp.concatenate([part(h0_re[:, 0]), part(h0_re[:, 1]), part(h0_im[:, 0]), part(h0_im[:, 1])],
                         axis=-1).transpose(1, 0, 2)
    per_b = lambda shape: pl.BlockSpec((1,) + shape, lambda b, s: (b, 0, 0))
    body, dst_spec, dst_arg, alias = _into(functools.partial(_s5_kernel, n_seq=n_seq, n_chunks=n_chunks), 8, dst)
    y, hf = pl.pallas_call(
        body,
        grid=(nb, 2 * S5_SPLIT),
        in_specs=[
            pl.BlockSpec((rows, LANES), lambda b, s: (row0 // rows, U_COL // LANES + b)),
            pl.BlockSpec((1, kw, 4 * S5_SW), lambda b, s: (b, jnp.minimum(s, S5_SPLIT - 1), 0)),
            per_b((LANES, 2 * q * LANES)),
            pl.BlockSpec((1, 4 * S5_SW, kw), lambda b, s: (b, 0, jnp.maximum(s - S5_SPLIT, 0))),
            per_b((1, 2 * S5_SW)), per_b((1, 2 * S5_SW)), per_b((1, LANES)), per_b((n_seq, 4 * S5_SW)),
        ] + dst_spec,
        out_specs=[pl.BlockSpec((rows, LANES), lambda b, s: (row0 // rows, b)), per_b((n_seq, 4 * S5_SW))],
        out_shape=[jax.ShapeDtypeStruct((proj.shape[0], S5_WIDTH), F32),
                   jax.ShapeDtypeStruct((nb, n_seq, 4 * S5_SW), F32)],
        scratch_shapes=[pltpu.VMEM((S5_SPLIT, m, kw), BF16), pltpu.VMEM((nblk, m, LANES), F32),
                        pltpu.VMEM((nblk, m, LANES), F32), pltpu.VMEM((m, 4 * S5_SW), BF16),
                        pltpu.VMEM((S5_SPLIT, S5_W, kw), BF16)],
        input_output_aliases=alias,
        compiler_params=_cparams(("parallel", "arbitrary")),
        name="s5",
    )(proj, bq, tm, cq, ar, ai, dd, h0, *dst_arg)
    hf = hf.reshape(nb, n_seq, 4, S5_GB, p).transpose(1, 2, 0, 3, 4).reshape(n_seq, 4, S5_GROUPS, p)
    return y, hf[:, 0:2], hf[:, 2:4]


def _conv3(x, w, b):
    n = x.shape[0]
    row = lax.broadcasted_iota(jnp.int32, x.shape, 0)
    prev = jnp.where(row == 0, 0.0, pltpu.roll(x, 1, 0))
    nxt = jnp.where(row == n - 1, 0.0, pltpu.roll(x, n - 1, 0))
    return prev * w[0:1] + x * w[1:2] + nxt * w[2:3] + b


def _hy_fwd_kernel(x0_ref, x1_ref, v_ref, w0_ref, w1_ref, wv_ref, b0_ref, b1_ref, bv_ref,
                   fc_ref, fs_ref, m1_ref, m2_ref, m3_ref, p_ref, z_ref, x0c_ref, zb_scr):
    @pl.when(pl.program_id(2) == 0)
    def _():
        z = _conv3(x1_ref[...], w1_ref[...], b1_ref[...]) * _conv3(v_ref[...], wv_ref[...], bv_ref[...])
        z_ref[...] = z
        zb_scr[...] = z.astype(BF16)
        x0c_ref[...] = _conv3(x0_ref[...], w0_ref[...], b0_ref[...])

    zb = zb_scr[...]
    a = _dot(fc_ref[...], zb)
    b = _dot(fs_ref[...], zb)
    m2 = m2_ref[...]
    p_ref[0, 0] = (m1_ref[...] * a + m2 * b).astype(BF16)
    p_ref[0, 1] = (m3_ref[...] * b - m2 * a).astype(BF16)


def _hy_inv_kernel(p_ref, gc_ref, gs_ref, z_ref, x0c_ref, bias_ref, o_ref):
    conv = _dot(gc_ref[...], p_ref[0, 0]) + _dot(gs_ref[...], p_ref[0, 1])
    o_ref[...] = (x0c_ref[...] * (conv + bias_ref[...] * z_ref[...])).astype(o_ref.dtype)


def _dft_mats(seq_len):
    n, w = seq_len, 64
    k = jnp.arange(n, dtype=jnp.int32)
    ang = lambda j: ((k[:, None] * j[None, :]) % (2 * n)).astype(F32) * (math.pi / n)
    ang_a = ang(jnp.arange(n // w, dtype=jnp.int32) * w)
    ang_b = ang(jnp.arange(w, dtype=jnp.int32))
    ca, sa = jnp.cos(ang_a)[:, :, None], jnp.sin(ang_a)[:, :, None]
    cb, sb = jnp.cos(ang_b)[:, None, :], jnp.sin(ang_b)[:, None, :]
    cm = (ca * cb - sa * sb).reshape(n, n)
    sm = -(sa * cb + ca * sb).reshape(n, n)
    nyq = jnp.where(k % 2 == 0, 1.0, -1.0).astype(F32)
    return cm.astype(BF16), sm.at[0, :].set(nyq).astype(BF16), sm.at[:, 0].set(nyq).astype(BF16)


def _hy_filter_mults(seq_len, f1_w, f1_b, f2_w, f2_b, f3_w, f3_b, freq, decay):
    n = seq_len
    t = (jnp.arange(n, dtype=F32) / n)[:, None]
    bands = jnp.arange(1, HY_BANDS + 1, dtype=F32)[None, :]
    z = jnp.concatenate([t, jnp.cos(2.0 * math.pi * t * bands), jnp.sin(2.0 * math.pi * t * bands)], axis=-1)
    hi = lax.Precision.HIGHEST
    fr = freq.astype(F32)
    h = jnp.sin(fr * (jnp.dot(z, f1_w.astype(F32), precision=hi) + f1_b.astype(F32)))
    h = jnp.sin(fr * (jnp.dot(h, f2_w.astype(F32), precision=hi) + f2_b.astype(F32)))
    h = jnp.dot(h, f3_w.astype(F32), precision=hi) + f3_b.astype(F32)
    h = h * jnp.exp(-t * jnp.abs(decay.astype(F32)))
    h = h.reshape(n, 2, HY_WIDTH)
    h = h / jnp.sum(jnp.abs(h), axis=(0, 1), keepdims=True)
    hc = jnp.concatenate([h[:, 0], h[::-1, 1]], axis=0)
    spec = jnp.fft.rfft(hc, axis=0)
    hr, him = jnp.real(spec), jnp.imag(spec)
    wk = jnp.where(jnp.arange(n) == 0, 1.0, 2.0)[:, None] / (2.0 * n)
    first = (jnp.arange(n) == 0)[:, None]
    m1 = hr[:n] * wk
    m2 = jnp.where(first, 0.0, -him[:n]) * wk
    m3 = jnp.where(first, hr[n:n + 1], hr[:n]) * wk
    return m1.astype(F32), m2.astype(F32), m3.astype(F32)


def _hyena(proj, conv_w, conv_b, bias, dft, mults, *, row0, n_seq, seq_len, cb, tk, dst=None):
    blk0 = row0 // seq_len
    nc = HY_WIDTH // cb
    nk = seq_len // tk
    c0 = HY_COL // cb
    cm, sm, smt = dft
    m1, m2, m3 = mults
    xcol = lambda part: pl.BlockSpec((seq_len, cb), lambda b, c, k: (blk0 + b, c0 + part * nc + c))
    wcol = lambda part: pl.BlockSpec((3, cb), lambda b, c, k: (0, part * nc + c))
    bcol = lambda part: pl.BlockSpec((1, cb), lambda b, c, k: (0, part * nc + c))
    frow = pl.BlockSpec((tk, seq_len), lambda b, c, k: (k, 0))
    mblk = pl.BlockSpec((tk, cb), lambda b, c, k: (k, c))
    cb2 = conv_b.reshape(1, 3 * HY_WIDTH)
    pspec, z, x0c = pl.pallas_call(
        _hy_fwd_kernel,
        grid=(n_seq, nc, nk),
        in_specs=[xcol(0), xcol(1), xcol(2), wcol(0), wcol(1), wcol(2), bcol(0), bcol(1), bcol(2),
                  frow, frow, mblk, mblk, mblk],
        out_specs=[
            pl.BlockSpec((1, 2, tk, cb), lambda b, c, k: (b, 0, k, c)),
            pl.BlockSpec((seq_len, cb), lambda b, c, k: (b, c)),
            pl.BlockSpec((seq_len, cb), lambda b, c, k: (b, c)),
        ],
        out_shape=[
            jax.ShapeDtypeStruct((n_seq, 2, seq_len, HY_WIDTH), BF16),
            jax.ShapeDtypeStruct((n_seq * seq_len, HY_WIDTH), F32),
            jax.ShapeDtypeStruct((n_seq * seq_len, HY_WIDTH), F32),
        ],
        scratch_shapes=[pltpu.VMEM((seq_len, cb), BF16)],
        compiler_params=_cparams(("parallel", "parallel", "arbitrary")),
        name="hyena_fwd",
    )(proj, proj, proj, conv_w, conv_w, conv_w, cb2, cb2, cb2, cm, sm, m1, m2, m3)
    grow = pl.BlockSpec((tk, seq_len), lambda b, c, k: (k, 0))
    tile = pl.BlockSpec((tk, cb), lambda b, c, k: (b * nk + k, c))
    body, dst_spec, dst_arg, alias = _into(_hy_inv_kernel, 6, dst)
    return pl.pallas_call(
        body,
        grid=(n_seq, nc, nk),
        in_specs=[pl.BlockSpec((1, 2, seq_len, cb), lambda b, c, k: (b, 0, 0, c)),
                  grow, grow, tile, tile, pl.BlockSpec((1, cb), lambda b, c, k: (0, c))] + dst_spec,
        out_specs=pl.BlockSpec((tk, cb), lambda b, c, k: (row0 // tk + b * nk + k, c)),
        out_shape=jax.ShapeDtypeStruct((proj.shape[0], HY_WIDTH), BF16),
        input_output_aliases=alias,
        compiler_params=_cparams(("parallel", "parallel", "arbitrary")),
        name="hyena_inv",
    )(pspec, cm, smt, z, x0c, bias.reshape(1, HY_WIDTH), *dst_arg)


def _out_kernel(x_ref, ret_ref, s5_ref, hy_ref, mod_ref, g_ref, gw_ref, gb_ref,
                wr_ref, ws_ref, wh_ref, rt_ref, xo_ref, h_ref, lg_ref):
    m = mod_ref[0]
    y = s5_ref[...]
    s5o = y * jax.nn.sigmoid(_dot(y.astype(BF16), gw_ref[...]) + gb_ref[...])
    mix = (_dot(ret_ref[...], wr_ref[...]) + _dot(s5o.astype(BF16), ws_ref[...])
           + _dot(hy_ref[...], wh_ref[...]))
    x = x_ref[...] + m[2:3] * mix
    xo_ref[...] = x
    h = _rms(x, g_ref[...]) * (1.0 + m[4:5]) + m[3:4]
    hb = h.astype(BF16)
    _store_token_tiles(h_ref, _pack_halves(hb))
    lg_ref[...] = lax.dot_general(rt_ref[...], hb, (((1,), (1,)), ((), ())), preferred_element_type=F32)


def _out_proj(x, ret_o, s5_y, hy_o, mod, g, glu_w, glu_b, w_out, router, n_ctx_rows, dec_seq):
    t = x.shape[0]
    tm = ROW_TILE
    grp = functools.partial(_group_of, tile=tm, n_ctx_rows=n_ctx_rows, dec_seq=dec_seq)
    row = lambda w: pl.BlockSpec((tm, w), lambda i: (i, 0))
    full = lambda a, b: pl.BlockSpec((a, b), lambda i: (0, 0))
    wo = w_out.astype(BF16)
    return pl.pallas_call(
        _out_kernel,
        grid=(t // tm,),
        in_specs=[row(D_MODEL), row(RET_WIDTH), row(S5_WIDTH), row(HY_WIDTH),
                  pl.BlockSpec((1, 6, D_MODEL), lambda i: (grp(i), 0, 0)),
                  full(1, D_MODEL), full(S5_WIDTH, S5_WIDTH), full(1, S5_WIDTH),
                  full(RET_WIDTH, D_MODEL), full(S5_WIDTH, D_MODEL), full(HY_WIDTH, D_MODEL),
                  full(N_EXPERTS, D_MODEL)],
        out_specs=[row(D_MODEL), pl.BlockSpec((tm * TOKEN_ROWS, LANES), lambda i: (i, 0)),
                   pl.BlockSpec((N_EXPERTS, tm), lambda i: (0, i))],
        out_shape=[jax.ShapeDtypeStruct((t, D_MODEL), F32), jax.ShapeDtypeStruct((t * TOKEN_ROWS, LANES), jnp.uint32),
                   jax.ShapeDtypeStruct((N_EXPERTS, t), F32)],
        compiler_params=_cparams(("parallel",)),
        name="out_proj",
    )(x, ret_o, s5_y, hy_o, mod, g.reshape(1, D_MODEL), glu_w.astype(BF16), glu_b.reshape(1, S5_WIDTH),
      wo[:RET_WIDTH], wo[RET_WIDTH:RET_WIDTH + S5_WIDTH], wo[RET_WIDTH + S5_WIDTH:], router.T.astype(BF16))


def _moe_kernel(be_ref, first_ref, slot_ref, nxt_ref, nu_ref, xs_ref, wg_hbm, wu_hbm, wd_hbm, o_ref,
                wg_f, wu_f, wd_f, wg_b, wu_b, wd_b, sem, *, layer):
    i = pl.program_id(0)

    def copies(e, s):
        return (pltpu.make_async_copy(wg_hbm.at[layer, e], wg_f.at[s], sem.at[s, 0]),
                pltpu.make_async_copy(wu_hbm.at[layer, e], wu_f.at[s], sem.at[s, 1]),
                pltpu.make_async_copy(wd_hbm.at[layer, e], wd_f.at[s], sem.at[s, 2]))

    @pl.when(i == 0)
    def _():
        for cp in copies(be_ref[0], 0):
            cp.start()

    @pl.when(first_ref[i] == 1)
    def _():
        s = slot_ref[i]
        for cp in copies(be_ref[i], s):
            cp.wait()

        @pl.when(nxt_ref[i] >= 0)
        def _():
            for cp in copies(nxt_ref[i], 1 - s):
                cp.start()

        wg_b[...] = wg_f[s].astype(BF16)
        wu_b[...] = wu_f[s].astype(BF16)
        wd_b[...] = wd_f[s].astype(BF16)

    @pl.when(i < nu_ref[0])
    def _():
        half = D_MODEL // 2
        x_lo, x_hi = _unpack_halves(_load_token_tiles(xs_ref, MOE_BM))
        x_lo = x_lo.astype(BF16)
        x_hi = x_hi.astype(BF16)
        gate = _dot(x_lo, wg_b[0:half, :]) + _dot(x_hi, wg_b[half:, :])
        up = _dot(x_lo, wu_b[0:half, :]) + _dot(x_hi, wu_b[half:, :])
        hb = gate * jax.nn.sigmoid(gate) * up
        _store_token_tiles(o_ref, _pack_halves(_dot(hb.astype(BF16), wd_b[...]).astype(BF16)))

    @pl.when(i >= nu_ref[0])
    def _():
        o_ref[...] = jnp.zeros_like(o_ref)


def _moe_grouped(xs, blk_e, first, slot, nxt, n_used, w_gate, w_up, w_down, layer):
    pr = xs.shape[0] // TOKEN_ROWS
    bm = MOE_BM
    nb = pr // bm
    grid_spec = pltpu.PrefetchScalarGridSpec(
        num_scalar_prefetch=5,
        grid=(nb,),
        in_specs=[
            pl.BlockSpec((bm * TOKEN_ROWS, LANES), lambda i, *_: (i, 0)),
            pl.BlockSpec(memory_space=pl.ANY),
            pl.BlockSpec(memory_space=pl.ANY),
            pl.BlockSpec(memory_space=pl.ANY),
        ],
        out_specs=pl.BlockSpec((bm * TOKEN_ROWS, LANES), lambda i, *_: (i, 0)),
        scratch_shapes=[pltpu.VMEM((2, D_MODEL, D_EXPERT), F32), pltpu.VMEM((2, D_MODEL, D_EXPERT), F32),
                        pltpu.VMEM((2, D_EXPERT, D_MODEL), F32),
                        pltpu.VMEM((D_MODEL, D_EXPERT), BF16), pltpu.VMEM((D_MODEL, D_EXPERT), BF16),
                        pltpu.VMEM((D_EXPERT, D_MODEL), BF16),
                        pltpu.SemaphoreType.DMA((2, 3))],
    )
    return pl.pallas_call(
        functools.partial(_moe_kernel, layer=layer),
        grid_spec=grid_spec,
        out_shape=jax.ShapeDtypeStruct((pr * TOKEN_ROWS, LANES), jnp.uint32),
        compiler_params=_cparams(("arbitrary",)),
        name="moe_grouped",
    )(blk_e, first, slot, nxt, n_used, xs, w_gate, w_up, w_down)


DISPATCH_TILE = 512


def _dispatch_kernel(info_ref, nu_ref, pos_ref, h_ref, xs_out, zbuf, sem, zsem, *, nb):
    tm = pos_ref.shape[1]
    tr = TOKEN_ROWS
    bm = MOE_BM

    @pl.when(pl.program_id(0) == 0)
    def _():
        zbuf[...] = jnp.zeros_like(zbuf)

        def zero_block(first_row):
            rows = pl.ds(pl.multiple_of(first_row * tr, tr), bm * tr)
            return pltpu.make_async_copy(zbuf, xs_out.at[rows], zsem)

        for act in ("start", "wait"):
            def last_of_expert(e, carry, act=act):
                @pl.when(info_ref[e, 3] > info_ref[e, 1])
                def _():
                    getattr(zero_block(info_ref[e, 3] - bm), act)()
                return carry

            def tail_block(b, carry, act=act):
                getattr(zero_block(b * bm), act)()
                return carry

            lax.fori_loop(0, N_EXPERTS, last_of_expert, 0)
            lax.fori_loop(nu_ref[0], nb, tail_block, 0)

    def send(tok, carry):
        src = h_ref.at[pl.ds(pl.multiple_of(tok * tr, tr), tr)]
        for k in range(TOP_K):
            row = pl.multiple_of(pos_ref[k, tok] * tr, tr)
            pltpu.make_async_copy(src, xs_out.at[pl.ds(row, tr)], sem).start(priority=k % 2)
        return carry

    lax.fori_loop(0, tm, send, 0, unroll=4)
    n = tm * TOP_K * tr
    pltpu.make_async_copy(xs_out.at[pl.ds(0, n)], xs_out.at[pl.ds(0, n)], sem).wait()


def _dispatch(h2, pos, info, n_used, nb):
    t = pos.shape[1]
    tm = DISPATCH_TILE
    return pl.pallas_call(
        functools.partial(_dispatch_kernel, nb=nb),
        grid=(t // tm,),
        in_specs=[pl.BlockSpec(memory_space=pltpu.SMEM),
                  pl.BlockSpec(memory_space=pltpu.SMEM),
                  pl.BlockSpec((SUBLANES, tm), lambda i: (0, i), memory_space=pltpu.SMEM),
                  pl.BlockSpec((tm * TOKEN_ROWS, LANES), lambda i: (i, 0))],
        out_specs=pl.BlockSpec(memory_space=pl.ANY),
        out_shape=jax.ShapeDtypeStruct((nb * MOE_BM * TOKEN_ROWS, LANES), h2.dtype),
        scratch_shapes=[pltpu.VMEM((MOE_BM * TOKEN_ROWS, LANES), h2.dtype),
                        pltpu.SemaphoreType.DMA(()), pltpu.SemaphoreType.DMA(())],
        compiler_params=pltpu.CompilerParams(dimension_semantics=("arbitrary",)),
        name="dispatch",
    )(info, n_used, pos, h2)


ROUTE_TILE = 512


def _router_kernel(lg_ref, bias_ref, pos_ref, gate_ref, be_ref, info_ref, rank_scr, ek_scr, *, t, nbp):
    tl = ROUTE_TILE
    ne = N_EXPERTS
    bm = MOE_BM
    row = lax.broadcasted_iota(jnp.int32, (ne, tl), 0)
    tri = (lax.broadcasted_iota(jnp.int32, (tl, tl), 0) < lax.broadcasted_iota(jnp.int32, (tl, tl), 1)).astype(BF16)
    bias = bias_ref[...]

    def select(i, counts):
        cols = pl.ds(pl.multiple_of(i * tl, tl), tl)
        s = jax.nn.sigmoid(lg_ref[:, cols])
        sel = s + bias
        mask = jnp.zeros((ne, tl), F32)
        vals = []
        for k in range(TOP_K):
            best = jnp.max(sel, axis=0, keepdims=True)
            idx = jnp.min(jnp.where(sel == best, row, ne), axis=0, keepdims=True)
            hit = row == idx
            vals.append(jnp.sum(jnp.where(hit, s, 0.0), axis=0, keepdims=True))
            sel = jnp.where(hit, -jnp.inf, sel)
            mask = jnp.where(hit, 1.0, mask)
            ek_scr[k:k + 1, cols] = idx
        total = vals[0]
        for v in vals[1:]:
            total = total + v
        scale = ROUTED_SCALE / total
        for k in range(TOP_K):
            gate_ref[k:k + 1, cols] = vals[k] * scale
        gate_ref[TOP_K:SUBLANES, cols] = jnp.zeros((SUBLANES - TOP_K, tl), F32)
        rank_scr[:, cols] = _dot(mask.astype(BF16), tri) + counts
        return counts + jnp.sum(mask, axis=1, keepdims=True)

    counts = lax.fori_loop(0, t // tl, select, jnp.zeros((ne, 1), F32))
    counts = counts.astype(jnp.int32)
    shift = bm.bit_length() - 1
    assert bm == 1 << shift
    padded = ((counts + (bm - 1)) >> shift) << shift
    e0 = lax.broadcasted_iota(jnp.int32, (ne, ne), 0)
    e1 = lax.broadcasted_iota(jnp.int32, (ne, ne), 1)
    padded_row = jnp.sum(jnp.where(e0 == e1, padded, 0), axis=0, keepdims=True)
    counts_row = jnp.sum(jnp.where(e0 == e1, counts, 0), axis=0, keepdims=True)
    pstart = jnp.sum(jnp.where(e1 < e0, padded_row, 0), axis=1, keepdims=True)
    ustart = jnp.sum(jnp.where(e1 < e0, counts_row, 0), axis=1, keepdims=True)
    pend = pstart + padded
    lane = lax.broadcasted_iota(jnp.int32, (ne, LANES), 1)
    info_ref[...] = jnp.where(lane == 0, counts, jnp.where(lane == 1, pstart, jnp.where(lane == 2, ustart, pend)))
    blk = lax.broadcasted_iota(jnp.int32, (ne, nbp), 1) * bm
    owner = jnp.sum(jnp.where(pend <= blk, 1, 0), axis=0, keepdims=True)
    be_ref[...] = jnp.minimum(owner, ne - 1)
    pstart_f = pstart.astype(F32)

    def place(i, carry):
        cols = pl.ds(pl.multiple_of(i * tl, tl), tl)
        dest = rank_scr[:, cols] + pstart_f
        for k in range(TOP_K):
            hit = row == ek_scr[k:k + 1, cols]
            pos_ref[k:k + 1, cols] = jnp.sum(jnp.where(hit, dest, 0.0), axis=0, keepdims=True).astype(jnp.int32)
        pos_ref[TOP_K:SUBLANES, cols] = jnp.zeros((SUBLANES - TOP_K, tl), jnp.int32)
        return carry

    lax.fori_loop(0, t // tl, place, 0)


def _router(logits_t, router_bias, nb):
    t = logits_t.shape[1]
    nbp = -(-nb // LANES) * LANES
    return pl.pallas_call(
        functools.partial(_router_kernel, t=t, nbp=nbp),
        out_shape=[jax.ShapeDtypeStruct((SUBLANES, t), jnp.int32), jax.ShapeDtypeStruct((SUBLANES, t), F32),
                   jax.ShapeDtypeStruct((1, nbp), jnp.int32), jax.ShapeDtypeStruct((N_EXPERTS, LANES), jnp.int32)],
        scratch_shapes=[pltpu.VMEM((N_EXPERTS, t), F32), pltpu.VMEM((SUBLANES, t), jnp.int32)],
        compiler_params=pltpu.CompilerParams(vmem_limit_bytes=VMEM_LIMIT),
        name="router",
    )(logits_t, router_bias.astype(F32).reshape(N_EXPERTS, 1))


def _dispatch_plan(blk_e_row, info, nb):
    bm = MOE_BM
    pend = info[:, 3]
    blk_e = blk_e_row[0, :nb]
    n_used = pend[-1] // bm
    blk = jnp.arange(nb, dtype=jnp.int32)
    prev_e = jnp.concatenate([jnp.full((1,), -1, jnp.int32), blk_e[:-1]])
    first = jnp.logical_and(blk < n_used, blk_e != prev_e)
    slot = (jnp.cumsum(first.astype(jnp.int32)) - 1) % 2
    first_at = jnp.where(first, blk, nb)
    nxt_first = lax.cummin(jnp.concatenate([first_at[1:], jnp.full((1,), nb, jnp.int32)]), reverse=True)
    nxt = jnp.where(nxt_first < nb, blk_e[jnp.minimum(nxt_first, nb - 1)], -1)
    return (blk_e, first.astype(jnp.int32), slot.astype(jnp.int32), nxt.astype(jnp.int32),
            n_used.astype(jnp.int32).reshape(1))


def _shared_kernel(pos_ref, nxt_ref, x_ref, h_ref, gt_ref, mod_ref, sg_ref, su_ref, sd_ref, fn_ref, eo_hbm,
                   o_ref, buf0, buf1, acc_scr, sem, *, final):
    i = pl.program_id(0)
    n = pl.num_programs(0)
    tm = x_ref.shape[0]
    tr = TOKEN_ROWS

    half = D_MODEL // 2
    grp = SUBLANES

    def fetch_tokens(idx_ref, buf, slot, tok0):
        for tt in range(grp):
            tok = tok0 + tt
            for k in range(TOP_K):
                row = pl.multiple_of(idx_ref[k, tok] * tr, tr)
                pltpu.make_async_copy(eo_hbm.at[pl.ds(row, tr)], buf.at[k, pl.ds(pl.multiple_of(tok * tr, tr), tr)],
                                      sem.at[slot]).start(priority=k % 2)

    def arrived(buf, slot):
        for k in range(TOP_K):
            pltpu.make_async_copy(eo_hbm.at[pl.ds(0, tm * tr)], buf.at[k], sem.at[slot]).wait()

    def step(cur, cur_slot, nxt, nxt_slot):
        arrived(cur, cur_slot)
        h_lo, h_hi = _unpack_halves(_load_token_tiles(h_ref, tm))
        h_lo = h_lo.astype(BF16)
        h_hi = h_hi.astype(BF16)
        gate = _dot(h_lo, sg_ref[0:half, :]) + _dot(h_hi, sg_ref[half:, :])
        up = _dot(h_lo, su_ref[0:half, :]) + _dot(h_hi, su_ref[half:, :])
        act = gate * jax.nn.sigmoid(gate) * up
        acc_scr[...] = _dot(act.astype(BF16), sd_ref[...])

        def group(c, carry):
            tok0 = pl.multiple_of(c * grp, grp)
            fetch_tokens(nxt_ref, nxt, nxt_slot, tok0)
            gt = gt_ref[pl.ds(tok0, grp), :]
            r_lo = jnp.zeros((grp, half), F32)
            r_hi = r_lo
            for k in range(TOP_K):
                words = jnp.concatenate(
                    [cur[k, pl.ds(tok0 * tr + s, grp, stride=tr), :] for s in range(tr)], axis=1)
                e_lo, e_hi = _unpack_halves(words)
                r_lo = r_lo + gt[:, k:k + 1] * e_lo
                r_hi = r_hi + gt[:, k:k + 1] * e_hi
            acc_scr[pl.ds(tok0, grp), :] += jnp.concatenate([r_lo, r_hi], axis=1)
            return carry

        lax.fori_loop(0, tm // grp, group, 0)
        x = x_ref[...] + mod_ref[0][5:6] * acc_scr[...]
        if final:
            x = _rms(x, fn_ref[...])
        o_ref[...] = x

        @pl.when(i == n - 1)
        def _():
            arrived(nxt, nxt_slot)

    @pl.when(i == 0)
    def _():
        def first(c, carry):
            fetch_tokens(pos_ref, buf0, 0, pl.multiple_of(c * grp, grp))
            return carry
        lax.fori_loop(0, tm // grp, first, 0)

    for parity, cur, nxt in ((0, buf0, buf1), (1, buf1, buf0)):
        @pl.when(i % 2 == parity)
        def _(parity=parity, cur=cur, nxt=nxt):
            step(cur, parity, nxt, 1 - parity)


def _shared(x, h, gates_t, pos, eo, mod, sg, su, sd, final_norm, n_ctx_rows, dec_seq, *, final, row0=0, rows=None):
    rows = x.shape[0] if rows is None else rows
    tm = ROW_TILE // 2
    b0 = row0 // tm
    steps = rows // tm
    grp = lambda i: _group_of(i + b0, tm, n_ctx_rows, dec_seq)
    row = pl.BlockSpec((tm, D_MODEL), lambda i: (i + b0, 0))
    prow = pl.BlockSpec((tm * TOKEN_ROWS, LANES), lambda i: (i + b0, 0))
    full = lambda a, b: pl.BlockSpec((a, b), lambda i: (0, 0))
    buf = pltpu.VMEM((TOP_K, tm * TOKEN_ROWS, LANES), eo.dtype)
    return pl.pallas_call(
        functools.partial(_shared_kernel, final=final),
        grid=(steps,),
        in_specs=[pl.BlockSpec((SUBLANES, tm), lambda i: (0, i + b0), memory_space=pltpu.SMEM),
                  pl.BlockSpec((SUBLANES, tm), lambda i: (0, jnp.minimum(i + 1, steps - 1) + b0),
                               memory_space=pltpu.SMEM),
                  row, prow, pl.BlockSpec((tm, SUBLANES), lambda i: (i + b0, 0)),
                  pl.BlockSpec((1, 6, D_MODEL), lambda i: (grp(i), 0, 0)),
                  full(D_MODEL, D_SHARED), full(D_MODEL, D_SHARED), full(D_SHARED, D_MODEL), full(1, D_MODEL),
                  pl.BlockSpec(memory_space=pl.ANY)],
        out_specs=pl.BlockSpec((tm, D_MODEL), lambda i: (i, 0)),
        out_shape=jax.ShapeDtypeStruct((rows, D_MODEL), F32),
        scratch_shapes=[buf, buf, pltpu.VMEM((tm, D_MODEL), F32), pltpu.SemaphoreType.DMA((2,))],
        compiler_params=_cparams(("arbitrary",)),
        name="shared_final" if final else "shared",
    )(pos, pos, x, h, gates_t, mod, sg.astype(BF16), su.astype(BF16), sd.astype(BF16),
      final_norm.reshape(1, D_MODEL), eo)


def kernel(x_prompt, x_sample, state_ret, state_s5_re, state_s5_im, c, c_ctx, w_ada, b_ada, norm_mix, norm_ffn, w_in, w_out, ret_decay, s5_lam_re, s5_lam_im, s5_log_dt, s5_b_re, s5_b_im, s5_c_re, s5_c_im, s5_d, s5_glu_w, s5_glu_b, hy_conv_w, hy_conv_b, hy_f1_w, hy_f1_b, hy_f2_w, hy_f2_b, hy_f3_w, hy_f3_b, hy_freq, hy_decay, hy_bias, moe_router, moe_router_bias, moe_w_gate, moe_w_up, moe_w_down, sh_w_gate, sh_w_up, sh_w_down, final_norm):
    n_ctx, seq, d = x_prompt.shape
    n_dec, dec_seq, _ = x_sample.shape
    n_ctx_rows = n_ctx * seq
    t = n_ctx_rows + n_dec * dec_seq

    x = jnp.concatenate([x_prompt.reshape(n_ctx_rows, d), x_sample.reshape(n_dec * dec_seq, d)], axis=0)
    cond = jnp.concatenate([c_ctx[None, :], c], axis=0)
    cond8 = jnp.pad(cond, ((0, SUBLANES - cond.shape[0]), (0, 0)))
    mods = _ada(cond8, w_ada, b_ada)[:, :1 + n_dec].reshape(DEPTH, 1 + n_dec, 6, d)

    cos2, sin2 = _rope_tables(dec_seq)
    no_rope = jnp.zeros((seq, LANES), F32)
    zero_ret = jnp.zeros((n_ctx, 2, RET_HEADS, RET_DK, RET_DV), F32)
    zero_s5 = jnp.zeros((n_ctx, 2, S5_GROUPS, S5_STATE), F32)
    dft_ctx = _dft_mats(seq)
    dft_dec = _dft_mats(dec_seq)

    ret_states = jnp.zeros((n_ctx, DEPTH, 2, RET_HEADS, RET_DK, RET_DV), F32)
    s5r_list, s5i_list = [], []
    for l in range(DEPTH):
        mod = mods[l]
        proj = _in_proj(x, mod, norm_mix[l], w_in[l].astype(BF16), n_ctx_rows, dec_seq)

        log_gamma = jax.nn.log_sigmoid(ret_decay[l].astype(F32))
        ret_o, ret_states = _retention(proj, log_gamma, zero_ret, no_rope, no_rope,
                                       row0=0, n_seq=n_ctx, seq_len=seq, hb=RET_HEADS, rope=False,
                                       dst=jnp.zeros((t, RET_WIDTH), BF16), states=ret_states, layer=l)
        ret_o, _ = _retention(proj, log_gamma, state_ret[:, l].astype(F32), cos2, sin2,
                              row0=n_ctx_rows, n_seq=n_dec, seq_len=dec_seq, hb=2, rope=True, dst=ret_o)

        mats = _s5_mats(s5_lam_re[l], s5_lam_im[l], s5_log_dt[l], s5_b_re[l], s5_b_im[l],
                        s5_c_re[l], s5_c_im[l], s5_d[l])
        s5_y, s5_re, s5_im = _s5(proj, mats, zero_s5, zero_s5, row0=0, n_seq=n_ctx, seq_len=seq,
                                 dst=jnp.zeros((t, S5_WIDTH), F32))
        s5_y, _, _ = _s5(proj, mats, state_s5_re[:, l], state_s5_im[:, l],
                         row0=n_ctx_rows, n_seq=n_dec, seq_len=dec_seq, dst=s5_y)
        s5r_list.append(s5_re)
        s5i_list.append(s5_im)

        filt = (hy_f1_w[l], hy_f1_b[l], hy_f2_w[l], hy_f2_b[l], hy_f3_w[l], hy_f3_b[l], hy_freq[l], hy_decay[l])
        hy_o = _hyena(proj, hy_conv_w[l], hy_conv_b[l], hy_bias[l], dft_ctx, _hy_filter_mults(seq, *filt),
                      row0=0, n_seq=n_ctx, seq_len=seq, cb=HY_WIDTH, tk=seq, dst=jnp.zeros((t, HY_WIDTH), BF16))
        hy_o = _hyena(proj, hy_conv_w[l], hy_conv_b[l], hy_bias[l], dft_dec, _hy_filter_mults(dec_seq, *filt),
                      row0=n_ctx_rows, n_seq=n_dec, seq_len=dec_seq, cb=HY_WIDTH // 2, tk=512, dst=hy_o)

        x, h2, logits = _out_proj(x, ret_o, s5_y, hy_o, mod, norm_ffn[l], s5_glu_w[l], s5_glu_b[l],
                                  w_out[l], moe_router[l], n_ctx_rows, dec_seq)

        nb = -(-(t * TOP_K) // MOE_BM) + N_EXPERTS
        pos, gates, blk_e_row, info = _router(logits, moe_router_bias[l], nb)
        blk_e, first, slot, nxt, n_used = _dispatch_plan(blk_e_row, info, nb)
        xs = _dispatch(h2, pos, info, n_used, nb)
        eo = _moe_grouped(xs, blk_e, first, slot, nxt, n_used, moe_w_gate, moe_w_up, moe_w_down, l)
        gates_t = gates.T

        sh = (sh_w_gate[l], sh_w_up[l], sh_w_down[l])
        if l < DEPTH - 1:
            x = _shared(x, h2, gates_t, pos, eo, mod, *sh, final_norm, n_ctx_rows, dec_seq, final=False)
        else:
            y_c = _shared(x, h2, gates_t, pos, eo, mod, *sh, final_norm, n_ctx_rows, dec_seq, final=True,
                          row0=0, rows=n_ctx_rows)
            y_d = _shared(x, h2, gates_t, pos, eo, mod, *sh, final_norm, n_ctx_rows, dec_seq, final=True,
                          row0=n_ctx_rows, rows=n_dec * dec_seq)

    return (y_c.reshape(n_ctx, seq, d), y_d.reshape(n_dec, dec_seq, d),
            ret_states, jnp.stack(s5r_list, axis=1), jnp.stack(s5i_list, axis=1))
```

```python
import functools
import math

import jax
import jax.numpy as jnp
from jax import lax
from jax.experimental import pallas as pl
from jax.experimental.pallas import tpu as pltpu

F32 = jnp.float32
BF16 = jnp.bfloat16

D_MODEL = 2048
DEPTH = 2
GRID_W = 64
RET_HEADS = 8
RET_DK = 128
RET_DV = 128
RET_WIDTH = RET_HEADS * RET_DV
RET_CHUNK = 128
ROPE_BASE = 10000.0
S5_WIDTH = 512
S5_GROUP = 16
S5_GROUPS = S5_WIDTH // S5_GROUP
S5_STATE = 64
S5_Q = 16
HY_WIDTH = 512
HY_BANDS = 16
IN_WIDTH = 4 * RET_WIDTH + S5_WIDTH + 3 * HY_WIDTH
U_COL = 4 * RET_WIDTH
HY_COL = U_COL + S5_WIDTH
N_EXPERTS = 64
TOP_K = 6
D_EXPERT = 512
D_SHARED = 512
ROUTED_SCALE = 2.5
EPS = 1e-6

LANES = 128
SUBLANES = 8
VMEM_LIMIT = 56 * 1024 * 1024

ROW_TILE = 512
MOE_BM = 256


def _cparams(sem):
    return pltpu.CompilerParams(dimension_semantics=sem, vmem_limit_bytes=VMEM_LIMIT)


def _dot(a, b):
    return jnp.dot(a, b, preferred_element_type=F32)


def _rms(x, g):
    var = jnp.mean(x * x, axis=-1, keepdims=True)
    return x * lax.rsqrt(var + EPS) * g


def _pack_halves(xb):
    n = xb.shape[1] // 2
    lo = lax.bitcast_convert_type(xb[:, :n].astype(F32), jnp.uint32) >> 16
    hi = lax.bitcast_convert_type(xb[:, n:].astype(F32), jnp.uint32)
    return lo | hi


def _unpack_halves(w):
    lo = lax.bitcast_convert_type(w << 16, F32)
    hi = lax.bitcast_convert_type(w & jnp.uint32(0xFFFF0000), F32)
    return lo, hi


TOKEN_ROWS = D_MODEL // 2 // LANES


def _store_token_tiles(ref, w):
    m = w.shape[0]
    for s in range(TOKEN_ROWS):
        ref[pl.ds(s, m, stride=TOKEN_ROWS), :] = w[:, s * LANES:(s + 1) * LANES]


def _load_token_tiles(ref, m):
    return jnp.concatenate([ref[pl.ds(s, m, stride=TOKEN_ROWS), :] for s in range(TOKEN_ROWS)], axis=1)


def _ada_kernel(c_ref, w_ref, b_ref, o_ref):
    c = c_ref[...]
    s = (c * jax.nn.sigmoid(c)).astype(BF16)
    o_ref[0] = _dot(s, w_ref[0].astype(BF16)) + b_ref[0]


def _ada(cond8, w_ada, b_ada):
    tn = 512
    n = w_ada.shape[-1]
    return pl.pallas_call(
        _ada_kernel,
        grid=(DEPTH, n // tn),
        in_specs=[
            pl.BlockSpec((SUBLANES, D_MODEL), lambda l, j: (0, 0)),
            pl.BlockSpec((1, D_MODEL, tn), lambda l, j: (l, 0, j)),
            pl.BlockSpec((1, 1, tn), lambda l, j: (l, 0, j)),
        ],
        out_specs=pl.BlockSpec((1, SUBLANES, tn), lambda l, j: (l, 0, j)),
        out_shape=jax.ShapeDtypeStruct((DEPTH, SUBLANES, n), F32),
        compiler_params=_cparams(("parallel", "parallel")),
        name="ada",
    )(cond8, w_ada, b_ada.reshape(DEPTH, 1, n))


def _group_of(i, tile, n_ctx_rows, dec_seq):
    ctx_tiles = n_ctx_rows // tile
    per = dec_seq // tile
    return jnp.where(i < ctx_tiles, 0, 1 + (i - ctx_tiles) // per)


def _in_kernel(x_ref, mod_ref, g_ref, w_ref, o_ref, h_scr):
    @pl.when(pl.program_id(1) == 0)
    def _():
        m = mod_ref[0]
        h = _rms(x_ref[...], g_ref[...]) * (1.0 + m[1:2]) + m[0:1]
        h_scr[...] = h.astype(BF16)

    o_ref[...] = _dot(h_scr[...], w_ref[...])


def _in_proj(x, mod, g, w_bf, layer, n_ctx_rows, dec_seq):
    t = x.shape[0]
    tm, tn = 1024, 1024
    grp = functools.partial(_group_of, tile=tm, n_ctx_rows=n_ctx_rows, dec_seq=dec_seq)
    return pl.pallas_call(
        _in_kernel,
        grid=(t // tm, IN_WIDTH // tn),
        in_specs=[
            pl.BlockSpec((tm, D_MODEL), lambda i, j: (i, 0)),
            pl.BlockSpec((1, 6, D_MODEL), lambda i, j: (grp(i), 0, 0)),
            pl.BlockSpec((1, D_MODEL), lambda i, j: (0, 0)),
            pl.BlockSpec((None, D_MODEL, tn), lambda i, j: (layer, 0, j)),
        ],
        out_specs=pl.BlockSpec((tm, tn), lambda i, j: (i, j)),
        out_shape=jax.ShapeDtypeStruct((t, IN_WIDTH), F32),
        scratch_shapes=[pltpu.VMEM((tm, D_MODEL), BF16)],
        compiler_params=_cparams(("parallel", "arbitrary")),
        name="in_proj",
    )(x, mod, g.reshape(1, D_MODEL), w_bf)


def _ret_kernel(lg_ref, q_ref, k_ref, v_ref, gt_ref, cos_ref, sin_ref, s0_ref,
                o_ref, sfin_ref, acc_scr, q_scr, k_scr, *, seq_len, hb, rope):
    c = RET_CHUNK
    n_chunks = seq_len // c
    ii = lax.broadcasted_iota(jnp.int32, (c, c), 0)
    jj = lax.broadcasted_iota(jnp.int32, (c, c), 1)
    rel = (ii - jj).astype(F32)
    ci = lax.broadcasted_iota(jnp.int32, (c, 1), 0).astype(F32)
    one = jnp.ones((1, 1), F32)
    tdot = functools.partial(lax.dot_general, preferred_element_type=F32)

    def make_head(hh):
        head = pl.program_id(1) * hb + hh
        lgf = lg_ref[0, head]
        lgb = lg_ref[1, head]
        dmask = (jnp.where(rel >= 0, jnp.exp(lgf * jnp.maximum(rel, 0.0)), 0.0)
                 + jnp.where(rel <= 0, jnp.exp(lgb * jnp.maximum(-rel, 0.0)), 0.0))
        qd_f = jnp.exp(lgf * (ci + 1.0))
        kd_f = jnp.exp(lgf * (c - 1.0 - ci))
        cd_f = jnp.exp(lgf * c * one)
        qd_b = jnp.exp(lgb * (c - ci))
        kd_b = jnp.exp(lgb * ci)
        cd_b = jnp.exp(lgb * c * one)
        lanes = slice(hh * LANES, (hh + 1) * LANES)

        def rows_of(n):
            if isinstance(n, int):
                return slice(n * c, (n + 1) * c)
            return pl.ds(pl.multiple_of(n * c, c), c)

        def fwd_chunk(n, s_f):
            rows = rows_of(n)
            q = q_ref[rows, lanes]
            k = k_ref[rows, lanes] * (RET_DK ** -0.5)
            if rope:
                cs = cos_ref[rows, :]
                sn = sin_ref[rows, :]
                q = q * cs + pltpu.roll(q, RET_DK // 2, 1) * sn
                k = k * cs + pltpu.roll(k, RET_DK // 2, 1) * sn
            qb = q.astype(BF16)
            vb = v_ref[rows, lanes].astype(BF16)
            q_scr[rows, lanes] = qb
            k_scr[rows, lanes] = k
            scores = tdot(qb, k.astype(BF16), (((1,), (1,)), ((), ()))) * dmask
            inner = _dot(scores.astype(BF16), vb)
            cross = _dot(qb, s_f.astype(BF16)) * qd_f
            acc_scr[rows, lanes] = inner + cross
            upd = tdot((k * kd_f).astype(BF16), vb, (((0,), (0,)), ((), ())))
            return s_f * cd_f + upd

        def bwd_chunk(m, s_b):
            n = n_chunks - 1 - m
            rows = rows_of(n)
            qb = q_scr[rows, lanes]
            k = k_scr[rows, lanes]
            vb = v_ref[rows, lanes].astype(BF16)
            o = acc_scr[rows, lanes] + _dot(qb, s_b.astype(BF16)) * qd_b
            mu = jnp.mean(o, axis=-1, keepdims=True)
            oc = o - mu
            var = jnp.mean(oc * oc, axis=-1, keepdims=True)
            o = oc * lax.rsqrt(var + EPS)
            g = gt_ref[rows, lanes]
            o_ref[rows, lanes] = (g * jax.nn.sigmoid(g) * o).astype(o_ref.dtype)
            upd = tdot((k * kd_b).astype(BF16), vb, (((0,), (0,)), ((), ())))
            return s_b * cd_b + upd

        return fwd_chunk, bwd_chunk

    if n_chunks <= 4:
        for hh in range(hb):
            fwd_chunk, bwd_chunk = make_head(hh)
            s_f = s0_ref[0, 0, hh]
            s_b = s0_ref[0, 1, hh]
            for n in range(n_chunks):
                s_f = fwd_chunk(n, s_f)
            for m in range(n_chunks):
                s_b = bwd_chunk(m, s_b)
            sfin_ref[0, 0, hh] = s_f
            sfin_ref[0, 1, hh] = s_b
    else:
        fns = [make_head(hh) for hh in range(hb)]
        s_f = lax.fori_loop(0, n_chunks, lambda n, ss: tuple(f[0](n, s) for f, s in zip(fns, ss)),
                            tuple(s0_ref[0, 0, hh] for hh in range(hb)))
        s_b = lax.fori_loop(0, n_chunks, lambda m, ss: tuple(f[1](m, s) for f, s in zip(fns, ss)),
                            tuple(s0_ref[0, 1, hh] for hh in range(hb)))
        for hh in range(hb):
            sfin_ref[0, 0, hh] = s_f[hh]
            sfin_ref[0, 1, hh] = s_b[hh]


def _into(kernel_fn, n_in, dst):
    dsts = [d for d in (dst if isinstance(dst, (list, tuple)) else [dst])]
    outs = [k for k, d in enumerate(dsts) if d is not None]
    if not outs:
        return kernel_fn, [], [], {}

    def body(*refs):
        return kernel_fn(*refs[:n_in], *refs[n_in + len(outs):])

    return (body, [pl.BlockSpec(memory_space=pl.ANY)] * len(outs), [dsts[k] for k in outs],
            {n_in + pos: k for pos, k in enumerate(outs)})


def _retention(proj, log_gamma, s0, cos2, sin2, *, row0, n_seq, seq_len, hb, rope, dst=None,
               states=None, layer=0):
    blk0 = row0 // seq_len
    body, dst_spec, dst_arg, alias = _into(
        functools.partial(_ret_kernel, seq_len=seq_len, hb=hb, rope=rope), 8, [dst, states])
    if states is None:
        st_spec = pl.BlockSpec((1, 2, hb, RET_DK, RET_DV), lambda b, h, lg: (b, 0, h, 0, 0))
        st_shape = jax.ShapeDtypeStruct((n_seq, 2, RET_HEADS, RET_DK, RET_DV), F32)
    else:
        st_spec = pl.BlockSpec((1, None, 2, hb, RET_DK, RET_DV), lambda b, h, lg: (b, layer, 0, h, 0, 0))
        st_shape = jax.ShapeDtypeStruct(states.shape, F32)
    w = hb * LANES
    hblocks = RET_HEADS // hb
    col = lambda part: (lambda b, h, lg: (blk0 + b, part * hblocks + h))
    grid_spec = pltpu.PrefetchScalarGridSpec(
        num_scalar_prefetch=1,
        grid=(n_seq, hblocks),
        in_specs=[
            pl.BlockSpec((seq_len, w), col(0)),
            pl.BlockSpec((seq_len, w), col(1)),
            pl.BlockSpec((seq_len, w), col(2)),
            pl.BlockSpec((seq_len, w), col(3)),
            pl.BlockSpec((seq_len, LANES), lambda b, h, lg: (0, 0)),
            pl.BlockSpec((seq_len, LANES), lambda b, h, lg: (0, 0)),
            pl.BlockSpec((1, 2, hb, RET_DK, RET_DV), lambda b, h, lg: (b, 0, h, 0, 0)),
        ] + dst_spec,
        out_specs=[
            pl.BlockSpec((seq_len, w), lambda b, h, lg: (blk0 + b, h)),
            st_spec,
        ],
        scratch_shapes=[
            pltpu.VMEM((seq_len, w), F32),
            pltpu.VMEM((seq_len, w), BF16),
            pltpu.VMEM((seq_len, w), F32),
        ],
    )
    return pl.pallas_call(
        body,
        grid_spec=grid_spec,
        out_shape=[jax.ShapeDtypeStruct((proj.shape[0], RET_WIDTH), BF16), st_shape],
        input_output_aliases=alias,
        compiler_params=_cparams(("parallel", "arbitrary")),
        name="retention",
    )(log_gamma, proj, proj, proj, proj, cos2, sin2, s0, *dst_arg)


def _rope_tables(seq_len):
    rows_n = seq_len // GRID_W
    rows = jnp.repeat(jnp.arange(rows_n, dtype=F32), GRID_W)
    cols = jnp.tile(jnp.arange(GRID_W, dtype=F32), rows_n)
    nf = RET_DK // 4
    inv = ROPE_BASE ** (-jnp.arange(nf, dtype=F32) / nf)
    ang = jnp.concatenate([rows[:, None] * inv, cols[:, None] * inv], axis=-1)
    cs, sn = jnp.cos(ang), jnp.sin(ang)
    return jnp.concatenate([cs, cs], axis=-1), jnp.concatenate([-sn, sn], axis=-1)


S5_GB = LANES // S5_GROUP
S5_W = S5_Q * LANES
S5_SPLIT = 4
S5_SW = S5_GB * S5_STATE
S5_SB = S5_SW // LANES


def _s5_kernel(u_ref, bq_ref, k_ref, cq_ref, ar_ref, ai_ref, d_ref, h0_ref,
               y_ref, hf_ref, ub_scr, sm_scr, hp_scr, hpb_scr, t_scr, *, n_seq, n_chunks):
    s = pl.program_id(1)
    m = n_seq * n_chunks
    q = S5_Q
    nblk = S5_W // LANES
    sb = S5_SB

    per = q // S5_SPLIT
    kw = S5_W // S5_SPLIT

    @pl.when(s == 0)
    def _():
        for j in range(q):
            ub_scr[j // per, :, (j % per) * LANES:(j % per + 1) * LANES] = (
                u_ref[pl.ds(j, m, stride=q), :].astype(BF16))
        for ib in range(S5_SPLIT):
            for j in range(q):
                c0 = (q - 1 - j) * LANES + ib * kw
                t_scr[ib, j * LANES:(j + 1) * LANES, :] = k_ref[0, :, c0:c0 + kw]

    @pl.when(s < S5_SPLIT)
    def _():
        part = _dot(ub_scr[jnp.minimum(s, S5_SPLIT - 1)], bq_ref[0])

        @pl.when(s == 0)
        def _():
            for cb in range(nblk):
                sm_scr[cb] = part[:, cb * LANES:(cb + 1) * LANES]

        @pl.when(s > 0)
        def _():
            for cb in range(nblk):
                sm_scr[cb] += part[:, cb * LANES:(cb + 1) * LANES]

    @pl.when(s == S5_SPLIT - 1)
    def _():
        ar = ar_ref[0]
        ai = ai_ref[0]
        h0 = h0_ref[0]
        blk = lambda a, cb: a[:, cb * LANES:(cb + 1) * LANES]

        def body(n, carry):
            rows_f = pl.ds(n, n_seq, stride=n_chunks)
            rows_b = pl.ds(n_chunks - 1 - n, n_seq, stride=n_chunks)
            new = list(carry)
            for d, rows in ((0, rows_f), (1, rows_b)):
                for c in range(sb):
                    re_i = d * sb + c
                    im_i = (2 + d) * sb + c
                    hr, hi = carry[re_i], carry[im_i]
                    hp_scr[re_i, rows, :] = hr
                    hp_scr[im_i, rows, :] = hi
                    a_r, a_i = blk(ar, re_i), blk(ai, re_i)
                    new[re_i] = a_r * hr - a_i * hi + sm_scr[re_i, rows, :]
                    new[im_i] = a_r * hi + a_i * hr + sm_scr[im_i, rows, :]
            return tuple(new)

        fin = lax.fori_loop(0, n_chunks, body, tuple(blk(h0, cb) for cb in range(nblk)))
        hf_ref[0] = jnp.concatenate(fin, axis=1)
        for cb in range(nblk):
            hpb_scr[:, cb * LANES:(cb + 1) * LANES] = hp_scr[cb].astype(BF16)

    @pl.when(s >= S5_SPLIT)
    def _():
        ub = jnp.concatenate([ub_scr[k] for k in range(S5_SPLIT)], axis=1)
        y = _dot(ub, t_scr[jnp.maximum(s - S5_SPLIT, 0)]) + _dot(hpb_scr[...], cq_ref[0])
        dd = d_ref[0]
        for ii in range(per):
            rows = pl.ds((s - S5_SPLIT) * per + ii, m, stride=q)
            yi = y[:, ii * LANES:(ii + 1) * LANES] + dd * u_ref[rows, :]
            y_ref[rows, :] = jax.nn.gelu(yi)


def _s5_expand_kernel(mc_ref, o_ref, *, xsize, ysize):
    xs, ys, gs = xsize.bit_length() - 1, ysize.bit_length() - 1, S5_GB.bit_length() - 1
    assert xsize == 1 << xs and ysize == 1 << ys and S5_GB == 1 << gs
    cw = o_ref.shape[2]
    nc = mc_ref.shape[2]
    col0 = pl.program_id(1) * cw
    r = lax.broadcasted_iota(jnp.int32, (nc, cw), 0)
    col = lax.broadcasted_iota(jnp.int32, (nc, cw), 1) + col0
    spread = jnp.logical_and(r >> ys == col >> (ys + gs), (r & (ysize - 1)) == (col & (ysize - 1)))
    big = _dot(mc_ref[0], jnp.where(spread, 1.0, 0.0).astype(BF16))
    row = lax.broadcasted_iota(jnp.int32, big.shape, 0)
    colb = lax.broadcasted_iota(jnp.int32, big.shape, 1) + col0
    same = ((row >> xs) & (S5_GB - 1)) == ((colb >> ys) & (S5_GB - 1))
    o_ref[0] = jnp.where(same, big, 0.0).astype(BF16)


def _s5_expand(mc, *, xsize, ysize):
    nb, rows, nc = mc.shape
    cols = nc * S5_GB
    cw = 512
    return pl.pallas_call(
        functools.partial(_s5_expand_kernel, xsize=xsize, ysize=ysize),
        grid=(nb, cols // cw),
        in_specs=[pl.BlockSpec((1, rows, nc), lambda b, j: (b, 0, 0))],
        out_specs=pl.BlockSpec((1, rows, cw), lambda b, j: (b, 0, j)),
        out_shape=jax.ShapeDtypeStruct((nb, rows, cols), BF16),
        compiler_params=_cparams(("parallel", "parallel")),
        name="s5_expand",
    )(mc)


def _s5_mats(lam_re, lam_im, log_dt, b_re, b_im, c_re, c_im, d):
    flat = lambda a: a.reshape((-1,) + a.shape[2:])
    kc, bq, cq, ar, ai, dd = map(flat, jax.vmap(_s5_compact)(lam_re, lam_im, log_dt, b_re, b_im, c_re, c_im, d))
    ch, p = S5_GROUP, S5_STATE
    return (_s5_expand(kc, xsize=ch, ysize=ch),
            _s5_expand(bq, xsize=ch, ysize=p),
            _s5_expand(cq, xsize=p, ysize=ch),
            ar, ai, dd)


def _s5_compact(lam_re, lam_im, log_dt, b_re, b_im, c_re, c_im, d):
    q, g, p, ch = S5_Q, S5_GROUPS, S5_STATE, S5_GROUP
    lam = lax.complex(jnp.minimum(lam_re.astype(F32), -1e-4), lam_im.astype(F32))
    ldt = lam * jnp.exp(log_dt.astype(F32))[..., None]
    lam_bar = jnp.exp(ldt)
    b_bar = ((lam_bar - 1.0) / lam)[..., None] * lax.complex(b_re.astype(F32), b_im.astype(F32))
    cc = lax.complex(c_re.astype(F32), c_im.astype(F32))
    pw = jnp.exp(ldt[..., None] * jnp.arange(q + 1, dtype=F32))
    hi = lax.Precision.HIGHEST
    lag = jnp.arange(2 * q, dtype=F32) - (q - 1)
    wf = jnp.where(lag >= 0, jnp.exp(ldt[0][..., None] * jnp.maximum(lag, 0.0)), 0.0)
    wb = jnp.where(lag <= 0, jnp.exp(ldt[1][..., None] * jnp.maximum(-lag, 0.0)), 0.0)
    kc = jnp.real(jnp.einsum('gcp,gpd,gpe->gedc', cc[0], wf, b_bar[0], precision=hi)
                  + jnp.einsum('gcp,gpd,gpe->gedc', cc[1], wb, b_bar[1], precision=hi))

    pw_dn = jnp.exp(ldt[..., None] * (q - jnp.arange(q + 1, dtype=F32)))
    bf = pw_dn[0][..., 1:][:, :, :, None] * b_bar[0][:, :, None, :]
    bb = pw[1][..., :q][:, :, :, None] * b_bar[1][:, :, None, :]
    to_rows = lambda m: m.transpose(0, 2, 3, 1).reshape(g, q * ch, p)
    bq = jnp.concatenate([to_rows(jnp.real(bf)), to_rows(jnp.real(bb)),
                          to_rows(jnp.imag(bf)), to_rows(jnp.imag(bb))], axis=-1)

    cf = cc[0].transpose(0, 2, 1)[:, :, None, :] * pw[0][..., 1:][:, :, :, None]
    cb = cc[1].transpose(0, 2, 1)[:, :, None, :] * pw_dn[1][..., :q][:, :, :, None]
    to_cols = lambda m: m.reshape(g, p, q * ch)
    cq = jnp.concatenate([to_cols(jnp.real(cf)), to_cols(jnp.real(cb)),
                          to_cols(-jnp.imag(cf)), to_cols(-jnp.imag(cb))], axis=1)

    gb, nb = S5_GB, g // S5_GB
    rows_of = lambda a, outer, inner: (a.reshape(nb, gb, outer, inner, a.shape[-1]).transpose(0, 2, 1, 3, 4)
                                       .reshape(nb, outer * gb * inner, a.shape[-1]).astype(BF16))
    kc = kc.reshape(nb, gb * ch, 2 * q * ch).astype(BF16)

    lq = pw[..., q].reshape(2, nb, 1, S5_SW)
    ar = jnp.concatenate([jnp.real(lq[0]), jnp.real(lq[1])], axis=-1)
    ai = jnp.concatenate([jnp.imag(lq[0]), jnp.imag(lq[1])], axis=-1)
    dd = d.astype(F32).reshape(nb, 1, LANES)
    return kc, rows_of(bq, q, ch), rows_of(cq, 4, p), ar, ai, dd


def _s5(proj, mats, h0_re, h0_im, *, row0, n_seq, seq_len, layer=0, dst=None):
    q, p = S5_Q, S5_STATE
    n_chunks = seq_len // q
    m = n_seq * n_chunks
    rows = n_seq * seq_len
    nb = S5_GROUPS // S5_GB
    nblk = S5_W // LANES
    kw = S5_W // S5_SPLIT
    tm, bq, cq, ar, ai, dd = mats
    part = lambda a: a.astype(F32).reshape(n_seq, nb, S5_SW)
    h0 = jnp.concatenate([part(h0_re[:, 0]), part(h0_re[:, 1]), part(h0_im[:, 0]), part(h0_im[:, 1])],
                         axis=-1).transpose(1, 0, 2)
    per_b = lambda shape: pl.BlockSpec((1,) + shape, lambda b, s: (b, 0, 0))
    b0 = layer * nb
    per_l = lambda shape: pl.BlockSpec((1,) + shape, lambda b, s: (b0 + b, 0, 0))
    body, dst_spec, dst_arg, alias = _into(functools.partial(_s5_kernel, n_seq=n_seq, n_chunks=n_chunks), 8, dst)
    y, hf = pl.pallas_call(
        body,
        grid=(nb, 2 * S5_SPLIT),
        in_specs=[
            pl.BlockSpec((rows, LANES), lambda b, s: (row0 // rows, U_COL // LANES + b)),
            pl.BlockSpec((1, kw, 4 * S5_SW), lambda b, s: (b0 + b, jnp.minimum(s, S5_SPLIT - 1), 0)),
            per_l((LANES, 2 * q * LANES)),
            pl.BlockSpec((1, 4 * S5_SW, kw), lambda b, s: (b0 + b, 0, jnp.maximum(s - S5_SPLIT, 0))),
            per_l((1, 2 * S5_SW)), per_l((1, 2 * S5_SW)), per_l((1, LANES)), per_b((n_seq, 4 * S5_SW)),
        ] + dst_spec,
        out_specs=[pl.BlockSpec((rows, LANES), lambda b, s: (row0 // rows, b)), per_b((n_seq, 4 * S5_SW))],
        out_shape=[jax.ShapeDtypeStruct((proj.shape[0], S5_WIDTH), F32),
                   jax.ShapeDtypeStruct((nb, n_seq, 4 * S5_SW), F32)],
        scratch_shapes=[pltpu.VMEM((S5_SPLIT, m, kw), BF16), pltpu.VMEM((nblk, m, LANES), F32),
                        pltpu.VMEM((nblk, m, LANES), F32), pltpu.VMEM((m, 4 * S5_SW), BF16),
                        pltpu.VMEM((S5_SPLIT, S5_W, kw), BF16)],
        input_output_aliases=alias,
        compiler_params=_cparams(("parallel", "arbitrary")),
        name="s5",
    )(proj, bq, tm, cq, ar, ai, dd, h0, *dst_arg)
    hf = hf.reshape(nb, n_seq, 4, S5_GB, p).transpose(1, 2, 0, 3, 4).reshape(n_seq, 4, S5_GROUPS, p)
    return y, hf[:, 0:2], hf[:, 2:4]


def _conv3(x, w, b):
    n = x.shape[0]
    row = lax.broadcasted_iota(jnp.int32, x.shape, 0)
    prev = jnp.where(row == 0, 0.0, pltpu.roll(x, 1, 0))
    nxt = jnp.where(row == n - 1, 0.0, pltpu.roll(x, n - 1, 0))
    return prev * w[0:1] + x * w[1:2] + nxt * w[2:3] + b


def _hy_fwd_kernel(x0_ref, x1_ref, v_ref, w0_ref, w1_ref, wv_ref, b0_ref, b1_ref, bv_ref,
                   fc_ref, fs_ref, m1_ref, m2_ref, m3_ref, p_ref, z_ref, x0c_ref, zb_scr):
    @pl.when(pl.program_id(2) == 0)
    def _():
        z = _conv3(x1_ref[...], w1_ref[...], b1_ref[...]) * _conv3(v_ref[...], wv_ref[...], bv_ref[...])
        z_ref[...] = z
        zb_scr[...] = z.astype(BF16)
        x0c_ref[...] = _conv3(x0_ref[...], w0_ref[...], b0_ref[...])

    zb = zb_scr[...]
    a = _dot(fc_ref[...], zb)
    b = _dot(fs_ref[...], zb)
    m2 = m2_ref[...]
    p_ref[0, 0] = (m1_ref[...] * a + m2 * b).astype(BF16)
    p_ref[0, 1] = (m3_ref[...] * b - m2 * a).astype(BF16)


def _hy_inv_kernel(p_ref, gc_ref, gs_ref, z_ref, x0c_ref, bias_ref, o_ref):
    conv = _dot(gc_ref[...], p_ref[0, 0]) + _dot(gs_ref[...], p_ref[0, 1])
    o_ref[...] = (x0c_ref[...] * (conv + bias_ref[...] * z_ref[...])).astype(o_ref.dtype)


def _dft_mats(seq_len):
    n, w = seq_len, 64
    k = jnp.arange(n, dtype=jnp.int32)
    ang = lambda j: ((k[:, None] * j[None, :]) % (2 * n)).astype(F32) * (math.pi / n)
    ang_a = ang(jnp.arange(n // w, dtype=jnp.int32) * w)
    ang_b = ang(jnp.arange(w, dtype=jnp.int32))
    ca, sa = jnp.cos(ang_a)[:, :, None], jnp.sin(ang_a)[:, :, None]
    cb, sb = jnp.cos(ang_b)[:, None, :], jnp.sin(ang_b)[:, None, :]
    cm = (ca * cb - sa * sb).reshape(n, n)
    sm = -(sa * cb + ca * sb).reshape(n, n)
    nyq = jnp.where(k % 2 == 0, 1.0, -1.0).astype(F32)
    return cm.astype(BF16), sm.at[0, :].set(nyq).astype(BF16), sm.at[:, 0].set(nyq).astype(BF16)


def _hy_filter_mults(seq_len, f1_w, f1_b, f2_w, f2_b, f3_w, f3_b, freq, decay):
    n = seq_len
    t = (jnp.arange(n, dtype=F32) / n)[:, None]
    bands = jnp.arange(1, HY_BANDS + 1, dtype=F32)[None, :]
    z = jnp.concatenate([t, jnp.cos(2.0 * math.pi * t * bands), jnp.sin(2.0 * math.pi * t * bands)], axis=-1)
    hi = lax.Precision.HIGHEST
    fr = freq.astype(F32)
    h = jnp.sin(fr * (jnp.dot(z, f1_w.astype(F32), precision=hi) + f1_b.astype(F32)))
    h = jnp.sin(fr * (jnp.dot(h, f2_w.astype(F32), precision=hi) + f2_b.astype(F32)))
    h = jnp.dot(h, f3_w.astype(F32), precision=hi) + f3_b.astype(F32)
    h = h * jnp.exp(-t * jnp.abs(decay.astype(F32)))
    h = h.reshape(n, 2, HY_WIDTH)
    h = h / jnp.sum(jnp.abs(h), axis=(0, 1), keepdims=True)
    hc = jnp.concatenate([h[:, 0], h[::-1, 1]], axis=0)
    spec = jnp.fft.rfft(hc, axis=0)
    hr, him = jnp.real(spec), jnp.imag(spec)
    wk = jnp.where(jnp.arange(n) == 0, 1.0, 2.0)[:, None] / (2.0 * n)
    first = (jnp.arange(n) == 0)[:, None]
    m1 = hr[:n] * wk
    m2 = jnp.where(first, 0.0, -him[:n]) * wk
    m3 = jnp.where(first, hr[n:n + 1], hr[:n]) * wk
    return m1.astype(F32), m2.astype(F32), m3.astype(F32)


def _hyena(proj, conv_w, conv_b, bias, dft, mults, *, row0, n_seq, seq_len, cb, tk, layer=0, dst=None):
    blk0 = row0 // seq_len
    nc = HY_WIDTH // cb
    nk = seq_len // tk
    c0 = HY_COL // cb
    cm, sm, smt = dft
    m1, m2, m3 = mults
    xcol = lambda part: pl.BlockSpec((seq_len, cb), lambda b, c, k: (blk0 + b, c0 + part * nc + c))
    wcol = lambda part: pl.BlockSpec((3, cb), lambda b, c, k: (0, part * nc + c))
    bcol = lambda part: pl.BlockSpec((1, cb), lambda b, c, k: (0, part * nc + c))
    frow = pl.BlockSpec((tk, seq_len), lambda b, c, k: (k, 0))
    mblk = pl.BlockSpec((None, tk, cb), lambda b, c, k: (layer, k, c))
    cb2 = conv_b.reshape(1, 3 * HY_WIDTH)
    pspec, z, x0c = pl.pallas_call(
        _hy_fwd_kernel,
        grid=(n_seq, nc, nk),
        in_specs=[xcol(0), xcol(1), xcol(2), wcol(0), wcol(1), wcol(2), bcol(0), bcol(1), bcol(2),
                  frow, frow, mblk, mblk, mblk],
        out_specs=[
            pl.BlockSpec((1, 2, tk, cb), lambda b, c, k: (b, 0, k, c)),
            pl.BlockSpec((seq_len, cb), lambda b, c, k: (b, c)),
            pl.BlockSpec((seq_len, cb), lambda b, c, k: (b, c)),
        ],
        out_shape=[
            jax.ShapeDtypeStruct((n_seq, 2, seq_len, HY_WIDTH), BF16),
            jax.ShapeDtypeStruct((n_seq * seq_len, HY_WIDTH), F32),
            jax.ShapeDtypeStruct((n_seq * seq_len, HY_WIDTH), F32),
        ],
        scratch_shapes=[pltpu.VMEM((seq_len, cb), BF16)],
        compiler_params=_cparams(("parallel", "parallel", "arbitrary")),
        name="hyena_fwd",
    )(proj, proj, proj, conv_w, conv_w, conv_w, cb2, cb2, cb2, cm, sm, m1, m2, m3)
    grow = pl.BlockSpec((tk, seq_len), lambda b, c, k: (k, 0))
    tile = pl.BlockSpec((tk, cb), lambda b, c, k: (b * nk + k, c))
    body, dst_spec, dst_arg, alias = _into(_hy_inv_kernel, 6, dst)
    return pl.pallas_call(
        body,
        grid=(n_seq, nc, nk),
        in_specs=[pl.BlockSpec((1, 2, seq_len, cb), lambda b, c, k: (b, 0, 0, c)),
                  grow, grow, tile, tile, pl.BlockSpec((1, cb), lambda b, c, k: (0, c))] + dst_spec,
        out_specs=pl.BlockSpec((tk, cb), lambda b, c, k: (row0 // tk + b * nk + k, c)),
        out_shape=jax.ShapeDtypeStruct((proj.shape[0], HY_WIDTH), BF16),
        input_output_aliases=alias,
        compiler_params=_cparams(("parallel", "parallel", "arbitrary")),
        name="hyena_inv",
    )(pspec, cm, smt, z, x0c, bias.reshape(1, HY_WIDTH), *dst_arg)


def _out_kernel(x_ref, ret_ref, s5_ref, hy_ref, mod_ref, g_ref, gw_ref, gb_ref,
                wr_ref, ws_ref, wh_ref, rt_ref, xo_ref, h_ref, lg_ref):
    m = mod_ref[0]
    y = s5_ref[...]
    s5o = y * jax.nn.sigmoid(_dot(y.astype(BF16), gw_ref[...]) + gb_ref[...])
    mix = (_dot(ret_ref[...], wr_ref[...]) + _dot(s5o.astype(BF16), ws_ref[...])
           + _dot(hy_ref[...], wh_ref[...]))
    x = x_ref[...] + m[2:3] * mix
    xo_ref[...] = x
    h = _rms(x, g_ref[...]) * (1.0 + m[4:5]) + m[3:4]
    hb = h.astype(BF16)
    _store_token_tiles(h_ref, _pack_halves(hb))
    lg_ref[...] = lax.dot_general(rt_ref[...], hb, (((1,), (1,)), ((), ())), preferred_element_type=F32)


def _out_proj(x, ret_o, s5_y, hy_o, mod, g, glu_w, glu_b, w_out, router, n_ctx_rows, dec_seq):
    t = x.shape[0]
    tm = ROW_TILE
    grp = functools.partial(_group_of, tile=tm, n_ctx_rows=n_ctx_rows, dec_seq=dec_seq)
    row = lambda w: pl.BlockSpec((tm, w), lambda i: (i, 0))
    full = lambda a, b: pl.BlockSpec((a, b), lambda i: (0, 0))
    wo = w_out.astype(BF16)
    return pl.pallas_call(
        _out_kernel,
        grid=(t // tm,),
        in_specs=[row(D_MODEL), row(RET_WIDTH), row(S5_WIDTH), row(HY_WIDTH),
                  pl.BlockSpec((1, 6, D_MODEL), lambda i: (grp(i), 0, 0)),
                  full(1, D_MODEL), full(S5_WIDTH, S5_WIDTH), full(1, S5_WIDTH),
                  full(RET_WIDTH, D_MODEL), full(S5_WIDTH, D_MODEL), full(HY_WIDTH, D_MODEL),
                  full(N_EXPERTS, D_MODEL)],
        out_specs=[row(D_MODEL), pl.BlockSpec((tm * TOKEN_ROWS, LANES), lambda i: (i, 0)),
                   pl.BlockSpec((N_EXPERTS, tm), lambda i: (0, i))],
        out_shape=[jax.ShapeDtypeStruct((t, D_MODEL), F32), jax.ShapeDtypeStruct((t * TOKEN_ROWS, LANES), jnp.uint32),
                   jax.ShapeDtypeStruct((N_EXPERTS, t), F32)],
        compiler_params=_cparams(("parallel",)),
        name="out_proj",
    )(x, ret_o, s5_y, hy_o, mod, g.reshape(1, D_MODEL), glu_w.astype(BF16), glu_b.reshape(1, S5_WIDTH),
      wo[:RET_WIDTH], wo[RET_WIDTH:RET_WIDTH + S5_WIDTH], wo[RET_WIDTH + S5_WIDTH:], router.T.astype(BF16))


def _moe_kernel(be_ref, first_ref, slot_ref, nxt_ref, nu_ref, xs_ref, wg_hbm, wu_hbm, wd_hbm, o_ref,
                wg_f, wu_f, wd_f, wg_b, wu_b, wd_b, sem, *, layer):
    i = pl.program_id(0)

    def copies(e, s):
        return (pltpu.make_async_copy(wg_hbm.at[layer, e], wg_f.at[s], sem.at[s, 0]),
                pltpu.make_async_copy(wu_hbm.at[layer, e], wu_f.at[s], sem.at[s, 1]),
                pltpu.make_async_copy(wd_hbm.at[layer, e], wd_f.at[s], sem.at[s, 2]))

    @pl.when(i == 0)
    def _():
        for cp in copies(be_ref[0], 0):
            cp.start()

    @pl.when(first_ref[i] == 1)
    def _():
        s = slot_ref[i]
        for cp in copies(be_ref[i], s):
            cp.wait()

        @pl.when(nxt_ref[i] >= 0)
        def _():
            for cp in copies(nxt_ref[i], 1 - s):
                cp.start()

        wg_b[...] = wg_f[s].astype(BF16)
        wu_b[...] = wu_f[s].astype(BF16)
        wd_b[...] = wd_f[s].astype(BF16)

    @pl.when(i < nu_ref[0])
    def _():
        half = D_MODEL // 2
        x_lo, x_hi = _unpack_halves(_load_token_tiles(xs_ref, MOE_BM))
        x_lo = x_lo.astype(BF16)
        x_hi = x_hi.astype(BF16)
        gate = _dot(x_lo, wg_b[0:half, :]) + _dot(x_hi, wg_b[half:, :])
        up = _dot(x_lo, wu_b[0:half, :]) + _dot(x_hi, wu_b[half:, :])
        hb = gate * jax.nn.sigmoid(gate) * up
        _store_token_tiles(o_ref, _pack_halves(_dot(hb.astype(BF16), wd_b[...]).astype(BF16)))

    @pl.when(i >= nu_ref[0])
    def _():
        o_ref[...] = jnp.zeros_like(o_ref)


def _moe_grouped(xs, blk_e, first, slot, nxt, n_used, w_gate, w_up, w_down, layer):
    pr = xs.shape[0] // TOKEN_ROWS
    bm = MOE_BM
    nb = pr // bm
    grid_spec = pltpu.PrefetchScalarGridSpec(
        num_scalar_prefetch=5,
        grid=(nb,),
        in_specs=[
            pl.BlockSpec((bm * TOKEN_ROWS, LANES), lambda i, *_: (i, 0)),
            pl.BlockSpec(memory_space=pl.ANY),
            pl.BlockSpec(memory_space=pl.ANY),
            pl.BlockSpec(memory_space=pl.ANY),
        ],
        out_specs=pl.BlockSpec((bm * TOKEN_ROWS, LANES), lambda i, *_: (i, 0)),
        scratch_shapes=[pltpu.VMEM((2, D_MODEL, D_EXPERT), F32), pltpu.VMEM((2, D_MODEL, D_EXPERT), F32),
                        pltpu.VMEM((2, D_EXPERT, D_MODEL), F32),
                        pltpu.VMEM((D_MODEL, D_EXPERT), BF16), pltpu.VMEM((D_MODEL, D_EXPERT), BF16),
                        pltpu.VMEM((D_EXPERT, D_MODEL), BF16),
                        pltpu.SemaphoreType.DMA((2, 3))],
    )
    return pl.pallas_call(
        functools.partial(_moe_kernel, layer=layer),
        grid_spec=grid_spec,
        out_shape=jax.ShapeDtypeStruct((pr * TOKEN_ROWS, LANES), jnp.uint32),
        compiler_params=_cparams(("arbitrary",)),
        name="moe_grouped",
    )(blk_e, first, slot, nxt, n_used, xs, w_gate, w_up, w_down)


DISPATCH_TILE = 512


def _dispatch_kernel(info_ref, nu_ref, pos_ref, h_ref, xs_out, zbuf, sem, zsem, *, nb):
    tm = pos_ref.shape[1]
    tr = TOKEN_ROWS
    bm = MOE_BM

    @pl.when(pl.program_id(0) == 0)
    def _():
        zbuf[...] = jnp.zeros_like(zbuf)

        def zero_block(first_row):
            rows = pl.ds(pl.multiple_of(first_row * tr, tr), bm * tr)
            return pltpu.make_async_copy(zbuf, xs_out.at[rows], zsem)

        for act in ("start", "wait"):
            def last_of_expert(e, carry, act=act):
                @pl.when(info_ref[e, 3] > info_ref[e, 1])
                def _():
                    getattr(zero_block(info_ref[e, 3] - bm), act)()
                return carry

            def tail_block(b, carry, act=act):
                getattr(zero_block(b * bm), act)()
                return carry

            lax.fori_loop(0, N_EXPERTS, last_of_expert, 0)
            lax.fori_loop(nu_ref[0], nb, tail_block, 0)

    def send(tok, carry):
        src = h_ref.at[pl.ds(pl.multiple_of(tok * tr, tr), tr)]
        for k in range(TOP_K):
            row = pl.multiple_of(pos_ref[k, tok] * tr, tr)
            pltpu.make_async_copy(src, xs_out.at[pl.ds(row, tr)], sem).start(priority=k % 2)
        return carry

    lax.fori_loop(0, tm, send, 0, unroll=4)
    n = tm * TOP_K * tr
    pltpu.make_async_copy(xs_out.at[pl.ds(0, n)], xs_out.at[pl.ds(0, n)], sem).wait()


def _dispatch(h2, pos, info, n_used, nb):
    t = pos.shape[1]
    tm = DISPATCH_TILE
    return pl.pallas_call(
        functools.partial(_dispatch_kernel, nb=nb),
        grid=(t // tm,),
        in_specs=[pl.BlockSpec(memory_space=pltpu.SMEM),
                  pl.BlockSpec(memory_space=pltpu.SMEM),
                  pl.BlockSpec((SUBLANES, tm), lambda i: (0, i), memory_space=pltpu.SMEM),
                  pl.BlockSpec((tm * TOKEN_ROWS, LANES), lambda i: (i, 0))],
        out_specs=pl.BlockSpec(memory_space=pl.ANY),
        out_shape=jax.ShapeDtypeStruct((nb * MOE_BM * TOKEN_ROWS, LANES), h2.dtype),
        scratch_shapes=[pltpu.VMEM((MOE_BM * TOKEN_ROWS, LANES), h2.dtype),
                        pltpu.SemaphoreType.DMA(()), pltpu.SemaphoreType.DMA(())],
        compiler_params=pltpu.CompilerParams(dimension_semantics=("arbitrary",)),
        name="dispatch",
    )(info, n_used, pos, h2)


ROUTE_TILE = 512


def _router_kernel(lg_ref, bias_ref, pos_ref, gate_ref, be_ref, info_ref, rank_scr, ek_scr, *, t, nbp):
    tl = ROUTE_TILE
    ne = N_EXPERTS
    bm = MOE_BM
    row = lax.broadcasted_iota(jnp.int32, (ne, tl), 0)
    tri = (lax.broadcasted_iota(jnp.int32, (tl, tl), 0) < lax.broadcasted_iota(jnp.int32, (tl, tl), 1)).astype(BF16)
    bias = bias_ref[...]

    def select(i, counts):
        cols = pl.ds(pl.multiple_of(i * tl, tl), tl)
        s = jax.nn.sigmoid(lg_ref[:, cols])
        sel = s + bias
        mask = jnp.zeros((ne, tl), F32)
        vals = []
        for k in range(TOP_K):
            best = jnp.max(sel, axis=0, keepdims=True)
            idx = jnp.min(jnp.where(sel == best, row, ne), axis=0, keepdims=True)
            hit = row == idx
            vals.append(jnp.sum(jnp.where(hit, s, 0.0), axis=0, keepdims=True))
            sel = jnp.where(hit, -jnp.inf, sel)
            mask = jnp.where(hit, 1.0, mask)
            ek_scr[k:k + 1, cols] = idx
        total = vals[0]
        for v in vals[1:]:
            total = total + v
        scale = ROUTED_SCALE / total
        for k in range(TOP_K):
            gate_ref[k:k + 1, cols] = vals[k] * scale
        gate_ref[TOP_K:SUBLANES, cols] = jnp.zeros((SUBLANES - TOP_K, tl), F32)
        rank_scr[:, cols] = _dot(mask.astype(BF16), tri) + counts
        return counts + jnp.sum(mask, axis=1, keepdims=True)

    counts = lax.fori_loop(0, t // tl, select, jnp.zeros((ne, 1), F32))
    counts = counts.astype(jnp.int32)
    shift = bm.bit_length() - 1
    assert bm == 1 << shift
    padded = ((counts + (bm - 1)) >> shift) << shift
    e0 = lax.broadcasted_iota(jnp.int32, (ne, ne), 0)
    e1 = lax.broadcasted_iota(jnp.int32, (ne, ne), 1)
    padded_row = jnp.sum(jnp.where(e0 == e1, padded, 0), axis=0, keepdims=True)
    counts_row = jnp.sum(jnp.where(e0 == e1, counts, 0), axis=0, keepdims=True)
    pstart = jnp.sum(jnp.where(e1 < e0, padded_row, 0), axis=1, keepdims=True)
    ustart = jnp.sum(jnp.where(e1 < e0, counts_row, 0), axis=1, keepdims=True)
    pend = pstart + padded
    lane = lax.broadcasted_iota(jnp.int32, (ne, LANES), 1)
    info_ref[...] = jnp.where(lane == 0, counts, jnp.where(lane == 1, pstart, jnp.where(lane == 2, ustart, pend)))
    blk = lax.broadcasted_iota(jnp.int32, (ne, nbp), 1) * bm
    owner = jnp.sum(jnp.where(pend <= blk, 1, 0), axis=0, keepdims=True)
    be_ref[...] = jnp.minimum(owner, ne - 1)
    pstart_f = pstart.astype(F32)

    def place(i, carry):
        cols = pl.ds(pl.multiple_of(i * tl, tl), tl)
        dest = rank_scr[:, cols] + pstart_f
        for k in range(TOP_K):
            hit = row == ek_scr[k:k + 1, cols]
            pos_ref[k:k + 1, cols] = jnp.sum(jnp.where(hit, dest, 0.0), axis=0, keepdims=True).astype(jnp.int32)
        pos_ref[TOP_K:SUBLANES, cols] = jnp.zeros((SUBLANES - TOP_K, tl), jnp.int32)
        return carry

    lax.fori_loop(0, t // tl, place, 0)


def _router(logits_t, router_bias, nb):
    t = logits_t.shape[1]
    nbp = -(-nb // LANES) * LANES
    return pl.pallas_call(
        functools.partial(_router_kernel, t=t, nbp=nbp),
        out_shape=[jax.ShapeDtypeStruct((SUBLANES, t), jnp.int32), jax.ShapeDtypeStruct((SUBLANES, t), F32),
                   jax.ShapeDtypeStruct((1, nbp), jnp.int32), jax.ShapeDtypeStruct((N_EXPERTS, LANES), jnp.int32)],
        scratch_shapes=[pltpu.VMEM((N_EXPERTS, t), F32), pltpu.VMEM((SUBLANES, t), jnp.int32)],
        compiler_params=pltpu.CompilerParams(vmem_limit_bytes=VMEM_LIMIT),
        name="router",
    )(logits_t, router_bias.astype(F32).reshape(N_EXPERTS, 1))


def _dispatch_plan(blk_e_row, info, nb):
    bm = MOE_BM
    pend = info[:, 3]
    blk_e = blk_e_row[0, :nb]
    n_used = pend[-1] // bm
    blk = jnp.arange(nb, dtype=jnp.int32)
    prev_e = jnp.concatenate([jnp.full((1,), -1, jnp.int32), blk_e[:-1]])
    first = jnp.logical_and(blk < n_used, blk_e != prev_e)
    slot = (jnp.cumsum(first.astype(jnp.int32)) - 1) % 2
    first_at = jnp.where(first, blk, nb)
    nxt_first = lax.cummin(jnp.concatenate([first_at[1:], jnp.full((1,), nb, jnp.int32)]), reverse=True)
    nxt = jnp.where(nxt_first < nb, blk_e[jnp.minimum(nxt_first, nb - 1)], -1)
    return (blk_e, first.astype(jnp.int32), slot.astype(jnp.int32), nxt.astype(jnp.int32),
            n_used.astype(jnp.int32).reshape(1))


def _shared_kernel(pos_ref, nxt_ref, x_ref, h_ref, gt_ref, mod_ref, sg_ref, su_ref, sd_ref, fn_ref, eo_hbm,
                   o_ref, buf0, buf1, acc_scr, sem, *, final):
    i = pl.program_id(0)
    n = pl.num_programs(0)
    tm = x_ref.shape[0]
    tr = TOKEN_ROWS

    half = D_MODEL // 2
    grp = SUBLANES

    def fetch_tokens(idx_ref, buf, slot, tok0):
        for tt in range(grp):
            tok = tok0 + tt
            for k in range(TOP_K):
                row = pl.multiple_of(idx_ref[k, tok] * tr, tr)
                pltpu.make_async_copy(eo_hbm.at[pl.ds(row, tr)], buf.at[k, pl.ds(pl.multiple_of(tok * tr, tr), tr)],
                                      sem.at[slot]).start(priority=k % 2)

    def arrived(buf, slot):
        for k in range(TOP_K):
            pltpu.make_async_copy(eo_hbm.at[pl.ds(0, tm * tr)], buf.at[k], sem.at[slot]).wait()

    def step(cur, cur_slot, nxt, nxt_slot):
        arrived(cur, cur_slot)
        h_lo, h_hi = _unpack_halves(_load_token_tiles(h_ref, tm))
        h_lo = h_lo.astype(BF16)
        h_hi = h_hi.astype(BF16)
        gate = _dot(h_lo, sg_ref[0:half, :]) + _dot(h_hi, sg_ref[half:, :])
        up = _dot(h_lo, su_ref[0:half, :]) + _dot(h_hi, su_ref[half:, :])
        act = gate * jax.nn.sigmoid(gate) * up
        acc_scr[...] = _dot(act.astype(BF16), sd_ref[...])

        def group(c, carry):
            tok0 = pl.multiple_of(c * grp, grp)
            fetch_tokens(nxt_ref, nxt, nxt_slot, tok0)
            gt = gt_ref[pl.ds(tok0, grp), :]
            r_lo = jnp.zeros((grp, half), F32)
            r_hi = r_lo
            for k in range(TOP_K):
                words = jnp.concatenate(
                    [cur[k, pl.ds(tok0 * tr + s, grp, stride=tr), :] for s in range(tr)], axis=1)
                e_lo, e_hi = _unpack_halves(words)
                r_lo = r_lo + gt[:, k:k + 1] * e_lo
                r_hi = r_hi + gt[:, k:k + 1] * e_hi
            acc_scr[pl.ds(tok0, grp), :] += jnp.concatenate([r_lo, r_hi], axis=1)
            return carry

        lax.fori_loop(0, tm // grp, group, 0)
        x = x_ref[...] + mod_ref[0][5:6] * acc_scr[...]
        if final:
            x = _rms(x, fn_ref[...])
        o_ref[...] = x

        @pl.when(i == n - 1)
        def _():
            arrived(nxt, nxt_slot)

    @pl.when(i == 0)
    def _():
        def first(c, carry):
            fetch_tokens(pos_ref, buf0, 0, pl.multiple_of(c * grp, grp))
            return carry
        lax.fori_loop(0, tm // grp, first, 0)

    for parity, cur, nxt in ((0, buf0, buf1), (1, buf1, buf0)):
        @pl.when(i % 2 == parity)
        def _(parity=parity, cur=cur, nxt=nxt):
            step(cur, parity, nxt, 1 - parity)


def _shared(x, h, gates_t, pos, eo, mod, sg, su, sd, final_norm, n_ctx_rows, dec_seq, *, final, row0=0, rows=None):
    rows = x.shape[0] if rows is None else rows
    tm = ROW_TILE // 2
    b0 = row0 // tm
    steps = rows // tm
    grp = lambda i: _group_of(i + b0, tm, n_ctx_rows, dec_seq)
    row = pl.BlockSpec((tm, D_MODEL), lambda i: (i + b0, 0))
    prow = pl.BlockSpec((tm * TOKEN_ROWS, LANES), lambda i: (i + b0, 0))
    full = lambda a, b: pl.BlockSpec((a, b), lambda i: (0, 0))
    buf = pltpu.VMEM((TOP_K, tm * TOKEN_ROWS, LANES), eo.dtype)
    return pl.pallas_call(
        functools.partial(_shared_kernel, final=final),
        grid=(steps,),
        in_specs=[pl.BlockSpec((SUBLANES, tm), lambda i: (0, i + b0), memory_space=pltpu.SMEM),
                  pl.BlockSpec((SUBLANES, tm), lambda i: (0, jnp.minimum(i + 1, steps - 1) + b0),
                               memory_space=pltpu.SMEM),
                  row, prow, pl.BlockSpec((tm, SUBLANES), lambda i: (i + b0, 0)),
                  pl.BlockSpec((1, 6, D_MODEL), lambda i: (grp(i), 0, 0)),
                  full(D_MODEL, D_SHARED), full(D_MODEL, D_SHARED), full(D_SHARED, D_MODEL), full(1, D_MODEL),
                  pl.BlockSpec(memory_space=pl.ANY)],
        out_specs=pl.BlockSpec((tm, D_MODEL), lambda i: (i, 0)),
        out_shape=jax.ShapeDtypeStruct((rows, D_MODEL), F32),
        scratch_shapes=[buf, buf, pltpu.VMEM((tm, D_MODEL), F32), pltpu.SemaphoreType.DMA((2,))],
        compiler_params=_cparams(("arbitrary",)),
        name="shared_final" if final else "shared",
    )(pos, pos, x, h, gates_t, mod, sg.astype(BF16), su.astype(BF16), sd.astype(BF16),
      final_norm.reshape(1, D_MODEL), eo)


def kernel(x_prompt, x_sample, state_ret, state_s5_re, state_s5_im, c, c_ctx, w_ada, b_ada, norm_mix, norm_ffn, w_in, w_out, ret_decay, s5_lam_re, s5_lam_im, s5_log_dt, s5_b_re, s5_b_im, s5_c_re, s5_c_im, s5_d, s5_glu_w, s5_glu_b, hy_conv_w, hy_conv_b, hy_f1_w, hy_f1_b, hy_f2_w, hy_f2_b, hy_f3_w, hy_f3_b, hy_freq, hy_decay, hy_bias, moe_router, moe_router_bias, moe_w_gate, moe_w_up, moe_w_down, sh_w_gate, sh_w_up, sh_w_down, final_norm):
    n_ctx, seq, d = x_prompt.shape
    n_dec, dec_seq, _ = x_sample.shape
    n_ctx_rows = n_ctx * seq
    t = n_ctx_rows + n_dec * dec_seq

    x = jnp.concatenate([x_prompt.reshape(n_ctx_rows, d), x_sample.reshape(n_dec * dec_seq, d)], axis=0)
    cond = jnp.concatenate([c_ctx[None, :], c], axis=0)
    cond8 = jnp.pad(cond, ((0, SUBLANES - cond.shape[0]), (0, 0)))
    mods = _ada(cond8, w_ada, b_ada)[:, :1 + n_dec].reshape(DEPTH, 1 + n_dec, 6, d)

    cos2, sin2 = _rope_tables(dec_seq)
    no_rope = jnp.zeros((seq, LANES), F32)
    zero_ret = jnp.zeros((n_ctx, 2, RET_HEADS, RET_DK, RET_DV), F32)
    zero_s5 = jnp.zeros((n_ctx, 2, S5_GROUPS, S5_STATE), F32)
    dft_ctx = _dft_mats(seq)
    dft_dec = _dft_mats(dec_seq)

    w_in_bf = w_in.astype(BF16)
    mats = _s5_mats(s5_lam_re, s5_lam_im, s5_log_dt, s5_b_re, s5_b_im, s5_c_re, s5_c_im, s5_d)
    filt = (hy_f1_w, hy_f1_b, hy_f2_w, hy_f2_b, hy_f3_w, hy_f3_b, hy_freq, hy_decay)
    mults_ctx = jax.vmap(functools.partial(_hy_filter_mults, seq))(*filt)
    mults_dec = jax.vmap(functools.partial(_hy_filter_mults, dec_seq))(*filt)

    ret_states = jnp.zeros((n_ctx, DEPTH, 2, RET_HEADS, RET_DK, RET_DV), F32)
    s5r_list, s5i_list = [], []
    for l in range(DEPTH):
        mod = mods[l]
        proj = _in_proj(x, mod, norm_mix[l], w_in_bf, l, n_ctx_rows, dec_seq)

        log_gamma = jax.nn.log_sigmoid(ret_decay[l].astype(F32))
        ret_o, ret_states = _retention(proj, log_gamma, zero_ret, no_rope, no_rope,
                                       row0=0, n_seq=n_ctx, seq_len=seq, hb=RET_HEADS, rope=False,
                                       dst=jnp.zeros((t, RET_WIDTH), BF16), states=ret_states, layer=l)
        ret_o, _ = _retention(proj, log_gamma, state_ret[:, l].astype(F32), cos2, sin2,
                              row0=n_ctx_rows, n_seq=n_dec, seq_len=dec_seq, hb=2, rope=True, dst=ret_o)

        s5_y, s5_re, s5_im = _s5(proj, mats, zero_s5, zero_s5, row0=0, n_seq=n_ctx, seq_len=seq, layer=l,
                                 dst=jnp.zeros((t, S5_WIDTH), F32))
        s5_y, _, _ = _s5(proj, mats, state_s5_re[:, l], state_s5_im[:, l],
                         row0=n_ctx_rows, n_seq=n_dec, seq_len=dec_seq, layer=l, dst=s5_y)
        s5r_list.append(s5_re)
        s5i_list.append(s5_im)

        hy_o = _hyena(proj, hy_conv_w[l], hy_conv_b[l], hy_bias[l], dft_ctx, mults_ctx, layer=l,
                      row0=0, n_seq=n_ctx, seq_len=seq, cb=HY_WIDTH, tk=seq, dst=jnp.zeros((t, HY_WIDTH), BF16))
        hy_o = _hyena(proj, hy_conv_w[l], hy_conv_b[l], hy_bias[l], dft_dec, mults_dec, layer=l,
                      row0=n_ctx_rows, n_seq=n_dec, seq_len=dec_seq, cb=HY_WIDTH // 2, tk=512, dst=hy_o)

        x, h2, logits = _out_proj(x, ret_o, s5_y, hy_o, mod, norm_ffn[l], s5_glu_w[l], s5_glu_b[l],
                                  w_out[l], moe_router[l], n_ctx_rows, dec_seq)

        nb = -(-(t * TOP_K) // MOE_BM) + N_EXPERTS
        pos, gates, blk_e_row, info = _router(logits, moe_router_bias[l], nb)
        blk_e, first, slot, nxt, n_used = _dispatch_plan(blk_e_row, info, nb)
        xs = _dispatch(h2, pos, info, n_used, nb)
        eo = _moe_grouped(xs, blk_e, first, slot, nxt, n_used, moe_w_gate, moe_w_up, moe_w_down, l)
        gates_t = gates.T

        sh = (sh_w_gate[l], sh_w_up[l], sh_w_down[l])
        if l < DEPTH - 1:
            x = _shared(x, h2, gates_t, pos, eo, mod, *sh, final_norm, n_ctx_rows, dec_seq, final=False)
        else:
            y_c = _shared(x, h2, gates_t, pos, eo, mod, *sh, final_norm, n_ctx_rows, dec_seq, final=True,
                          row0=0, rows=n_ctx_rows)
            y_d = _shared(x, h2, gates_t, pos, eo, mod, *sh, final_norm, n_ctx_rows, dec_seq, final=True,
                          row0=n_ctx_rows, rows=n_dec * dec_seq)

    return (y_c.reshape(n_ctx, seq, d), y_d.reshape(n_dec, dec_seq, d),
            ret_states, jnp.stack(s5r_list, axis=1), jnp.stack(s5i_list, axis=1))
```

```python
import functools
import math

import jax
import jax.numpy as jnp
from jax import lax
from jax.experimental import pallas as pl
from jax.experimental.pallas import tpu as pltpu

F32 = jnp.float32
BF16 = jnp.bfloat16

D_MODEL = 2048
DEPTH = 2
GRID_W = 64
RET_HEADS = 8
RET_DK = 128
RET_DV = 128
RET_WIDTH = RET_HEADS * RET_DV
RET_CHUNK = 256
ROPE_BASE = 10000.0
S5_WIDTH = 512
S5_GROUP = 16
S5_GROUPS = S5_WIDTH // S5_GROUP
S5_STATE = 64
S5_Q = 16
HY_WIDTH = 512
HY_BANDS = 16
IN_WIDTH = 4 * RET_WIDTH + S5_WIDTH + 3 * HY_WIDTH
U_COL = 4 * RET_WIDTH
HY_COL = U_COL + S5_WIDTH
N_EXPERTS = 64
TOP_K = 6
D_EXPERT = 512
D_SHARED = 512
ROUTED_SCALE = 2.5
EPS = 1e-6

LANES = 128
SUBLANES = 8
VMEM_LIMIT = 56 * 1024 * 1024

ROW_TILE = 512
MOE_BM = 256


def _cparams(sem):
    return pltpu.CompilerParams(dimension_semantics=sem, vmem_limit_bytes=VMEM_LIMIT)


def _dot(a, b):
    return jnp.dot(a, b, preferred_element_type=F32)


def _rms(x, g):
    var = jnp.mean(x * x, axis=-1, keepdims=True)
    return x * lax.rsqrt(var + EPS) * g


def _pack_halves(xb):
    n = xb.shape[1] // 2
    lo = lax.bitcast_convert_type(xb[:, :n].astype(F32), jnp.uint32) >> 16
    hi = lax.bitcast_convert_type(xb[:, n:].astype(F32), jnp.uint32)
    return lo | hi


def _unpack_halves(w):
    lo = lax.bitcast_convert_type(w << 16, F32)
    hi = lax.bitcast_convert_type(w & jnp.uint32(0xFFFF0000), F32)
    return lo, hi


TOKEN_ROWS = D_MODEL // 2 // LANES


def _store_token_tiles(ref, w):
    m = w.shape[0]
    for s in range(TOKEN_ROWS):
        ref[pl.ds(s, m, stride=TOKEN_ROWS), :] = w[:, s * LANES:(s + 1) * LANES]


def _load_token_tiles(ref, m):
    return jnp.concatenate([ref[pl.ds(s, m, stride=TOKEN_ROWS), :] for s in range(TOKEN_ROWS)], axis=1)


def _ada_kernel(c_ref, w_ref, b_ref, o_ref):
    c = c_ref[...]
    s = (c * jax.nn.sigmoid(c)).astype(BF16)
    o_ref[0] = _dot(s, w_ref[0].astype(BF16)) + b_ref[0]


def _ada(cond8, w_ada, b_ada):
    tn = 512
    n = w_ada.shape[-1]
    return pl.pallas_call(
        _ada_kernel,
        grid=(DEPTH, n // tn),
        in_specs=[
            pl.BlockSpec((SUBLANES, D_MODEL), lambda l, j: (0, 0)),
            pl.BlockSpec((1, D_MODEL, tn), lambda l, j: (l, 0, j)),
            pl.BlockSpec((1, 1, tn), lambda l, j: (l, 0, j)),
        ],
        out_specs=pl.BlockSpec((1, SUBLANES, tn), lambda l, j: (l, 0, j)),
        out_shape=jax.ShapeDtypeStruct((DEPTH, SUBLANES, n), F32),
        compiler_params=_cparams(("parallel", "parallel")),
        name="ada",
    )(cond8, w_ada, b_ada.reshape(DEPTH, 1, n))


def _group_of(i, tile, n_ctx_rows, dec_seq):
    ctx_tiles = n_ctx_rows // tile
    per = dec_seq // tile
    return jnp.where(i < ctx_tiles, 0, 1 + (i - ctx_tiles) // per)


def _in_kernel(x_ref, mod_ref, g_ref, w_ref, o_ref, h_scr):
    @pl.when(pl.program_id(1) == 0)
    def _():
        m = mod_ref[0]
        h = _rms(x_ref[...], g_ref[...]) * (1.0 + m[1:2]) + m[0:1]
        h_scr[...] = h.astype(BF16)

    o_ref[...] = _dot(h_scr[...], w_ref[...])


def _in_proj(x, mod, g, w_bf, layer, n_ctx_rows, dec_seq):
    t = x.shape[0]
    tm, tn = 1024, 1024
    grp = functools.partial(_group_of, tile=tm, n_ctx_rows=n_ctx_rows, dec_seq=dec_seq)
    return pl.pallas_call(
        _in_kernel,
        grid=(t // tm, IN_WIDTH // tn),
        in_specs=[
            pl.BlockSpec((tm, D_MODEL), lambda i, j: (i, 0)),
            pl.BlockSpec((1, 6, D_MODEL), lambda i, j: (grp(i), 0, 0)),
            pl.BlockSpec((1, D_MODEL), lambda i, j: (0, 0)),
            pl.BlockSpec((None, D_MODEL, tn), lambda i, j: (layer, 0, j)),
        ],
        out_specs=pl.BlockSpec((tm, tn), lambda i, j: (i, j)),
        out_shape=jax.ShapeDtypeStruct((t, IN_WIDTH), F32),
        scratch_shapes=[pltpu.VMEM((tm, D_MODEL), BF16)],
        compiler_params=_cparams(("parallel", "arbitrary")),
        name="in_proj",
    )(x, mod, g.reshape(1, D_MODEL), w_bf)


def _ret_kernel(lg_ref, q_ref, k_ref, v_ref, gt_ref, cos_ref, sin_ref, s0_ref,
                o_ref, sfin_ref, acc_scr, q_scr, k_scr, *, seq_len, hb, rope):
    c = RET_CHUNK
    n_chunks = seq_len // c
    ii = lax.broadcasted_iota(jnp.int32, (c, c), 0)
    jj = lax.broadcasted_iota(jnp.int32, (c, c), 1)
    rel = (ii - jj).astype(F32)
    ci = lax.broadcasted_iota(jnp.int32, (c, 1), 0).astype(F32)
    one = jnp.ones((1, 1), F32)
    tdot = functools.partial(lax.dot_general, preferred_element_type=F32)

    def make_head(hh):
        head = pl.program_id(1) * hb + hh
        lgf = lg_ref[0, head]
        lgb = lg_ref[1, head]
        dmask = (jnp.where(rel >= 0, jnp.exp(lgf * jnp.maximum(rel, 0.0)), 0.0)
                 + jnp.where(rel <= 0, jnp.exp(lgb * jnp.maximum(-rel, 0.0)), 0.0))
        qd_f = jnp.exp(lgf * (ci + 1.0))
        kd_f = jnp.exp(lgf * (c - 1.0 - ci))
        cd_f = jnp.exp(lgf * c * one)
        qd_b = jnp.exp(lgb * (c - ci))
        kd_b = jnp.exp(lgb * ci)
        cd_b = jnp.exp(lgb * c * one)
        lanes = slice(hh * LANES, (hh + 1) * LANES)

        def rows_of(n):
            if isinstance(n, int):
                return slice(n * c, (n + 1) * c)
            return pl.ds(pl.multiple_of(n * c, c), c)

        def fwd_chunk(n, s_f):
            rows = rows_of(n)
            q = q_ref[rows, lanes]
            k = k_ref[rows, lanes] * (RET_DK ** -0.5)
            if rope:
                cs = cos_ref[rows, :]
                sn = sin_ref[rows, :]
                q = q * cs + pltpu.roll(q, RET_DK // 2, 1) * sn
                k = k * cs + pltpu.roll(k, RET_DK // 2, 1) * sn
            qb = q.astype(BF16)
            vb = v_ref[rows, lanes].astype(BF16)
            q_scr[rows, lanes] = qb
            k_scr[rows, lanes] = k
            scores = tdot(qb, k.astype(BF16), (((1,), (1,)), ((), ()))) * dmask
            inner = _dot(scores.astype(BF16), vb)
            cross = _dot(qb, s_f.astype(BF16)) * qd_f
            acc_scr[rows, lanes] = inner + cross
            upd = tdot((k * kd_f).astype(BF16), vb, (((0,), (0,)), ((), ())))
            return s_f * cd_f + upd

        def bwd_chunk(m, s_b):
            n = n_chunks - 1 - m
            rows = rows_of(n)
            qb = q_scr[rows, lanes]
            k = k_scr[rows, lanes]
            vb = v_ref[rows, lanes].astype(BF16)
            o = acc_scr[rows, lanes] + _dot(qb, s_b.astype(BF16)) * qd_b
            mu = jnp.mean(o, axis=-1, keepdims=True)
            oc = o - mu
            var = jnp.mean(oc * oc, axis=-1, keepdims=True)
            o = oc * lax.rsqrt(var + EPS)
            g = gt_ref[rows, lanes]
            o_ref[rows, lanes] = (g * jax.nn.sigmoid(g) * o).astype(o_ref.dtype)
            upd = tdot((k * kd_b).astype(BF16), vb, (((0,), (0,)), ((), ())))
            return s_b * cd_b + upd

        return fwd_chunk, bwd_chunk

    if n_chunks <= 4:
        for hh in range(hb):
            fwd_chunk, bwd_chunk = make_head(hh)
            s_f = s0_ref[0, 0, hh]
            s_b = s0_ref[0, 1, hh]
            for n in range(n_chunks):
                s_f = fwd_chunk(n, s_f)
            for m in range(n_chunks):
                s_b = bwd_chunk(m, s_b)
            sfin_ref[0, 0, hh] = s_f
            sfin_ref[0, 1, hh] = s_b
    else:
        fns = [make_head(hh) for hh in range(hb)]
        s_f = lax.fori_loop(0, n_chunks, lambda n, ss: tuple(f[0](n, s) for f, s in zip(fns, ss)),
                            tuple(s0_ref[0, 0, hh] for hh in range(hb)))
        s_b = lax.fori_loop(0, n_chunks, lambda m, ss: tuple(f[1](m, s) for f, s in zip(fns, ss)),
                            tuple(s0_ref[0, 1, hh] for hh in range(hb)))
        for hh in range(hb):
            sfin_ref[0, 0, hh] = s_f[hh]
            sfin_ref[0, 1, hh] = s_b[hh]


def _into(kernel_fn, n_in, dst):
    dsts = [d for d in (dst if isinstance(dst, (list, tuple)) else [dst])]
    outs = [k for k, d in enumerate(dsts) if d is not None]
    if not outs:
        return kernel_fn, [], [], {}

    def body(*refs):
        return kernel_fn(*refs[:n_in], *refs[n_in + len(outs):])

    return (body, [pl.BlockSpec(memory_space=pl.ANY)] * len(outs), [dsts[k] for k in outs],
            {n_in + pos: k for pos, k in enumerate(outs)})


def _retention(proj, log_gamma, s0, cos2, sin2, *, row0, n_seq, seq_len, hb, rope, dst=None,
               states=None, layer=0):
    blk0 = row0 // seq_len
    body, dst_spec, dst_arg, alias = _into(
        functools.partial(_ret_kernel, seq_len=seq_len, hb=hb, rope=rope), 8, [dst, states])
    if states is None:
        st_spec = pl.BlockSpec((1, 2, hb, RET_DK, RET_DV), lambda b, h, lg: (b, 0, h, 0, 0))
        st_shape = jax.ShapeDtypeStruct((n_seq, 2, RET_HEADS, RET_DK, RET_DV), F32)
    else:
        st_spec = pl.BlockSpec((1, None, 2, hb, RET_DK, RET_DV), lambda b, h, lg: (b, layer, 0, h, 0, 0))
        st_shape = jax.ShapeDtypeStruct(states.shape, F32)
    w = hb * LANES
    hblocks = RET_HEADS // hb
    col = lambda part: (lambda b, h, lg: (blk0 + b, part * hblocks + h))
    grid_spec = pltpu.PrefetchScalarGridSpec(
        num_scalar_prefetch=1,
        grid=(n_seq, hblocks),
        in_specs=[
            pl.BlockSpec((seq_len, w), col(0)),
            pl.BlockSpec((seq_len, w), col(1)),
            pl.BlockSpec((seq_len, w), col(2)),
            pl.BlockSpec((seq_len, w), col(3)),
            pl.BlockSpec((seq_len, LANES), lambda b, h, lg: (0, 0)),
            pl.BlockSpec((seq_len, LANES), lambda b, h, lg: (0, 0)),
            pl.BlockSpec((1, 2, hb, RET_DK, RET_DV), lambda b, h, lg: (b, 0, h, 0, 0)),
        ] + dst_spec,
        out_specs=[
            pl.BlockSpec((seq_len, w), lambda b, h, lg: (blk0 + b, h)),
            st_spec,
        ],
        scratch_shapes=[
            pltpu.VMEM((seq_len, w), F32),
            pltpu.VMEM((seq_len, w), BF16),
            pltpu.VMEM((seq_len, w), F32),
        ],
    )
    return pl.pallas_call(
        body,
        grid_spec=grid_spec,
        out_shape=[jax.ShapeDtypeStruct((proj.shape[0], RET_WIDTH), BF16), st_shape],
        input_output_aliases=alias,
        compiler_params=_cparams(("parallel", "arbitrary")),
        name="retention",
    )(log_gamma, proj, proj, proj, proj, cos2, sin2, s0, *dst_arg)


def _rope_tables(seq_len):
    rows_n = seq_len // GRID_W
    rows = jnp.repeat(jnp.arange(rows_n, dtype=F32), GRID_W)
    cols = jnp.tile(jnp.arange(GRID_W, dtype=F32), rows_n)
    nf = RET_DK // 4
    inv = ROPE_BASE ** (-jnp.arange(nf, dtype=F32) / nf)
    ang = jnp.concatenate([rows[:, None] * inv, cols[:, None] * inv], axis=-1)
    cs, sn = jnp.cos(ang), jnp.sin(ang)
    return jnp.concatenate([cs, cs], axis=-1), jnp.concatenate([-sn, sn], axis=-1)


S5_GB = LANES // S5_GROUP
S5_W = S5_Q * LANES
S5_SPLIT = 4
S5_SW = S5_GB * S5_STATE
S5_SB = S5_SW // LANES


def _s5_kernel(u_ref, bq_ref, k_ref, cq_ref, ar_ref, ai_ref, d_ref, h0_ref,
               y_ref, hf_ref, ub_scr, sm_scr, hp_scr, hpb_scr, t_scr, *, n_seq, n_chunks):
    s = pl.program_id(1)
    m = n_seq * n_chunks
    q = S5_Q
    nblk = S5_W // LANES
    sb = S5_SB

    per = q // S5_SPLIT
    kw = S5_W // S5_SPLIT

    @pl.when(s == 0)
    def _():
        for j in range(q):
            ub_scr[j // per, :, (j % per) * LANES:(j % per + 1) * LANES] = (
                u_ref[pl.ds(j, m, stride=q), :].astype(BF16))
        for ib in range(S5_SPLIT):
            for j in range(q):
                c0 = (q - 1 - j) * LANES + ib * kw
                t_scr[ib, j * LANES:(j + 1) * LANES, :] = k_ref[0, :, c0:c0 + kw]

    @pl.when(s < S5_SPLIT)
    def _():
        part = _dot(ub_scr[jnp.minimum(s, S5_SPLIT - 1)], bq_ref[0])

        @pl.when(s == 0)
        def _():
            for cb in range(nblk):
                sm_scr[cb] = part[:, cb * LANES:(cb + 1) * LANES]

        @pl.when(s > 0)
        def _():
            for cb in range(nblk):
                sm_scr[cb] += part[:, cb * LANES:(cb + 1) * LANES]

    @pl.when(s == S5_SPLIT - 1)
    def _():
        ar = ar_ref[0]
        ai = ai_ref[0]
        h0 = h0_ref[0]
        blk = lambda a, cb: a[:, cb * LANES:(cb + 1) * LANES]

        def body(n, carry):
            rows_f = pl.ds(n, n_seq, stride=n_chunks)
            rows_b = pl.ds(n_chunks - 1 - n, n_seq, stride=n_chunks)
            new = list(carry)
            for d, rows in ((0, rows_f), (1, rows_b)):
                for c in range(sb):
                    re_i = d * sb + c
                    im_i = (2 + d) * sb + c
                    hr, hi = carry[re_i], carry[im_i]
                    hp_scr[re_i, rows, :] = hr
                    hp_scr[im_i, rows, :] = hi
                    a_r, a_i = blk(ar, re_i), blk(ai, re_i)
                    new[re_i] = a_r * hr - a_i * hi + sm_scr[re_i, rows, :]
                    new[im_i] = a_r * hi + a_i * hr + sm_scr[im_i, rows, :]
            return tuple(new)

        fin = lax.fori_loop(0, n_chunks, body, tuple(blk(h0, cb) for cb in range(nblk)))
        hf_ref[0] = jnp.concatenate(fin, axis=1)
        for cb in range(nblk):
            hpb_scr[:, cb * LANES:(cb + 1) * LANES] = hp_scr[cb].astype(BF16)

    @pl.when(s >= S5_SPLIT)
    def _():
        ub = jnp.concatenate([ub_scr[k] for k in range(S5_SPLIT)], axis=1)
        y = _dot(ub, t_scr[jnp.maximum(s - S5_SPLIT, 0)]) + _dot(hpb_scr[...], cq_ref[0])
        dd = d_ref[0]
        for ii in range(per):
            rows = pl.ds((s - S5_SPLIT) * per + ii, m, stride=q)
            yi = y[:, ii * LANES:(ii + 1) * LANES] + dd * u_ref[rows, :]
            y_ref[rows, :] = jax.nn.gelu(yi)


def _s5_expand_kernel(mc_ref, o_ref, *, xsize, ysize):
    xs, ys, gs = xsize.bit_length() - 1, ysize.bit_length() - 1, S5_GB.bit_length() - 1
    assert xsize == 1 << xs and ysize == 1 << ys and S5_GB == 1 << gs
    cw = o_ref.shape[2]
    nc = mc_ref.shape[2]
    col0 = pl.program_id(1) * cw
    r = lax.broadcasted_iota(jnp.int32, (nc, cw), 0)
    col = lax.broadcasted_iota(jnp.int32, (nc, cw), 1) + col0
    spread = jnp.logical_and(r >> ys == col >> (ys + gs), (r & (ysize - 1)) == (col & (ysize - 1)))
    big = _dot(mc_ref[0], jnp.where(spread, 1.0, 0.0).astype(BF16))
    row = lax.broadcasted_iota(jnp.int32, big.shape, 0)
    colb = lax.broadcasted_iota(jnp.int32, big.shape, 1) + col0
    same = ((row >> xs) & (S5_GB - 1)) == ((colb >> ys) & (S5_GB - 1))
    o_ref[0] = jnp.where(same, big, 0.0).astype(BF16)


def _s5_expand(mc, *, xsize, ysize):
    nb, rows, nc = mc.shape
    cols = nc * S5_GB
    cw = 512
    return pl.pallas_call(
        functools.partial(_s5_expand_kernel, xsize=xsize, ysize=ysize),
        grid=(nb, cols // cw),
        in_specs=[pl.BlockSpec((1, rows, nc), lambda b, j: (b, 0, 0))],
        out_specs=pl.BlockSpec((1, rows, cw), lambda b, j: (b, 0, j)),
        out_shape=jax.ShapeDtypeStruct((nb, rows, cols), BF16),
        compiler_params=_cparams(("parallel", "parallel")),
        name="s5_expand",
    )(mc)


def _s5_mats(lam_re, lam_im, log_dt, b_re, b_im, c_re, c_im, d):
    flat = lambda a: a.reshape((-1,) + a.shape[2:])
    kc, bq, cq, ar, ai, dd = map(flat, jax.vmap(_s5_compact)(lam_re, lam_im, log_dt, b_re, b_im, c_re, c_im, d))
    ch, p = S5_GROUP, S5_STATE
    return (_s5_expand(kc, xsize=ch, ysize=ch),
            _s5_expand(bq, xsize=ch, ysize=p),
            _s5_expand(cq, xsize=p, ysize=ch),
            ar, ai, dd)


def _s5_compact(lam_re, lam_im, log_dt, b_re, b_im, c_re, c_im, d):
    q, g, p, ch = S5_Q, S5_GROUPS, S5_STATE, S5_GROUP
    lam = lax.complex(jnp.minimum(lam_re.astype(F32), -1e-4), lam_im.astype(F32))
    ldt = lam * jnp.exp(log_dt.astype(F32))[..., None]
    lam_bar = jnp.exp(ldt)
    b_bar = ((lam_bar - 1.0) / lam)[..., None] * lax.complex(b_re.astype(F32), b_im.astype(F32))
    cc = lax.complex(c_re.astype(F32), c_im.astype(F32))
    pw = jnp.exp(ldt[..., None] * jnp.arange(q + 1, dtype=F32))
    hi = lax.Precision.HIGHEST
    lag = jnp.arange(2 * q, dtype=F32) - (q - 1)
    wf = jnp.where(lag >= 0, jnp.exp(ldt[0][..., None] * jnp.maximum(lag, 0.0)), 0.0)
    wb = jnp.where(lag <= 0, jnp.exp(ldt[1][..., None] * jnp.maximum(-lag, 0.0)), 0.0)
    kc = jnp.real(jnp.einsum('gcp,gpd,gpe->gedc', cc[0], wf, b_bar[0], precision=hi)
                  + jnp.einsum('gcp,gpd,gpe->gedc', cc[1], wb, b_bar[1], precision=hi))

    pw_dn = jnp.exp(ldt[..., None] * (q - jnp.arange(q + 1, dtype=F32)))
    bf = pw_dn[0][..., 1:][:, :, :, None] * b_bar[0][:, :, None, :]
    bb = pw[1][..., :q][:, :, :, None] * b_bar[1][:, :, None, :]
    to_rows = lambda m: m.transpose(0, 2, 3, 1).reshape(g, q * ch, p)
    bq = jnp.concatenate([to_rows(jnp.real(bf)), to_rows(jnp.real(bb)),
                          to_rows(jnp.imag(bf)), to_rows(jnp.imag(bb))], axis=-1)

    cf = cc[0].transpose(0, 2, 1)[:, :, None, :] * pw[0][..., 1:][:, :, :, None]
    cb = cc[1].transpose(0, 2, 1)[:, :, None, :] * pw_dn[1][..., :q][:, :, :, None]
    to_cols = lambda m: m.reshape(g, p, q * ch)
    cq = jnp.concatenate([to_cols(jnp.real(cf)), to_cols(jnp.real(cb)),
                          to_cols(-jnp.imag(cf)), to_cols(-jnp.imag(cb))], axis=1)

    gb, nb = S5_GB, g // S5_GB
    rows_of = lambda a, outer, inner: (a.reshape(nb, gb, outer, inner, a.shape[-1]).transpose(0, 2, 1, 3, 4)
                                       .reshape(nb, outer * gb * inner, a.shape[-1]).astype(BF16))
    kc = kc.reshape(nb, gb * ch, 2 * q * ch).astype(BF16)

    lq = pw[..., q].reshape(2, nb, 1, S5_SW)
    ar = jnp.concatenate([jnp.real(lq[0]), jnp.real(lq[1])], axis=-1)
    ai = jnp.concatenate([jnp.imag(lq[0]), jnp.imag(lq[1])], axis=-1)
    dd = d.astype(F32).reshape(nb, 1, LANES)
    return kc, rows_of(bq, q, ch), rows_of(cq, 4, p), ar, ai, dd


def _s5(proj, mats, h0_re, h0_im, *, row0, n_seq, seq_len, layer=0, dst=None):
    q, p = S5_Q, S5_STATE
    n_chunks = seq_len // q
    m = n_seq * n_chunks
    rows = n_seq * seq_len
    nb = S5_GROUPS // S5_GB
    nblk = S5_W // LANES
    kw = S5_W // S5_SPLIT
    tm, bq, cq, ar, ai, dd = mats
    part = lambda a: a.astype(F32).reshape(n_seq, nb, S5_SW)
    h0 = jnp.concatenate([part(h0_re[:, 0]), part(h0_re[:, 1]), part(h0_im[:, 0]), part(h0_im[:, 1])],
                         axis=-1).transpose(1, 0, 2)
    per_b = lambda shape: pl.BlockSpec((1,) + shape, lambda b, s: (b, 0, 0))
    b0 = layer * nb
    per_l = lambda shape: pl.BlockSpec((1,) + shape, lambda b, s: (b0 + b, 0, 0))
    body, dst_spec, dst_arg, alias = _into(functools.partial(_s5_kernel, n_seq=n_seq, n_chunks=n_chunks), 8, dst)
    y, hf = pl.pallas_call(
        body,
        grid=(nb, 2 * S5_SPLIT),
        in_specs=[
            pl.BlockSpec((rows, LANES), lambda b, s: (row0 // rows, U_COL // LANES + b)),
            pl.BlockSpec((1, kw, 4 * S5_SW), lambda b, s: (b0 + b, jnp.minimum(s, S5_SPLIT - 1), 0)),
            per_l((LANES, 2 * q * LANES)),
            pl.BlockSpec((1, 4 * S5_SW, kw), lambda b, s: (b0 + b, 0, jnp.maximum(s - S5_SPLIT, 0))),
            per_l((1, 2 * S5_SW)), per_l((1, 2 * S5_SW)), per_l((1, LANES)), per_b((n_seq, 4 * S5_SW)),
        ] + dst_spec,
        out_specs=[pl.BlockSpec((rows, LANES), lambda b, s: (row0 // rows, b)), per_b((n_seq, 4 * S5_SW))],
        out_shape=[jax.ShapeDtypeStruct((proj.shape[0], S5_WIDTH), F32),
                   jax.ShapeDtypeStruct((nb, n_seq, 4 * S5_SW), F32)],
        scratch_shapes=[pltpu.VMEM((S5_SPLIT, m, kw), BF16), pltpu.VMEM((nblk, m, LANES), F32),
                        pltpu.VMEM((nblk, m, LANES), F32), pltpu.VMEM((m, 4 * S5_SW), BF16),
                        pltpu.VMEM((S5_SPLIT, S5_W, kw), BF16)],
        input_output_aliases=alias,
        compiler_params=_cparams(("parallel", "arbitrary")),
        name="s5",
    )(proj, bq, tm, cq, ar, ai, dd, h0, *dst_arg)
    hf = hf.reshape(nb, n_seq, 4, S5_GB, p).transpose(1, 2, 0, 3, 4).reshape(n_seq, 4, S5_GROUPS, p)
    return y, hf[:, 0:2], hf[:, 2:4]


def _conv3(x, w, b):
    n = x.shape[0]
    row = lax.broadcasted_iota(jnp.int32, x.shape, 0)
    prev = jnp.where(row == 0, 0.0, pltpu.roll(x, 1, 0))
    nxt = jnp.where(row == n - 1, 0.0, pltpu.roll(x, n - 1, 0))
    return prev * w[0:1] + x * w[1:2] + nxt * w[2:3] + b


def _hy_fwd_kernel(x0_ref, x1_ref, v_ref, w0_ref, w1_ref, wv_ref, b0_ref, b1_ref, bv_ref,
                   fc_ref, fs_ref, m1_ref, m2_ref, m3_ref, p_ref, z_ref, x0c_ref, zb_scr):
    @pl.when(pl.program_id(2) == 0)
    def _():
        z = _conv3(x1_ref[...], w1_ref[...], b1_ref[...]) * _conv3(v_ref[...], wv_ref[...], bv_ref[...])
        z_ref[...] = z
        zb_scr[...] = z.astype(BF16)
        x0c_ref[...] = _conv3(x0_ref[...], w0_ref[...], b0_ref[...])

    zb = zb_scr[...]
    a = _dot(fc_ref[...], zb)
    b = _dot(fs_ref[...], zb)
    m2 = m2_ref[...]
    p_ref[0, 0] = (m1_ref[...] * a + m2 * b).astype(BF16)
    p_ref[0, 1] = (m3_ref[...] * b - m2 * a).astype(BF16)


def _hy_inv_kernel(p_ref, gc_ref, gs_ref, z_ref, x0c_ref, bias_ref, o_ref):
    conv = _dot(gc_ref[...], p_ref[0, 0]) + _dot(gs_ref[...], p_ref[0, 1])
    o_ref[...] = (x0c_ref[...] * (conv + bias_ref[...] * z_ref[...])).astype(o_ref.dtype)


def _dft_mats(seq_len):
    n, w = seq_len, 64
    k = jnp.arange(n, dtype=jnp.int32)
    ang = lambda j: ((k[:, None] * j[None, :]) % (2 * n)).astype(F32) * (math.pi / n)
    ang_a = ang(jnp.arange(n // w, dtype=jnp.int32) * w)
    ang_b = ang(jnp.arange(w, dtype=jnp.int32))
    ca, sa = jnp.cos(ang_a)[:, :, None], jnp.sin(ang_a)[:, :, None]
    cb, sb = jnp.cos(ang_b)[:, None, :], jnp.sin(ang_b)[:, None, :]
    cm = (ca * cb - sa * sb).reshape(n, n)
    sm = -(sa * cb + ca * sb).reshape(n, n)
    nyq = jnp.where(k % 2 == 0, 1.0, -1.0).astype(F32)
    return cm.astype(BF16), sm.at[0, :].set(nyq).astype(BF16), sm.at[:, 0].set(nyq).astype(BF16)


def _hy_filter_taps(seq_len, f1_w, f1_b, f2_w, f2_b, f3_w, f3_b, freq, decay):
    n = seq_len
    t = (jnp.arange(n, dtype=F32) / n)[:, None]
    bands = jnp.arange(1, HY_BANDS + 1, dtype=F32)[None, :]
    z = jnp.concatenate([t, jnp.cos(2.0 * math.pi * t * bands), jnp.sin(2.0 * math.pi * t * bands)], axis=-1)
    hi = lax.Precision.HIGHEST
    fr = freq.astype(F32)
    h = jnp.sin(fr * (jnp.dot(z, f1_w.astype(F32), precision=hi) + f1_b.astype(F32)))
    h = jnp.sin(fr * (jnp.dot(h, f2_w.astype(F32), precision=hi) + f2_b.astype(F32)))
    h = jnp.dot(h, f3_w.astype(F32), precision=hi) + f3_b.astype(F32)
    h = h * jnp.exp(-t * jnp.abs(decay.astype(F32)))
    h = h.reshape(n, 2, HY_WIDTH)
    h = h / jnp.sum(jnp.abs(h), axis=(0, 1), keepdims=True)
    return h.reshape(n, 2 * HY_WIDTH)


def _hy_spectrum_kernel(h_ref, cm_ref, sm_ref, ck_ref, sk_ref, m1_ref, m2_ref, m3_ref, *, seq_len):
    w = HY_WIDTH
    tk = cm_ref.shape[0]
    hb = h_ref[0].astype(BF16)
    xc = _dot(cm_ref[...], hb)
    xs = _dot(sm_ref[...], hb)
    k = pl.program_id(1) * tk + lax.broadcasted_iota(jnp.int32, (tk, 1), 0)
    first = k == 0
    sf = jnp.where(first, 0.0, -xs[:, :w])
    sb = jnp.where(first, 0.0, -xs[:, w:])
    ck, sk = ck_ref[...], sk_ref[...]
    hr = xc[:, :w] + ck * xc[:, w:] - sk * sb
    him = -sf + ck * sb + sk * xc[:, w:]
    nyq = xs[:, :w] - xs[:, w:]
    wk = jnp.where(first, 1.0, 2.0) / (2.0 * seq_len)
    m1_ref[0] = hr * wk
    m2_ref[0] = jnp.where(first, 0.0, -him) * wk
    m3_ref[0] = jnp.where(first, nyq, hr) * wk


def _hy_spectrum(h, dft):
    depth, n, _ = h.shape
    cm, sm, _ = dft
    tk = min(n, 512)
    ang = jnp.arange(n, dtype=F32)[:, None] * (math.pi / n)
    out = jax.ShapeDtypeStruct((depth, n, HY_WIDTH), F32)
    oblk = pl.BlockSpec((1, tk, HY_WIDTH), lambda l, k: (l, k, 0))
    return pl.pallas_call(
        functools.partial(_hy_spectrum_kernel, seq_len=n),
        grid=(depth, n // tk),
        in_specs=[pl.BlockSpec((1, n, 2 * HY_WIDTH), lambda l, k: (l, 0, 0)),
                  pl.BlockSpec((tk, n), lambda l, k: (k, 0)), pl.BlockSpec((tk, n), lambda l, k: (k, 0)),
                  pl.BlockSpec((tk, 1), lambda l, k: (k, 0)), pl.BlockSpec((tk, 1), lambda l, k: (k, 0))],
        out_specs=[oblk, oblk, oblk],
        out_shape=[out, out, out],
        compiler_params=_cparams(("parallel", "parallel")),
        name="hyena_spectrum",
    )(h, cm, sm, jnp.cos(ang), jnp.sin(ang))


def _hyena(proj, conv_w, conv_b, bias, dft, mults, *, row0, n_seq, seq_len, cb, tk, layer=0, dst=None):
    blk0 = row0 // seq_len
    nc = HY_WIDTH // cb
    nk = seq_len // tk
    c0 = HY_COL // cb
    cm, sm, smt = dft
    m1, m2, m3 = mults
    xcol = lambda part: pl.BlockSpec((seq_len, cb), lambda b, c, k: (blk0 + b, c0 + part * nc + c))
    wcol = lambda part: pl.BlockSpec((3, cb), lambda b, c, k: (0, part * nc + c))
    bcol = lambda part: pl.BlockSpec((1, cb), lambda b, c, k: (0, part * nc + c))
    frow = pl.BlockSpec((tk, seq_len), lambda b, c, k: (k, 0))
    mblk = pl.BlockSpec((None, tk, cb), lambda b, c, k: (layer, k, c))
    cb2 = conv_b.reshape(1, 3 * HY_WIDTH)
    pspec, z, x0c = pl.pallas_call(
        _hy_fwd_kernel,
        grid=(n_seq, nc, nk),
        in_specs=[xcol(0), xcol(1), xcol(2), wcol(0), wcol(1), wcol(2), bcol(0), bcol(1), bcol(2),
                  frow, frow, mblk, mblk, mblk],
        out_specs=[
            pl.BlockSpec((1, 2, tk, cb), lambda b, c, k: (b, 0, k, c)),
            pl.BlockSpec((seq_len, cb), lambda b, c, k: (b, c)),
            pl.BlockSpec((seq_len, cb), lambda b, c, k: (b, c)),
        ],
        out_shape=[
            jax.ShapeDtypeStruct((n_seq, 2, seq_len, HY_WIDTH), BF16),
            jax.ShapeDtypeStruct((n_seq * seq_len, HY_WIDTH), F32),
            jax.ShapeDtypeStruct((n_seq * seq_len, HY_WIDTH), F32),
        ],
        scratch_shapes=[pltpu.VMEM((seq_len, cb), BF16)],
        compiler_params=_cparams(("parallel", "parallel", "arbitrary")),
        name="hyena_fwd",
    )(proj, proj, proj, conv_w, conv_w, conv_w, cb2, cb2, cb2, cm, sm, m1, m2, m3)
    grow = pl.BlockSpec((tk, seq_len), lambda b, c, k: (k, 0))
    tile = pl.BlockSpec((tk, cb), lambda b, c, k: (b * nk + k, c))
    body, dst_spec, dst_arg, alias = _into(_hy_inv_kernel, 6, dst)
    return pl.pallas_call(
        body,
        grid=(n_seq, nc, nk),
        in_specs=[pl.BlockSpec((1, 2, seq_len, cb), lambda b, c, k: (b, 0, 0, c)),
                  grow, grow, tile, tile, pl.BlockSpec((1, cb), lambda b, c, k: (0, c))] + dst_spec,
        out_specs=pl.BlockSpec((tk, cb), lambda b, c, k: (row0 // tk + b * nk + k, c)),
        out_shape=jax.ShapeDtypeStruct((proj.shape[0], HY_WIDTH), BF16),
        input_output_aliases=alias,
        compiler_params=_cparams(("parallel", "parallel", "arbitrary")),
        name="hyena_inv",
    )(pspec, cm, smt, z, x0c, bias.reshape(1, HY_WIDTH), *dst_arg)


def _out_kernel(x_ref, ret_ref, s5_ref, hy_ref, mod_ref, g_ref, gw_ref, gb_ref,
                wr_ref, ws_ref, wh_ref, rt_ref, xo_ref, h_ref, lg_ref):
    m = mod_ref[0]
    y = s5_ref[...]
    s5o = y * jax.nn.sigmoid(_dot(y.astype(BF16), gw_ref[...]) + gb_ref[...])
    mix = (_dot(ret_ref[...], wr_ref[...]) + _dot(s5o.astype(BF16), ws_ref[...])
           + _dot(hy_ref[...], wh_ref[...]))
    x = x_ref[...] + m[2:3] * mix
    xo_ref[...] = x
    h = _rms(x, g_ref[...]) * (1.0 + m[4:5]) + m[3:4]
    hb = h.astype(BF16)
    _store_token_tiles(h_ref, _pack_halves(hb))
    lg_ref[...] = lax.dot_general(rt_ref[...], hb, (((1,), (1,)), ((), ())), preferred_element_type=F32)


def _out_proj(x, ret_o, s5_y, hy_o, mod, g, glu_w, glu_b, w_out, router, n_ctx_rows, dec_seq):
    t = x.shape[0]
    tm = ROW_TILE
    grp = functools.partial(_group_of, tile=tm, n_ctx_rows=n_ctx_rows, dec_seq=dec_seq)
    row = lambda w: pl.BlockSpec((tm, w), lambda i: (i, 0))
    full = lambda a, b: pl.BlockSpec((a, b), lambda i: (0, 0))
    wo = w_out.astype(BF16)
    return pl.pallas_call(
        _out_kernel,
        grid=(t // tm,),
        in_specs=[row(D_MODEL), row(RET_WIDTH), row(S5_WIDTH), row(HY_WIDTH),
                  pl.BlockSpec((1, 6, D_MODEL), lambda i: (grp(i), 0, 0)),
                  full(1, D_MODEL), full(S5_WIDTH, S5_WIDTH), full(1, S5_WIDTH),
                  full(RET_WIDTH, D_MODEL), full(S5_WIDTH, D_MODEL), full(HY_WIDTH, D_MODEL),
                  full(N_EXPERTS, D_MODEL)],
        out_specs=[row(D_MODEL), pl.BlockSpec((tm * TOKEN_ROWS, LANES), lambda i: (i, 0)),
                   pl.BlockSpec((N_EXPERTS, tm), lambda i: (0, i))],
        out_shape=[jax.ShapeDtypeStruct((t, D_MODEL), F32), jax.ShapeDtypeStruct((t * TOKEN_ROWS, LANES), jnp.uint32),
                   jax.ShapeDtypeStruct((N_EXPERTS, t), F32)],
        compiler_params=_cparams(("parallel",)),
        name="out_proj",
    )(x, ret_o, s5_y, hy_o, mod, g.reshape(1, D_MODEL), glu_w.astype(BF16), glu_b.reshape(1, S5_WIDTH),
      wo[:RET_WIDTH], wo[RET_WIDTH:RET_WIDTH + S5_WIDTH], wo[RET_WIDTH + S5_WIDTH:], router.T.astype(BF16))


def _moe_kernel(be_ref, first_ref, slot_ref, nxt_ref, nu_ref, xs_ref, wg_hbm, wu_hbm, wd_hbm, o_ref,
                wg_f, wu_f, wd_f, wg_b, wu_b, wd_b, sem, *, layer):
    i = pl.program_id(0)

    def copies(e, s):
        return (pltpu.make_async_copy(wg_hbm.at[layer, e], wg_f.at[s], sem.at[s, 0]),
                pltpu.make_async_copy(wu_hbm.at[layer, e], wu_f.at[s], sem.at[s, 1]),
                pltpu.make_async_copy(wd_hbm.at[layer, e], wd_f.at[s], sem.at[s, 2]))

    @pl.when(i == 0)
    def _():
        for cp in copies(be_ref[0], 0):
            cp.start()

    @pl.when(first_ref[i] == 1)
    def _():
        s = slot_ref[i]
        for cp in copies(be_ref[i], s):
            cp.wait()

        @pl.when(nxt_ref[i] >= 0)
        def _():
            for cp in copies(nxt_ref[i], 1 - s):
                cp.start()

        wg_b[...] = wg_f[s].astype(BF16)
        wu_b[...] = wu_f[s].astype(BF16)
        wd_b[...] = wd_f[s].astype(BF16)

    @pl.when(i < nu_ref[0])
    def _():
        half = D_MODEL // 2
        x_lo, x_hi = _unpack_halves(_load_token_tiles(xs_ref, MOE_BM))
        x_lo = x_lo.astype(BF16)
        x_hi = x_hi.astype(BF16)
        gate = _dot(x_lo, wg_b[0:half, :]) + _dot(x_hi, wg_b[half:, :])
        up = _dot(x_lo, wu_b[0:half, :]) + _dot(x_hi, wu_b[half:, :])
        hb = gate * jax.nn.sigmoid(gate) * up
        _store_token_tiles(o_ref, _pack_halves(_dot(hb.astype(BF16), wd_b[...]).astype(BF16)))

    @pl.when(i >= nu_ref[0])
    def _():
        o_ref[...] = jnp.zeros_like(o_ref)


def _moe_grouped(xs, blk_e, first, slot, nxt, n_used, w_gate, w_up, w_down, layer):
    pr = xs.shape[0] // TOKEN_ROWS
    bm = MOE_BM
    nb = pr // bm
    grid_spec = pltpu.PrefetchScalarGridSpec(
        num_scalar_prefetch=5,
        grid=(nb,),
        in_specs=[
            pl.BlockSpec((bm * TOKEN_ROWS, LANES), lambda i, *_: (i, 0)),
            pl.BlockSpec(memory_space=pl.ANY),
            pl.BlockSpec(memory_space=pl.ANY),
            pl.BlockSpec(memory_space=pl.ANY),
        ],
        out_specs=pl.BlockSpec((bm * TOKEN_ROWS, LANES), lambda i, *_: (i, 0)),
        scratch_shapes=[pltpu.VMEM((2, D_MODEL, D_EXPERT), F32), pltpu.VMEM((2, D_MODEL, D_EXPERT), F32),
                        pltpu.VMEM((2, D_EXPERT, D_MODEL), F32),
                        pltpu.VMEM((D_MODEL, D_EXPERT), BF16), pltpu.VMEM((D_MODEL, D_EXPERT), BF16),
                        pltpu.VMEM((D_EXPERT, D_MODEL), BF16),
                        pltpu.SemaphoreType.DMA((2, 3))],
    )
    return pl.pallas_call(
        functools.partial(_moe_kernel, layer=layer),
        grid_spec=grid_spec,
        out_shape=jax.ShapeDtypeStruct((pr * TOKEN_ROWS, LANES), jnp.uint32),
        compiler_params=_cparams(("arbitrary",)),
        name="moe_grouped",
    )(blk_e, first, slot, nxt, n_used, xs, w_gate, w_up, w_down)


DISPATCH_TILE = 512


def _dispatch_kernel(info_ref, nu_ref, pos_ref, h_ref, xs_out, zbuf, sem, zsem, *, nb):
    tm = pos_ref.shape[1]
    tr = TOKEN_ROWS
    bm = MOE_BM

    @pl.when(pl.program_id(0) == 0)
    def _():
        zbuf[...] = jnp.zeros_like(zbuf)

        def zero_block(first_row):
            rows = pl.ds(pl.multiple_of(first_row * tr, tr), bm * tr)
            return pltpu.make_async_copy(zbuf, xs_out.at[rows], zsem)

        for act in ("start", "wait"):
            def last_of_expert(e, carry, act=act):
                @pl.when(info_ref[e, 3] > info_ref[e, 1])
                def _():
                    getattr(zero_block(info_ref[e, 3] - bm), act)()
                return carry

            def tail_block(b, carry, act=act):
                getattr(zero_block(b * bm), act)()
                return carry

            lax.fori_loop(0, N_EXPERTS, last_of_expert, 0)
            lax.fori_loop(nu_ref[0], nb, tail_block, 0)

    def send(tok, carry):
        src = h_ref.at[pl.ds(pl.multiple_of(tok * tr, tr), tr)]
        for k in range(TOP_K):
            row = pl.multiple_of(pos_ref[k, tok] * tr, tr)
            pltpu.make_async_copy(src, xs_out.at[pl.ds(row, tr)], sem).start(priority=k % 2)
        return carry

    lax.fori_loop(0, tm, send, 0, unroll=4)
    n = tm * TOP_K * tr
    pltpu.make_async_copy(xs_out.at[pl.ds(0, n)], xs_out.at[pl.ds(0, n)], sem).wait()


def _dispatch(h2, pos, info, n_used, nb):
    t = pos.shape[1]
    tm = DISPATCH_TILE
    return pl.pallas_call(
        functools.partial(_dispatch_kernel, nb=nb),
        grid=(t // tm,),
        in_specs=[pl.BlockSpec(memory_space=pltpu.SMEM),
                  pl.BlockSpec(memory_space=pltpu.SMEM),
                  pl.BlockSpec((SUBLANES, tm), lambda i: (0, i), memory_space=pltpu.SMEM),
                  pl.BlockSpec((tm * TOKEN_ROWS, LANES), lambda i: (i, 0))],
        out_specs=pl.BlockSpec(memory_space=pl.ANY),
        out_shape=jax.ShapeDtypeStruct((nb * MOE_BM * TOKEN_ROWS, LANES), h2.dtype),
        scratch_shapes=[pltpu.VMEM((MOE_BM * TOKEN_ROWS, LANES), h2.dtype),
                        pltpu.SemaphoreType.DMA(()), pltpu.SemaphoreType.DMA(())],
        compiler_params=pltpu.CompilerParams(dimension_semantics=("arbitrary",)),
        name="dispatch",
    )(info, n_used, pos, h2)


ROUTE_TILE = 512


def _router_kernel(lg_ref, bias_ref, pos_ref, gate_ref, be_ref, info_ref, rank_scr, ek_scr, *, t, nbp):
    tl = ROUTE_TILE
    ne = N_EXPERTS
    bm = MOE_BM
    row = lax.broadcasted_iota(jnp.int32, (ne, tl), 0)
    tri = (lax.broadcasted_iota(jnp.int32, (tl, tl), 0) < lax.broadcasted_iota(jnp.int32, (tl, tl), 1)).astype(BF16)
    bias = bias_ref[...]

    def select(i, counts):
        cols = pl.ds(pl.multiple_of(i * tl, tl), tl)
        s = jax.nn.sigmoid(lg_ref[:, cols])
        sel = s + bias
        mask = jnp.zeros((ne, tl), F32)
        vals = []
        for k in range(TOP_K):
            best = jnp.max(sel, axis=0, keepdims=True)
            idx = jnp.min(jnp.where(sel == best, row, ne), axis=0, keepdims=True)
            hit = row == idx
            vals.append(jnp.sum(jnp.where(hit, s, 0.0), axis=0, keepdims=True))
            sel = jnp.where(hit, -jnp.inf, sel)
            mask = jnp.where(hit, 1.0, mask)
            ek_scr[k:k + 1, cols] = idx
        total = vals[0]
        for v in vals[1:]:
            total = total + v
        scale = ROUTED_SCALE / total
        for k in range(TOP_K):
            gate_ref[k:k + 1, cols] = vals[k] * scale
        gate_ref[TOP_K:SUBLANES, cols] = jnp.zeros((SUBLANES - TOP_K, tl), F32)
        rank_scr[:, cols] = _dot(mask.astype(BF16), tri) + counts
        return counts + jnp.sum(mask, axis=1, keepdims=True)

    counts = lax.fori_loop(0, t // tl, select, jnp.zeros((ne, 1), F32))
    counts = counts.astype(jnp.int32)
    shift = bm.bit_length() - 1
    assert bm == 1 << shift
    padded = ((counts + (bm - 1)) >> shift) << shift
    e0 = lax.broadcasted_iota(jnp.int32, (ne, ne), 0)
    e1 = lax.broadcasted_iota(jnp.int32, (ne, ne), 1)
    padded_row = jnp.sum(jnp.where(e0 == e1, padded, 0), axis=0, keepdims=True)
    counts_row = jnp.sum(jnp.where(e0 == e1, counts, 0), axis=0, keepdims=True)
    pstart = jnp.sum(jnp.where(e1 < e0, padded_row, 0), axis=1, keepdims=True)
    ustart = jnp.sum(jnp.where(e1 < e0, counts_row, 0), axis=1, keepdims=True)
    pend = pstart + padded
    lane = lax.broadcasted_iota(jnp.int32, (ne, LANES), 1)
    info_ref[...] = jnp.where(lane == 0, counts, jnp.where(lane == 1, pstart, jnp.where(lane == 2, ustart, pend)))
    blk = lax.broadcasted_iota(jnp.int32, (ne, nbp), 1) * bm
    owner = jnp.sum(jnp.where(pend <= blk, 1, 0), axis=0, keepdims=True)
    be_ref[...] = jnp.minimum(owner, ne - 1)
    pstart_f = pstart.astype(F32)

    def place(i, carry):
        cols = pl.ds(pl.multiple_of(i * tl, tl), tl)
        dest = rank_scr[:, cols] + pstart_f
        for k in range(TOP_K):
            hit = row == ek_scr[k:k + 1, cols]
            pos_ref[k:k + 1, cols] = jnp.sum(jnp.where(hit, dest, 0.0), axis=0, keepdims=True).astype(jnp.int32)
        pos_ref[TOP_K:SUBLANES, cols] = jnp.zeros((SUBLANES - TOP_K, tl), jnp.int32)
        return carry

    lax.fori_loop(0, t // tl, place, 0)


def _router(logits_t, router_bias, nb):
    t = logits_t.shape[1]
    nbp = -(-nb // LANES) * LANES
    return pl.pallas_call(
        functools.partial(_router_kernel, t=t, nbp=nbp),
        out_shape=[jax.ShapeDtypeStruct((SUBLANES, t), jnp.int32), jax.ShapeDtypeStruct((SUBLANES, t), F32),
                   jax.ShapeDtypeStruct((1, nbp), jnp.int32), jax.ShapeDtypeStruct((N_EXPERTS, LANES), jnp.int32)],
        scratch_shapes=[pltpu.VMEM((N_EXPERTS, t), F32), pltpu.VMEM((SUBLANES, t), jnp.int32)],
        compiler_params=pltpu.CompilerParams(vmem_limit_bytes=VMEM_LIMIT),
        name="router",
    )(logits_t, router_bias.astype(F32).reshape(N_EXPERTS, 1))


def _dispatch_plan(blk_e_row, info, nb):
    bm = MOE_BM
    pend = info[:, 3]
    blk_e = blk_e_row[0, :nb]
    n_used = pend[-1] // bm
    blk = jnp.arange(nb, dtype=jnp.int32)
    prev_e = jnp.concatenate([jnp.full((1,), -1, jnp.int32), blk_e[:-1]])
    first = jnp.logical_and(blk < n_used, blk_e != prev_e)
    slot = (jnp.cumsum(first.astype(jnp.int32)) - 1) % 2
    first_at = jnp.where(first, blk, nb)
    nxt_first = lax.cummin(jnp.concatenate([first_at[1:], jnp.full((1,), nb, jnp.int32)]), reverse=True)
    nxt = jnp.where(nxt_first < nb, blk_e[jnp.minimum(nxt_first, nb - 1)], -1)
    return (blk_e, first.astype(jnp.int32), slot.astype(jnp.int32), nxt.astype(jnp.int32),
            n_used.astype(jnp.int32).reshape(1))


def _shared_kernel(pos_ref, nxt_ref, x_ref, h_ref, gt_ref, mod_ref, sg_ref, su_ref, sd_ref, fn_ref, eo_hbm,
                   o_ref, buf0, buf1, acc_scr, sem, *, final):
    i = pl.program_id(0)
    n = pl.num_programs(0)
    tm = x_ref.shape[0]
    tr = TOKEN_ROWS

    half = D_MODEL // 2
    grp = SUBLANES

    def fetch_tokens(idx_ref, buf, slot, tok0):
        for tt in range(grp):
            tok = tok0 + tt
            for k in range(TOP_K):
                row = pl.multiple_of(idx_ref[k, tok] * tr, tr)
                pltpu.make_async_copy(eo_hbm.at[pl.ds(row, tr)], buf.at[k, pl.ds(pl.multiple_of(tok * tr, tr), tr)],
                                      sem.at[slot]).start(priority=k % 2)

    def arrived(buf, slot):
        for k in range(TOP_K):
            pltpu.make_async_copy(eo_hbm.at[pl.ds(0, tm * tr)], buf.at[k], sem.at[slot]).wait()

    def step(cur, cur_slot, nxt, nxt_slot):
        arrived(cur, cur_slot)
        h_lo, h_hi = _unpack_halves(_load_token_tiles(h_ref, tm))
        h_lo = h_lo.astype(BF16)
        h_hi = h_hi.astype(BF16)
        gate = _dot(h_lo, sg_ref[0:half, :]) + _dot(h_hi, sg_ref[half:, :])
        up = _dot(h_lo, su_ref[0:half, :]) + _dot(h_hi, su_ref[half:, :])
        act = gate * jax.nn.sigmoid(gate) * up
        acc_scr[...] = _dot(act.astype(BF16), sd_ref[...])

        def group(c, carry):
            tok0 = pl.multiple_of(c * grp, grp)
            fetch_tokens(nxt_ref, nxt, nxt_slot, tok0)
            gt = gt_ref[pl.ds(tok0, grp), :]
            r_lo = jnp.zeros((grp, half), F32)
            r_hi = r_lo
            for k in range(TOP_K):
                words = jnp.concatenate(
                    [cur[k, pl.ds(tok0 * tr + s, grp, stride=tr), :] for s in range(tr)], axis=1)
                e_lo, e_hi = _unpack_halves(words)
                r_lo = r_lo + gt[:, k:k + 1] * e_lo
                r_hi = r_hi + gt[:, k:k + 1] * e_hi
            acc_scr[pl.ds(tok0, grp), :] += jnp.concatenate([r_lo, r_hi], axis=1)
            return carry

        lax.fori_loop(0, tm // grp, group, 0)
        x = x_ref[...] + mod_ref[0][5:6] * acc_scr[...]
        if final:
            x = _rms(x, fn_ref[...])
        o_ref[...] = x

        @pl.when(i == n - 1)
        def _():
            arrived(nxt, nxt_slot)

    @pl.when(i == 0)
    def _():
        def first(c, carry):
            fetch_tokens(pos_ref, buf0, 0, pl.multiple_of(c * grp, grp))
            return carry
        lax.fori_loop(0, tm // grp, first, 0)

    for parity, cur, nxt in ((0, buf0, buf1), (1, buf1, buf0)):
        @pl.when(i % 2 == parity)
        def _(parity=parity, cur=cur, nxt=nxt):
            step(cur, parity, nxt, 1 - parity)


def _shared(x, h, gates_t, pos, eo, mod, sg, su, sd, final_norm, n_ctx_rows, dec_seq, *, final, row0=0, rows=None):
    rows = x.shape[0] if rows is None else rows
    tm = ROW_TILE // 2
    b0 = row0 // tm
    steps = rows // tm
    grp = lambda i: _group_of(i + b0, tm, n_ctx_rows, dec_seq)
    row = pl.BlockSpec((tm, D_MODEL), lambda i: (i + b0, 0))
    prow = pl.BlockSpec((tm * TOKEN_ROWS, LANES), lambda i: (i + b0, 0))
    full = lambda a, b: pl.BlockSpec((a, b), lambda i: (0, 0))
    buf = pltpu.VMEM((TOP_K, tm * TOKEN_ROWS, LANES), eo.dtype)
    return pl.pallas_call(
        functools.partial(_shared_kernel, final=final),
        grid=(steps,),
        in_specs=[pl.BlockSpec((SUBLANES, tm), lambda i: (0, i + b0), memory_space=pltpu.SMEM),
                  pl.BlockSpec((SUBLANES, tm), lambda i: (0, jnp.minimum(i + 1, steps - 1) + b0),
                               memory_space=pltpu.SMEM),
                  row, prow, pl.BlockSpec((tm, SUBLANES), lambda i: (i + b0, 0)),
                  pl.BlockSpec((1, 6, D_MODEL), lambda i: (grp(i), 0, 0)),
                  full(D_MODEL, D_SHARED), full(D_MODEL, D_SHARED), full(D_SHARED, D_MODEL), full(1, D_MODEL),
                  pl.BlockSpec(memory_space=pl.ANY)],
        out_specs=pl.BlockSpec((tm, D_MODEL), lambda i: (i, 0)),
        out_shape=jax.ShapeDtypeStruct((rows, D_MODEL), F32),
        scratch_shapes=[buf, buf, pltpu.VMEM((tm, D_MODEL), F32), pltpu.SemaphoreType.DMA((2,))],
        compiler_params=_cparams(("arbitrary",)),
        name="shared_final" if final else "shared",
    )(pos, pos, x, h, gates_t, mod, sg.astype(BF16), su.astype(BF16), sd.astype(BF16),
      final_norm.reshape(1, D_MODEL), eo)


def kernel(x_prompt, x_sample, state_ret, state_s5_re, state_s5_im, c, c_ctx, w_ada, b_ada, norm_mix, norm_ffn, w_in, w_out, ret_decay, s5_lam_re, s5_lam_im, s5_log_dt, s5_b_re, s5_b_im, s5_c_re, s5_c_im, s5_d, s5_glu_w, s5_glu_b, hy_conv_w, hy_conv_b, hy_f1_w, hy_f1_b, hy_f2_w, hy_f2_b, hy_f3_w, hy_f3_b, hy_freq, hy_decay, hy_bias, moe_router, moe_router_bias, moe_w_gate, moe_w_up, moe_w_down, sh_w_gate, sh_w_up, sh_w_down, final_norm):
    n_ctx, seq, d = x_prompt.shape
    n_dec, dec_seq, _ = x_sample.shape
    n_ctx_rows = n_ctx * seq
    t = n_ctx_rows + n_dec * dec_seq

    x = jnp.concatenate([x_prompt.reshape(n_ctx_rows, d), x_sample.reshape(n_dec * dec_seq, d)], axis=0)
    cond = jnp.concatenate([c_ctx[None, :], c], axis=0)
    cond8 = jnp.pad(cond, ((0, SUBLANES - cond.shape[0]), (0, 0)))
    mods = _ada(cond8, w_ada, b_ada)[:, :1 + n_dec].reshape(DEPTH, 1 + n_dec, 6, d)

    cos2, sin2 = _rope_tables(dec_seq)
    no_rope = jnp.zeros((seq, LANES), F32)
    zero_ret = jnp.zeros((n_ctx, 2, RET_HEADS, RET_DK, RET_DV), F32)
    zero_s5 = jnp.zeros((n_ctx, 2, S5_GROUPS, S5_STATE), F32)
    dft_ctx = _dft_mats(seq)
    dft_dec = _dft_mats(dec_seq)

    w_in_bf = w_in.astype(BF16)
    mats = _s5_mats(s5_lam_re, s5_lam_im, s5_log_dt, s5_b_re, s5_b_im, s5_c_re, s5_c_im, s5_d)
    filt = (hy_f1_w, hy_f1_b, hy_f2_w, hy_f2_b, hy_f3_w, hy_f3_b, hy_freq, hy_decay)
    mults_ctx = _hy_spectrum(jax.vmap(functools.partial(_hy_filter_taps, seq))(*filt), dft_ctx)
    mults_dec = _hy_spectrum(jax.vmap(functools.partial(_hy_filter_taps, dec_seq))(*filt), dft_dec)

    ret_states = jnp.zeros((n_ctx, DEPTH, 2, RET_HEADS, RET_DK, RET_DV), F32)
    s5r_list, s5i_list = [], []
    for l in range(DEPTH):
        mod = mods[l]
        proj = _in_proj(x, mod, norm_mix[l], w_in_bf, l, n_ctx_rows, dec_seq)

        log_gamma = jax.nn.log_sigmoid(ret_decay[l].astype(F32))
        ret_o, ret_states = _retention(proj, log_gamma, zero_ret, no_rope, no_rope,
                                       row0=0, n_seq=n_ctx, seq_len=seq, hb=RET_HEADS, rope=False,
                                       dst=jnp.zeros((t, RET_WIDTH), BF16), states=ret_states, layer=l)
        ret_o, _ = _retention(proj, log_gamma, state_ret[:, l].astype(F32), cos2, sin2,
                              row0=n_ctx_rows, n_seq=n_dec, seq_len=dec_seq, hb=2, rope=True, dst=ret_o)

        s5_y, s5_re, s5_im = _s5(proj, mats, zero_s5, zero_s5, row0=0, n_seq=n_ctx, seq_len=seq, layer=l,
                                 dst=jnp.zeros((t, S5_WIDTH), F32))
        s5_y, _, _ = _s5(proj, mats, state_s5_re[:, l], state_s5_im[:, l],
                         row0=n_ctx_rows, n_seq=n_dec, seq_len=dec_seq, layer=l, dst=s5_y)
        s5r_list.append(s5_re)
        s5i_list.append(s5_im)

        hy_o = _hyena(proj, hy_conv_w[l], hy_conv_b[l], hy_bias[l], dft_ctx, mults_ctx, layer=l,
                      row0=0, n_seq=n_ctx, seq_len=seq, cb=HY_WIDTH, tk=seq, dst=jnp.zeros((t, HY_WIDTH), BF16))
        hy_o = _hyena(proj, hy_conv_w[l], hy_conv_b[l], hy_bias[l], dft_dec, mults_dec, layer=l,
                      row0=n_ctx_rows, n_seq=n_dec, seq_len=dec_seq, cb=HY_WIDTH // 2, tk=512, dst=hy_o)

        x, h2, logits = _out_proj(x, ret_o, s5_y, hy_o, mod, norm_ffn[l], s5_glu_w[l], s5_glu_b[l],
                                  w_out[l], moe_router[l], n_ctx_rows, dec_seq)

        nb = -(-(t * TOP_K) // MOE_BM) + N_EXPERTS
        pos, gates, blk_e_row, info = _router(logits, moe_router_bias[l], nb)
        blk_e, first, slot, nxt, n_used = _dispatch_plan(blk_e_row, info, nb)
        xs = _dispatch(h2, pos, info, n_used, nb)
        eo = _moe_grouped(xs, blk_e, first, slot, nxt, n_used, moe_w_gate, moe_w_up, moe_w_down, l)
        gates_t = gates.T

        sh = (sh_w_gate[l], sh_w_up[l], sh_w_down[l])
        if l < DEPTH - 1:
            x = _shared(x, h2, gates_t, pos, eo, mod, *sh, final_norm, n_ctx_rows, dec_seq, final=False)
        else:
            y_c = _shared(x, h2, gates_t, pos, eo, mod, *sh, final_norm, n_ctx_rows, dec_seq, final=True,
                          row0=0, rows=n_ctx_rows)
            y_d = _shared(x, h2, gates_t, pos, eo, mod, *sh, final_norm, n_ctx_rows, dec_seq, final=True,
                          row0=n_ctx_rows, rows=n_dec * dec_seq)

    return (y_c.reshape(n_ctx, seq, d), y_d.reshape(n_dec, dec_seq, d),
            ret_states, jnp.stack(s5r_list, axis=1), jnp.stack(s5i_list, axis=1))
```

```python
import functools
import math

import jax
import jax.numpy as jnp
from jax import lax
from jax.experimental import pallas as pl
from jax.experimental.pallas import tpu as pltpu

F32 = jnp.float32
BF16 = jnp.bfloat16

D_MODEL = 2048
DEPTH = 2
GRID_W = 64
RET_HEADS = 8
RET_DK = 128
RET_DV = 128
RET_WIDTH = RET_HEADS * RET_DV
RET_CHUNK = 256
ROPE_BASE = 10000.0
S5_WIDTH = 512
S5_GROUP = 16
S5_GROUPS = S5_WIDTH // S5_GROUP
S5_STATE = 64
S5_Q = 16
HY_WIDTH = 512
HY_BANDS = 16
IN_WIDTH = 4 * RET_WIDTH + S5_WIDTH + 3 * HY_WIDTH
U_COL = 4 * RET_WIDTH
HY_COL = U_COL + S5_WIDTH
N_EXPERTS = 64
TOP_K = 6
D_EXPERT = 512
D_SHARED = 512
ROUTED_SCALE = 2.5
EPS = 1e-6

LANES = 128
SUBLANES = 8
VMEM_LIMIT = 56 * 1024 * 1024

ROW_TILE = 512
MOE_BM = 256


def _cparams(sem):
    return pltpu.CompilerParams(dimension_semantics=sem, vmem_limit_bytes=VMEM_LIMIT)


def _dot(a, b):
    return jnp.dot(a, b, preferred_element_type=F32)


def _rms(x, g):
    var = jnp.mean(x * x, axis=-1, keepdims=True)
    return x * lax.rsqrt(var + EPS) * g


def _pack_halves(xb):
    n = xb.shape[1] // 2
    lo = lax.bitcast_convert_type(xb[:, :n].astype(F32), jnp.uint32) >> 16
    hi = lax.bitcast_convert_type(xb[:, n:].astype(F32), jnp.uint32)
    return lo | hi


def _unpack_halves(w):
    lo = lax.bitcast_convert_type(w << 16, F32)
    hi = lax.bitcast_convert_type(w & jnp.uint32(0xFFFF0000), F32)
    return lo, hi


TOKEN_ROWS = D_MODEL // 2 // LANES


def _store_token_tiles(ref, w):
    m = w.shape[0]
    for s in range(TOKEN_ROWS):
        ref[pl.ds(s, m, stride=TOKEN_ROWS), :] = w[:, s * LANES:(s + 1) * LANES]


def _load_token_tiles(ref, m):
    return jnp.concatenate([ref[pl.ds(s, m, stride=TOKEN_ROWS), :] for s in range(TOKEN_ROWS)], axis=1)


def _ada_kernel(c_ref, w_ref, b_ref, o_ref):
    c = c_ref[...]
    s = (c * jax.nn.sigmoid(c)).astype(BF16)
    o_ref[0] = _dot(s, w_ref[0].astype(BF16)) + b_ref[0]


def _ada(cond8, w_ada, b_ada):
    tn = 512
    n = w_ada.shape[-1]
    return pl.pallas_call(
        _ada_kernel,
        grid=(DEPTH, n // tn),
        in_specs=[
            pl.BlockSpec((SUBLANES, D_MODEL), lambda l, j: (0, 0)),
            pl.BlockSpec((1, D_MODEL, tn), lambda l, j: (l, 0, j)),
            pl.BlockSpec((1, 1, tn), lambda l, j: (l, 0, j)),
        ],
        out_specs=pl.BlockSpec((1, SUBLANES, tn), lambda l, j: (l, 0, j)),
        out_shape=jax.ShapeDtypeStruct((DEPTH, SUBLANES, n), F32),
        compiler_params=_cparams(("parallel", "parallel")),
        name="ada",
    )(cond8, w_ada, b_ada.reshape(DEPTH, 1, n))


def _group_of(i, tile, n_ctx_rows, dec_seq):
    ctx_tiles = n_ctx_rows // tile
    per = dec_seq // tile
    return jnp.where(i < ctx_tiles, 0, 1 + (i - ctx_tiles) // per)


def _in_kernel(x_ref, mod_ref, g_ref, w_ref, o_ref, h_scr):
    @pl.when(pl.program_id(1) == 0)
    def _():
        m = mod_ref[0]
        h = _rms(x_ref[...], g_ref[...]) * (1.0 + m[1:2]) + m[0:1]
        h_scr[...] = h.astype(BF16)

    o_ref[...] = _dot(h_scr[...], w_ref[...])


def _in_proj(x, mod, g, w_bf, layer, n_ctx_rows, dec_seq):
    t = x.shape[0]
    tm, tn = 1024, 1024
    grp = functools.partial(_group_of, tile=tm, n_ctx_rows=n_ctx_rows, dec_seq=dec_seq)
    return pl.pallas_call(
        _in_kernel,
        grid=(t // tm, IN_WIDTH // tn),
        in_specs=[
            pl.BlockSpec((tm, D_MODEL), lambda i, j: (i, 0)),
            pl.BlockSpec((1, 6, D_MODEL), lambda i, j: (grp(i), 0, 0)),
            pl.BlockSpec((1, D_MODEL), lambda i, j: (0, 0)),
            pl.BlockSpec((None, D_MODEL, tn), lambda i, j: (layer, 0, j)),
        ],
        out_specs=pl.BlockSpec((tm, tn), lambda i, j: (i, j)),
        out_shape=jax.ShapeDtypeStruct((t, IN_WIDTH), F32),
        scratch_shapes=[pltpu.VMEM((tm, D_MODEL), BF16)],
        compiler_params=_cparams(("parallel", "arbitrary")),
        name="in_proj",
    )(x, mod, g.reshape(1, D_MODEL), w_bf)


def _ret_kernel(lg_ref, q_ref, k_ref, v_ref, gt_ref, cos_ref, sin_ref, s0_ref,
                o_ref, sfin_ref, acc_scr, q_scr, k_scr, *, seq_len, hb, rope):
    c = RET_CHUNK
    n_chunks = seq_len // c
    ii = lax.broadcasted_iota(jnp.int32, (c, c), 0)
    jj = lax.broadcasted_iota(jnp.int32, (c, c), 1)
    rel = (ii - jj).astype(F32)
    ci = lax.broadcasted_iota(jnp.int32, (c, 1), 0).astype(F32)
    one = jnp.ones((1, 1), F32)
    tdot = functools.partial(lax.dot_general, preferred_element_type=F32)

    def make_head(hh):
        head = pl.program_id(1) * hb + hh
        lgf = lg_ref[0, head]
        lgb = lg_ref[1, head]
        dmask = (jnp.where(rel >= 0, jnp.exp(lgf * jnp.maximum(rel, 0.0)), 0.0)
                 + jnp.where(rel <= 0, jnp.exp(lgb * jnp.maximum(-rel, 0.0)), 0.0))
        qd_f = jnp.exp(lgf * (ci + 1.0))
        kd_f = jnp.exp(lgf * (c - 1.0 - ci))
        cd_f = jnp.exp(lgf * c * one)
        qd_b = jnp.exp(lgb * (c - ci))
        kd_b = jnp.exp(lgb * ci)
        cd_b = jnp.exp(lgb * c * one)
        lanes = slice(hh * LANES, (hh + 1) * LANES)

        def rows_of(n):
            if isinstance(n, int):
                return slice(n * c, (n + 1) * c)
            return pl.ds(pl.multiple_of(n * c, c), c)

        def fwd_chunk(n, s_f):
            rows = rows_of(n)
            q = q_ref[rows, lanes]
            k = k_ref[rows, lanes] * (RET_DK ** -0.5)
            if rope:
                cs = cos_ref[rows, :]
                sn = sin_ref[rows, :]
                q = q * cs + pltpu.roll(q, RET_DK // 2, 1) * sn
                k = k * cs + pltpu.roll(k, RET_DK // 2, 1) * sn
            qb = q.astype(BF16)
            vb = v_ref[rows, lanes].astype(BF16)
            q_scr[rows, lanes] = qb
            k_scr[rows, lanes] = k
            scores = tdot(qb, k.astype(BF16), (((1,), (1,)), ((), ()))) * dmask
            inner = _dot(scores.astype(BF16), vb)
            cross = _dot(qb, s_f.astype(BF16)) * qd_f
            acc_scr[rows, lanes] = inner + cross
            upd = tdot((k * kd_f).astype(BF16), vb, (((0,), (0,)), ((), ())))
            return s_f * cd_f + upd

        def bwd_chunk(m, s_b):
            n = n_chunks - 1 - m
            rows = rows_of(n)
            qb = q_scr[rows, lanes]
            k = k_scr[rows, lanes]
            vb = v_ref[rows, lanes].astype(BF16)
            o = acc_scr[rows, lanes] + _dot(qb, s_b.astype(BF16)) * qd_b
            mu = jnp.mean(o, axis=-1, keepdims=True)
            oc = o - mu
            var = jnp.mean(oc * oc, axis=-1, keepdims=True)
            o = oc * lax.rsqrt(var + EPS)
            g = gt_ref[rows, lanes]
            o_ref[rows, lanes] = (g * jax.nn.sigmoid(g) * o).astype(o_ref.dtype)
            upd = tdot((k * kd_b).astype(BF16), vb, (((0,), (0,)), ((), ())))
            return s_b * cd_b + upd

        return fwd_chunk, bwd_chunk

    if n_chunks <= 4:
        for hh in range(hb):
            fwd_chunk, bwd_chunk = make_head(hh)
            s_f = s0_ref[0, 0, hh]
            s_b = s0_ref[0, 1, hh]
            for n in range(n_chunks):
                s_f = fwd_chunk(n, s_f)
            for m in range(n_chunks):
                s_b = bwd_chunk(m, s_b)
            sfin_ref[0, 0, hh] = s_f
            sfin_ref[0, 1, hh] = s_b
    else:
        fns = [make_head(hh) for hh in range(hb)]
        s_f = lax.fori_loop(0, n_chunks, lambda n, ss: tuple(f[0](n, s) for f, s in zip(fns, ss)),
                            tuple(s0_ref[0, 0, hh] for hh in range(hb)))
        s_b = lax.fori_loop(0, n_chunks, lambda m, ss: tuple(f[1](m, s) for f, s in zip(fns, ss)),
                            tuple(s0_ref[0, 1, hh] for hh in range(hb)))
        for hh in range(hb):
            sfin_ref[0, 0, hh] = s_f[hh]
            sfin_ref[0, 1, hh] = s_b[hh]


def _into(kernel_fn, n_in, dst):
    dsts = [d for d in (dst if isinstance(dst, (list, tuple)) else [dst])]
    outs = [k for k, d in enumerate(dsts) if d is not None]
    if not outs:
        return kernel_fn, [], [], {}

    def body(*refs):
        return kernel_fn(*refs[:n_in], *refs[n_in + len(outs):])

    return (body, [pl.BlockSpec(memory_space=pl.ANY)] * len(outs), [dsts[k] for k in outs],
            {n_in + pos: k for pos, k in enumerate(outs)})


def _retention(proj, log_gamma, s0, cos2, sin2, *, row0, n_seq, seq_len, hb, rope, dst=None,
               states=None, layer=0):
    blk0 = row0 // seq_len
    body, dst_spec, dst_arg, alias = _into(
        functools.partial(_ret_kernel, seq_len=seq_len, hb=hb, rope=rope), 8, [dst, states])
    if states is None:
        st_spec = pl.BlockSpec((1, 2, hb, RET_DK, RET_DV), lambda b, h, lg: (b, 0, h, 0, 0))
        st_shape = jax.ShapeDtypeStruct((n_seq, 2, RET_HEADS, RET_DK, RET_DV), F32)
    else:
        st_spec = pl.BlockSpec((1, None, 2, hb, RET_DK, RET_DV), lambda b, h, lg: (b, layer, 0, h, 0, 0))
        st_shape = jax.ShapeDtypeStruct(states.shape, F32)
    w = hb * LANES
    hblocks = RET_HEADS // hb
    col = lambda part: (lambda b, h, lg: (blk0 + b, part * hblocks + h))
    grid_spec = pltpu.PrefetchScalarGridSpec(
        num_scalar_prefetch=1,
        grid=(n_seq, hblocks),
        in_specs=[
            pl.BlockSpec((seq_len, w), col(0)),
            pl.BlockSpec((seq_len, w), col(1)),
            pl.BlockSpec((seq_len, w), col(2)),
            pl.BlockSpec((seq_len, w), col(3)),
            pl.BlockSpec((seq_len, LANES), lambda b, h, lg: (0, 0)),
            pl.BlockSpec((seq_len, LANES), lambda b, h, lg: (0, 0)),
            pl.BlockSpec((1, 2, hb, RET_DK, RET_DV), lambda b, h, lg: (b, 0, h, 0, 0)),
        ] + dst_spec,
        out_specs=[
            pl.BlockSpec((seq_len, w), lambda b, h, lg: (blk0 + b, h)),
            st_spec,
        ],
        scratch_shapes=[
            pltpu.VMEM((seq_len, w), F32),
            pltpu.VMEM((seq_len, w), BF16),
            pltpu.VMEM((seq_len, w), F32),
        ],
    )
    return pl.pallas_call(
        body,
        grid_spec=grid_spec,
        out_shape=[jax.ShapeDtypeStruct((proj.shape[0], RET_WIDTH), BF16), st_shape],
        input_output_aliases=alias,
        compiler_params=_cparams(("parallel", "arbitrary")),
        name="retention",
    )(log_gamma, proj, proj, proj, proj, cos2, sin2, s0, *dst_arg)


def _rope_tables(seq_len):
    rows_n = seq_len // GRID_W
    rows = jnp.repeat(jnp.arange(rows_n, dtype=F32), GRID_W)
    cols = jnp.tile(jnp.arange(GRID_W, dtype=F32), rows_n)
    nf = RET_DK // 4
    inv = ROPE_BASE ** (-jnp.arange(nf, dtype=F32) / nf)
    ang = jnp.concatenate([rows[:, None] * inv, cols[:, None] * inv], axis=-1)
    cs, sn = jnp.cos(ang), jnp.sin(ang)
    return jnp.concatenate([cs, cs], axis=-1), jnp.concatenate([-sn, sn], axis=-1)


S5_GB = LANES // S5_GROUP
S5_W = S5_Q * LANES
S5_SPLIT = 4
S5_SW = S5_GB * S5_STATE
S5_SB = S5_SW // LANES


def _s5_kernel(u_ref, bq_ref, k_ref, cq_ref, ar_ref, ai_ref, d_ref, h0_ref,
               y_ref, hf_ref, ub_scr, sm_scr, hp_scr, hpb_scr, t_scr, *, n_seq, n_chunks):
    s = pl.program_id(1)
    m = n_seq * n_chunks
    q = S5_Q
    nblk = S5_W // LANES
    sb = S5_SB

    per = q // S5_SPLIT
    kw = S5_W // S5_SPLIT

    @pl.when(s == 0)
    def _():
        for j in range(q):
            ub_scr[j // per, :, (j % per) * LANES:(j % per + 1) * LANES] = (
                u_ref[pl.ds(j, m, stride=q), :].astype(BF16))
        for ib in range(S5_SPLIT):
            for j in range(q):
                c0 = (q - 1 - j) * LANES + ib * kw
                t_scr[ib, j * LANES:(j + 1) * LANES, :] = k_ref[0, :, c0:c0 + kw]

    @pl.when(s < S5_SPLIT)
    def _():
        part = _dot(ub_scr[jnp.minimum(s, S5_SPLIT - 1)], bq_ref[0])

        @pl.when(s == 0)
        def _():
            for cb in range(nblk):
                sm_scr[cb] = part[:, cb * LANES:(cb + 1) * LANES]

        @pl.when(s > 0)
        def _():
            for cb in range(nblk):
                sm_scr[cb] += part[:, cb * LANES:(cb + 1) * LANES]

    @pl.when(s == S5_SPLIT - 1)
    def _():
        ar = ar_ref[0]
        ai = ai_ref[0]
        h0 = h0_ref[0]
        blk = lambda a, cb: a[:, cb * LANES:(cb + 1) * LANES]

        def body(n, carry):
            rows_f = pl.ds(n, n_seq, stride=n_chunks)
            rows_b = pl.ds(n_chunks - 1 - n, n_seq, stride=n_chunks)
            new = list(carry)
            for d, rows in ((0, rows_f), (1, rows_b)):
                for c in range(sb):
                    re_i = d * sb + c
                    im_i = (2 + d) * sb + c
                    hr, hi = carry[re_i], carry[im_i]
                    hp_scr[re_i, rows, :] = hr
                    hp_scr[im_i, rows, :] = hi
                    a_r, a_i = blk(ar, re_i), blk(ai, re_i)
                    new[re_i] = a_r * hr - a_i * hi + sm_scr[re_i, rows, :]
                    new[im_i] = a_r * hi + a_i * hr + sm_scr[im_i, rows, :]
            return tuple(new)

        fin = lax.fori_loop(0, n_chunks, body, tuple(blk(h0, cb) for cb in range(nblk)))
        hf_ref[0] = jnp.concatenate(fin, axis=1)
        for cb in range(nblk):
            hpb_scr[:, cb * LANES:(cb + 1) * LANES] = hp_scr[cb].astype(BF16)

    @pl.when(s >= S5_SPLIT)
    def _():
        ub = jnp.concatenate([ub_scr[k] for k in range(S5_SPLIT)], axis=1)
        y = _dot(ub, t_scr[jnp.maximum(s - S5_SPLIT, 0)]) + _dot(hpb_scr[...], cq_ref[0])
        dd = d_ref[0]
        for ii in range(per):
            rows = pl.ds((s - S5_SPLIT) * per + ii, m, stride=q)
            yi = y[:, ii * LANES:(ii + 1) * LANES] + dd * u_ref[rows, :]
            y_ref[rows, :] = jax.nn.gelu(yi)


def _s5_expand_kernel(mc_ref, o_ref, *, xsize, ysize):
    xs, ys, gs = xsize.bit_length() - 1, ysize.bit_length() - 1, S5_GB.bit_length() - 1
    assert xsize == 1 << xs and ysize == 1 << ys and S5_GB == 1 << gs
    cw = o_ref.shape[2]
    nc = mc_ref.shape[2]
    col0 = pl.program_id(1) * cw
    r = lax.broadcasted_iota(jnp.int32, (nc, cw), 0)
    col = lax.broadcasted_iota(jnp.int32, (nc, cw), 1) + col0
    spread = jnp.logical_and(r >> ys == col >> (ys + gs), (r & (ysize - 1)) == (col & (ysize - 1)))
    big = _dot(mc_ref[0], jnp.where(spread, 1.0, 0.0).astype(BF16))
    row = lax.broadcasted_iota(jnp.int32, big.shape, 0)
    colb = lax.broadcasted_iota(jnp.int32, big.shape, 1) + col0
    same = ((row >> xs) & (S5_GB - 1)) == ((colb >> ys) & (S5_GB - 1))
    o_ref[0] = jnp.where(same, big, 0.0).astype(BF16)


def _s5_expand(mc, *, xsize, ysize):
    nb, rows, nc = mc.shape
    cols = nc * S5_GB
    cw = 512
    return pl.pallas_call(
        functools.partial(_s5_expand_kernel, xsize=xsize, ysize=ysize),
        grid=(nb, cols // cw),
        in_specs=[pl.BlockSpec((1, rows, nc), lambda b, j: (b, 0, 0))],
        out_specs=pl.BlockSpec((1, rows, cw), lambda b, j: (b, 0, j)),
        out_shape=jax.ShapeDtypeStruct((nb, rows, cols), BF16),
        compiler_params=_cparams(("parallel", "parallel")),
        name="s5_expand",
    )(mc)


def _s5_mats(lam_re, lam_im, log_dt, b_re, b_im, c_re, c_im, d):
    flat = lambda a: a.reshape((-1,) + a.shape[2:])
    kc, bq, cq, ar, ai, dd = map(flat, jax.vmap(_s5_compact)(lam_re, lam_im, log_dt, b_re, b_im, c_re, c_im, d))
    ch, p = S5_GROUP, S5_STATE
    return (_s5_expand(kc, xsize=ch, ysize=ch),
            _s5_expand(bq, xsize=ch, ysize=p),
            _s5_expand(cq, xsize=p, ysize=ch),
            ar, ai, dd)


def _s5_compact(lam_re, lam_im, log_dt, b_re, b_im, c_re, c_im, d):
    q, g, p, ch = S5_Q, S5_GROUPS, S5_STATE, S5_GROUP
    lam = lax.complex(jnp.minimum(lam_re.astype(F32), -1e-4), lam_im.astype(F32))
    ldt = lam * jnp.exp(log_dt.astype(F32))[..., None]
    lam_bar = jnp.exp(ldt)
    b_bar = ((lam_bar - 1.0) / lam)[..., None] * lax.complex(b_re.astype(F32), b_im.astype(F32))
    cc = lax.complex(c_re.astype(F32), c_im.astype(F32))
    pw = jnp.exp(ldt[..., None] * jnp.arange(q + 1, dtype=F32))
    hi = lax.Precision.HIGHEST
    lag = jnp.arange(2 * q, dtype=F32) - (q - 1)
    wf = jnp.where(lag >= 0, jnp.exp(ldt[0][..., None] * jnp.maximum(lag, 0.0)), 0.0)
    wb = jnp.where(lag <= 0, jnp.exp(ldt[1][..., None] * jnp.maximum(-lag, 0.0)), 0.0)
    kc = jnp.real(jnp.einsum('gcp,gpd,gpe->gedc', cc[0], wf, b_bar[0], precision=hi)
                  + jnp.einsum('gcp,gpd,gpe->gedc', cc[1], wb, b_bar[1], precision=hi))

    pw_dn = jnp.exp(ldt[..., None] * (q - jnp.arange(q + 1, dtype=F32)))
    bf = pw_dn[0][..., 1:][:, :, :, None] * b_bar[0][:, :, None, :]
    bb = pw[1][..., :q][:, :, :, None] * b_bar[1][:, :, None, :]
    to_rows = lambda m: m.transpose(0, 2, 3, 1).reshape(g, q * ch, p)
    bq = jnp.concatenate([to_rows(jnp.real(bf)), to_rows(jnp.real(bb)),
                          to_rows(jnp.imag(bf)), to_rows(jnp.imag(bb))], axis=-1)

    cf = cc[0].transpose(0, 2, 1)[:, :, None, :] * pw[0][..., 1:][:, :, :, None]
    cb = cc[1].transpose(0, 2, 1)[:, :, None, :] * pw_dn[1][..., :q][:, :, :, None]
    to_cols = lambda m: m.reshape(g, p, q * ch)
    cq = jnp.concatenate([to_cols(jnp.real(cf)), to_cols(jnp.real(cb)),
                          to_cols(-jnp.imag(cf)), to_cols(-jnp.imag(cb))], axis=1)

    gb, nb = S5_GB, g // S5_GB
    rows_of = lambda a, outer, inner: (a.reshape(nb, gb, outer, inner, a.shape[-1]).transpose(0, 2, 1, 3, 4)
                                       .reshape(nb, outer * gb * inner, a.shape[-1]).astype(BF16))
    kc = kc.reshape(nb, gb * ch, 2 * q * ch).astype(BF16)

    lq = pw[..., q].reshape(2, nb, 1, S5_SW)
    ar = jnp.concatenate([jnp.real(lq[0]), jnp.real(lq[1])], axis=-1)
    ai = jnp.concatenate([jnp.imag(lq[0]), jnp.imag(lq[1])], axis=-1)
    dd = d.astype(F32).reshape(nb, 1, LANES)
    return kc, rows_of(bq, q, ch), rows_of(cq, 4, p), ar, ai, dd


def _s5(proj, mats, h0_re, h0_im, *, row0, n_seq, seq_len, layer=0, dst=None):
    q, p = S5_Q, S5_STATE
    n_chunks = seq_len // q
    m = n_seq * n_chunks
    rows = n_seq * seq_len
    nb = S5_GROUPS // S5_GB
    nblk = S5_W // LANES
    kw = S5_W // S5_SPLIT
    tm, bq, cq, ar, ai, dd = mats
    part = lambda a: a.astype(F32).reshape(n_seq, nb, S5_SW)
    h0 = jnp.concatenate([part(h0_re[:, 0]), part(h0_re[:, 1]), part(h0_im[:, 0]), part(h0_im[:, 1])],
                         axis=-1).transpose(1, 0, 2)
    per_b = lambda shape: pl.BlockSpec((1,) + shape, lambda b, s: (b, 0, 0))
    b0 = layer * nb
    per_l = lambda shape: pl.BlockSpec((1,) + shape, lambda b, s: (b0 + b, 0, 0))
    body, dst_spec, dst_arg, alias = _into(functools.partial(_s5_kernel, n_seq=n_seq, n_chunks=n_chunks), 8, dst)
    y, hf = pl.pallas_call(
        body,
        grid=(nb, 2 * S5_SPLIT),
        in_specs=[
            pl.BlockSpec((rows, LANES), lambda b, s: (row0 // rows, U_COL // LANES + b)),
            pl.BlockSpec((1, kw, 4 * S5_SW), lambda b, s: (b0 + b, jnp.minimum(s, S5_SPLIT - 1), 0)),
            per_l((LANES, 2 * q * LANES)),
            pl.BlockSpec((1, 4 * S5_SW, kw), lambda b, s: (b0 + b, 0, jnp.maximum(s - S5_SPLIT, 0))),
            per_l((1, 2 * S5_SW)), per_l((1, 2 * S5_SW)), per_l((1, LANES)), per_b((n_seq, 4 * S5_SW)),
        ] + dst_spec,
        out_specs=[pl.BlockSpec((rows, LANES), lambda b, s: (row0 // rows, b)), per_b((n_seq, 4 * S5_SW))],
        out_shape=[jax.ShapeDtypeStruct((proj.shape[0], S5_WIDTH), F32),
                   jax.ShapeDtypeStruct((nb, n_seq, 4 * S5_SW), F32)],
        scratch_shapes=[pltpu.VMEM((S5_SPLIT, m, kw), BF16), pltpu.VMEM((nblk, m, LANES), F32),
                        pltpu.VMEM((nblk, m, LANES), F32), pltpu.VMEM((m, 4 * S5_SW), BF16),
                        pltpu.VMEM((S5_SPLIT, S5_W, kw), BF16)],
        input_output_aliases=alias,
        compiler_params=_cparams(("parallel", "arbitrary")),
        name="s5",
    )(proj, bq, tm, cq, ar, ai, dd, h0, *dst_arg)
    hf = hf.reshape(nb, n_seq, 4, S5_GB, p).transpose(1, 2, 0, 3, 4).reshape(n_seq, 4, S5_GROUPS, p)
    return y, hf[:, 0:2], hf[:, 2:4]


def _conv3(x, w, b):
    n = x.shape[0]
    row = lax.broadcasted_iota(jnp.int32, x.shape, 0)
    prev = jnp.where(row == 0, 0.0, pltpu.roll(x, 1, 0))
    nxt = jnp.where(row == n - 1, 0.0, pltpu.roll(x, n - 1, 0))
    return prev * w[0:1] + x * w[1:2] + nxt * w[2:3] + b


def _hy_fwd_kernel(x0_ref, x1_ref, v_ref, w0_ref, w1_ref, wv_ref, b0_ref, b1_ref, bv_ref,
                   fc_ref, fs_ref, m1_ref, m2_ref, m3_ref, p_ref, z_ref, x0c_ref, zb_scr):
    @pl.when(pl.program_id(2) == 0)
    def _():
        z = _conv3(x1_ref[...], w1_ref[...], b1_ref[...]) * _conv3(v_ref[...], wv_ref[...], bv_ref[...])
        z_ref[...] = z
        zb_scr[...] = z.astype(BF16)
        x0c_ref[...] = _conv3(x0_ref[...], w0_ref[...], b0_ref[...])

    zb = zb_scr[...]
    a = _dot(fc_ref[...], zb)
    b = _dot(fs_ref[...], zb)
    m2 = m2_ref[...]
    p_ref[0, 0] = (m1_ref[...] * a + m2 * b).astype(BF16)
    p_ref[0, 1] = (m3_ref[...] * b - m2 * a).astype(BF16)


def _hy_inv_kernel(p_ref, gc_ref, gs_ref, z_ref, x0c_ref, bias_ref, o_ref):
    conv = _dot(gc_ref[...], p_ref[0, 0]) + _dot(gs_ref[...], p_ref[0, 1])
    o_ref[...] = (x0c_ref[...] * (conv + bias_ref[...] * z_ref[...])).astype(o_ref.dtype)


def _hy_short_kernel(x0_ref, x1_ref, v_ref, w0_ref, w1_ref, wv_ref, b0_ref, b1_ref, bv_ref,
                     fc_ref, fs_ref, gs_ref, m1_ref, m2_ref, m3_ref, bias_ref, o_ref):
    z = _conv3(x1_ref[...], w1_ref[...], b1_ref[...]) * _conv3(v_ref[...], wv_ref[...], bv_ref[...])
    zb = z.astype(BF16)
    a = _dot(fc_ref[...], zb)
    b = _dot(fs_ref[...], zb)
    m2 = m2_ref[...]
    p_re = (m1_ref[...] * a + m2 * b).astype(BF16)
    p_im = (m3_ref[...] * b - m2 * a).astype(BF16)
    conv = _dot(fc_ref[...], p_re) + _dot(gs_ref[...], p_im)
    x0c = _conv3(x0_ref[...], w0_ref[...], b0_ref[...])
    o_ref[...] = (x0c * (conv + bias_ref[...] * z)).astype(o_ref.dtype)


def _dft_mats(seq_len):
    n, w = seq_len, 64
    k = jnp.arange(n, dtype=jnp.int32)
    ang = lambda j: ((k[:, None] * j[None, :]) % (2 * n)).astype(F32) * (math.pi / n)
    ang_a = ang(jnp.arange(n // w, dtype=jnp.int32) * w)
    ang_b = ang(jnp.arange(w, dtype=jnp.int32))
    ca, sa = jnp.cos(ang_a)[:, :, None], jnp.sin(ang_a)[:, :, None]
    cb, sb = jnp.cos(ang_b)[:, None, :], jnp.sin(ang_b)[:, None, :]
    cm = (ca * cb - sa * sb).reshape(n, n)
    sm = -(sa * cb + ca * sb).reshape(n, n)
    nyq = jnp.where(k % 2 == 0, 1.0, -1.0).astype(F32)
    return cm.astype(BF16), sm.at[0, :].set(nyq).astype(BF16), sm.at[:, 0].set(nyq).astype(BF16)


def _hy_filter_taps(seq_len, f1_w, f1_b, f2_w, f2_b, f3_w, f3_b, freq, decay):
    n = seq_len
    t = (jnp.arange(n, dtype=F32) / n)[:, None]
    bands = jnp.arange(1, HY_BANDS + 1, dtype=F32)[None, :]
    z = jnp.concatenate([t, jnp.cos(2.0 * math.pi * t * bands), jnp.sin(2.0 * math.pi * t * bands)], axis=-1)
    hi = lax.Precision.HIGHEST
    fr = freq.astype(F32)
    h = jnp.sin(fr * (jnp.dot(z, f1_w.astype(F32), precision=hi) + f1_b.astype(F32)))
    h = jnp.sin(fr * (jnp.dot(h, f2_w.astype(F32), precision=hi) + f2_b.astype(F32)))
    h = jnp.dot(h, f3_w.astype(F32), precision=hi) + f3_b.astype(F32)
    h = h * jnp.exp(-t * jnp.abs(decay.astype(F32)))
    h = h.reshape(n, 2, HY_WIDTH)
    h = h / jnp.sum(jnp.abs(h), axis=(0, 1), keepdims=True)
    return h.reshape(n, 2 * HY_WIDTH)


def _hy_spectrum_kernel(h_ref, cm_ref, sm_ref, ck_ref, sk_ref, m1_ref, m2_ref, m3_ref, *, seq_len):
    w = HY_WIDTH
    tk = cm_ref.shape[0]
    hb = h_ref[0].astype(BF16)
    xc = _dot(cm_ref[...], hb)
    xs = _dot(sm_ref[...], hb)
    k = pl.program_id(1) * tk + lax.broadcasted_iota(jnp.int32, (tk, 1), 0)
    first = k == 0
    sf = jnp.where(first, 0.0, -xs[:, :w])
    sb = jnp.where(first, 0.0, -xs[:, w:])
    ck, sk = ck_ref[...], sk_ref[...]
    hr = xc[:, :w] + ck * xc[:, w:] - sk * sb
    him = -sf + ck * sb + sk * xc[:, w:]
    nyq = xs[:, :w] - xs[:, w:]
    wk = jnp.where(first, 1.0, 2.0) / (2.0 * seq_len)
    m1_ref[0] = hr * wk
    m2_ref[0] = jnp.where(first, 0.0, -him) * wk
    m3_ref[0] = jnp.where(first, nyq, hr) * wk


def _hy_spectrum(h, dft):
    depth, n, _ = h.shape
    cm, sm, _ = dft
    tk = min(n, 512)
    ang = jnp.arange(n, dtype=F32)[:, None] * (math.pi / n)
    out = jax.ShapeDtypeStruct((depth, n, HY_WIDTH), F32)
    oblk = pl.BlockSpec((1, tk, HY_WIDTH), lambda l, k: (l, k, 0))
    return pl.pallas_call(
        functools.partial(_hy_spectrum_kernel, seq_len=n),
        grid=(depth, n // tk),
        in_specs=[pl.BlockSpec((1, n, 2 * HY_WIDTH), lambda l, k: (l, 0, 0)),
                  pl.BlockSpec((tk, n), lambda l, k: (k, 0)), pl.BlockSpec((tk, n), lambda l, k: (k, 0)),
                  pl.BlockSpec((tk, 1), lambda l, k: (k, 0)), pl.BlockSpec((tk, 1), lambda l, k: (k, 0))],
        out_specs=[oblk, oblk, oblk],
        out_shape=[out, out, out],
        compiler_params=_cparams(("parallel", "parallel")),
        name="hyena_spectrum",
    )(h, cm, sm, jnp.cos(ang), jnp.sin(ang))


def _hyena(proj, conv_w, conv_b, bias, dft, mults, *, row0, n_seq, seq_len, cb, tk, layer=0, dst=None):
    blk0 = row0 // seq_len
    nc = HY_WIDTH // cb
    nk = seq_len // tk
    c0 = HY_COL // cb
    cm, sm, smt = dft
    m1, m2, m3 = mults
    xcol = lambda part: pl.BlockSpec((seq_len, cb), lambda b, c, k: (blk0 + b, c0 + part * nc + c))
    wcol = lambda part: pl.BlockSpec((3, cb), lambda b, c, k: (0, part * nc + c))
    bcol = lambda part: pl.BlockSpec((1, cb), lambda b, c, k: (0, part * nc + c))
    frow = pl.BlockSpec((tk, seq_len), lambda b, c, k: (k, 0))
    mblk = pl.BlockSpec((None, tk, cb), lambda b, c, k: (layer, k, c))
    cb2 = conv_b.reshape(1, 3 * HY_WIDTH)
    if nk == 1:
        body, dst_spec, dst_arg, alias = _into(_hy_short_kernel, 16, dst)
        return pl.pallas_call(
            body,
            grid=(n_seq, nc, 1),
            in_specs=[xcol(0), xcol(1), xcol(2), wcol(0), wcol(1), wcol(2), bcol(0), bcol(1), bcol(2),
                      frow, frow, frow, mblk, mblk, mblk,
                      pl.BlockSpec((1, cb), lambda b, c, k: (0, c))] + dst_spec,
            out_specs=pl.BlockSpec((seq_len, cb), lambda b, c, k: (blk0 + b, c)),
            out_shape=jax.ShapeDtypeStruct((proj.shape[0], HY_WIDTH), BF16),
            input_output_aliases=alias,
            compiler_params=_cparams(("parallel", "parallel", "arbitrary")),
            name="hyena_short",
        )(proj, proj, proj, conv_w, conv_w, conv_w, cb2, cb2, cb2, cm, sm, smt, m1, m2, m3,
          bias.reshape(1, HY_WIDTH), *dst_arg)
    pspec, z, x0c = pl.pallas_call(
        _hy_fwd_kernel,
        grid=(n_seq, nc, nk),
        in_specs=[xcol(0), xcol(1), xcol(2), wcol(0), wcol(1), wcol(2), bcol(0), bcol(1), bcol(2),
                  frow, frow, mblk, mblk, mblk],
        out_specs=[
            pl.BlockSpec((1, 2, tk, cb), lambda b, c, k: (b, 0, k, c)),
            pl.BlockSpec((seq_len, cb), lambda b, c, k: (b, c)),
            pl.BlockSpec((seq_len, cb), lambda b, c, k: (b, c)),
        ],
        out_shape=[
            jax.ShapeDtypeStruct((n_seq, 2, seq_len, HY_WIDTH), BF16),
            jax.ShapeDtypeStruct((n_seq * seq_len, HY_WIDTH), F32),
            jax.ShapeDtypeStruct((n_seq * seq_len, HY_WIDTH), F32),
        ],
        scratch_shapes=[pltpu.VMEM((seq_len, cb), BF16)],
        compiler_params=_cparams(("parallel", "parallel", "arbitrary")),
        name="hyena_fwd",
    )(proj, proj, proj, conv_w, conv_w, conv_w, cb2, cb2, cb2, cm, sm, m1, m2, m3)
    grow = pl.BlockSpec((tk, seq_len), lambda b, c, k: (k, 0))
    tile = pl.BlockSpec((tk, cb), lambda b, c, k: (b * nk + k, c))
    body, dst_spec, dst_arg, alias = _into(_hy_inv_kernel, 6, dst)
    return pl.pallas_call(
        body,
        grid=(n_seq, nc, nk),
        in_specs=[pl.BlockSpec((1, 2, seq_len, cb), lambda b, c, k: (b, 0, 0, c)),
                  grow, grow, tile, tile, pl.BlockSpec((1, cb), lambda b, c, k: (0, c))] + dst_spec,
        out_specs=pl.BlockSpec((tk, cb), lambda b, c, k: (row0 // tk + b * nk + k, c)),
        out_shape=jax.ShapeDtypeStruct((proj.shape[0], HY_WIDTH), BF16),
        input_output_aliases=alias,
        compiler_params=_cparams(("parallel", "parallel", "arbitrary")),
        name="hyena_inv",
    )(pspec, cm, smt, z, x0c, bias.reshape(1, HY_WIDTH), *dst_arg)


def _out_kernel(x_ref, ret_ref, s5_ref, hy_ref, mod_ref, g_ref, gw_ref, gb_ref,
                wr_ref, ws_ref, wh_ref, rt_ref, xo_ref, h_ref, lg_ref):
    m = mod_ref[0]
    y = s5_ref[...]
    s5o = y * jax.nn.sigmoid(_dot(y.astype(BF16), gw_ref[...]) + gb_ref[...])
    mix = (_dot(ret_ref[...], wr_ref[...]) + _dot(s5o.astype(BF16), ws_ref[...])
           + _dot(hy_ref[...], wh_ref[...]))
    x = x_ref[...] + m[2:3] * mix
    xo_ref[...] = x
    h = _rms(x, g_ref[...]) * (1.0 + m[4:5]) + m[3:4]
    hb = h.astype(BF16)
    _store_token_tiles(h_ref, _pack_halves(hb))
    lg_ref[...] = lax.dot_general(rt_ref[...], hb, (((1,), (1,)), ((), ())), preferred_element_type=F32)


def _out_proj(x, ret_o, s5_y, hy_o, mod, g, glu_w, glu_b, w_out, router, n_ctx_rows, dec_seq):
    t = x.shape[0]
    tm = ROW_TILE
    grp = functools.partial(_group_of, tile=tm, n_ctx_rows=n_ctx_rows, dec_seq=dec_seq)
    row = lambda w: pl.BlockSpec((tm, w), lambda i: (i, 0))
    full = lambda a, b: pl.BlockSpec((a, b), lambda i: (0, 0))
    wo = w_out.astype(BF16)
    return pl.pallas_call(
        _out_kernel,
        grid=(t // tm,),
        in_specs=[row(D_MODEL), row(RET_WIDTH), row(S5_WIDTH), row(HY_WIDTH),
                  pl.BlockSpec((1, 6, D_MODEL), lambda i: (grp(i), 0, 0)),
                  full(1, D_MODEL), full(S5_WIDTH, S5_WIDTH), full(1, S5_WIDTH),
                  full(RET_WIDTH, D_MODEL), full(S5_WIDTH, D_MODEL), full(HY_WIDTH, D_MODEL),
                  full(N_EXPERTS, D_MODEL)],
        out_specs=[row(D_MODEL), pl.BlockSpec((tm * TOKEN_ROWS, LANES), lambda i: (i, 0)),
                   pl.BlockSpec((N_EXPERTS, tm), lambda i: (0, i))],
        out_shape=[jax.ShapeDtypeStruct((t, D_MODEL), F32), jax.ShapeDtypeStruct((t * TOKEN_ROWS, LANES), jnp.uint32),
                   jax.ShapeDtypeStruct((N_EXPERTS, t), F32)],
        compiler_params=_cparams(("parallel",)),
        name="out_proj",
    )(x, ret_o, s5_y, hy_o, mod, g.reshape(1, D_MODEL), glu_w.astype(BF16), glu_b.reshape(1, S5_WIDTH),
      wo[:RET_WIDTH], wo[RET_WIDTH:RET_WIDTH + S5_WIDTH], wo[RET_WIDTH + S5_WIDTH:], router.T.astype(BF16))


def _moe_kernel(be_ref, first_ref, slot_ref, nxt_ref, nu_ref, xs_ref, wg_hbm, wu_hbm, wd_hbm, o_ref,
                wg_f, wu_f, wd_f, wg_b, wu_b, wd_b, sem, *, layer):
    i = pl.program_id(0)

    def copies(e, s):
        return (pltpu.make_async_copy(wg_hbm.at[layer, e], wg_f.at[s], sem.at[s, 0]),
                pltpu.make_async_copy(wu_hbm.at[layer, e], wu_f.at[s], sem.at[s, 1]),
                pltpu.make_async_copy(wd_hbm.at[layer, e], wd_f.at[s], sem.at[s, 2]))

    @pl.when(i == 0)
    def _():
        for cp in copies(be_ref[0], 0):
            cp.start()

    @pl.when(first_ref[i] == 1)
    def _():
        s = slot_ref[i]
        for cp in copies(be_ref[i], s):
            cp.wait()

        @pl.when(nxt_ref[i] >= 0)
        def _():
            for cp in copies(nxt_ref[i], 1 - s):
                cp.start()

        wg_b[...] = wg_f[s].astype(BF16)
        wu_b[...] = wu_f[s].astype(BF16)
        wd_b[...] = wd_f[s].astype(BF16)

    @pl.when(i < nu_ref[0])
    def _():
        half = D_MODEL // 2
        x_lo, x_hi = _unpack_halves(_load_token_tiles(xs_ref, MOE_BM))
        x_lo = x_lo.astype(BF16)
        x_hi = x_hi.astype(BF16)
        gate = _dot(x_lo, wg_b[0:half, :]) + _dot(x_hi, wg_b[half:, :])
        up = _dot(x_lo, wu_b[0:half, :]) + _dot(x_hi, wu_b[half:, :])
        hb = gate * jax.nn.sigmoid(gate) * up
        _store_token_tiles(o_ref, _pack_halves(_dot(hb.astype(BF16), wd_b[...]).astype(BF16)))

    @pl.when(i >= nu_ref[0])
    def _():
        o_ref[...] = jnp.zeros_like(o_ref)


def _moe_grouped(xs, blk_e, first, slot, nxt, n_used, w_gate, w_up, w_down, layer):
    pr = xs.shape[0] // TOKEN_ROWS
    bm = MOE_BM
    nb = pr // bm
    grid_spec = pltpu.PrefetchScalarGridSpec(
        num_scalar_prefetch=5,
        grid=(nb,),
        in_specs=[
            pl.BlockSpec((bm * TOKEN_ROWS, LANES), lambda i, *_: (i, 0)),
            pl.BlockSpec(memory_space=pl.ANY),
            pl.BlockSpec(memory_space=pl.ANY),
            pl.BlockSpec(memory_space=pl.ANY),
        ],
        out_specs=pl.BlockSpec((bm * TOKEN_ROWS, LANES), lambda i, *_: (i, 0)),
        scratch_shapes=[pltpu.VMEM((2, D_MODEL, D_EXPERT), F32), pltpu.VMEM((2, D_MODEL, D_EXPERT), F32),
                        pltpu.VMEM((2, D_EXPERT, D_MODEL), F32),
                        pltpu.VMEM((D_MODEL, D_EXPERT), BF16), pltpu.VMEM((D_MODEL, D_EXPERT), BF16),
                        pltpu.VMEM((D_EXPERT, D_MODEL), BF16),
                        pltpu.SemaphoreType.DMA((2, 3))],
    )
    return pl.pallas_call(
        functools.partial(_moe_kernel, layer=layer),
        grid_spec=grid_spec,
        out_shape=jax.ShapeDtypeStruct((pr * TOKEN_ROWS, LANES), jnp.uint32),
        compiler_params=_cparams(("arbitrary",)),
        name="moe_grouped",
    )(blk_e, first, slot, nxt, n_used, xs, w_gate, w_up, w_down)


DISPATCH_TILE = 1024


def _dispatch_kernel(info_ref, nu_ref, pos_ref, h_ref, xs_out, zbuf, sem, zsem, *, nb):
    tm = pos_ref.shape[1]
    tr = TOKEN_ROWS
    bm = MOE_BM

    @pl.when(pl.program_id(0) == 0)
    def _():
        zbuf[...] = jnp.zeros_like(zbuf)

        def zero_block(first_row):
            rows = pl.ds(pl.multiple_of(first_row * tr, tr), bm * tr)
            return pltpu.make_async_copy(zbuf, xs_out.at[rows], zsem)

        for act in ("start", "wait"):
            def last_of_expert(e, carry, act=act):
                @pl.when(info_ref[e, 3] > info_ref[e, 1])
                def _():
                    getattr(zero_block(info_ref[e, 3] - bm), act)()
                return carry

            def tail_block(b, carry, act=act):
                getattr(zero_block(b * bm), act)()
                return carry

            lax.fori_loop(0, N_EXPERTS, last_of_expert, 0)
            lax.fori_loop(nu_ref[0], nb, tail_block, 0)

    def send(tok, carry):
        src = h_ref.at[pl.ds(pl.multiple_of(tok * tr, tr), tr)]
        for k in range(TOP_K):
            row = pl.multiple_of(pos_ref[k, tok] * tr, tr)
            pltpu.make_async_copy(src, xs_out.at[pl.ds(row, tr)], sem).start(priority=k % 2)
        return carry

    lax.fori_loop(0, tm, send, 0, unroll=4)
    n = tm * TOP_K * tr
    pltpu.make_async_copy(xs_out.at[pl.ds(0, n)], xs_out.at[pl.ds(0, n)], sem).wait()


def _dispatch(h2, pos, info, n_used, nb):
    t = pos.shape[1]
    tm = DISPATCH_TILE
    return pl.pallas_call(
        functools.partial(_dispatch_kernel, nb=nb),
        grid=(t // tm,),
        in_specs=[pl.BlockSpec(memory_space=pltpu.SMEM),
                  pl.BlockSpec(memory_space=pltpu.SMEM),
                  pl.BlockSpec((SUBLANES, tm), lambda i: (0, i), memory_space=pltpu.SMEM),
                  pl.BlockSpec((tm * TOKEN_ROWS, LANES), lambda i: (i, 0))],
        out_specs=pl.BlockSpec(memory_space=pl.ANY),
        out_shape=jax.ShapeDtypeStruct((nb * MOE_BM * TOKEN_ROWS, LANES), h2.dtype),
        scratch_shapes=[pltpu.VMEM((MOE_BM * TOKEN_ROWS, LANES), h2.dtype),
                        pltpu.SemaphoreType.DMA(()), pltpu.SemaphoreType.DMA(())],
        compiler_params=pltpu.CompilerParams(dimension_semantics=("arbitrary",)),
        name="dispatch",
    )(info, n_used, pos, h2)


ROUTE_TILE = 512


def _router_kernel(lg_ref, bias_ref, pos_ref, gate_ref, be_ref, info_ref, rank_scr, ek_scr, *, t, nbp):
    tl = ROUTE_TILE
    ne = N_EXPERTS
    bm = MOE_BM
    row = lax.broadcasted_iota(jnp.int32, (ne, tl), 0)
    tri = (lax.broadcasted_iota(jnp.int32, (tl, tl), 0) < lax.broadcasted_iota(jnp.int32, (tl, tl), 1)).astype(BF16)
    bias = bias_ref[...]

    def select(i, counts):
        cols = pl.ds(pl.multiple_of(i * tl, tl), tl)
        s = jax.nn.sigmoid(lg_ref[:, cols])
        sel = s + bias
        mask = jnp.zeros((ne, tl), F32)
        vals = []
        for k in range(TOP_K):
            best = jnp.max(sel, axis=0, keepdims=True)
            idx = jnp.min(jnp.where(sel == best, row, ne), axis=0, keepdims=True)
            hit = row == idx
            vals.append(jnp.sum(jnp.where(hit, s, 0.0), axis=0, keepdims=True))
            sel = jnp.where(hit, -jnp.inf, sel)
            mask = jnp.where(hit, 1.0, mask)
            ek_scr[k:k + 1, cols] = idx
        total = vals[0]
        for v in vals[1:]:
            total = total + v
        scale = ROUTED_SCALE / total
        for k in range(TOP_K):
            gate_ref[k:k + 1, cols] = vals[k] * scale
        gate_ref[TOP_K:SUBLANES, cols] = jnp.zeros((SUBLANES - TOP_K, tl), F32)
        rank_scr[:, cols] = _dot(mask.astype(BF16), tri) + counts
        return counts + jnp.sum(mask, axis=1, keepdims=True)

    counts = lax.fori_loop(0, t // tl, select, jnp.zeros((ne, 1), F32))
    counts = counts.astype(jnp.int32)
    shift = bm.bit_length() - 1
    assert bm == 1 << shift
    padded = ((counts + (bm - 1)) >> shift) << shift
    e0 = lax.broadcasted_iota(jnp.int32, (ne, ne), 0)
    e1 = lax.broadcasted_iota(jnp.int32, (ne, ne), 1)
    padded_row = jnp.sum(jnp.where(e0 == e1, padded, 0), axis=0, keepdims=True)
    counts_row = jnp.sum(jnp.where(e0 == e1, counts, 0), axis=0, keepdims=True)
    pstart = jnp.sum(jnp.where(e1 < e0, padded_row, 0), axis=1, keepdims=True)
    ustart = jnp.sum(jnp.where(e1 < e0, counts_row, 0), axis=1, keepdims=True)
    pend = pstart + padded
    lane = lax.broadcasted_iota(jnp.int32, (ne, LANES), 1)
    info_ref[...] = jnp.where(lane == 0, counts, jnp.where(lane == 1, pstart, jnp.where(lane == 2, ustart, pend)))
    blk = lax.broadcasted_iota(jnp.int32, (ne, nbp), 1) * bm
    owner = jnp.sum(jnp.where(pend <= blk, 1, 0), axis=0, keepdims=True)
    be_ref[...] = jnp.minimum(owner, ne - 1)
    pstart_f = pstart.astype(F32)

    def place(i, carry):
        cols = pl.ds(pl.multiple_of(i * tl, tl), tl)
        dest = rank_scr[:, cols] + pstart_f
        for k in range(TOP_K):
            hit = row == ek_scr[k:k + 1, cols]
            pos_ref[k:k + 1, cols] = jnp.sum(jnp.where(hit, dest, 0.0), axis=0, keepdims=True).astype(jnp.int32)
        pos_ref[TOP_K:SUBLANES, cols] = jnp.zeros((SUBLANES - TOP_K, tl), jnp.int32)
        return carry

    lax.fori_loop(0, t // tl, place, 0)


def _router(logits_t, router_bias, nb):
    t = logits_t.shape[1]
    nbp = -(-nb // LANES) * LANES
    return pl.pallas_call(
        functools.partial(_router_kernel, t=t, nbp=nbp),
        out_shape=[jax.ShapeDtypeStruct((SUBLANES, t), jnp.int32), jax.ShapeDtypeStruct((SUBLANES, t), F32),
                   jax.ShapeDtypeStruct((1, nbp), jnp.int32), jax.ShapeDtypeStruct((N_EXPERTS, LANES), jnp.int32)],
        scratch_shapes=[pltpu.VMEM((N_EXPERTS, t), F32), pltpu.VMEM((SUBLANES, t), jnp.int32)],
        compiler_params=pltpu.CompilerParams(vmem_limit_bytes=VMEM_LIMIT),
        name="router",
    )(logits_t, router_bias.astype(F32).reshape(N_EXPERTS, 1))


def _dispatch_plan(blk_e_row, info, nb):
    bm = MOE_BM
    pend = info[:, 3]
    blk_e = blk_e_row[0, :nb]
    n_used = pend[-1] // bm
    blk = jnp.arange(nb, dtype=jnp.int32)
    prev_e = jnp.concatenate([jnp.full((1,), -1, jnp.int32), blk_e[:-1]])
    first = jnp.logical_and(blk < n_used, blk_e != prev_e)
    slot = (jnp.cumsum(first.astype(jnp.int32)) - 1) % 2
    first_at = jnp.where(first, blk, nb)
    nxt_first = lax.cummin(jnp.concatenate([first_at[1:], jnp.full((1,), nb, jnp.int32)]), reverse=True)
    nxt = jnp.where(nxt_first < nb, blk_e[jnp.minimum(nxt_first, nb - 1)], -1)
    return (blk_e, first.astype(jnp.int32), slot.astype(jnp.int32), nxt.astype(jnp.int32),
            n_used.astype(jnp.int32).reshape(1))


def _shared_kernel(pos_ref, nxt_ref, x_ref, h_ref, gt_ref, mod_ref, sg_ref, su_ref, sd_ref, fn_ref, eo_hbm,
                   o_ref, buf0, buf1, acc_scr, sem, *, final):
    i = pl.program_id(0)
    n = pl.num_programs(0)
    tm = x_ref.shape[0]
    tr = TOKEN_ROWS

    half = D_MODEL // 2
    grp = SUBLANES

    def fetch_tokens(idx_ref, buf, slot, tok0):
        for tt in range(grp):
            tok = tok0 + tt
            for k in range(TOP_K):
                row = pl.multiple_of(idx_ref[k, tok] * tr, tr)
                pltpu.make_async_copy(eo_hbm.at[pl.ds(row, tr)], buf.at[k, pl.ds(pl.multiple_of(tok * tr, tr), tr)],
                                      sem.at[slot]).start(priority=k % 2)

    def arrived(buf, slot):
        for k in range(TOP_K):
            pltpu.make_async_copy(eo_hbm.at[pl.ds(0, tm * tr)], buf.at[k], sem.at[slot]).wait()

    def step(cur, cur_slot, nxt, nxt_slot):
        arrived(cur, cur_slot)
        h_lo, h_hi = _unpack_halves(_load_token_tiles(h_ref, tm))
        h_lo = h_lo.astype(BF16)
        h_hi = h_hi.astype(BF16)
        gate = _dot(h_lo, sg_ref[0:half, :]) + _dot(h_hi, sg_ref[half:, :])
        up = _dot(h_lo, su_ref[0:half, :]) + _dot(h_hi, su_ref[half:, :])
        act = gate * jax.nn.sigmoid(gate) * up
        acc_scr[...] = _dot(act.astype(BF16), sd_ref[...])

        def group(c, carry):
            tok0 = pl.multiple_of(c * grp, grp)
            fetch_tokens(nxt_ref, nxt, nxt_slot, tok0)
            gt = gt_ref[pl.ds(tok0, grp), :]
            r_lo = jnp.zeros((grp, half), F32)
            r_hi = r_lo
            for k in range(TOP_K):
                words = jnp.concatenate(
                    [cur[k, pl.ds(tok0 * tr + s, grp, stride=tr), :] for s in range(tr)], axis=1)
                e_lo, e_hi = _unpack_halves(words)
                r_lo = r_lo + gt[:, k:k + 1] * e_lo
                r_hi = r_hi + gt[:, k:k + 1] * e_hi
            acc_scr[pl.ds(tok0, grp), :] += jnp.concatenate([r_lo, r_hi], axis=1)
            return carry

        lax.fori_loop(0, tm // grp, group, 0)
        x = x_ref[...] + mod_ref[0][5:6] * acc_scr[...]
        if final:
            x = _rms(x, fn_ref[...])
        o_ref[...] = x

        @pl.when(i == n - 1)
        def _():
            arrived(nxt, nxt_slot)

    @pl.when(i == 0)
    def _():
        def first(c, carry):
            fetch_tokens(pos_ref, buf0, 0, pl.multiple_of(c * grp, grp))
            return carry
        lax.fori_loop(0, tm // grp, first, 0)

    for parity, cur, nxt in ((0, buf0, buf1), (1, buf1, buf0)):
        @pl.when(i % 2 == parity)
        def _(parity=parity, cur=cur, nxt=nxt):
            step(cur, parity, nxt, 1 - parity)


def _shared(x, h, gates_t, pos, eo, mod, sg, su, sd, final_norm, n_ctx_rows, dec_seq, *, final, row0=0, rows=None):
    rows = x.shape[0] if rows is None else rows
    tm = ROW_TILE // 2
    b0 = row0 // tm
    steps = rows // tm
    grp = lambda i: _group_of(i + b0, tm, n_ctx_rows, dec_seq)
    row = pl.BlockSpec((tm, D_MODEL), lambda i: (i + b0, 0))
    prow = pl.BlockSpec((tm * TOKEN_ROWS, LANES), lambda i: (i + b0, 0))
    full = lambda a, b: pl.BlockSpec((a, b), lambda i: (0, 0))
    buf = pltpu.VMEM((TOP_K, tm * TOKEN_ROWS, LANES), eo.dtype)
    return pl.pallas_call(
        functools.partial(_shared_kernel, final=final),
        grid=(steps,),
        in_specs=[pl.BlockSpec((SUBLANES, tm), lambda i: (0, i + b0), memory_space=pltpu.SMEM),
                  pl.BlockSpec((SUBLANES, tm), lambda i: (0, jnp.minimum(i + 1, steps - 1) + b0),
                               memory_space=pltpu.SMEM),
                  row, prow, pl.BlockSpec((tm, SUBLANES), lambda i: (i + b0, 0)),
                  pl.BlockSpec((1, 6, D_MODEL), lambda i: (grp(i), 0, 0)),
                  full(D_MODEL, D_SHARED), full(D_MODEL, D_SHARED), full(D_SHARED, D_MODEL), full(1, D_MODEL),
                  pl.BlockSpec(memory_space=pl.ANY)],
        out_specs=pl.BlockSpec((tm, D_MODEL), lambda i: (i, 0)),
        out_shape=jax.ShapeDtypeStruct((rows, D_MODEL), F32),
        scratch_shapes=[buf, buf, pltpu.VMEM((tm, D_MODEL), F32), pltpu.SemaphoreType.DMA((2,))],
        compiler_params=_cparams(("arbitrary",)),
        name="shared_final" if final else "shared",
    )(pos, pos, x, h, gates_t, mod, sg.astype(BF16), su.astype(BF16), sd.astype(BF16),
      final_norm.reshape(1, D_MODEL), eo)


def kernel(x_prompt, x_sample, state_ret, state_s5_re, state_s5_im, c, c_ctx, w_ada, b_ada, norm_mix, norm_ffn, w_in, w_out, ret_decay, s5_lam_re, s5_lam_im, s5_log_dt, s5_b_re, s5_b_im, s5_c_re, s5_c_im, s5_d, s5_glu_w, s5_glu_b, hy_conv_w, hy_conv_b, hy_f1_w, hy_f1_b, hy_f2_w, hy_f2_b, hy_f3_w, hy_f3_b, hy_freq, hy_decay, hy_bias, moe_router, moe_router_bias, moe_w_gate, moe_w_up, moe_w_down, sh_w_gate, sh_w_up, sh_w_down, final_norm):
    n_ctx, seq, d = x_prompt.shape
    n_dec, dec_seq, _ = x_sample.shape
    n_ctx_rows = n_ctx * seq
    t = n_ctx_rows + n_dec * dec_seq

    x = jnp.concatenate([x_prompt.reshape(n_ctx_rows, d), x_sample.reshape(n_dec * dec_seq, d)], axis=0)
    cond = jnp.concatenate([c_ctx[None, :], c], axis=0)
    cond8 = jnp.pad(cond, ((0, SUBLANES - cond.shape[0]), (0, 0)))
    mods = _ada(cond8, w_ada, b_ada)[:, :1 + n_dec].reshape(DEPTH, 1 + n_dec, 6, d)

    cos2, sin2 = _rope_tables(dec_seq)
    no_rope = jnp.zeros((seq, LANES), F32)
    zero_ret = jnp.zeros((n_ctx, 2, RET_HEADS, RET_DK, RET_DV), F32)
    zero_s5 = jnp.zeros((n_ctx, 2, S5_GROUPS, S5_STATE), F32)
    dft_ctx = _dft_mats(seq)
    dft_dec = _dft_mats(dec_seq)

    w_in_bf = w_in.astype(BF16)
    mats = _s5_mats(s5_lam_re, s5_lam_im, s5_log_dt, s5_b_re, s5_b_im, s5_c_re, s5_c_im, s5_d)
    filt = (hy_f1_w, hy_f1_b, hy_f2_w, hy_f2_b, hy_f3_w, hy_f3_b, hy_freq, hy_decay)
    mults_ctx = _hy_spectrum(jax.vmap(functools.partial(_hy_filter_taps, seq))(*filt), dft_ctx)
    mults_dec = _hy_spectrum(jax.vmap(functools.partial(_hy_filter_taps, dec_seq))(*filt), dft_dec)

    ret_states = jnp.zeros((n_ctx, DEPTH, 2, RET_HEADS, RET_DK, RET_DV), F32)
    s5r_list, s5i_list = [], []
    for l in range(DEPTH):
        mod = mods[l]
        proj = _in_proj(x, mod, norm_mix[l], w_in_bf, l, n_ctx_rows, dec_seq)

        log_gamma = jax.nn.log_sigmoid(ret_decay[l].astype(F32))
        ret_o, ret_states = _retention(proj, log_gamma, zero_ret, no_rope, no_rope,
                                       row0=0, n_seq=n_ctx, seq_len=seq, hb=RET_HEADS, rope=False,
                                       dst=jnp.zeros((t, RET_WIDTH), BF16), states=ret_states, layer=l)
        ret_o, _ = _retention(proj, log_gamma, state_ret[:, l].astype(F32), cos2, sin2,
                              row0=n_ctx_rows, n_seq=n_dec, seq_len=dec_seq, hb=2, rope=True, dst=ret_o)

        s5_y, s5_re, s5_im = _s5(proj, mats, zero_s5, zero_s5, row0=0, n_seq=n_ctx, seq_len=seq, layer=l,
                                 dst=jnp.zeros((t, S5_WIDTH), F32))
        s5_y, _, _ = _s5(proj, mats, state_s5_re[:, l], state_s5_im[:, l],
                         row0=n_ctx_rows, n_seq=n_dec, seq_len=dec_seq, layer=l, dst=s5_y)
        s5r_list.append(s5_re)
        s5i_list.append(s5_im)

        hy_o = _hyena(proj, hy_conv_w[l], hy_conv_b[l], hy_bias[l], dft_ctx, mults_ctx, layer=l,
                      row0=0, n_seq=n_ctx, seq_len=seq, cb=HY_WIDTH, tk=seq, dst=jnp.zeros((t, HY_WIDTH), BF16))
        hy_o = _hyena(proj, hy_conv_w[l], hy_conv_b[l], hy_bias[l], dft_dec, mults_dec, layer=l,
                      row0=n_ctx_rows, n_seq=n_dec, seq_len=dec_seq, cb=HY_WIDTH // 2, tk=512, dst=hy_o)

        x, h2, logits = _out_proj(x, ret_o, s5_y, hy_o, mod, norm_ffn[l], s5_glu_w[l], s5_glu_b[l],
                                  w_out[l], moe_router[l], n_ctx_rows, dec_seq)

        nb = -(-(t * TOP_K) // MOE_BM) + N_EXPERTS
        pos, gates, blk_e_row, info = _router(logits, moe_router_bias[l], nb)
        blk_e, first, slot, nxt, n_used = _dispatch_plan(blk_e_row, info, nb)
        xs = _dispatch(h2, pos, info, n_used, nb)
        eo = _moe_grouped(xs, blk_e, first, slot, nxt, n_used, moe_w_gate, moe_w_up, moe_w_down, l)
        gates_t = gates.T

        sh = (sh_w_gate[l], sh_w_up[l], sh_w_down[l])
        if l < DEPTH - 1:
            x = _shared(x, h2, gates_t, pos, eo, mod, *sh, final_norm, n_ctx_rows, dec_seq, final=False)
        else:
            y_c = _shared(x, h2, gates_t, pos, eo, mod, *sh, final_norm, n_ctx_rows, dec_seq, final=True,
                          row0=0, rows=n_ctx_rows)
            y_d = _shared(x, h2, gates_t, pos, eo, mod, *sh, final_norm, n_ctx_rows, dec_seq, final=True,
                          row0=n_ctx_rows, rows=n_dec * dec_seq)

    return (y_c.reshape(n_ctx, seq, d), y_d.reshape(n_dec, dec_seq, d),
            ret_states, jnp.stack(s5r_list, axis=1), jnp.stack(s5i_list, axis=1))
```

```python
import functools
import math

import jax
import jax.numpy as jnp
from jax import lax
from jax.experimental import pallas as pl
from jax.experimental.pallas import tpu as pltpu

F32 = jnp.float32
BF16 = jnp.bfloat16

D_MODEL = 2048
DEPTH = 2
GRID_W = 64
RET_HEADS = 8
RET_DK = 128
RET_DV = 128
RET_WIDTH = RET_HEADS * RET_DV
RET_CHUNK = 256
ROPE_BASE = 10000.0
S5_WIDTH = 512
S5_GROUP = 16
S5_GROUPS = S5_WIDTH // S5_GROUP
S5_STATE = 64
S5_Q = 16
HY_WIDTH = 512
HY_BANDS = 16
IN_WIDTH = 4 * RET_WIDTH + S5_WIDTH + 3 * HY_WIDTH
U_COL = 4 * RET_WIDTH
HY_COL = U_COL + S5_WIDTH
N_EXPERTS = 64
TOP_K = 6
D_EXPERT = 512
D_SHARED = 512
ROUTED_SCALE = 2.5
EPS = 1e-6

LANES = 128
SUBLANES = 8
VMEM_LIMIT = 56 * 1024 * 1024

ROW_TILE = 512
MOE_BM = 256


def _cparams(sem):
    return pltpu.CompilerParams(dimension_semantics=sem, vmem_limit_bytes=VMEM_LIMIT)


def _dot(a, b):
    return jnp.dot(a, b, preferred_element_type=F32)


def _rms(x, g):
    var = jnp.mean(x * x, axis=-1, keepdims=True)
    return x * lax.rsqrt(var + EPS) * g


def _pack_halves(xb):
    n = xb.shape[1] // 2
    lo = lax.bitcast_convert_type(xb[:, :n].astype(F32), jnp.uint32) >> 16
    hi = lax.bitcast_convert_type(xb[:, n:].astype(F32), jnp.uint32)
    return lo | hi


def _unpack_halves(w):
    lo = lax.bitcast_convert_type(w << 16, F32)
    hi = lax.bitcast_convert_type(w & jnp.uint32(0xFFFF0000), F32)
    return lo, hi


TOKEN_ROWS = D_MODEL // 2 // LANES


def _store_token_tiles(ref, w):
    m = w.shape[0]
    for s in range(TOKEN_ROWS):
        ref[pl.ds(s, m, stride=TOKEN_ROWS), :] = w[:, s * LANES:(s + 1) * LANES]


def _load_token_tiles(ref, m):
    return jnp.concatenate([ref[pl.ds(s, m, stride=TOKEN_ROWS), :] for s in range(TOKEN_ROWS)], axis=1)


def _ada_kernel(c_ref, w_ref, b_ref, o_ref):
    c = c_ref[...]
    s = (c * jax.nn.sigmoid(c)).astype(BF16)
    o_ref[0] = _dot(s, w_ref[0].astype(BF16)) + b_ref[0]


def _ada(cond8, w_ada, b_ada):
    tn = 1024
    n = w_ada.shape[-1]
    return pl.pallas_call(
        _ada_kernel,
        grid=(DEPTH, n // tn),
        in_specs=[
            pl.BlockSpec((SUBLANES, D_MODEL), lambda l, j: (0, 0)),
            pl.BlockSpec((1, D_MODEL, tn), lambda l, j: (l, 0, j)),
            pl.BlockSpec((1, 1, tn), lambda l, j: (l, 0, j)),
        ],
        out_specs=pl.BlockSpec((1, SUBLANES, tn), lambda l, j: (l, 0, j)),
        out_shape=jax.ShapeDtypeStruct((DEPTH, SUBLANES, n), F32),
        compiler_params=_cparams(("parallel", "parallel")),
        name="ada",
    )(cond8, w_ada, b_ada.reshape(DEPTH, 1, n))


def _group_of(i, tile, n_ctx_rows, dec_seq):
    ctx_tiles = n_ctx_rows // tile
    per = dec_seq // tile
    return jnp.where(i < ctx_tiles, 0, 1 + (i - ctx_tiles) // per)


def _in_kernel(x_ref, mod_ref, g_ref, w_ref, o_ref, h_scr):
    @pl.when(pl.program_id(1) == 0)
    def _():
        m = mod_ref[0]
        h = _rms(x_ref[...], g_ref[...]) * (1.0 + m[1:2]) + m[0:1]
        h_scr[...] = h.astype(BF16)

    o_ref[...] = _dot(h_scr[...], w_ref[...])


def _in_proj(x, mod, g, w_bf, layer, n_ctx_rows, dec_seq):
    t = x.shape[0]
    tm, tn = 1024, 1024
    grp = functools.partial(_group_of, tile=tm, n_ctx_rows=n_ctx_rows, dec_seq=dec_seq)
    return pl.pallas_call(
        _in_kernel,
        grid=(t // tm, IN_WIDTH // tn),
        in_specs=[
            pl.BlockSpec((tm, D_MODEL), lambda i, j: (i, 0)),
            pl.BlockSpec((1, 6, D_MODEL), lambda i, j: (grp(i), 0, 0)),
            pl.BlockSpec((1, D_MODEL), lambda i, j: (0, 0)),
            pl.BlockSpec((None, D_MODEL, tn), lambda i, j: (layer, 0, j)),
        ],
        out_specs=pl.BlockSpec((tm, tn), lambda i, j: (i, j)),
        out_shape=jax.ShapeDtypeStruct((t, IN_WIDTH), F32),
        scratch_shapes=[pltpu.VMEM((tm, D_MODEL), BF16)],
        compiler_params=_cparams(("parallel", "arbitrary")),
        name="in_proj",
    )(x, mod, g.reshape(1, D_MODEL), w_bf)


def _ret_kernel(lg_ref, q_ref, k_ref, v_ref, gt_ref, cos_ref, sin_ref, s0_ref,
                o_ref, sfin_ref, acc_scr, q_scr, k_scr, *, seq_len, hb, rope):
    c = RET_CHUNK
    n_chunks = seq_len // c
    ii = lax.broadcasted_iota(jnp.int32, (c, c), 0)
    jj = lax.broadcasted_iota(jnp.int32, (c, c), 1)
    rel = (ii - jj).astype(F32)
    ci = lax.broadcasted_iota(jnp.int32, (c, 1), 0).astype(F32)
    one = jnp.ones((1, 1), F32)
    tdot = functools.partial(lax.dot_general, preferred_element_type=F32)

    def make_head(hh):
        head = pl.program_id(1) * hb + hh
        lgf = lg_ref[0, head]
        lgb = lg_ref[1, head]
        dmask = (jnp.where(rel >= 0, jnp.exp(lgf * jnp.maximum(rel, 0.0)), 0.0)
                 + jnp.where(rel <= 0, jnp.exp(lgb * jnp.maximum(-rel, 0.0)), 0.0))
        qd_f = jnp.exp(lgf * (ci + 1.0))
        kd_f = jnp.exp(lgf * (c - 1.0 - ci))
        cd_f = jnp.exp(lgf * c * one)
        qd_b = jnp.exp(lgb * (c - ci))
        kd_b = jnp.exp(lgb * ci)
        cd_b = jnp.exp(lgb * c * one)
        lanes = slice(hh * LANES, (hh + 1) * LANES)

        def rows_of(n):
            if isinstance(n, int):
                return slice(n * c, (n + 1) * c)
            return pl.ds(pl.multiple_of(n * c, c), c)

        def fwd_chunk(n, s_f):
            rows = rows_of(n)
            q = q_ref[rows, lanes]
            k = k_ref[rows, lanes] * (RET_DK ** -0.5)
            if rope:
                cs = cos_ref[rows, :]
                sn = sin_ref[rows, :]
                q = q * cs + pltpu.roll(q, RET_DK // 2, 1) * sn
                k = k * cs + pltpu.roll(k, RET_DK // 2, 1) * sn
            qb = q.astype(BF16)
            vb = v_ref[rows, lanes].astype(BF16)
            q_scr[rows, lanes] = qb
            k_scr[rows, lanes] = k
            scores = tdot(qb, k.astype(BF16), (((1,), (1,)), ((), ()))) * dmask
            inner = _dot(scores.astype(BF16), vb)
            cross = _dot(qb, s_f.astype(BF16)) * qd_f
            acc_scr[rows, lanes] = inner + cross
            upd = tdot((k * kd_f).astype(BF16), vb, (((0,), (0,)), ((), ())))
            return s_f * cd_f + upd

        def bwd_chunk(m, s_b):
            n = n_chunks - 1 - m
            rows = rows_of(n)
            qb = q_scr[rows, lanes]
            k = k_scr[rows, lanes]
            vb = v_ref[rows, lanes].astype(BF16)
            o = acc_scr[rows, lanes] + _dot(qb, s_b.astype(BF16)) * qd_b
            mu = jnp.mean(o, axis=-1, keepdims=True)
            oc = o - mu
            var = jnp.mean(oc * oc, axis=-1, keepdims=True)
            o = oc * lax.rsqrt(var + EPS)
            g = gt_ref[rows, lanes]
            o_ref[rows, lanes] = (g * jax.nn.sigmoid(g) * o).astype(o_ref.dtype)
            upd = tdot((k * kd_b).astype(BF16), vb, (((0,), (0,)), ((), ())))
            return s_b * cd_b + upd

        return fwd_chunk, bwd_chunk

    if n_chunks <= 4:
        for hh in range(hb):
            fwd_chunk, bwd_chunk = make_head(hh)
            s_f = s0_ref[0, 0, hh]
            s_b = s0_ref[0, 1, hh]
            for n in range(n_chunks):
                s_f = fwd_chunk(n, s_f)
            for m in range(n_chunks):
                s_b = bwd_chunk(m, s_b)
            sfin_ref[0, 0, hh] = s_f
            sfin_ref[0, 1, hh] = s_b
    else:
        fns = [make_head(hh) for hh in range(hb)]
        s_f = lax.fori_loop(0, n_chunks, lambda n, ss: tuple(f[0](n, s) for f, s in zip(fns, ss)),
                            tuple(s0_ref[0, 0, hh] for hh in range(hb)))
        s_b = lax.fori_loop(0, n_chunks, lambda m, ss: tuple(f[1](m, s) for f, s in zip(fns, ss)),
                            tuple(s0_ref[0, 1, hh] for hh in range(hb)))
        for hh in range(hb):
            sfin_ref[0, 0, hh] = s_f[hh]
            sfin_ref[0, 1, hh] = s_b[hh]


def _into(kernel_fn, n_in, dst):
    dsts = [d for d in (dst if isinstance(dst, (list, tuple)) else [dst])]
    outs = [k for k, d in enumerate(dsts) if d is not None]
    if not outs:
        return kernel_fn, [], [], {}

    def body(*refs):
        return kernel_fn(*refs[:n_in], *refs[n_in + len(outs):])

    return (body, [pl.BlockSpec(memory_space=pl.ANY)] * len(outs), [dsts[k] for k in outs],
            {n_in + pos: k for pos, k in enumerate(outs)})


def _retention(proj, log_gamma, s0, cos2, sin2, *, row0, n_seq, seq_len, hb, rope, dst=None,
               states=None, layer=0):
    blk0 = row0 // seq_len
    body, dst_spec, dst_arg, alias = _into(
        functools.partial(_ret_kernel, seq_len=seq_len, hb=hb, rope=rope), 8, [dst, states])
    if states is None:
        st_spec = pl.BlockSpec((1, 2, hb, RET_DK, RET_DV), lambda b, h, lg: (b, 0, h, 0, 0))
        st_shape = jax.ShapeDtypeStruct((n_seq, 2, RET_HEADS, RET_DK, RET_DV), F32)
    else:
        st_spec = pl.BlockSpec((1, None, 2, hb, RET_DK, RET_DV), lambda b, h, lg: (b, layer, 0, h, 0, 0))
        st_shape = jax.ShapeDtypeStruct(states.shape, F32)
    w = hb * LANES
    hblocks = RET_HEADS // hb
    col = lambda part: (lambda b, h, lg: (blk0 + b, part * hblocks + h))
    grid_spec = pltpu.PrefetchScalarGridSpec(
        num_scalar_prefetch=1,
        grid=(n_seq, hblocks),
        in_specs=[
            pl.BlockSpec((seq_len, w), col(0)),
            pl.BlockSpec((seq_len, w), col(1)),
            pl.BlockSpec((seq_len, w), col(2)),
            pl.BlockSpec((seq_len, w), col(3)),
            pl.BlockSpec((seq_len, LANES), lambda b, h, lg: (0, 0)),
            pl.BlockSpec((seq_len, LANES), lambda b, h, lg: (0, 0)),
            pl.BlockSpec((1, 2, hb, RET_DK, RET_DV), lambda b, h, lg: (b if s0.shape[0] > 1 else 0, 0, h, 0, 0)),
        ] + dst_spec,
        out_specs=[
            pl.BlockSpec((seq_len, w), lambda b, h, lg: (blk0 + b, h)),
            st_spec,
        ],
        scratch_shapes=[
            pltpu.VMEM((seq_len, w), F32),
            pltpu.VMEM((seq_len, w), BF16),
            pltpu.VMEM((seq_len, w), F32),
        ],
    )
    return pl.pallas_call(
        body,
        grid_spec=grid_spec,
        out_shape=[jax.ShapeDtypeStruct((proj.shape[0], RET_WIDTH), BF16), st_shape],
        input_output_aliases=alias,
        compiler_params=_cparams(("parallel", "arbitrary")),
        name="retention",
    )(log_gamma, proj, proj, proj, proj, cos2, sin2, s0, *dst_arg)


def _rope_tables(seq_len):
    rows_n = seq_len // GRID_W
    rows = jnp.repeat(jnp.arange(rows_n, dtype=F32), GRID_W)
    cols = jnp.tile(jnp.arange(GRID_W, dtype=F32), rows_n)
    nf = RET_DK // 4
    inv = ROPE_BASE ** (-jnp.arange(nf, dtype=F32) / nf)
    ang = jnp.concatenate([rows[:, None] * inv, cols[:, None] * inv], axis=-1)
    cs, sn = jnp.cos(ang), jnp.sin(ang)
    return jnp.concatenate([cs, cs], axis=-1), jnp.concatenate([-sn, sn], axis=-1)


S5_GB = LANES // S5_GROUP
S5_W = S5_Q * LANES
S5_SPLIT = 4
S5_SW = S5_GB * S5_STATE
S5_SB = S5_SW // LANES


def _s5_kernel(u_ref, bq_ref, k_ref, cq_ref, ar_ref, ai_ref, d_ref, h0_ref,
               y_ref, hf_ref, ub_scr, sm_scr, hp_scr, hpb_scr, t_scr, *, n_seq, n_chunks):
    s = pl.program_id(1)
    m = n_seq * n_chunks
    q = S5_Q
    nblk = S5_W // LANES
    sb = S5_SB

    per = q // S5_SPLIT
    kw = S5_W // S5_SPLIT

    @pl.when(s == 0)
    def _():
        for j in range(q):
            ub_scr[j // per, :, (j % per) * LANES:(j % per + 1) * LANES] = (
                u_ref[pl.ds(j, m, stride=q), :].astype(BF16))
        for ib in range(S5_SPLIT):
            for j in range(q):
                c0 = (q - 1 - j) * LANES + ib * kw
                t_scr[ib, j * LANES:(j + 1) * LANES, :] = k_ref[0, :, c0:c0 + kw]

    @pl.when(s < S5_SPLIT)
    def _():
        part = _dot(ub_scr[jnp.minimum(s, S5_SPLIT - 1)], bq_ref[0])

        @pl.when(s == 0)
        def _():
            for cb in range(nblk):
                sm_scr[cb] = part[:, cb * LANES:(cb + 1) * LANES]

        @pl.when(s > 0)
        def _():
            for cb in range(nblk):
                sm_scr[cb] += part[:, cb * LANES:(cb + 1) * LANES]

    @pl.when(s == S5_SPLIT - 1)
    def _():
        ar = ar_ref[0]
        ai = ai_ref[0]
        h0 = h0_ref[0]
        blk = lambda a, cb: a[:, cb * LANES:(cb + 1) * LANES]

        def body(n, carry):
            rows_f = pl.ds(n, n_seq, stride=n_chunks)
            rows_b = pl.ds(n_chunks - 1 - n, n_seq, stride=n_chunks)
            new = list(carry)
            for d, rows in ((0, rows_f), (1, rows_b)):
                for c in range(sb):
                    re_i = d * sb + c
                    im_i = (2 + d) * sb + c
                    hr, hi = carry[re_i], carry[im_i]
                    hp_scr[re_i, rows, :] = hr
                    hp_scr[im_i, rows, :] = hi
                    a_r, a_i = blk(ar, re_i), blk(ai, re_i)
                    new[re_i] = a_r * hr - a_i * hi + sm_scr[re_i, rows, :]
                    new[im_i] = a_r * hi + a_i * hr + sm_scr[im_i, rows, :]
            return tuple(new)

        fin = lax.fori_loop(0, n_chunks, body, tuple(blk(h0, cb) for cb in range(nblk)))
        hf_ref[0] = jnp.concatenate(fin, axis=1)
        for cb in range(nblk):
            hpb_scr[:, cb * LANES:(cb + 1) * LANES] = hp_scr[cb].astype(BF16)

    @pl.when(s >= S5_SPLIT)
    def _():
        ub = jnp.concatenate([ub_scr[k] for k in range(S5_SPLIT)], axis=1)
        y = _dot(ub, t_scr[jnp.maximum(s - S5_SPLIT, 0)]) + _dot(hpb_scr[...], cq_ref[0])
        dd = d_ref[0]
        for ii in range(per):
            rows = pl.ds((s - S5_SPLIT) * per + ii, m, stride=q)
            yi = y[:, ii * LANES:(ii + 1) * LANES] + dd * u_ref[rows, :]
            y_ref[rows, :] = jax.nn.gelu(yi)


def _s5_expand_kernel(mc_ref, o_ref, *, xsize, ysize):
    xs, ys, gs = xsize.bit_length() - 1, ysize.bit_length() - 1, S5_GB.bit_length() - 1
    assert xsize == 1 << xs and ysize == 1 << ys and S5_GB == 1 << gs
    cw = o_ref.shape[2]
    nc = mc_ref.shape[2]
    col0 = pl.program_id(1) * cw
    r = lax.broadcasted_iota(jnp.int32, (nc, cw), 0)
    col = lax.broadcasted_iota(jnp.int32, (nc, cw), 1) + col0
    spread = jnp.logical_and(r >> ys == col >> (ys + gs), (r & (ysize - 1)) == (col & (ysize - 1)))
    big = _dot(mc_ref[0], jnp.where(spread, 1.0, 0.0).astype(BF16))
    row = lax.broadcasted_iota(jnp.int32, big.shape, 0)
    colb = lax.broadcasted_iota(jnp.int32, big.shape, 1) + col0
    same = ((row >> xs) & (S5_GB - 1)) == ((colb >> ys) & (S5_GB - 1))
    o_ref[0] = jnp.where(same, big, 0.0).astype(BF16)


def _s5_expand(mc, *, xsize, ysize):
    nb, rows, nc = mc.shape
    cols = nc * S5_GB
    cw = 512
    return pl.pallas_call(
        functools.partial(_s5_expand_kernel, xsize=xsize, ysize=ysize),
        grid=(nb, cols // cw),
        in_specs=[pl.BlockSpec((1, rows, nc), lambda b, j: (b, 0, 0))],
        out_specs=pl.BlockSpec((1, rows, cw), lambda b, j: (b, 0, j)),
        out_shape=jax.ShapeDtypeStruct((nb, rows, cols), BF16),
        compiler_params=_cparams(("parallel", "parallel")),
        name="s5_expand",
    )(mc)


def _s5_mats(lam_re, lam_im, log_dt, b_re, b_im, c_re, c_im, d):
    flat = lambda a: a.reshape((-1,) + a.shape[2:])
    kc, bq, cq, ar, ai, dd = map(flat, jax.vmap(_s5_compact)(lam_re, lam_im, log_dt, b_re, b_im, c_re, c_im, d))
    ch, p = S5_GROUP, S5_STATE
    return (_s5_expand(kc, xsize=ch, ysize=ch),
            _s5_expand(bq, xsize=ch, ysize=p),
            _s5_expand(cq, xsize=p, ysize=ch),
            ar, ai, dd)


def _s5_compact(lam_re, lam_im, log_dt, b_re, b_im, c_re, c_im, d):
    q, g, p, ch = S5_Q, S5_GROUPS, S5_STATE, S5_GROUP
    lam = lax.complex(jnp.minimum(lam_re.astype(F32), -1e-4), lam_im.astype(F32))
    ldt = lam * jnp.exp(log_dt.astype(F32))[..., None]
    lam_bar = jnp.exp(ldt)
    b_bar = ((lam_bar - 1.0) / lam)[..., None] * lax.complex(b_re.astype(F32), b_im.astype(F32))
    cc = lax.complex(c_re.astype(F32), c_im.astype(F32))
    pw = jnp.exp(ldt[..., None] * jnp.arange(q + 1, dtype=F32))
    hi = lax.Precision.HIGHEST
    lag = jnp.arange(2 * q, dtype=F32) - (q - 1)
    wf = jnp.where(lag >= 0, jnp.exp(ldt[0][..., None] * jnp.maximum(lag, 0.0)), 0.0)
    wb = jnp.where(lag <= 0, jnp.exp(ldt[1][..., None] * jnp.maximum(-lag, 0.0)), 0.0)
    kc = jnp.real(jnp.einsum('gcp,gpd,gpe->gedc', cc[0], wf, b_bar[0], precision=hi)
                  + jnp.einsum('gcp,gpd,gpe->gedc', cc[1], wb, b_bar[1], precision=hi))

    pw_dn = jnp.exp(ldt[..., None] * (q - jnp.arange(q + 1, dtype=F32)))
    bf = pw_dn[0][..., 1:][:, :, :, None] * b_bar[0][:, :, None, :]
    bb = pw[1][..., :q][:, :, :, None] * b_bar[1][:, :, None, :]
    to_rows = lambda m: m.transpose(0, 2, 3, 1).reshape(g, q * ch, p)
    bq = jnp.concatenate([to_rows(jnp.real(bf)), to_rows(jnp.real(bb)),
                          to_rows(jnp.imag(bf)), to_rows(jnp.imag(bb))], axis=-1)

    cf = cc[0].transpose(0, 2, 1)[:, :, None, :] * pw[0][..., 1:][:, :, :, None]
    cb = cc[1].transpose(0, 2, 1)[:, :, None, :] * pw_dn[1][..., :q][:, :, :, None]
    to_cols = lambda m: m.reshape(g, p, q * ch)
    cq = jnp.concatenate([to_cols(jnp.real(cf)), to_cols(jnp.real(cb)),
                          to_cols(-jnp.imag(cf)), to_cols(-jnp.imag(cb))], axis=1)

    gb, nb = S5_GB, g // S5_GB
    rows_of = lambda a, outer, inner: (a.reshape(nb, gb, outer, inner, a.shape[-1]).transpose(0, 2, 1, 3, 4)
                                       .reshape(nb, outer * gb * inner, a.shape[-1]).astype(BF16))
    kc = kc.reshape(nb, gb * ch, 2 * q * ch).astype(BF16)

    lq = pw[..., q].reshape(2, nb, 1, S5_SW)
    ar = jnp.concatenate([jnp.real(lq[0]), jnp.real(lq[1])], axis=-1)
    ai = jnp.concatenate([jnp.imag(lq[0]), jnp.imag(lq[1])], axis=-1)
    dd = d.astype(F32).reshape(nb, 1, LANES)
    return kc, rows_of(bq, q, ch), rows_of(cq, 4, p), ar, ai, dd


def _s5(proj, mats, h0_re, h0_im, *, row0, n_seq, seq_len, layer=0, dst=None):
    q, p = S5_Q, S5_STATE
    n_chunks = seq_len // q
    m = n_seq * n_chunks
    rows = n_seq * seq_len
    nb = S5_GROUPS // S5_GB
    nblk = S5_W // LANES
    kw = S5_W // S5_SPLIT
    tm, bq, cq, ar, ai, dd = mats
    part = lambda a: a.astype(F32).reshape(n_seq, nb, S5_SW)
    h0 = jnp.concatenate([part(h0_re[:, 0]), part(h0_re[:, 1]), part(h0_im[:, 0]), part(h0_im[:, 1])],
                         axis=-1).transpose(1, 0, 2)
    per_b = lambda shape: pl.BlockSpec((1,) + shape, lambda b, s: (b, 0, 0))
    b0 = layer * nb
    per_l = lambda shape: pl.BlockSpec((1,) + shape, lambda b, s: (b0 + b, 0, 0))
    body, dst_spec, dst_arg, alias = _into(functools.partial(_s5_kernel, n_seq=n_seq, n_chunks=n_chunks), 8, dst)
    y, hf = pl.pallas_call(
        body,
        grid=(nb, 2 * S5_SPLIT),
        in_specs=[
            pl.BlockSpec((rows, LANES), lambda b, s: (row0 // rows, U_COL // LANES + b)),
            pl.BlockSpec((1, kw, 4 * S5_SW), lambda b, s: (b0 + b, jnp.minimum(s, S5_SPLIT - 1), 0)),
            per_l((LANES, 2 * q * LANES)),
            pl.BlockSpec((1, 4 * S5_SW, kw), lambda b, s: (b0 + b, 0, jnp.maximum(s - S5_SPLIT, 0))),
            per_l((1, 2 * S5_SW)), per_l((1, 2 * S5_SW)), per_l((1, LANES)), per_b((n_seq, 4 * S5_SW)),
        ] + dst_spec,
        out_specs=[pl.BlockSpec((rows, LANES), lambda b, s: (row0 // rows, b)), per_b((n_seq, 4 * S5_SW))],
        out_shape=[jax.ShapeDtypeStruct((proj.shape[0], S5_WIDTH), F32),
                   jax.ShapeDtypeStruct((nb, n_seq, 4 * S5_SW), F32)],
        scratch_shapes=[pltpu.VMEM((S5_SPLIT, m, kw), BF16), pltpu.VMEM((nblk, m, LANES), F32),
                        pltpu.VMEM((nblk, m, LANES), F32), pltpu.VMEM((m, 4 * S5_SW), BF16),
                        pltpu.VMEM((S5_SPLIT, S5_W, kw), BF16)],
        input_output_aliases=alias,
        compiler_params=_cparams(("parallel", "arbitrary")),
        name="s5",
    )(proj, bq, tm, cq, ar, ai, dd, h0, *dst_arg)
    hf = hf.reshape(nb, n_seq, 4, S5_GB, p).transpose(1, 2, 0, 3, 4).reshape(n_seq, 4, S5_GROUPS, p)
    return y, hf[:, 0:2], hf[:, 2:4]


def _conv3(x, w, b):
    n = x.shape[0]
    row = lax.broadcasted_iota(jnp.int32, x.shape, 0)
    prev = jnp.where(row == 0, 0.0, pltpu.roll(x, 1, 0))
    nxt = jnp.where(row == n - 1, 0.0, pltpu.roll(x, n - 1, 0))
    return prev * w[0:1] + x * w[1:2] + nxt * w[2:3] + b


def _hy_fwd_kernel(x0_ref, x1_ref, v_ref, w0_ref, w1_ref, wv_ref, b0_ref, b1_ref, bv_ref,
                   fc_ref, fs_ref, m1_ref, m2_ref, m3_ref, p_ref, z_ref, x0c_ref, zb_scr):
    @pl.when(pl.program_id(2) == 0)
    def _():
        z = _conv3(x1_ref[...], w1_ref[...], b1_ref[...]) * _conv3(v_ref[...], wv_ref[...], bv_ref[...])
        z_ref[...] = z
        zb_scr[...] = z.astype(BF16)
        x0c_ref[...] = _conv3(x0_ref[...], w0_ref[...], b0_ref[...])

    zb = zb_scr[...]
    a = _dot(fc_ref[...], zb)
    b = _dot(fs_ref[...], zb)
    m2 = m2_ref[...]
    p_ref[0, 0] = (m1_ref[...] * a + m2 * b).astype(BF16)
    p_ref[0, 1] = (m3_ref[...] * b - m2 * a).astype(BF16)


def _hy_inv_kernel(p_ref, gc_ref, gs_ref, z_ref, x0c_ref, bias_ref, o_ref):
    conv = _dot(gc_ref[...], p_ref[0, 0]) + _dot(gs_ref[...], p_ref[0, 1])
    o_ref[...] = (x0c_ref[...] * (conv + bias_ref[...] * z_ref[...])).astype(o_ref.dtype)


def _hy_short_kernel(x0_ref, x1_ref, v_ref, w0_ref, w1_ref, wv_ref, b0_ref, b1_ref, bv_ref,
                     fc_ref, fs_ref, gs_ref, m1_ref, m2_ref, m3_ref, bias_ref, o_ref):
    z = _conv3(x1_ref[...], w1_ref[...], b1_ref[...]) * _conv3(v_ref[...], wv_ref[...], bv_ref[...])
    zb = z.astype(BF16)
    a = _dot(fc_ref[...], zb)
    b = _dot(fs_ref[...], zb)
    m2 = m2_ref[...]
    p_re = (m1_ref[...] * a + m2 * b).astype(BF16)
    p_im = (m3_ref[...] * b - m2 * a).astype(BF16)
    conv = _dot(fc_ref[...], p_re) + _dot(gs_ref[...], p_im)
    x0c = _conv3(x0_ref[...], w0_ref[...], b0_ref[...])
    o_ref[...] = (x0c * (conv + bias_ref[...] * z)).astype(o_ref.dtype)


def _dft_mats(seq_len):
    n, w = seq_len, 64
    k = jnp.arange(n, dtype=jnp.int32)
    ang = lambda j: ((k[:, None] * j[None, :]) % (2 * n)).astype(F32) * (math.pi / n)
    ang_a = ang(jnp.arange(n // w, dtype=jnp.int32) * w)
    ang_b = ang(jnp.arange(w, dtype=jnp.int32))
    ca, sa = jnp.cos(ang_a)[:, :, None], jnp.sin(ang_a)[:, :, None]
    cb, sb = jnp.cos(ang_b)[:, None, :], jnp.sin(ang_b)[:, None, :]
    cm = (ca * cb - sa * sb).reshape(n, n)
    sm = -(sa * cb + ca * sb).reshape(n, n)
    nyq = jnp.where(k % 2 == 0, 1.0, -1.0).astype(F32)
    return cm.astype(BF16), sm.at[0, :].set(nyq).astype(BF16), sm.at[:, 0].set(nyq).astype(BF16)


def _hy_filter_taps(seq_len, f1_w, f1_b, f2_w, f2_b, f3_w, f3_b, freq, decay):
    n = seq_len
    t = (jnp.arange(n, dtype=F32) / n)[:, None]
    bands = jnp.arange(1, HY_BANDS + 1, dtype=F32)[None, :]
    z = jnp.concatenate([t, jnp.cos(2.0 * math.pi * t * bands), jnp.sin(2.0 * math.pi * t * bands)], axis=-1)
    hi = lax.Precision.HIGHEST
    fr = freq.astype(F32)
    h = jnp.sin(fr * (jnp.dot(z, f1_w.astype(F32), precision=hi) + f1_b.astype(F32)))
    h = jnp.sin(fr * (jnp.dot(h, f2_w.astype(F32), precision=hi) + f2_b.astype(F32)))
    h = jnp.dot(h, f3_w.astype(F32), precision=hi) + f3_b.astype(F32)
    h = h * jnp.exp(-t * jnp.abs(decay.astype(F32)))
    h = h.reshape(n, 2, HY_WIDTH)
    h = h / jnp.sum(jnp.abs(h), axis=(0, 1), keepdims=True)
    return h.reshape(n, 2 * HY_WIDTH)


def _hy_spectrum_kernel(h_ref, cm_ref, sm_ref, ck_ref, sk_ref, m1_ref, m2_ref, m3_ref, *, seq_len):
    w = HY_WIDTH
    tk = cm_ref.shape[0]
    hb = h_ref[0].astype(BF16)
    xc = _dot(cm_ref[...], hb)
    xs = _dot(sm_ref[...], hb)
    k = pl.program_id(1) * tk + lax.broadcasted_iota(jnp.int32, (tk, 1), 0)
    first = k == 0
    sf = jnp.where(first, 0.0, -xs[:, :w])
    sb = jnp.where(first, 0.0, -xs[:, w:])
    ck, sk = ck_ref[...], sk_ref[...]
    hr = xc[:, :w] + ck * xc[:, w:] - sk * sb
    him = -sf + ck * sb + sk * xc[:, w:]
    nyq = xs[:, :w] - xs[:, w:]
    wk = jnp.where(first, 1.0, 2.0) / (2.0 * seq_len)
    m1_ref[0] = hr * wk
    m2_ref[0] = jnp.where(first, 0.0, -him) * wk
    m3_ref[0] = jnp.where(first, nyq, hr) * wk


def _hy_spectrum(h, dft):
    depth, n, _ = h.shape
    cm, sm, _ = dft
    tk = min(n, 512)
    ang = jnp.arange(n, dtype=F32)[:, None] * (math.pi / n)
    out = jax.ShapeDtypeStruct((depth, n, HY_WIDTH), F32)
    oblk = pl.BlockSpec((1, tk, HY_WIDTH), lambda l, k: (l, k, 0))
    return pl.pallas_call(
        functools.partial(_hy_spectrum_kernel, seq_len=n),
        grid=(depth, n // tk),
        in_specs=[pl.BlockSpec((1, n, 2 * HY_WIDTH), lambda l, k: (l, 0, 0)),
                  pl.BlockSpec((tk, n), lambda l, k: (k, 0)), pl.BlockSpec((tk, n), lambda l, k: (k, 0)),
                  pl.BlockSpec((tk, 1), lambda l, k: (k, 0)), pl.BlockSpec((tk, 1), lambda l, k: (k, 0))],
        out_specs=[oblk, oblk, oblk],
        out_shape=[out, out, out],
        compiler_params=_cparams(("parallel", "parallel")),
        name="hyena_spectrum",
    )(h, cm, sm, jnp.cos(ang), jnp.sin(ang))


def _hyena(proj, conv_w, conv_b, bias, dft, mults, *, row0, n_seq, seq_len, cb, tk, layer=0, dst=None):
    blk0 = row0 // seq_len
    nc = HY_WIDTH // cb
    nk = seq_len // tk
    c0 = HY_COL // cb
    cm, sm, smt = dft
    m1, m2, m3 = mults
    xcol = lambda part: pl.BlockSpec((seq_len, cb), lambda b, c, k: (blk0 + b, c0 + part * nc + c))
    wcol = lambda part: pl.BlockSpec((3, cb), lambda b, c, k: (0, part * nc + c))
    bcol = lambda part: pl.BlockSpec((1, cb), lambda b, c, k: (0, part * nc + c))
    frow = pl.BlockSpec((tk, seq_len), lambda b, c, k: (k, 0))
    mblk = pl.BlockSpec((None, tk, cb), lambda b, c, k: (layer, k, c))
    cb2 = conv_b.reshape(1, 3 * HY_WIDTH)
    if nk == 1:
        body, dst_spec, dst_arg, alias = _into(_hy_short_kernel, 16, dst)
        return pl.pallas_call(
            body,
            grid=(n_seq, nc, 1),
            in_specs=[xcol(0), xcol(1), xcol(2), wcol(0), wcol(1), wcol(2), bcol(0), bcol(1), bcol(2),
                      frow, frow, frow, mblk, mblk, mblk,
                      pl.BlockSpec((1, cb), lambda b, c, k: (0, c))] + dst_spec,
            out_specs=pl.BlockSpec((seq_len, cb), lambda b, c, k: (blk0 + b, c)),
            out_shape=jax.ShapeDtypeStruct((proj.shape[0], HY_WIDTH), BF16),
            input_output_aliases=alias,
            compiler_params=_cparams(("parallel", "parallel", "arbitrary")),
            name="hyena_short",
        )(proj, proj, proj, conv_w, conv_w, conv_w, cb2, cb2, cb2, cm, sm, smt, m1, m2, m3,
          bias.reshape(1, HY_WIDTH), *dst_arg)
    pspec, z, x0c = pl.pallas_call(
        _hy_fwd_kernel,
        grid=(n_seq, nc, nk),
        in_specs=[xcol(0), xcol(1), xcol(2), wcol(0), wcol(1), wcol(2), bcol(0), bcol(1), bcol(2),
                  frow, frow, mblk, mblk, mblk],
        out_specs=[
            pl.BlockSpec((1, 2, tk, cb), lambda b, c, k: (b, 0, k, c)),
            pl.BlockSpec((seq_len, cb), lambda b, c, k: (b, c)),
            pl.BlockSpec((seq_len, cb), lambda b, c, k: (b, c)),
        ],
        out_shape=[
            jax.ShapeDtypeStruct((n_seq, 2, seq_len, HY_WIDTH), BF16),
            jax.ShapeDtypeStruct((n_seq * seq_len, HY_WIDTH), F32),
            jax.ShapeDtypeStruct((n_seq * seq_len, HY_WIDTH), F32),
        ],
        scratch_shapes=[pltpu.VMEM((seq_len, cb), BF16)],
        compiler_params=_cparams(("parallel", "parallel", "arbitrary")),
        name="hyena_fwd",
    )(proj, proj, proj, conv_w, conv_w, conv_w, cb2, cb2, cb2, cm, sm, m1, m2, m3)
    grow = pl.BlockSpec((tk, seq_len), lambda b, c, k: (k, 0))
    tile = pl.BlockSpec((tk, cb), lambda b, c, k: (b * nk + k, c))
    body, dst_spec, dst_arg, alias = _into(_hy_inv_kernel, 6, dst)
    return pl.pallas_call(
        body,
        grid=(n_seq, nc, nk),
        in_specs=[pl.BlockSpec((1, 2, seq_len, cb), lambda b, c, k: (b, 0, 0, c)),
                  grow, grow, tile, tile, pl.BlockSpec((1, cb), lambda b, c, k: (0, c))] + dst_spec,
        out_specs=pl.BlockSpec((tk, cb), lambda b, c, k: (row0 // tk + b * nk + k, c)),
        out_shape=jax.ShapeDtypeStruct((proj.shape[0], HY_WIDTH), BF16),
        input_output_aliases=alias,
        compiler_params=_cparams(("parallel", "parallel", "arbitrary")),
        name="hyena_inv",
    )(pspec, cm, smt, z, x0c, bias.reshape(1, HY_WIDTH), *dst_arg)


def _out_kernel(x_ref, ret_ref, s5_ref, hy_ref, mod_ref, g_ref, gw_ref, gb_ref,
                wr_ref, ws_ref, wh_ref, rt_ref, xo_ref, h_ref, lg_ref):
    m = mod_ref[0]
    y = s5_ref[...]
    s5o = y * jax.nn.sigmoid(_dot(y.astype(BF16), gw_ref[...]) + gb_ref[...])
    mix = (_dot(ret_ref[...], wr_ref[...]) + _dot(s5o.astype(BF16), ws_ref[...])
           + _dot(hy_ref[...], wh_ref[...]))
    x = x_ref[...] + m[2:3] * mix
    xo_ref[...] = x
    h = _rms(x, g_ref[...]) * (1.0 + m[4:5]) + m[3:4]
    hb = h.astype(BF16)
    _store_token_tiles(h_ref, _pack_halves(hb))
    lg_ref[...] = lax.dot_general(rt_ref[...], hb, (((1,), (1,)), ((), ())), preferred_element_type=F32)


def _out_proj(x, ret_o, s5_y, hy_o, mod, g, glu_w, glu_b, w_out, router, n_ctx_rows, dec_seq):
    t = x.shape[0]
    tm = ROW_TILE
    grp = functools.partial(_group_of, tile=tm, n_ctx_rows=n_ctx_rows, dec_seq=dec_seq)
    row = lambda w: pl.BlockSpec((tm, w), lambda i: (i, 0))
    full = lambda a, b: pl.BlockSpec((a, b), lambda i: (0, 0))
    wo = w_out.astype(BF16)
    return pl.pallas_call(
        _out_kernel,
        grid=(t // tm,),
        in_specs=[row(D_MODEL), row(RET_WIDTH), row(S5_WIDTH), row(HY_WIDTH),
                  pl.BlockSpec((1, 6, D_MODEL), lambda i: (grp(i), 0, 0)),
                  full(1, D_MODEL), full(S5_WIDTH, S5_WIDTH), full(1, S5_WIDTH),
                  full(RET_WIDTH, D_MODEL), full(S5_WIDTH, D_MODEL), full(HY_WIDTH, D_MODEL),
                  full(N_EXPERTS, D_MODEL)],
        out_specs=[row(D_MODEL), pl.BlockSpec((tm * TOKEN_ROWS, LANES), lambda i: (i, 0)),
                   pl.BlockSpec((N_EXPERTS, tm), lambda i: (0, i))],
        out_shape=[jax.ShapeDtypeStruct((t, D_MODEL), F32), jax.ShapeDtypeStruct((t * TOKEN_ROWS, LANES), jnp.uint32),
                   jax.ShapeDtypeStruct((N_EXPERTS, t), F32)],
        compiler_params=_cparams(("parallel",)),
        name="out_proj",
    )(x, ret_o, s5_y, hy_o, mod, g.reshape(1, D_MODEL), glu_w.astype(BF16), glu_b.reshape(1, S5_WIDTH),
      wo[:RET_WIDTH], wo[RET_WIDTH:RET_WIDTH + S5_WIDTH], wo[RET_WIDTH + S5_WIDTH:], router.T.astype(BF16))


def _moe_kernel(be_ref, first_ref, slot_ref, nxt_ref, nu_ref, xs_ref, wg_hbm, wu_hbm, wd_hbm, o_ref,
                wg_f, wu_f, wd_f, wg_b, wu_b, wd_b, sem, *, layer):
    i = pl.program_id(0)

    def copies(e, s):
        return (pltpu.make_async_copy(wg_hbm.at[layer, e], wg_f.at[s], sem.at[s, 0]),
                pltpu.make_async_copy(wu_hbm.at[layer, e], wu_f.at[s], sem.at[s, 1]),
                pltpu.make_async_copy(wd_hbm.at[layer, e], wd_f.at[s], sem.at[s, 2]))

    @pl.when(i == 0)
    def _():
        for cp in copies(be_ref[0], 0):
            cp.start()

    @pl.when(first_ref[i] == 1)
    def _():
        s = slot_ref[i]
        for cp in copies(be_ref[i], s):
            cp.wait()

        @pl.when(nxt_ref[i] >= 0)
        def _():
            for cp in copies(nxt_ref[i], 1 - s):
                cp.start()

        wg_b[...] = wg_f[s].astype(BF16)
        wu_b[...] = wu_f[s].astype(BF16)
        wd_b[...] = wd_f[s].astype(BF16)

    @pl.when(i < nu_ref[0])
    def _():
        half = D_MODEL // 2
        x_lo, x_hi = _unpack_halves(_load_token_tiles(xs_ref, MOE_BM))
        x_lo = x_lo.astype(BF16)
        x_hi = x_hi.astype(BF16)
        gate = _dot(x_lo, wg_b[0:half, :]) + _dot(x_hi, wg_b[half:, :])
        up = _dot(x_lo, wu_b[0:half, :]) + _dot(x_hi, wu_b[half:, :])
        hb = gate * jax.nn.sigmoid(gate) * up
        _store_token_tiles(o_ref, _pack_halves(_dot(hb.astype(BF16), wd_b[...]).astype(BF16)))

    @pl.when(i >= nu_ref[0])
    def _():
        o_ref[...] = jnp.zeros_like(o_ref)


def _moe_grouped(xs, blk_e, first, slot, nxt, n_used, w_gate, w_up, w_down, layer):
    pr = xs.shape[0] // TOKEN_ROWS
    bm = MOE_BM
    nb = pr // bm
    grid_spec = pltpu.PrefetchScalarGridSpec(
        num_scalar_prefetch=5,
        grid=(nb,),
        in_specs=[
            pl.BlockSpec((bm * TOKEN_ROWS, LANES), lambda i, *_: (i, 0)),
            pl.BlockSpec(memory_space=pl.ANY),
            pl.BlockSpec(memory_space=pl.ANY),
            pl.BlockSpec(memory_space=pl.ANY),
        ],
        out_specs=pl.BlockSpec((bm * TOKEN_ROWS, LANES), lambda i, *_: (i, 0)),
        scratch_shapes=[pltpu.VMEM((2, D_MODEL, D_EXPERT), F32), pltpu.VMEM((2, D_MODEL, D_EXPERT), F32),
                        pltpu.VMEM((2, D_EXPERT, D_MODEL), F32),
                        pltpu.VMEM((D_MODEL, D_EXPERT), BF16), pltpu.VMEM((D_MODEL, D_EXPERT), BF16),
                        pltpu.VMEM((D_EXPERT, D_MODEL), BF16),
                        pltpu.SemaphoreType.DMA((2, 3))],
    )
    return pl.pallas_call(
        functools.partial(_moe_kernel, layer=layer),
        grid_spec=grid_spec,
        out_shape=jax.ShapeDtypeStruct((pr * TOKEN_ROWS, LANES), jnp.uint32),
        compiler_params=_cparams(("arbitrary",)),
        name="moe_grouped",
    )(blk_e, first, slot, nxt, n_used, xs, w_gate, w_up, w_down)


DISPATCH_TILE = 1024


def _dispatch_kernel(info_ref, nu_ref, pos_ref, h_ref, xs_out, zbuf, sem, zsem, *, nb):
    tm = pos_ref.shape[1]
    tr = TOKEN_ROWS
    bm = MOE_BM

    @pl.when(pl.program_id(0) == 0)
    def _():
        zbuf[...] = jnp.zeros_like(zbuf)

        def zero_block(first_row):
            rows = pl.ds(pl.multiple_of(first_row * tr, tr), bm * tr)
            return pltpu.make_async_copy(zbuf, xs_out.at[rows], zsem)

        for act in ("start", "wait"):
            def last_of_expert(e, carry, act=act):
                @pl.when(info_ref[e, 3] > info_ref[e, 1])
                def _():
                    getattr(zero_block(info_ref[e, 3] - bm), act)()
                return carry

            def tail_block(b, carry, act=act):
                getattr(zero_block(b * bm), act)()
                return carry

            lax.fori_loop(0, N_EXPERTS, last_of_expert, 0)
            lax.fori_loop(nu_ref[0], nb, tail_block, 0)

    def send(tok, carry):
        src = h_ref.at[pl.ds(pl.multiple_of(tok * tr, tr), tr)]
        for k in range(TOP_K):
            row = pl.multiple_of(pos_ref[k, tok], tr)
            pltpu.make_async_copy(src, xs_out.at[pl.ds(row, tr)], sem).start(priority=k % 2)
        return carry

    lax.fori_loop(0, tm, send, 0, unroll=4)
    n = tm * TOP_K * tr
    pltpu.make_async_copy(xs_out.at[pl.ds(0, n)], xs_out.at[pl.ds(0, n)], sem).wait()


def _dispatch(h2, pos, info, n_used, nb):
    t = pos.shape[1]
    tm = DISPATCH_TILE
    return pl.pallas_call(
        functools.partial(_dispatch_kernel, nb=nb),
        grid=(t // tm,),
        in_specs=[pl.BlockSpec(memory_space=pltpu.SMEM),
                  pl.BlockSpec(memory_space=pltpu.SMEM),
                  pl.BlockSpec((SUBLANES, tm), lambda i: (0, i), memory_space=pltpu.SMEM),
                  pl.BlockSpec((tm * TOKEN_ROWS, LANES), lambda i: (i, 0))],
        out_specs=pl.BlockSpec(memory_space=pl.ANY),
        out_shape=jax.ShapeDtypeStruct((nb * MOE_BM * TOKEN_ROWS, LANES), h2.dtype),
        scratch_shapes=[pltpu.VMEM((MOE_BM * TOKEN_ROWS, LANES), h2.dtype),
                        pltpu.SemaphoreType.DMA(()), pltpu.SemaphoreType.DMA(())],
        compiler_params=pltpu.CompilerParams(dimension_semantics=("arbitrary",)),
        name="dispatch",
    )(info, n_used, pos, h2)


ROUTE_TILE = 512


def _router_kernel(lg_ref, bias_ref, pos_ref, gate_ref, be_ref, info_ref, rank_scr, ek_scr, *, t, nbp):
    tl = ROUTE_TILE
    ne = N_EXPERTS
    bm = MOE_BM
    row = lax.broadcasted_iota(jnp.int32, (ne, tl), 0)
    tri = (lax.broadcasted_iota(jnp.int32, (tl, tl), 0) < lax.broadcasted_iota(jnp.int32, (tl, tl), 1)).astype(BF16)
    bias = bias_ref[...]

    def select(i, counts):
        cols = pl.ds(pl.multiple_of(i * tl, tl), tl)
        s = jax.nn.sigmoid(lg_ref[:, cols])
        sel = s + bias
        mask = jnp.zeros((ne, tl), F32)
        vals = []
        for k in range(TOP_K):
            best = jnp.max(sel, axis=0, keepdims=True)
            idx = jnp.min(jnp.where(sel == best, row, ne), axis=0, keepdims=True)
            hit = row == idx
            vals.append(jnp.sum(jnp.where(hit, s, 0.0), axis=0, keepdims=True))
            sel = jnp.where(hit, -jnp.inf, sel)
            mask = jnp.where(hit, 1.0, mask)
            ek_scr[k:k + 1, cols] = idx
        total = vals[0]
        for v in vals[1:]:
            total = total + v
        scale = ROUTED_SCALE / total
        for k in range(TOP_K):
            gate_ref[k:k + 1, cols] = vals[k] * scale
        gate_ref[TOP_K:SUBLANES, cols] = jnp.zeros((SUBLANES - TOP_K, tl), F32)
        rank_scr[:, cols] = _dot(mask.astype(BF16), tri) + counts
        return counts + jnp.sum(mask, axis=1, keepdims=True)

    counts = lax.fori_loop(0, t // tl, select, jnp.zeros((ne, 1), F32))
    counts = counts.astype(jnp.int32)
    shift = bm.bit_length() - 1
    assert bm == 1 << shift
    padded = ((counts + (bm - 1)) >> shift) << shift
    e0 = lax.broadcasted_iota(jnp.int32, (ne, ne), 0)
    e1 = lax.broadcasted_iota(jnp.int32, (ne, ne), 1)
    padded_row = jnp.sum(jnp.where(e0 == e1, padded, 0), axis=0, keepdims=True)
    counts_row = jnp.sum(jnp.where(e0 == e1, counts, 0), axis=0, keepdims=True)
    pstart = jnp.sum(jnp.where(e1 < e0, padded_row, 0), axis=1, keepdims=True)
    ustart = jnp.sum(jnp.where(e1 < e0, counts_row, 0), axis=1, keepdims=True)
    pend = pstart + padded
    lane = lax.broadcasted_iota(jnp.int32, (ne, LANES), 1)
    info_ref[...] = jnp.where(lane == 0, counts, jnp.where(lane == 1, pstart, jnp.where(lane == 2, ustart, pend)))
    blk = lax.broadcasted_iota(jnp.int32, (ne, nbp), 1) * bm
    owner = jnp.sum(jnp.where(pend <= blk, 1, 0), axis=0, keepdims=True)
    be_ref[...] = jnp.minimum(owner, ne - 1)
    pstart_f = pstart.astype(F32)

    def place(i, carry):
        cols = pl.ds(pl.multiple_of(i * tl, tl), tl)
        dest = rank_scr[:, cols] + pstart_f
        for k in range(TOP_K):
            hit = row == ek_scr[k:k + 1, cols]
            pos_ref[k:k + 1, cols] = (jnp.sum(jnp.where(hit, dest, 0.0), axis=0, keepdims=True)
                                      * float(TOKEN_ROWS)).astype(jnp.int32)
        pos_ref[TOP_K:SUBLANES, cols] = jnp.zeros((SUBLANES - TOP_K, tl), jnp.int32)
        return carry

    lax.fori_loop(0, t // tl, place, 0)


def _router(logits_t, router_bias, nb):
    t = logits_t.shape[1]
    nbp = -(-nb // LANES) * LANES
    return pl.pallas_call(
        functools.partial(_router_kernel, t=t, nbp=nbp),
        out_shape=[jax.ShapeDtypeStruct((SUBLANES, t), jnp.int32), jax.ShapeDtypeStruct((SUBLANES, t), F32),
                   jax.ShapeDtypeStruct((1, nbp), jnp.int32), jax.ShapeDtypeStruct((N_EXPERTS, LANES), jnp.int32)],
        scratch_shapes=[pltpu.VMEM((N_EXPERTS, t), F32), pltpu.VMEM((SUBLANES, t), jnp.int32)],
        compiler_params=pltpu.CompilerParams(vmem_limit_bytes=VMEM_LIMIT),
        name="router",
    )(logits_t, router_bias.astype(F32).reshape(N_EXPERTS, 1))


def _dispatch_plan(blk_e_row, info, nb):
    bm = MOE_BM
    pend = info[:, 3]
    blk_e = blk_e_row[0, :nb]
    n_used = pend[-1] // bm
    blk = jnp.arange(nb, dtype=jnp.int32)
    prev_e = jnp.concatenate([jnp.full((1,), -1, jnp.int32), blk_e[:-1]])
    first = jnp.logical_and(blk < n_used, blk_e != prev_e)
    slot = (jnp.cumsum(first.astype(jnp.int32)) - 1) % 2
    first_at = jnp.where(first, blk, nb)
    nxt_first = lax.cummin(jnp.concatenate([first_at[1:], jnp.full((1,), nb, jnp.int32)]), reverse=True)
    nxt = jnp.where(nxt_first < nb, blk_e[jnp.minimum(nxt_first, nb - 1)], -1)
    return (blk_e, first.astype(jnp.int32), slot.astype(jnp.int32), nxt.astype(jnp.int32),
            n_used.astype(jnp.int32).reshape(1))


def _shared_kernel(pos_ref, nxt_ref, x_ref, h_ref, gt_ref, mod_ref, sg_ref, su_ref, sd_ref, fn_ref, eo_hbm,
                   o_ref, buf0, buf1, acc_scr, sem, *, final):
    i = pl.program_id(0)
    n = pl.num_programs(0)
    tm = x_ref.shape[0]
    tr = TOKEN_ROWS

    half = D_MODEL // 2
    grp = SUBLANES

    def fetch_tokens(idx_ref, buf, slot, tok0):
        for tt in range(grp):
            tok = tok0 + tt
            for k in range(TOP_K):
                row = pl.multiple_of(idx_ref[k, tok], tr)
                pltpu.make_async_copy(eo_hbm.at[pl.ds(row, tr)], buf.at[k, pl.ds(pl.multiple_of(tok * tr, tr), tr)],
                                      sem.at[slot]).start(priority=k % 2)

    def arrived(buf, slot):
        for k in range(TOP_K):
            pltpu.make_async_copy(eo_hbm.at[pl.ds(0, tm * tr)], buf.at[k], sem.at[slot]).wait()

    def step(cur, cur_slot, nxt, nxt_slot):
        arrived(cur, cur_slot)
        h_lo, h_hi = _unpack_halves(_load_token_tiles(h_ref, tm))
        h_lo = h_lo.astype(BF16)
        h_hi = h_hi.astype(BF16)
        gate = _dot(h_lo, sg_ref[0:half, :]) + _dot(h_hi, sg_ref[half:, :])
        up = _dot(h_lo, su_ref[0:half, :]) + _dot(h_hi, su_ref[half:, :])
        act = gate * jax.nn.sigmoid(gate) * up
        acc_scr[...] = _dot(act.astype(BF16), sd_ref[...])

        def group(c, carry):
            tok0 = pl.multiple_of(c * grp, grp)
            fetch_tokens(nxt_ref, nxt, nxt_slot, tok0)
            gt = gt_ref[pl.ds(tok0, grp), :]
            r_lo = jnp.zeros((grp, half), F32)
            r_hi = r_lo
            for k in range(TOP_K):
                words = jnp.concatenate(
                    [cur[k, pl.ds(tok0 * tr + s, grp, stride=tr), :] for s in range(tr)], axis=1)
                e_lo, e_hi = _unpack_halves(words)
                r_lo = r_lo + gt[:, k:k + 1] * e_lo
                r_hi = r_hi + gt[:, k:k + 1] * e_hi
            acc_scr[pl.ds(tok0, grp), :] += jnp.concatenate([r_lo, r_hi], axis=1)
            return carry

        lax.fori_loop(0, tm // grp, group, 0)
        x = x_ref[...] + mod_ref[0][5:6] * acc_scr[...]
        if final:
            x = _rms(x, fn_ref[...])
        o_ref[...] = x

        @pl.when(i == n - 1)
        def _():
            arrived(nxt, nxt_slot)

    @pl.when(i == 0)
    def _():
        def first(c, carry):
            fetch_tokens(pos_ref, buf0, 0, pl.multiple_of(c * grp, grp))
            return carry
        lax.fori_loop(0, tm // grp, first, 0)

    for parity, cur, nxt in ((0, buf0, buf1), (1, buf1, buf0)):
        @pl.when(i % 2 == parity)
        def _(parity=parity, cur=cur, nxt=nxt):
            step(cur, parity, nxt, 1 - parity)


def _shared(x, h, gates_t, pos, eo, mod, sg, su, sd, final_norm, n_ctx_rows, dec_seq, *, final, row0=0, rows=None):
    rows = x.shape[0] if rows is None else rows
    tm = ROW_TILE // 2
    b0 = row0 // tm
    steps = rows // tm
    grp = lambda i: _group_of(i + b0, tm, n_ctx_rows, dec_seq)
    row = pl.BlockSpec((tm, D_MODEL), lambda i: (i + b0, 0))
    prow = pl.BlockSpec((tm * TOKEN_ROWS, LANES), lambda i: (i + b0, 0))
    full = lambda a, b: pl.BlockSpec((a, b), lambda i: (0, 0))
    buf = pltpu.VMEM((TOP_K, tm * TOKEN_ROWS, LANES), eo.dtype)
    return pl.pallas_call(
        functools.partial(_shared_kernel, final=final),
        grid=(steps,),
        in_specs=[pl.BlockSpec((SUBLANES, tm), lambda i: (0, i + b0), memory_space=pltpu.SMEM),
                  pl.BlockSpec((SUBLANES, tm), lambda i: (0, jnp.minimum(i + 1, steps - 1) + b0),
                               memory_space=pltpu.SMEM),
                  row, prow, pl.BlockSpec((tm, SUBLANES), lambda i: (i + b0, 0)),
                  pl.BlockSpec((1, 6, D_MODEL), lambda i: (grp(i), 0, 0)),
                  full(D_MODEL, D_SHARED), full(D_MODEL, D_SHARED), full(D_SHARED, D_MODEL), full(1, D_MODEL),
                  pl.BlockSpec(memory_space=pl.ANY)],
        out_specs=pl.BlockSpec((tm, D_MODEL), lambda i: (i, 0)),
        out_shape=jax.ShapeDtypeStruct((rows, D_MODEL), F32),
        scratch_shapes=[buf, buf, pltpu.VMEM((tm, D_MODEL), F32), pltpu.SemaphoreType.DMA((2,))],
        compiler_params=_cparams(("arbitrary",)),
        name="shared_final" if final else "shared",
    )(pos, pos, x, h, gates_t, mod, sg.astype(BF16), su.astype(BF16), sd.astype(BF16),
      final_norm.reshape(1, D_MODEL), eo)


def kernel(x_prompt, x_sample, state_ret, state_s5_re, state_s5_im, c, c_ctx, w_ada, b_ada, norm_mix, norm_ffn, w_in, w_out, ret_decay, s5_lam_re, s5_lam_im, s5_log_dt, s5_b_re, s5_b_im, s5_c_re, s5_c_im, s5_d, s5_glu_w, s5_glu_b, hy_conv_w, hy_conv_b, hy_f1_w, hy_f1_b, hy_f2_w, hy_f2_b, hy_f3_w, hy_f3_b, hy_freq, hy_decay, hy_bias, moe_router, moe_router_bias, moe_w_gate, moe_w_up, moe_w_down, sh_w_gate, sh_w_up, sh_w_down, final_norm):
    n_ctx, seq, d = x_prompt.shape
    n_dec, dec_seq, _ = x_sample.shape
    n_ctx_rows = n_ctx * seq
    t = n_ctx_rows + n_dec * dec_seq

    x = jnp.concatenate([x_prompt.reshape(n_ctx_rows, d), x_sample.reshape(n_dec * dec_seq, d)], axis=0)
    cond = jnp.concatenate([c_ctx[None, :], c], axis=0)
    cond8 = jnp.pad(cond, ((0, SUBLANES - cond.shape[0]), (0, 0)))
    mods = _ada(cond8, w_ada, b_ada)[:, :1 + n_dec].reshape(DEPTH, 1 + n_dec, 6, d)

    cos2, sin2 = _rope_tables(dec_seq)
    no_rope = jnp.zeros((seq, LANES), F32)
    zero_ret = jnp.zeros((1, 2, RET_HEADS, RET_DK, RET_DV), F32)
    zero_s5 = jnp.zeros((n_ctx, 2, S5_GROUPS, S5_STATE), F32)
    dft_ctx = _dft_mats(seq)
    dft_dec = _dft_mats(dec_seq)

    w_in_bf = w_in.astype(BF16)
    mats = _s5_mats(s5_lam_re, s5_lam_im, s5_log_dt, s5_b_re, s5_b_im, s5_c_re, s5_c_im, s5_d)
    filt = (hy_f1_w, hy_f1_b, hy_f2_w, hy_f2_b, hy_f3_w, hy_f3_b, hy_freq, hy_decay)
    mults_ctx = _hy_spectrum(jax.vmap(functools.partial(_hy_filter_taps, seq))(*filt), dft_ctx)
    mults_dec = _hy_spectrum(jax.vmap(functools.partial(_hy_filter_taps, dec_seq))(*filt), dft_dec)

    ret_states = jnp.zeros((n_ctx, DEPTH, 2, RET_HEADS, RET_DK, RET_DV), F32)
    s5r_list, s5i_list = [], []
    for l in range(DEPTH):
        mod = mods[l]
        proj = _in_proj(x, mod, norm_mix[l], w_in_bf, l, n_ctx_rows, dec_seq)

        log_gamma = jax.nn.log_sigmoid(ret_decay[l].astype(F32))
        ret_o, ret_states = _retention(proj, log_gamma, zero_ret, no_rope, no_rope,
                                       row0=0, n_seq=n_ctx, seq_len=seq, hb=RET_HEADS, rope=False,
                                       dst=jnp.zeros((t, RET_WIDTH), BF16), states=ret_states, layer=l)
        ret_o, _ = _retention(proj, log_gamma, state_ret[:, l].astype(F32), cos2, sin2,
                              row0=n_ctx_rows, n_seq=n_dec, seq_len=dec_seq, hb=2, rope=True, dst=ret_o)

        s5_y, s5_re, s5_im = _s5(proj, mats, zero_s5, zero_s5, row0=0, n_seq=n_ctx, seq_len=seq, layer=l,
                                 dst=jnp.zeros((t, S5_WIDTH), F32))
        s5_y, _, _ = _s5(proj, mats, state_s5_re[:, l], state_s5_im[:, l],
                         row0=n_ctx_rows, n_seq=n_dec, seq_len=dec_seq, layer=l, dst=s5_y)
        s5r_list.append(s5_re)
        s5i_list.append(s5_im)

        hy_o = _hyena(proj, hy_conv_w[l], hy_conv_b[l], hy_bias[l], dft_ctx, mults_ctx, layer=l,
                      row0=0, n_seq=n_ctx, seq_len=seq, cb=HY_WIDTH, tk=seq, dst=jnp.zeros((t, HY_WIDTH), BF16))
        hy_o = _hyena(proj, hy_conv_w[l], hy_conv_b[l], hy_bias[l], dft_dec, mults_dec, layer=l,
                      row0=n_ctx_rows, n_seq=n_dec, seq_len=dec_seq, cb=HY_WIDTH // 2, tk=512, dst=hy_o)

        x, h2, logits = _out_proj(x, ret_o, s5_y, hy_o, mod, norm_ffn[l], s5_glu_w[l], s5_glu_b[l],
                                  w_out[l], moe_router[l], n_ctx_rows, dec_seq)

        nb = -(-(t * TOP_K) // MOE_BM) + N_EXPERTS
        pos, gates, blk_e_row, info = _router(logits, moe_router_bias[l], nb)
        blk_e, first, slot, nxt, n_used = _dispatch_plan(blk_e_row, info, nb)
        xs = _dispatch(h2, pos, info, n_used, nb)
        eo = _moe_grouped(xs, blk_e, first, slot, nxt, n_used, moe_w_gate, moe_w_up, moe_w_down, l)
        gates_t = gates.T

        sh = (sh_w_gate[l], sh_w_up[l], sh_w_down[l])
        if l < DEPTH - 1:
            x = _shared(x, h2, gates_t, pos, eo, mod, *sh, final_norm, n_ctx_rows, dec_seq, final=False)
        else:
            y_c = _shared(x, h2, gates_t, pos, eo, mod, *sh, final_norm, n_ctx_rows, dec_seq, final=True,
                          row0=0, rows=n_ctx_rows)
            y_d = _shared(x, h2, gates_t, pos, eo, mod, *sh, final_norm, n_ctx_rows, dec_seq, final=True,
                          row0=n_ctx_rows, rows=n_dec * dec_seq)

    return (y_c.reshape(n_ctx, seq, d), y_d.reshape(n_dec, dec_seq, d),
            ret_states, jnp.stack(s5r_list, axis=1), jnp.stack(s5i_list, axis=1))
```

```python
import functools
import math

import jax
import jax.numpy as jnp
from jax import lax
from jax.experimental import pallas as pl
from jax.experimental.pallas import tpu as pltpu

F32 = jnp.float32
BF16 = jnp.bfloat16

D_MODEL = 2048
DEPTH = 2
GRID_W = 64
RET_HEADS = 8
RET_DK = 128
RET_DV = 128
RET_WIDTH = RET_HEADS * RET_DV
RET_CHUNK = 256
ROPE_BASE = 10000.0
S5_WIDTH = 512
S5_GROUP = 16
S5_GROUPS = S5_WIDTH // S5_GROUP
S5_STATE = 64
S5_Q = 16
HY_WIDTH = 512
HY_BANDS = 16
IN_WIDTH = 4 * RET_WIDTH + S5_WIDTH + 3 * HY_WIDTH
U_COL = 4 * RET_WIDTH
HY_COL = U_COL + S5_WIDTH
N_EXPERTS = 64
TOP_K = 6
D_EXPERT = 512
D_SHARED = 512
ROUTED_SCALE = 2.5
EPS = 1e-6

LANES = 128
SUBLANES = 8
VMEM_LIMIT = 56 * 1024 * 1024

ROW_TILE = 512
MOE_BM = 256


def _cparams(sem):
    return pltpu.CompilerParams(dimension_semantics=sem, vmem_limit_bytes=VMEM_LIMIT)


def _dot(a, b):
    return jnp.dot(a, b, preferred_element_type=F32)


def _rms(x, g):
    var = jnp.mean(x * x, axis=-1, keepdims=True)
    return x * lax.rsqrt(var + EPS) * g


def _pack_halves(xb):
    n = xb.shape[1] // 2
    lo = lax.bitcast_convert_type(xb[:, :n].astype(F32), jnp.uint32) >> 16
    hi = lax.bitcast_convert_type(xb[:, n:].astype(F32), jnp.uint32)
    return lo | hi


def _unpack_halves(w):
    lo = lax.bitcast_convert_type(w << 16, F32)
    hi = lax.bitcast_convert_type(w & jnp.uint32(0xFFFF0000), F32)
    return lo, hi


TOKEN_ROWS = D_MODEL // 2 // LANES


def _store_token_tiles(ref, w):
    m = w.shape[0]
    for s in range(TOKEN_ROWS):
        ref[pl.ds(s, m, stride=TOKEN_ROWS), :] = w[:, s * LANES:(s + 1) * LANES]


def _load_token_tiles(ref, m):
    return jnp.concatenate([ref[pl.ds(s, m, stride=TOKEN_ROWS), :] for s in range(TOKEN_ROWS)], axis=1)


def _ada_kernel(c_ref, w_ref, b_ref, o_ref):
    c = c_ref[...]
    s = (c * jax.nn.sigmoid(c)).astype(BF16)
    o_ref[0] = _dot(s, w_ref[0].astype(BF16)) + b_ref[0]


def _ada(cond8, w_ada, b_ada):
    tn = 1024
    n = w_ada.shape[-1]
    return pl.pallas_call(
        _ada_kernel,
        grid=(DEPTH, n // tn),
        in_specs=[
            pl.BlockSpec((SUBLANES, D_MODEL), lambda l, j: (0, 0)),
            pl.BlockSpec((1, D_MODEL, tn), lambda l, j: (l, 0, j)),
            pl.BlockSpec((1, 1, tn), lambda l, j: (l, 0, j)),
        ],
        out_specs=pl.BlockSpec((1, SUBLANES, tn), lambda l, j: (l, 0, j)),
        out_shape=jax.ShapeDtypeStruct((DEPTH, SUBLANES, n), F32),
        compiler_params=_cparams(("parallel", "parallel")),
        name="ada",
    )(cond8, w_ada, b_ada.reshape(DEPTH, 1, n))


def _group_of(i, tile, n_ctx_rows, dec_seq):
    ctx_tiles = n_ctx_rows // tile
    per = dec_seq // tile
    return jnp.where(i < ctx_tiles, 0, 1 + (i - ctx_tiles) // per)


def _in_kernel(x_ref, mod_ref, g_ref, w_ref, o_ref, h_scr):
    @pl.when(pl.program_id(1) == 0)
    def _():
        m = mod_ref[0]
        h = _rms(x_ref[...], g_ref[...]) * (1.0 + m[1:2]) + m[0:1]
        h_scr[...] = h.astype(BF16)

    o_ref[...] = _dot(h_scr[...], w_ref[...])


def _in_proj(x, mod, g, w_bf, layer, n_ctx_rows, dec_seq):
    t = x.shape[0]
    tm, tn = 1024, 1024
    grp = functools.partial(_group_of, tile=tm, n_ctx_rows=n_ctx_rows, dec_seq=dec_seq)
    return pl.pallas_call(
        _in_kernel,
        grid=(t // tm, IN_WIDTH // tn),
        in_specs=[
            pl.BlockSpec((tm, D_MODEL), lambda i, j: (i, 0)),
            pl.BlockSpec((1, 6, D_MODEL), lambda i, j: (grp(i), 0, 0)),
            pl.BlockSpec((1, D_MODEL), lambda i, j: (0, 0)),
            pl.BlockSpec((None, D_MODEL, tn), lambda i, j: (layer, 0, j)),
        ],
        out_specs=pl.BlockSpec((tm, tn), lambda i, j: (i, j)),
        out_shape=jax.ShapeDtypeStruct((t, IN_WIDTH), F32),
        scratch_shapes=[pltpu.VMEM((tm, D_MODEL), BF16)],
        compiler_params=_cparams(("parallel", "arbitrary")),
        name="in_proj",
    )(x, mod, g.reshape(1, D_MODEL), w_bf)


def _ret_kernel(lg_ref, q_ref, k_ref, v_ref, gt_ref, cos_ref, sin_ref, s0_ref,
                o_ref, sfin_ref, acc_scr, q_scr, k_scr, *, seq_len, hb, rope):
    c = RET_CHUNK
    n_chunks = seq_len // c
    ii = lax.broadcasted_iota(jnp.int32, (c, c), 0)
    jj = lax.broadcasted_iota(jnp.int32, (c, c), 1)
    rel = (ii - jj).astype(F32)
    ci = lax.broadcasted_iota(jnp.int32, (c, 1), 0).astype(F32)
    one = jnp.ones((1, 1), F32)
    tdot = functools.partial(lax.dot_general, preferred_element_type=F32)

    def make_head(hh):
        head = pl.program_id(1) * hb + hh
        lgf = lg_ref[0, head]
        lgb = lg_ref[1, head]
        dmask = (jnp.where(rel >= 0, jnp.exp(lgf * jnp.maximum(rel, 0.0)), 0.0)
                 + jnp.where(rel <= 0, jnp.exp(lgb * jnp.maximum(-rel, 0.0)), 0.0))
        qd_f = jnp.exp(lgf * (ci + 1.0))
        kd_f = jnp.exp(lgf * (c - 1.0 - ci))
        cd_f = jnp.exp(lgf * c * one)
        qd_b = jnp.exp(lgb * (c - ci))
        kd_b = jnp.exp(lgb * ci)
        cd_b = jnp.exp(lgb * c * one)
        lanes = slice(hh * LANES, (hh + 1) * LANES)

        def rows_of(n):
            if isinstance(n, int):
                return slice(n * c, (n + 1) * c)
            return pl.ds(pl.multiple_of(n * c, c), c)

        def fwd_chunk(n, s_f):
            rows = rows_of(n)
            q = q_ref[rows, lanes]
            k = k_ref[rows, lanes] * (RET_DK ** -0.5)
            if rope:
                cs = cos_ref[rows, :]
                sn = sin_ref[rows, :]
                q = q * cs + pltpu.roll(q, RET_DK // 2, 1) * sn
                k = k * cs + pltpu.roll(k, RET_DK // 2, 1) * sn
            qb = q.astype(BF16)
            vb = v_ref[rows, lanes].astype(BF16)
            q_scr[rows, lanes] = qb
            k_scr[rows, lanes] = k
            scores = tdot(qb, k.astype(BF16), (((1,), (1,)), ((), ()))) * dmask
            inner = _dot(scores.astype(BF16), vb)
            cross = _dot(qb, s_f.astype(BF16)) * qd_f
            acc_scr[rows, lanes] = inner + cross
            upd = tdot((k * kd_f).astype(BF16), vb, (((0,), (0,)), ((), ())))
            return s_f * cd_f + upd

        def bwd_chunk(m, s_b):
            n = n_chunks - 1 - m
            rows = rows_of(n)
            qb = q_scr[rows, lanes]
            k = k_scr[rows, lanes]
            vb = v_ref[rows, lanes].astype(BF16)
            o = acc_scr[rows, lanes] + _dot(qb, s_b.astype(BF16)) * qd_b
            mu = jnp.mean(o, axis=-1, keepdims=True)
            oc = o - mu
            var = jnp.mean(oc * oc, axis=-1, keepdims=True)
            o = oc * lax.rsqrt(var + EPS)
            g = gt_ref[rows, lanes]
            o_ref[rows, lanes] = (g * jax.nn.sigmoid(g) * o).astype(o_ref.dtype)
            upd = tdot((k * kd_b).astype(BF16), vb, (((0,), (0,)), ((), ())))
            return s_b * cd_b + upd

        return fwd_chunk, bwd_chunk

    if n_chunks <= 4:
        for hh in range(hb):
            fwd_chunk, bwd_chunk = make_head(hh)
            s_f = s0_ref[0, 0, hh]
            s_b = s0_ref[0, 1, hh]
            for n in range(n_chunks):
                s_f = fwd_chunk(n, s_f)
            for m in range(n_chunks):
                s_b = bwd_chunk(m, s_b)
            sfin_ref[0, 0, hh] = s_f
            sfin_ref[0, 1, hh] = s_b
    else:
        fns = [make_head(hh) for hh in range(hb)]
        s_f = lax.fori_loop(0, n_chunks, lambda n, ss: tuple(f[0](n, s) for f, s in zip(fns, ss)),
                            tuple(s0_ref[0, 0, hh] for hh in range(hb)))
        s_b = lax.fori_loop(0, n_chunks, lambda m, ss: tuple(f[1](m, s) for f, s in zip(fns, ss)),
                            tuple(s0_ref[0, 1, hh] for hh in range(hb)))
        for hh in range(hb):
            sfin_ref[0, 0, hh] = s_f[hh]
            sfin_ref[0, 1, hh] = s_b[hh]


def _into(kernel_fn, n_in, dst):
    dsts = [d for d in (dst if isinstance(dst, (list, tuple)) else [dst])]
    outs = [k for k, d in enumerate(dsts) if d is not None]
    if not outs:
        return kernel_fn, [], [], {}

    def body(*refs):
        return kernel_fn(*refs[:n_in], *refs[n_in + len(outs):])

    return (body, [pl.BlockSpec(memory_space=pl.ANY)] * len(outs), [dsts[k] for k in outs],
            {n_in + pos: k for pos, k in enumerate(outs)})


def _retention(proj, log_gamma, s0, cos2, sin2, *, row0, n_seq, seq_len, hb, rope, dst=None,
               states=None, layer=0):
    blk0 = row0 // seq_len
    body, dst_spec, dst_arg, alias = _into(
        functools.partial(_ret_kernel, seq_len=seq_len, hb=hb, rope=rope), 8, [dst, states])
    if states is None:
        st_spec = pl.BlockSpec((1, 2, hb, RET_DK, RET_DV), lambda b, h, lg: (b, 0, h, 0, 0))
        st_shape = jax.ShapeDtypeStruct((n_seq, 2, RET_HEADS, RET_DK, RET_DV), F32)
    else:
        st_spec = pl.BlockSpec((1, None, 2, hb, RET_DK, RET_DV), lambda b, h, lg: (b, layer, 0, h, 0, 0))
        st_shape = jax.ShapeDtypeStruct(states.shape, F32)
    w = hb * LANES
    hblocks = RET_HEADS // hb
    col = lambda part: (lambda b, h, lg: (blk0 + b, part * hblocks + h))
    grid_spec = pltpu.PrefetchScalarGridSpec(
        num_scalar_prefetch=1,
        grid=(n_seq, hblocks),
        in_specs=[
            pl.BlockSpec((seq_len, w), col(0)),
            pl.BlockSpec((seq_len, w), col(1)),
            pl.BlockSpec((seq_len, w), col(2)),
            pl.BlockSpec((seq_len, w), col(3)),
            pl.BlockSpec((seq_len, LANES), lambda b, h, lg: (0, 0)),
            pl.BlockSpec((seq_len, LANES), lambda b, h, lg: (0, 0)),
            pl.BlockSpec((1, 2, hb, RET_DK, RET_DV), lambda b, h, lg: (b if s0.shape[0] > 1 else 0, 0, h, 0, 0)),
        ] + dst_spec,
        out_specs=[
            pl.BlockSpec((seq_len, w), lambda b, h, lg: (blk0 + b, h)),
            st_spec,
        ],
        scratch_shapes=[
            pltpu.VMEM((seq_len, w), F32),
            pltpu.VMEM((seq_len, w), BF16),
            pltpu.VMEM((seq_len, w), F32),
        ],
    )
    return pl.pallas_call(
        body,
        grid_spec=grid_spec,
        out_shape=[jax.ShapeDtypeStruct((proj.shape[0], RET_WIDTH), BF16), st_shape],
        input_output_aliases=alias,
        compiler_params=_cparams(("parallel", "arbitrary")),
        name="retention",
    )(log_gamma, proj, proj, proj, proj, cos2, sin2, s0, *dst_arg)


def _rope_tables(seq_len):
    rows_n = seq_len // GRID_W
    rows = jnp.repeat(jnp.arange(rows_n, dtype=F32), GRID_W)
    cols = jnp.tile(jnp.arange(GRID_W, dtype=F32), rows_n)
    nf = RET_DK // 4
    inv = ROPE_BASE ** (-jnp.arange(nf, dtype=F32) / nf)
    ang = jnp.concatenate([rows[:, None] * inv, cols[:, None] * inv], axis=-1)
    cs, sn = jnp.cos(ang), jnp.sin(ang)
    return jnp.concatenate([cs, cs], axis=-1), jnp.concatenate([-sn, sn], axis=-1)


S5_GB = LANES // S5_GROUP
S5_W = S5_Q * LANES
S5_SPLIT = 4
S5_SW = S5_GB * S5_STATE
S5_SB = S5_SW // LANES


def _s5_kernel(u_ref, bq_ref, k_ref, cq_ref, ar_ref, ai_ref, d_ref, h0_ref,
               y_ref, hf_ref, ub_scr, sm_scr, hp_scr, hpb_scr, t_scr, *, n_seq, n_chunks):
    s = pl.program_id(1)
    m = n_seq * n_chunks
    q = S5_Q
    nblk = S5_W // LANES
    sb = S5_SB

    per = q // S5_SPLIT
    kw = S5_W // S5_SPLIT

    @pl.when(s == 0)
    def _():
        for j in range(q):
            ub_scr[j // per, :, (j % per) * LANES:(j % per + 1) * LANES] = (
                u_ref[pl.ds(j, m, stride=q), :].astype(BF16))
        for ib in range(S5_SPLIT):
            for j in range(q):
                c0 = (q - 1 - j) * LANES + ib * kw
                t_scr[ib, j * LANES:(j + 1) * LANES, :] = k_ref[0, :, c0:c0 + kw]

    @pl.when(s < S5_SPLIT)
    def _():
        part = _dot(ub_scr[jnp.minimum(s, S5_SPLIT - 1)], bq_ref[0])

        @pl.when(s == 0)
        def _():
            for cb in range(nblk):
                sm_scr[cb] = part[:, cb * LANES:(cb + 1) * LANES]

        @pl.when(s > 0)
        def _():
            for cb in range(nblk):
                sm_scr[cb] += part[:, cb * LANES:(cb + 1) * LANES]

    @pl.when(s == S5_SPLIT - 1)
    def _():
        ar = ar_ref[0]
        ai = ai_ref[0]
        h0 = h0_ref[0]
        blk = lambda a, cb: a[:, cb * LANES:(cb + 1) * LANES]

        def body(n, carry):
            rows_f = pl.ds(n, n_seq, stride=n_chunks)
            rows_b = pl.ds(n_chunks - 1 - n, n_seq, stride=n_chunks)
            new = list(carry)
            for d, rows in ((0, rows_f), (1, rows_b)):
                for c in range(sb):
                    re_i = d * sb + c
                    im_i = (2 + d) * sb + c
                    hr, hi = carry[re_i], carry[im_i]
                    hp_scr[re_i, rows, :] = hr
                    hp_scr[im_i, rows, :] = hi
                    a_r, a_i = blk(ar, re_i), blk(ai, re_i)
                    new[re_i] = a_r * hr - a_i * hi + sm_scr[re_i, rows, :]
                    new[im_i] = a_r * hi + a_i * hr + sm_scr[im_i, rows, :]
            return tuple(new)

        fin = lax.fori_loop(0, n_chunks, body, tuple(blk(h0, cb) for cb in range(nblk)))
        hf_ref[0] = jnp.concatenate(fin, axis=1)
        for cb in range(nblk):
            hpb_scr[:, cb * LANES:(cb + 1) * LANES] = hp_scr[cb].astype(BF16)

    @pl.when(s >= S5_SPLIT)
    def _():
        ub = jnp.concatenate([ub_scr[k] for k in range(S5_SPLIT)], axis=1)
        y = _dot(ub, t_scr[jnp.maximum(s - S5_SPLIT, 0)]) + _dot(hpb_scr[...], cq_ref[0])
        dd = d_ref[0]
        for ii in range(per):
            rows = pl.ds((s - S5_SPLIT) * per + ii, m, stride=q)
            yi = y[:, ii * LANES:(ii + 1) * LANES] + dd * u_ref[rows, :]
            y_ref[rows, :] = jax.nn.gelu(yi)


def _s5_expand_kernel(mc_ref, o_ref, *, xsize, ysize):
    xs, ys, gs = xsize.bit_length() - 1, ysize.bit_length() - 1, S5_GB.bit_length() - 1
    assert xsize == 1 << xs and ysize == 1 << ys and S5_GB == 1 << gs
    cw = o_ref.shape[2]
    nc = mc_ref.shape[2]
    col0 = pl.program_id(1) * cw
    r = lax.broadcasted_iota(jnp.int32, (nc, cw), 0)
    col = lax.broadcasted_iota(jnp.int32, (nc, cw), 1) + col0
    spread = jnp.logical_and(r >> ys == col >> (ys + gs), (r & (ysize - 1)) == (col & (ysize - 1)))
    big = _dot(mc_ref[0], jnp.where(spread, 1.0, 0.0).astype(BF16))
    row = lax.broadcasted_iota(jnp.int32, big.shape, 0)
    colb = lax.broadcasted_iota(jnp.int32, big.shape, 1) + col0
    same = ((row >> xs) & (S5_GB - 1)) == ((colb >> ys) & (S5_GB - 1))
    o_ref[0] = jnp.where(same, big, 0.0).astype(BF16)


def _s5_expand(mc, *, xsize, ysize):
    nb, rows, nc = mc.shape
    cols = nc * S5_GB
    cw = 512
    return pl.pallas_call(
        functools.partial(_s5_expand_kernel, xsize=xsize, ysize=ysize),
        grid=(nb, cols // cw),
        in_specs=[pl.BlockSpec((1, rows, nc), lambda b, j: (b, 0, 0))],
        out_specs=pl.BlockSpec((1, rows, cw), lambda b, j: (b, 0, j)),
        out_shape=jax.ShapeDtypeStruct((nb, rows, cols), BF16),
        compiler_params=_cparams(("parallel", "parallel")),
        name="s5_expand",
    )(mc)


def _s5_mats(lam_re, lam_im, log_dt, b_re, b_im, c_re, c_im, d):
    flat = lambda a: a.reshape((-1,) + a.shape[2:])
    kc, bq, cq, ar, ai, dd = map(flat, jax.vmap(_s5_compact)(lam_re, lam_im, log_dt, b_re, b_im, c_re, c_im, d))
    ch, p = S5_GROUP, S5_STATE
    return (_s5_expand(kc, xsize=ch, ysize=ch),
            _s5_expand(bq, xsize=ch, ysize=p),
            _s5_expand(cq, xsize=p, ysize=ch),
            ar, ai, dd)


def _s5_compact(lam_re, lam_im, log_dt, b_re, b_im, c_re, c_im, d):
    q, g, p, ch = S5_Q, S5_GROUPS, S5_STATE, S5_GROUP
    lam = lax.complex(jnp.minimum(lam_re.astype(F32), -1e-4), lam_im.astype(F32))
    ldt = lam * jnp.exp(log_dt.astype(F32))[..., None]
    lam_bar = jnp.exp(ldt)
    b_bar = ((lam_bar - 1.0) / lam)[..., None] * lax.complex(b_re.astype(F32), b_im.astype(F32))
    cc = lax.complex(c_re.astype(F32), c_im.astype(F32))
    pw = jnp.exp(ldt[..., None] * jnp.arange(q + 1, dtype=F32))
    hi = lax.Precision.HIGHEST
    lag = jnp.arange(2 * q, dtype=F32) - (q - 1)
    wf = jnp.where(lag >= 0, jnp.exp(ldt[0][..., None] * jnp.maximum(lag, 0.0)), 0.0)
    wb = jnp.where(lag <= 0, jnp.exp(ldt[1][..., None] * jnp.maximum(-lag, 0.0)), 0.0)
    kc = jnp.real(jnp.einsum('gcp,gpd,gpe->gedc', cc[0], wf, b_bar[0], precision=hi)
                  + jnp.einsum('gcp,gpd,gpe->gedc', cc[1], wb, b_bar[1], precision=hi))

    pw_dn = jnp.exp(ldt[..., None] * (q - jnp.arange(q + 1, dtype=F32)))
    bf = pw_dn[0][..., 1:][:, :, :, None] * b_bar[0][:, :, None, :]
    bb = pw[1][..., :q][:, :, :, None] * b_bar[1][:, :, None, :]
    to_rows = lambda m: m.transpose(0, 2, 3, 1).reshape(g, q * ch, p)
    bq = jnp.concatenate([to_rows(jnp.real(bf)), to_rows(jnp.real(bb)),
                          to_rows(jnp.imag(bf)), to_rows(jnp.imag(bb))], axis=-1)

    cf = cc[0].transpose(0, 2, 1)[:, :, None, :] * pw[0][..., 1:][:, :, :, None]
    cb = cc[1].transpose(0, 2, 1)[:, :, None, :] * pw_dn[1][..., :q][:, :, :, None]
    to_cols = lambda m: m.reshape(g, p, q * ch)
    cq = jnp.concatenate([to_cols(jnp.real(cf)), to_cols(jnp.real(cb)),
                          to_cols(-jnp.imag(cf)), to_cols(-jnp.imag(cb))], axis=1)

    gb, nb = S5_GB, g // S5_GB
    rows_of = lambda a, outer, inner: (a.reshape(nb, gb, outer, inner, a.shape[-1]).transpose(0, 2, 1, 3, 4)
                                       .reshape(nb, outer * gb * inner, a.shape[-1]).astype(BF16))
    kc = kc.reshape(nb, gb * ch, 2 * q * ch).astype(BF16)

    lq = pw[..., q].reshape(2, nb, 1, S5_SW)
    ar = jnp.concatenate([jnp.real(lq[0]), jnp.real(lq[1])], axis=-1)
    ai = jnp.concatenate([jnp.imag(lq[0]), jnp.imag(lq[1])], axis=-1)
    dd = d.astype(F32).reshape(nb, 1, LANES)
    return kc, rows_of(bq, q, ch), rows_of(cq, 4, p), ar, ai, dd


def _s5(proj, mats, h0_re, h0_im, *, row0, n_seq, seq_len, layer=0, dst=None):
    q, p = S5_Q, S5_STATE
    n_chunks = seq_len // q
    m = n_seq * n_chunks
    rows = n_seq * seq_len
    nb = S5_GROUPS // S5_GB
    nblk = S5_W // LANES
    kw = S5_W // S5_SPLIT
    tm, bq, cq, ar, ai, dd = mats
    part = lambda a: a.astype(F32).reshape(n_seq, nb, S5_SW)
    h0 = jnp.concatenate([part(h0_re[:, 0]), part(h0_re[:, 1]), part(h0_im[:, 0]), part(h0_im[:, 1])],
                         axis=-1).transpose(1, 0, 2)
    per_b = lambda shape: pl.BlockSpec((1,) + shape, lambda b, s: (b, 0, 0))
    b0 = layer * nb
    per_l = lambda shape: pl.BlockSpec((1,) + shape, lambda b, s: (b0 + b, 0, 0))
    body, dst_spec, dst_arg, alias = _into(functools.partial(_s5_kernel, n_seq=n_seq, n_chunks=n_chunks), 8, dst)
    y, hf = pl.pallas_call(
        body,
        grid=(nb, 2 * S5_SPLIT),
        in_specs=[
            pl.BlockSpec((rows, LANES), lambda b, s: (row0 // rows, U_COL // LANES + b)),
            pl.BlockSpec((1, kw, 4 * S5_SW), lambda b, s: (b0 + b, jnp.minimum(s, S5_SPLIT - 1), 0)),
            per_l((LANES, 2 * q * LANES)),
            pl.BlockSpec((1, 4 * S5_SW, kw), lambda b, s: (b0 + b, 0, jnp.maximum(s - S5_SPLIT, 0))),
            per_l((1, 2 * S5_SW)), per_l((1, 2 * S5_SW)), per_l((1, LANES)), per_b((n_seq, 4 * S5_SW)),
        ] + dst_spec,
        out_specs=[pl.BlockSpec((rows, LANES), lambda b, s: (row0 // rows, b)), per_b((n_seq, 4 * S5_SW))],
        out_shape=[jax.ShapeDtypeStruct((proj.shape[0], S5_WIDTH), F32),
                   jax.ShapeDtypeStruct((nb, n_seq, 4 * S5_SW), F32)],
        scratch_shapes=[pltpu.VMEM((S5_SPLIT, m, kw), BF16), pltpu.VMEM((nblk, m, LANES), F32),
                        pltpu.VMEM((nblk, m, LANES), F32), pltpu.VMEM((m, 4 * S5_SW), BF16),
                        pltpu.VMEM((S5_SPLIT, S5_W, kw), BF16)],
        input_output_aliases=alias,
        compiler_params=_cparams(("parallel", "arbitrary")),
        name="s5",
    )(proj, bq, tm, cq, ar, ai, dd, h0, *dst_arg)
    hf = hf.reshape(nb, n_seq, 4, S5_GB, p).transpose(1, 2, 0, 3, 4).reshape(n_seq, 4, S5_GROUPS, p)
    return y, hf[:, 0:2], hf[:, 2:4]


def _conv3(x, w, b):
    n = x.shape[0]
    row = lax.broadcasted_iota(jnp.int32, x.shape, 0)
    prev = jnp.where(row == 0, 0.0, pltpu.roll(x, 1, 0))
    nxt = jnp.where(row == n - 1, 0.0, pltpu.roll(x, n - 1, 0))
    return prev * w[0:1] + x * w[1:2] + nxt * w[2:3] + b


def _hy_fwd_kernel(x0_ref, x1_ref, v_ref, w0_ref, w1_ref, wv_ref, b0_ref, b1_ref, bv_ref,
                   fc_ref, fs_ref, m1_ref, m2_ref, m3_ref, p_ref, z_ref, x0c_ref, zb_scr):
    @pl.when(pl.program_id(2) == 0)
    def _():
        z = _conv3(x1_ref[...], w1_ref[...], b1_ref[...]) * _conv3(v_ref[...], wv_ref[...], bv_ref[...])
        z_ref[...] = z
        zb_scr[...] = z.astype(BF16)
        x0c_ref[...] = _conv3(x0_ref[...], w0_ref[...], b0_ref[...])

    zb = zb_scr[...]
    a = _dot(fc_ref[...], zb)
    b = _dot(fs_ref[...], zb)
    m2 = m2_ref[...]
    p_ref[0, 0] = (m1_ref[...] * a + m2 * b).astype(BF16)
    p_ref[0, 1] = (m3_ref[...] * b - m2 * a).astype(BF16)


def _hy_inv_kernel(p_ref, gc_ref, gs_ref, z_ref, x0c_ref, bias_ref, o_ref):
    conv = _dot(gc_ref[...], p_ref[0, 0]) + _dot(gs_ref[...], p_ref[0, 1])
    o_ref[...] = (x0c_ref[...] * (conv + bias_ref[...] * z_ref[...])).astype(o_ref.dtype)


def _hy_short_kernel(x0_ref, x1_ref, v_ref, w0_ref, w1_ref, wv_ref, b0_ref, b1_ref, bv_ref,
                     fc_ref, fs_ref, gs_ref, m1_ref, m2_ref, m3_ref, bias_ref, o_ref):
    z = _conv3(x1_ref[...], w1_ref[...], b1_ref[...]) * _conv3(v_ref[...], wv_ref[...], bv_ref[...])
    zb = z.astype(BF16)
    a = _dot(fc_ref[...], zb)
    b = _dot(fs_ref[...], zb)
    m2 = m2_ref[...]
    p_re = (m1_ref[...] * a + m2 * b).astype(BF16)
    p_im = (m3_ref[...] * b - m2 * a).astype(BF16)
    conv = _dot(fc_ref[...], p_re) + _dot(gs_ref[...], p_im)
    x0c = _conv3(x0_ref[...], w0_ref[...], b0_ref[...])
    o_ref[...] = (x0c * (conv + bias_ref[...] * z)).astype(o_ref.dtype)


def _dft_mats(seq_len):
    n, w = seq_len, 64
    k = jnp.arange(n, dtype=jnp.int32)
    ang = lambda j: ((k[:, None] * j[None, :]) % (2 * n)).astype(F32) * (math.pi / n)
    ang_a = ang(jnp.arange(n // w, dtype=jnp.int32) * w)
    ang_b = ang(jnp.arange(w, dtype=jnp.int32))
    ca, sa = jnp.cos(ang_a)[:, :, None], jnp.sin(ang_a)[:, :, None]
    cb, sb = jnp.cos(ang_b)[:, None, :], jnp.sin(ang_b)[:, None, :]
    cm = (ca * cb - sa * sb).reshape(n, n)
    sm = -(sa * cb + ca * sb).reshape(n, n)
    nyq = jnp.where(k % 2 == 0, 1.0, -1.0).astype(F32)
    return cm.astype(BF16), sm.at[0, :].set(nyq).astype(BF16), sm.at[:, 0].set(nyq).astype(BF16)


def _hy_filter_taps(seq_len, f1_w, f1_b, f2_w, f2_b, f3_w, f3_b, freq, decay):
    n = seq_len
    t = (jnp.arange(n, dtype=F32) / n)[:, None]
    bands = jnp.arange(1, HY_BANDS + 1, dtype=F32)[None, :]
    z = jnp.concatenate([t, jnp.cos(2.0 * math.pi * t * bands), jnp.sin(2.0 * math.pi * t * bands)], axis=-1)
    hi = lax.Precision.HIGHEST
    fr = freq.astype(F32)
    h = jnp.sin(fr * (jnp.dot(z, f1_w.astype(F32), precision=hi) + f1_b.astype(F32)))
    h = jnp.sin(fr * (jnp.dot(h, f2_w.astype(F32), precision=hi) + f2_b.astype(F32)))
    h = jnp.dot(h, f3_w.astype(F32), precision=hi) + f3_b.astype(F32)
    h = h * jnp.exp(-t * jnp.abs(decay.astype(F32)))
    h = h.reshape(n, 2, HY_WIDTH)
    h = h / jnp.sum(jnp.abs(h), axis=(0, 1), keepdims=True)
    return h.reshape(n, 2 * HY_WIDTH)


def _hy_spectrum_kernel(h_ref, cm_ref, sm_ref, ck_ref, sk_ref, m1_ref, m2_ref, m3_ref, *, seq_len):
    w = HY_WIDTH
    tk = cm_ref.shape[0]
    hb = h_ref[0].astype(BF16)
    xc = _dot(cm_ref[...], hb)
    xs = _dot(sm_ref[...], hb)
    k = pl.program_id(1) * tk + lax.broadcasted_iota(jnp.int32, (tk, 1), 0)
    first = k == 0
    sf = jnp.where(first, 0.0, -xs[:, :w])
    sb = jnp.where(first, 0.0, -xs[:, w:])
    ck, sk = ck_ref[...], sk_ref[...]
    hr = xc[:, :w] + ck * xc[:, w:] - sk * sb
    him = -sf + ck * sb + sk * xc[:, w:]
    nyq = xs[:, :w] - xs[:, w:]
    wk = jnp.where(first, 1.0, 2.0) / (2.0 * seq_len)
    m1_ref[0] = hr * wk
    m2_ref[0] = jnp.where(first, 0.0, -him) * wk
    m3_ref[0] = jnp.where(first, nyq, hr) * wk


def _hy_spectrum(h, dft):
    depth, n, _ = h.shape
    cm, sm, _ = dft
    tk = min(n, 512)
    ang = jnp.arange(n, dtype=F32)[:, None] * (math.pi / n)
    out = jax.ShapeDtypeStruct((depth, n, HY_WIDTH), F32)
    oblk = pl.BlockSpec((1, tk, HY_WIDTH), lambda l, k: (l, k, 0))
    return pl.pallas_call(
        functools.partial(_hy_spectrum_kernel, seq_len=n),
        grid=(depth, n // tk),
        in_specs=[pl.BlockSpec((1, n, 2 * HY_WIDTH), lambda l, k: (l, 0, 0)),
                  pl.BlockSpec((tk, n), lambda l, k: (k, 0)), pl.BlockSpec((tk, n), lambda l, k: (k, 0)),
                  pl.BlockSpec((tk, 1), lambda l, k: (k, 0)), pl.BlockSpec((tk, 1), lambda l, k: (k, 0))],
        out_specs=[oblk, oblk, oblk],
        out_shape=[out, out, out],
        compiler_params=_cparams(("parallel", "parallel")),
        name="hyena_spectrum",
    )(h, cm, sm, jnp.cos(ang), jnp.sin(ang))


def _hyena(proj, conv_w, conv_b, bias, dft, mults, *, row0, n_seq, seq_len, cb, tk, layer=0, dst=None):
    blk0 = row0 // seq_len
    nc = HY_WIDTH // cb
    nk = seq_len // tk
    c0 = HY_COL // cb
    cm, sm, smt = dft
    m1, m2, m3 = mults
    xcol = lambda part: pl.BlockSpec((seq_len, cb), lambda b, c, k: (blk0 + b, c0 + part * nc + c))
    wcol = lambda part: pl.BlockSpec((3, cb), lambda b, c, k: (0, part * nc + c))
    bcol = lambda part: pl.BlockSpec((1, cb), lambda b, c, k: (0, part * nc + c))
    frow = pl.BlockSpec((tk, seq_len), lambda b, c, k: (k, 0))
    mblk = pl.BlockSpec((None, tk, cb), lambda b, c, k: (layer, k, c))
    cb2 = conv_b.reshape(1, 3 * HY_WIDTH)
    if nk == 1:
        body, dst_spec, dst_arg, alias = _into(_hy_short_kernel, 16, dst)
        return pl.pallas_call(
            body,
            grid=(n_seq, nc, 1),
            in_specs=[xcol(0), xcol(1), xcol(2), wcol(0), wcol(1), wcol(2), bcol(0), bcol(1), bcol(2),
                      frow, frow, frow, mblk, mblk, mblk,
                      pl.BlockSpec((1, cb), lambda b, c, k: (0, c))] + dst_spec,
            out_specs=pl.BlockSpec((seq_len, cb), lambda b, c, k: (blk0 + b, c)),
            out_shape=jax.ShapeDtypeStruct((proj.shape[0], HY_WIDTH), BF16),
            input_output_aliases=alias,
            compiler_params=_cparams(("parallel", "parallel", "arbitrary")),
            name="hyena_short",
        )(proj, proj, proj, conv_w, conv_w, conv_w, cb2, cb2, cb2, cm, sm, smt, m1, m2, m3,
          bias.reshape(1, HY_WIDTH), *dst_arg)
    pspec, z, x0c = pl.pallas_call(
        _hy_fwd_kernel,
        grid=(n_seq, nc, nk),
        in_specs=[xcol(0), xcol(1), xcol(2), wcol(0), wcol(1), wcol(2), bcol(0), bcol(1), bcol(2),
                  frow, frow, mblk, mblk, mblk],
        out_specs=[
            pl.BlockSpec((1, 2, tk, cb), lambda b, c, k: (b, 0, k, c)),
            pl.BlockSpec((seq_len, cb), lambda b, c, k: (b, c)),
            pl.BlockSpec((seq_len, cb), lambda b, c, k: (b, c)),
        ],
        out_shape=[
            jax.ShapeDtypeStruct((n_seq, 2, seq_len, HY_WIDTH), BF16),
            jax.ShapeDtypeStruct((n_seq * seq_len, HY_WIDTH), F32),
            jax.ShapeDtypeStruct((n_seq * seq_len, HY_WIDTH), F32),
        ],
        scratch_shapes=[pltpu.VMEM((seq_len, cb), BF16)],
        compiler_params=_cparams(("parallel", "parallel", "arbitrary")),
        name="hyena_fwd",
    )(proj, proj, proj, conv_w, conv_w, conv_w, cb2, cb2, cb2, cm, sm, m1, m2, m3)
    grow = pl.BlockSpec((tk, seq_len), lambda b, c, k: (k, 0))
    tile = pl.BlockSpec((tk, cb), lambda b, c, k: (b * nk + k, c))
    body, dst_spec, dst_arg, alias = _into(_hy_inv_kernel, 6, dst)
    return pl.pallas_call(
        body,
        grid=(n_seq, nc, nk),
        in_specs=[pl.BlockSpec((1, 2, seq_len, cb), lambda b, c, k: (b, 0, 0, c)),
                  grow, grow, tile, tile, pl.BlockSpec((1, cb), lambda b, c, k: (0, c))] + dst_spec,
        out_specs=pl.BlockSpec((tk, cb), lambda b, c, k: (row0 // tk + b * nk + k, c)),
        out_shape=jax.ShapeDtypeStruct((proj.shape[0], HY_WIDTH), BF16),
        input_output_aliases=alias,
        compiler_params=_cparams(("parallel", "parallel", "arbitrary")),
        name="hyena_inv",
    )(pspec, cm, smt, z, x0c, bias.reshape(1, HY_WIDTH), *dst_arg)


def _out_kernel(x_ref, ret_ref, s5_ref, hy_ref, mod_ref, g_ref, gw_ref, gb_ref,
                wr_ref, ws_ref, wh_ref, rt_ref, xo_ref, h_ref, lg_ref):
    m = mod_ref[0]
    y = s5_ref[...]
    s5o = y * jax.nn.sigmoid(_dot(y.astype(BF16), gw_ref[...]) + gb_ref[...])
    mix = (_dot(ret_ref[...], wr_ref[...]) + _dot(s5o.astype(BF16), ws_ref[...])
           + _dot(hy_ref[...], wh_ref[...]))
    x = x_ref[...] + m[2:3] * mix
    xo_ref[...] = x
    h = _rms(x, g_ref[...]) * (1.0 + m[4:5]) + m[3:4]
    hb = h.astype(BF16)
    _store_token_tiles(h_ref, _pack_halves(hb))
    lg_ref[...] = lax.dot_general(rt_ref[...], hb, (((1,), (1,)), ((), ())), preferred_element_type=F32)


def _out_proj(x, ret_o, s5_y, hy_o, mod, g, glu_w, glu_b, w_out, router, n_ctx_rows, dec_seq):
    t = x.shape[0]
    tm = ROW_TILE
    grp = functools.partial(_group_of, tile=tm, n_ctx_rows=n_ctx_rows, dec_seq=dec_seq)
    row = lambda w: pl.BlockSpec((tm, w), lambda i: (i, 0))
    full = lambda a, b: pl.BlockSpec((a, b), lambda i: (0, 0))
    wo = w_out.astype(BF16)
    return pl.pallas_call(
        _out_kernel,
        grid=(t // tm,),
        in_specs=[row(D_MODEL), row(RET_WIDTH), row(S5_WIDTH), row(HY_WIDTH),
                  pl.BlockSpec((1, 6, D_MODEL), lambda i: (grp(i), 0, 0)),
                  full(1, D_MODEL), full(S5_WIDTH, S5_WIDTH), full(1, S5_WIDTH),
                  full(RET_WIDTH, D_MODEL), full(S5_WIDTH, D_MODEL), full(HY_WIDTH, D_MODEL),
                  full(N_EXPERTS, D_MODEL)],
        out_specs=[row(D_MODEL), pl.BlockSpec((tm * TOKEN_ROWS, LANES), lambda i: (i, 0)),
                   pl.BlockSpec((N_EXPERTS, tm), lambda i: (0, i))],
        out_shape=[jax.ShapeDtypeStruct((t, D_MODEL), F32), jax.ShapeDtypeStruct((t * TOKEN_ROWS, LANES), jnp.uint32),
                   jax.ShapeDtypeStruct((N_EXPERTS, t), F32)],
        compiler_params=_cparams(("parallel",)),
        name="out_proj",
    )(x, ret_o, s5_y, hy_o, mod, g.reshape(1, D_MODEL), glu_w.astype(BF16), glu_b.reshape(1, S5_WIDTH),
      wo[:RET_WIDTH], wo[RET_WIDTH:RET_WIDTH + S5_WIDTH], wo[RET_WIDTH + S5_WIDTH:], router.T.astype(BF16))


def _moe_kernel(be_ref, first_ref, slot_ref, nxt_ref, nu_ref, xs_ref, wg_hbm, wu_hbm, wd_hbm, o_ref,
                wg_f, wu_f, wd_f, wg_b, wu_b, wd_b, sem, *, layer):
    i = pl.program_id(0)

    def copies(e, s):
        return (pltpu.make_async_copy(wg_hbm.at[layer, e], wg_f.at[s], sem.at[s, 0]),
                pltpu.make_async_copy(wu_hbm.at[layer, e], wu_f.at[s], sem.at[s, 1]),
                pltpu.make_async_copy(wd_hbm.at[layer, e], wd_f.at[s], sem.at[s, 2]))

    @pl.when(i == 0)
    def _():
        for cp in copies(be_ref[0], 0):
            cp.start()

    @pl.when(first_ref[i] == 1)
    def _():
        s = slot_ref[i]
        for cp in copies(be_ref[i], s):
            cp.wait()

        @pl.when(nxt_ref[i] >= 0)
        def _():
            for cp in copies(nxt_ref[i], 1 - s):
                cp.start()

        wg_b[...] = wg_f[s].astype(BF16)
        wu_b[...] = wu_f[s].astype(BF16)
        wd_b[...] = wd_f[s].astype(BF16)

    @pl.when(i < nu_ref[0])
    def _():
        half = D_MODEL // 2
        x_lo, x_hi = _unpack_halves(_load_token_tiles(xs_ref, MOE_BM))
        x_lo = x_lo.astype(BF16)
        x_hi = x_hi.astype(BF16)
        gate = _dot(x_lo, wg_b[0:half, :]) + _dot(x_hi, wg_b[half:, :])
        up = _dot(x_lo, wu_b[0:half, :]) + _dot(x_hi, wu_b[half:, :])
        hb = gate * jax.nn.sigmoid(gate) * up
        _store_token_tiles(o_ref, _pack_halves(_dot(hb.astype(BF16), wd_b[...]).astype(BF16)))

    @pl.when(i >= nu_ref[0])
    def _():
        o_ref[...] = jnp.zeros_like(o_ref)


def _moe_grouped(xs, blk_e, first, slot, nxt, n_used, w_gate, w_up, w_down, layer):
    pr = xs.shape[0] // TOKEN_ROWS
    bm = MOE_BM
    nb = pr // bm
    grid_spec = pltpu.PrefetchScalarGridSpec(
        num_scalar_prefetch=5,
        grid=(nb,),
        in_specs=[
            pl.BlockSpec((bm * TOKEN_ROWS, LANES), lambda i, *_: (i, 0)),
            pl.BlockSpec(memory_space=pl.ANY),
            pl.BlockSpec(memory_space=pl.ANY),
            pl.BlockSpec(memory_space=pl.ANY),
        ],
        out_specs=pl.BlockSpec((bm * TOKEN_ROWS, LANES), lambda i, *_: (i, 0)),
        scratch_shapes=[pltpu.VMEM((2, D_MODEL, D_EXPERT), F32), pltpu.VMEM((2, D_MODEL, D_EXPERT), F32),
                        pltpu.VMEM((2, D_EXPERT, D_MODEL), F32),
                        pltpu.VMEM((D_MODEL, D_EXPERT), BF16), pltpu.VMEM((D_MODEL, D_EXPERT), BF16),
                        pltpu.VMEM((D_EXPERT, D_MODEL), BF16),
                        pltpu.SemaphoreType.DMA((2, 3))],
    )
    return pl.pallas_call(
        functools.partial(_moe_kernel, layer=layer),
        grid_spec=grid_spec,
        out_shape=jax.ShapeDtypeStruct((pr * TOKEN_ROWS, LANES), jnp.uint32),
        compiler_params=_cparams(("arbitrary",)),
        name="moe_grouped",
    )(blk_e, first, slot, nxt, n_used, xs, w_gate, w_up, w_down)


DISPATCH_TILE = 1024


def _dispatch_kernel(info_ref, nu_ref, pos_ref, h_ref, xs_out, zbuf, sem, zsem, *, nb):
    tm = pos_ref.shape[1]
    tr = TOKEN_ROWS
    bm = MOE_BM

    @pl.when(pl.program_id(0) == 0)
    def _():
        zbuf[...] = jnp.zeros_like(zbuf)

        def zero_block(first_row):
            rows = pl.ds(pl.multiple_of(first_row * tr, tr), bm * tr)
            return pltpu.make_async_copy(zbuf, xs_out.at[rows], zsem)

        for act in ("start", "wait"):
            def last_of_expert(e, carry, act=act):
                @pl.when(info_ref[e, 3] > info_ref[e, 1])
                def _():
                    getattr(zero_block(info_ref[e, 3] - bm), act)()
                return carry

            def tail_block(b, carry, act=act):
                getattr(zero_block(b * bm), act)()
                return carry

            lax.fori_loop(0, N_EXPERTS, last_of_expert, 0)
            lax.fori_loop(nu_ref[0], nb, tail_block, 0)

    def send(tok, carry):
        src = h_ref.at[pl.ds(pl.multiple_of(tok * tr, tr), tr)]
        for k in range(TOP_K):
            row = pl.multiple_of(pos_ref[k, tok], tr)
            pltpu.make_async_copy(src, xs_out.at[pl.ds(row, tr)], sem).start(priority=k % 2)
        return carry

    lax.fori_loop(0, tm, send, 0, unroll=4)
    n = tm * TOP_K * tr
    pltpu.make_async_copy(xs_out.at[pl.ds(0, n)], xs_out.at[pl.ds(0, n)], sem).wait()


def _dispatch(h2, pos, info, n_used, nb):
    t = pos.shape[1]
    tm = DISPATCH_TILE
    return pl.pallas_call(
        functools.partial(_dispatch_kernel, nb=nb),
        grid=(t // tm,),
        in_specs=[pl.BlockSpec(memory_space=pltpu.SMEM),
                  pl.BlockSpec(memory_space=pltpu.SMEM),
                  pl.BlockSpec((SUBLANES, tm), lambda i: (0, i), memory_space=pltpu.SMEM),
                  pl.BlockSpec((tm * TOKEN_ROWS, LANES), lambda i: (i, 0))],
        out_specs=pl.BlockSpec(memory_space=pl.ANY),
        out_shape=jax.ShapeDtypeStruct((nb * MOE_BM * TOKEN_ROWS, LANES), h2.dtype),
        scratch_shapes=[pltpu.VMEM((MOE_BM * TOKEN_ROWS, LANES), h2.dtype),
                        pltpu.SemaphoreType.DMA(()), pltpu.SemaphoreType.DMA(())],
        compiler_params=pltpu.CompilerParams(dimension_semantics=("arbitrary",)),
        name="dispatch",
    )(info, n_used, pos, h2)


ROUTE_TILE = 512


def _router_kernel(lg_ref, bias_ref, pos_ref, gate_ref, be_ref, info_ref, rank_scr, ek_scr, *, t, nbp):
    tl = ROUTE_TILE
    ne = N_EXPERTS
    bm = MOE_BM
    row = lax.broadcasted_iota(jnp.int32, (ne, tl), 0)
    tri = (lax.broadcasted_iota(jnp.int32, (tl, tl), 0) < lax.broadcasted_iota(jnp.int32, (tl, tl), 1)).astype(BF16)
    bias = bias_ref[...]

    def select(i, counts):
        cols = pl.ds(pl.multiple_of(i * tl, tl), tl)
        s = jax.nn.sigmoid(lg_ref[:, cols])
        sel = s + bias
        mask = jnp.zeros((ne, tl), F32)
        vals = []
        for k in range(TOP_K):
            best = jnp.max(sel, axis=0, keepdims=True)
            idx = jnp.min(jnp.where(sel == best, row, ne), axis=0, keepdims=True)
            hit = row == idx
            vals.append(jnp.sum(jnp.where(hit, s, 0.0), axis=0, keepdims=True))
            sel = jnp.where(hit, -jnp.inf, sel)
            mask = jnp.where(hit, 1.0, mask)
            ek_scr[k:k + 1, cols] = idx
        total = vals[0]
        for v in vals[1:]:
            total = total + v
        scale = ROUTED_SCALE / total
        for k in range(TOP_K):
            gate_ref[k:k + 1, cols] = vals[k] * scale
        gate_ref[TOP_K:SUBLANES, cols] = jnp.zeros((SUBLANES - TOP_K, tl), F32)
        rank_scr[:, cols] = _dot(mask.astype(BF16), tri) + counts
        return counts + jnp.sum(mask, axis=1, keepdims=True)

    counts = lax.fori_loop(0, t // tl, select, jnp.zeros((ne, 1), F32))
    counts = counts.astype(jnp.int32)
    shift = bm.bit_length() - 1
    assert bm == 1 << shift
    padded = ((counts + (bm - 1)) >> shift) << shift
    e0 = lax.broadcasted_iota(jnp.int32, (ne, ne), 0)
    e1 = lax.broadcasted_iota(jnp.int32, (ne, ne), 1)
    padded_row = jnp.sum(jnp.where(e0 == e1, padded, 0), axis=0, keepdims=True)
    counts_row = jnp.sum(jnp.where(e0 == e1, counts, 0), axis=0, keepdims=True)
    pstart = jnp.sum(jnp.where(e1 < e0, padded_row, 0), axis=1, keepdims=True)
    ustart = jnp.sum(jnp.where(e1 < e0, counts_row, 0), axis=1, keepdims=True)
    pend = pstart + padded
    lane = lax.broadcasted_iota(jnp.int32, (ne, LANES), 1)
    info_ref[...] = jnp.where(lane == 0, counts, jnp.where(lane == 1, pstart, jnp.where(lane == 2, ustart, pend)))
    blk = lax.broadcasted_iota(jnp.int32, (ne, nbp), 1) * bm
    owner = jnp.sum(jnp.where(pend <= blk, 1, 0), axis=0, keepdims=True)
    be_ref[...] = jnp.minimum(owner, ne - 1)
    pstart_f = pstart.astype(F32)

    def place(i, carry):
        cols = pl.ds(pl.multiple_of(i * tl, tl), tl)
        dest = rank_scr[:, cols] + pstart_f
        for k in range(TOP_K):
            hit = row == ek_scr[k:k + 1, cols]
            pos_ref[k:k + 1, cols] = (jnp.sum(jnp.where(hit, dest, 0.0), axis=0, keepdims=True)
                                      * float(TOKEN_ROWS)).astype(jnp.int32)
        pos_ref[TOP_K:SUBLANES, cols] = jnp.zeros((SUBLANES - TOP_K, tl), jnp.int32)
        return carry

    lax.fori_loop(0, t // tl, place, 0)


def _router(logits_t, router_bias, nb):
    t = logits_t.shape[1]
    nbp = -(-nb // LANES) * LANES
    return pl.pallas_call(
        functools.partial(_router_kernel, t=t, nbp=nbp),
        out_shape=[jax.ShapeDtypeStruct((SUBLANES, t), jnp.int32), jax.ShapeDtypeStruct((SUBLANES, t), F32),
                   jax.ShapeDtypeStruct((1, nbp), jnp.int32), jax.ShapeDtypeStruct((N_EXPERTS, LANES), jnp.int32)],
        scratch_shapes=[pltpu.VMEM((N_EXPERTS, t), F32), pltpu.VMEM((SUBLANES, t), jnp.int32)],
        compiler_params=pltpu.CompilerParams(vmem_limit_bytes=VMEM_LIMIT),
        name="router",
    )(logits_t, router_bias.astype(F32).reshape(N_EXPERTS, 1))


def _dispatch_plan(blk_e_row, info, nb):
    bm = MOE_BM
    pend = info[:, 3]
    blk_e = blk_e_row[0, :nb]
    n_used = pend[-1] // bm
    blk = jnp.arange(nb, dtype=jnp.int32)
    prev_e = jnp.concatenate([jnp.full((1,), -1, jnp.int32), blk_e[:-1]])
    first = jnp.logical_and(blk < n_used, blk_e != prev_e)
    slot = (jnp.cumsum(first.astype(jnp.int32)) - 1) % 2
    first_at = jnp.where(first, blk, nb)
    nxt_first = lax.cummin(jnp.concatenate([first_at[1:], jnp.full((1,), nb, jnp.int32)]), reverse=True)
    nxt = jnp.where(nxt_first < nb, blk_e[jnp.minimum(nxt_first, nb - 1)], -1)
    return (blk_e, first.astype(jnp.int32), slot.astype(jnp.int32), nxt.astype(jnp.int32),
            n_used.astype(jnp.int32).reshape(1))


def _shared_kernel(pos_ref, nxt_ref, x_ref, h_ref, gt_ref, mod_ref, sg_ref, su_ref, sd_ref, fn_ref, eo_hbm,
                   o_ref, buf0, buf1, acc_scr, sem, *, final):
    i = pl.program_id(0)
    n = pl.num_programs(0)
    tm = x_ref.shape[0]
    tr = TOKEN_ROWS

    half = D_MODEL // 2
    grp = SUBLANES

    def fetch_tokens(idx_ref, buf, slot, tok0):
        for tt in range(grp):
            tok = tok0 + tt
            for k in range(TOP_K):
                row = pl.multiple_of(idx_ref[k, tok], tr)
                pltpu.make_async_copy(eo_hbm.at[pl.ds(row, tr)], buf.at[k, pl.ds(pl.multiple_of(tok * tr, tr), tr)],
                                      sem.at[slot]).start(priority=k % 2)

    def arrived(buf, slot):
        for k in range(TOP_K):
            pltpu.make_async_copy(eo_hbm.at[pl.ds(0, tm * tr)], buf.at[k], sem.at[slot]).wait()

    def step(cur, cur_slot, nxt, nxt_slot):
        arrived(cur, cur_slot)

        def group(c, carry):
            tok0 = pl.multiple_of(c * grp, grp)
            fetch_tokens(nxt_ref, nxt, nxt_slot, tok0)
            gt = gt_ref[pl.ds(tok0, grp), :]
            r_lo = jnp.zeros((grp, half), F32)
            r_hi = r_lo
            for k in range(TOP_K):
                words = jnp.concatenate(
                    [cur[k, pl.ds(tok0 * tr + s, grp, stride=tr), :] for s in range(tr)], axis=1)
                e_lo, e_hi = _unpack_halves(words)
                r_lo = r_lo + gt[:, k:k + 1] * e_lo
                r_hi = r_hi + gt[:, k:k + 1] * e_hi
            acc_scr[pl.ds(tok0, grp), :] = jnp.concatenate([r_lo, r_hi], axis=1)
            return carry

        lax.fori_loop(0, tm // grp, group, 0)
        h_lo, h_hi = _unpack_halves(_load_token_tiles(h_ref, tm))
        h_lo = h_lo.astype(BF16)
        h_hi = h_hi.astype(BF16)
        gate = _dot(h_lo, sg_ref[0:half, :]) + _dot(h_hi, sg_ref[half:, :])
        up = _dot(h_lo, su_ref[0:half, :]) + _dot(h_hi, su_ref[half:, :])
        act = gate * jax.nn.sigmoid(gate) * up
        shared = _dot(act.astype(BF16), sd_ref[...])
        x = x_ref[...] + mod_ref[0][5:6] * (acc_scr[...] + shared)
        if final:
            x = _rms(x, fn_ref[...])
        o_ref[...] = x

        @pl.when(i == n - 1)
        def _():
            arrived(nxt, nxt_slot)

    @pl.when(i == 0)
    def _():
        def first(c, carry):
            fetch_tokens(pos_ref, buf0, 0, pl.multiple_of(c * grp, grp))
            return carry
        lax.fori_loop(0, tm // grp, first, 0)

    for parity, cur, nxt in ((0, buf0, buf1), (1, buf1, buf0)):
        @pl.when(i % 2 == parity)
        def _(parity=parity, cur=cur, nxt=nxt):
            step(cur, parity, nxt, 1 - parity)


def _shared(x, h, gates_t, pos, eo, mod, sg, su, sd, final_norm, n_ctx_rows, dec_seq, *, final, row0=0, rows=None):
    rows = x.shape[0] if rows is None else rows
    tm = ROW_TILE // 2
    b0 = row0 // tm
    steps = rows // tm
    grp = lambda i: _group_of(i + b0, tm, n_ctx_rows, dec_seq)
    row = pl.BlockSpec((tm, D_MODEL), lambda i: (i + b0, 0))
    prow = pl.BlockSpec((tm * TOKEN_ROWS, LANES), lambda i: (i + b0, 0))
    full = lambda a, b: pl.BlockSpec((a, b), lambda i: (0, 0))
    buf = pltpu.VMEM((TOP_K, tm * TOKEN_ROWS, LANES), eo.dtype)
    return pl.pallas_call(
        functools.partial(_shared_kernel, final=final),
        grid=(steps,),
        in_specs=[pl.BlockSpec((SUBLANES, tm), lambda i: (0, i + b0), memory_space=pltpu.SMEM),
                  pl.BlockSpec((SUBLANES, tm), lambda i: (0, jnp.minimum(i + 1, steps - 1) + b0),
                               memory_space=pltpu.SMEM),
                  row, prow, pl.BlockSpec((tm, SUBLANES), lambda i: (i + b0, 0)),
                  pl.BlockSpec((1, 6, D_MODEL), lambda i: (grp(i), 0, 0)),
                  full(D_MODEL, D_SHARED), full(D_MODEL, D_SHARED), full(D_SHARED, D_MODEL), full(1, D_MODEL),
                  pl.BlockSpec(memory_space=pl.ANY)],
        out_specs=pl.BlockSpec((tm, D_MODEL), lambda i: (i, 0)),
        out_shape=jax.ShapeDtypeStruct((rows, D_MODEL), F32),
        scratch_shapes=[buf, buf, pltpu.VMEM((tm, D_MODEL), F32), pltpu.SemaphoreType.DMA((2,))],
        compiler_params=_cparams(("arbitrary",)),
        name="shared_final" if final else "shared",
    )(pos, pos, x, h, gates_t, mod, sg.astype(BF16), su.astype(BF16), sd.astype(BF16),
      final_norm.reshape(1, D_MODEL), eo)


def kernel(x_prompt, x_sample, state_ret, state_s5_re, state_s5_im, c, c_ctx, w_ada, b_ada, norm_mix, norm_ffn, w_in, w_out, ret_decay, s5_lam_re, s5_lam_im, s5_log_dt, s5_b_re, s5_b_im, s5_c_re, s5_c_im, s5_d, s5_glu_w, s5_glu_b, hy_conv_w, hy_conv_b, hy_f1_w, hy_f1_b, hy_f2_w, hy_f2_b, hy_f3_w, hy_f3_b, hy_freq, hy_decay, hy_bias, moe_router, moe_router_bias, moe_w_gate, moe_w_up, moe_w_down, sh_w_gate, sh_w_up, sh_w_down, final_norm):
    n_ctx, seq, d = x_prompt.shape
    n_dec, dec_seq, _ = x_sample.shape
    n_ctx_rows = n_ctx * seq
    t = n_ctx_rows + n_dec * dec_seq

    x = jnp.concatenate([x_prompt.reshape(n_ctx_rows, d), x_sample.reshape(n_dec * dec_seq, d)], axis=0)
    cond = jnp.concatenate([c_ctx[None, :], c], axis=0)
    cond8 = jnp.pad(cond, ((0, SUBLANES - cond.shape[0]), (0, 0)))
    mods = _ada(cond8, w_ada, b_ada)[:, :1 + n_dec].reshape(DEPTH, 1 + n_dec, 6, d)

    cos2, sin2 = _rope_tables(dec_seq)
    no_rope = jnp.zeros((seq, LANES), F32)
    zero_ret = jnp.zeros((1, 2, RET_HEADS, RET_DK, RET_DV), F32)
    zero_s5 = jnp.zeros((n_ctx, 2, S5_GROUPS, S5_STATE), F32)
    dft_ctx = _dft_mats(seq)
    dft_dec = _dft_mats(dec_seq)

    w_in_bf = w_in.astype(BF16)
    mats = _s5_mats(s5_lam_re, s5_lam_im, s5_log_dt, s5_b_re, s5_b_im, s5_c_re, s5_c_im, s5_d)
    filt = (hy_f1_w, hy_f1_b, hy_f2_w, hy_f2_b, hy_f3_w, hy_f3_b, hy_freq, hy_decay)
    mults_ctx = _hy_spectrum(jax.vmap(functools.partial(_hy_filter_taps, seq))(*filt), dft_ctx)
    mults_dec = _hy_spectrum(jax.vmap(functools.partial(_hy_filter_taps, dec_seq))(*filt), dft_dec)

    ret_states = jnp.zeros((n_ctx, DEPTH, 2, RET_HEADS, RET_DK, RET_DV), F32)
    s5r_list, s5i_list = [], []
    for l in range(DEPTH):
        mod = mods[l]
        proj = _in_proj(x, mod, norm_mix[l], w_in_bf, l, n_ctx_rows, dec_seq)

        log_gamma = jax.nn.log_sigmoid(ret_decay[l].astype(F32))
        ret_o, ret_states = _retention(proj, log_gamma, zero_ret, no_rope, no_rope,
                                       row0=0, n_seq=n_ctx, seq_len=seq, hb=RET_HEADS, rope=False,
                                       dst=jnp.zeros((t, RET_WIDTH), BF16), states=ret_states, layer=l)
        ret_o, _ = _retention(proj, log_gamma, state_ret[:, l].astype(F32), cos2, sin2,
                              row0=n_ctx_rows, n_seq=n_dec, seq_len=dec_seq, hb=2, rope=True, dst=ret_o)

        s5_y, s5_re, s5_im = _s5(proj, mats, zero_s5, zero_s5, row0=0, n_seq=n_ctx, seq_len=seq, layer=l,
                                 dst=jnp.zeros((t, S5_WIDTH), F32))
        s5_y, _, _ = _s5(proj, mats, state_s5_re[:, l], state_s5_im[:, l],
                         row0=n_ctx_rows, n_seq=n_dec, seq_len=dec_seq, layer=l, dst=s5_y)
        s5r_list.append(s5_re)
        s5i_list.append(s5_im)

        hy_o = _hyena(proj, hy_conv_w[l], hy_conv_b[l], hy_bias[l], dft_ctx, mults_ctx, layer=l,
                      row0=0, n_seq=n_ctx, seq_len=seq, cb=HY_WIDTH, tk=seq, dst=jnp.zeros((t, HY_WIDTH), BF16))
        hy_o = _hyena(proj, hy_conv_w[l], hy_conv_b[l], hy_bias[l], dft_dec, mults_dec, layer=l,
                      row0=n_ctx_rows, n_seq=n_dec, seq_len=dec_seq, cb=HY_WIDTH // 2, tk=512, dst=hy_o)

        x, h2, logits = _out_proj(x, ret_o, s5_y, hy_o, mod, norm_ffn[l], s5_glu_w[l], s5_glu_b[l],
                                  w_out[l], moe_router[l], n_ctx_rows, dec_seq)

        nb = -(-(t * TOP_K) // MOE_BM) + N_EXPERTS
        pos, gates, blk_e_row, info = _router(logits, moe_router_bias[l], nb)
        blk_e, first, slot, nxt, n_used = _dispatch_plan(blk_e_row, info, nb)
        xs = _dispatch(h2, pos, info, n_used, nb)
        eo = _moe_grouped(xs, blk_e, first, slot, nxt, n_used, moe_w_gate, moe_w_up, moe_w_down, l)
        gates_t = gates.T

        sh = (sh_w_gate[l], sh_w_up[l], sh_w_down[l])
        if l < DEPTH - 1:
            x = _shared(x, h2, gates_t, pos, eo, mod, *sh, final_norm, n_ctx_rows, dec_seq, final=False)
        else:
            y_c = _shared(x, h2, gates_t, pos, eo, mod, *sh, final_norm, n_ctx_rows, dec_seq, final=True,
                          row0=0, rows=n_ctx_rows)
            y_d = _shared(x, h2, gates_t, pos, eo, mod, *sh, final_norm, n_ctx_rows, dec_seq, final=True,
                          row0=n_ctx_rows, rows=n_dec * dec_seq)

    return (y_c.reshape(n_ctx, seq, d), y_d.reshape(n_dec, dec_seq, d),
            ret_states, jnp.stack(s5r_list, axis=1), jnp.stack(s5i_list, axis=1))
```

```python
import functools
import math

import jax
import jax.numpy as jnp
from jax import lax
from jax.experimental import pallas as pl
from jax.experimental.pallas import tpu as pltpu

F32 = jnp.float32
BF16 = jnp.bfloat16

D_MODEL = 2048
DEPTH = 2
GRID_W = 64
RET_HEADS = 8
RET_DK = 128
RET_DV = 128
RET_WIDTH = RET_HEADS * RET_DV
RET_CHUNK = 256
ROPE_BASE = 10000.0
S5_WIDTH = 512
S5_GROUP = 16
S5_GROUPS = S5_WIDTH // S5_GROUP
S5_STATE = 64
S5_Q = 16
HY_WIDTH = 512
HY_BANDS = 16
IN_WIDTH = 4 * RET_WIDTH + S5_WIDTH + 3 * HY_WIDTH
U_COL = 4 * RET_WIDTH
HY_COL = U_COL + S5_WIDTH
N_EXPERTS = 64
TOP_K = 6
D_EXPERT = 512
D_SHARED = 512
ROUTED_SCALE = 2.5
EPS = 1e-6

LANES = 128
SUBLANES = 8
VMEM_LIMIT = 56 * 1024 * 1024

ROW_TILE = 512
MOE_BM = 256


def _cparams(sem):
    return pltpu.CompilerParams(dimension_semantics=sem, vmem_limit_bytes=VMEM_LIMIT)


def _dot(a, b):
    return jnp.dot(a, b, preferred_element_type=F32)


def _rms(x, g):
    var = jnp.mean(x * x, axis=-1, keepdims=True)
    return x * lax.rsqrt(var + EPS) * g


def _pack_halves(xb):
    n = xb.shape[1] // 2
    lo = lax.bitcast_convert_type(xb[:, :n].astype(F32), jnp.uint32) >> 16
    hi = lax.bitcast_convert_type(xb[:, n:].astype(F32), jnp.uint32)
    return lo | hi


def _unpack_halves(w):
    lo = lax.bitcast_convert_type(w << 16, F32)
    hi = lax.bitcast_convert_type(w & jnp.uint32(0xFFFF0000), F32)
    return lo, hi


TOKEN_ROWS = D_MODEL // 2 // LANES


def _store_token_tiles(ref, w):
    m = w.shape[0]
    for s in range(TOKEN_ROWS):
        ref[pl.ds(s, m, stride=TOKEN_ROWS), :] = w[:, s * LANES:(s + 1) * LANES]


def _load_token_tiles(ref, m):
    return jnp.concatenate([ref[pl.ds(s, m, stride=TOKEN_ROWS), :] for s in range(TOKEN_ROWS)], axis=1)


def _ada_kernel(c_ref, w_ref, b_ref, o_ref):
    c = c_ref[...]
    s = (c * jax.nn.sigmoid(c)).astype(BF16)
    o_ref[0] = _dot(s, w_ref[0].astype(BF16)) + b_ref[0]


def _ada(cond8, w_ada, b_ada):
    tn = 1024
    n = w_ada.shape[-1]
    return pl.pallas_call(
        _ada_kernel,
        grid=(DEPTH, n // tn),
        in_specs=[
            pl.BlockSpec((SUBLANES, D_MODEL), lambda l, j: (0, 0)),
            pl.BlockSpec((1, D_MODEL, tn), lambda l, j: (l, 0, j)),
            pl.BlockSpec((1, 1, tn), lambda l, j: (l, 0, j)),
        ],
        out_specs=pl.BlockSpec((1, SUBLANES, tn), lambda l, j: (l, 0, j)),
        out_shape=jax.ShapeDtypeStruct((DEPTH, SUBLANES, n), F32),
        compiler_params=_cparams(("parallel", "parallel")),
        name="ada",
    )(cond8, w_ada, b_ada.reshape(DEPTH, 1, n))


def _group_of(i, tile, n_ctx_rows, dec_seq):
    ctx_tiles = n_ctx_rows // tile
    per = dec_seq // tile
    return jnp.where(i < ctx_tiles, 0, 1 + (i - ctx_tiles) // per)


def _in_kernel(x_ref, mod_ref, g_ref, w_ref, o_ref, h_scr):
    @pl.when(pl.program_id(1) == 0)
    def _():
        m = mod_ref[0]
        h = _rms(x_ref[...], g_ref[...]) * (1.0 + m[1:2]) + m[0:1]
        h_scr[...] = h.astype(BF16)

    o_ref[...] = _dot(h_scr[...], w_ref[...])


def _in_proj(x, mod, g, w_bf, layer, n_ctx_rows, dec_seq):
    t = x.shape[0]
    tm, tn = 1024, 1024
    grp = functools.partial(_group_of, tile=tm, n_ctx_rows=n_ctx_rows, dec_seq=dec_seq)
    return pl.pallas_call(
        _in_kernel,
        grid=(t // tm, IN_WIDTH // tn),
        in_specs=[
            pl.BlockSpec((tm, D_MODEL), lambda i, j: (i, 0)),
            pl.BlockSpec((1, 6, D_MODEL), lambda i, j: (grp(i), 0, 0)),
            pl.BlockSpec((1, D_MODEL), lambda i, j: (0, 0)),
            pl.BlockSpec((None, D_MODEL, tn), lambda i, j: (layer, 0, j)),
        ],
        out_specs=pl.BlockSpec((tm, tn), lambda i, j: (i, j)),
        out_shape=jax.ShapeDtypeStruct((t, IN_WIDTH), F32),
        scratch_shapes=[pltpu.VMEM((tm, D_MODEL), BF16)],
        compiler_params=_cparams(("parallel", "arbitrary")),
        name="in_proj",
    )(x, mod, g.reshape(1, D_MODEL), w_bf)


def _ret_kernel(lg_ref, q_ref, k_ref, v_ref, gt_ref, cos_ref, sin_ref, s0_ref,
                o_ref, sfin_ref, acc_scr, q_scr, k_scr, *, seq_len, hb, rope):
    c = RET_CHUNK
    n_chunks = seq_len // c
    ii = lax.broadcasted_iota(jnp.int32, (c, c), 0)
    jj = lax.broadcasted_iota(jnp.int32, (c, c), 1)
    rel = (ii - jj).astype(F32)
    ci = lax.broadcasted_iota(jnp.int32, (c, 1), 0).astype(F32)
    one = jnp.ones((1, 1), F32)
    tdot = functools.partial(lax.dot_general, preferred_element_type=F32)

    def make_head(hh):
        head = pl.program_id(1) * hb + hh
        lgf = lg_ref[0, head]
        lgb = lg_ref[1, head]
        dmask = (jnp.where(rel >= 0, jnp.exp(lgf * jnp.maximum(rel, 0.0)), 0.0)
                 + jnp.where(rel <= 0, jnp.exp(lgb * jnp.maximum(-rel, 0.0)), 0.0))
        qd_f = jnp.exp(lgf * (ci + 1.0))
        kd_f = jnp.exp(lgf * (c - 1.0 - ci))
        cd_f = jnp.exp(lgf * c * one)
        qd_b = jnp.exp(lgb * (c - ci))
        kd_b = jnp.exp(lgb * ci)
        cd_b = jnp.exp(lgb * c * one)
        lanes = slice(hh * LANES, (hh + 1) * LANES)

        def rows_of(n):
            if isinstance(n, int):
                return slice(n * c, (n + 1) * c)
            return pl.ds(pl.multiple_of(n * c, c), c)

        def fwd_chunk(n, s_f):
            rows = rows_of(n)
            q = q_ref[rows, lanes]
            k = k_ref[rows, lanes] * (RET_DK ** -0.5)
            if rope:
                cs = cos_ref[rows, :]
                sn = sin_ref[rows, :]
                q = q * cs + pltpu.roll(q, RET_DK // 2, 1) * sn
                k = k * cs + pltpu.roll(k, RET_DK // 2, 1) * sn
            qb = q.astype(BF16)
            vb = v_ref[rows, lanes].astype(BF16)
            q_scr[rows, lanes] = qb
            k_scr[rows, lanes] = k
            scores = tdot(qb, k.astype(BF16), (((1,), (1,)), ((), ()))) * dmask
            inner = _dot(scores.astype(BF16), vb)
            cross = _dot(qb, s_f.astype(BF16)) * qd_f
            acc_scr[rows, lanes] = inner + cross
            upd = tdot((k * kd_f).astype(BF16), vb, (((0,), (0,)), ((), ())))
            return s_f * cd_f + upd

        def bwd_chunk(m, s_b):
            n = n_chunks - 1 - m
            rows = rows_of(n)
            qb = q_scr[rows, lanes]
            k = k_scr[rows, lanes]
            vb = v_ref[rows, lanes].astype(BF16)
            o = acc_scr[rows, lanes] + _dot(qb, s_b.astype(BF16)) * qd_b
            mu = jnp.mean(o, axis=-1, keepdims=True)
            oc = o - mu
            var = jnp.mean(oc * oc, axis=-1, keepdims=True)
            o = oc * lax.rsqrt(var + EPS)
            g = gt_ref[rows, lanes]
            o_ref[rows, lanes] = (g * jax.nn.sigmoid(g) * o).astype(o_ref.dtype)
            upd = tdot((k * kd_b).astype(BF16), vb, (((0,), (0,)), ((), ())))
            return s_b * cd_b + upd

        return fwd_chunk, bwd_chunk

    if n_chunks <= 4:
        for hh in range(hb):
            fwd_chunk, bwd_chunk = make_head(hh)
            s_f = s0_ref[0, 0, hh]
            s_b = s0_ref[0, 1, hh]
            for n in range(n_chunks):
                s_f = fwd_chunk(n, s_f)
            for m in range(n_chunks):
                s_b = bwd_chunk(m, s_b)
            sfin_ref[0, 0, hh] = s_f
            sfin_ref[0, 1, hh] = s_b
    else:
        fns = [make_head(hh) for hh in range(hb)]
        s_f = lax.fori_loop(0, n_chunks, lambda n, ss: tuple(f[0](n, s) for f, s in zip(fns, ss)),
                            tuple(s0_ref[0, 0, hh] for hh in range(hb)))
        s_b = lax.fori_loop(0, n_chunks, lambda m, ss: tuple(f[1](m, s) for f, s in zip(fns, ss)),
                            tuple(s0_ref[0, 1, hh] for hh in range(hb)))
        for hh in range(hb):
            sfin_ref[0, 0, hh] = s_f[hh]
            sfin_ref[0, 1, hh] = s_b[hh]


def _into(kernel_fn, n_in, dst):
    dsts = [d for d in (dst if isinstance(dst, (list, tuple)) else [dst])]
    outs = [k for k, d in enumerate(dsts) if d is not None]
    if not outs:
        return kernel_fn, [], [], {}

    def body(*refs):
        return kernel_fn(*refs[:n_in], *refs[n_in + len(outs):])

    return (body, [pl.BlockSpec(memory_space=pl.ANY)] * len(outs), [dsts[k] for k in outs],
            {n_in + pos: k for pos, k in enumerate(outs)})


def _retention(proj, log_gamma, s0, cos2, sin2, *, row0, n_seq, seq_len, hb, rope, dst=None,
               states=None, layer=0):
    blk0 = row0 // seq_len
    body, dst_spec, dst_arg, alias = _into(
        functools.partial(_ret_kernel, seq_len=seq_len, hb=hb, rope=rope), 8, [dst, states])
    if states is None:
        st_spec = pl.BlockSpec((1, 2, hb, RET_DK, RET_DV), lambda b, h, lg: (b, 0, h, 0, 0))
        st_shape = jax.ShapeDtypeStruct((n_seq, 2, RET_HEADS, RET_DK, RET_DV), F32)
    else:
        st_spec = pl.BlockSpec((1, None, 2, hb, RET_DK, RET_DV), lambda b, h, lg: (b, layer, 0, h, 0, 0))
        st_shape = jax.ShapeDtypeStruct(states.shape, F32)
    w = hb * LANES
    hblocks = RET_HEADS // hb
    col = lambda part: (lambda b, h, lg: (blk0 + b, part * hblocks + h))
    grid_spec = pltpu.PrefetchScalarGridSpec(
        num_scalar_prefetch=1,
        grid=(n_seq, hblocks),
        in_specs=[
            pl.BlockSpec((seq_len, w), col(0)),
            pl.BlockSpec((seq_len, w), col(1)),
            pl.BlockSpec((seq_len, w), col(2)),
            pl.BlockSpec((seq_len, w), col(3)),
            pl.BlockSpec((seq_len, LANES), lambda b, h, lg: (0, 0)),
            pl.BlockSpec((seq_len, LANES), lambda b, h, lg: (0, 0)),
            pl.BlockSpec((1, 2, hb, RET_DK, RET_DV), lambda b, h, lg: (b if s0.shape[0] > 1 else 0, 0, h, 0, 0)),
        ] + dst_spec,
        out_specs=[
            pl.BlockSpec((seq_len, w), lambda b, h, lg: (blk0 + b, h)),
            st_spec,
        ],
        scratch_shapes=[
            pltpu.VMEM((seq_len, w), F32),
            pltpu.VMEM((seq_len, w), BF16),
            pltpu.VMEM((seq_len, w), F32),
        ],
    )
    return pl.pallas_call(
        body,
        grid_spec=grid_spec,
        out_shape=[jax.ShapeDtypeStruct((proj.shape[0], RET_WIDTH), BF16), st_shape],
        input_output_aliases=alias,
        compiler_params=_cparams(("parallel", "arbitrary")),
        name="retention",
    )(log_gamma, proj, proj, proj, proj, cos2, sin2, s0, *dst_arg)


def _rope_tables(seq_len):
    rows_n = seq_len // GRID_W
    rows = jnp.repeat(jnp.arange(rows_n, dtype=F32), GRID_W)
    cols = jnp.tile(jnp.arange(GRID_W, dtype=F32), rows_n)
    nf = RET_DK // 4
    inv = ROPE_BASE ** (-jnp.arange(nf, dtype=F32) / nf)
    ang = jnp.concatenate([rows[:, None] * inv, cols[:, None] * inv], axis=-1)
    cs, sn = jnp.cos(ang), jnp.sin(ang)
    return jnp.concatenate([cs, cs], axis=-1), jnp.concatenate([-sn, sn], axis=-1)


S5_GB = LANES // S5_GROUP
S5_W = S5_Q * LANES
S5_SPLIT = 4
S5_SW = S5_GB * S5_STATE
S5_SB = S5_SW // LANES


def _s5_kernel(u_ref, bq_ref, k_ref, cq_ref, ar_ref, ai_ref, d_ref, h0_ref,
               y_ref, hf_ref, ub_scr, sm_scr, hp_scr, hpb_scr, t_scr, *, n_seq, n_chunks):
    s = pl.program_id(1)
    m = n_seq * n_chunks
    q = S5_Q
    nblk = S5_W // LANES
    sb = S5_SB

    per = q // S5_SPLIT
    kw = S5_W // S5_SPLIT

    @pl.when(s == 0)
    def _():
        for j in range(q):
            ub_scr[j // per, :, (j % per) * LANES:(j % per + 1) * LANES] = (
                u_ref[pl.ds(j, m, stride=q), :].astype(BF16))
        for ib in range(S5_SPLIT):
            for j in range(q):
                c0 = (q - 1 - j) * LANES + ib * kw
                t_scr[ib, j * LANES:(j + 1) * LANES, :] = k_ref[0, :, c0:c0 + kw]

    @pl.when(s < S5_SPLIT)
    def _():
        part = _dot(ub_scr[jnp.minimum(s, S5_SPLIT - 1)], bq_ref[0])

        @pl.when(s == 0)
        def _():
            for cb in range(nblk):
                sm_scr[cb] = part[:, cb * LANES:(cb + 1) * LANES]

        @pl.when(s > 0)
        def _():
            for cb in range(nblk):
                sm_scr[cb] += part[:, cb * LANES:(cb + 1) * LANES]

    @pl.when(s == S5_SPLIT - 1)
    def _():
        ar = ar_ref[0]
        ai = ai_ref[0]
        h0 = h0_ref[0]
        blk = lambda a, cb: a[:, cb * LANES:(cb + 1) * LANES]

        def body(n, carry):
            rows_f = pl.ds(n, n_seq, stride=n_chunks)
            rows_b = pl.ds(n_chunks - 1 - n, n_seq, stride=n_chunks)
            new = list(carry)
            for d, rows in ((0, rows_f), (1, rows_b)):
                for c in range(sb):
                    re_i = d * sb + c
                    im_i = (2 + d) * sb + c
                    hr, hi = carry[re_i], carry[im_i]
                    hp_scr[re_i, rows, :] = hr
                    hp_scr[im_i, rows, :] = hi
                    a_r, a_i = blk(ar, re_i), blk(ai, re_i)
                    new[re_i] = a_r * hr - a_i * hi + sm_scr[re_i, rows, :]
                    new[im_i] = a_r * hi + a_i * hr + sm_scr[im_i, rows, :]
            return tuple(new)

        fin = lax.fori_loop(0, n_chunks, body, tuple(blk(h0, cb) for cb in range(nblk)))
        hf_ref[0] = jnp.concatenate(fin, axis=1)
        for cb in range(nblk):
            hpb_scr[:, cb * LANES:(cb + 1) * LANES] = hp_scr[cb].astype(BF16)

    @pl.when(s >= S5_SPLIT)
    def _():
        ub = jnp.concatenate([ub_scr[k] for k in range(S5_SPLIT)], axis=1)
        y = _dot(ub, t_scr[jnp.maximum(s - S5_SPLIT, 0)]) + _dot(hpb_scr[...], cq_ref[0])
        dd = d_ref[0]
        for ii in range(per):
            rows = pl.ds((s - S5_SPLIT) * per + ii, m, stride=q)
            yi = y[:, ii * LANES:(ii + 1) * LANES] + dd * u_ref[rows, :]
            y_ref[rows, :] = jax.nn.gelu(yi)


def _s5_expand_kernel(mc_ref, o_ref, *, xsize, ysize):
    xs, ys, gs = xsize.bit_length() - 1, ysize.bit_length() - 1, S5_GB.bit_length() - 1
    assert xsize == 1 << xs and ysize == 1 << ys and S5_GB == 1 << gs
    cw = o_ref.shape[2]
    nc = mc_ref.shape[2]
    col0 = pl.program_id(1) * cw
    r = lax.broadcasted_iota(jnp.int32, (nc, cw), 0)
    col = lax.broadcasted_iota(jnp.int32, (nc, cw), 1) + col0
    spread = jnp.logical_and(r >> ys == col >> (ys + gs), (r & (ysize - 1)) == (col & (ysize - 1)))
    big = _dot(mc_ref[0], jnp.where(spread, 1.0, 0.0).astype(BF16))
    row = lax.broadcasted_iota(jnp.int32, big.shape, 0)
    colb = lax.broadcasted_iota(jnp.int32, big.shape, 1) + col0
    same = ((row >> xs) & (S5_GB - 1)) == ((colb >> ys) & (S5_GB - 1))
    o_ref[0] = jnp.where(same, big, 0.0).astype(BF16)


def _s5_expand(mc, *, xsize, ysize):
    nb, rows, nc = mc.shape
    cols = nc * S5_GB
    cw = 512
    return pl.pallas_call(
        functools.partial(_s5_expand_kernel, xsize=xsize, ysize=ysize),
        grid=(nb, cols // cw),
        in_specs=[pl.BlockSpec((1, rows, nc), lambda b, j: (b, 0, 0))],
        out_specs=pl.BlockSpec((1, rows, cw), lambda b, j: (b, 0, j)),
        out_shape=jax.ShapeDtypeStruct((nb, rows, cols), BF16),
        compiler_params=_cparams(("parallel", "parallel")),
        name="s5_expand",
    )(mc)


def _s5_mats(lam_re, lam_im, log_dt, b_re, b_im, c_re, c_im, d):
    flat = lambda a: a.reshape((-1,) + a.shape[2:])
    kc, bq, cq, ar, ai, dd = map(flat, jax.vmap(_s5_compact)(lam_re, lam_im, log_dt, b_re, b_im, c_re, c_im, d))
    ch, p = S5_GROUP, S5_STATE
    return (_s5_expand(kc, xsize=ch, ysize=ch),
            _s5_expand(bq, xsize=ch, ysize=p),
            _s5_expand(cq, xsize=p, ysize=ch),
            ar, ai, dd)


def _s5_compact(lam_re, lam_im, log_dt, b_re, b_im, c_re, c_im, d):
    q, g, p, ch = S5_Q, S5_GROUPS, S5_STATE, S5_GROUP
    lam = lax.complex(jnp.minimum(lam_re.astype(F32), -1e-4), lam_im.astype(F32))
    ldt = lam * jnp.exp(log_dt.astype(F32))[..., None]
    lam_bar = jnp.exp(ldt)
    b_bar = ((lam_bar - 1.0) / lam)[..., None] * lax.complex(b_re.astype(F32), b_im.astype(F32))
    cc = lax.complex(c_re.astype(F32), c_im.astype(F32))
    pw = jnp.exp(ldt[..., None] * jnp.arange(q + 1, dtype=F32))
    hi = lax.Precision.HIGHEST
    lag = jnp.arange(2 * q, dtype=F32) - (q - 1)
    wf = jnp.where(lag >= 0, jnp.exp(ldt[0][..., None] * jnp.maximum(lag, 0.0)), 0.0)
    wb = jnp.where(lag <= 0, jnp.exp(ldt[1][..., None] * jnp.maximum(-lag, 0.0)), 0.0)
    kc = jnp.real(jnp.einsum('gcp,gpd,gpe->gedc', cc[0], wf, b_bar[0], precision=hi)
                  + jnp.einsum('gcp,gpd,gpe->gedc', cc[1], wb, b_bar[1], precision=hi))

    pw_dn = jnp.exp(ldt[..., None] * (q - jnp.arange(q + 1, dtype=F32)))
    bf = pw_dn[0][..., 1:][:, :, :, None] * b_bar[0][:, :, None, :]
    bb = pw[1][..., :q][:, :, :, None] * b_bar[1][:, :, None, :]
    to_rows = lambda m: m.transpose(0, 2, 3, 1).reshape(g, q * ch, p)
    bq = jnp.concatenate([to_rows(jnp.real(bf)), to_rows(jnp.real(bb)),
                          to_rows(jnp.imag(bf)), to_rows(jnp.imag(bb))], axis=-1)

    cf = cc[0].transpose(0, 2, 1)[:, :, None, :] * pw[0][..., 1:][:, :, :, None]
    cb = cc[1].transpose(0, 2, 1)[:, :, None, :] * pw_dn[1][..., :q][:, :, :, None]
    to_cols = lambda m: m.reshape(g, p, q * ch)
    cq = jnp.concatenate([to_cols(jnp.real(cf)), to_cols(jnp.real(cb)),
                          to_cols(-jnp.imag(cf)), to_cols(-jnp.imag(cb))], axis=1)

    gb, nb = S5_GB, g // S5_GB
    rows_of = lambda a, outer, inner: (a.reshape(nb, gb, outer, inner, a.shape[-1]).transpose(0, 2, 1, 3, 4)
                                       .reshape(nb, outer * gb * inner, a.shape[-1]).astype(BF16))
    kc = kc.reshape(nb, gb * ch, 2 * q * ch).astype(BF16)

    lq = pw[..., q].reshape(2, nb, 1, S5_SW)
    ar = jnp.concatenate([jnp.real(lq[0]), jnp.real(lq[1])], axis=-1)
    ai = jnp.concatenate([jnp.imag(lq[0]), jnp.imag(lq[1])], axis=-1)
    dd = d.astype(F32).reshape(nb, 1, LANES)
    return kc, rows_of(bq, q, ch), rows_of(cq, 4, p), ar, ai, dd


def _s5(proj, mats, h0_re, h0_im, *, row0, n_seq, seq_len, layer=0, dst=None):
    q, p = S5_Q, S5_STATE
    n_chunks = seq_len // q
    m = n_seq * n_chunks
    rows = n_seq * seq_len
    nb = S5_GROUPS // S5_GB
    nblk = S5_W // LANES
    kw = S5_W // S5_SPLIT
    tm, bq, cq, ar, ai, dd = mats
    part = lambda a: a.astype(F32).reshape(n_seq, nb, S5_SW)
    h0 = jnp.concatenate([part(h0_re[:, 0]), part(h0_re[:, 1]), part(h0_im[:, 0]), part(h0_im[:, 1])],
                         axis=-1).transpose(1, 0, 2)
    per_b = lambda shape: pl.BlockSpec((1,) + shape, lambda b, s: (b, 0, 0))
    b0 = layer * nb
    per_l = lambda shape: pl.BlockSpec((1,) + shape, lambda b, s: (b0 + b, 0, 0))
    body, dst_spec, dst_arg, alias = _into(functools.partial(_s5_kernel, n_seq=n_seq, n_chunks=n_chunks), 8, dst)
    y, hf = pl.pallas_call(
        body,
        grid=(nb, 2 * S5_SPLIT),
        in_specs=[
            pl.BlockSpec((rows, LANES), lambda b, s: (row0 // rows, U_COL // LANES + b)),
            pl.BlockSpec((1, kw, 4 * S5_SW), lambda b, s: (b0 + b, jnp.minimum(s, S5_SPLIT - 1), 0)),
            per_l((LANES, 2 * q * LANES)),
            pl.BlockSpec((1, 4 * S5_SW, kw), lambda b, s: (b0 + b, 0, jnp.maximum(s - S5_SPLIT, 0))),
            per_l((1, 2 * S5_SW)), per_l((1, 2 * S5_SW)), per_l((1, LANES)), per_b((n_seq, 4 * S5_SW)),
        ] + dst_spec,
        out_specs=[pl.BlockSpec((rows, LANES), lambda b, s: (row0 // rows, b)), per_b((n_seq, 4 * S5_SW))],
        out_shape=[jax.ShapeDtypeStruct((proj.shape[0], S5_WIDTH), F32),
                   jax.ShapeDtypeStruct((nb, n_seq, 4 * S5_SW), F32)],
        scratch_shapes=[pltpu.VMEM((S5_SPLIT, m, kw), BF16), pltpu.VMEM((nblk, m, LANES), F32),
                        pltpu.VMEM((nblk, m, LANES), F32), pltpu.VMEM((m, 4 * S5_SW), BF16),
                        pltpu.VMEM((S5_SPLIT, S5_W, kw), BF16)],
        input_output_aliases=alias,
        compiler_params=_cparams(("parallel", "arbitrary")),
        name="s5",
    )(proj, bq, tm, cq, ar, ai, dd, h0, *dst_arg)
    hf = hf.reshape(nb, n_seq, 4, S5_GB, p).transpose(1, 2, 0, 3, 4).reshape(n_seq, 4, S5_GROUPS, p)
    return y, hf[:, 0:2], hf[:, 2:4]


def _conv3(x, w, b):
    n = x.shape[0]
    row = lax.broadcasted_iota(jnp.int32, x.shape, 0)
    prev = jnp.where(row == 0, 0.0, pltpu.roll(x, 1, 0))
    nxt = jnp.where(row == n - 1, 0.0, pltpu.roll(x, n - 1, 0))
    return prev * w[0:1] + x * w[1:2] + nxt * w[2:3] + b


def _hy_fwd_kernel(x0_ref, x1_ref, v_ref, w0_ref, w1_ref, wv_ref, b0_ref, b1_ref, bv_ref,
                   fc_ref, fs_ref, m1_ref, m2_ref, m3_ref, p_ref, z_ref, x0c_ref, zb_scr):
    @pl.when(pl.program_id(2) == 0)
    def _():
        z = _conv3(x1_ref[...], w1_ref[...], b1_ref[...]) * _conv3(v_ref[...], wv_ref[...], bv_ref[...])
        z_ref[...] = z
        zb_scr[...] = z.astype(BF16)
        x0c_ref[...] = _conv3(x0_ref[...], w0_ref[...], b0_ref[...])

    zb = zb_scr[...]
    a = _dot(fc_ref[...], zb)
    b = _dot(fs_ref[...], zb)
    m2 = m2_ref[...]
    p_ref[0, 0] = (m1_ref[...] * a + m2 * b).astype(BF16)
    p_ref[0, 1] = (m3_ref[...] * b - m2 * a).astype(BF16)


def _hy_inv_kernel(p_ref, gc_ref, gs_ref, z_ref, x0c_ref, bias_ref, o_ref):
    conv = _dot(gc_ref[...], p_ref[0, 0]) + _dot(gs_ref[...], p_ref[0, 1])
    o_ref[...] = (x0c_ref[...] * (conv + bias_ref[...] * z_ref[...])).astype(o_ref.dtype)


def _hy_short_kernel(x0_ref, x1_ref, v_ref, w0_ref, w1_ref, wv_ref, b0_ref, b1_ref, bv_ref,
                     fc_ref, fs_ref, gs_ref, m1_ref, m2_ref, m3_ref, bias_ref, o_ref):
    z = _conv3(x1_ref[...], w1_ref[...], b1_ref[...]) * _conv3(v_ref[...], wv_ref[...], bv_ref[...])
    zb = z.astype(BF16)
    a = _dot(fc_ref[...], zb)
    b = _dot(fs_ref[...], zb)
    m2 = m2_ref[...]
    p_re = (m1_ref[...] * a + m2 * b).astype(BF16)
    p_im = (m3_ref[...] * b - m2 * a).astype(BF16)
    conv = _dot(fc_ref[...], p_re) + _dot(gs_ref[...], p_im)
    x0c = _conv3(x0_ref[...], w0_ref[...], b0_ref[...])
    o_ref[...] = (x0c * (conv + bias_ref[...] * z)).astype(o_ref.dtype)


def _dft_mats(seq_len):
    n, w = seq_len, 64
    k = jnp.arange(n, dtype=jnp.int32)
    ang = lambda j: ((k[:, None] * j[None, :]) % (2 * n)).astype(F32) * (math.pi / n)
    ang_a = ang(jnp.arange(n // w, dtype=jnp.int32) * w)
    ang_b = ang(jnp.arange(w, dtype=jnp.int32))
    ca, sa = jnp.cos(ang_a)[:, :, None], jnp.sin(ang_a)[:, :, None]
    cb, sb = jnp.cos(ang_b)[:, None, :], jnp.sin(ang_b)[:, None, :]
    cm = (ca * cb - sa * sb).reshape(n, n)
    sm = -(sa * cb + ca * sb).reshape(n, n)
    nyq = jnp.where(k % 2 == 0, 1.0, -1.0).astype(F32)
    return cm.astype(BF16), sm.at[0, :].set(nyq).astype(BF16), sm.at[:, 0].set(nyq).astype(BF16)


def _hy_filter_taps(seq_len, f1_w, f1_b, f2_w, f2_b, f3_w, f3_b, freq, decay):
    n = seq_len
    t = (jnp.arange(n, dtype=F32) / n)[:, None]
    bands = jnp.arange(1, HY_BANDS + 1, dtype=F32)[None, :]
    z = jnp.concatenate([t, jnp.cos(2.0 * math.pi * t * bands), jnp.sin(2.0 * math.pi * t * bands)], axis=-1)
    hi = lax.Precision.HIGHEST
    fr = freq.astype(F32)
    h = jnp.sin(fr * (jnp.dot(z, f1_w.astype(F32), precision=hi) + f1_b.astype(F32)))
    h = jnp.sin(fr * (jnp.dot(h, f2_w.astype(F32), precision=hi) + f2_b.astype(F32)))
    h = jnp.dot(h, f3_w.astype(F32), precision=hi) + f3_b.astype(F32)
    h = h * jnp.exp(-t * jnp.abs(decay.astype(F32)))
    h = h.reshape(n, 2, HY_WIDTH)
    h = h / jnp.sum(jnp.abs(h), axis=(0, 1), keepdims=True)
    return h.reshape(n, 2 * HY_WIDTH)


def _hy_spectrum_kernel(h_ref, cm_ref, sm_ref, ck_ref, sk_ref, m1_ref, m2_ref, m3_ref, *, seq_len):
    w = HY_WIDTH
    tk = cm_ref.shape[0]
    hb = h_ref[0].astype(BF16)
    xc = _dot(cm_ref[...], hb)
    xs = _dot(sm_ref[...], hb)
    k = pl.program_id(1) * tk + lax.broadcasted_iota(jnp.int32, (tk, 1), 0)
    first = k == 0
    sf = jnp.where(first, 0.0, -xs[:, :w])
    sb = jnp.where(first, 0.0, -xs[:, w:])
    ck, sk = ck_ref[...], sk_ref[...]
    hr = xc[:, :w] + ck * xc[:, w:] - sk * sb
    him = -sf + ck * sb + sk * xc[:, w:]
    nyq = xs[:, :w] - xs[:, w:]
    wk = jnp.where(first, 1.0, 2.0) / (2.0 * seq_len)
    m1_ref[0] = hr * wk
    m2_ref[0] = jnp.where(first, 0.0, -him) * wk
    m3_ref[0] = jnp.where(first, nyq, hr) * wk


def _hy_spectrum(h, dft):
    depth, n, _ = h.shape
    cm, sm, _ = dft
    tk = min(n, 512)
    ang = jnp.arange(n, dtype=F32)[:, None] * (math.pi / n)
    out = jax.ShapeDtypeStruct((depth, n, HY_WIDTH), F32)
    oblk = pl.BlockSpec((1, tk, HY_WIDTH), lambda l, k: (l, k, 0))
    return pl.pallas_call(
        functools.partial(_hy_spectrum_kernel, seq_len=n),
        grid=(depth, n // tk),
        in_specs=[pl.BlockSpec((1, n, 2 * HY_WIDTH), lambda l, k: (l, 0, 0)),
                  pl.BlockSpec((tk, n), lambda l, k: (k, 0)), pl.BlockSpec((tk, n), lambda l, k: (k, 0)),
                  pl.BlockSpec((tk, 1), lambda l, k: (k, 0)), pl.BlockSpec((tk, 1), lambda l, k: (k, 0))],
        out_specs=[oblk, oblk, oblk],
        out_shape=[out, out, out],
        compiler_params=_cparams(("parallel", "parallel")),
        name="hyena_spectrum",
    )(h, cm, sm, jnp.cos(ang), jnp.sin(ang))


def _hyena(proj, conv_w, conv_b, bias, dft, mults, *, row0, n_seq, seq_len, cb, tk, layer=0, dst=None):
    blk0 = row0 // seq_len
    nc = HY_WIDTH // cb
    nk = seq_len // tk
    c0 = HY_COL // cb
    cm, sm, smt = dft
    m1, m2, m3 = mults
    xcol = lambda part: pl.BlockSpec((seq_len, cb), lambda b, c, k: (blk0 + b, c0 + part * nc + c))
    wcol = lambda part: pl.BlockSpec((3, cb), lambda b, c, k: (0, part * nc + c))
    bcol = lambda part: pl.BlockSpec((1, cb), lambda b, c, k: (0, part * nc + c))
    frow = pl.BlockSpec((tk, seq_len), lambda b, c, k: (k, 0))
    mblk = pl.BlockSpec((None, tk, cb), lambda b, c, k: (layer, k, c))
    cb2 = conv_b.reshape(1, 3 * HY_WIDTH)
    if nk == 1:
        body, dst_spec, dst_arg, alias = _into(_hy_short_kernel, 16, dst)
        return pl.pallas_call(
            body,
            grid=(n_seq, nc, 1),
            in_specs=[xcol(0), xcol(1), xcol(2), wcol(0), wcol(1), wcol(2), bcol(0), bcol(1), bcol(2),
                      frow, frow, frow, mblk, mblk, mblk,
                      pl.BlockSpec((1, cb), lambda b, c, k: (0, c))] + dst_spec,
            out_specs=pl.BlockSpec((seq_len, cb), lambda b, c, k: (blk0 + b, c)),
            out_shape=jax.ShapeDtypeStruct((proj.shape[0], HY_WIDTH), BF16),
            input_output_aliases=alias,
            compiler_params=_cparams(("parallel", "parallel", "arbitrary")),
            name="hyena_short",
        )(proj, proj, proj, conv_w, conv_w, conv_w, cb2, cb2, cb2, cm, sm, smt, m1, m2, m3,
          bias.reshape(1, HY_WIDTH), *dst_arg)
    pspec, z, x0c = pl.pallas_call(
        _hy_fwd_kernel,
        grid=(n_seq, nc, nk),
        in_specs=[xcol(0), xcol(1), xcol(2), wcol(0), wcol(1), wcol(2), bcol(0), bcol(1), bcol(2),
                  frow, frow, mblk, mblk, mblk],
        out_specs=[
            pl.BlockSpec((1, 2, tk, cb), lambda b, c, k: (b, 0, k, c)),
            pl.BlockSpec((seq_len, cb), lambda b, c, k: (b, c)),
            pl.BlockSpec((seq_len, cb), lambda b, c, k: (b, c)),
        ],
        out_shape=[
            jax.ShapeDtypeStruct((n_seq, 2, seq_len, HY_WIDTH), BF16),
            jax.ShapeDtypeStruct((n_seq * seq_len, HY_WIDTH), F32),
            jax.ShapeDtypeStruct((n_seq * seq_len, HY_WIDTH), F32),
        ],
        scratch_shapes=[pltpu.VMEM((seq_len, cb), BF16)],
        compiler_params=_cparams(("parallel", "parallel", "arbitrary")),
        name="hyena_fwd",
    )(proj, proj, proj, conv_w, conv_w, conv_w, cb2, cb2, cb2, cm, sm, m1, m2, m3)
    grow = pl.BlockSpec((tk, seq_len), lambda b, c, k: (k, 0))
    tile = pl.BlockSpec((tk, cb), lambda b, c, k: (b * nk + k, c))
    body, dst_spec, dst_arg, alias = _into(_hy_inv_kernel, 6, dst)
    return pl.pallas_call(
        body,
        grid=(n_seq, nc, nk),
        in_specs=[pl.BlockSpec((1, 2, seq_len, cb), lambda b, c, k: (b, 0, 0, c)),
                  grow, grow, tile, tile, pl.BlockSpec((1, cb), lambda b, c, k: (0, c))] + dst_spec,
        out_specs=pl.BlockSpec((tk, cb), lambda b, c, k: (row0 // tk + b * nk + k, c)),
        out_shape=jax.ShapeDtypeStruct((proj.shape[0], HY_WIDTH), BF16),
        input_output_aliases=alias,
        compiler_params=_cparams(("parallel", "parallel", "arbitrary")),
        name="hyena_inv",
    )(pspec, cm, smt, z, x0c, bias.reshape(1, HY_WIDTH), *dst_arg)


def _out_kernel(x_ref, ret_ref, s5_ref, hy_ref, mod_ref, g_ref, gw_ref, gb_ref,
                wr_ref, ws_ref, wh_ref, rt_ref, xo_ref, h_ref, lg_ref):
    m = mod_ref[0]
    y = s5_ref[...]
    s5o = y * jax.nn.sigmoid(_dot(y.astype(BF16), gw_ref[...]) + gb_ref[...])
    mix = (_dot(ret_ref[...], wr_ref[...]) + _dot(s5o.astype(BF16), ws_ref[...])
           + _dot(hy_ref[...], wh_ref[...]))
    x = x_ref[...] + m[2:3] * mix
    xo_ref[...] = x
    h = _rms(x, g_ref[...]) * (1.0 + m[4:5]) + m[3:4]
    hb = h.astype(BF16)
    _store_token_tiles(h_ref, _pack_halves(hb))
    lg_ref[...] = lax.dot_general(rt_ref[...], hb, (((1,), (1,)), ((), ())), preferred_element_type=F32)


def _out_proj(x, ret_o, s5_y, hy_o, mod, g, glu_w, glu_b, w_out, router, n_ctx_rows, dec_seq):
    t = x.shape[0]
    tm = ROW_TILE
    grp = functools.partial(_group_of, tile=tm, n_ctx_rows=n_ctx_rows, dec_seq=dec_seq)
    row = lambda w: pl.BlockSpec((tm, w), lambda i: (i, 0))
    full = lambda a, b: pl.BlockSpec((a, b), lambda i: (0, 0))
    wo = w_out.astype(BF16)
    return pl.pallas_call(
        _out_kernel,
        grid=(t // tm,),
        in_specs=[row(D_MODEL), row(RET_WIDTH), row(S5_WIDTH), row(HY_WIDTH),
                  pl.BlockSpec((1, 6, D_MODEL), lambda i: (grp(i), 0, 0)),
                  full(1, D_MODEL), full(S5_WIDTH, S5_WIDTH), full(1, S5_WIDTH),
                  full(RET_WIDTH, D_MODEL), full(S5_WIDTH, D_MODEL), full(HY_WIDTH, D_MODEL),
                  full(N_EXPERTS, D_MODEL)],
        out_specs=[row(D_MODEL), pl.BlockSpec((tm * TOKEN_ROWS, LANES), lambda i: (i, 0)),
                   pl.BlockSpec((N_EXPERTS, tm), lambda i: (0, i))],
        out_shape=[jax.ShapeDtypeStruct((t, D_MODEL), F32), jax.ShapeDtypeStruct((t * TOKEN_ROWS, LANES), jnp.uint32),
                   jax.ShapeDtypeStruct((N_EXPERTS, t), F32)],
        compiler_params=_cparams(("parallel",)),
        name="out_proj",
    )(x, ret_o, s5_y, hy_o, mod, g.reshape(1, D_MODEL), glu_w.astype(BF16), glu_b.reshape(1, S5_WIDTH),
      wo[:RET_WIDTH], wo[RET_WIDTH:RET_WIDTH + S5_WIDTH], wo[RET_WIDTH + S5_WIDTH:], router.T.astype(BF16))


def _moe_kernel(be_ref, first_ref, slot_ref, nxt_ref, nu_ref, xs_ref, wg_hbm, wu_hbm, wd_hbm, o_ref,
                wg_f, wu_f, wd_f, wg_b, wu_b, wd_b, sem, *, layer):
    i = pl.program_id(0)

    def copies(e, s):
        return (pltpu.make_async_copy(wg_hbm.at[layer, e], wg_f.at[s], sem.at[s, 0]),
                pltpu.make_async_copy(wu_hbm.at[layer, e], wu_f.at[s], sem.at[s, 1]),
                pltpu.make_async_copy(wd_hbm.at[layer, e], wd_f.at[s], sem.at[s, 2]))

    @pl.when(i == 0)
    def _():
        for cp in copies(be_ref[0], 0):
            cp.start()

    @pl.when(first_ref[i] == 1)
    def _():
        s = slot_ref[i]
        for cp in copies(be_ref[i], s):
            cp.wait()

        @pl.when(nxt_ref[i] >= 0)
        def _():
            for cp in copies(nxt_ref[i], 1 - s):
                cp.start(priority=1)

        wg_b[...] = wg_f[s].astype(BF16)
        wu_b[...] = wu_f[s].astype(BF16)
        wd_b[...] = wd_f[s].astype(BF16)

    @pl.when(i < nu_ref[0])
    def _():
        half = D_MODEL // 2
        x_lo, x_hi = _unpack_halves(_load_token_tiles(xs_ref, MOE_BM))
        x_lo = x_lo.astype(BF16)
        x_hi = x_hi.astype(BF16)
        gate = _dot(x_lo, wg_b[0:half, :]) + _dot(x_hi, wg_b[half:, :])
        up = _dot(x_lo, wu_b[0:half, :]) + _dot(x_hi, wu_b[half:, :])
        hb = gate * jax.nn.sigmoid(gate) * up
        _store_token_tiles(o_ref, _pack_halves(_dot(hb.astype(BF16), wd_b[...]).astype(BF16)))

    @pl.when(i >= nu_ref[0])
    def _():
        o_ref[...] = jnp.zeros_like(o_ref)


def _moe_grouped(xs, blk_e, first, slot, nxt, n_used, w_gate, w_up, w_down, layer):
    pr = xs.shape[0] // TOKEN_ROWS
    bm = MOE_BM
    nb = pr // bm
    grid_spec = pltpu.PrefetchScalarGridSpec(
        num_scalar_prefetch=5,
        grid=(nb,),
        in_specs=[
            pl.BlockSpec((bm * TOKEN_ROWS, LANES), lambda i, *_: (i, 0)),
            pl.BlockSpec(memory_space=pl.ANY),
            pl.BlockSpec(memory_space=pl.ANY),
            pl.BlockSpec(memory_space=pl.ANY),
        ],
        out_specs=pl.BlockSpec((bm * TOKEN_ROWS, LANES), lambda i, *_: (i, 0)),
        scratch_shapes=[pltpu.VMEM((2, D_MODEL, D_EXPERT), F32), pltpu.VMEM((2, D_MODEL, D_EXPERT), F32),
                        pltpu.VMEM((2, D_EXPERT, D_MODEL), F32),
                        pltpu.VMEM((D_MODEL, D_EXPERT), BF16), pltpu.VMEM((D_MODEL, D_EXPERT), BF16),
                        pltpu.VMEM((D_EXPERT, D_MODEL), BF16),
                        pltpu.SemaphoreType.DMA((2, 3))],
    )
    return pl.pallas_call(
        functools.partial(_moe_kernel, layer=layer),
        grid_spec=grid_spec,
        out_shape=jax.ShapeDtypeStruct((pr * TOKEN_ROWS, LANES), jnp.uint32),
        compiler_params=_cparams(("arbitrary",)),
        name="moe_grouped",
    )(blk_e, first, slot, nxt, n_used, xs, w_gate, w_up, w_down)


DISPATCH_TILE = 1024


def _dispatch_kernel(info_ref, nu_ref, pos_ref, h_ref, xs_out, zbuf, sem, zsem, *, nb):
    tm = pos_ref.shape[1]
    tr = TOKEN_ROWS
    bm = MOE_BM

    @pl.when(pl.program_id(0) == 0)
    def _():
        zbuf[...] = jnp.zeros_like(zbuf)

        def zero_block(first_row):
            rows = pl.ds(pl.multiple_of(first_row * tr, tr), bm * tr)
            return pltpu.make_async_copy(zbuf, xs_out.at[rows], zsem)

        for act in ("start", "wait"):
            def last_of_expert(e, carry, act=act):
                @pl.when(info_ref[e, 3] > info_ref[e, 1])
                def _():
                    getattr(zero_block(info_ref[e, 3] - bm), act)()
                return carry

            def tail_block(b, carry, act=act):
                getattr(zero_block(b * bm), act)()
                return carry

            lax.fori_loop(0, N_EXPERTS, last_of_expert, 0)
            lax.fori_loop(nu_ref[0], nb, tail_block, 0)

    def send(tok, carry):
        src = h_ref.at[pl.ds(pl.multiple_of(tok * tr, tr), tr)]
        for k in range(TOP_K):
            row = pl.multiple_of(pos_ref[k, tok], tr)
            pltpu.make_async_copy(src, xs_out.at[pl.ds(row, tr)], sem).start(priority=k % 2)
        return carry

    lax.fori_loop(0, tm, send, 0, unroll=4)
    n = tm * TOP_K * tr
    pltpu.make_async_copy(xs_out.at[pl.ds(0, n)], xs_out.at[pl.ds(0, n)], sem).wait()


def _dispatch(h2, pos, info, n_used, nb):
    t = pos.shape[1]
    tm = DISPATCH_TILE
    return pl.pallas_call(
        functools.partial(_dispatch_kernel, nb=nb),
        grid=(t // tm,),
        in_specs=[pl.BlockSpec(memory_space=pltpu.SMEM),
                  pl.BlockSpec(memory_space=pltpu.SMEM),
                  pl.BlockSpec((SUBLANES, tm), lambda i: (0, i), memory_space=pltpu.SMEM),
                  pl.BlockSpec((tm * TOKEN_ROWS, LANES), lambda i: (i, 0))],
        out_specs=pl.BlockSpec(memory_space=pl.ANY),
        out_shape=jax.ShapeDtypeStruct((nb * MOE_BM * TOKEN_ROWS, LANES), h2.dtype),
        scratch_shapes=[pltpu.VMEM((MOE_BM * TOKEN_ROWS, LANES), h2.dtype),
                        pltpu.SemaphoreType.DMA(()), pltpu.SemaphoreType.DMA(())],
        compiler_params=pltpu.CompilerParams(dimension_semantics=("arbitrary",)),
        name="dispatch",
    )(info, n_used, pos, h2)


ROUTE_TILE = 512


def _router_kernel(lg_ref, bias_ref, pos_ref, gate_ref, be_ref, info_ref, rank_scr, ek_scr, *, t, nbp):
    tl = ROUTE_TILE
    ne = N_EXPERTS
    bm = MOE_BM
    row = lax.broadcasted_iota(jnp.int32, (ne, tl), 0)
    tri = (lax.broadcasted_iota(jnp.int32, (tl, tl), 0) < lax.broadcasted_iota(jnp.int32, (tl, tl), 1)).astype(BF16)
    bias = bias_ref[...]

    def select(i, counts):
        cols = pl.ds(pl.multiple_of(i * tl, tl), tl)
        s = jax.nn.sigmoid(lg_ref[:, cols])
        sel = s + bias
        mask = jnp.zeros((ne, tl), F32)
        vals = []
        for k in range(TOP_K):
            best = jnp.max(sel, axis=0, keepdims=True)
            idx = jnp.min(jnp.where(sel == best, row, ne), axis=0, keepdims=True)
            hit = row == idx
            vals.append(jnp.sum(jnp.where(hit, s, 0.0), axis=0, keepdims=True))
            sel = jnp.where(hit, -jnp.inf, sel)
            mask = jnp.where(hit, 1.0, mask)
            ek_scr[k:k + 1, cols] = idx
        total = vals[0]
        for v in vals[1:]:
            total = total + v
        scale = ROUTED_SCALE / total
        for k in range(TOP_K):
            gate_ref[k:k + 1, cols] = vals[k] * scale
        gate_ref[TOP_K:SUBLANES, cols] = jnp.zeros((SUBLANES - TOP_K, tl), F32)
        rank_scr[:, cols] = _dot(mask.astype(BF16), tri) + counts
        return counts + jnp.sum(mask, axis=1, keepdims=True)

    counts = lax.fori_loop(0, t // tl, select, jnp.zeros((ne, 1), F32))
    counts = counts.astype(jnp.int32)
    shift = bm.bit_length() - 1
    assert bm == 1 << shift
    padded = ((counts + (bm - 1)) >> shift) << shift
    e0 = lax.broadcasted_iota(jnp.int32, (ne, ne), 0)
    e1 = lax.broadcasted_iota(jnp.int32, (ne, ne), 1)
    padded_row = jnp.sum(jnp.where(e0 == e1, padded, 0), axis=0, keepdims=True)
    counts_row = jnp.sum(jnp.where(e0 == e1, counts, 0), axis=0, keepdims=True)
    pstart = jnp.sum(jnp.where(e1 < e0, padded_row, 0), axis=1, keepdims=True)
    ustart = jnp.sum(jnp.where(e1 < e0, counts_row, 0), axis=1, keepdims=True)
    pend = pstart + padded
    lane = lax.broadcasted_iota(jnp.int32, (ne, LANES), 1)
    info_ref[...] = jnp.where(lane == 0, counts, jnp.where(lane == 1, pstart, jnp.where(lane == 2, ustart, pend)))
    blk = lax.broadcasted_iota(jnp.int32, (ne, nbp), 1) * bm
    owner = jnp.sum(jnp.where(pend <= blk, 1, 0), axis=0, keepdims=True)
    be_ref[...] = jnp.minimum(owner, ne - 1)
    pstart_f = pstart.astype(F32)

    def place(i, carry):
        cols = pl.ds(pl.multiple_of(i * tl, tl), tl)
        dest = rank_scr[:, cols] + pstart_f
        for k in range(TOP_K):
            hit = row == ek_scr[k:k + 1, cols]
            pos_ref[k:k + 1, cols] = (jnp.sum(jnp.where(hit, dest, 0.0), axis=0, keepdims=True)
                                      * float(TOKEN_ROWS)).astype(jnp.int32)
        pos_ref[TOP_K:SUBLANES, cols] = jnp.zeros((SUBLANES - TOP_K, tl), jnp.int32)
        return carry

    lax.fori_loop(0, t // tl, place, 0)


def _router(logits_t, router_bias, nb):
    t = logits_t.shape[1]
    nbp = -(-nb // LANES) * LANES
    return pl.pallas_call(
        functools.partial(_router_kernel, t=t, nbp=nbp),
        out_shape=[jax.ShapeDtypeStruct((SUBLANES, t), jnp.int32), jax.ShapeDtypeStruct((SUBLANES, t), F32),
                   jax.ShapeDtypeStruct((1, nbp), jnp.int32), jax.ShapeDtypeStruct((N_EXPERTS, LANES), jnp.int32)],
        scratch_shapes=[pltpu.VMEM((N_EXPERTS, t), F32), pltpu.VMEM((SUBLANES, t), jnp.int32)],
        compiler_params=pltpu.CompilerParams(vmem_limit_bytes=VMEM_LIMIT),
        name="router",
    )(logits_t, router_bias.astype(F32).reshape(N_EXPERTS, 1))


def _dispatch_plan(blk_e_row, info, nb):
    bm = MOE_BM
    pend = info[:, 3]
    blk_e = blk_e_row[0, :nb]
    n_used = pend[-1] // bm
    blk = jnp.arange(nb, dtype=jnp.int32)
    prev_e = jnp.concatenate([jnp.full((1,), -1, jnp.int32), blk_e[:-1]])
    first = jnp.logical_and(blk < n_used, blk_e != prev_e)
    slot = (jnp.cumsum(first.astype(jnp.int32)) - 1) % 2
    first_at = jnp.where(first, blk, nb)
    nxt_first = lax.cummin(jnp.concatenate([first_at[1:], jnp.full((1,), nb, jnp.int32)]), reverse=True)
    nxt = jnp.where(nxt_first < nb, blk_e[jnp.minimum(nxt_first, nb - 1)], -1)
    return (blk_e, first.astype(jnp.int32), slot.astype(jnp.int32), nxt.astype(jnp.int32),
            n_used.astype(jnp.int32).reshape(1))


def _shared_kernel(pos_ref, nxt_ref, x_ref, h_ref, gt_ref, mod_ref, sg_ref, su_ref, sd_ref, fn_ref, eo_hbm,
                   o_ref, buf0, buf1, acc_scr, sem, *, final):
    i = pl.program_id(0)
    n = pl.num_programs(0)
    tm = x_ref.shape[0]
    tr = TOKEN_ROWS

    half = D_MODEL // 2
    grp = SUBLANES

    def fetch_tokens(idx_ref, buf, slot, tok0):
        for tt in range(grp):
            tok = tok0 + tt
            for k in range(TOP_K):
                row = pl.multiple_of(idx_ref[k, tok], tr)
                pltpu.make_async_copy(eo_hbm.at[pl.ds(row, tr)], buf.at[k, pl.ds(pl.multiple_of(tok * tr, tr), tr)],
                                      sem.at[slot]).start(priority=k % 2)

    def arrived(buf, slot):
        for k in range(TOP_K):
            pltpu.make_async_copy(eo_hbm.at[pl.ds(0, tm * tr)], buf.at[k], sem.at[slot]).wait()

    def step(cur, cur_slot, nxt, nxt_slot):
        arrived(cur, cur_slot)

        def group(c, carry):
            tok0 = pl.multiple_of(c * grp, grp)
            fetch_tokens(nxt_ref, nxt, nxt_slot, tok0)
            gt = gt_ref[pl.ds(tok0, grp), :]
            r_lo = jnp.zeros((grp, half), F32)
            r_hi = r_lo
            for k in range(TOP_K):
                words = jnp.concatenate(
                    [cur[k, pl.ds(tok0 * tr + s, grp, stride=tr), :] for s in range(tr)], axis=1)
                e_lo, e_hi = _unpack_halves(words)
                r_lo = r_lo + gt[:, k:k + 1] * e_lo
                r_hi = r_hi + gt[:, k:k + 1] * e_hi
            acc_scr[pl.ds(tok0, grp), :] = jnp.concatenate([r_lo, r_hi], axis=1)
            return carry

        lax.fori_loop(0, tm // grp, group, 0)
        h_lo, h_hi = _unpack_halves(_load_token_tiles(h_ref, tm))
        h_lo = h_lo.astype(BF16)
        h_hi = h_hi.astype(BF16)
        gate = _dot(h_lo, sg_ref[0:half, :]) + _dot(h_hi, sg_ref[half:, :])
        up = _dot(h_lo, su_ref[0:half, :]) + _dot(h_hi, su_ref[half:, :])
        act = gate * jax.nn.sigmoid(gate) * up
        shared = _dot(act.astype(BF16), sd_ref[...])
        x = x_ref[...] + mod_ref[0][5:6] * (acc_scr[...] + shared)
        if final:
            x = _rms(x, fn_ref[...])
        o_ref[...] = x

        @pl.when(i == n - 1)
        def _():
            arrived(nxt, nxt_slot)

    @pl.when(i == 0)
    def _():
        def first(c, carry):
            fetch_tokens(pos_ref, buf0, 0, pl.multiple_of(c * grp, grp))
            return carry
        lax.fori_loop(0, tm // grp, first, 0)

    for parity, cur, nxt in ((0, buf0, buf1), (1, buf1, buf0)):
        @pl.when(i % 2 == parity)
        def _(parity=parity, cur=cur, nxt=nxt):
            step(cur, parity, nxt, 1 - parity)


def _shared(x, h, gates_t, pos, eo, mod, sg, su, sd, final_norm, n_ctx_rows, dec_seq, *, final, row0=0, rows=None):
    rows = x.shape[0] if rows is None else rows
    tm = ROW_TILE // 2
    b0 = row0 // tm
    steps = rows // tm
    grp = lambda i: _group_of(i + b0, tm, n_ctx_rows, dec_seq)
    row = pl.BlockSpec((tm, D_MODEL), lambda i: (i + b0, 0))
    prow = pl.BlockSpec((tm * TOKEN_ROWS, LANES), lambda i: (i + b0, 0))
    full = lambda a, b: pl.BlockSpec((a, b), lambda i: (0, 0))
    buf = pltpu.VMEM((TOP_K, tm * TOKEN_ROWS, LANES), eo.dtype)
    return pl.pallas_call(
        functools.partial(_shared_kernel, final=final),
        grid=(steps,),
        in_specs=[pl.BlockSpec((SUBLANES, tm), lambda i: (0, i + b0), memory_space=pltpu.SMEM),
                  pl.BlockSpec((SUBLANES, tm), lambda i: (0, jnp.minimum(i + 1, steps - 1) + b0),
                               memory_space=pltpu.SMEM),
                  row, prow, pl.BlockSpec((tm, SUBLANES), lambda i: (i + b0, 0)),
                  pl.BlockSpec((1, 6, D_MODEL), lambda i: (grp(i), 0, 0)),
                  full(D_MODEL, D_SHARED), full(D_MODEL, D_SHARED), full(D_SHARED, D_MODEL), full(1, D_MODEL),
                  pl.BlockSpec(memory_space=pl.ANY)],
        out_specs=pl.BlockSpec((tm, D_MODEL), lambda i: (i, 0)),
        out_shape=jax.ShapeDtypeStruct((rows, D_MODEL), F32),
        scratch_shapes=[buf, buf, pltpu.VMEM((tm, D_MODEL), F32), pltpu.SemaphoreType.DMA((2,))],
        compiler_params=_cparams(("arbitrary",)),
        name="shared_final" if final else "shared",
    )(pos, pos, x, h, gates_t, mod, sg.astype(BF16), su.astype(BF16), sd.astype(BF16),
      final_norm.reshape(1, D_MODEL), eo)


def kernel(x_prompt, x_sample, state_ret, state_s5_re, state_s5_im, c, c_ctx, w_ada, b_ada, norm_mix, norm_ffn, w_in, w_out, ret_decay, s5_lam_re, s5_lam_im, s5_log_dt, s5_b_re, s5_b_im, s5_c_re, s5_c_im, s5_d, s5_glu_w, s5_glu_b, hy_conv_w, hy_conv_b, hy_f1_w, hy_f1_b, hy_f2_w, hy_f2_b, hy_f3_w, hy_f3_b, hy_freq, hy_decay, hy_bias, moe_router, moe_router_bias, moe_w_gate, moe_w_up, moe_w_down, sh_w_gate, sh_w_up, sh_w_down, final_norm):
    n_ctx, seq, d = x_prompt.shape
    n_dec, dec_seq, _ = x_sample.shape
    n_ctx_rows = n_ctx * seq
    t = n_ctx_rows + n_dec * dec_seq

    x = jnp.concatenate([x_prompt.reshape(n_ctx_rows, d), x_sample.reshape(n_dec * dec_seq, d)], axis=0)
    cond = jnp.concatenate([c_ctx[None, :], c], axis=0)
    cond8 = jnp.pad(cond, ((0, SUBLANES - cond.shape[0]), (0, 0)))
    mods = _ada(cond8, w_ada, b_ada)[:, :1 + n_dec].reshape(DEPTH, 1 + n_dec, 6, d)

    cos2, sin2 = _rope_tables(dec_seq)
    no_rope = jnp.zeros((seq, LANES), F32)
    zero_ret = jnp.zeros((1, 2, RET_HEADS, RET_DK, RET_DV), F32)
    zero_s5 = jnp.zeros((n_ctx, 2, S5_GROUPS, S5_STATE), F32)
    dft_ctx = _dft_mats(seq)
    dft_dec = _dft_mats(dec_seq)

    w_in_bf = w_in.astype(BF16)
    mats = _s5_mats(s5_lam_re, s5_lam_im, s5_log_dt, s5_b_re, s5_b_im, s5_c_re, s5_c_im, s5_d)
    filt = (hy_f1_w, hy_f1_b, hy_f2_w, hy_f2_b, hy_f3_w, hy_f3_b, hy_freq, hy_decay)
    mults_ctx = _hy_spectrum(jax.vmap(functools.partial(_hy_filter_taps, seq))(*filt), dft_ctx)
    mults_dec = _hy_spectrum(jax.vmap(functools.partial(_hy_filter_taps, dec_seq))(*filt), dft_dec)

    ret_states = jnp.zeros((n_ctx, DEPTH, 2, RET_HEADS, RET_DK, RET_DV), F32)
    s5r_list, s5i_list = [], []
    for l in range(DEPTH):
        mod = mods[l]
        proj = _in_proj(x, mod, norm_mix[l], w_in_bf, l, n_ctx_rows, dec_seq)

        log_gamma = jax.nn.log_sigmoid(ret_decay[l].astype(F32))
        ret_o, ret_states = _retention(proj, log_gamma, zero_ret, no_rope, no_rope,
                                       row0=0, n_seq=n_ctx, seq_len=seq, hb=RET_HEADS, rope=False,
                                       dst=jnp.zeros((t, RET_WIDTH), BF16), states=ret_states, layer=l)
        ret_o, _ = _retention(proj, log_gamma, state_ret[:, l].astype(F32), cos2, sin2,
                              row0=n_ctx_rows, n_seq=n_dec, seq_len=dec_seq, hb=2, rope=True, dst=ret_o)

        s5_y, s5_re, s5_im = _s5(proj, mats, zero_s5, zero_s5, row0=0, n_seq=n_ctx, seq_len=seq, layer=l,
                                 dst=jnp.zeros((t, S5_WIDTH), F32))
        s5_y, _, _ = _s5(proj, mats, state_s5_re[:, l], state_s5_im[:, l],
                         row0=n_ctx_rows, n_seq=n_dec, seq_len=dec_seq, layer=l, dst=s5_y)
        s5r_list.append(s5_re)
        s5i_list.append(s5_im)

        hy_o = _hyena(proj, hy_conv_w[l], hy_conv_b[l], hy_bias[l], dft_ctx, mults_ctx, layer=l,
                      row0=0, n_seq=n_ctx, seq_len=seq, cb=HY_WIDTH, tk=seq, dst=jnp.zeros((t, HY_WIDTH), BF16))
        hy_o = _hyena(proj, hy_conv_w[l], hy_conv_b[l], hy_bias[l], dft_dec, mults_dec, layer=l,
                      row0=n_ctx_rows, n_seq=n_dec, seq_len=dec_seq, cb=HY_WIDTH // 2, tk=512, dst=hy_o)

        x, h2, logits = _out_proj(x, ret_o, s5_y, hy_o, mod, norm_ffn[l], s5_glu_w[l], s5_glu_b[l],
                                  w_out[l], moe_router[l], n_ctx_rows, dec_seq)

        nb = -(-(t * TOP_K) // MOE_BM) + N_EXPERTS
        pos, gates, blk_e_row, info = _router(logits, moe_router_bias[l], nb)
        blk_e, first, slot, nxt, n_used = _dispatch_plan(blk_e_row, info, nb)
        xs = _dispatch(h2, pos, info, n_used, nb)
        eo = _moe_grouped(xs, blk_e, first, slot, nxt, n_used, moe_w_gate, moe_w_up, moe_w_down, l)
        gates_t = gates.T

        sh = (sh_w_gate[l], sh_w_up[l], sh_w_down[l])
        if l < DEPTH - 1:
            x = _shared(x, h2, gates_t, pos, eo, mod, *sh, final_norm, n_ctx_rows, dec_seq, final=False)
        else:
            y_c = _shared(x, h2, gates_t, pos, eo, mod, *sh, final_norm, n_ctx_rows, dec_seq, final=True,
                          row0=0, rows=n_ctx_rows)
            y_d = _shared(x, h2, gates_t, pos, eo, mod, *sh, final_norm, n_ctx_rows, dec_seq, final=True,
                          row0=n_ctx_rows, rows=n_dec * dec_seq)

    return (y_c.reshape(n_ctx, seq, d), y_d.reshape(n_dec, dec_seq, d),
            ret_states, jnp.stack(s5r_list, axis=1), jnp.stack(s5i_list, axis=1))
```

```python
import functools
import math

import jax
import jax.numpy as jnp
from jax import lax
from jax.experimental import pallas as pl
from jax.experimental.pallas import tpu as pltpu

F32 = jnp.float32
BF16 = jnp.bfloat16

D_MODEL = 2048
DEPTH = 2
GRID_W = 64
RET_HEADS = 8
RET_DK = 128
RET_DV = 128
RET_WIDTH = RET_HEADS * RET_DV
RET_CHUNK = 256
ROPE_BASE = 10000.0
S5_WIDTH = 512
S5_GROUP = 16
S5_GROUPS = S5_WIDTH // S5_GROUP
S5_STATE = 64
S5_Q = 16
HY_WIDTH = 512
HY_BANDS = 16
IN_WIDTH = 4 * RET_WIDTH + S5_WIDTH + 3 * HY_WIDTH
U_COL = 4 * RET_WIDTH
HY_COL = U_COL + S5_WIDTH
N_EXPERTS = 64
TOP_K = 6
D_EXPERT = 512
D_SHARED = 512
ROUTED_SCALE = 2.5
EPS = 1e-6

LANES = 128
SUBLANES = 8
VMEM_LIMIT = 56 * 1024 * 1024

ROW_TILE = 512
MOE_BM = 256


def _cparams(sem):
    return pltpu.CompilerParams(dimension_semantics=sem, vmem_limit_bytes=VMEM_LIMIT)


def _dot(a, b):
    return jnp.dot(a, b, preferred_element_type=F32)


def _rms(x, g):
    var = jnp.mean(x * x, axis=-1, keepdims=True)
    return x * lax.rsqrt(var + EPS) * g


def _pack_halves(xb):
    n = xb.shape[1] // 2
    lo = lax.bitcast_convert_type(xb[:, :n].astype(F32), jnp.uint32) >> 16
    hi = lax.bitcast_convert_type(xb[:, n:].astype(F32), jnp.uint32)
    return lo | hi


def _unpack_halves(w):
    lo = lax.bitcast_convert_type(w << 16, F32)
    hi = lax.bitcast_convert_type(w & jnp.uint32(0xFFFF0000), F32)
    return lo, hi


TOKEN_ROWS = D_MODEL // 2 // LANES


def _store_token_tiles(ref, w):
    m = w.shape[0]
    for s in range(TOKEN_ROWS):
        ref[pl.ds(s, m, stride=TOKEN_ROWS), :] = w[:, s * LANES:(s + 1) * LANES]


def _load_token_tiles(ref, m):
    return jnp.concatenate([ref[pl.ds(s, m, stride=TOKEN_ROWS), :] for s in range(TOKEN_ROWS)], axis=1)


def _ada_kernel(c_ref, w_ref, b_ref, o_ref):
    c = c_ref[...]
    s = (c * jax.nn.sigmoid(c)).astype(BF16)
    o_ref[0] = _dot(s, w_ref[0].astype(BF16)) + b_ref[0]


def _ada(cond8, w_ada, b_ada):
    tn = 1024
    n = w_ada.shape[-1]
    return pl.pallas_call(
        _ada_kernel,
        grid=(DEPTH, n // tn),
        in_specs=[
            pl.BlockSpec((SUBLANES, D_MODEL), lambda l, j: (0, 0)),
            pl.BlockSpec((1, D_MODEL, tn), lambda l, j: (l, 0, j)),
            pl.BlockSpec((1, 1, tn), lambda l, j: (l, 0, j)),
        ],
        out_specs=pl.BlockSpec((1, SUBLANES, tn), lambda l, j: (l, 0, j)),
        out_shape=jax.ShapeDtypeStruct((DEPTH, SUBLANES, n), F32),
        compiler_params=_cparams(("parallel", "parallel")),
        name="ada",
    )(cond8, w_ada, b_ada.reshape(DEPTH, 1, n))


def _group_of(i, tile, n_ctx_rows, dec_seq):
    ctx_tiles = n_ctx_rows // tile
    per = dec_seq // tile
    return jnp.where(i < ctx_tiles, 0, 1 + (i - ctx_tiles) // per)


def _in_kernel(x_ref, mod_ref, g_ref, w_ref, o_ref, h_scr):
    @pl.when(pl.program_id(1) == 0)
    def _():
        m = mod_ref[0]
        g = g_ref[...]
        rc = 128

        def rows_pass(c, carry):
            rows = pl.ds(pl.multiple_of(c * rc, rc), rc)
            h = _rms(x_ref[rows, :], g) * (1.0 + m[1:2]) + m[0:1]
            h_scr[rows, :] = h.astype(BF16)
            return carry

        lax.fori_loop(0, x_ref.shape[0] // rc, rows_pass, 0)

    o_ref[...] = _dot(h_scr[...], w_ref[...])


def _in_proj(x, mod, g, w_bf, layer, n_ctx_rows, dec_seq):
    t = x.shape[0]
    tm, tn = 1024, 1024
    grp = functools.partial(_group_of, tile=tm, n_ctx_rows=n_ctx_rows, dec_seq=dec_seq)
    return pl.pallas_call(
        _in_kernel,
        grid=(t // tm, IN_WIDTH // tn),
        in_specs=[
            pl.BlockSpec((tm, D_MODEL), lambda i, j: (i, 0)),
            pl.BlockSpec((1, 6, D_MODEL), lambda i, j: (grp(i), 0, 0)),
            pl.BlockSpec((1, D_MODEL), lambda i, j: (0, 0)),
            pl.BlockSpec((None, D_MODEL, tn), lambda i, j: (layer, 0, j)),
        ],
        out_specs=pl.BlockSpec((tm, tn), lambda i, j: (i, j)),
        out_shape=jax.ShapeDtypeStruct((t, IN_WIDTH), F32),
        scratch_shapes=[pltpu.VMEM((tm, D_MODEL), BF16)],
        compiler_params=_cparams(("parallel", "arbitrary")),
        name="in_proj",
    )(x, mod, g.reshape(1, D_MODEL), w_bf)


def _ret_kernel(lg_ref, q_ref, k_ref, v_ref, gt_ref, cos_ref, sin_ref, s0_ref,
                o_ref, sfin_ref, acc_scr, q_scr, k_scr, *, seq_len, hb, rope):
    c = RET_CHUNK
    n_chunks = seq_len // c
    ii = lax.broadcasted_iota(jnp.int32, (c, c), 0)
    jj = lax.broadcasted_iota(jnp.int32, (c, c), 1)
    rel = (ii - jj).astype(F32)
    ci = lax.broadcasted_iota(jnp.int32, (c, 1), 0).astype(F32)
    one = jnp.ones((1, 1), F32)
    tdot = functools.partial(lax.dot_general, preferred_element_type=F32)

    def make_head(hh):
        head = pl.program_id(1) * hb + hh
        lgf = lg_ref[0, head]
        lgb = lg_ref[1, head]
        dmask = (jnp.where(rel >= 0, jnp.exp(lgf * jnp.maximum(rel, 0.0)), 0.0)
                 + jnp.where(rel <= 0, jnp.exp(lgb * jnp.maximum(-rel, 0.0)), 0.0))
        qd_f = jnp.exp(lgf * (ci + 1.0))
        kd_f = jnp.exp(lgf * (c - 1.0 - ci))
        cd_f = jnp.exp(lgf * c * one)
        qd_b = jnp.exp(lgb * (c - ci))
        kd_b = jnp.exp(lgb * ci)
        cd_b = jnp.exp(lgb * c * one)
        lanes = slice(hh * LANES, (hh + 1) * LANES)

        def rows_of(n):
            if isinstance(n, int):
                return slice(n * c, (n + 1) * c)
            return pl.ds(pl.multiple_of(n * c, c), c)

        def fwd_chunk(n, s_f):
            rows = rows_of(n)
            q = q_ref[rows, lanes]
            k = k_ref[rows, lanes] * (RET_DK ** -0.5)
            if rope:
                cs = cos_ref[rows, :]
                sn = sin_ref[rows, :]
                q = q * cs + pltpu.roll(q, RET_DK // 2, 1) * sn
                k = k * cs + pltpu.roll(k, RET_DK // 2, 1) * sn
            qb = q.astype(BF16)
            vb = v_ref[rows, lanes].astype(BF16)
            q_scr[rows, lanes] = qb
            k_scr[rows, lanes] = k
            scores = tdot(qb, k.astype(BF16), (((1,), (1,)), ((), ()))) * dmask
            inner = _dot(scores.astype(BF16), vb)
            cross = _dot(qb, s_f.astype(BF16)) * qd_f
            acc_scr[rows, lanes] = inner + cross
            upd = tdot((k * kd_f).astype(BF16), vb, (((0,), (0,)), ((), ())))
            return s_f * cd_f + upd

        def bwd_chunk(m, s_b):
            n = n_chunks - 1 - m
            rows = rows_of(n)
            qb = q_scr[rows, lanes]
            k = k_scr[rows, lanes]
            vb = v_ref[rows, lanes].astype(BF16)
            o = acc_scr[rows, lanes] + _dot(qb, s_b.astype(BF16)) * qd_b
            mu = jnp.mean(o, axis=-1, keepdims=True)
            oc = o - mu
            var = jnp.mean(oc * oc, axis=-1, keepdims=True)
            o = oc * lax.rsqrt(var + EPS)
            g = gt_ref[rows, lanes]
            o_ref[rows, lanes] = (g * jax.nn.sigmoid(g) * o).astype(o_ref.dtype)
            upd = tdot((k * kd_b).astype(BF16), vb, (((0,), (0,)), ((), ())))
            return s_b * cd_b + upd

        return fwd_chunk, bwd_chunk

    if n_chunks <= 4:
        for hh in range(hb):
            fwd_chunk, bwd_chunk = make_head(hh)
            s_f = s0_ref[0, 0, hh]
            s_b = s0_ref[0, 1, hh]
            for n in range(n_chunks):
                s_f = fwd_chunk(n, s_f)
            for m in range(n_chunks):
                s_b = bwd_chunk(m, s_b)
            sfin_ref[0, 0, hh] = s_f
            sfin_ref[0, 1, hh] = s_b
    else:
        fns = [make_head(hh) for hh in range(hb)]
        s_f = lax.fori_loop(0, n_chunks, lambda n, ss: tuple(f[0](n, s) for f, s in zip(fns, ss)),
                            tuple(s0_ref[0, 0, hh] for hh in range(hb)))
        s_b = lax.fori_loop(0, n_chunks, lambda m, ss: tuple(f[1](m, s) for f, s in zip(fns, ss)),
                            tuple(s0_ref[0, 1, hh] for hh in range(hb)))
        for hh in range(hb):
            sfin_ref[0, 0, hh] = s_f[hh]
            sfin_ref[0, 1, hh] = s_b[hh]


def _into(kernel_fn, n_in, dst):
    dsts = [d for d in (dst if isinstance(dst, (list, tuple)) else [dst])]
    outs = [k for k, d in enumerate(dsts) if d is not None]
    if not outs:
        return kernel_fn, [], [], {}

    def body(*refs):
        return kernel_fn(*refs[:n_in], *refs[n_in + len(outs):])

    return (body, [pl.BlockSpec(memory_space=pl.ANY)] * len(outs), [dsts[k] for k in outs],
            {n_in + pos: k for pos, k in enumerate(outs)})


def _retention(proj, log_gamma, s0, cos2, sin2, *, row0, n_seq, seq_len, hb, rope, dst=None,
               states=None, layer=0):
    blk0 = row0 // seq_len
    body, dst_spec, dst_arg, alias = _into(
        functools.partial(_ret_kernel, seq_len=seq_len, hb=hb, rope=rope), 8, [dst, states])
    if states is None:
        st_spec = pl.BlockSpec((1, 2, hb, RET_DK, RET_DV), lambda b, h, lg: (b, 0, h, 0, 0))
        st_shape = jax.ShapeDtypeStruct((n_seq, 2, RET_HEADS, RET_DK, RET_DV), F32)
    else:
        st_spec = pl.BlockSpec((1, None, 2, hb, RET_DK, RET_DV), lambda b, h, lg: (b, layer, 0, h, 0, 0))
        st_shape = jax.ShapeDtypeStruct(states.shape, F32)
    w = hb * LANES
    hblocks = RET_HEADS // hb
    col = lambda part: (lambda b, h, lg: (blk0 + b, part * hblocks + h))
    grid_spec = pltpu.PrefetchScalarGridSpec(
        num_scalar_prefetch=1,
        grid=(n_seq, hblocks),
        in_specs=[
            pl.BlockSpec((seq_len, w), col(0)),
            pl.BlockSpec((seq_len, w), col(1)),
            pl.BlockSpec((seq_len, w), col(2)),
            pl.BlockSpec((seq_len, w), col(3)),
            pl.BlockSpec((seq_len, LANES), lambda b, h, lg: (0, 0)),
            pl.BlockSpec((seq_len, LANES), lambda b, h, lg: (0, 0)),
            pl.BlockSpec((1, 2, hb, RET_DK, RET_DV), lambda b, h, lg: (b if s0.shape[0] > 1 else 0, 0, h, 0, 0)),
        ] + dst_spec,
        out_specs=[
            pl.BlockSpec((seq_len, w), lambda b, h, lg: (blk0 + b, h)),
            st_spec,
        ],
        scratch_shapes=[
            pltpu.VMEM((seq_len, w), F32),
            pltpu.VMEM((seq_len, w), BF16),
            pltpu.VMEM((seq_len, w), F32),
        ],
    )
    return pl.pallas_call(
        body,
        grid_spec=grid_spec,
        out_shape=[jax.ShapeDtypeStruct((proj.shape[0], RET_WIDTH), BF16), st_shape],
        input_output_aliases=alias,
        compiler_params=_cparams(("parallel", "arbitrary")),
        name="retention",
    )(log_gamma, proj, proj, proj, proj, cos2, sin2, s0, *dst_arg)


def _rope_tables(seq_len):
    rows_n = seq_len // GRID_W
    rows = jnp.repeat(jnp.arange(rows_n, dtype=F32), GRID_W)
    cols = jnp.tile(jnp.arange(GRID_W, dtype=F32), rows_n)
    nf = RET_DK // 4
    inv = ROPE_BASE ** (-jnp.arange(nf, dtype=F32) / nf)
    ang = jnp.concatenate([rows[:, None] * inv, cols[:, None] * inv], axis=-1)
    cs, sn = jnp.cos(ang), jnp.sin(ang)
    return jnp.concatenate([cs, cs], axis=-1), jnp.concatenate([-sn, sn], axis=-1)


S5_GB = LANES // S5_GROUP
S5_W = S5_Q * LANES
S5_SPLIT = 4
S5_SW = S5_GB * S5_STATE
S5_SB = S5_SW // LANES


def _s5_kernel(u_ref, bq_ref, k_ref, cq_ref, ar_ref, ai_ref, d_ref, h0_ref,
               y_ref, hf_ref, ub_scr, sm_scr, hp_scr, hpb_scr, t_scr, *, n_seq, n_chunks):
    s = pl.program_id(1)
    m = n_seq * n_chunks
    q = S5_Q
    nblk = S5_W // LANES
    sb = S5_SB

    per = q // S5_SPLIT
    kw = S5_W // S5_SPLIT

    @pl.when(s == 0)
    def _():
        for j in range(q):
            ub_scr[j // per, :, (j % per) * LANES:(j % per + 1) * LANES] = (
                u_ref[pl.ds(j, m, stride=q), :].astype(BF16))
        for ib in range(S5_SPLIT):
            for j in range(q):
                c0 = (q - 1 - j) * LANES + ib * kw
                t_scr[ib, j * LANES:(j + 1) * LANES, :] = k_ref[0, :, c0:c0 + kw]

    @pl.when(s < S5_SPLIT)
    def _():
        part = _dot(ub_scr[jnp.minimum(s, S5_SPLIT - 1)], bq_ref[0])

        @pl.when(s == 0)
        def _():
            for cb in range(nblk):
                sm_scr[cb] = part[:, cb * LANES:(cb + 1) * LANES]

        @pl.when(s > 0)
        def _():
            for cb in range(nblk):
                sm_scr[cb] += part[:, cb * LANES:(cb + 1) * LANES]

    @pl.when(s == S5_SPLIT - 1)
    def _():
        ar = ar_ref[0]
        ai = ai_ref[0]
        h0 = h0_ref[0]
        blk = lambda a, cb: a[:, cb * LANES:(cb + 1) * LANES]

        def body(n, carry):
            rows_f = pl.ds(n, n_seq, stride=n_chunks)
            rows_b = pl.ds(n_chunks - 1 - n, n_seq, stride=n_chunks)
            new = list(carry)
            for d, rows in ((0, rows_f), (1, rows_b)):
                for c in range(sb):
                    re_i = d * sb + c
                    im_i = (2 + d) * sb + c
                    hr, hi = carry[re_i], carry[im_i]
                    hp_scr[re_i, rows, :] = hr
                    hp_scr[im_i, rows, :] = hi
                    a_r, a_i = blk(ar, re_i), blk(ai, re_i)
                    new[re_i] = a_r * hr - a_i * hi + sm_scr[re_i, rows, :]
                    new[im_i] = a_r * hi + a_i * hr + sm_scr[im_i, rows, :]
            return tuple(new)

        fin = lax.fori_loop(0, n_chunks, body, tuple(blk(h0, cb) for cb in range(nblk)))
        hf_ref[0] = jnp.concatenate(fin, axis=1)
        for cb in range(nblk):
            hpb_scr[:, cb * LANES:(cb + 1) * LANES] = hp_scr[cb].astype(BF16)

    @pl.when(s >= S5_SPLIT)
    def _():
        ub = jnp.concatenate([ub_scr[k] for k in range(S5_SPLIT)], axis=1)
        y = _dot(ub, t_scr[jnp.maximum(s - S5_SPLIT, 0)]) + _dot(hpb_scr[...], cq_ref[0])
        dd = d_ref[0]
        for ii in range(per):
            rows = pl.ds((s - S5_SPLIT) * per + ii, m, stride=q)
            yi = y[:, ii * LANES:(ii + 1) * LANES] + dd * u_ref[rows, :]
            y_ref[rows, :] = jax.nn.gelu(yi)


def _s5_expand_kernel(mc_ref, o_ref, *, xsize, ysize):
    xs, ys, gs = xsize.bit_length() - 1, ysize.bit_length() - 1, S5_GB.bit_length() - 1
    assert xsize == 1 << xs and ysize == 1 << ys and S5_GB == 1 << gs
    cw = o_ref.shape[2]
    nc = mc_ref.shape[2]
    col0 = pl.program_id(1) * cw
    r = lax.broadcasted_iota(jnp.int32, (nc, cw), 0)
    col = lax.broadcasted_iota(jnp.int32, (nc, cw), 1) + col0
    spread = jnp.logical_and(r >> ys == col >> (ys + gs), (r & (ysize - 1)) == (col & (ysize - 1)))
    big = _dot(mc_ref[0], jnp.where(spread, 1.0, 0.0).astype(BF16))
    row = lax.broadcasted_iota(jnp.int32, big.shape, 0)
    colb = lax.broadcasted_iota(jnp.int32, big.shape, 1) + col0
    same = ((row >> xs) & (S5_GB - 1)) == ((colb >> ys) & (S5_GB - 1))
    o_ref[0] = jnp.where(same, big, 0.0).astype(BF16)


def _s5_expand(mc, *, xsize, ysize):
    nb, rows, nc = mc.shape
    cols = nc * S5_GB
    cw = 512
    return pl.pallas_call(
        functools.partial(_s5_expand_kernel, xsize=xsize, ysize=ysize),
        grid=(nb, cols // cw),
        in_specs=[pl.BlockSpec((1, rows, nc), lambda b, j: (b, 0, 0))],
        out_specs=pl.BlockSpec((1, rows, cw), lambda b, j: (b, 0, j)),
        out_shape=jax.ShapeDtypeStruct((nb, rows, cols), BF16),
        compiler_params=_cparams(("parallel", "parallel")),
        name="s5_expand",
    )(mc)


def _s5_mats(lam_re, lam_im, log_dt, b_re, b_im, c_re, c_im, d):
    flat = lambda a: a.reshape((-1,) + a.shape[2:])
    kc, bq, cq, ar, ai, dd = map(flat, jax.vmap(_s5_compact)(lam_re, lam_im, log_dt, b_re, b_im, c_re, c_im, d))
    ch, p = S5_GROUP, S5_STATE
    return (_s5_expand(kc, xsize=ch, ysize=ch),
            _s5_expand(bq, xsize=ch, ysize=p),
            _s5_expand(cq, xsize=p, ysize=ch),
            ar, ai, dd)


def _s5_compact(lam_re, lam_im, log_dt, b_re, b_im, c_re, c_im, d):
    q, g, p, ch = S5_Q, S5_GROUPS, S5_STATE, S5_GROUP
    lam = lax.complex(jnp.minimum(lam_re.astype(F32), -1e-4), lam_im.astype(F32))
    ldt = lam * jnp.exp(log_dt.astype(F32))[..., None]
    lam_bar = jnp.exp(ldt)
    b_bar = ((lam_bar - 1.0) / lam)[..., None] * lax.complex(b_re.astype(F32), b_im.astype(F32))
    cc = lax.complex(c_re.astype(F32), c_im.astype(F32))
    pw = jnp.exp(ldt[..., None] * jnp.arange(q + 1, dtype=F32))
    hi = lax.Precision.HIGHEST
    lag = jnp.arange(2 * q, dtype=F32) - (q - 1)
    wf = jnp.where(lag >= 0, jnp.exp(ldt[0][..., None] * jnp.maximum(lag, 0.0)), 0.0)
    wb = jnp.where(lag <= 0, jnp.exp(ldt[1][..., None] * jnp.maximum(-lag, 0.0)), 0.0)
    kc = jnp.real(jnp.einsum('gcp,gpd,gpe->gedc', cc[0], wf, b_bar[0], precision=hi)
                  + jnp.einsum('gcp,gpd,gpe->gedc', cc[1], wb, b_bar[1], precision=hi))

    pw_dn = jnp.exp(ldt[..., None] * (q - jnp.arange(q + 1, dtype=F32)))
    bf = pw_dn[0][..., 1:][:, :, :, None] * b_bar[0][:, :, None, :]
    bb = pw[1][..., :q][:, :, :, None] * b_bar[1][:, :, None, :]
    to_rows = lambda m: m.transpose(0, 2, 3, 1).reshape(g, q * ch, p)
    bq = jnp.concatenate([to_rows(jnp.real(bf)), to_rows(jnp.real(bb)),
                          to_rows(jnp.imag(bf)), to_rows(jnp.imag(bb))], axis=-1)

    cf = cc[0].transpose(0, 2, 1)[:, :, None, :] * pw[0][..., 1:][:, :, :, None]
    cb = cc[1].transpose(0, 2, 1)[:, :, None, :] * pw_dn[1][..., :q][:, :, :, None]
    to_cols = lambda m: m.reshape(g, p, q * ch)
    cq = jnp.concatenate([to_cols(jnp.real(cf)), to_cols(jnp.real(cb)),
                          to_cols(-jnp.imag(cf)), to_cols(-jnp.imag(cb))], axis=1)

    gb, nb = S5_GB, g // S5_GB
    rows_of = lambda a, outer, inner: (a.reshape(nb, gb, outer, inner, a.shape[-1]).transpose(0, 2, 1, 3, 4)
                                       .reshape(nb, outer * gb * inner, a.shape[-1]).astype(BF16))
    kc = kc.reshape(nb, gb * ch, 2 * q * ch).astype(BF16)

    lq = pw[..., q].reshape(2, nb, 1, S5_SW)
    ar = jnp.concatenate([jnp.real(lq[0]), jnp.real(lq[1])], axis=-1)
    ai = jnp.concatenate([jnp.imag(lq[0]), jnp.imag(lq[1])], axis=-1)
    dd = d.astype(F32).reshape(nb, 1, LANES)
    return kc, rows_of(bq, q, ch), rows_of(cq, 4, p), ar, ai, dd


def _s5(proj, mats, h0_re, h0_im, *, row0, n_seq, seq_len, layer=0, dst=None):
    q, p = S5_Q, S5_STATE
    n_chunks = seq_len // q
    m = n_seq * n_chunks
    rows = n_seq * seq_len
    nb = S5_GROUPS // S5_GB
    nblk = S5_W // LANES
    kw = S5_W // S5_SPLIT
    tm, bq, cq, ar, ai, dd = mats
    part = lambda a: a.astype(F32).reshape(n_seq, nb, S5_SW)
    h0 = jnp.concatenate([part(h0_re[:, 0]), part(h0_re[:, 1]), part(h0_im[:, 0]), part(h0_im[:, 1])],
                         axis=-1).transpose(1, 0, 2)
    per_b = lambda shape: pl.BlockSpec((1,) + shape, lambda b, s: (b, 0, 0))
    b0 = layer * nb
    per_l = lambda shape: pl.BlockSpec((1,) + shape, lambda b, s: (b0 + b, 0, 0))
    body, dst_spec, dst_arg, alias = _into(functools.partial(_s5_kernel, n_seq=n_seq, n_chunks=n_chunks), 8, dst)
    y, hf = pl.pallas_call(
        body,
        grid=(nb, 2 * S5_SPLIT),
        in_specs=[
            pl.BlockSpec((rows, LANES), lambda b, s: (row0 // rows, U_COL // LANES + b)),
            pl.BlockSpec((1, kw, 4 * S5_SW), lambda b, s: (b0 + b, jnp.minimum(s, S5_SPLIT - 1), 0)),
            per_l((LANES, 2 * q * LANES)),
            pl.BlockSpec((1, 4 * S5_SW, kw), lambda b, s: (b0 + b, 0, jnp.maximum(s - S5_SPLIT, 0))),
            per_l((1, 2 * S5_SW)), per_l((1, 2 * S5_SW)), per_l((1, LANES)), per_b((n_seq, 4 * S5_SW)),
        ] + dst_spec,
        out_specs=[pl.BlockSpec((rows, LANES), lambda b, s: (row0 // rows, b)), per_b((n_seq, 4 * S5_SW))],
        out_shape=[jax.ShapeDtypeStruct((proj.shape[0], S5_WIDTH), F32),
                   jax.ShapeDtypeStruct((nb, n_seq, 4 * S5_SW), F32)],
        scratch_shapes=[pltpu.VMEM((S5_SPLIT, m, kw), BF16), pltpu.VMEM((nblk, m, LANES), F32),
                        pltpu.VMEM((nblk, m, LANES), F32), pltpu.VMEM((m, 4 * S5_SW), BF16),
                        pltpu.VMEM((S5_SPLIT, S5_W, kw), BF16)],
        input_output_aliases=alias,
        compiler_params=_cparams(("parallel", "arbitrary")),
        name="s5",
    )(proj, bq, tm, cq, ar, ai, dd, h0, *dst_arg)
    hf = hf.reshape(nb, n_seq, 4, S5_GB, p).transpose(1, 2, 0, 3, 4).reshape(n_seq, 4, S5_GROUPS, p)
    return y, hf[:, 0:2], hf[:, 2:4]


def _conv3(x, w, b):
    n = x.shape[0]
    row = lax.broadcasted_iota(jnp.int32, x.shape, 0)
    prev = jnp.where(row == 0, 0.0, pltpu.roll(x, 1, 0))
    nxt = jnp.where(row == n - 1, 0.0, pltpu.roll(x, n - 1, 0))
    return prev * w[0:1] + x * w[1:2] + nxt * w[2:3] + b


def _hy_fwd_kernel(x0_ref, x1_ref, v_ref, w0_ref, w1_ref, wv_ref, b0_ref, b1_ref, bv_ref,
                   fc_ref, fs_ref, m1_ref, m2_ref, m3_ref, p_ref, z_ref, x0c_ref, zb_scr):
    @pl.when(pl.program_id(2) == 0)
    def _():
        z = _conv3(x1_ref[...], w1_ref[...], b1_ref[...]) * _conv3(v_ref[...], wv_ref[...], bv_ref[...])
        z_ref[...] = z
        zb_scr[...] = z.astype(BF16)
        x0c_ref[...] = _conv3(x0_ref[...], w0_ref[...], b0_ref[...])

    zb = zb_scr[...]
    a = _dot(fc_ref[...], zb)
    b = _dot(fs_ref[...], zb)
    m2 = m2_ref[...]
    p_ref[0, 0] = (m1_ref[...] * a + m2 * b).astype(BF16)
    p_ref[0, 1] = (m3_ref[...] * b - m2 * a).astype(BF16)


def _hy_inv_kernel(p_ref, gc_ref, gs_ref, z_ref, x0c_ref, bias_ref, o_ref):
    conv = _dot(gc_ref[...], p_ref[0, 0]) + _dot(gs_ref[...], p_ref[0, 1])
    o_ref[...] = (x0c_ref[...] * (conv + bias_ref[...] * z_ref[...])).astype(o_ref.dtype)


def _hy_short_kernel(x0_ref, x1_ref, v_ref, w0_ref, w1_ref, wv_ref, b0_ref, b1_ref, bv_ref,
                     fc_ref, fs_ref, gs_ref, m1_ref, m2_ref, m3_ref, bias_ref, o_ref):
    z = _conv3(x1_ref[...], w1_ref[...], b1_ref[...]) * _conv3(v_ref[...], wv_ref[...], bv_ref[...])
    zb = z.astype(BF16)
    a = _dot(fc_ref[...], zb)
    b = _dot(fs_ref[...], zb)
    m2 = m2_ref[...]
    p_re = (m1_ref[...] * a + m2 * b).astype(BF16)
    p_im = (m3_ref[...] * b - m2 * a).astype(BF16)
    conv = _dot(fc_ref[...], p_re) + _dot(gs_ref[...], p_im)
    x0c = _conv3(x0_ref[...], w0_ref[...], b0_ref[...])
    o_ref[...] = (x0c * (conv + bias_ref[...] * z)).astype(o_ref.dtype)


def _dft_mats(seq_len):
    n, w = seq_len, 64
    k = jnp.arange(n, dtype=jnp.int32)
    ang = lambda j: ((k[:, None] * j[None, :]) % (2 * n)).astype(F32) * (math.pi / n)
    ang_a = ang(jnp.arange(n // w, dtype=jnp.int32) * w)
    ang_b = ang(jnp.arange(w, dtype=jnp.int32))
    ca, sa = jnp.cos(ang_a)[:, :, None], jnp.sin(ang_a)[:, :, None]
    cb, sb = jnp.cos(ang_b)[:, None, :], jnp.sin(ang_b)[:, None, :]
    cm = (ca * cb - sa * sb).reshape(n, n)
    sm = -(sa * cb + ca * sb).reshape(n, n)
    nyq = jnp.where(k % 2 == 0, 1.0, -1.0).astype(F32)
    return cm.astype(BF16), sm.at[0, :].set(nyq).astype(BF16), sm.at[:, 0].set(nyq).astype(BF16)


def _hy_filter_taps(seq_len, f1_w, f1_b, f2_w, f2_b, f3_w, f3_b, freq, decay):
    n = seq_len
    t = (jnp.arange(n, dtype=F32) / n)[:, None]
    bands = jnp.arange(1, HY_BANDS + 1, dtype=F32)[None, :]
    z = jnp.concatenate([t, jnp.cos(2.0 * math.pi * t * bands), jnp.sin(2.0 * math.pi * t * bands)], axis=-1)
    hi = lax.Precision.HIGHEST
    fr = freq.astype(F32)
    h = jnp.sin(fr * (jnp.dot(z, f1_w.astype(F32), precision=hi) + f1_b.astype(F32)))
    h = jnp.sin(fr * (jnp.dot(h, f2_w.astype(F32), precision=hi) + f2_b.astype(F32)))
    h = jnp.dot(h, f3_w.astype(F32), precision=hi) + f3_b.astype(F32)
    h = h * jnp.exp(-t * jnp.abs(decay.astype(F32)))
    h = h.reshape(n, 2, HY_WIDTH)
    h = h / jnp.sum(jnp.abs(h), axis=(0, 1), keepdims=True)
    return h.reshape(n, 2 * HY_WIDTH)


def _hy_spectrum_kernel(h_ref, cm_ref, sm_ref, ck_ref, sk_ref, m1_ref, m2_ref, m3_ref, *, seq_len):
    w = HY_WIDTH
    tk = cm_ref.shape[0]
    hb = h_ref[0].astype(BF16)
    xc = _dot(cm_ref[...], hb)
    xs = _dot(sm_ref[...], hb)
    k = pl.program_id(1) * tk + lax.broadcasted_iota(jnp.int32, (tk, 1), 0)
    first = k == 0
    sf = jnp.where(first, 0.0, -xs[:, :w])
    sb = jnp.where(first, 0.0, -xs[:, w:])
    ck, sk = ck_ref[...], sk_ref[...]
    hr = xc[:, :w] + ck * xc[:, w:] - sk * sb
    him = -sf + ck * sb + sk * xc[:, w:]
    nyq = xs[:, :w] - xs[:, w:]
    wk = jnp.where(first, 1.0, 2.0) / (2.0 * seq_len)
    m1_ref[0] = hr * wk
    m2_ref[0] = jnp.where(first, 0.0, -him) * wk
    m3_ref[0] = jnp.where(first, nyq, hr) * wk


def _hy_spectrum(h, dft):
    depth, n, _ = h.shape
    cm, sm, _ = dft
    tk = min(n, 512)
    ang = jnp.arange(n, dtype=F32)[:, None] * (math.pi / n)
    out = jax.ShapeDtypeStruct((depth, n, HY_WIDTH), F32)
    oblk = pl.BlockSpec((1, tk, HY_WIDTH), lambda l, k: (l, k, 0))
    return pl.pallas_call(
        functools.partial(_hy_spectrum_kernel, seq_len=n),
        grid=(depth, n // tk),
        in_specs=[pl.BlockSpec((1, n, 2 * HY_WIDTH), lambda l, k: (l, 0, 0)),
                  pl.BlockSpec((tk, n), lambda l, k: (k, 0)), pl.BlockSpec((tk, n), lambda l, k: (k, 0)),
                  pl.BlockSpec((tk, 1), lambda l, k: (k, 0)), pl.BlockSpec((tk, 1), lambda l, k: (k, 0))],
        out_specs=[oblk, oblk, oblk],
        out_shape=[out, out, out],
        compiler_params=_cparams(("parallel", "parallel")),
        name="hyena_spectrum",
    )(h, cm, sm, jnp.cos(ang), jnp.sin(ang))


def _hyena(proj, conv_w, conv_b, bias, dft, mults, *, row0, n_seq, seq_len, cb, tk, layer=0, dst=None):
    blk0 = row0 // seq_len
    nc = HY_WIDTH // cb
    nk = seq_len // tk
    c0 = HY_COL // cb
    cm, sm, smt = dft
    m1, m2, m3 = mults
    xcol = lambda part: pl.BlockSpec((seq_len, cb), lambda b, c, k: (blk0 + b, c0 + part * nc + c))
    wcol = lambda part: pl.BlockSpec((3, cb), lambda b, c, k: (0, part * nc + c))
    bcol = lambda part: pl.BlockSpec((1, cb), lambda b, c, k: (0, part * nc + c))
    frow = pl.BlockSpec((tk, seq_len), lambda b, c, k: (k, 0))
    mblk = pl.BlockSpec((None, tk, cb), lambda b, c, k: (layer, k, c))
    cb2 = conv_b.reshape(1, 3 * HY_WIDTH)
    if nk == 1:
        body, dst_spec, dst_arg, alias = _into(_hy_short_kernel, 16, dst)
        return pl.pallas_call(
            body,
            grid=(n_seq, nc, 1),
            in_specs=[xcol(0), xcol(1), xcol(2), wcol(0), wcol(1), wcol(2), bcol(0), bcol(1), bcol(2),
                      frow, frow, frow, mblk, mblk, mblk,
                      pl.BlockSpec((1, cb), lambda b, c, k: (0, c))] + dst_spec,
            out_specs=pl.BlockSpec((seq_len, cb), lambda b, c, k: (blk0 + b, c)),
            out_shape=jax.ShapeDtypeStruct((proj.shape[0], HY_WIDTH), BF16),
            input_output_aliases=alias,
            compiler_params=_cparams(("parallel", "parallel", "arbitrary")),
            name="hyena_short",
        )(proj, proj, proj, conv_w, conv_w, conv_w, cb2, cb2, cb2, cm, sm, smt, m1, m2, m3,
          bias.reshape(1, HY_WIDTH), *dst_arg)
    pspec, z, x0c = pl.pallas_call(
        _hy_fwd_kernel,
        grid=(n_seq, nc, nk),
        in_specs=[xcol(0), xcol(1), xcol(2), wcol(0), wcol(1), wcol(2), bcol(0), bcol(1), bcol(2),
                  frow, frow, mblk, mblk, mblk],
        out_specs=[
            pl.BlockSpec((1, 2, tk, cb), lambda b, c, k: (b, 0, k, c)),
            pl.BlockSpec((seq_len, cb), lambda b, c, k: (b, c)),
            pl.BlockSpec((seq_len, cb), lambda b, c, k: (b, c)),
        ],
        out_shape=[
            jax.ShapeDtypeStruct((n_seq, 2, seq_len, HY_WIDTH), BF16),
            jax.ShapeDtypeStruct((n_seq * seq_len, HY_WIDTH), F32),
            jax.ShapeDtypeStruct((n_seq * seq_len, HY_WIDTH), F32),
        ],
        scratch_shapes=[pltpu.VMEM((seq_len, cb), BF16)],
        compiler_params=_cparams(("parallel", "parallel", "arbitrary")),
        name="hyena_fwd",
    )(proj, proj, proj, conv_w, conv_w, conv_w, cb2, cb2, cb2, cm, sm, m1, m2, m3)
    grow = pl.BlockSpec((tk, seq_len), lambda b, c, k: (k, 0))
    tile = pl.BlockSpec((tk, cb), lambda b, c, k: (b * nk + k, c))
    body, dst_spec, dst_arg, alias = _into(_hy_inv_kernel, 6, dst)
    return pl.pallas_call(
        body,
        grid=(n_seq, nc, nk),
        in_specs=[pl.BlockSpec((1, 2, seq_len, cb), lambda b, c, k: (b, 0, 0, c)),
                  grow, grow, tile, tile, pl.BlockSpec((1, cb), lambda b, c, k: (0, c))] + dst_spec,
        out_specs=pl.BlockSpec((tk, cb), lambda b, c, k: (row0 // tk + b * nk + k, c)),
        out_shape=jax.ShapeDtypeStruct((proj.shape[0], HY_WIDTH), BF16),
        input_output_aliases=alias,
        compiler_params=_cparams(("parallel", "parallel", "arbitrary")),
        name="hyena_inv",
    )(pspec, cm, smt, z, x0c, bias.reshape(1, HY_WIDTH), *dst_arg)


def _out_kernel(x_ref, ret_ref, s5_ref, hy_ref, mod_ref, g_ref, gw_ref, gb_ref,
                wr_ref, ws_ref, wh_ref, rt_ref, xo_ref, h_ref, lg_ref):
    m = mod_ref[0]
    y = s5_ref[...]
    s5o = y * jax.nn.sigmoid(_dot(y.astype(BF16), gw_ref[...]) + gb_ref[...])
    mix = (_dot(ret_ref[...], wr_ref[...]) + _dot(s5o.astype(BF16), ws_ref[...])
           + _dot(hy_ref[...], wh_ref[...]))
    x = x_ref[...] + m[2:3] * mix
    xo_ref[...] = x
    h = _rms(x, g_ref[...]) * (1.0 + m[4:5]) + m[3:4]
    hb = h.astype(BF16)
    _store_token_tiles(h_ref, _pack_halves(hb))
    lg_ref[...] = lax.dot_general(rt_ref[...], hb, (((1,), (1,)), ((), ())), preferred_element_type=F32)


def _out_proj(x, ret_o, s5_y, hy_o, mod, g, glu_w, glu_b, w_out, router, n_ctx_rows, dec_seq):
    t = x.shape[0]
    tm = ROW_TILE
    grp = functools.partial(_group_of, tile=tm, n_ctx_rows=n_ctx_rows, dec_seq=dec_seq)
    row = lambda w: pl.BlockSpec((tm, w), lambda i: (i, 0))
    full = lambda a, b: pl.BlockSpec((a, b), lambda i: (0, 0))
    wo = w_out.astype(BF16)
    return pl.pallas_call(
        _out_kernel,
        grid=(t // tm,),
        in_specs=[row(D_MODEL), row(RET_WIDTH), row(S5_WIDTH), row(HY_WIDTH),
                  pl.BlockSpec((1, 6, D_MODEL), lambda i: (grp(i), 0, 0)),
                  full(1, D_MODEL), full(S5_WIDTH, S5_WIDTH), full(1, S5_WIDTH),
                  full(RET_WIDTH, D_MODEL), full(S5_WIDTH, D_MODEL), full(HY_WIDTH, D_MODEL),
                  full(N_EXPERTS, D_MODEL)],
        out_specs=[row(D_MODEL), pl.BlockSpec((tm * TOKEN_ROWS, LANES), lambda i: (i, 0)),
                   pl.BlockSpec((N_EXPERTS, tm), lambda i: (0, i))],
        out_shape=[jax.ShapeDtypeStruct((t, D_MODEL), F32), jax.ShapeDtypeStruct((t * TOKEN_ROWS, LANES), jnp.uint32),
                   jax.ShapeDtypeStruct((N_EXPERTS, t), F32)],
        compiler_params=_cparams(("parallel",)),
        name="out_proj",
    )(x, ret_o, s5_y, hy_o, mod, g.reshape(1, D_MODEL), glu_w.astype(BF16), glu_b.reshape(1, S5_WIDTH),
      wo[:RET_WIDTH], wo[RET_WIDTH:RET_WIDTH + S5_WIDTH], wo[RET_WIDTH + S5_WIDTH:], router.T.astype(BF16))


def _moe_kernel(be_ref, first_ref, slot_ref, nxt_ref, nu_ref, xs_ref, wg_hbm, wu_hbm, wd_hbm, o_ref,
                wg_f, wu_f, wd_f, wg_b, wu_b, wd_b, sem, *, layer):
    i = pl.program_id(0)

    def copies(e, s):
        return (pltpu.make_async_copy(wg_hbm.at[layer, e], wg_f.at[s], sem.at[s, 0]),
                pltpu.make_async_copy(wu_hbm.at[layer, e], wu_f.at[s], sem.at[s, 1]),
                pltpu.make_async_copy(wd_hbm.at[layer, e], wd_f.at[s], sem.at[s, 2]))

    @pl.when(i == 0)
    def _():
        for cp in copies(be_ref[0], 0):
            cp.start()

    @pl.when(first_ref[i] == 1)
    def _():
        s = slot_ref[i]
        for cp in copies(be_ref[i], s):
            cp.wait()

        @pl.when(nxt_ref[i] >= 0)
        def _():
            for cp in copies(nxt_ref[i], 1 - s):
                cp.start(priority=1)

        wg_b[...] = wg_f[s].astype(BF16)
        wu_b[...] = wu_f[s].astype(BF16)
        wd_b[...] = wd_f[s].astype(BF16)

    @pl.when(i < nu_ref[0])
    def _():
        half = D_MODEL // 2
        x_lo, x_hi = _unpack_halves(_load_token_tiles(xs_ref, MOE_BM))
        x_lo = x_lo.astype(BF16)
        x_hi = x_hi.astype(BF16)
        gate = _dot(x_lo, wg_b[0:half, :]) + _dot(x_hi, wg_b[half:, :])
        up = _dot(x_lo, wu_b[0:half, :]) + _dot(x_hi, wu_b[half:, :])
        hb = gate * jax.nn.sigmoid(gate) * up
        _store_token_tiles(o_ref, _pack_halves(_dot(hb.astype(BF16), wd_b[...]).astype(BF16)))

    @pl.when(i >= nu_ref[0])
    def _():
        o_ref[...] = jnp.zeros_like(o_ref)


def _moe_grouped(xs, blk_e, first, slot, nxt, n_used, w_gate, w_up, w_down, layer):
    pr = xs.shape[0] // TOKEN_ROWS
    bm = MOE_BM
    nb = pr // bm
    grid_spec = pltpu.PrefetchScalarGridSpec(
        num_scalar_prefetch=5,
        grid=(nb,),
        in_specs=[
            pl.BlockSpec((bm * TOKEN_ROWS, LANES), lambda i, *_: (i, 0)),
            pl.BlockSpec(memory_space=pl.ANY),
            pl.BlockSpec(memory_space=pl.ANY),
            pl.BlockSpec(memory_space=pl.ANY),
        ],
        out_specs=pl.BlockSpec((bm * TOKEN_ROWS, LANES), lambda i, *_: (i, 0)),
        scratch_shapes=[pltpu.VMEM((2, D_MODEL, D_EXPERT), F32), pltpu.VMEM((2, D_MODEL, D_EXPERT), F32),
                        pltpu.VMEM((2, D_EXPERT, D_MODEL), F32),
                        pltpu.VMEM((D_MODEL, D_EXPERT), BF16), pltpu.VMEM((D_MODEL, D_EXPERT), BF16),
                        pltpu.VMEM((D_EXPERT, D_MODEL), BF16),
                        pltpu.SemaphoreType.DMA((2, 3))],
    )
    return pl.pallas_call(
        functools.partial(_moe_kernel, layer=layer),
        grid_spec=grid_spec,
        out_shape=jax.ShapeDtypeStruct((pr * TOKEN_ROWS, LANES), jnp.uint32),
        compiler_params=_cparams(("arbitrary",)),
        name="moe_grouped",
    )(blk_e, first, slot, nxt, n_used, xs, w_gate, w_up, w_down)


DISPATCH_TILE = 1024


def _dispatch_kernel(info_ref, nu_ref, pos_ref, h_ref, xs_out, zbuf, sem, zsem, *, nb):
    tm = pos_ref.shape[1]
    tr = TOKEN_ROWS
    bm = MOE_BM

    @pl.when(pl.program_id(0) == 0)
    def _():
        zbuf[...] = jnp.zeros_like(zbuf)

        def zero_block(first_row):
            rows = pl.ds(pl.multiple_of(first_row * tr, tr), bm * tr)
            return pltpu.make_async_copy(zbuf, xs_out.at[rows], zsem)

        for act in ("start", "wait"):
            def last_of_expert(e, carry, act=act):
                @pl.when(info_ref[e, 3] > info_ref[e, 1])
                def _():
                    getattr(zero_block(info_ref[e, 3] - bm), act)()
                return carry

            def tail_block(b, carry, act=act):
                getattr(zero_block(b * bm), act)()
                return carry

            lax.fori_loop(0, N_EXPERTS, last_of_expert, 0)
            lax.fori_loop(nu_ref[0], nb, tail_block, 0)

    def send(tok, carry):
        src = h_ref.at[pl.ds(pl.multiple_of(tok * tr, tr), tr)]
        for k in range(TOP_K):
            row = pl.multiple_of(pos_ref[k, tok], tr)
            pltpu.make_async_copy(src, xs_out.at[pl.ds(row, tr)], sem).start(priority=k % 2)
        return carry

    lax.fori_loop(0, tm, send, 0, unroll=4)
    n = tm * TOP_K * tr
    pltpu.make_async_copy(xs_out.at[pl.ds(0, n)], xs_out.at[pl.ds(0, n)], sem).wait()


def _dispatch(h2, pos, info, n_used, nb):
    t = pos.shape[1]
    tm = DISPATCH_TILE
    return pl.pallas_call(
        functools.partial(_dispatch_kernel, nb=nb),
        grid=(t // tm,),
        in_specs=[pl.BlockSpec(memory_space=pltpu.SMEM),
                  pl.BlockSpec(memory_space=pltpu.SMEM),
                  pl.BlockSpec((SUBLANES, tm), lambda i: (0, i), memory_space=pltpu.SMEM),
                  pl.BlockSpec((tm * TOKEN_ROWS, LANES), lambda i: (i, 0))],
        out_specs=pl.BlockSpec(memory_space=pl.ANY),
        out_shape=jax.ShapeDtypeStruct((nb * MOE_BM * TOKEN_ROWS, LANES), h2.dtype),
        scratch_shapes=[pltpu.VMEM((MOE_BM * TOKEN_ROWS, LANES), h2.dtype),
                        pltpu.SemaphoreType.DMA(()), pltpu.SemaphoreType.DMA(())],
        compiler_params=pltpu.CompilerParams(dimension_semantics=("arbitrary",)),
        name="dispatch",
    )(info, n_used, pos, h2)


ROUTE_TILE = 512


def _router_kernel(lg_ref, bias_ref, pos_ref, gate_ref, be_ref, info_ref, rank_scr, ek_scr, *, t, nbp):
    tl = ROUTE_TILE
    ne = N_EXPERTS
    bm = MOE_BM
    row = lax.broadcasted_iota(jnp.int32, (ne, tl), 0)
    tri = (lax.broadcasted_iota(jnp.int32, (tl, tl), 0) < lax.broadcasted_iota(jnp.int32, (tl, tl), 1)).astype(BF16)
    bias = bias_ref[...]

    def select(i, counts):
        cols = pl.ds(pl.multiple_of(i * tl, tl), tl)
        s = jax.nn.sigmoid(lg_ref[:, cols])
        sel = s + bias
        mask = jnp.zeros((ne, tl), F32)
        vals = []
        for k in range(TOP_K):
            best = jnp.max(sel, axis=0, keepdims=True)
            idx = jnp.min(jnp.where(sel == best, row, ne), axis=0, keepdims=True)
            hit = row == idx
            vals.append(jnp.sum(jnp.where(hit, s, 0.0), axis=0, keepdims=True))
            sel = jnp.where(hit, -jnp.inf, sel)
            mask = jnp.where(hit, 1.0, mask)
            ek_scr[k:k + 1, cols] = idx
        total = vals[0]
        for v in vals[1:]:
            total = total + v
        scale = ROUTED_SCALE / total
        for k in range(TOP_K):
            gate_ref[k:k + 1, cols] = vals[k] * scale
        gate_ref[TOP_K:SUBLANES, cols] = jnp.zeros((SUBLANES - TOP_K, tl), F32)
        rank_scr[:, cols] = _dot(mask.astype(BF16), tri) + counts
        return counts + jnp.sum(mask, axis=1, keepdims=True)

    counts = lax.fori_loop(0, t // tl, select, jnp.zeros((ne, 1), F32))
    counts = counts.astype(jnp.int32)
    shift = bm.bit_length() - 1
    assert bm == 1 << shift
    padded = ((counts + (bm - 1)) >> shift) << shift
    e0 = lax.broadcasted_iota(jnp.int32, (ne, ne), 0)
    e1 = lax.broadcasted_iota(jnp.int32, (ne, ne), 1)
    padded_row = jnp.sum(jnp.where(e0 == e1, padded, 0), axis=0, keepdims=True)
    counts_row = jnp.sum(jnp.where(e0 == e1, counts, 0), axis=0, keepdims=True)
    pstart = jnp.sum(jnp.where(e1 < e0, padded_row, 0), axis=1, keepdims=True)
    ustart = jnp.sum(jnp.where(e1 < e0, counts_row, 0), axis=1, keepdims=True)
    pend = pstart + padded
    lane = lax.broadcasted_iota(jnp.int32, (ne, LANES), 1)
    info_ref[...] = jnp.where(lane == 0, counts, jnp.where(lane == 1, pstart, jnp.where(lane == 2, ustart, pend)))
    blk = lax.broadcasted_iota(jnp.int32, (ne, nbp), 1) * bm
    owner = jnp.sum(jnp.where(pend <= blk, 1, 0), axis=0, keepdims=True)
    be_ref[...] = jnp.minimum(owner, ne - 1)
    pstart_f = pstart.astype(F32)

    def place(i, carry):
        cols = pl.ds(pl.multiple_of(i * tl, tl), tl)
        dest = rank_scr[:, cols] + pstart_f
        for k in range(TOP_K):
            hit = row == ek_scr[k:k + 1, cols]
            pos_ref[k:k + 1, cols] = (jnp.sum(jnp.where(hit, dest, 0.0), axis=0, keepdims=True)
                                      * float(TOKEN_ROWS)).astype(jnp.int32)
        pos_ref[TOP_K:SUBLANES, cols] = jnp.zeros((SUBLANES - TOP_K, tl), jnp.int32)
        return carry

    lax.fori_loop(0, t // tl, place, 0)


def _router(logits_t, router_bias, nb):
    t = logits_t.shape[1]
    nbp = -(-nb // LANES) * LANES
    return pl.pallas_call(
        functools.partial(_router_kernel, t=t, nbp=nbp),
        out_shape=[jax.ShapeDtypeStruct((SUBLANES, t), jnp.int32), jax.ShapeDtypeStruct((SUBLANES, t), F32),
                   jax.ShapeDtypeStruct((1, nbp), jnp.int32), jax.ShapeDtypeStruct((N_EXPERTS, LANES), jnp.int32)],
        scratch_shapes=[pltpu.VMEM((N_EXPERTS, t), F32), pltpu.VMEM((SUBLANES, t), jnp.int32)],
        compiler_params=pltpu.CompilerParams(vmem_limit_bytes=VMEM_LIMIT),
        name="router",
    )(logits_t, router_bias.astype(F32).reshape(N_EXPERTS, 1))


def _dispatch_plan(blk_e_row, info, nb):
    bm = MOE_BM
    pend = info[:, 3]
    blk_e = blk_e_row[0, :nb]
    n_used = pend[-1] // bm
    blk = jnp.arange(nb, dtype=jnp.int32)
    prev_e = jnp.concatenate([jnp.full((1,), -1, jnp.int32), blk_e[:-1]])
    first = jnp.logical_and(blk < n_used, blk_e != prev_e)
    slot = (jnp.cumsum(first.astype(jnp.int32)) - 1) % 2
    first_at = jnp.where(first, blk, nb)
    nxt_first = lax.cummin(jnp.concatenate([first_at[1:], jnp.full((1,), nb, jnp.int32)]), reverse=True)
    nxt = jnp.where(nxt_first < nb, blk_e[jnp.minimum(nxt_first, nb - 1)], -1)
    return (blk_e, first.astype(jnp.int32), slot.astype(jnp.int32), nxt.astype(jnp.int32),
            n_used.astype(jnp.int32).reshape(1))


def _shared_kernel(pos_ref, nxt_ref, x_ref, h_ref, gt_ref, mod_ref, sg_ref, su_ref, sd_ref, fn_ref, eo_hbm,
                   o_ref, buf0, buf1, acc_scr, sem, *, final):
    i = pl.program_id(0)
    n = pl.num_programs(0)
    tm = x_ref.shape[0]
    tr = TOKEN_ROWS

    half = D_MODEL // 2
    grp = SUBLANES

    def fetch_tokens(idx_ref, buf, slot, tok0):
        for tt in range(grp):
            tok = tok0 + tt
            for k in range(TOP_K):
                row = pl.multiple_of(idx_ref[k, tok], tr)
                pltpu.make_async_copy(eo_hbm.at[pl.ds(row, tr)], buf.at[k, pl.ds(pl.multiple_of(tok * tr, tr), tr)],
                                      sem.at[slot]).start(priority=k % 2)

    def arrived(buf, slot):
        for k in range(TOP_K):
            pltpu.make_async_copy(eo_hbm.at[pl.ds(0, tm * tr)], buf.at[k], sem.at[slot]).wait()

    def step(cur, cur_slot, nxt, nxt_slot):
        arrived(cur, cur_slot)

        def group(c, carry):
            tok0 = pl.multiple_of(c * grp, grp)
            fetch_tokens(nxt_ref, nxt, nxt_slot, tok0)
            gt = gt_ref[pl.ds(tok0, grp), :]
            r_lo = jnp.zeros((grp, half), F32)
            r_hi = r_lo
            for k in range(TOP_K):
                words = jnp.concatenate(
                    [cur[k, pl.ds(tok0 * tr + s, grp, stride=tr), :] for s in range(tr)], axis=1)
                e_lo, e_hi = _unpack_halves(words)
                r_lo = r_lo + gt[:, k:k + 1] * e_lo
                r_hi = r_hi + gt[:, k:k + 1] * e_hi
            acc_scr[pl.ds(tok0, grp), :] = jnp.concatenate([r_lo, r_hi], axis=1)
            return carry

        lax.fori_loop(0, tm // grp, group, 0)
        h_lo, h_hi = _unpack_halves(_load_token_tiles(h_ref, tm))
        h_lo = h_lo.astype(BF16)
        h_hi = h_hi.astype(BF16)
        gate = _dot(h_lo, sg_ref[0:half, :]) + _dot(h_hi, sg_ref[half:, :])
        up = _dot(h_lo, su_ref[0:half, :]) + _dot(h_hi, su_ref[half:, :])
        act = gate * jax.nn.sigmoid(gate) * up
        shared = _dot(act.astype(BF16), sd_ref[...])
        x = x_ref[...] + mod_ref[0][5:6] * (acc_scr[...] + shared)
        if final:
            x = _rms(x, fn_ref[...])
        o_ref[...] = x

        @pl.when(i == n - 1)
        def _():
            arrived(nxt, nxt_slot)

    @pl.when(i == 0)
    def _():
        def first(c, carry):
            fetch_tokens(pos_ref, buf0, 0, pl.multiple_of(c * grp, grp))
            return carry
        lax.fori_loop(0, tm // grp, first, 0)

    for parity, cur, nxt in ((0, buf0, buf1), (1, buf1, buf0)):
        @pl.when(i % 2 == parity)
        def _(parity=parity, cur=cur, nxt=nxt):
            step(cur, parity, nxt, 1 - parity)


def _shared(x, h, gates_t, pos, eo, mod, sg, su, sd, final_norm, n_ctx_rows, dec_seq, *, final, row0=0, rows=None):
    rows = x.shape[0] if rows is None else rows
    tm = ROW_TILE // 2
    b0 = row0 // tm
    steps = rows // tm
    grp = lambda i: _group_of(i + b0, tm, n_ctx_rows, dec_seq)
    row = pl.BlockSpec((tm, D_MODEL), lambda i: (i + b0, 0))
    prow = pl.BlockSpec((tm * TOKEN_ROWS, LANES), lambda i: (i + b0, 0))
    full = lambda a, b: pl.BlockSpec((a, b), lambda i: (0, 0))
    buf = pltpu.VMEM((TOP_K, tm * TOKEN_ROWS, LANES), eo.dtype)
    return pl.pallas_call(
        functools.partial(_shared_kernel, final=final),
        grid=(steps,),
        in_specs=[pl.BlockSpec((SUBLANES, tm), lambda i: (0, i + b0), memory_space=pltpu.SMEM),
                  pl.BlockSpec((SUBLANES, tm), lambda i: (0, jnp.minimum(i + 1, steps - 1) + b0),
                               memory_space=pltpu.SMEM),
                  row, prow, pl.BlockSpec((tm, SUBLANES), lambda i: (i + b0, 0)),
                  pl.BlockSpec((1, 6, D_MODEL), lambda i: (grp(i), 0, 0)),
                  full(D_MODEL, D_SHARED), full(D_MODEL, D_SHARED), full(D_SHARED, D_MODEL), full(1, D_MODEL),
                  pl.BlockSpec(memory_space=pl.ANY)],
        out_specs=pl.BlockSpec((tm, D_MODEL), lambda i: (i, 0)),
        out_shape=jax.ShapeDtypeStruct((rows, D_MODEL), F32),
        scratch_shapes=[buf, buf, pltpu.VMEM((tm, D_MODEL), F32), pltpu.SemaphoreType.DMA((2,))],
        compiler_params=_cparams(("arbitrary",)),
        name="shared_final" if final else "shared",
    )(pos, pos, x, h, gates_t, mod, sg.astype(BF16), su.astype(BF16), sd.astype(BF16),
      final_norm.reshape(1, D_MODEL), eo)


def kernel(x_prompt, x_sample, state_ret, state_s5_re, state_s5_im, c, c_ctx, w_ada, b_ada, norm_mix, norm_ffn, w_in, w_out, ret_decay, s5_lam_re, s5_lam_im, s5_log_dt, s5_b_re, s5_b_im, s5_c_re, s5_c_im, s5_d, s5_glu_w, s5_glu_b, hy_conv_w, hy_conv_b, hy_f1_w, hy_f1_b, hy_f2_w, hy_f2_b, hy_f3_w, hy_f3_b, hy_freq, hy_decay, hy_bias, moe_router, moe_router_bias, moe_w_gate, moe_w_up, moe_w_down, sh_w_gate, sh_w_up, sh_w_down, final_norm):
    n_ctx, seq, d = x_prompt.shape
    n_dec, dec_seq, _ = x_sample.shape
    n_ctx_rows = n_ctx * seq
    t = n_ctx_rows + n_dec * dec_seq

    x = jnp.concatenate([x_prompt.reshape(n_ctx_rows, d), x_sample.reshape(n_dec * dec_seq, d)], axis=0)
    cond = jnp.concatenate([c_ctx[None, :], c], axis=0)
    cond8 = jnp.pad(cond, ((0, SUBLANES - cond.shape[0]), (0, 0)))
    mods = _ada(cond8, w_ada, b_ada)[:, :1 + n_dec].reshape(DEPTH, 1 + n_dec, 6, d)

    cos2, sin2 = _rope_tables(dec_seq)
    no_rope = jnp.zeros((seq, LANES), F32)
    zero_ret = jnp.zeros((1, 2, RET_HEADS, RET_DK, RET_DV), F32)
    zero_s5 = jnp.zeros((n_ctx, 2, S5_GROUPS, S5_STATE), F32)
    dft_ctx = _dft_mats(seq)
    dft_dec = _dft_mats(dec_seq)

    w_in_bf = w_in.astype(BF16)
    mats = _s5_mats(s5_lam_re, s5_lam_im, s5_log_dt, s5_b_re, s5_b_im, s5_c_re, s5_c_im, s5_d)
    filt = (hy_f1_w, hy_f1_b, hy_f2_w, hy_f2_b, hy_f3_w, hy_f3_b, hy_freq, hy_decay)
    mults_ctx = _hy_spectrum(jax.vmap(functools.partial(_hy_filter_taps, seq))(*filt), dft_ctx)
    mults_dec = _hy_spectrum(jax.vmap(functools.partial(_hy_filter_taps, dec_seq))(*filt), dft_dec)

    ret_states = jnp.zeros((n_ctx, DEPTH, 2, RET_HEADS, RET_DK, RET_DV), F32)
    s5r_list, s5i_list = [], []
    for l in range(DEPTH):
        mod = mods[l]
        proj = _in_proj(x, mod, norm_mix[l], w_in_bf, l, n_ctx_rows, dec_seq)

        log_gamma = jax.nn.log_sigmoid(ret_decay[l].astype(F32))
        ret_o, ret_states = _retention(proj, log_gamma, zero_ret, no_rope, no_rope,
                                       row0=0, n_seq=n_ctx, seq_len=seq, hb=RET_HEADS, rope=False,
                                       dst=jnp.zeros((t, RET_WIDTH), BF16), states=ret_states, layer=l)
        ret_o, _ = _retention(proj, log_gamma, state_ret[:, l].astype(F32), cos2, sin2,
                              row0=n_ctx_rows, n_seq=n_dec, seq_len=dec_seq, hb=2, rope=True, dst=ret_o)

        s5_y, s5_re, s5_im = _s5(proj, mats, zero_s5, zero_s5, row0=0, n_seq=n_ctx, seq_len=seq, layer=l,
                                 dst=jnp.zeros((t, S5_WIDTH), F32))
        s5_y, _, _ = _s5(proj, mats, state_s5_re[:, l], state_s5_im[:, l],
                         row0=n_ctx_rows, n_seq=n_dec, seq_len=dec_seq, layer=l, dst=s5_y)
        s5r_list.append(s5_re)
        s5i_list.append(s5_im)

        hy_o = _hyena(proj, hy_conv_w[l], hy_conv_b[l], hy_bias[l], dft_ctx, mults_ctx, layer=l,
                      row0=0, n_seq=n_ctx, seq_len=seq, cb=HY_WIDTH, tk=seq, dst=jnp.zeros((t, HY_WIDTH), BF16))
        hy_o = _hyena(proj, hy_conv_w[l], hy_conv_b[l], hy_bias[l], dft_dec, mults_dec, layer=l,
                      row0=n_ctx_rows, n_seq=n_dec, seq_len=dec_seq, cb=HY_WIDTH // 2, tk=512, dst=hy_o)

        x, h2, logits = _out_proj(x, ret_o, s5_y, hy_o, mod, norm_ffn[l], s5_glu_w[l], s5_glu_b[l],
                                  w_out[l], moe_router[l], n_ctx_rows, dec_seq)

        nb = -(-(t * TOP_K) // MOE_BM) + N_EXPERTS
        pos, gates, blk_e_row, info = _router(logits, moe_router_bias[l], nb)
        blk_e, first, slot, nxt, n_used = _dispatch_plan(blk_e_row, info, nb)
        xs = _dispatch(h2, pos, info, n_used, nb)
        eo = _moe_grouped(xs, blk_e, first, slot, nxt, n_used, moe_w_gate, moe_w_up, moe_w_down, l)
        gates_t = gates.T

        sh = (sh_w_gate[l], sh_w_up[l], sh_w_down[l])
        if l < DEPTH - 1:
            x = _shared(x, h2, gates_t, pos, eo, mod, *sh, final_norm, n_ctx_rows, dec_seq, final=False)
        else:
            y_c = _shared(x, h2, gates_t, pos, eo, mod, *sh, final_norm, n_ctx_rows, dec_seq, final=True,
                          row0=0, rows=n_ctx_rows)
            y_d = _shared(x, h2, gates_t, pos, eo, mod, *sh, final_norm, n_ctx_rows, dec_seq, final=True,
                          row0=n_ctx_rows, rows=n_dec * dec_seq)

    return (y_c.reshape(n_ctx, seq, d), y_d.reshape(n_dec, dec_seq, d),
            ret_states, jnp.stack(s5r_list, axis=1), jnp.stack(s5i_list, axis=1))
```

```python
import functools
import math

import jax
import jax.numpy as jnp
from jax import lax
from jax.experimental import pallas as pl
from jax.experimental.pallas import tpu as pltpu

F32 = jnp.float32
BF16 = jnp.bfloat16

D_MODEL = 2048
DEPTH = 2
GRID_W = 64
RET_HEADS = 8
RET_DK = 128
RET_DV = 128
RET_WIDTH = RET_HEADS * RET_DV
RET_CHUNK = 256
ROPE_BASE = 10000.0
S5_WIDTH = 512
S5_GROUP = 16
S5_GROUPS = S5_WIDTH // S5_GROUP
S5_STATE = 64
S5_Q = 16
HY_WIDTH = 512
HY_BANDS = 16
IN_WIDTH = 4 * RET_WIDTH + S5_WIDTH + 3 * HY_WIDTH
U_COL = 4 * RET_WIDTH
HY_COL = U_COL + S5_WIDTH
N_EXPERTS = 64
TOP_K = 6
D_EXPERT = 512
D_SHARED = 512
ROUTED_SCALE = 2.5
EPS = 1e-6

LANES = 128
SUBLANES = 8
VMEM_LIMIT = 56 * 1024 * 1024

ROW_TILE = 512
MOE_BM = 256


def _cparams(sem):
    return pltpu.CompilerParams(dimension_semantics=sem, vmem_limit_bytes=VMEM_LIMIT)


def _dot(a, b):
    return jnp.dot(a, b, preferred_element_type=F32)


def _rms(x, g):
    var = jnp.mean(x * x, axis=-1, keepdims=True)
    return x * lax.rsqrt(var + EPS) * g


def _pack_halves(xb):
    n = xb.shape[1] // 2
    lo = lax.bitcast_convert_type(xb[:, :n].astype(F32), jnp.uint32) >> 16
    hi = lax.bitcast_convert_type(xb[:, n:].astype(F32), jnp.uint32)
    return lo | hi


def _unpack_halves(w):
    lo = lax.bitcast_convert_type(w << 16, F32)
    hi = lax.bitcast_convert_type(w & jnp.uint32(0xFFFF0000), F32)
    return lo, hi


TOKEN_ROWS = D_MODEL // 2 // LANES


def _store_token_tiles(ref, w):
    m = w.shape[0]
    for s in range(TOKEN_ROWS):
        ref[pl.ds(s, m, stride=TOKEN_ROWS), :] = w[:, s * LANES:(s + 1) * LANES]


def _load_token_tiles(ref, m):
    return jnp.concatenate([ref[pl.ds(s, m, stride=TOKEN_ROWS), :] for s in range(TOKEN_ROWS)], axis=1)


def _ada_kernel(c_ref, w_ref, b_ref, o_ref):
    c = c_ref[...]
    s = (c * jax.nn.sigmoid(c)).astype(BF16)
    o_ref[0] = _dot(s, w_ref[0].astype(BF16)) + b_ref[0]


def _ada(cond8, w_ada, b_ada):
    tn = 1024
    n = w_ada.shape[-1]
    return pl.pallas_call(
        _ada_kernel,
        grid=(DEPTH, n // tn),
        in_specs=[
            pl.BlockSpec((SUBLANES, D_MODEL), lambda l, j: (0, 0)),
            pl.BlockSpec((1, D_MODEL, tn), lambda l, j: (l, 0, j)),
            pl.BlockSpec((1, 1, tn), lambda l, j: (l, 0, j)),
        ],
        out_specs=pl.BlockSpec((1, SUBLANES, tn), lambda l, j: (l, 0, j)),
        out_shape=jax.ShapeDtypeStruct((DEPTH, SUBLANES, n), F32),
        compiler_params=_cparams(("parallel", "parallel")),
        name="ada",
    )(cond8, w_ada, b_ada.reshape(DEPTH, 1, n))


def _group_of(i, tile, n_ctx_rows, dec_seq):
    ctx_tiles = n_ctx_rows // tile
    per = dec_seq // tile
    return jnp.where(i < ctx_tiles, 0, 1 + (i - ctx_tiles) // per)


def _in_kernel(x_ref, mod_ref, g_ref, w_ref, o_ref, h_scr):
    @pl.when(pl.program_id(1) == 0)
    def _():
        m = mod_ref[0]
        g = g_ref[...]
        rc = 128

        def rows_pass(c, carry):
            rows = pl.ds(pl.multiple_of(c * rc, rc), rc)
            h = _rms(x_ref[rows, :], g) * (1.0 + m[1:2]) + m[0:1]
            h_scr[rows, :] = h.astype(BF16)
            return carry

        lax.fori_loop(0, x_ref.shape[0] // rc, rows_pass, 0)

    o_ref[...] = _dot(h_scr[...], w_ref[...])


def _in_proj(x, mod, g, w_bf, layer, n_ctx_rows, dec_seq):
    t = x.shape[0]
    tm, tn = 1024, 1024
    grp = functools.partial(_group_of, tile=tm, n_ctx_rows=n_ctx_rows, dec_seq=dec_seq)
    return pl.pallas_call(
        _in_kernel,
        grid=(t // tm, IN_WIDTH // tn),
        in_specs=[
            pl.BlockSpec((tm, D_MODEL), lambda i, j: (i, 0)),
            pl.BlockSpec((1, 6, D_MODEL), lambda i, j: (grp(i), 0, 0)),
            pl.BlockSpec((1, D_MODEL), lambda i, j: (0, 0)),
            pl.BlockSpec((None, D_MODEL, tn), lambda i, j: (layer, 0, j)),
        ],
        out_specs=pl.BlockSpec((tm, tn), lambda i, j: (i, j)),
        out_shape=jax.ShapeDtypeStruct((t, IN_WIDTH), F32),
        scratch_shapes=[pltpu.VMEM((tm, D_MODEL), BF16)],
        compiler_params=_cparams(("parallel", "arbitrary")),
        name="in_proj",
    )(x, mod, g.reshape(1, D_MODEL), w_bf)


def _ret_kernel(lg_ref, q_ref, k_ref, v_ref, gt_ref, cos_ref, sin_ref, s0_ref,
                o_ref, sfin_ref, acc_scr, q_scr, k_scr, *, seq_len, hb, rope, nsb=1):
    c = RET_CHUNK
    n_chunks = seq_len // c
    ii = lax.broadcasted_iota(jnp.int32, (c, c), 0)
    jj = lax.broadcasted_iota(jnp.int32, (c, c), 1)
    rel = (ii - jj).astype(F32)
    ci = lax.broadcasted_iota(jnp.int32, (c, 1), 0).astype(F32)
    one = jnp.ones((1, 1), F32)
    tdot = functools.partial(lax.dot_general, preferred_element_type=F32)

    def make_head(hh, base=0):
        head = pl.program_id(1) * hb + hh
        lgf = lg_ref[0, head]
        lgb = lg_ref[1, head]
        dmask = (jnp.where(rel >= 0, jnp.exp(lgf * jnp.maximum(rel, 0.0)), 0.0)
                 + jnp.where(rel <= 0, jnp.exp(lgb * jnp.maximum(-rel, 0.0)), 0.0))
        qd_f = jnp.exp(lgf * (ci + 1.0))
        kd_f = jnp.exp(lgf * (c - 1.0 - ci))
        cd_f = jnp.exp(lgf * c * one)
        qd_b = jnp.exp(lgb * (c - ci))
        kd_b = jnp.exp(lgb * ci)
        cd_b = jnp.exp(lgb * c * one)
        lanes = slice(hh * LANES, (hh + 1) * LANES)

        def rows_of(n):
            if isinstance(n, int):
                return slice(base + n * c, base + (n + 1) * c)
            return pl.ds(pl.multiple_of(n * c, c), c)

        def fwd_chunk(n, s_f):
            rows = rows_of(n)
            q = q_ref[rows, lanes]
            k = k_ref[rows, lanes] * (RET_DK ** -0.5)
            if rope:
                cs = cos_ref[rows, :]
                sn = sin_ref[rows, :]
                q = q * cs + pltpu.roll(q, RET_DK // 2, 1) * sn
                k = k * cs + pltpu.roll(k, RET_DK // 2, 1) * sn
            qb = q.astype(BF16)
            vb = v_ref[rows, lanes].astype(BF16)
            q_scr[rows, lanes] = qb
            k_scr[rows, lanes] = k
            scores = tdot(qb, k.astype(BF16), (((1,), (1,)), ((), ()))) * dmask
            inner = _dot(scores.astype(BF16), vb)
            cross = _dot(qb, s_f.astype(BF16)) * qd_f
            acc_scr[rows, lanes] = inner + cross
            upd = tdot((k * kd_f).astype(BF16), vb, (((0,), (0,)), ((), ())))
            return s_f * cd_f + upd

        def bwd_chunk(m, s_b):
            n = n_chunks - 1 - m
            rows = rows_of(n)
            qb = q_scr[rows, lanes]
            k = k_scr[rows, lanes]
            vb = v_ref[rows, lanes].astype(BF16)
            o = acc_scr[rows, lanes] + _dot(qb, s_b.astype(BF16)) * qd_b
            mu = jnp.mean(o, axis=-1, keepdims=True)
            oc = o - mu
            var = jnp.mean(oc * oc, axis=-1, keepdims=True)
            o = oc * lax.rsqrt(var + EPS)
            g = gt_ref[rows, lanes]
            o_ref[rows, lanes] = (g * jax.nn.sigmoid(g) * o).astype(o_ref.dtype)
            upd = tdot((k * kd_b).astype(BF16), vb, (((0,), (0,)), ((), ())))
            return s_b * cd_b + upd

        return fwd_chunk, bwd_chunk

    if n_chunks <= 4:
        for sq in range(nsb):
            s0_row = sq if s0_ref.shape[0] > 1 else 0
            for hh in range(hb):
                fwd_chunk, bwd_chunk = make_head(hh, sq * seq_len)
                s_f = s0_ref[s0_row, 0, hh]
                s_b = s0_ref[s0_row, 1, hh]
                for n in range(n_chunks):
                    s_f = fwd_chunk(n, s_f)
                for m in range(n_chunks):
                    s_b = bwd_chunk(m, s_b)
                sfin_ref[sq, 0, hh] = s_f
                sfin_ref[sq, 1, hh] = s_b
    else:
        fns = [make_head(hh) for hh in range(hb)]
        s_f = lax.fori_loop(0, n_chunks, lambda n, ss: tuple(f[0](n, s) for f, s in zip(fns, ss)),
                            tuple(s0_ref[0, 0, hh] for hh in range(hb)))
        s_b = lax.fori_loop(0, n_chunks, lambda m, ss: tuple(f[1](m, s) for f, s in zip(fns, ss)),
                            tuple(s0_ref[0, 1, hh] for hh in range(hb)))
        for hh in range(hb):
            sfin_ref[0, 0, hh] = s_f[hh]
            sfin_ref[0, 1, hh] = s_b[hh]


def _into(kernel_fn, n_in, dst):
    dsts = [d for d in (dst if isinstance(dst, (list, tuple)) else [dst])]
    outs = [k for k, d in enumerate(dsts) if d is not None]
    if not outs:
        return kernel_fn, [], [], {}

    def body(*refs):
        return kernel_fn(*refs[:n_in], *refs[n_in + len(outs):])

    return (body, [pl.BlockSpec(memory_space=pl.ANY)] * len(outs), [dsts[k] for k in outs],
            {n_in + pos: k for pos, k in enumerate(outs)})


def _retention(proj, log_gamma, s0, cos2, sin2, *, row0, n_seq, seq_len, hb, rope, dst=None,
               states=None, layer=0, nsb=1):
    assert nsb == 1 or not rope
    rb = nsb * seq_len
    blk0 = row0 // rb
    body, dst_spec, dst_arg, alias = _into(
        functools.partial(_ret_kernel, seq_len=seq_len, hb=hb, rope=rope, nsb=nsb), 8, [dst, states])
    if states is None:
        st_spec = pl.BlockSpec((nsb, 2, hb, RET_DK, RET_DV), lambda b, h, lg: (b, 0, h, 0, 0))
        st_shape = jax.ShapeDtypeStruct((n_seq, 2, RET_HEADS, RET_DK, RET_DV), F32)
    else:
        st_spec = pl.BlockSpec((nsb, None, 2, hb, RET_DK, RET_DV), lambda b, h, lg: (b, layer, 0, h, 0, 0))
        st_shape = jax.ShapeDtypeStruct(states.shape, F32)
    w = hb * LANES
    hblocks = RET_HEADS // hb
    col = lambda part: (lambda b, h, lg: (blk0 + b, part * hblocks + h))
    shared_s0 = s0.shape[0] == 1
    grid_spec = pltpu.PrefetchScalarGridSpec(
        num_scalar_prefetch=1,
        grid=(n_seq // nsb, hblocks),
        in_specs=[
            pl.BlockSpec((rb, w), col(0)),
            pl.BlockSpec((rb, w), col(1)),
            pl.BlockSpec((rb, w), col(2)),
            pl.BlockSpec((rb, w), col(3)),
            pl.BlockSpec((seq_len, LANES), lambda b, h, lg: (0, 0)),
            pl.BlockSpec((seq_len, LANES), lambda b, h, lg: (0, 0)),
            pl.BlockSpec((1 if shared_s0 else nsb, 2, hb, RET_DK, RET_DV),
                         lambda b, h, lg: (0 if shared_s0 else b, 0, h, 0, 0)),
        ] + dst_spec,
        out_specs=[
            pl.BlockSpec((rb, w), lambda b, h, lg: (blk0 + b, h)),
            st_spec,
        ],
        scratch_shapes=[
            pltpu.VMEM((rb, w), F32),
            pltpu.VMEM((rb, w), BF16),
            pltpu.VMEM((rb, w), F32),
        ],
    )
    return pl.pallas_call(
        body,
        grid_spec=grid_spec,
        out_shape=[jax.ShapeDtypeStruct((proj.shape[0], RET_WIDTH), BF16), st_shape],
        input_output_aliases=alias,
        compiler_params=_cparams(("parallel", "arbitrary")),
        name="retention",
    )(log_gamma, proj, proj, proj, proj, cos2, sin2, s0, *dst_arg)


def _rope_tables(seq_len):
    rows_n = seq_len // GRID_W
    rows = jnp.repeat(jnp.arange(rows_n, dtype=F32), GRID_W)
    cols = jnp.tile(jnp.arange(GRID_W, dtype=F32), rows_n)
    nf = RET_DK // 4
    inv = ROPE_BASE ** (-jnp.arange(nf, dtype=F32) / nf)
    ang = jnp.concatenate([rows[:, None] * inv, cols[:, None] * inv], axis=-1)
    cs, sn = jnp.cos(ang), jnp.sin(ang)
    return jnp.concatenate([cs, cs], axis=-1), jnp.concatenate([-sn, sn], axis=-1)


S5_GB = LANES // S5_GROUP
S5_W = S5_Q * LANES
S5_SPLIT = 4
S5_SW = S5_GB * S5_STATE
S5_SB = S5_SW // LANES


def _s5_kernel(u_ref, bq_ref, k_ref, cq_ref, ar_ref, ai_ref, d_ref, h0_ref,
               y_ref, hf_ref, ub_scr, sm_scr, hp_scr, hpb_scr, t_scr, *, n_seq, n_chunks):
    s = pl.program_id(1)
    m = n_seq * n_chunks
    q = S5_Q
    nblk = S5_W // LANES
    sb = S5_SB

    per = q // S5_SPLIT
    kw = S5_W // S5_SPLIT

    @pl.when(s == 0)
    def _():
        for j in range(q):
            ub_scr[j // per, :, (j % per) * LANES:(j % per + 1) * LANES] = (
                u_ref[pl.ds(j, m, stride=q), :].astype(BF16))
        for ib in range(S5_SPLIT):
            for j in range(q):
                c0 = (q - 1 - j) * LANES + ib * kw
                t_scr[ib, j * LANES:(j + 1) * LANES, :] = k_ref[0, :, c0:c0 + kw]

    @pl.when(s < S5_SPLIT)
    def _():
        part = _dot(ub_scr[jnp.minimum(s, S5_SPLIT - 1)], bq_ref[0])

        @pl.when(s == 0)
        def _():
            for cb in range(nblk):
                sm_scr[cb] = part[:, cb * LANES:(cb + 1) * LANES]

        @pl.when(s > 0)
        def _():
            for cb in range(nblk):
                sm_scr[cb] += part[:, cb * LANES:(cb + 1) * LANES]

    @pl.when(s == S5_SPLIT - 1)
    def _():
        ar = ar_ref[0]
        ai = ai_ref[0]
        h0 = h0_ref[0]
        blk = lambda a, cb: a[:, cb * LANES:(cb + 1) * LANES]

        def body(n, carry):
            rows_f = pl.ds(n, n_seq, stride=n_chunks)
            rows_b = pl.ds(n_chunks - 1 - n, n_seq, stride=n_chunks)
            new = list(carry)
            for d, rows in ((0, rows_f), (1, rows_b)):
                for c in range(sb):
                    re_i = d * sb + c
                    im_i = (2 + d) * sb + c
                    hr, hi = carry[re_i], carry[im_i]
                    hp_scr[re_i, rows, :] = hr
                    hp_scr[im_i, rows, :] = hi
                    a_r, a_i = blk(ar, re_i), blk(ai, re_i)
                    new[re_i] = a_r * hr - a_i * hi + sm_scr[re_i, rows, :]
                    new[im_i] = a_r * hi + a_i * hr + sm_scr[im_i, rows, :]
            return tuple(new)

        fin = lax.fori_loop(0, n_chunks, body, tuple(blk(h0, cb) for cb in range(nblk)))
        hf_ref[0] = jnp.concatenate(fin, axis=1)
        for cb in range(nblk):
            hpb_scr[:, cb * LANES:(cb + 1) * LANES] = hp_scr[cb].astype(BF16)

    @pl.when(s >= S5_SPLIT)
    def _():
        ub = jnp.concatenate([ub_scr[k] for k in range(S5_SPLIT)], axis=1)
        y = _dot(ub, t_scr[jnp.maximum(s - S5_SPLIT, 0)]) + _dot(hpb_scr[...], cq_ref[0])
        dd = d_ref[0]
        for ii in range(per):
            rows = pl.ds((s - S5_SPLIT) * per + ii, m, stride=q)
            yi = y[:, ii * LANES:(ii + 1) * LANES] + dd * u_ref[rows, :]
            y_ref[rows, :] = jax.nn.gelu(yi)


def _s5_expand_kernel(mc_ref, o_ref, *, xsize, ysize):
    xs, ys, gs = xsize.bit_length() - 1, ysize.bit_length() - 1, S5_GB.bit_length() - 1
    assert xsize == 1 << xs and ysize == 1 << ys and S5_GB == 1 << gs
    cw = o_ref.shape[2]
    nc = mc_ref.shape[2]
    col0 = pl.program_id(1) * cw
    r = lax.broadcasted_iota(jnp.int32, (nc, cw), 0)
    col = lax.broadcasted_iota(jnp.int32, (nc, cw), 1) + col0
    spread = jnp.logical_and(r >> ys == col >> (ys + gs), (r & (ysize - 1)) == (col & (ysize - 1)))
    big = _dot(mc_ref[0], jnp.where(spread, 1.0, 0.0).astype(BF16))
    row = lax.broadcasted_iota(jnp.int32, big.shape, 0)
    colb = lax.broadcasted_iota(jnp.int32, big.shape, 1) + col0
    same = ((row >> xs) & (S5_GB - 1)) == ((colb >> ys) & (S5_GB - 1))
    o_ref[0] = jnp.where(same, big, 0.0).astype(BF16)


def _s5_expand(mc, *, xsize, ysize):
    nb, rows, nc = mc.shape
    cols = nc * S5_GB
    cw = 512
    return pl.pallas_call(
        functools.partial(_s5_expand_kernel, xsize=xsize, ysize=ysize),
        grid=(nb, cols // cw),
        in_specs=[pl.BlockSpec((1, rows, nc), lambda b, j: (b, 0, 0))],
        out_specs=pl.BlockSpec((1, rows, cw), lambda b, j: (b, 0, j)),
        out_shape=jax.ShapeDtypeStruct((nb, rows, cols), BF16),
        compiler_params=_cparams(("parallel", "parallel")),
        name="s5_expand",
    )(mc)


def _s5_mats(lam_re, lam_im, log_dt, b_re, b_im, c_re, c_im, d):
    flat = lambda a: a.reshape((-1,) + a.shape[2:])
    kc, bq, cq, ar, ai, dd = map(flat, jax.vmap(_s5_compact)(lam_re, lam_im, log_dt, b_re, b_im, c_re, c_im, d))
    ch, p = S5_GROUP, S5_STATE
    return (_s5_expand(kc, xsize=ch, ysize=ch),
            _s5_expand(bq, xsize=ch, ysize=p),
            _s5_expand(cq, xsize=p, ysize=ch),
            ar, ai, dd)


def _s5_compact(lam_re, lam_im, log_dt, b_re, b_im, c_re, c_im, d):
    q, g, p, ch = S5_Q, S5_GROUPS, S5_STATE, S5_GROUP
    lam = lax.complex(jnp.minimum(lam_re.astype(F32), -1e-4), lam_im.astype(F32))
    ldt = lam * jnp.exp(log_dt.astype(F32))[..., None]
    lam_bar = jnp.exp(ldt)
    b_bar = ((lam_bar - 1.0) / lam)[..., None] * lax.complex(b_re.astype(F32), b_im.astype(F32))
    cc = lax.complex(c_re.astype(F32), c_im.astype(F32))
    pw = jnp.exp(ldt[..., None] * jnp.arange(q + 1, dtype=F32))
    hi = lax.Precision.HIGHEST
    lag = jnp.arange(2 * q, dtype=F32) - (q - 1)
    wf = jnp.where(lag >= 0, jnp.exp(ldt[0][..., None] * jnp.maximum(lag, 0.0)), 0.0)
    wb = jnp.where(lag <= 0, jnp.exp(ldt[1][..., None] * jnp.maximum(-lag, 0.0)), 0.0)
    kc = jnp.real(jnp.einsum('gcp,gpd,gpe->gedc', cc[0], wf, b_bar[0], precision=hi)
                  + jnp.einsum('gcp,gpd,gpe->gedc', cc[1], wb, b_bar[1], precision=hi))

    pw_dn = jnp.exp(ldt[..., None] * (q - jnp.arange(q + 1, dtype=F32)))
    bf = pw_dn[0][..., 1:][:, :, :, None] * b_bar[0][:, :, None, :]
    bb = pw[1][..., :q][:, :, :, None] * b_bar[1][:, :, None, :]
    to_rows = lambda m: m.transpose(0, 2, 3, 1).reshape(g, q * ch, p)
    bq = jnp.concatenate([to_rows(jnp.real(bf)), to_rows(jnp.real(bb)),
                          to_rows(jnp.imag(bf)), to_rows(jnp.imag(bb))], axis=-1)

    cf = cc[0].transpose(0, 2, 1)[:, :, None, :] * pw[0][..., 1:][:, :, :, None]
    cb = cc[1].transpose(0, 2, 1)[:, :, None, :] * pw_dn[1][..., :q][:, :, :, None]
    to_cols = lambda m: m.reshape(g, p, q * ch)
    cq = jnp.concatenate([to_cols(jnp.real(cf)), to_cols(jnp.real(cb)),
                          to_cols(-jnp.imag(cf)), to_cols(-jnp.imag(cb))], axis=1)

    gb, nb = S5_GB, g // S5_GB
    rows_of = lambda a, outer, inner: (a.reshape(nb, gb, outer, inner, a.shape[-1]).transpose(0, 2, 1, 3, 4)
                                       .reshape(nb, outer * gb * inner, a.shape[-1]).astype(BF16))
    kc = kc.reshape(nb, gb * ch, 2 * q * ch).astype(BF16)

    lq = pw[..., q].reshape(2, nb, 1, S5_SW)
    ar = jnp.concatenate([jnp.real(lq[0]), jnp.real(lq[1])], axis=-1)
    ai = jnp.concatenate([jnp.imag(lq[0]), jnp.imag(lq[1])], axis=-1)
    dd = d.astype(F32).reshape(nb, 1, LANES)
    return kc, rows_of(bq, q, ch), rows_of(cq, 4, p), ar, ai, dd


def _s5(proj, mats, h0_re, h0_im, *, row0, n_seq, seq_len, layer=0, dst=None):
    q, p = S5_Q, S5_STATE
    n_chunks = seq_len // q
    m = n_seq * n_chunks
    rows = n_seq * seq_len
    nb = S5_GROUPS // S5_GB
    nblk = S5_W // LANES
    kw = S5_W // S5_SPLIT
    tm, bq, cq, ar, ai, dd = mats
    part = lambda a: a.astype(F32).reshape(n_seq, nb, S5_SW)
    h0 = jnp.concatenate([part(h0_re[:, 0]), part(h0_re[:, 1]), part(h0_im[:, 0]), part(h0_im[:, 1])],
                         axis=-1).transpose(1, 0, 2)
    per_b = lambda shape: pl.BlockSpec((1,) + shape, lambda b, s: (b, 0, 0))
    b0 = layer * nb
    per_l = lambda shape: pl.BlockSpec((1,) + shape, lambda b, s: (b0 + b, 0, 0))
    body, dst_spec, dst_arg, alias = _into(functools.partial(_s5_kernel, n_seq=n_seq, n_chunks=n_chunks), 8, dst)
    y, hf = pl.pallas_call(
        body,
        grid=(nb, 2 * S5_SPLIT),
        in_specs=[
            pl.BlockSpec((rows, LANES), lambda b, s: (row0 // rows, U_COL // LANES + b)),
            pl.BlockSpec((1, kw, 4 * S5_SW), lambda b, s: (b0 + b, jnp.minimum(s, S5_SPLIT - 1), 0)),
            per_l((LANES, 2 * q * LANES)),
            pl.BlockSpec((1, 4 * S5_SW, kw), lambda b, s: (b0 + b, 0, jnp.maximum(s - S5_SPLIT, 0))),
            per_l((1, 2 * S5_SW)), per_l((1, 2 * S5_SW)), per_l((1, LANES)), per_b((n_seq, 4 * S5_SW)),
        ] + dst_spec,
        out_specs=[pl.BlockSpec((rows, LANES), lambda b, s: (row0 // rows, b)), per_b((n_seq, 4 * S5_SW))],
        out_shape=[jax.ShapeDtypeStruct((proj.shape[0], S5_WIDTH), F32),
                   jax.ShapeDtypeStruct((nb, n_seq, 4 * S5_SW), F32)],
        scratch_shapes=[pltpu.VMEM((S5_SPLIT, m, kw), BF16), pltpu.VMEM((nblk, m, LANES), F32),
                        pltpu.VMEM((nblk, m, LANES), F32), pltpu.VMEM((m, 4 * S5_SW), BF16),
                        pltpu.VMEM((S5_SPLIT, S5_W, kw), BF16)],
        input_output_aliases=alias,
        compiler_params=_cparams(("parallel", "arbitrary")),
        name="s5",
    )(proj, bq, tm, cq, ar, ai, dd, h0, *dst_arg)
    hf = hf.reshape(nb, n_seq, 4, S5_GB, p).transpose(1, 2, 0, 3, 4).reshape(n_seq, 4, S5_GROUPS, p)
    return y, hf[:, 0:2], hf[:, 2:4]


def _conv3(x, w, b):
    n = x.shape[0]
    row = lax.broadcasted_iota(jnp.int32, x.shape, 0)
    prev = jnp.where(row == 0, 0.0, pltpu.roll(x, 1, 0))
    nxt = jnp.where(row == n - 1, 0.0, pltpu.roll(x, n - 1, 0))
    return prev * w[0:1] + x * w[1:2] + nxt * w[2:3] + b


def _hy_fwd_kernel(x0_ref, x1_ref, v_ref, w0_ref, w1_ref, wv_ref, b0_ref, b1_ref, bv_ref,
                   fc_ref, fs_ref, m1_ref, m2_ref, m3_ref, p_ref, z_ref, x0c_ref, zb_scr):
    @pl.when(pl.program_id(2) == 0)
    def _():
        z = _conv3(x1_ref[...], w1_ref[...], b1_ref[...]) * _conv3(v_ref[...], wv_ref[...], bv_ref[...])
        z_ref[...] = z
        zb_scr[...] = z.astype(BF16)
        x0c_ref[...] = _conv3(x0_ref[...], w0_ref[...], b0_ref[...])

    zb = zb_scr[...]
    a = _dot(fc_ref[...], zb)
    b = _dot(fs_ref[...], zb)
    m2 = m2_ref[...]
    p_ref[0, 0] = (m1_ref[...] * a + m2 * b).astype(BF16)
    p_ref[0, 1] = (m3_ref[...] * b - m2 * a).astype(BF16)


def _hy_inv_kernel(p_ref, gc_ref, gs_ref, z_ref, x0c_ref, bias_ref, o_ref):
    conv = _dot(gc_ref[...], p_ref[0, 0]) + _dot(gs_ref[...], p_ref[0, 1])
    o_ref[...] = (x0c_ref[...] * (conv + bias_ref[...] * z_ref[...])).astype(o_ref.dtype)


def _hy_short_kernel(x0_ref, x1_ref, v_ref, w0_ref, w1_ref, wv_ref, b0_ref, b1_ref, bv_ref,
                     fc_ref, fs_ref, gs_ref, m1_ref, m2_ref, m3_ref, bias_ref, o_ref):
    z = _conv3(x1_ref[...], w1_ref[...], b1_ref[...]) * _conv3(v_ref[...], wv_ref[...], bv_ref[...])
    zb = z.astype(BF16)
    a = _dot(fc_ref[...], zb)
    b = _dot(fs_ref[...], zb)
    m2 = m2_ref[...]
    p_re = (m1_ref[...] * a + m2 * b).astype(BF16)
    p_im = (m3_ref[...] * b - m2 * a).astype(BF16)
    conv = _dot(fc_ref[...], p_re) + _dot(gs_ref[...], p_im)
    x0c = _conv3(x0_ref[...], w0_ref[...], b0_ref[...])
    o_ref[...] = (x0c * (conv + bias_ref[...] * z)).astype(o_ref.dtype)


def _dft_mats(seq_len):
    n, w = seq_len, 64
    k = jnp.arange(n, dtype=jnp.int32)
    ang = lambda j: ((k[:, None] * j[None, :]) % (2 * n)).astype(F32) * (math.pi / n)
    ang_a = ang(jnp.arange(n // w, dtype=jnp.int32) * w)
    ang_b = ang(jnp.arange(w, dtype=jnp.int32))
    ca, sa = jnp.cos(ang_a)[:, :, None], jnp.sin(ang_a)[:, :, None]
    cb, sb = jnp.cos(ang_b)[:, None, :], jnp.sin(ang_b)[:, None, :]
    cm = (ca * cb - sa * sb).reshape(n, n)
    sm = -(sa * cb + ca * sb).reshape(n, n)
    nyq = jnp.where(k % 2 == 0, 1.0, -1.0).astype(F32)
    return cm.astype(BF16), sm.at[0, :].set(nyq).astype(BF16), sm.at[:, 0].set(nyq).astype(BF16)


def _hy_filter_taps(seq_len, f1_w, f1_b, f2_w, f2_b, f3_w, f3_b, freq, decay):
    n = seq_len
    t = (jnp.arange(n, dtype=F32) / n)[:, None]
    bands = jnp.arange(1, HY_BANDS + 1, dtype=F32)[None, :]
    z = jnp.concatenate([t, jnp.cos(2.0 * math.pi * t * bands), jnp.sin(2.0 * math.pi * t * bands)], axis=-1)
    hi = lax.Precision.HIGHEST
    fr = freq.astype(F32)
    h = jnp.sin(fr * (jnp.dot(z, f1_w.astype(F32), precision=hi) + f1_b.astype(F32)))
    h = jnp.sin(fr * (jnp.dot(h, f2_w.astype(F32), precision=hi) + f2_b.astype(F32)))
    h = jnp.dot(h, f3_w.astype(F32), precision=hi) + f3_b.astype(F32)
    h = h * jnp.exp(-t * jnp.abs(decay.astype(F32)))
    h = h.reshape(n, 2, HY_WIDTH)
    h = h / jnp.sum(jnp.abs(h), axis=(0, 1), keepdims=True)
    return h.reshape(n, 2 * HY_WIDTH)


def _hy_spectrum_kernel(h_ref, cm_ref, sm_ref, ck_ref, sk_ref, m1_ref, m2_ref, m3_ref, *, seq_len):
    w = HY_WIDTH
    tk = cm_ref.shape[0]
    hb = h_ref[0].astype(BF16)
    xc = _dot(cm_ref[...], hb)
    xs = _dot(sm_ref[...], hb)
    k = pl.program_id(1) * tk + lax.broadcasted_iota(jnp.int32, (tk, 1), 0)
    first = k == 0
    sf = jnp.where(first, 0.0, -xs[:, :w])
    sb = jnp.where(first, 0.0, -xs[:, w:])
    ck, sk = ck_ref[...], sk_ref[...]
    hr = xc[:, :w] + ck * xc[:, w:] - sk * sb
    him = -sf + ck * sb + sk * xc[:, w:]
    nyq = xs[:, :w] - xs[:, w:]
    wk = jnp.where(first, 1.0, 2.0) / (2.0 * seq_len)
    m1_ref[0] = hr * wk
    m2_ref[0] = jnp.where(first, 0.0, -him) * wk
    m3_ref[0] = jnp.where(first, nyq, hr) * wk


def _hy_spectrum(h, dft):
    depth, n, _ = h.shape
    cm, sm, _ = dft
    tk = min(n, 512)
    ang = jnp.arange(n, dtype=F32)[:, None] * (math.pi / n)
    out = jax.ShapeDtypeStruct((depth, n, HY_WIDTH), F32)
    oblk = pl.BlockSpec((1, tk, HY_WIDTH), lambda l, k: (l, k, 0))
    return pl.pallas_call(
        functools.partial(_hy_spectrum_kernel, seq_len=n),
        grid=(depth, n // tk),
        in_specs=[pl.BlockSpec((1, n, 2 * HY_WIDTH), lambda l, k: (l, 0, 0)),
                  pl.BlockSpec((tk, n), lambda l, k: (k, 0)), pl.BlockSpec((tk, n), lambda l, k: (k, 0)),
                  pl.BlockSpec((tk, 1), lambda l, k: (k, 0)), pl.BlockSpec((tk, 1), lambda l, k: (k, 0))],
        out_specs=[oblk, oblk, oblk],
        out_shape=[out, out, out],
        compiler_params=_cparams(("parallel", "parallel")),
        name="hyena_spectrum",
    )(h, cm, sm, jnp.cos(ang), jnp.sin(ang))


def _hyena(proj, conv_w, conv_b, bias, dft, mults, *, row0, n_seq, seq_len, cb, tk, layer=0, dst=None):
    blk0 = row0 // seq_len
    nc = HY_WIDTH // cb
    nk = seq_len // tk
    c0 = HY_COL // cb
    cm, sm, smt = dft
    m1, m2, m3 = mults
    xcol = lambda part: pl.BlockSpec((seq_len, cb), lambda b, c, k: (blk0 + b, c0 + part * nc + c))
    wcol = lambda part: pl.BlockSpec((3, cb), lambda b, c, k: (0, part * nc + c))
    bcol = lambda part: pl.BlockSpec((1, cb), lambda b, c, k: (0, part * nc + c))
    frow = pl.BlockSpec((tk, seq_len), lambda b, c, k: (k, 0))
    mblk = pl.BlockSpec((None, tk, cb), lambda b, c, k: (layer, k, c))
    cb2 = conv_b.reshape(1, 3 * HY_WIDTH)
    if nk == 1:
        body, dst_spec, dst_arg, alias = _into(_hy_short_kernel, 16, dst)
        return pl.pallas_call(
            body,
            grid=(n_seq, nc, 1),
            in_specs=[xcol(0), xcol(1), xcol(2), wcol(0), wcol(1), wcol(2), bcol(0), bcol(1), bcol(2),
                      frow, frow, frow, mblk, mblk, mblk,
                      pl.BlockSpec((1, cb), lambda b, c, k: (0, c))] + dst_spec,
            out_specs=pl.BlockSpec((seq_len, cb), lambda b, c, k: (blk0 + b, c)),
            out_shape=jax.ShapeDtypeStruct((proj.shape[0], HY_WIDTH), BF16),
            input_output_aliases=alias,
            compiler_params=_cparams(("parallel", "parallel", "arbitrary")),
            name="hyena_short",
        )(proj, proj, proj, conv_w, conv_w, conv_w, cb2, cb2, cb2, cm, sm, smt, m1, m2, m3,
          bias.reshape(1, HY_WIDTH), *dst_arg)
    pspec, z, x0c = pl.pallas_call(
        _hy_fwd_kernel,
        grid=(n_seq, nc, nk),
        in_specs=[xcol(0), xcol(1), xcol(2), wcol(0), wcol(1), wcol(2), bcol(0), bcol(1), bcol(2),
                  frow, frow, mblk, mblk, mblk],
        out_specs=[
            pl.BlockSpec((1, 2, tk, cb), lambda b, c, k: (b, 0, k, c)),
            pl.BlockSpec((seq_len, cb), lambda b, c, k: (b, c)),
            pl.BlockSpec((seq_len, cb), lambda b, c, k: (b, c)),
        ],
        out_shape=[
            jax.ShapeDtypeStruct((n_seq, 2, seq_len, HY_WIDTH), BF16),
            jax.ShapeDtypeStruct((n_seq * seq_len, HY_WIDTH), F32),
            jax.ShapeDtypeStruct((n_seq * seq_len, HY_WIDTH), F32),
        ],
        scratch_shapes=[pltpu.VMEM((seq_len, cb), BF16)],
        compiler_params=_cparams(("parallel", "parallel", "arbitrary")),
        name="hyena_fwd",
    )(proj, proj, proj, conv_w, conv_w, conv_w, cb2, cb2, cb2, cm, sm, m1, m2, m3)
    grow = pl.BlockSpec((tk, seq_len), lambda b, c, k: (k, 0))
    tile = pl.BlockSpec((tk, cb), lambda b, c, k: (b * nk + k, c))
    body, dst_spec, dst_arg, alias = _into(_hy_inv_kernel, 6, dst)
    return pl.pallas_call(
        body,
        grid=(n_seq, nc, nk),
        in_specs=[pl.BlockSpec((1, 2, seq_len, cb), lambda b, c, k: (b, 0, 0, c)),
                  grow, grow, tile, tile, pl.BlockSpec((1, cb), lambda b, c, k: (0, c))] + dst_spec,
        out_specs=pl.BlockSpec((tk, cb), lambda b, c, k: (row0 // tk + b * nk + k, c)),
        out_shape=jax.ShapeDtypeStruct((proj.shape[0], HY_WIDTH), BF16),
        input_output_aliases=alias,
        compiler_params=_cparams(("parallel", "parallel", "arbitrary")),
        name="hyena_inv",
    )(pspec, cm, smt, z, x0c, bias.reshape(1, HY_WIDTH), *dst_arg)


def _out_kernel(x_ref, ret_ref, s5_ref, hy_ref, mod_ref, g_ref, gw_ref, gb_ref,
                wr_ref, ws_ref, wh_ref, rt_ref, xo_ref, h_ref, lg_ref):
    m = mod_ref[0]
    y = s5_ref[...]
    s5o = y * jax.nn.sigmoid(_dot(y.astype(BF16), gw_ref[...]) + gb_ref[...])
    mix = (_dot(ret_ref[...], wr_ref[...]) + _dot(s5o.astype(BF16), ws_ref[...])
           + _dot(hy_ref[...], wh_ref[...]))
    x = x_ref[...] + m[2:3] * mix
    xo_ref[...] = x
    h = _rms(x, g_ref[...]) * (1.0 + m[4:5]) + m[3:4]
    hb = h.astype(BF16)
    _store_token_tiles(h_ref, _pack_halves(hb))
    lg_ref[...] = lax.dot_general(rt_ref[...], hb, (((1,), (1,)), ((), ())), preferred_element_type=F32)


def _out_proj(x, ret_o, s5_y, hy_o, mod, g, glu_w, glu_b, w_out, router, n_ctx_rows, dec_seq):
    t = x.shape[0]
    tm = ROW_TILE
    grp = functools.partial(_group_of, tile=tm, n_ctx_rows=n_ctx_rows, dec_seq=dec_seq)
    row = lambda w: pl.BlockSpec((tm, w), lambda i: (i, 0))
    full = lambda a, b: pl.BlockSpec((a, b), lambda i: (0, 0))
    wo = w_out.astype(BF16)
    return pl.pallas_call(
        _out_kernel,
        grid=(t // tm,),
        in_specs=[row(D_MODEL), row(RET_WIDTH), row(S5_WIDTH), row(HY_WIDTH),
                  pl.BlockSpec((1, 6, D_MODEL), lambda i: (grp(i), 0, 0)),
                  full(1, D_MODEL), full(S5_WIDTH, S5_WIDTH), full(1, S5_WIDTH),
                  full(RET_WIDTH, D_MODEL), full(S5_WIDTH, D_MODEL), full(HY_WIDTH, D_MODEL),
                  full(N_EXPERTS, D_MODEL)],
        out_specs=[row(D_MODEL), pl.BlockSpec((tm * TOKEN_ROWS, LANES), lambda i: (i, 0)),
                   pl.BlockSpec((N_EXPERTS, tm), lambda i: (0, i))],
        out_shape=[jax.ShapeDtypeStruct((t, D_MODEL), F32), jax.ShapeDtypeStruct((t * TOKEN_ROWS, LANES), jnp.uint32),
                   jax.ShapeDtypeStruct((N_EXPERTS, t), F32)],
        compiler_params=_cparams(("parallel",)),
        name="out_proj",
    )(x, ret_o, s5_y, hy_o, mod, g.reshape(1, D_MODEL), glu_w.astype(BF16), glu_b.reshape(1, S5_WIDTH),
      wo[:RET_WIDTH], wo[RET_WIDTH:RET_WIDTH + S5_WIDTH], wo[RET_WIDTH + S5_WIDTH:], router.T.astype(BF16))


def _moe_kernel(be_ref, first_ref, slot_ref, nxt_ref, nu_ref, xs_ref, wg_hbm, wu_hbm, wd_hbm, o_ref,
                wg_f, wu_f, wd_f, wg_b, wu_b, wd_b, sem, *, layer):
    i = pl.program_id(0)

    def copies(e, s):
        return (pltpu.make_async_copy(wg_hbm.at[layer, e], wg_f.at[s], sem.at[s, 0]),
                pltpu.make_async_copy(wu_hbm.at[layer, e], wu_f.at[s], sem.at[s, 1]),
                pltpu.make_async_copy(wd_hbm.at[layer, e], wd_f.at[s], sem.at[s, 2]))

    @pl.when(i == 0)
    def _():
        for cp in copies(be_ref[0], 0):
            cp.start()

    @pl.when(first_ref[i] == 1)
    def _():
        s = slot_ref[i]
        for cp in copies(be_ref[i], s):
            cp.wait()

        @pl.when(nxt_ref[i] >= 0)
        def _():
            for cp in copies(nxt_ref[i], 1 - s):
                cp.start(priority=1)

        wg_b[...] = wg_f[s].astype(BF16)
        wu_b[...] = wu_f[s].astype(BF16)
        wd_b[...] = wd_f[s].astype(BF16)

    @pl.when(i < nu_ref[0])
    def _():
        half = D_MODEL // 2
        x_lo, x_hi = _unpack_halves(_load_token_tiles(xs_ref, MOE_BM))
        x_lo = x_lo.astype(BF16)
        x_hi = x_hi.astype(BF16)
        gate = _dot(x_lo, wg_b[0:half, :]) + _dot(x_hi, wg_b[half:, :])
        up = _dot(x_lo, wu_b[0:half, :]) + _dot(x_hi, wu_b[half:, :])
        hb = gate * jax.nn.sigmoid(gate) * up
        _store_token_tiles(o_ref, _pack_halves(_dot(hb.astype(BF16), wd_b[...]).astype(BF16)))

    @pl.when(i >= nu_ref[0])
    def _():
        o_ref[...] = jnp.zeros_like(o_ref)


def _moe_grouped(xs, blk_e, first, slot, nxt, n_used, w_gate, w_up, w_down, layer):
    pr = xs.shape[0] // TOKEN_ROWS
    bm = MOE_BM
    nb = pr // bm
    grid_spec = pltpu.PrefetchScalarGridSpec(
        num_scalar_prefetch=5,
        grid=(nb,),
        in_specs=[
            pl.BlockSpec((bm * TOKEN_ROWS, LANES), lambda i, *_: (i, 0)),
            pl.BlockSpec(memory_space=pl.ANY),
            pl.BlockSpec(memory_space=pl.ANY),
            pl.BlockSpec(memory_space=pl.ANY),
        ],
        out_specs=pl.BlockSpec((bm * TOKEN_ROWS, LANES), lambda i, *_: (i, 0)),
        scratch_shapes=[pltpu.VMEM((2, D_MODEL, D_EXPERT), F32), pltpu.VMEM((2, D_MODEL, D_EXPERT), F32),
                        pltpu.VMEM((2, D_EXPERT, D_MODEL), F32),
                        pltpu.VMEM((D_MODEL, D_EXPERT), BF16), pltpu.VMEM((D_MODEL, D_EXPERT), BF16),
                        pltpu.VMEM((D_EXPERT, D_MODEL), BF16),
                        pltpu.SemaphoreType.DMA((2, 3))],
    )
    return pl.pallas_call(
        functools.partial(_moe_kernel, layer=layer),
        grid_spec=grid_spec,
        out_shape=jax.ShapeDtypeStruct((pr * TOKEN_ROWS, LANES), jnp.uint32),
        compiler_params=_cparams(("arbitrary",)),
        name="moe_grouped",
    )(blk_e, first, slot, nxt, n_used, xs, w_gate, w_up, w_down)


DISPATCH_TILE = 1024


def _dispatch_kernel(info_ref, nu_ref, pos_ref, h_ref, xs_out, zbuf, sem, zsem, *, nb):
    tm = pos_ref.shape[1]
    tr = TOKEN_ROWS
    bm = MOE_BM

    @pl.when(pl.program_id(0) == 0)
    def _():
        zbuf[...] = jnp.zeros_like(zbuf)

        def zero_block(first_row):
            rows = pl.ds(pl.multiple_of(first_row * tr, tr), bm * tr)
            return pltpu.make_async_copy(zbuf, xs_out.at[rows], zsem)

        for act in ("start", "wait"):
            def last_of_expert(e, carry, act=act):
                @pl.when(info_ref[e, 3] > info_ref[e, 1])
                def _():
                    getattr(zero_block(info_ref[e, 3] - bm), act)()
                return carry

            def tail_block(b, carry, act=act):
                getattr(zero_block(b * bm), act)()
                return carry

            lax.fori_loop(0, N_EXPERTS, last_of_expert, 0)
            lax.fori_loop(nu_ref[0], nb, tail_block, 0)

    def send(tok, carry):
        src = h_ref.at[pl.ds(pl.multiple_of(tok * tr, tr), tr)]
        for k in range(TOP_K):
            row = pl.multiple_of(pos_ref[k, tok], tr)
            pltpu.make_async_copy(src, xs_out.at[pl.ds(row, tr)], sem).start(priority=k % 2)
        return carry

    lax.fori_loop(0, tm, send, 0, unroll=4)
    n = tm * TOP_K * tr
    pltpu.make_async_copy(xs_out.at[pl.ds(0, n)], xs_out.at[pl.ds(0, n)], sem).wait()


def _dispatch(h2, pos, info, n_used, nb):
    t = pos.shape[1]
    tm = DISPATCH_TILE
    return pl.pallas_call(
        functools.partial(_dispatch_kernel, nb=nb),
        grid=(t // tm,),
        in_specs=[pl.BlockSpec(memory_space=pltpu.SMEM),
                  pl.BlockSpec(memory_space=pltpu.SMEM),
                  pl.BlockSpec((SUBLANES, tm), lambda i: (0, i), memory_space=pltpu.SMEM),
                  pl.BlockSpec((tm * TOKEN_ROWS, LANES), lambda i: (i, 0))],
        out_specs=pl.BlockSpec(memory_space=pl.ANY),
        out_shape=jax.ShapeDtypeStruct((nb * MOE_BM * TOKEN_ROWS, LANES), h2.dtype),
        scratch_shapes=[pltpu.VMEM((MOE_BM * TOKEN_ROWS, LANES), h2.dtype),
                        pltpu.SemaphoreType.DMA(()), pltpu.SemaphoreType.DMA(())],
        compiler_params=pltpu.CompilerParams(dimension_semantics=("arbitrary",)),
        name="dispatch",
    )(info, n_used, pos, h2)


ROUTE_TILE = 512


def _router_kernel(lg_ref, bias_ref, pos_ref, gate_ref, be_ref, info_ref, rank_scr, ek_scr, *, t, nbp):
    tl = ROUTE_TILE
    ne = N_EXPERTS
    bm = MOE_BM
    row = lax.broadcasted_iota(jnp.int32, (ne, tl), 0)
    tri = (lax.broadcasted_iota(jnp.int32, (tl, tl), 0) < lax.broadcasted_iota(jnp.int32, (tl, tl), 1)).astype(BF16)
    bias = bias_ref[...]

    def select(i, counts):
        cols = pl.ds(pl.multiple_of(i * tl, tl), tl)
        s = jax.nn.sigmoid(lg_ref[:, cols])
        sel = s + bias
        mask = jnp.zeros((ne, tl), F32)
        vals = []
        for k in range(TOP_K):
            best = jnp.max(sel, axis=0, keepdims=True)
            idx = jnp.min(jnp.where(sel == best, row, ne), axis=0, keepdims=True)
            hit = row == idx
            vals.append(jnp.sum(jnp.where(hit, s, 0.0), axis=0, keepdims=True))
            sel = jnp.where(hit, -jnp.inf, sel)
            mask = jnp.where(hit, 1.0, mask)
            ek_scr[k:k + 1, cols] = idx
        total = vals[0]
        for v in vals[1:]:
            total = total + v
        scale = ROUTED_SCALE / total
        for k in range(TOP_K):
            gate_ref[k:k + 1, cols] = vals[k] * scale
        gate_ref[TOP_K:SUBLANES, cols] = jnp.zeros((SUBLANES - TOP_K, tl), F32)
        rank_scr[:, cols] = _dot(mask.astype(BF16), tri) + counts
        return counts + jnp.sum(mask, axis=1, keepdims=True)

    counts = lax.fori_loop(0, t // tl, select, jnp.zeros((ne, 1), F32))
    counts = counts.astype(jnp.int32)
    shift = bm.bit_length() - 1
    assert bm == 1 << shift
    padded = ((counts + (bm - 1)) >> shift) << shift
    e0 = lax.broadcasted_iota(jnp.int32, (ne, ne), 0)
    e1 = lax.broadcasted_iota(jnp.int32, (ne, ne), 1)
    padded_row = jnp.sum(jnp.where(e0 == e1, padded, 0), axis=0, keepdims=True)
    counts_row = jnp.sum(jnp.where(e0 == e1, counts, 0), axis=0, keepdims=True)
    pstart = jnp.sum(jnp.where(e1 < e0, padded_row, 0), axis=1, keepdims=True)
    ustart = jnp.sum(jnp.where(e1 < e0, counts_row, 0), axis=1, keepdims=True)
    pend = pstart + padded
    lane = lax.broadcasted_iota(jnp.int32, (ne, LANES), 1)
    info_ref[...] = jnp.where(lane == 0, counts, jnp.where(lane == 1, pstart, jnp.where(lane == 2, ustart, pend)))
    blk = lax.broadcasted_iota(jnp.int32, (ne, nbp), 1) * bm
    owner = jnp.sum(jnp.where(pend <= blk, 1, 0), axis=0, keepdims=True)
    be_ref[...] = jnp.minimum(owner, ne - 1)
    pstart_f = pstart.astype(F32)

    def place(i, carry):
        cols = pl.ds(pl.multiple_of(i * tl, tl), tl)
        dest = rank_scr[:, cols] + pstart_f
        for k in range(TOP_K):
            hit = row == ek_scr[k:k + 1, cols]
            pos_ref[k:k + 1, cols] = (jnp.sum(jnp.where(hit, dest, 0.0), axis=0, keepdims=True)
                                      * float(TOKEN_ROWS)).astype(jnp.int32)
        pos_ref[TOP_K:SUBLANES, cols] = jnp.zeros((SUBLANES - TOP_K, tl), jnp.int32)
        return carry

    lax.fori_loop(0, t // tl, place, 0)


def _router(logits_t, router_bias, nb):
    t = logits_t.shape[1]
    nbp = -(-nb // LANES) * LANES
    return pl.pallas_call(
        functools.partial(_router_kernel, t=t, nbp=nbp),
        out_shape=[jax.ShapeDtypeStruct((SUBLANES, t), jnp.int32), jax.ShapeDtypeStruct((SUBLANES, t), F32),
                   jax.ShapeDtypeStruct((1, nbp), jnp.int32), jax.ShapeDtypeStruct((N_EXPERTS, LANES), jnp.int32)],
        scratch_shapes=[pltpu.VMEM((N_EXPERTS, t), F32), pltpu.VMEM((SUBLANES, t), jnp.int32)],
        compiler_params=pltpu.CompilerParams(vmem_limit_bytes=VMEM_LIMIT),
        name="router",
    )(logits_t, router_bias.astype(F32).reshape(N_EXPERTS, 1))


def _dispatch_plan(blk_e_row, info, nb):
    bm = MOE_BM
    pend = info[:, 3]
    blk_e = blk_e_row[0, :nb]
    n_used = pend[-1] // bm
    blk = jnp.arange(nb, dtype=jnp.int32)
    prev_e = jnp.concatenate([jnp.full((1,), -1, jnp.int32), blk_e[:-1]])
    first = jnp.logical_and(blk < n_used, blk_e != prev_e)
    slot = (jnp.cumsum(first.astype(jnp.int32)) - 1) % 2
    first_at = jnp.where(first, blk, nb)
    nxt_first = lax.cummin(jnp.concatenate([first_at[1:], jnp.full((1,), nb, jnp.int32)]), reverse=True)
    nxt = jnp.where(nxt_first < nb, blk_e[jnp.minimum(nxt_first, nb - 1)], -1)
    return (blk_e, first.astype(jnp.int32), slot.astype(jnp.int32), nxt.astype(jnp.int32),
            n_used.astype(jnp.int32).reshape(1))


def _shared_kernel(pos_ref, nxt_ref, x_ref, h_ref, gt_ref, mod_ref, sg_ref, su_ref, sd_ref, fn_ref, eo_hbm,
                   o_ref, buf0, buf1, acc_scr, sem, *, final):
    i = pl.program_id(0)
    n = pl.num_programs(0)
    tm = x_ref.shape[0]
    tr = TOKEN_ROWS

    half = D_MODEL // 2
    grp = SUBLANES

    def fetch_tokens(idx_ref, buf, slot, tok0):
        for tt in range(grp):
            tok = tok0 + tt
            for k in range(TOP_K):
                row = pl.multiple_of(idx_ref[k, tok], tr)
                pltpu.make_async_copy(eo_hbm.at[pl.ds(row, tr)], buf.at[k, pl.ds(pl.multiple_of(tok * tr, tr), tr)],
                                      sem.at[slot]).start(priority=k % 2)

    def arrived(buf, slot):
        for k in range(TOP_K):
            pltpu.make_async_copy(eo_hbm.at[pl.ds(0, tm * tr)], buf.at[k], sem.at[slot]).wait()

    def step(cur, cur_slot, nxt, nxt_slot):
        arrived(cur, cur_slot)

        def group(c, carry):
            tok0 = pl.multiple_of(c * grp, grp)
            fetch_tokens(nxt_ref, nxt, nxt_slot, tok0)
            gt = gt_ref[pl.ds(tok0, grp), :]
            r_lo = jnp.zeros((grp, half), F32)
            r_hi = r_lo
            for k in range(TOP_K):
                words = jnp.concatenate(
                    [cur[k, pl.ds(tok0 * tr + s, grp, stride=tr), :] for s in range(tr)], axis=1)
                e_lo, e_hi = _unpack_halves(words)
                r_lo = r_lo + gt[:, k:k + 1] * e_lo
                r_hi = r_hi + gt[:, k:k + 1] * e_hi
            acc_scr[pl.ds(tok0, grp), :] = jnp.concatenate([r_lo, r_hi], axis=1)
            return carry

        lax.fori_loop(0, tm // grp, group, 0)
        h_lo, h_hi = _unpack_halves(_load_token_tiles(h_ref, tm))
        h_lo = h_lo.astype(BF16)
        h_hi = h_hi.astype(BF16)
        gate = _dot(h_lo, sg_ref[0:half, :]) + _dot(h_hi, sg_ref[half:, :])
        up = _dot(h_lo, su_ref[0:half, :]) + _dot(h_hi, su_ref[half:, :])
        act = gate * jax.nn.sigmoid(gate) * up
        shared = _dot(act.astype(BF16), sd_ref[...])
        x = x_ref[...] + mod_ref[0][5:6] * (acc_scr[...] + shared)
        if final:
            x = _rms(x, fn_ref[...])
        o_ref[...] = x

        @pl.when(i == n - 1)
        def _():
            arrived(nxt, nxt_slot)

    @pl.when(i == 0)
    def _():
        def first(c, carry):
            fetch_tokens(pos_ref, buf0, 0, pl.multiple_of(c * grp, grp))
            return carry
        lax.fori_loop(0, tm // grp, first, 0)

    for parity, cur, nxt in ((0, buf0, buf1), (1, buf1, buf0)):
        @pl.when(i % 2 == parity)
        def _(parity=parity, cur=cur, nxt=nxt):
            step(cur, parity, nxt, 1 - parity)


def _shared(x, h, gates_t, pos, eo, mod, sg, su, sd, final_norm, n_ctx_rows, dec_seq, *, final, row0=0, rows=None):
    rows = x.shape[0] if rows is None else rows
    tm = ROW_TILE // 2
    b0 = row0 // tm
    steps = rows // tm
    grp = lambda i: _group_of(i + b0, tm, n_ctx_rows, dec_seq)
    row = pl.BlockSpec((tm, D_MODEL), lambda i: (i + b0, 0))
    prow = pl.BlockSpec((tm * TOKEN_ROWS, LANES), lambda i: (i + b0, 0))
    full = lambda a, b: pl.BlockSpec((a, b), lambda i: (0, 0))
    buf = pltpu.VMEM((TOP_K, tm * TOKEN_ROWS, LANES), eo.dtype)
    return pl.pallas_call(
        functools.partial(_shared_kernel, final=final),
        grid=(steps,),
        in_specs=[pl.BlockSpec((SUBLANES, tm), lambda i: (0, i + b0), memory_space=pltpu.SMEM),
                  pl.BlockSpec((SUBLANES, tm), lambda i: (0, jnp.minimum(i + 1, steps - 1) + b0),
                               memory_space=pltpu.SMEM),
                  row, prow, pl.BlockSpec((tm, SUBLANES), lambda i: (i + b0, 0)),
                  pl.BlockSpec((1, 6, D_MODEL), lambda i: (grp(i), 0, 0)),
                  full(D_MODEL, D_SHARED), full(D_MODEL, D_SHARED), full(D_SHARED, D_MODEL), full(1, D_MODEL),
                  pl.BlockSpec(memory_space=pl.ANY)],
        out_specs=pl.BlockSpec((tm, D_MODEL), lambda i: (i, 0)),
        out_shape=jax.ShapeDtypeStruct((rows, D_MODEL), F32),
        scratch_shapes=[buf, buf, pltpu.VMEM((tm, D_MODEL), F32), pltpu.SemaphoreType.DMA((2,))],
        compiler_params=_cparams(("arbitrary",)),
        name="shared_final" if final else "shared",
    )(pos, pos, x, h, gates_t, mod, sg.astype(BF16), su.astype(BF16), sd.astype(BF16),
      final_norm.reshape(1, D_MODEL), eo)


def kernel(x_prompt, x_sample, state_ret, state_s5_re, state_s5_im, c, c_ctx, w_ada, b_ada, norm_mix, norm_ffn, w_in, w_out, ret_decay, s5_lam_re, s5_lam_im, s5_log_dt, s5_b_re, s5_b_im, s5_c_re, s5_c_im, s5_d, s5_glu_w, s5_glu_b, hy_conv_w, hy_conv_b, hy_f1_w, hy_f1_b, hy_f2_w, hy_f2_b, hy_f3_w, hy_f3_b, hy_freq, hy_decay, hy_bias, moe_router, moe_router_bias, moe_w_gate, moe_w_up, moe_w_down, sh_w_gate, sh_w_up, sh_w_down, final_norm):
    n_ctx, seq, d = x_prompt.shape
    n_dec, dec_seq, _ = x_sample.shape
    n_ctx_rows = n_ctx * seq
    t = n_ctx_rows + n_dec * dec_seq

    x = jnp.concatenate([x_prompt.reshape(n_ctx_rows, d), x_sample.reshape(n_dec * dec_seq, d)], axis=0)
    cond = jnp.concatenate([c_ctx[None, :], c], axis=0)
    cond8 = jnp.pad(cond, ((0, SUBLANES - cond.shape[0]), (0, 0)))
    mods = _ada(cond8, w_ada, b_ada)[:, :1 + n_dec].reshape(DEPTH, 1 + n_dec, 6, d)

    cos2, sin2 = _rope_tables(dec_seq)
    no_rope = jnp.zeros((seq, LANES), F32)
    zero_ret = jnp.zeros((1, 2, RET_HEADS, RET_DK, RET_DV), F32)
    zero_s5 = jnp.zeros((n_ctx, 2, S5_GROUPS, S5_STATE), F32)
    dft_ctx = _dft_mats(seq)
    dft_dec = _dft_mats(dec_seq)

    w_in_bf = w_in.astype(BF16)
    mats = _s5_mats(s5_lam_re, s5_lam_im, s5_log_dt, s5_b_re, s5_b_im, s5_c_re, s5_c_im, s5_d)
    filt = (hy_f1_w, hy_f1_b, hy_f2_w, hy_f2_b, hy_f3_w, hy_f3_b, hy_freq, hy_decay)
    mults_ctx = _hy_spectrum(jax.vmap(functools.partial(_hy_filter_taps, seq))(*filt), dft_ctx)
    mults_dec = _hy_spectrum(jax.vmap(functools.partial(_hy_filter_taps, dec_seq))(*filt), dft_dec)

    ret_states = jnp.zeros((n_ctx, DEPTH, 2, RET_HEADS, RET_DK, RET_DV), F32)
    s5r_list, s5i_list = [], []
    for l in range(DEPTH):
        mod = mods[l]
        proj = _in_proj(x, mod, norm_mix[l], w_in_bf, l, n_ctx_rows, dec_seq)

        log_gamma = jax.nn.log_sigmoid(ret_decay[l].astype(F32))
        ret_o, ret_states = _retention(proj, log_gamma, zero_ret, no_rope, no_rope,
                                       row0=0, n_seq=n_ctx, seq_len=seq, hb=RET_HEADS, rope=False,
                                       dst=jnp.zeros((t, RET_WIDTH), BF16), states=ret_states, layer=l, nsb=2)
        ret_o, _ = _retention(proj, log_gamma, state_ret[:, l].astype(F32), cos2, sin2,
                              row0=n_ctx_rows, n_seq=n_dec, seq_len=dec_seq, hb=2, rope=True, dst=ret_o)

        s5_y, s5_re, s5_im = _s5(proj, mats, zero_s5, zero_s5, row0=0, n_seq=n_ctx, seq_len=seq, layer=l,
                                 dst=jnp.zeros((t, S5_WIDTH), F32))
        s5_y, _, _ = _s5(proj, mats, state_s5_re[:, l], state_s5_im[:, l],
                         row0=n_ctx_rows, n_seq=n_dec, seq_len=dec_seq, layer=l, dst=s5_y)
        s5r_list.append(s5_re)
        s5i_list.append(s5_im)

        hy_o = _hyena(proj, hy_conv_w[l], hy_conv_b[l], hy_bias[l], dft_ctx, mults_ctx, layer=l,
                      row0=0, n_seq=n_ctx, seq_len=seq, cb=HY_WIDTH, tk=seq, dst=jnp.zeros((t, HY_WIDTH), BF16))
        hy_o = _hyena(proj, hy_conv_w[l], hy_conv_b[l], hy_bias[l], dft_dec, mults_dec, layer=l,
                      row0=n_ctx_rows, n_seq=n_dec, seq_len=dec_seq, cb=HY_WIDTH // 2, tk=512, dst=hy_o)

        x, h2, logits = _out_proj(x, ret_o, s5_y, hy_o, mod, norm_ffn[l], s5_glu_w[l], s5_glu_b[l],
                                  w_out[l], moe_router[l], n_ctx_rows, dec_seq)

        nb = -(-(t * TOP_K) // MOE_BM) + N_EXPERTS
        pos, gates, blk_e_row, info = _router(logits, moe_router_bias[l], nb)
        blk_e, first, slot, nxt, n_used = _dispatch_plan(blk_e_row, info, nb)
        xs = _dispatch(h2, pos, info, n_used, nb)
        eo = _moe_grouped(xs, blk_e, first, slot, nxt, n_used, moe_w_gate, moe_w_up, moe_w_down, l)
        gates_t = gates.T

        sh = (sh_w_gate[l], sh_w_up[l], sh_w_down[l])
        if l < DEPTH - 1:
            x = _shared(x, h2, gates_t, pos, eo, mod, *sh, final_norm, n_ctx_rows, dec_seq, final=False)
        else:
            y_c = _shared(x, h2, gates_t, pos, eo, mod, *sh, final_norm, n_ctx_rows, dec_seq, final=True,
                          row0=0, rows=n_ctx_rows)
            y_d = _shared(x, h2, gates_t, pos, eo, mod, *sh, final_norm, n_ctx_rows, dec_seq, final=True,
                          row0=n_ctx_rows, rows=n_dec * dec_seq)

    return (y_c.reshape(n_ctx, seq, d), y_d.reshape(n_dec, dec_seq, d),
            ret_states, jnp.stack(s5r_list, axis=1), jnp.stack(s5i_list, axis=1))
```
